```python
import math
import jax, jax.numpy as jnp
from jax import lax
import numpy as np

D_MODEL = 1024
BATCH = 16
SEQ = 256
DEPTH = 4
DEC_BATCH = 4
DEC_SEQ = 1024
PAST_LEN = 256

GRID_W = 64
N_MIXERS = 3
N_A = (DEPTH + 2) // 3
N_B = (DEPTH + 1) // 3
N_C = DEPTH // 3
H_A = 8
DH_A = D_MODEL // (2 * H_A)
DV_A = 2 * DH_A
E_A = H_A * DV_A
H_B = 4
DK_B = D_MODEL // H_B
DV_B = 2 * DK_B
E_B = H_B * DV_B
CHUNK = 128
E_C = D_MODEL
CONV_W = 3
ALPHA = (2.0 * DEPTH) ** 0.25
BETA = (8.0 * DEPTH) ** -0.25
ROPE_BASE = 10000.0
Q_BLOCK = 128
LN_EPS = 1e-5

kernel_name = "hybrid_diffusion_diffattn_retnet_shortconv_step"


def layer_norm(x, g, b):
    xf = x.astype(jnp.float32)
    mu = jnp.mean(xf, -1, keepdims=True)
    var = jnp.mean(jnp.square(xf - mu), -1, keepdims=True)
    return ((xf - mu) * lax.rsqrt(var + LN_EPS)).astype(x.dtype) * g + b


def rms_norm(x, eps=1e-6):
    xf = x.astype(jnp.float32)
    return (xf * lax.rsqrt(jnp.mean(jnp.square(xf), -1, keepdims=True) + eps)).astype(x.dtype)


def ada_modulation(cvec, w_mod, b_mod):
    m = jax.nn.silu(cvec) @ w_mod + b_mod
    shift, scale, gate = jnp.split(m[:, None, :], 3, axis=-1)
    return shift, scale, gate


def axial_rope(n_tokens, dim):
    rows = n_tokens // GRID_W
    r = jnp.repeat(jnp.arange(rows, dtype=jnp.float32), GRID_W)
    col = jnp.tile(jnp.arange(GRID_W, dtype=jnp.float32), rows)
    n_freq = dim // 4
    inv = ROPE_BASE ** (-jnp.arange(n_freq, dtype=jnp.float32) / n_freq)
    ang = jnp.concatenate([r[:, None] * inv, col[:, None] * inv], -1)
    return jnp.cos(ang), jnp.sin(ang)


def apply_rope(x, cos, sin):
    xf = x.astype(jnp.float32)
    x1, x2 = jnp.split(xf, 2, axis=-1)
    c = cos[None, :, None, None, :]
    s = sin[None, :, None, None, :]
    return jnp.concatenate([x1 * c - x2 * s, x1 * s + x2 * c], -1).astype(x.dtype)


def diff_lambda(lam, layer_idx):
    lam = lam.astype(jnp.float32)
    lam_init = 0.8 - 0.6 * math.exp(-0.3 * layer_idx)
    val = jnp.exp(jnp.sum(lam[0] * lam[1])) - jnp.exp(jnp.sum(lam[2] * lam[3])) + lam_init
    return val, lam_init


def diff_project(h, w_in):
    b, t, _ = h.shape
    q, k, v, z = jnp.split(h @ w_in, 4, axis=-1)
    return (q.reshape(b, t, H_A, 2, DH_A), k.reshape(b, t, H_A, 2, DH_A),
            v.reshape(b, t, H_A, DV_A), z)


def diff_attention(q, k, v, lam):
    b, t = q.shape[:2]
    nb = t // Q_BLOCK
    qb = jnp.moveaxis(q.reshape(b, nb, Q_BLOCK, H_A, 2, DH_A), 1, 0)
    scale = DH_A ** -0.5

    def block(qblk):
        s = jnp.einsum('bqhmd,bshmd->bhmqs', qblk, k).astype(jnp.float32) * scale
        p = jax.nn.softmax(s, axis=-1)
        a = p[:, :, 0] - lam * p[:, :, 1]
        return jnp.einsum('bhqs,bshd->bqhd', a.astype(v.dtype), v)

    o = lax.map(block, qb)
    return jnp.moveaxis(o, 0, 1).reshape(b, t, H_A, DV_A)


def diff_output(o, z, lam_init, subln_g, w_out):
    o = rms_norm(o) * subln_g * (1.0 - lam_init)
    o = o.reshape(o.shape[0], o.shape[1], E_A)
    return (o * jax.nn.silu(z)) @ w_out


def ret_project(h, w_in):
    b, t, _ = h.shape
    q, k, v, g = jnp.split(h @ w_in, [H_B * DK_B, 2 * H_B * DK_B, 2 * H_B * DK_B + E_B], axis=-1)
    q = q.reshape(b, t, H_B, DK_B)
    k = k.reshape(b, t, H_B, DK_B) * (DK_B ** -0.5)
    v = v.reshape(b, t, H_B, DV_B)
    return q, k, v, g


def log_decay(a):
    return jnp.log1p(-jnp.exp(a.astype(jnp.float32)))


def retention_chunkwise(q, k, v, lg, state0):
    b, t = q.shape[:2]
    nc = t // CHUNK

    def chunks(x):
        return jnp.moveaxis(x.reshape(b, nc, CHUNK, *x.shape[2:]), 1, 0)

    idx = jnp.arange(CHUNK, dtype=jnp.float32)
    diff = idx[:, None] - idx[None, :]
    dmask = jnp.where(diff[None] >= 0, jnp.exp(jnp.maximum(diff, 0.0)[None] * lg[:, None, None]), 0.0)
    q_decay = jnp.exp((idx[:, None] + 1.0) * lg[None, :])
    k_decay = jnp.exp((CHUNK - 1.0 - idx)[:, None] * lg[None, :])
    c_decay = jnp.exp(CHUNK * lg)

    def step(s, xs):
        qc, kc, vc = xs
        qf, kf, vf = qc.astype(jnp.float32), kc.astype(jnp.float32), vc.astype(jnp.float32)
        qk = jnp.einsum('bnhd,bmhd->bhnm', qf, kf) * dmask
        o = (jnp.einsum('bhnm,bmhv->bnhv', qk, vf)
             + jnp.einsum('bnhd,bhdv->bnhv', qf, s) * q_decay[None, :, :, None])
        s = s * c_decay[None, :, None, None] + jnp.einsum('bmhd,bmhv->bhdv', kf * k_decay[None, :, :, None], vf)
        return s, o

    s_fin, o = lax.scan(step, state0.astype(jnp.float32), (chunks(q), chunks(k), chunks(v)))
    return jnp.moveaxis(o, 0, 1).reshape(b, t, H_B, DV_B).astype(v.dtype), s_fin


def bi_retention(q, k, v, lg_f, lg_b, s0_f, s0_b):
    o_f, s_f = retention_chunkwise(q, k, v, lg_f, s0_f)
    fl = lambda a: jnp.flip(a, axis=1)
    o_b, s_b = retention_chunkwise(fl(q), fl(k), fl(v), lg_b, s0_b)
    return o_f + fl(o_b), s_f, s_b


def ret_output(o, g, w_out):
    o = rms_norm(o).reshape(o.shape[0], o.shape[1], E_B)
    return (o * jax.nn.silu(g)) @ w_out


def short_conv3(u, w):
    up = jnp.pad(u, ((0, 0), (1, 1), (0, 0)))
    return up[:, :-2] * w[0] + up[:, 1:-1] * w[1] + up[:, 2:] * w[2]


def conv_mixer(h, w_in, conv_w, w_out):
    bg, cg, u, z = jnp.split(h @ w_in, 4, axis=-1)
    y = bg * short_conv3(cg * u, conv_w)
    return (y * jax.nn.silu(z)) @ w_out


def setup_inputs(seed: int = 0) -> dict:
    key = jax.random.key(seed)
    ks = jax.random.split(key, 26)
    f32 = jnp.float32

    def nrm(k, shape, s):
        return jax.random.normal(k, shape, f32) * s

    D = D_MODEL
    heads_b = jnp.arange(H_B, dtype=f32)
    decay_init = -(5.0 + heads_b) * math.log(2.0)
    return {
        "x_prompt": nrm(ks[0], (BATCH, SEQ, D), 1.0),
        "x_sample": nrm(ks[1], (DEC_BATCH, DEC_SEQ, D), 1.0),
        "cache_k": nrm(ks[2], (DEC_BATCH, N_A, PAST_LEN, H_A, 2 * DH_A), 1.0),
        "cache_v": nrm(ks[3], (DEC_BATCH, N_A, PAST_LEN, H_A, DV_A), 1.0),
        "state_fwd": nrm(ks[4], (DEC_BATCH, N_B, H_B, DK_B, DV_B), 0.5),
        "state_bwd": nrm(ks[5], (DEC_BATCH, N_B, H_B, DK_B, DV_B), 0.5),
        "c": nrm(ks[6], (DEC_BATCH, D), 1.0),
        "c_ctx": nrm(ks[7], (D,), 1.0),
        "w_mod": nrm(ks[8], (DEPTH, D, 3 * D), 0.5 * D ** -0.5),
        "b_mod": nrm(ks[9], (DEPTH, 3 * D), 0.02),
        "ln_g": 1.0 + nrm(ks[10], (DEPTH, D), 0.02),
        "ln_b": nrm(ks[11], (DEPTH, D), 0.02),
        "w_in_a": nrm(ks[12], (N_A, D, 4 * E_A), D ** -0.5),
        "lam_a": nrm(ks[13], (N_A, 4, DH_A), 0.1),
        "subln_a": 1.0 + nrm(ks[14], (N_A, DV_A), 0.02),
        "w_out_a": nrm(ks[15], (N_A, E_A, D), BETA * E_A ** -0.5),
        "w_in_b": nrm(ks[16], (N_B, D, 2 * H_B * DK_B + 2 * E_B), D ** -0.5),
        "decay_fwd": decay_init[None] + nrm(ks[17], (N_B, H_B), 0.1),
        "decay_bwd": decay_init[None] + nrm(ks[18], (N_B, H_B), 0.1),
        "w_out_b": nrm(ks[19], (N_B, E_B, D), BETA * E_B ** -0.5),
        "w_in_c": nrm(ks[20], (N_C, D, 4 * E_C), D ** -0.5),
        "conv_c": nrm(ks[21], (N_C, CONV_W, E_C), CONV_W ** -0.5),
        "w_out_c": nrm(ks[22], (N_C, E_C, D), BETA * E_C ** -0.5),
    }


def reference(x_prompt, x_sample, cache_k, cache_v, state_fwd, state_bwd, c, c_ctx,
              w_mod, b_mod, ln_g, ln_b, w_in_a, lam_a, subln_a, w_out_a,
              w_in_b, decay_fwd, decay_bwd, w_out_b, w_in_c, conv_c, w_out_c):
    xp = x_prompt
    xs = x_sample
    b_s, t_s = xs.shape[0], xs.shape[1]
    l_c = cache_k.shape[2]
    new_k, new_v, new_sf, new_sb = [], [], [], []
    for i in range(DEPTH):
        kind, j = i % N_MIXERS, i // N_MIXERS
        sh_p, sc_p, g_p = ada_modulation(c_ctx[None], w_mod[i], b_mod[i])
        sh_s, sc_s, g_s = ada_modulation(c, w_mod[i], b_mod[i])
        hp = xp * (1 + sc_p) + sh_p
        hs = xs * (1 + sc_s) + sh_s
        if kind == 0:
            lam, lam_init = diff_lambda(lam_a[j], i)
            q, k, v, z = diff_project(hp, w_in_a[j])
            out_p = diff_output(diff_attention(q, k, v, lam), z, lam_init, subln_a[j], w_out_a[j])
            new_k.append(k.reshape(k.shape[0], k.shape[1], H_A, 2 * DH_A))
            new_v.append(v)
            q, k, v, z = diff_project(hs, w_in_a[j])
            cos, sin = axial_rope(t_s, DH_A)
            q = apply_rope(q, cos, sin)
            k = apply_rope(k, cos, sin)
            k_all = jnp.concatenate([k, cache_k[:, j].reshape(b_s, l_c, H_A, 2, DH_A).astype(k.dtype)], axis=1)
            v_all = jnp.concatenate([v, cache_v[:, j].astype(v.dtype)], axis=1)
            out_s = diff_output(diff_attention(q, k_all, v_all, lam), z, lam_init, subln_a[j], w_out_a[j])
        elif kind == 1:
            lg_f = log_decay(decay_fwd[j])
            lg_b = log_decay(decay_bwd[j])
            q, k, v, g = ret_project(hp, w_in_b[j])
            zero = jnp.zeros((q.shape[0], H_B, DK_B, DV_B), jnp.float32)
            o, s_f, s_b = bi_retention(q, k, v, lg_f, lg_b, zero, zero)
            out_p = ret_output(o, g, w_out_b[j])
            new_sf.append(s_f)
            new_sb.append(s_b)
            q, k, v, g = ret_project(hs, w_in_b[j])
            o, _, _ = bi_retention(q, k, v, lg_f, lg_b, state_fwd[:, j], state_bwd[:, j])
            out_s = ret_output(o, g, w_out_b[j])
        else:
            out_p = conv_mixer(hp, w_in_c[j], conv_c[j], w_out_c[j])
            out_s = conv_mixer(hs, w_in_c[j], conv_c[j], w_out_c[j])
        xp = layer_norm(ALPHA * xp + g_p * out_p, ln_g[i], ln_b[i])
        xs = layer_norm(ALPHA * xs + g_s * out_s, ln_g[i], ln_b[i])
    y_prompt = xp
    y_sample = xs
    new_cache_k = jnp.stack(new_k, axis=1)
    new_cache_v = jnp.stack(new_v, axis=1)
    new_state_fwd = jnp.stack(new_sf, axis=1)
    new_state_bwd = jnp.stack(new_sb, axis=1)
    return (y_prompt, y_sample, new_cache_k, new_cache_v, new_state_fwd, new_state_bwd)
```

```python
import functools
import math

import jax
import jax.numpy as jnp
from jax import lax
from jax.experimental import pallas as pl
from jax.experimental.pallas import tpu as pltpu

F32 = jnp.float32
BF16 = jnp.bfloat16

D_MODEL = 1024
DEPTH = 4
N_MIXERS = 3
GRID_W = 64
H_A = 8
DH_A = 64
DV_A = 128
H_B = 4
DK_B = 256
DV_B = 512
E_B = H_B * DV_B
CHUNK = 128
ALPHA = (2.0 * DEPTH) ** 0.25
ROPE_BASE = 10000.0
LN_EPS = 1e-5
RMS_EPS = 1e-6

MOD_ROWS = 8
CTX_ROW = 4
VMEM_LIMIT_BYTES = 56 * 1024 * 1024
ROW_TILE = 512


def _params(n_axes):
    return pltpu.CompilerParams(dimension_semantics=("arbitrary",) * n_axes,
                                vmem_limit_bytes=VMEM_LIMIT_BYTES)


def _silu(x):
    return x * jax.nn.sigmoid(x)


def _residual_layer_norm(x, out, gate, g, b):
    r = ALPHA * x + gate * out
    mu = jnp.mean(r, axis=-1, keepdims=True)
    d = r - mu
    var = jnp.mean(d * d, axis=-1, keepdims=True)
    return d * lax.rsqrt(var + LN_EPS) * g + b


def _modulated(x_ref, mod_ref):
    shift = mod_ref[:, 0:D_MODEL]
    scale = mod_ref[:, D_MODEL:2 * D_MODEL]
    return (x_ref[...] * (1.0 + scale) + shift).astype(BF16)


def _mod_row_map(layer, rows_per_batch, tile):
    if rows_per_batch is None:
        return lambda m, *_: (layer * MOD_ROWS + CTX_ROW, 0, 0)
    return lambda m, *_: (layer * MOD_ROWS + (m * tile) // rows_per_batch, 0, 0)


def _mod_kernel(cv_ref, w_ref, b_ref, o_ref):
    s = _silu(cv_ref[...])
    o_ref[...] = jnp.dot(s.astype(BF16), w_ref[...].astype(BF16),
                         preferred_element_type=F32) + b_ref[...]


def _modulation(cvec, w_mod, b_mod):
    tn = 1024
    n = 3 * D_MODEL
    out = pl.pallas_call(
        _mod_kernel,
        out_shape=jax.ShapeDtypeStruct((DEPTH, MOD_ROWS, n), F32),
        grid=(DEPTH, n // tn),
        in_specs=[pl.BlockSpec((MOD_ROWS, D_MODEL), lambda i, j: (0, 0)),
                  pl.BlockSpec((None, D_MODEL, tn), lambda i, j: (i, 0, j)),
                  pl.BlockSpec((None, 1, tn), lambda i, j: (i, 0, j))],
        out_specs=pl.BlockSpec((None, MOD_ROWS, tn), lambda i, j: (i, 0, j)),
        compiler_params=_params(2),
        name="modulation",
    )(cvec, w_mod, b_mod.reshape(DEPTH, 1, n))
    return out.reshape(DEPTH * MOD_ROWS, 1, n)


def _rope(xh, cos4, sin4, first_half):
    swapped = jnp.where(first_half, pltpu.roll(xh, 96, 1), pltpu.roll(xh, 32, 1))
    return xh * cos4 + swapped * sin4


def _attn_in_kernel(*refs, rope, keep_f32):
    if rope:
        x_ref, mod_ref, w_ref, cos_ref, sin_ref = refs[:5]
        outs = refs[5:]
    else:
        x_ref, mod_ref, w_ref = refs[:3]
        outs = refs[3:]
    q_ref, k_ref, v_ref, z_ref = outs[:4]
    h = _modulated(x_ref, mod_ref)
    if rope:
        lane = lax.broadcasted_iota(jnp.int32, (1, DV_A), 1)
        first_half = (lane % (2 * 32)) < 32
        cos4 = cos_ref[...]
        sin4 = sin_ref[...]
    qk_scale = DH_A ** -0.5
    q_all = jnp.dot(h, w_ref[:, 0:D_MODEL], preferred_element_type=F32)
    k_all = jnp.dot(h, w_ref[:, D_MODEL:2 * D_MODEL], preferred_element_type=F32)
    if keep_f32:
        outs[4][...] = k_all
    for hd in range(H_A):
        cols = slice(hd * DV_A, (hd + 1) * DV_A)
        q = q_all[:, cols]
        k = k_all[:, cols]
        if rope:
            q = _rope(q, cos4, sin4, first_half)
            k = _rope(k, cos4, sin4, first_half)
        q_ref[:, cols] = (q * qk_scale).astype(BF16)
        k_ref[:, cols] = k.astype(BF16)
    v = jnp.dot(h, w_ref[:, 2 * D_MODEL:3 * D_MODEL], preferred_element_type=F32)
    if keep_f32:
        outs[5][...] = v
    v_ref[...] = v.astype(BF16)
    z_ref[...] = jnp.dot(h, w_ref[:, 3 * D_MODEL:4 * D_MODEL], preferred_element_type=F32)


def _attn_in(x2d, mod3, w_bf16, layer, rows_per_batch, rope_tables, keep_f32):
    m_rows = x2d.shape[0]
    tm = ROW_TILE
    rope = rope_tables is not None
    row_spec = pl.BlockSpec((tm, D_MODEL), lambda m: (m, 0))
    in_specs = [row_spec,
                pl.BlockSpec((None, 1, 3 * D_MODEL), _mod_row_map(layer, rows_per_batch, tm)),
                pl.BlockSpec((D_MODEL, 4 * D_MODEL), lambda m: (0, 0))]
    args = [x2d, mod3, w_bf16]
    if rope:
        seq_tiles = rows_per_batch // tm
        tab_spec = pl.BlockSpec((tm, DV_A), lambda m: (m % seq_tiles, 0))
        in_specs += [tab_spec, tab_spec]
        args += list(rope_tables)
    out_shape = [jax.ShapeDtypeStruct((m_rows, D_MODEL), BF16)] * 3 + [
        jax.ShapeDtypeStruct((m_rows, D_MODEL), F32)]
    out_specs = [row_spec] * 4
    if keep_f32:
        out_shape += [jax.ShapeDtypeStruct((m_rows, D_MODEL), F32)] * 2
        out_specs += [row_spec] * 2
    return pl.pallas_call(
        functools.partial(_attn_in_kernel, rope=rope, keep_f32=keep_f32),
        out_shape=out_shape,
        grid=(m_rows // tm,),
        in_specs=in_specs,
        out_specs=out_specs,
        compiler_params=_params(1),
        name="attn_in_rope" if rope else "attn_in",
    )(*args)


def _attn_kernel(*refs, layer_idx, has_ctx):
    if has_ctx:
        (q_ref, k_ref, v_ref, z_ref, x_ref, kc_ref, vc_ref, mod_ref, lam_ref, subln_ref,
         w_ref, g_ref, b_ref, o_ref, y_ref) = refs
    else:
        (q_ref, k_ref, v_ref, z_ref, x_ref, mod_ref, lam_ref, subln_ref,
         w_ref, g_ref, b_ref, o_ref, y_ref) = refs
    tq = q_ref.shape[0]
    lam_init = 0.8 - 0.6 * math.exp(-0.3 * layer_idx)
    lm = lam_ref[...]
    lam = (jnp.exp(jnp.sum(lm[0:1] * lm[1:2], axis=-1, keepdims=True))
           - jnp.exp(jnp.sum(lm[2:3] * lm[3:4], axis=-1, keepdims=True)) + lam_init)
    lane = lax.broadcasted_iota(jnp.int32, (1, DV_A), 1)
    first = lane < DH_A
    subln = subln_ref[...]
    nt = (((1,), (1,)), ((), ()))
    for hd in range(H_A):
        cols = slice(hd * DV_A, (hd + 1) * DV_A)
        qh = q_ref[:, cols]
        zero = jnp.zeros_like(qh)
        qq = jnp.concatenate([jnp.where(first, qh, zero), jnp.where(first, zero, qh)], axis=0)
        s = lax.dot_general(qq, k_ref[:, cols], nt, preferred_element_type=F32)
        m = jnp.max(s, axis=-1, keepdims=True)
        if has_ctx:
            kc = kc_ref[:, cols].astype(BF16)
            sc = lax.dot_general(qq, kc, nt, preferred_element_type=F32)
            m = jnp.maximum(m, jnp.max(sc, axis=-1, keepdims=True))
        e = jnp.exp(s - m)
        l = jnp.sum(e, axis=-1, keepdims=True)
        if has_ctx:
            ec = jnp.exp(sc - m)
            l = l + jnp.sum(ec, axis=-1, keepdims=True)
        inv = 1.0 / l
        inv1 = inv[:tq]
        inv2 = inv[tq:] * lam
        a = e[:tq] * inv1 - e[tq:] * inv2
        o = jnp.dot(a.astype(BF16), v_ref[:, cols], preferred_element_type=F32)
        if has_ctx:
            ac = ec[:tq] * inv1 - ec[tq:] * inv2
            o = o + jnp.dot(ac.astype(BF16), vc_ref[:, cols].astype(BF16),
                            preferred_element_type=F32)
        o = o * lax.rsqrt(jnp.mean(o * o, axis=-1, keepdims=True) + RMS_EPS)
        o = o * subln * (1.0 - lam_init)
        y_ref[:, cols] = (o * _silu(z_ref[:, cols])).astype(BF16)
    out = jnp.dot(y_ref[...], w_ref[...], preferred_element_type=F32)
    gate = mod_ref[:, 2 * D_MODEL:3 * D_MODEL]
    o_ref[...] = _residual_layer_norm(x_ref[...], out, gate, g_ref[...], b_ref[...])


def _attention(q, k, v, z, x, ctx, mod3, lam, subln, w_out, ln_g, ln_b, layer, per_batch_rows):
    b, t, _ = q.shape
    tq = 256
    has_ctx = ctx is not None
    q_spec = pl.BlockSpec((None, tq, D_MODEL), lambda i, j: (i, j, 0))
    kv_spec = pl.BlockSpec((None, t, D_MODEL), lambda i, j: (i, 0, 0))
    if per_batch_rows:
        mod_map = lambda i, j: (layer * MOD_ROWS + i, 0, 0)
    else:
        mod_map = lambda i, j: (layer * MOD_ROWS + CTX_ROW, 0, 0)
    const2 = lambda i, j: (0, 0)
    in_specs = [q_spec, kv_spec, kv_spec, q_spec, q_spec]
    args = [q, k, v, z, x]
    if has_ctx:
        cache_k, cache_v, jj = ctx
        c_spec = pl.BlockSpec((None, None, cache_k.shape[2], D_MODEL), lambda i, j: (i, jj, 0, 0))
        in_specs += [c_spec, c_spec]
        args += [cache_k, cache_v]
    in_specs += [pl.BlockSpec((None, 1, 3 * D_MODEL), mod_map),
                 pl.BlockSpec((4, DH_A), const2),
                 pl.BlockSpec((1, DV_A), const2),
                 pl.BlockSpec((D_MODEL, D_MODEL), const2),
                 pl.BlockSpec((1, D_MODEL), const2),
                 pl.BlockSpec((1, D_MODEL), const2)]
    args += [mod3, lam, subln.reshape(1, DV_A), w_out, ln_g.reshape(1, D_MODEL),
             ln_b.reshape(1, D_MODEL)]
    return pl.pallas_call(
        functools.partial(_attn_kernel, layer_idx=layer, has_ctx=has_ctx),
        out_shape=jax.ShapeDtypeStruct((b, t, D_MODEL), F32),
        grid=(b, t // tq),
        in_specs=in_specs,
        out_specs=q_spec,
        scratch_shapes=[pltpu.VMEM((tq, D_MODEL), BF16)],
        compiler_params=_params(2),
        name="diff_attn_ctx" if has_ctx else "diff_attn",
    )(*args)


def _rope_tables(n_tokens):
    rows = n_tokens // GRID_W
    r = jnp.repeat(jnp.arange(rows, dtype=F32), GRID_W)
    col = jnp.tile(jnp.arange(GRID_W, dtype=F32), rows)
    n_freq = DH_A // 4
    inv = ROPE_BASE ** (-jnp.arange(n_freq, dtype=F32) / n_freq)
    ang = jnp.concatenate([r[:, None] * inv, col[:, None] * inv], -1)
    cos, sin = jnp.cos(ang), jnp.sin(ang)
    return jnp.tile(cos, (1, 4)), jnp.concatenate([-sin, sin, -sin, sin], -1)


def _ret_in_kernel(x_ref, mod_ref, w_ref, q_ref, k_ref, v_ref, g_ref):
    h = _modulated(x_ref, mod_ref)
    nq = H_B * DK_B
    q_ref[...] = jnp.dot(h, w_ref[:, 0:nq], preferred_element_type=F32).astype(BF16)
    k_ref[...] = jnp.dot(h, w_ref[:, nq:2 * nq], preferred_element_type=F32) * (DK_B ** -0.5)
    for c in range(E_B // 1024):
        v = jnp.dot(h, w_ref[:, 2 * nq + c * 1024:2 * nq + (c + 1) * 1024],
                    preferred_element_type=F32)
        v_ref[:, c * 1024:(c + 1) * 1024] = v.astype(BF16)
    for c in range(E_B // 1024):
        g_ref[:, c * 1024:(c + 1) * 1024] = jnp.dot(
            h, w_ref[:, 2 * nq + E_B + c * 1024:2 * nq + E_B + (c + 1) * 1024],
            preferred_element_type=F32)


def _ret_in(x2d, mod3, w_bf16, layer, rows_per_batch):
    m_rows = x2d.shape[0]
    tm = ROW_TILE
    nq = H_B * DK_B
    return pl.pallas_call(
        _ret_in_kernel,
        out_shape=[jax.ShapeDtypeStruct((m_rows, nq), BF16),
                   jax.ShapeDtypeStruct((m_rows, nq), F32),
                   jax.ShapeDtypeStruct((m_rows, E_B), BF16),
                   jax.ShapeDtypeStruct((m_rows, E_B), F32)],
        grid=(m_rows // tm,),
        in_specs=[pl.BlockSpec((tm, D_MODEL), lambda m: (m, 0)),
                  pl.BlockSpec((None, 1, 3 * D_MODEL), _mod_row_map(layer, rows_per_batch, tm)),
                  pl.BlockSpec((D_MODEL, 2 * nq + 2 * E_B), lambda m: (0, 0))],
        out_specs=[pl.BlockSpec((tm, nq), lambda m: (m, 0)),
                   pl.BlockSpec((tm, nq), lambda m: (m, 0)),
                   pl.BlockSpec((tm, E_B), lambda m: (m, 0)),
                   pl.BlockSpec((tm, E_B), lambda m: (m, 0))],
        compiler_params=_params(1),
        name="ret_in",
    )(x2d, mod3, w_bf16)


def _retention_kernel(*refs, has_state):
    if has_state:
        q_ref, k_ref, v_ref, g_ref, af_ref, ab_ref, s0f_ref, s0b_ref, y_ref, o_ref = refs
    else:
        q_ref, k_ref, v_ref, g_ref, af_ref, ab_ref, y_ref, sf_ref, sb_ref, o_ref = refs
    t = q_ref.shape[0]
    nc = t // CHUNK
    lg_f = jnp.log1p(-jnp.exp(af_ref[...]))
    lg_b = jnp.log1p(-jnp.exp(ab_ref[...]))
    row = lax.broadcasted_iota(jnp.int32, (CHUNK, CHUNK), 0).astype(F32)
    col = lax.broadcasted_iota(jnp.int32, (CHUNK, CHUNK), 1).astype(F32)
    diff = row - col
    dmask = (jnp.where(diff >= 0, jnp.exp(jnp.maximum(diff, 0.0) * lg_f), 0.0)
             + jnp.where(diff <= 0, jnp.exp(jnp.maximum(-diff, 0.0) * lg_b), 0.0))
    idx = lax.broadcasted_iota(jnp.int32, (CHUNK, 1), 0).astype(F32)
    qd_f = jnp.exp((idx + 1.0) * lg_f)
    kd_f = jnp.exp((CHUNK - 1.0 - idx) * lg_f)
    qd_b = jnp.exp((CHUNK - idx) * lg_b)
    kd_b = jnp.exp(idx * lg_b)
    cd_f = jnp.exp(CHUNK * lg_f)
    cd_b = jnp.exp(CHUNK * lg_b)
    nt = (((1,), (1,)), ((), ()))
    tn = (((0,), (0,)), ((), ()))

    def chunk(c):
        return pl.ds(c * CHUNK, CHUNK)

    for c in range(nc):
        qc = q_ref[chunk(c), :]
        kc = k_ref[chunk(c), :]
        qk = lax.dot_general(qc, kc.astype(BF16), nt, preferred_element_type=F32)
        o_ref[chunk(c), :] = jnp.dot((qk * dmask).astype(BF16), v_ref[chunk(c), :],
                                     preferred_element_type=F32)

    def direction(order, qd, kd, cd, s0):
        s = s0
        for c in order:
            if s is not None:
                o_ref[chunk(c), :] += jnp.dot(q_ref[chunk(c), :], s.astype(BF16),
                                              preferred_element_type=F32) * qd
            u = lax.dot_general((k_ref[chunk(c), :] * kd).astype(BF16), v_ref[chunk(c), :], tn,
                                preferred_element_type=F32)
            s = u if s is None else s * cd + u
        return s

    s_f = direction(range(nc), qd_f, kd_f, cd_f, s0f_ref[...] if has_state else None)
    s_b = direction(range(nc - 1, -1, -1), qd_b, kd_b, cd_b, s0b_ref[...] if has_state else None)
    if not has_state:
        sf_ref[...] = s_f
        sb_ref[...] = s_b
    o = o_ref[...]
    o = o * lax.rsqrt(jnp.mean(o * o, axis=-1, keepdims=True) + RMS_EPS)
    y_ref[...] = (o * _silu(g_ref[...])).astype(BF16)


def _retention(q, k, v, g, decay_f, decay_b, states):
    b, t, _ = q.shape
    has_state = states is not None
    qk_spec = pl.BlockSpec((None, t, DK_B), lambda i, h: (i, 0, h))
    vg_spec = pl.BlockSpec((None, t, DV_B), lambda i, h: (i, 0, h))
    a_spec = pl.BlockSpec((None, 1, 1), lambda i, h: (h, 0, 0))
    in_specs = [qk_spec, qk_spec, vg_spec, vg_spec, a_spec, a_spec]
    args = [q, k, v, g, decay_f.reshape(H_B, 1, 1), decay_b.reshape(H_B, 1, 1)]
    out_shape = [jax.ShapeDtypeStruct((b, t, E_B), BF16)]
    out_specs = [vg_spec]
    if has_state:
        s_f, s_b, jj = states
        s_spec = pl.BlockSpec((None, None, None, DK_B, DV_B), lambda i, h: (i, jj, h, 0, 0))
        in_specs += [s_spec, s_spec]
        args += [s_f, s_b]
    else:
        so_spec = pl.BlockSpec((None, None, None, DK_B, DV_B), lambda i, h: (i, 0, h, 0, 0))
        out_shape += [jax.ShapeDtypeStruct((b, 1, H_B, DK_B, DV_B), F32)] * 2
        out_specs += [so_spec, so_spec]
    res = pl.pallas_call(
        functools.partial(_retention_kernel, has_state=has_state),
        out_shape=out_shape,
        grid=(b, H_B),
        in_specs=in_specs,
        out_specs=out_specs,
        scratch_shapes=[pltpu.VMEM((t, DV_B), F32)],
        compiler_params=_params(2),
        name="retention_state" if has_state else "retention",
    )(*args)
    return res


def _out_proj_kernel(y_ref, w_ref, x_ref, mod_ref, g_ref, b_ref, o_ref):
    out = jnp.dot(y_ref[...], w_ref[...], preferred_element_type=F32)
    gate = mod_ref[:, 2 * D_MODEL:3 * D_MODEL]
    o_ref[...] = _residual_layer_norm(x_ref[...], out, gate, g_ref[...], b_ref[...])


def _out_proj(y2d, w_bf16, x2d, mod3, ln_g, ln_b, layer, rows_per_batch):
    m_rows, e = y2d.shape
    tm = ROW_TILE
    const2 = lambda m: (0, 0)
    return pl.pallas_call(
        _out_proj_kernel,
        out_shape=jax.ShapeDtypeStruct((m_rows, D_MODEL), F32),
        grid=(m_rows // tm,),
        in_specs=[pl.BlockSpec((tm, e), lambda m: (m, 0)),
                  pl.BlockSpec((e, D_MODEL), const2),
                  pl.BlockSpec((tm, D_MODEL), lambda m: (m, 0)),
                  pl.BlockSpec((None, 1, 3 * D_MODEL), _mod_row_map(layer, rows_per_batch, tm)),
                  pl.BlockSpec((1, D_MODEL), const2),
                  pl.BlockSpec((1, D_MODEL), const2)],
        out_specs=pl.BlockSpec((tm, D_MODEL), lambda m: (m, 0)),
        compiler_params=_params(1),
        name="out_proj",
    )(y2d, w_bf16, x2d, mod3, ln_g.reshape(1, D_MODEL), ln_b.reshape(1, D_MODEL))


CONV_ROWS = 1024
CONV_COLS = 256


def _conv_kernel(x_ref, mod_ref, w_in_ref, cw_ref, w_out_ref, g_ref, b_ref, o_ref, acc_ref, *,
                 seq_len):
    h = _modulated(x_ref, mod_ref)
    rows = x_ref.shape[0]
    pos = lax.broadcasted_iota(jnp.int32, (rows, 1), 0) % seq_len
    has_prev = pos > 0
    has_next = pos < seq_len - 1
    e = D_MODEL
    for c in range(e // CONV_COLS):
        cols = slice(c * CONV_COLS, (c + 1) * CONV_COLS)

        def proj(part):
            lo = part * e + c * CONV_COLS
            return jnp.dot(h, w_in_ref[:, lo:lo + CONV_COLS], preferred_element_type=F32)

        p = proj(1) * proj(2)
        prev = jnp.where(has_prev, pltpu.roll(p, 1, 0), 0.0)
        nxt = jnp.where(has_next, pltpu.roll(p, rows - 1, 0), 0.0)
        conv = prev * cw_ref[0:1, cols] + p * cw_ref[1:2, cols] + nxt * cw_ref[2:3, cols]
        y = (proj(0) * conv * _silu(proj(3))).astype(BF16)
        part = jnp.dot(y, w_out_ref[cols, :], preferred_element_type=F32)
        if c == 0:
            acc_ref[...] = part
        else:
            acc_ref[...] += part
    gate = mod_ref[:, 2 * D_MODEL:3 * D_MODEL]
    o_ref[...] = _residual_layer_norm(x_ref[...], acc_ref[...], gate, g_ref[...], b_ref[...])


def _conv_layer(x2d, mod3, w_in, conv_w, w_out, ln_g, ln_b, layer, rows_per_batch, seq_len):
    m_rows = x2d.shape[0]
    tm = CONV_ROWS
    const2 = lambda m: (0, 0)
    return pl.pallas_call(
        functools.partial(_conv_kernel, seq_len=seq_len),
        out_shape=jax.ShapeDtypeStruct((m_rows, D_MODEL), F32),
        grid=(m_rows // tm,),
        in_specs=[pl.BlockSpec((tm, D_MODEL), lambda m: (m, 0)),
                  pl.BlockSpec((None, 1, 3 * D_MODEL), _mod_row_map(layer, rows_per_batch, tm)),
                  pl.BlockSpec((D_MODEL, 4 * D_MODEL), const2),
                  pl.BlockSpec((3, D_MODEL), const2),
                  pl.BlockSpec((D_MODEL, D_MODEL), const2),
                  pl.BlockSpec((1, D_MODEL), const2),
                  pl.BlockSpec((1, D_MODEL), const2)],
        out_specs=pl.BlockSpec((tm, D_MODEL), lambda m: (m, 0)),
        scratch_shapes=[pltpu.VMEM((tm, D_MODEL), F32)],
        compiler_params=_params(1),
        name="conv_layer",
    )(x2d, mod3, w_in, conv_w, w_out, ln_g.reshape(1, D_MODEL), ln_b.reshape(1, D_MODEL))


def kernel(x_prompt, x_sample, cache_k, cache_v, state_fwd, state_bwd, c, c_ctx, w_mod, b_mod, ln_g,
           ln_b, w_in_a, lam_a, subln_a, w_out_a, w_in_b, decay_fwd, decay_bwd, w_out_b, w_in_c,
           conv_c, w_out_c):
    bp, tp, d = x_prompt.shape
    bs, ts, _ = x_sample.shape
    l_c = cache_k.shape[2]
    cvec = jnp.concatenate([c, c_ctx[None], jnp.zeros((MOD_ROWS - bs - 1, d), F32)], axis=0)
    mod3 = _modulation(cvec, w_mod, b_mod)
    cache_k4 = cache_k.reshape(bs, -1, l_c, d)
    cache_v4 = cache_v.reshape(bs, -1, l_c, d)
    rope_tables = _rope_tables(ts)

    xp = x_prompt.reshape(bp * tp, d)
    xs = x_sample.reshape(bs * ts, d)
    new_k, new_v, new_sf, new_sb = [], [], [], []
    for i in range(DEPTH):
        kind, j = i % N_MIXERS, i // N_MIXERS
        if kind == 0:
            w_in = w_in_a[j].astype(BF16)
            w_out = w_out_a[j].astype(BF16)
            q, k, v, z, k32, v32 = _attn_in(xp, mod3, w_in, i, None, None, True)
            new_k.append(k32.reshape(bp, tp, H_A, 2 * DH_A))
            new_v.append(v32.reshape(bp, tp, H_A, DV_A))
            r3 = lambda a: a.reshape(bp, tp, d)
            xp = _attention(r3(q), r3(k), r3(v), r3(z), r3(xp), None, mod3, lam_a[j], subln_a[j],
                            w_out, ln_g[i], ln_b[i], i, False).reshape(bp * tp, d)
            q, k, v, z = _attn_in(xs, mod3, w_in, i, ts, rope_tables, False)
            r3 = lambda a: a.reshape(bs, ts, d)
            xs = _attention(r3(q), r3(k), r3(v), r3(z), r3(xs), (cache_k4, cache_v4, j), mod3,
                            lam_a[j], subln_a[j], w_out, ln_g[i], ln_b[i], i,
                            True).reshape(bs * ts, d)
        elif kind == 1:
            w_in = w_in_b[j].astype(BF16)
            w_out = w_out_b[j].astype(BF16)
            q, k, v, g = _ret_in(xp, mod3, w_in, i, None)
            y, s_f, s_b = _retention(q.reshape(bp, tp, -1), k.reshape(bp, tp, -1),
                                     v.reshape(bp, tp, -1), g.reshape(bp, tp, -1),
                                     decay_fwd[j], decay_bwd[j], None)
            new_sf.append(s_f)
            new_sb.append(s_b)
            xp = _out_proj(y.reshape(bp * tp, E_B), w_out, xp, mod3, ln_g[i], ln_b[i], i, None)
            q, k, v, g = _ret_in(xs, mod3, w_in, i, ts)
            (y,) = _retention(q.reshape(bs, ts, -1), k.reshape(bs, ts, -1), v.reshape(bs, ts, -1),
                              g.reshape(bs, ts, -1), decay_fwd[j], decay_bwd[j],
                              (state_fwd, state_bwd, j))
            xs = _out_proj(y.reshape(bs * ts, E_B), w_out, xs, mod3, ln_g[i], ln_b[i], i, ts)
        else:
            w_in = w_in_c[j].astype(BF16)
            w_out = w_out_c[j].astype(BF16)
            xp = _conv_layer(xp, mod3, w_in, conv_c[j], w_out, ln_g[i], ln_b[i], i, None, tp)
            xs = _conv_layer(xs, mod3, w_in, conv_c[j], w_out, ln_g[i], ln_b[i], i, ts, ts)
    y_prompt = xp.reshape(bp, tp, d)
    y_sample = xs.reshape(bs, ts, d)
    new_cache_k = jnp.stack(new_k, axis=1)
    new_cache_v = jnp.stack(new_v, axis=1)
    new_state_fwd = jnp.concatenate(new_sf, axis=1)
    new_state_bwd = jnp.concatenate(new_sb, axis=1)
    return (y_prompt, y_sample, new_cache_k, new_cache_v, new_state_fwd, new_state_bwd)
```

```python
import functools
import math

import jax
import jax.numpy as jnp
from jax import lax
from jax.experimental import pallas as pl
from jax.experimental.pallas import tpu as pltpu

F32 = jnp.float32
BF16 = jnp.bfloat16

D_MODEL = 1024
DEPTH = 4
N_MIXERS = 3
GRID_W = 64
H_A = 8
DH_A = 64
DV_A = 128
SCORE_SCALE = DH_A ** -0.5 * math.log2(math.e)
H_B = 4
DK_B = 256
DV_B = 512
E_B = H_B * DV_B
CHUNK = 128
ALPHA = (2.0 * DEPTH) ** 0.25
ROPE_BASE = 10000.0
LN_EPS = 1e-5
RMS_EPS = 1e-6

MOD_ROWS = 8
CTX_ROW = 4
VMEM_LIMIT_BYTES = 56 * 1024 * 1024
ROW_TILE = 512


def _params(n_axes):
    return pltpu.CompilerParams(dimension_semantics=("arbitrary",) * n_axes,
                                vmem_limit_bytes=VMEM_LIMIT_BYTES)


def _silu(x):
    return x * jax.nn.sigmoid(x)


def _residual_layer_norm(x, out, gate, g, b):
    r = ALPHA * x + gate * out
    mu = jnp.mean(r, axis=-1, keepdims=True)
    d = r - mu
    var = jnp.mean(d * d, axis=-1, keepdims=True)
    return d * lax.rsqrt(var + LN_EPS) * g + b


def _modulated(x_ref, mod_ref):
    shift = mod_ref[:, 0:D_MODEL]
    scale = mod_ref[:, D_MODEL:2 * D_MODEL]
    return (x_ref[...] * (1.0 + scale) + shift).astype(BF16)


def _mod_row_map(layer, rows_per_batch, tile):
    if rows_per_batch is None:
        return lambda m, *_: (layer * MOD_ROWS + CTX_ROW, 0, 0)
    return lambda m, *_: (layer * MOD_ROWS + (m * tile) // rows_per_batch, 0, 0)


def _mod_kernel(cv_ref, w_ref, b_ref, o_ref):
    s = _silu(cv_ref[...])
    o_ref[...] = jnp.dot(s.astype(BF16), w_ref[...].astype(BF16),
                         preferred_element_type=F32) + b_ref[...]


def _modulation(cvec, w_mod, b_mod):
    tn = 1024
    n = 3 * D_MODEL
    out = pl.pallas_call(
        _mod_kernel,
        out_shape=jax.ShapeDtypeStruct((DEPTH, MOD_ROWS, n), F32),
        grid=(DEPTH, n // tn),
        in_specs=[pl.BlockSpec((MOD_ROWS, D_MODEL), lambda i, j: (0, 0)),
                  pl.BlockSpec((None, D_MODEL, tn), lambda i, j: (i, 0, j)),
                  pl.BlockSpec((None, 1, tn), lambda i, j: (i, 0, j))],
        out_specs=pl.BlockSpec((None, MOD_ROWS, tn), lambda i, j: (i, 0, j)),
        compiler_params=_params(2),
        name="modulation",
    )(cvec, w_mod, b_mod.reshape(DEPTH, 1, n))
    return out.reshape(DEPTH * MOD_ROWS, 1, n)


def _rope(xh, cos4, sin4, first_half):
    swapped = jnp.where(first_half, pltpu.roll(xh, 96, 1), pltpu.roll(xh, 32, 1))
    return xh * cos4 + swapped * sin4


def _attn_in_kernel(*refs, rope, keep_f32):
    if rope:
        x_ref, mod_ref, w_ref, cos_ref, sin_ref = refs[:5]
        outs = refs[5:]
    else:
        x_ref, mod_ref, w_ref = refs[:3]
        outs = refs[3:]
    q_ref, k_ref, vt_ref, z_ref = outs[:4]
    h = _modulated(x_ref, mod_ref)
    if rope:
        lane = lax.broadcasted_iota(jnp.int32, (1, DV_A), 1)
        first_half = (lane % (2 * 32)) < 32
        cos4 = cos_ref[...]
        sin4 = sin_ref[...]
    q_all = jnp.dot(h, w_ref[:, 0:D_MODEL], preferred_element_type=F32)
    k_all = jnp.dot(h, w_ref[:, D_MODEL:2 * D_MODEL], preferred_element_type=F32)
    if keep_f32:
        outs[4][...] = k_all
    for hd in range(H_A):
        cols = slice(hd * DV_A, (hd + 1) * DV_A)
        q = q_all[:, cols]
        k = k_all[:, cols]
        if rope:
            q = _rope(q, cos4, sin4, first_half)
            k = _rope(k, cos4, sin4, first_half)
        q_ref[:, cols] = (q * SCORE_SCALE).astype(BF16)
        k_ref[:, cols] = k.astype(BF16)
    v = jnp.dot(h, w_ref[:, 2 * D_MODEL:3 * D_MODEL], preferred_element_type=F32)
    if keep_f32:
        outs[5][...] = v
    vt_ref[...] = v.T.astype(BF16)
    z_ref[...] = jnp.dot(h, w_ref[:, 3 * D_MODEL:4 * D_MODEL], preferred_element_type=F32)


def _attn_in(x2d, mod3, w_bf16, layer, rows_per_batch, rope_tables, keep_f32):
    m_rows = x2d.shape[0]
    tm = ROW_TILE
    rope = rope_tables is not None
    row_spec = pl.BlockSpec((tm, D_MODEL), lambda m: (m, 0))
    in_specs = [row_spec,
                pl.BlockSpec((None, 1, 3 * D_MODEL), _mod_row_map(layer, rows_per_batch, tm)),
                pl.BlockSpec((D_MODEL, 4 * D_MODEL), lambda m: (0, 0))]
    args = [x2d, mod3, w_bf16]
    if rope:
        seq_tiles = rows_per_batch // tm
        tab_spec = pl.BlockSpec((tm, DV_A), lambda m: (m % seq_tiles, 0))
        in_specs += [tab_spec, tab_spec]
        args += list(rope_tables)
    out_shape = [jax.ShapeDtypeStruct((m_rows, D_MODEL), BF16),
                 jax.ShapeDtypeStruct((m_rows, D_MODEL), BF16),
                 jax.ShapeDtypeStruct((D_MODEL, m_rows), BF16),
                 jax.ShapeDtypeStruct((m_rows, D_MODEL), F32)]
    out_specs = [row_spec, row_spec, pl.BlockSpec((D_MODEL, tm), lambda m: (0, m)), row_spec]
    if keep_f32:
        out_shape += [jax.ShapeDtypeStruct((m_rows, D_MODEL), F32)] * 2
        out_specs += [row_spec] * 2
    return pl.pallas_call(
        functools.partial(_attn_in_kernel, rope=rope, keep_f32=keep_f32),
        out_shape=out_shape,
        grid=(m_rows // tm,),
        in_specs=in_specs,
        out_specs=out_specs,
        compiler_params=_params(1),
        name="attn_in_rope" if rope else "attn_in",
    )(*args)


SCORE_ROWS = 128


def _slab_reduce(op, x):
    parts = [x[i:i + 8] for i in range(0, x.shape[0], 8)]
    while len(parts) > 1:
        parts = [op(parts[i], parts[i + 1]) for i in range(0, len(parts) - 1, 2)] + (
            [parts[-1]] if len(parts) % 2 else [])
    return parts[0]


def _attn_kernel(*refs, layer_idx, has_ctx):
    if has_ctx:
        (q_ref, k_ref, vt_ref, z_ref, x_ref, kc_ref, vc_ref, mod_ref, lam_ref, subln_ref,
         w_ref, g_ref, b_ref, o_ref, y_ref, st_ref, a_ref) = refs
    else:
        (q_ref, k_ref, vt_ref, z_ref, x_ref, mod_ref, lam_ref, subln_ref,
         w_ref, g_ref, b_ref, o_ref, y_ref, st_ref, a_ref) = refs
    tq = q_ref.shape[0]
    lam_init = 0.8 - 0.6 * math.exp(-0.3 * layer_idx)
    lm = lam_ref[...]
    lam = (jnp.exp(jnp.sum(lm[0:1] * lm[1:2], axis=-1, keepdims=True))
           - jnp.exp(jnp.sum(lm[2:3] * lm[3:4], axis=-1, keepdims=True)) + lam_init)
    lane = lax.broadcasted_iota(jnp.int32, (1, DV_A), 1)
    first = lane < DH_A
    subln = jnp.broadcast_to(subln_ref[...], (DV_A, tq))
    nt = (((1,), (1,)), ((), ()))
    t = k_ref.shape[0]
    s_total = st_ref.shape[0]
    for hd in range(H_A):
        cols = slice(hd * DV_A, (hd + 1) * DV_A)
        qh = q_ref[:, cols]
        zero = jnp.zeros_like(qh)
        qq = jnp.concatenate([jnp.where(first, qh, zero), jnp.where(first, zero, qh)], axis=0)
        st = lax.dot_general(k_ref[:, cols], qq, nt, preferred_element_type=F32)
        st_ref[0:t, :] = st
        m8 = _slab_reduce(jnp.maximum, st)
        if has_ctx:
            kc = kc_ref[:, cols].astype(BF16)
            sc = lax.dot_general(kc, qq, nt, preferred_element_type=F32)
            st_ref[t:s_total, :] = sc
            m8 = jnp.maximum(m8, _slab_reduce(jnp.maximum, sc))
        m = jnp.max(m8, axis=0, keepdims=True)
        l8 = None
        for c in range(s_total // SCORE_ROWS):
            rows = slice(c * SCORE_ROWS, (c + 1) * SCORE_ROWS)
            e = jnp.exp2(st_ref[rows, :] - m)
            st_ref[rows, :] = e
            part = _slab_reduce(jnp.add, e)
            l8 = part if l8 is None else l8 + part
        inv = 1.0 / jnp.sum(l8, axis=0, keepdims=True)
        inv1 = inv[:, :tq]
        inv2 = inv[:, tq:] * lam
        for c in range(s_total // SCORE_ROWS):
            rows = slice(c * SCORE_ROWS, (c + 1) * SCORE_ROWS)
            a_ref[rows, :] = (st_ref[rows, 0:tq] * inv1 - st_ref[rows, tq:2 * tq] * inv2).astype(BF16)
        ot = jnp.dot(vt_ref[cols, :], a_ref[0:t, :], preferred_element_type=F32)
        if has_ctx:
            vct = vc_ref[:, cols].T.astype(BF16)
            ot = ot + jnp.dot(vct, a_ref[t:s_total, :], preferred_element_type=F32)
        ot = ot * lax.rsqrt(jnp.mean(ot * ot, axis=0, keepdims=True) + RMS_EPS)
        ot = ot * subln * (1.0 - lam_init)
        y_ref[:, cols] = (ot.T * _silu(z_ref[:, cols])).astype(BF16)
    out = jnp.dot(y_ref[...], w_ref[...], preferred_element_type=F32)
    gate = mod_ref[:, 2 * D_MODEL:3 * D_MODEL]
    o_ref[...] = _residual_layer_norm(x_ref[...], out, gate, g_ref[...], b_ref[...])


def _attention(q, k, vt, z, x, seq_len, ctx, mod3, lam, subln, w_out, ln_g, ln_b, layer,
               per_batch_rows):
    m_rows = q.shape[0]
    t = seq_len
    b = m_rows // t
    tq = 256
    nq = t // tq
    has_ctx = ctx is not None
    q_spec = pl.BlockSpec((tq, D_MODEL), lambda i, j: (i * nq + j, 0))
    k_spec = pl.BlockSpec((t, D_MODEL), lambda i, j: (i, 0))
    vt_spec = pl.BlockSpec((D_MODEL, t), lambda i, j: (0, i))
    if per_batch_rows:
        mod_map = lambda i, j: (layer * MOD_ROWS + i, 0, 0)
    else:
        mod_map = lambda i, j: (layer * MOD_ROWS + CTX_ROW, 0, 0)
    const2 = lambda i, j: (0, 0)
    in_specs = [q_spec, k_spec, vt_spec, q_spec, q_spec]
    args = [q, k, vt, z, x]
    s_total = t
    if has_ctx:
        cache_k, cache_v, jj = ctx
        s_total = t + cache_k.shape[2]
        c_spec = pl.BlockSpec((None, None, cache_k.shape[2], D_MODEL), lambda i, j: (i, jj, 0, 0))
        in_specs += [c_spec, c_spec]
        args += [cache_k, cache_v]
    in_specs += [pl.BlockSpec((None, 1, 3 * D_MODEL), mod_map),
                 pl.BlockSpec((4, DH_A), const2),
                 pl.BlockSpec((DV_A, 1), const2),
                 pl.BlockSpec((D_MODEL, D_MODEL), const2),
                 pl.BlockSpec((1, D_MODEL), const2),
                 pl.BlockSpec((1, D_MODEL), const2)]
    args += [mod3, lam, subln.reshape(DV_A, 1), w_out, ln_g.reshape(1, D_MODEL),
             ln_b.reshape(1, D_MODEL)]
    return pl.pallas_call(
        functools.partial(_attn_kernel, layer_idx=layer, has_ctx=has_ctx),
        out_shape=jax.ShapeDtypeStruct((m_rows, D_MODEL), F32),
        grid=(b, nq),
        in_specs=in_specs,
        out_specs=q_spec,
        scratch_shapes=[pltpu.VMEM((tq, D_MODEL), BF16),
                        pltpu.VMEM((s_total, 2 * tq), F32),
                        pltpu.VMEM((s_total, tq), BF16)],
        compiler_params=_params(2),
        name="diff_attn_ctx" if has_ctx else "diff_attn",
    )(*args)


def _rope_tables(n_tokens):
    rows = n_tokens // GRID_W
    r = jnp.repeat(jnp.arange(rows, dtype=F32), GRID_W)
    col = jnp.tile(jnp.arange(GRID_W, dtype=F32), rows)
    n_freq = DH_A // 4
    inv = ROPE_BASE ** (-jnp.arange(n_freq, dtype=F32) / n_freq)
    ang = jnp.concatenate([r[:, None] * inv, col[:, None] * inv], -1)
    cos, sin = jnp.cos(ang), jnp.sin(ang)
    return jnp.tile(cos, (1, 4)), jnp.concatenate([-sin, sin, -sin, sin], -1)


def _ret_in_kernel(x_ref, mod_ref, w_ref, q_ref, k_ref, v_ref, g_ref):
    h = _modulated(x_ref, mod_ref)
    nq = H_B * DK_B
    q_ref[...] = jnp.dot(h, w_ref[:, 0:nq], preferred_element_type=F32).astype(BF16)
    k_ref[...] = jnp.dot(h, w_ref[:, nq:2 * nq], preferred_element_type=F32) * (DK_B ** -0.5)
    for c in range(E_B // 1024):
        v = jnp.dot(h, w_ref[:, 2 * nq + c * 1024:2 * nq + (c + 1) * 1024],
                    preferred_element_type=F32)
        v_ref[:, c * 1024:(c + 1) * 1024] = v.astype(BF16)
    for c in range(E_B // 1024):
        g_ref[:, c * 1024:(c + 1) * 1024] = jnp.dot(
            h, w_ref[:, 2 * nq + E_B + c * 1024:2 * nq + E_B + (c + 1) * 1024],
            preferred_element_type=F32)


def _ret_in(x2d, mod3, w_bf16, layer, rows_per_batch):
    m_rows = x2d.shape[0]
    tm = ROW_TILE
    nq = H_B * DK_B
    return pl.pallas_call(
        _ret_in_kernel,
        out_shape=[jax.ShapeDtypeStruct((m_rows, nq), BF16),
                   jax.ShapeDtypeStruct((m_rows, nq), F32),
                   jax.ShapeDtypeStruct((m_rows, E_B), BF16),
                   jax.ShapeDtypeStruct((m_rows, E_B), F32)],
        grid=(m_rows // tm,),
        in_specs=[pl.BlockSpec((tm, D_MODEL), lambda m: (m, 0)),
                  pl.BlockSpec((None, 1, 3 * D_MODEL), _mod_row_map(layer, rows_per_batch, tm)),
                  pl.BlockSpec((D_MODEL, 2 * nq + 2 * E_B), lambda m: (0, 0))],
        out_specs=[pl.BlockSpec((tm, nq), lambda m: (m, 0)),
                   pl.BlockSpec((tm, nq), lambda m: (m, 0)),
                   pl.BlockSpec((tm, E_B), lambda m: (m, 0)),
                   pl.BlockSpec((tm, E_B), lambda m: (m, 0))],
        compiler_params=_params(1),
        name="ret_in",
    )(x2d, mod3, w_bf16)


def _retention_kernel(*refs, has_state):
    if has_state:
        q_ref, k_ref, v_ref, g_ref, af_ref, ab_ref, s0f_ref, s0b_ref, y_ref, o_ref = refs
    else:
        q_ref, k_ref, v_ref, g_ref, af_ref, ab_ref, y_ref, sf_ref, sb_ref, o_ref = refs
    t = q_ref.shape[0]
    nc = t // CHUNK
    lg_f = jnp.log1p(-jnp.exp(af_ref[...]))
    lg_b = jnp.log1p(-jnp.exp(ab_ref[...]))
    row = lax.broadcasted_iota(jnp.int32, (CHUNK, CHUNK), 0).astype(F32)
    col = lax.broadcasted_iota(jnp.int32, (CHUNK, CHUNK), 1).astype(F32)
    diff = row - col
    dmask = (jnp.where(diff >= 0, jnp.exp(jnp.maximum(diff, 0.0) * lg_f), 0.0)
             + jnp.where(diff <= 0, jnp.exp(jnp.maximum(-diff, 0.0) * lg_b), 0.0))
    idx = lax.broadcasted_iota(jnp.int32, (CHUNK, 1), 0).astype(F32)
    qd_f = jnp.exp((idx + 1.0) * lg_f)
    kd_f = jnp.exp((CHUNK - 1.0 - idx) * lg_f)
    qd_b = jnp.exp((CHUNK - idx) * lg_b)
    kd_b = jnp.exp(idx * lg_b)
    cd_f = jnp.exp(CHUNK * lg_f)
    cd_b = jnp.exp(CHUNK * lg_b)
    nt = (((1,), (1,)), ((), ()))
    tn = (((0,), (0,)), ((), ()))

    def chunk(c):
        return pl.ds(c * CHUNK, CHUNK)

    for c in range(nc):
        qc = q_ref[chunk(c), :]
        kc = k_ref[chunk(c), :]
        qk = lax.dot_general(qc, kc.astype(BF16), nt, preferred_element_type=F32)
        o_ref[chunk(c), :] = jnp.dot((qk * dmask).astype(BF16), v_ref[chunk(c), :],
                                     preferred_element_type=F32)

    def direction(order, qd, kd, cd, s0):
        s = s0
        for c in order:
            if s is not None:
                o_ref[chunk(c), :] += jnp.dot(q_ref[chunk(c), :], s.astype(BF16),
                                              preferred_element_type=F32) * qd
            u = lax.dot_general((k_ref[chunk(c), :] * kd).astype(BF16), v_ref[chunk(c), :], tn,
                                preferred_element_type=F32)
            s = u if s is None else s * cd + u
        return s

    s_f = direction(range(nc), qd_f, kd_f, cd_f, s0f_ref[...] if has_state else None)
    s_b = direction(range(nc - 1, -1, -1), qd_b, kd_b, cd_b, s0b_ref[...] if has_state else None)
    if not has_state:
        sf_ref[...] = s_f
        sb_ref[...] = s_b
    o = o_ref[...]
    o = o * lax.rsqrt(jnp.mean(o * o, axis=-1, keepdims=True) + RMS_EPS)
    y_ref[...] = (o * _silu(g_ref[...])).astype(BF16)


def _retention(q, k, v, g, decay_f, decay_b, states):
    b, t, _ = q.shape
    has_state = states is not None
    qk_spec = pl.BlockSpec((None, t, DK_B), lambda i, h: (i, 0, h))
    vg_spec = pl.BlockSpec((None, t, DV_B), lambda i, h: (i, 0, h))
    a_spec = pl.BlockSpec((None, 1, 1), lambda i, h: (h, 0, 0))
    in_specs = [qk_spec, qk_spec, vg_spec, vg_spec, a_spec, a_spec]
    args = [q, k, v, g, decay_f.reshape(H_B, 1, 1), decay_b.reshape(H_B, 1, 1)]
    out_shape = [jax.ShapeDtypeStruct((b, t, E_B), BF16)]
    out_specs = [vg_spec]
    if has_state:
        s_f, s_b, jj = states
        s_spec = pl.BlockSpec((None, None, None, DK_B, DV_B), lambda i, h: (i, jj, h, 0, 0))
        in_specs += [s_spec, s_spec]
        args += [s_f, s_b]
    else:
        so_spec = pl.BlockSpec((None, None, None, DK_B, DV_B), lambda i, h: (i, 0, h, 0, 0))
        out_shape += [jax.ShapeDtypeStruct((b, 1, H_B, DK_B, DV_B), F32)] * 2
        out_specs += [so_spec, so_spec]
    res = pl.pallas_call(
        functools.partial(_retention_kernel, has_state=has_state),
        out_shape=out_shape,
        grid=(b, H_B),
        in_specs=in_specs,
        out_specs=out_specs,
        scratch_shapes=[pltpu.VMEM((t, DV_B), F32)],
        compiler_params=_params(2),
        name="retention_state" if has_state else "retention",
    )(*args)
    return res


def _out_proj_kernel(y_ref, w_ref, x_ref, mod_ref, g_ref, b_ref, o_ref):
    out = jnp.dot(y_ref[...], w_ref[...], preferred_element_type=F32)
    gate = mod_ref[:, 2 * D_MODEL:3 * D_MODEL]
    o_ref[...] = _residual_layer_norm(x_ref[...], out, gate, g_ref[...], b_ref[...])


def _out_proj(y2d, w_bf16, x2d, mod3, ln_g, ln_b, layer, rows_per_batch):
    m_rows, e = y2d.shape
    tm = ROW_TILE
    const2 = lambda m: (0, 0)
    return pl.pallas_call(
        _out_proj_kernel,
        out_shape=jax.ShapeDtypeStruct((m_rows, D_MODEL), F32),
        grid=(m_rows // tm,),
        in_specs=[pl.BlockSpec((tm, e), lambda m: (m, 0)),
                  pl.BlockSpec((e, D_MODEL), const2),
                  pl.BlockSpec((tm, D_MODEL), lambda m: (m, 0)),
                  pl.BlockSpec((None, 1, 3 * D_MODEL), _mod_row_map(layer, rows_per_batch, tm)),
                  pl.BlockSpec((1, D_MODEL), const2),
                  pl.BlockSpec((1, D_MODEL), const2)],
        out_specs=pl.BlockSpec((tm, D_MODEL), lambda m: (m, 0)),
        compiler_params=_params(1),
        name="out_proj",
    )(y2d, w_bf16, x2d, mod3, ln_g.reshape(1, D_MODEL), ln_b.reshape(1, D_MODEL))


CONV_ROWS = 1024
CONV_COLS = 256


def _conv_kernel(x_ref, mod_ref, w_in_ref, cw_ref, w_out_ref, g_ref, b_ref, o_ref, acc_ref, *,
                 seq_len):
    h = _modulated(x_ref, mod_ref)
    rows = x_ref.shape[0]
    pos = lax.broadcasted_iota(jnp.int32, (rows, 1), 0) % seq_len
    has_prev = pos > 0
    has_next = pos < seq_len - 1
    e = D_MODEL
    for c in range(e // CONV_COLS):
        cols = slice(c * CONV_COLS, (c + 1) * CONV_COLS)

        def proj(part):
            lo = part * e + c * CONV_COLS
            return jnp.dot(h, w_in_ref[:, lo:lo + CONV_COLS], preferred_element_type=F32)

        p = proj(1) * proj(2)
        prev = jnp.where(has_prev, pltpu.roll(p, 1, 0), 0.0)
        nxt = jnp.where(has_next, pltpu.roll(p, rows - 1, 0), 0.0)
        conv = prev * cw_ref[0:1, cols] + p * cw_ref[1:2, cols] + nxt * cw_ref[2:3, cols]
        y = (proj(0) * conv * _silu(proj(3))).astype(BF16)
        part = jnp.dot(y, w_out_ref[cols, :], preferred_element_type=F32)
        if c == 0:
            acc_ref[...] = part
        else:
            acc_ref[...] += part
    gate = mod_ref[:, 2 * D_MODEL:3 * D_MODEL]
    o_ref[...] = _residual_layer_norm(x_ref[...], acc_ref[...], gate, g_ref[...], b_ref[...])


def _conv_layer(x2d, mod3, w_in, conv_w, w_out, ln_g, ln_b, layer, rows_per_batch, seq_len):
    m_rows = x2d.shape[0]
    tm = CONV_ROWS
    const2 = lambda m: (0, 0)
    return pl.pallas_call(
        functools.partial(_conv_kernel, seq_len=seq_len),
        out_shape=jax.ShapeDtypeStruct((m_rows, D_MODEL), F32),
        grid=(m_rows // tm,),
        in_specs=[pl.BlockSpec((tm, D_MODEL), lambda m: (m, 0)),
                  pl.BlockSpec((None, 1, 3 * D_MODEL), _mod_row_map(layer, rows_per_batch, tm)),
                  pl.BlockSpec((D_MODEL, 4 * D_MODEL), const2),
                  pl.BlockSpec((3, D_MODEL), const2),
                  pl.BlockSpec((D_MODEL, D_MODEL), const2),
                  pl.BlockSpec((1, D_MODEL), const2),
                  pl.BlockSpec((1, D_MODEL), const2)],
        out_specs=pl.BlockSpec((tm, D_MODEL), lambda m: (m, 0)),
        scratch_shapes=[pltpu.VMEM((tm, D_MODEL), F32)],
        compiler_params=_params(1),
        name="conv_layer",
    )(x2d, mod3, w_in, conv_w, w_out, ln_g.reshape(1, D_MODEL), ln_b.reshape(1, D_MODEL))


def kernel(x_prompt, x_sample, cache_k, cache_v, state_fwd, state_bwd, c, c_ctx, w_mod, b_mod, ln_g,
           ln_b, w_in_a, lam_a, subln_a, w_out_a, w_in_b, decay_fwd, decay_bwd, w_out_b, w_in_c,
           conv_c, w_out_c):
    bp, tp, d = x_prompt.shape
    bs, ts, _ = x_sample.shape
    l_c = cache_k.shape[2]
    cvec = jnp.concatenate([c, c_ctx[None], jnp.zeros((MOD_ROWS - bs - 1, d), F32)], axis=0)
    mod3 = _modulation(cvec, w_mod, b_mod)
    cache_k4 = cache_k.reshape(bs, -1, l_c, d)
    cache_v4 = cache_v.reshape(bs, -1, l_c, d)
    rope_tables = _rope_tables(ts)

    xp = x_prompt.reshape(bp * tp, d)
    xs = x_sample.reshape(bs * ts, d)
    new_k, new_v, new_sf, new_sb = [], [], [], []
    for i in range(DEPTH):
        kind, j = i % N_MIXERS, i // N_MIXERS
        if kind == 0:
            w_in = w_in_a[j].astype(BF16)
            w_out = w_out_a[j].astype(BF16)
            q, k, vt, z, k32, v32 = _attn_in(xp, mod3, w_in, i, None, None, True)
            new_k.append(k32.reshape(bp, tp, H_A, 2 * DH_A))
            new_v.append(v32.reshape(bp, tp, H_A, DV_A))
            xp = _attention(q, k, vt, z, xp, tp, None, mod3, lam_a[j], subln_a[j], w_out, ln_g[i],
                            ln_b[i], i, False)
            q, k, vt, z = _attn_in(xs, mod3, w_in, i, ts, rope_tables, False)
            xs = _attention(q, k, vt, z, xs, ts, (cache_k4, cache_v4, j), mod3, lam_a[j],
                            subln_a[j], w_out, ln_g[i], ln_b[i], i, True)
        elif kind == 1:
            w_in = w_in_b[j].astype(BF16)
            w_out = w_out_b[j].astype(BF16)
            q, k, v, g = _ret_in(xp, mod3, w_in, i, None)
            y, s_f, s_b = _retention(q.reshape(bp, tp, -1), k.reshape(bp, tp, -1),
                                     v.reshape(bp, tp, -1), g.reshape(bp, tp, -1),
                                     decay_fwd[j], decay_bwd[j], None)
            new_sf.append(s_f)
            new_sb.append(s_b)
            xp = _out_proj(y.reshape(bp * tp, E_B), w_out, xp, mod3, ln_g[i], ln_b[i], i, None)
            q, k, v, g = _ret_in(xs, mod3, w_in, i, ts)
            (y,) = _retention(q.reshape(bs, ts, -1), k.reshape(bs, ts, -1), v.reshape(bs, ts, -1),
                              g.reshape(bs, ts, -1), decay_fwd[j], decay_bwd[j],
                              (state_fwd, state_bwd, j))
            xs = _out_proj(y.reshape(bs * ts, E_B), w_out, xs, mod3, ln_g[i], ln_b[i], i, ts)
        else:
            w_in = w_in_c[j].astype(BF16)
            w_out = w_out_c[j].astype(BF16)
            xp = _conv_layer(xp, mod3, w_in, conv_c[j], w_out, ln_g[i], ln_b[i], i, None, tp)
            xs = _conv_layer(xs, mod3, w_in, conv_c[j], w_out, ln_g[i], ln_b[i], i, ts, ts)
    y_prompt = xp.reshape(bp, tp, d)
    y_sample = xs.reshape(bs, ts, d)
    new_cache_k = jnp.stack(new_k, axis=1)
    new_cache_v = jnp.stack(new_v, axis=1)
    new_state_fwd = jnp.concatenate(new_sf, axis=1)
    new_state_bwd = jnp.concatenate(new_sb, axis=1)
    return (y_prompt, y_sample, new_cache_k, new_cache_v, new_state_fwd, new_state_bwd)
```

```python
import functools
import math

import jax
import jax.numpy as jnp
from jax import lax
from jax.experimental import pallas as pl
from jax.experimental.pallas import tpu as pltpu

F32 = jnp.float32
BF16 = jnp.bfloat16

D_MODEL = 1024
DEPTH = 4
N_MIXERS = 3
GRID_W = 64
H_A = 8
DH_A = 64
DV_A = 128
SCORE_SCALE = DH_A ** -0.5 * math.log2(math.e)
H_B = 4
DK_B = 256
DV_B = 512
E_B = H_B * DV_B
CHUNK = 128
ALPHA = (2.0 * DEPTH) ** 0.25
ROPE_BASE = 10000.0
LN_EPS = 1e-5
RMS_EPS = 1e-6

MOD_ROWS = 8
CTX_ROW = 4
VMEM_LIMIT_BYTES = 56 * 1024 * 1024
ROW_TILE = 512


def _params(n_axes):
    return pltpu.CompilerParams(dimension_semantics=("arbitrary",) * n_axes,
                                vmem_limit_bytes=VMEM_LIMIT_BYTES)


def _silu(x):
    return x * jax.nn.sigmoid(x)


def _residual_layer_norm(x, out, gate, g, b):
    r = ALPHA * x + gate * out
    mu = jnp.mean(r, axis=-1, keepdims=True)
    d = r - mu
    var = jnp.mean(d * d, axis=-1, keepdims=True)
    return d * lax.rsqrt(var + LN_EPS) * g + b


def _modulated(x_ref, mod_ref):
    shift = mod_ref[:, 0:D_MODEL]
    scale = mod_ref[:, D_MODEL:2 * D_MODEL]
    return x_ref[...] * (1.0 + scale) + shift


def _mod_row_map(layer, rows_per_batch, tile):
    if rows_per_batch is None:
        return lambda m, *_: (layer * MOD_ROWS + CTX_ROW, 0, 0)
    return lambda m, *_: (layer * MOD_ROWS + (m * tile) // rows_per_batch, 0, 0)


def _weight_spec(w, j):
    return pl.BlockSpec((None,) + w.shape[1:], lambda *_: (j, 0, 0), pipeline_mode=pl.Buffered(1))


def _mod_kernel(cv_ref, w_ref, b_ref, o_ref):
    s = _silu(cv_ref[...])
    o_ref[...] = jnp.dot(s, w_ref[...], preferred_element_type=F32) + b_ref[...]


def _modulation(cvec, w_mod, b_mod):
    tn = 1024
    n = 3 * D_MODEL
    out = pl.pallas_call(
        _mod_kernel,
        out_shape=jax.ShapeDtypeStruct((DEPTH, MOD_ROWS, n), F32),
        grid=(DEPTH, n // tn),
        in_specs=[pl.BlockSpec((MOD_ROWS, D_MODEL), lambda i, j: (0, 0)),
                  pl.BlockSpec((None, D_MODEL, tn), lambda i, j: (i, 0, j)),
                  pl.BlockSpec((None, 1, tn), lambda i, j: (i, 0, j))],
        out_specs=pl.BlockSpec((None, MOD_ROWS, tn), lambda i, j: (i, 0, j)),
        compiler_params=_params(2),
        name="modulation",
    )(cvec, w_mod, b_mod.reshape(DEPTH, 1, n))
    return out.reshape(DEPTH * MOD_ROWS, 1, n)


def _rope(xh, cos4, sin4, first_half):
    swapped = jnp.where(first_half, pltpu.roll(xh, 96, 1), pltpu.roll(xh, 32, 1))
    return xh * cos4 + swapped * sin4


def _store_heads(o_ref, x, seq_len):
    for b in range(o_ref.shape[0]):
        o_ref[b] = x[b * seq_len:(b + 1) * seq_len].reshape(seq_len, H_A, DV_A)


def _attn_in_kernel(*refs, rope, keep_f32, seq_len, n_aliased):
    if rope:
        x_ref, mod_ref, w_ref, cos_ref, sin_ref = refs[:5]
        outs = refs[5 + n_aliased:]
    else:
        x_ref, mod_ref, w_ref = refs[:3]
        outs = refs[3 + n_aliased:]
    q_ref, k_ref, vt_ref, z_ref = outs[:4]
    h = _modulated(x_ref, mod_ref)
    if rope:
        lane = lax.broadcasted_iota(jnp.int32, (1, DV_A), 1)
        first_half = (lane % (2 * 32)) < 32
        cos4 = cos_ref[...]
        sin4 = sin_ref[...]
    q_all = jnp.dot(h, w_ref[:, 0:D_MODEL], preferred_element_type=F32)
    k_all = jnp.dot(h, w_ref[:, D_MODEL:2 * D_MODEL], preferred_element_type=F32)
    if keep_f32:
        _store_heads(outs[4], k_all, seq_len)
    for hd in range(H_A):
        cols = slice(hd * DV_A, (hd + 1) * DV_A)
        q = q_all[:, cols]
        k = k_all[:, cols]
        if rope:
            q = _rope(q, cos4, sin4, first_half)
            k = _rope(k, cos4, sin4, first_half)
        q_ref[:, cols] = (q * SCORE_SCALE).astype(BF16)
        k_ref[:, cols] = k.astype(BF16)
    v = jnp.dot(h, w_ref[:, 2 * D_MODEL:3 * D_MODEL], preferred_element_type=F32)
    if keep_f32:
        _store_heads(outs[5], v, seq_len)
    vt_ref[...] = v.T.astype(BF16)
    z_ref[...] = jnp.dot(h, w_ref[:, 3 * D_MODEL:4 * D_MODEL], preferred_element_type=F32)


def _attn_in(x2d, mod3, w_in, j, layer, rows_per_batch, rope_tables, cache_out):
    m_rows = x2d.shape[0]
    tm = ROW_TILE
    rope = rope_tables is not None
    row_spec = pl.BlockSpec((tm, D_MODEL), lambda m: (m, 0))
    in_specs = [row_spec,
                pl.BlockSpec((None, 1, 3 * D_MODEL), _mod_row_map(layer, rows_per_batch, tm)),
                _weight_spec(w_in, j)]
    args = [x2d, mod3, w_in]
    if rope:
        seq_tiles = rows_per_batch // tm
        tab_spec = pl.BlockSpec((tm, DV_A), lambda m: (m % seq_tiles, 0))
        in_specs += [tab_spec, tab_spec]
        args += list(rope_tables)
    out_shape = [jax.ShapeDtypeStruct((m_rows, D_MODEL), BF16),
                 jax.ShapeDtypeStruct((m_rows, D_MODEL), BF16),
                 jax.ShapeDtypeStruct((D_MODEL, m_rows), BF16),
                 jax.ShapeDtypeStruct((m_rows, D_MODEL), F32)]
    out_specs = [row_spec, row_spec, pl.BlockSpec((D_MODEL, tm), lambda m: (0, m)), row_spec]
    aliases = {}
    seq_len = None
    if cache_out is not None:
        new_k, new_v = cache_out
        seq_len = new_k.shape[2]
        cache_shape = jax.ShapeDtypeStruct(new_k.shape, F32)
        cache_spec = pl.BlockSpec((tm // seq_len, None, seq_len, H_A, DV_A),
                                  lambda m: (m, j, 0, 0, 0))
        out_shape += [cache_shape, cache_shape]
        out_specs += [cache_spec, cache_spec]
        if j > 0:
            in_specs += [pl.BlockSpec(memory_space=pl.ANY)] * 2
            args += [new_k, new_v]
            aliases = {len(args) - 2: 4, len(args) - 1: 5}
    return pl.pallas_call(
        functools.partial(_attn_in_kernel, rope=rope, keep_f32=cache_out is not None,
                          seq_len=seq_len, n_aliased=len(aliases)),
        out_shape=out_shape,
        grid=(m_rows // tm,),
        in_specs=in_specs,
        out_specs=out_specs,
        input_output_aliases=aliases,
        compiler_params=_params(1),
        name="attn_in_rope" if rope else "attn_in",
    )(*args)


SCORE_ROWS = 128


def _slab_reduce(op, x):
    parts = [x[i:i + 8] for i in range(0, x.shape[0], 8)]
    while len(parts) > 1:
        parts = [op(parts[i], parts[i + 1]) for i in range(0, len(parts) - 1, 2)] + (
            [parts[-1]] if len(parts) % 2 else [])
    return parts[0]


def _attn_kernel(*refs, layer_idx, has_ctx):
    if has_ctx:
        (q_ref, k_ref, vt_ref, z_ref, x_ref, kc_ref, vc_ref, mod_ref, lam_ref, subln_ref,
         w_ref, g_ref, b_ref, o_ref, y_ref, st_ref, a_ref) = refs
    else:
        (q_ref, k_ref, vt_ref, z_ref, x_ref, mod_ref, lam_ref, subln_ref,
         w_ref, g_ref, b_ref, o_ref, y_ref, st_ref, a_ref) = refs
    tq = q_ref.shape[0]
    lam_init = 0.8 - 0.6 * math.exp(-0.3 * layer_idx)
    lm = lam_ref[...]
    lam = (jnp.exp(jnp.sum(lm[0:1] * lm[1:2], axis=-1, keepdims=True))
           - jnp.exp(jnp.sum(lm[2:3] * lm[3:4], axis=-1, keepdims=True)) + lam_init)
    lane = lax.broadcasted_iota(jnp.int32, (1, DV_A), 1)
    first = lane < DH_A
    subln = jnp.broadcast_to(subln_ref[...], (DV_A, tq))
    nt = (((1,), (1,)), ((), ()))
    t = k_ref.shape[0]
    s_total = st_ref.shape[0]
    for hd in range(H_A):
        cols = slice(hd * DV_A, (hd + 1) * DV_A)
        qh = q_ref[:, cols]
        zero = jnp.zeros_like(qh)
        qq = jnp.concatenate([jnp.where(first, qh, zero), jnp.where(first, zero, qh)], axis=0)
        st = lax.dot_general(k_ref[:, cols], qq, nt, preferred_element_type=F32)
        st_ref[0:t, :] = st
        m8 = _slab_reduce(jnp.maximum, st)
        if has_ctx:
            kc = kc_ref[:, cols].astype(BF16)
            sc = lax.dot_general(kc, qq, nt, preferred_element_type=F32)
            st_ref[t:s_total, :] = sc
            m8 = jnp.maximum(m8, _slab_reduce(jnp.maximum, sc))
        m = jnp.max(m8, axis=0, keepdims=True)
        l8 = None
        for c in range(s_total // SCORE_ROWS):
            rows = slice(c * SCORE_ROWS, (c + 1) * SCORE_ROWS)
            e = jnp.exp2(st_ref[rows, :] - m)
            st_ref[rows, :] = e
            part = _slab_reduce(jnp.add, e)
            l8 = part if l8 is None else l8 + part
        inv = 1.0 / jnp.sum(l8, axis=0, keepdims=True)
        inv1 = inv[:, :tq]
        inv2 = inv[:, tq:] * lam
        for c in range(s_total // SCORE_ROWS):
            rows = slice(c * SCORE_ROWS, (c + 1) * SCORE_ROWS)
            a_ref[rows, :] = (st_ref[rows, 0:tq] * inv1 - st_ref[rows, tq:2 * tq] * inv2).astype(BF16)
        ot = jnp.dot(vt_ref[cols, :], a_ref[0:t, :], preferred_element_type=F32)
        if has_ctx:
            vct = vc_ref[:, cols].T.astype(BF16)
            ot = ot + jnp.dot(vct, a_ref[t:s_total, :], preferred_element_type=F32)
        ot = ot * lax.rsqrt(jnp.mean(ot * ot, axis=0, keepdims=True) + RMS_EPS)
        ot = ot * subln * (1.0 - lam_init)
        y_ref[:, cols] = (ot.T * _silu(z_ref[:, cols])).astype(BF16)
    out = jnp.dot(y_ref[...], w_ref[...].astype(BF16), preferred_element_type=F32)
    gate = mod_ref[:, 2 * D_MODEL:3 * D_MODEL]
    o_ref[...] = _residual_layer_norm(x_ref[...], out, gate, g_ref[...], b_ref[...])


def _attention(q, k, vt, z, x, seq_len, ctx, mod3, lam, subln, w_out, j, ln_g, ln_b, layer,
               per_batch_rows):
    m_rows = q.shape[0]
    t = seq_len
    b = m_rows // t
    tq = 256
    nq = t // tq
    has_ctx = ctx is not None
    q_spec = pl.BlockSpec((tq, D_MODEL), lambda i, j: (i * nq + j, 0))
    k_spec = pl.BlockSpec((t, D_MODEL), lambda i, j: (i, 0))
    vt_spec = pl.BlockSpec((D_MODEL, t), lambda i, j: (0, i))
    if per_batch_rows:
        mod_map = lambda i, j: (layer * MOD_ROWS + i, 0, 0)
    else:
        mod_map = lambda i, j: (layer * MOD_ROWS + CTX_ROW, 0, 0)
    const2 = lambda i, j: (0, 0)
    in_specs = [q_spec, k_spec, vt_spec, q_spec, q_spec]
    args = [q, k, vt, z, x]
    s_total = t
    if has_ctx:
        cache_k, cache_v, jj = ctx
        s_total = t + cache_k.shape[2]
        c_spec = pl.BlockSpec((None, None, cache_k.shape[2], D_MODEL), lambda i, j: (i, jj, 0, 0))
        in_specs += [c_spec, c_spec]
        args += [cache_k, cache_v]
    in_specs += [pl.BlockSpec((None, 1, 3 * D_MODEL), mod_map),
                 pl.BlockSpec((4, DH_A), const2),
                 pl.BlockSpec((DV_A, 1), const2),
                 _weight_spec(w_out, j),
                 pl.BlockSpec((1, D_MODEL), const2),
                 pl.BlockSpec((1, D_MODEL), const2)]
    args += [mod3, lam, subln.reshape(DV_A, 1), w_out, ln_g.reshape(1, D_MODEL),
             ln_b.reshape(1, D_MODEL)]
    return pl.pallas_call(
        functools.partial(_attn_kernel, layer_idx=layer, has_ctx=has_ctx),
        out_shape=jax.ShapeDtypeStruct((m_rows, D_MODEL), F32),
        grid=(b, nq),
        in_specs=in_specs,
        out_specs=q_spec,
        scratch_shapes=[pltpu.VMEM((tq, D_MODEL), BF16),
                        pltpu.VMEM((s_total, 2 * tq), F32),
                        pltpu.VMEM((s_total, tq), BF16)],
        compiler_params=_params(2),
        name="diff_attn_ctx" if has_ctx else "diff_attn",
    )(*args)


def _rope_tables(n_tokens):
    rows = n_tokens // GRID_W
    r = jnp.repeat(jnp.arange(rows, dtype=F32), GRID_W)
    col = jnp.tile(jnp.arange(GRID_W, dtype=F32), rows)
    n_freq = DH_A // 4
    inv = ROPE_BASE ** (-jnp.arange(n_freq, dtype=F32) / n_freq)
    ang = jnp.concatenate([r[:, None] * inv, col[:, None] * inv], -1)
    cos, sin = jnp.cos(ang), jnp.sin(ang)
    return jnp.tile(cos, (1, 4)), jnp.concatenate([-sin, sin, -sin, sin], -1)


def _ret_in_kernel(x_ref, mod_ref, w_ref, q_ref, k_ref, v_ref, g_ref):
    h = _modulated(x_ref, mod_ref)
    nq = H_B * DK_B
    q_ref[...] = jnp.dot(h, w_ref[:, 0:nq], preferred_element_type=F32).astype(BF16)
    k_ref[...] = jnp.dot(h, w_ref[:, nq:2 * nq], preferred_element_type=F32) * (DK_B ** -0.5)
    for c in range(E_B // 1024):
        v = jnp.dot(h, w_ref[:, 2 * nq + c * 1024:2 * nq + (c + 1) * 1024],
                    preferred_element_type=F32)
        v_ref[:, c * 1024:(c + 1) * 1024] = v.astype(BF16)
    for c in range(E_B // 1024):
        g_ref[:, c * 1024:(c + 1) * 1024] = jnp.dot(
            h, w_ref[:, 2 * nq + E_B + c * 1024:2 * nq + E_B + (c + 1) * 1024],
            preferred_element_type=F32)


def _ret_in(x2d, mod3, w_in, j, layer, rows_per_batch):
    m_rows = x2d.shape[0]
    tm = ROW_TILE
    nq = H_B * DK_B
    return pl.pallas_call(
        _ret_in_kernel,
        out_shape=[jax.ShapeDtypeStruct((m_rows, nq), BF16),
                   jax.ShapeDtypeStruct((m_rows, nq), F32),
                   jax.ShapeDtypeStruct((m_rows, E_B), BF16),
                   jax.ShapeDtypeStruct((m_rows, E_B), F32)],
        grid=(m_rows // tm,),
        in_specs=[pl.BlockSpec((tm, D_MODEL), lambda m: (m, 0)),
                  pl.BlockSpec((None, 1, 3 * D_MODEL), _mod_row_map(layer, rows_per_batch, tm)),
                  _weight_spec(w_in, j)],
        out_specs=[pl.BlockSpec((tm, nq), lambda m: (m, 0)),
                   pl.BlockSpec((tm, nq), lambda m: (m, 0)),
                   pl.BlockSpec((tm, E_B), lambda m: (m, 0)),
                   pl.BlockSpec((tm, E_B), lambda m: (m, 0))],
        compiler_params=_params(1),
        name="ret_in",
    )(x2d, mod3, w_in)


def _retention_kernel(*refs, has_state):
    if has_state:
        q_ref, k_ref, v_ref, g_ref, af_ref, ab_ref, s0f_ref, s0b_ref, y_ref, o_ref = refs
    else:
        q_ref, k_ref, v_ref, g_ref, af_ref, ab_ref, y_ref, sf_ref, sb_ref, o_ref = refs
    t = q_ref.shape[0]
    nc = t // CHUNK
    lg_f = jnp.log1p(-jnp.exp(af_ref[...]))
    lg_b = jnp.log1p(-jnp.exp(ab_ref[...]))
    row = lax.broadcasted_iota(jnp.int32, (CHUNK, CHUNK), 0).astype(F32)
    col = lax.broadcasted_iota(jnp.int32, (CHUNK, CHUNK), 1).astype(F32)
    diff = row - col
    dmask = (jnp.where(diff >= 0, jnp.exp(jnp.maximum(diff, 0.0) * lg_f), 0.0)
             + jnp.where(diff <= 0, jnp.exp(jnp.maximum(-diff, 0.0) * lg_b), 0.0))
    idx = lax.broadcasted_iota(jnp.int32, (CHUNK, 1), 0).astype(F32)
    qd_f = jnp.exp((idx + 1.0) * lg_f)
    kd_f = jnp.exp((CHUNK - 1.0 - idx) * lg_f)
    qd_b = jnp.exp((CHUNK - idx) * lg_b)
    kd_b = jnp.exp(idx * lg_b)
    cd_f = jnp.exp(CHUNK * lg_f)
    cd_b = jnp.exp(CHUNK * lg_b)
    nt = (((1,), (1,)), ((), ()))
    tn = (((0,), (0,)), ((), ()))

    def chunk(c):
        return pl.ds(c * CHUNK, CHUNK)

    for c in range(nc):
        qc = q_ref[chunk(c), :]
        kc = k_ref[chunk(c), :]
        qk = lax.dot_general(qc, kc.astype(BF16), nt, preferred_element_type=F32)
        o_ref[chunk(c), :] = jnp.dot((qk * dmask).astype(BF16), v_ref[chunk(c), :],
                                     preferred_element_type=F32)

    def direction(order, qd, kd, cd, s0):
        s = s0
        for c in order:
            if s is not None:
                o_ref[chunk(c), :] += jnp.dot(q_ref[chunk(c), :], s.astype(BF16),
                                              preferred_element_type=F32) * qd
            u = lax.dot_general((k_ref[chunk(c), :] * kd).astype(BF16), v_ref[chunk(c), :], tn,
                                preferred_element_type=F32)
            s = u if s is None else s * cd + u
        return s

    s_f = direction(range(nc), qd_f, kd_f, cd_f, s0f_ref[...] if has_state else None)
    s_b = direction(range(nc - 1, -1, -1), qd_b, kd_b, cd_b, s0b_ref[...] if has_state else None)
    if not has_state:
        sf_ref[...] = s_f
        sb_ref[...] = s_b
    o = o_ref[...]
    o = o * lax.rsqrt(jnp.mean(o * o, axis=-1, keepdims=True) + RMS_EPS)
    y_ref[...] = (o * _silu(g_ref[...])).astype(BF16)


def _retention(q, k, v, g, decay_f, decay_b, states):
    b, t, _ = q.shape
    has_state = states is not None
    qk_spec = pl.BlockSpec((None, t, DK_B), lambda i, h: (i, 0, h))
    vg_spec = pl.BlockSpec((None, t, DV_B), lambda i, h: (i, 0, h))
    a_spec = pl.BlockSpec((None, 1, 1), lambda i, h: (h, 0, 0))
    in_specs = [qk_spec, qk_spec, vg_spec, vg_spec, a_spec, a_spec]
    args = [q, k, v, g, decay_f.reshape(H_B, 1, 1), decay_b.reshape(H_B, 1, 1)]
    out_shape = [jax.ShapeDtypeStruct((b, t, E_B), BF16)]
    out_specs = [vg_spec]
    if has_state:
        s_f, s_b, jj = states
        s_spec = pl.BlockSpec((None, None, None, DK_B, DV_B), lambda i, h: (i, jj, h, 0, 0))
        in_specs += [s_spec, s_spec]
        args += [s_f, s_b]
    else:
        so_spec = pl.BlockSpec((None, None, None, DK_B, DV_B), lambda i, h: (i, 0, h, 0, 0))
        out_shape += [jax.ShapeDtypeStruct((b, 1, H_B, DK_B, DV_B), F32)] * 2
        out_specs += [so_spec, so_spec]
    res = pl.pallas_call(
        functools.partial(_retention_kernel, has_state=has_state),
        out_shape=out_shape,
        grid=(b, H_B),
        in_specs=in_specs,
        out_specs=out_specs,
        scratch_shapes=[pltpu.VMEM((t, DV_B), F32)],
        compiler_params=_params(2),
        name="retention_state" if has_state else "retention",
    )(*args)
    return res


def _out_proj_kernel(y_ref, w_ref, x_ref, mod_ref, g_ref, b_ref, o_ref):
    out = jnp.dot(y_ref[...], w_ref[...].astype(BF16), preferred_element_type=F32)
    gate = mod_ref[:, 2 * D_MODEL:3 * D_MODEL]
    o_ref[...] = _residual_layer_norm(x_ref[...], out, gate, g_ref[...], b_ref[...])


def _out_proj(y2d, w_out, j, x2d, mod3, ln_g, ln_b, layer, rows_per_batch):
    m_rows, e = y2d.shape
    tm = ROW_TILE
    const2 = lambda m: (0, 0)
    return pl.pallas_call(
        _out_proj_kernel,
        out_shape=jax.ShapeDtypeStruct((m_rows, D_MODEL), F32),
        grid=(m_rows // tm,),
        in_specs=[pl.BlockSpec((tm, e), lambda m: (m, 0)),
                  _weight_spec(w_out, j),
                  pl.BlockSpec((tm, D_MODEL), lambda m: (m, 0)),
                  pl.BlockSpec((None, 1, 3 * D_MODEL), _mod_row_map(layer, rows_per_batch, tm)),
                  pl.BlockSpec((1, D_MODEL), const2),
                  pl.BlockSpec((1, D_MODEL), const2)],
        out_specs=pl.BlockSpec((tm, D_MODEL), lambda m: (m, 0)),
        compiler_params=_params(1),
        name="out_proj",
    )(y2d, w_out, x2d, mod3, ln_g.reshape(1, D_MODEL), ln_b.reshape(1, D_MODEL))


CONV_ROWS = 1024
CONV_COLS = 256


def _conv_kernel(x_ref, mod_ref, w_in_ref, cw_ref, w_out_ref, g_ref, b_ref, o_ref, acc_ref, *,
                 seq_len):
    h = _modulated(x_ref, mod_ref)
    rows = x_ref.shape[0]
    pos = lax.broadcasted_iota(jnp.int32, (rows, 1), 0) % seq_len
    has_prev = pos > 0
    has_next = pos < seq_len - 1
    e = D_MODEL
    for c in range(e // CONV_COLS):
        cols = slice(c * CONV_COLS, (c + 1) * CONV_COLS)

        def proj(part):
            lo = part * e + c * CONV_COLS
            return jnp.dot(h, w_in_ref[:, lo:lo + CONV_COLS], preferred_element_type=F32)

        p = proj(1) * proj(2)
        prev = jnp.where(has_prev, pltpu.roll(p, 1, 0), 0.0)
        nxt = jnp.where(has_next, pltpu.roll(p, rows - 1, 0), 0.0)
        conv = prev * cw_ref[0:1, cols] + p * cw_ref[1:2, cols] + nxt * cw_ref[2:3, cols]
        y = (proj(0) * conv * _silu(proj(3))).astype(BF16)
        part = jnp.dot(y, w_out_ref[cols, :].astype(BF16), preferred_element_type=F32)
        if c == 0:
            acc_ref[...] = part
        else:
            acc_ref[...] += part
    gate = mod_ref[:, 2 * D_MODEL:3 * D_MODEL]
    o_ref[...] = _residual_layer_norm(x_ref[...], acc_ref[...], gate, g_ref[...], b_ref[...])


def _conv_layer(x2d, mod3, w_in, conv_w, w_out, j, ln_g, ln_b, layer, rows_per_batch, seq_len):
    m_rows = x2d.shape[0]
    tm = CONV_ROWS
    const2 = lambda m: (0, 0)
    return pl.pallas_call(
        functools.partial(_conv_kernel, seq_len=seq_len),
        out_shape=jax.ShapeDtypeStruct((m_rows, D_MODEL), F32),
        grid=(m_rows // tm,),
        in_specs=[pl.BlockSpec((tm, D_MODEL), lambda m: (m, 0)),
                  pl.BlockSpec((None, 1, 3 * D_MODEL), _mod_row_map(layer, rows_per_batch, tm)),
                  _weight_spec(w_in, j),
                  pl.BlockSpec((None, 3, D_MODEL), lambda m: (j, 0, 0)),
                  _weight_spec(w_out, j),
                  pl.BlockSpec((1, D_MODEL), const2),
                  pl.BlockSpec((1, D_MODEL), const2)],
        out_specs=pl.BlockSpec((tm, D_MODEL), lambda m: (m, 0)),
        scratch_shapes=[pltpu.VMEM((tm, D_MODEL), F32)],
        compiler_params=_params(1),
        name="conv_layer",
    )(x2d, mod3, w_in, conv_w, w_out, ln_g.reshape(1, D_MODEL), ln_b.reshape(1, D_MODEL))


def kernel(x_prompt, x_sample, cache_k, cache_v, state_fwd, state_bwd, c, c_ctx, w_mod, b_mod, ln_g,
           ln_b, w_in_a, lam_a, subln_a, w_out_a, w_in_b, decay_fwd, decay_bwd, w_out_b, w_in_c,
           conv_c, w_out_c):
    bp, tp, d = x_prompt.shape
    bs, ts, _ = x_sample.shape
    l_c = cache_k.shape[2]
    cvec = jnp.concatenate([c, c_ctx[None], jnp.zeros((MOD_ROWS - bs - 1, d), F32)], axis=0)
    mod3 = _modulation(cvec, w_mod, b_mod)
    cache_k4 = cache_k.reshape(bs, -1, l_c, d)
    cache_v4 = cache_v.reshape(bs, -1, l_c, d)
    rope_tables = _rope_tables(ts)

    xp = x_prompt.reshape(bp * tp, d)
    xs = x_sample.reshape(bs * ts, d)
    n_attn = (DEPTH + N_MIXERS - 1) // N_MIXERS
    new_cache_k = new_cache_v = jax.ShapeDtypeStruct((bp, n_attn, tp, H_A, DV_A), F32)
    new_sf, new_sb = [], []
    for i in range(DEPTH):
        kind, j = i % N_MIXERS, i // N_MIXERS
        if kind == 0:
            q, k, vt, z, new_cache_k, new_cache_v = _attn_in(
                xp, mod3, w_in_a, j, i, None, None, (new_cache_k, new_cache_v))
            xp = _attention(q, k, vt, z, xp, tp, None, mod3, lam_a[j], subln_a[j], w_out_a, j,
                            ln_g[i], ln_b[i], i, False)
            q, k, vt, z = _attn_in(xs, mod3, w_in_a, j, i, ts, rope_tables, None)
            xs = _attention(q, k, vt, z, xs, ts, (cache_k4, cache_v4, j), mod3, lam_a[j],
                            subln_a[j], w_out_a, j, ln_g[i], ln_b[i], i, True)
        elif kind == 1:
            q, k, v, g = _ret_in(xp, mod3, w_in_b, j, i, None)
            y, s_f, s_b = _retention(q.reshape(bp, tp, -1), k.reshape(bp, tp, -1),
                                     v.reshape(bp, tp, -1), g.reshape(bp, tp, -1),
                                     decay_fwd[j], decay_bwd[j], None)
            new_sf.append(s_f)
            new_sb.append(s_b)
            xp = _out_proj(y.reshape(bp * tp, E_B), w_out_b, j, xp, mod3, ln_g[i], ln_b[i], i, None)
            q, k, v, g = _ret_in(xs, mod3, w_in_b, j, i, ts)
            (y,) = _retention(q.reshape(bs, ts, -1), k.reshape(bs, ts, -1), v.reshape(bs, ts, -1),
                              g.reshape(bs, ts, -1), decay_fwd[j], decay_bwd[j],
                              (state_fwd, state_bwd, j))
            xs = _out_proj(y.reshape(bs * ts, E_B), w_out_b, j, xs, mod3, ln_g[i], ln_b[i], i, ts)
        else:
            xp = _conv_layer(xp, mod3, w_in_c, conv_c, w_out_c, j, ln_g[i], ln_b[i], i, None, tp)
            xs = _conv_layer(xs, mod3, w_in_c, conv_c, w_out_c, j, ln_g[i], ln_b[i], i, ts, ts)
    y_prompt = xp.reshape(bp, tp, d)
    y_sample = xs.reshape(bs, ts, d)
    new_state_fwd = jnp.concatenate(new_sf, axis=1)
    new_state_bwd = jnp.concatenate(new_sb, axis=1)
    return (y_prompt, y_sample, new_cache_k, new_cache_v, new_state_fwd, new_state_bwd)
```

```python
import functools
import math

import jax
import jax.numpy as jnp
from jax import lax
from jax.experimental import pallas as pl
from jax.experimental.pallas import tpu as pltpu

F32 = jnp.float32
BF16 = jnp.bfloat16

D_MODEL = 1024
DEPTH = 4
N_MIXERS = 3
GRID_W = 64
H_A = 8
DH_A = 64
DV_A = 128
SCORE_SCALE = DH_A ** -0.5 * math.log2(math.e)
H_B = 4
DK_B = 256
DV_B = 512
E_B = H_B * DV_B
CHUNK = 256
ALPHA = (2.0 * DEPTH) ** 0.25
ROPE_BASE = 10000.0
LN_EPS = 1e-5
RMS_EPS = 1e-6

MOD_ROWS = 8
CTX_ROW = 4
VMEM_LIMIT_BYTES = 56 * 1024 * 1024
ROW_TILE = 512


def _params(n_axes):
    return pltpu.CompilerParams(dimension_semantics=("arbitrary",) * n_axes,
                                vmem_limit_bytes=VMEM_LIMIT_BYTES)


def _silu(x):
    return x * jax.nn.sigmoid(x)


def _residual_layer_norm(x, out, gate, g, b):
    r = ALPHA * x + gate * out
    mu = jnp.mean(r, axis=-1, keepdims=True)
    d = r - mu
    var = jnp.mean(d * d, axis=-1, keepdims=True)
    return d * lax.rsqrt(var + LN_EPS) * g + b


def _modulated(x_ref, mod_ref):
    shift = mod_ref[:, 0:D_MODEL]
    scale = mod_ref[:, D_MODEL:2 * D_MODEL]
    return x_ref[...] * (1.0 + scale) + shift


def _mod_row_map(layer, rows_per_batch, tile):
    if rows_per_batch is None:
        return lambda m, *_: (layer * MOD_ROWS + CTX_ROW, 0, 0)
    return lambda m, *_: (layer * MOD_ROWS + (m * tile) // rows_per_batch, 0, 0)


def _weight_spec(w, j):
    return pl.BlockSpec((None,) + w.shape[1:], lambda *_: (j, 0, 0), pipeline_mode=pl.Buffered(1))


def _mod_kernel(cv_ref, w_ref, b_ref, o_ref):
    s = _silu(cv_ref[...])
    o_ref[...] = jnp.dot(s, w_ref[...], preferred_element_type=F32) + b_ref[...]


def _modulation(cvec, w_mod, b_mod):
    tn = 1024
    n = 3 * D_MODEL
    out = pl.pallas_call(
        _mod_kernel,
        out_shape=jax.ShapeDtypeStruct((DEPTH, MOD_ROWS, n), F32),
        grid=(DEPTH, n // tn),
        in_specs=[pl.BlockSpec((MOD_ROWS, D_MODEL), lambda i, j: (0, 0)),
                  pl.BlockSpec((None, D_MODEL, tn), lambda i, j: (i, 0, j)),
                  pl.BlockSpec((None, 1, tn), lambda i, j: (i, 0, j))],
        out_specs=pl.BlockSpec((None, MOD_ROWS, tn), lambda i, j: (i, 0, j)),
        compiler_params=_params(2),
        name="modulation",
    )(cvec, w_mod, b_mod.reshape(DEPTH, 1, n))
    return out.reshape(DEPTH * MOD_ROWS, 1, n)


def _rope(xh, cos4, sin4, first_half):
    swapped = jnp.where(first_half, pltpu.roll(xh, 96, 1), pltpu.roll(xh, 32, 1))
    return xh * cos4 + swapped * sin4


def _store_heads(o_ref, x, seq_len):
    for b in range(o_ref.shape[0]):
        o_ref[b] = x[b * seq_len:(b + 1) * seq_len].reshape(seq_len, H_A, DV_A)


def _attn_in_kernel(*refs, rope, keep_f32, seq_len, n_aliased):
    if rope:
        x_ref, mod_ref, w_ref, cos_ref, sin_ref = refs[:5]
        outs = refs[5 + n_aliased:]
    else:
        x_ref, mod_ref, w_ref = refs[:3]
        outs = refs[3 + n_aliased:]
    q_ref, k_ref, vt_ref, z_ref = outs[:4]
    h = _modulated(x_ref, mod_ref)
    if rope:
        lane = lax.broadcasted_iota(jnp.int32, (1, DV_A), 1)
        first_half = (lane % (2 * 32)) < 32
        cos4 = cos_ref[...]
        sin4 = sin_ref[...]
    q_all = jnp.dot(h, w_ref[:, 0:D_MODEL], preferred_element_type=F32)
    k_all = jnp.dot(h, w_ref[:, D_MODEL:2 * D_MODEL], preferred_element_type=F32)
    if keep_f32:
        _store_heads(outs[4], k_all, seq_len)
    for hd in range(H_A):
        cols = slice(hd * DV_A, (hd + 1) * DV_A)
        q = q_all[:, cols]
        k = k_all[:, cols]
        if rope:
            q = _rope(q, cos4, sin4, first_half)
            k = _rope(k, cos4, sin4, first_half)
        q_ref[:, cols] = (q * SCORE_SCALE).astype(BF16)
        k_ref[:, cols] = k.astype(BF16)
    v = jnp.dot(h, w_ref[:, 2 * D_MODEL:3 * D_MODEL], preferred_element_type=F32)
    if keep_f32:
        _store_heads(outs[5], v, seq_len)
    vt_ref[...] = v.T.astype(BF16)
    z_ref[...] = jnp.dot(h, w_ref[:, 3 * D_MODEL:4 * D_MODEL], preferred_element_type=F32)


def _attn_in(x2d, mod3, w_in, j, layer, rows_per_batch, rope_tables, cache_out):
    m_rows = x2d.shape[0]
    tm = ROW_TILE
    rope = rope_tables is not None
    row_spec = pl.BlockSpec((tm, D_MODEL), lambda m: (m, 0))
    in_specs = [row_spec,
                pl.BlockSpec((None, 1, 3 * D_MODEL), _mod_row_map(layer, rows_per_batch, tm)),
                _weight_spec(w_in, j)]
    args = [x2d, mod3, w_in]
    if rope:
        seq_tiles = rows_per_batch // tm
        tab_spec = pl.BlockSpec((tm, DV_A), lambda m: (m % seq_tiles, 0))
        in_specs += [tab_spec, tab_spec]
        args += list(rope_tables)
    out_shape = [jax.ShapeDtypeStruct((m_rows, D_MODEL), BF16),
                 jax.ShapeDtypeStruct((m_rows, D_MODEL), BF16),
                 jax.ShapeDtypeStruct((D_MODEL, m_rows), BF16),
                 jax.ShapeDtypeStruct((m_rows, D_MODEL), F32)]
    out_specs = [row_spec, row_spec, pl.BlockSpec((D_MODEL, tm), lambda m: (0, m)), row_spec]
    aliases = {}
    seq_len = None
    if cache_out is not None:
        new_k, new_v = cache_out
        seq_len = new_k.shape[2]
        cache_shape = jax.ShapeDtypeStruct(new_k.shape, F32)
        cache_spec = pl.BlockSpec((tm // seq_len, None, seq_len, H_A, DV_A),
                                  lambda m: (m, j, 0, 0, 0))
        out_shape += [cache_shape, cache_shape]
        out_specs += [cache_spec, cache_spec]
        if j > 0:
            in_specs += [pl.BlockSpec(memory_space=pl.ANY)] * 2
            args += [new_k, new_v]
            aliases = {len(args) - 2: 4, len(args) - 1: 5}
    return pl.pallas_call(
        functools.partial(_attn_in_kernel, rope=rope, keep_f32=cache_out is not None,
                          seq_len=seq_len, n_aliased=len(aliases)),
        out_shape=out_shape,
        grid=(m_rows // tm,),
        in_specs=in_specs,
        out_specs=out_specs,
        input_output_aliases=aliases,
        compiler_params=_params(1),
        name="attn_in_rope" if rope else "attn_in",
    )(*args)


SCORE_ROWS = 128


def _slab_reduce(op, x):
    parts = [x[i:i + 8] for i in range(0, x.shape[0], 8)]
    while len(parts) > 1:
        parts = [op(parts[i], parts[i + 1]) for i in range(0, len(parts) - 1, 2)] + (
            [parts[-1]] if len(parts) % 2 else [])
    return parts[0]


def _attn_kernel(*refs, layer_idx, has_ctx):
    if has_ctx:
        (q_ref, k_ref, vt_ref, z_ref, x_ref, kc_ref, vc_ref, mod_ref, lam_ref, subln_ref,
         w_ref, g_ref, b_ref, o_ref, y_ref, st_ref, a_ref, kcb_ref, vct_ref) = refs

        @pl.when(pl.program_id(1) == 0)
        def _():
            n_ctx = kc_ref.shape[0]
            kcb_ref[...] = kc_ref[...].reshape(n_ctx, D_MODEL).astype(BF16)
            vct_ref[...] = vc_ref[...].reshape(n_ctx, D_MODEL).T.astype(BF16)
    else:
        (q_ref, k_ref, vt_ref, z_ref, x_ref, mod_ref, lam_ref, subln_ref,
         w_ref, g_ref, b_ref, o_ref, y_ref, st_ref, a_ref) = refs
    tq = q_ref.shape[0]
    lam_init = 0.8 - 0.6 * math.exp(-0.3 * layer_idx)
    lm = lam_ref[...]
    lam = (jnp.exp(jnp.sum(lm[0:1] * lm[1:2], axis=-1, keepdims=True))
           - jnp.exp(jnp.sum(lm[2:3] * lm[3:4], axis=-1, keepdims=True)) + lam_init)
    lane = lax.broadcasted_iota(jnp.int32, (1, DV_A), 1)
    first = lane < DH_A
    subln = jnp.broadcast_to(subln_ref[...], (DV_A, tq))
    nt = (((1,), (1,)), ((), ()))
    t = k_ref.shape[0]
    s_total = st_ref.shape[0]
    for hd in range(H_A):
        cols = slice(hd * DV_A, (hd + 1) * DV_A)
        qh = q_ref[:, cols]
        zero = jnp.zeros_like(qh)
        qq = jnp.concatenate([jnp.where(first, qh, zero), jnp.where(first, zero, qh)], axis=0)
        st = lax.dot_general(k_ref[:, cols], qq, nt, preferred_element_type=F32)
        st_ref[0:t, :] = st
        m8 = _slab_reduce(jnp.maximum, st)
        if has_ctx:
            sc = lax.dot_general(kcb_ref[:, cols], qq, nt, preferred_element_type=F32)
            st_ref[t:s_total, :] = sc
            m8 = jnp.maximum(m8, _slab_reduce(jnp.maximum, sc))
        m = jnp.max(m8, axis=0, keepdims=True)
        l8 = None
        for c in range(s_total // SCORE_ROWS):
            rows = slice(c * SCORE_ROWS, (c + 1) * SCORE_ROWS)
            e = jnp.exp2(st_ref[rows, :] - m)
            st_ref[rows, :] = e
            part = _slab_reduce(jnp.add, e)
            l8 = part if l8 is None else l8 + part
        inv = 1.0 / jnp.sum(l8, axis=0, keepdims=True)
        inv1 = inv[:, :tq]
        inv2 = inv[:, tq:] * lam
        for c in range(s_total // SCORE_ROWS):
            rows = slice(c * SCORE_ROWS, (c + 1) * SCORE_ROWS)
            a_ref[rows, :] = (st_ref[rows, 0:tq] * inv1 - st_ref[rows, tq:2 * tq] * inv2).astype(BF16)
        ot = jnp.dot(vt_ref[cols, :], a_ref[0:t, :], preferred_element_type=F32)
        if has_ctx:
            ot = ot + jnp.dot(vct_ref[cols, :], a_ref[t:s_total, :], preferred_element_type=F32)
        ot = ot * lax.rsqrt(jnp.mean(ot * ot, axis=0, keepdims=True) + RMS_EPS)
        ot = ot * subln * (1.0 - lam_init)
        y_ref[:, cols] = (ot.T * _silu(z_ref[:, cols])).astype(BF16)
    out = jnp.dot(y_ref[...], w_ref[...].astype(BF16), preferred_element_type=F32)
    gate = mod_ref[:, 2 * D_MODEL:3 * D_MODEL]
    o_ref[...] = _residual_layer_norm(x_ref[...], out, gate, g_ref[...], b_ref[...])


def _attention(q, k, vt, z, x, seq_len, ctx, mod3, lam, subln, w_out, j, ln_g, ln_b, layer,
               per_batch_rows):
    m_rows = q.shape[0]
    t = seq_len
    b = m_rows // t
    tq = 256
    nq = t // tq
    has_ctx = ctx is not None
    q_spec = pl.BlockSpec((tq, D_MODEL), lambda i, j: (i * nq + j, 0))
    k_spec = pl.BlockSpec((t, D_MODEL), lambda i, j: (i, 0))
    vt_spec = pl.BlockSpec((D_MODEL, t), lambda i, j: (0, i))
    if per_batch_rows:
        mod_map = lambda i, j: (layer * MOD_ROWS + i, 0, 0)
    else:
        mod_map = lambda i, j: (layer * MOD_ROWS + CTX_ROW, 0, 0)
    const2 = lambda i, j: (0, 0)
    in_specs = [q_spec, k_spec, vt_spec, q_spec, q_spec]
    args = [q, k, vt, z, x]
    s_total = t
    if has_ctx:
        cache_k, cache_v, jj = ctx
        s_total = t + cache_k.shape[2]
        c_spec = pl.BlockSpec((None, None, cache_k.shape[2], H_A, DV_A),
                              lambda i, j: (i, jj, 0, 0, 0))
        in_specs += [c_spec, c_spec]
        args += [cache_k, cache_v]
    in_specs += [pl.BlockSpec((None, 1, 3 * D_MODEL), mod_map),
                 pl.BlockSpec((4, DH_A), const2),
                 pl.BlockSpec((DV_A, 1), const2),
                 _weight_spec(w_out, j),
                 pl.BlockSpec((1, D_MODEL), const2),
                 pl.BlockSpec((1, D_MODEL), const2)]
    args += [mod3, lam, subln.reshape(DV_A, 1), w_out, ln_g.reshape(1, D_MODEL),
             ln_b.reshape(1, D_MODEL)]
    return pl.pallas_call(
        functools.partial(_attn_kernel, layer_idx=layer, has_ctx=has_ctx),
        out_shape=jax.ShapeDtypeStruct((m_rows, D_MODEL), F32),
        grid=(b, nq),
        in_specs=in_specs,
        out_specs=q_spec,
        scratch_shapes=[pltpu.VMEM((tq, D_MODEL), BF16),
                        pltpu.VMEM((s_total, 2 * tq), F32),
                        pltpu.VMEM((s_total, tq), BF16)] + (
            [pltpu.VMEM((s_total - t, D_MODEL), BF16),
             pltpu.VMEM((D_MODEL, s_total - t), BF16)] if has_ctx else []),
        compiler_params=_params(2),
        name="diff_attn_ctx" if has_ctx else "diff_attn",
    )(*args)


def _rope_tables(n_tokens):
    rows = n_tokens // GRID_W
    r = jnp.repeat(jnp.arange(rows, dtype=F32), GRID_W)
    col = jnp.tile(jnp.arange(GRID_W, dtype=F32), rows)
    n_freq = DH_A // 4
    inv = ROPE_BASE ** (-jnp.arange(n_freq, dtype=F32) / n_freq)
    ang = jnp.concatenate([r[:, None] * inv, col[:, None] * inv], -1)
    cos, sin = jnp.cos(ang), jnp.sin(ang)
    return jnp.tile(cos, (1, 4)), jnp.concatenate([-sin, sin, -sin, sin], -1)


def _ret_in_kernel(x_ref, mod_ref, w_ref, q_ref, kt_ref, v_ref, g_ref):
    h = _modulated(x_ref, mod_ref)
    nq = H_B * DK_B
    q_ref[...] = jnp.dot(h, w_ref[:, 0:nq], preferred_element_type=F32).astype(BF16)
    k = jnp.dot(h, w_ref[:, nq:2 * nq], preferred_element_type=F32) * (DK_B ** -0.5)
    kt_ref[...] = k.T
    for c in range(E_B // 1024):
        v = jnp.dot(h, w_ref[:, 2 * nq + c * 1024:2 * nq + (c + 1) * 1024],
                    preferred_element_type=F32)
        v_ref[:, c * 1024:(c + 1) * 1024] = v.astype(BF16)
    for c in range(E_B // 1024):
        g_ref[:, c * 1024:(c + 1) * 1024] = jnp.dot(
            h, w_ref[:, 2 * nq + E_B + c * 1024:2 * nq + E_B + (c + 1) * 1024],
            preferred_element_type=F32)


def _ret_in(x2d, mod3, w_in, j, layer, rows_per_batch):
    m_rows = x2d.shape[0]
    tm = ROW_TILE
    nq = H_B * DK_B
    return pl.pallas_call(
        _ret_in_kernel,
        out_shape=[jax.ShapeDtypeStruct((m_rows, nq), BF16),
                   jax.ShapeDtypeStruct((nq, m_rows), F32),
                   jax.ShapeDtypeStruct((m_rows, E_B), BF16),
                   jax.ShapeDtypeStruct((m_rows, E_B), F32)],
        grid=(m_rows // tm,),
        in_specs=[pl.BlockSpec((tm, D_MODEL), lambda m: (m, 0)),
                  pl.BlockSpec((None, 1, 3 * D_MODEL), _mod_row_map(layer, rows_per_batch, tm)),
                  _weight_spec(w_in, j)],
        out_specs=[pl.BlockSpec((tm, nq), lambda m: (m, 0)),
                   pl.BlockSpec((nq, tm), lambda m: (0, m)),
                   pl.BlockSpec((tm, E_B), lambda m: (m, 0)),
                   pl.BlockSpec((tm, E_B), lambda m: (m, 0))],
        compiler_params=_params(1),
        name="ret_in",
    )(x2d, mod3, w_in)


def _retention_kernel(*refs, has_state, heads):
    if has_state:
        q_ref, kt_ref, v_ref, g_ref, af_ref, ab_ref, s0f_ref, s0b_ref, y_ref = refs
    else:
        q_ref, kt_ref, v_ref, g_ref, af_ref, ab_ref, y_ref, sf_ref, sb_ref = refs
    t = q_ref.shape[0]
    nc = t // CHUNK
    row = lax.broadcasted_iota(jnp.int32, (CHUNK, CHUNK), 0).astype(F32)
    col = lax.broadcasted_iota(jnp.int32, (CHUNK, CHUNK), 1).astype(F32)
    diff = row - col
    idx_col = lax.broadcasted_iota(jnp.int32, (CHUNK, 1), 0).astype(F32)
    idx_row = lax.broadcasted_iota(jnp.int32, (1, CHUNK), 1).astype(F32)

    def chunk(c):
        return slice(c * CHUNK, (c + 1) * CHUNK)

    for hh in range(heads):
        qk_cols = slice(hh * DK_B, (hh + 1) * DK_B)
        v_cols = slice(hh * DV_B, (hh + 1) * DV_B)
        lg_f = jnp.log1p(-jnp.exp(af_ref[hh]))
        lg_b = jnp.log1p(-jnp.exp(ab_ref[hh]))
        dmask = (jnp.where(diff >= 0, jnp.exp(jnp.maximum(diff, 0.0) * lg_f), 0.0)
                 + jnp.where(diff <= 0, jnp.exp(jnp.maximum(-diff, 0.0) * lg_b), 0.0))
        qd_f = jnp.exp((idx_col + 1.0) * lg_f)
        qd_b = jnp.exp((CHUNK - idx_col) * lg_b)
        kd_f = jnp.exp((CHUNK - 1.0 - idx_row) * lg_f)
        kd_b = jnp.exp(idx_row * lg_b)
        cd_f = jnp.exp(CHUNK * lg_f)
        cd_b = jnp.exp(CHUNK * lg_b)

        def states(order, kd, cd, s):
            seen = {}
            for n, c in enumerate(order):
                seen[c] = None if s is None else s.astype(BF16)
                if has_state and n == nc - 1:
                    return seen, None
                u = jnp.dot((kt_ref[qk_cols, chunk(c)] * kd).astype(BF16), v_ref[chunk(c), v_cols],
                            preferred_element_type=F32)
                s = u if s is None else s * cd + u
            return seen, s

        seen_f, s_f = states(range(nc), kd_f, cd_f, s0f_ref[hh] if has_state else None)
        seen_b, s_b = states(range(nc - 1, -1, -1), kd_b, cd_b, s0b_ref[hh] if has_state else None)
        if not has_state:
            sf_ref[hh] = s_f
            sb_ref[hh] = s_b
        for c in range(nc):
            qc = q_ref[chunk(c), qk_cols]
            qk = jnp.dot(qc, kt_ref[qk_cols, chunk(c)].astype(BF16), preferred_element_type=F32)
            o = jnp.dot((qk * dmask).astype(BF16), v_ref[chunk(c), v_cols],
                        preferred_element_type=F32)
            if seen_f[c] is not None:
                o = o + jnp.dot(qc, seen_f[c], preferred_element_type=F32) * qd_f
            if seen_b[c] is not None:
                o = o + jnp.dot(qc, seen_b[c], preferred_element_type=F32) * qd_b
            o = o * lax.rsqrt(jnp.mean(o * o, axis=-1, keepdims=True) + RMS_EPS)
            y_ref[chunk(c), v_cols] = (o * _silu(g_ref[chunk(c), v_cols])).astype(BF16)


def _retention(q, kt, v, g, seq_len, heads, decay_f, decay_b, states):
    t = seq_len
    b = q.shape[0] // t
    has_state = states is not None
    q_spec = pl.BlockSpec((t, heads * DK_B), lambda i, h: (i, h))
    kt_spec = pl.BlockSpec((heads * DK_B, t), lambda i, h: (h, i))
    vg_spec = pl.BlockSpec((t, heads * DV_B), lambda i, h: (i, h))
    a_spec = pl.BlockSpec((heads, 1, 1), lambda i, h: (h, 0, 0))
    in_specs = [q_spec, kt_spec, vg_spec, vg_spec, a_spec, a_spec]
    args = [q, kt, v, g, decay_f.reshape(H_B, 1, 1), decay_b.reshape(H_B, 1, 1)]
    out_shape = [jax.ShapeDtypeStruct((b * t, E_B), BF16)]
    out_specs = [vg_spec]
    if has_state:
        s_f, s_b, jj = states
        s_spec = pl.BlockSpec((None, None, heads, DK_B, DV_B), lambda i, h: (i, jj, h, 0, 0))
        in_specs += [s_spec, s_spec]
        args += [s_f, s_b]
    else:
        so_spec = pl.BlockSpec((None, None, heads, DK_B, DV_B), lambda i, h: (i, 0, h, 0, 0))
        out_shape += [jax.ShapeDtypeStruct((b, 1, H_B, DK_B, DV_B), F32)] * 2
        out_specs += [so_spec, so_spec]
    return pl.pallas_call(
        functools.partial(_retention_kernel, has_state=has_state, heads=heads),
        out_shape=out_shape,
        grid=(b, H_B // heads),
        in_specs=in_specs,
        out_specs=out_specs,
        compiler_params=_params(2),
        name="retention_state" if has_state else "retention",
    )(*args)


def _out_proj_kernel(y_ref, w_ref, x_ref, mod_ref, g_ref, b_ref, o_ref):
    out = jnp.dot(y_ref[...], w_ref[...].astype(BF16), preferred_element_type=F32)
    gate = mod_ref[:, 2 * D_MODEL:3 * D_MODEL]
    o_ref[...] = _residual_layer_norm(x_ref[...], out, gate, g_ref[...], b_ref[...])


def _out_proj(y2d, w_out, j, x2d, mod3, ln_g, ln_b, layer, rows_per_batch):
    m_rows, e = y2d.shape
    tm = ROW_TILE
    const2 = lambda m: (0, 0)
    return pl.pallas_call(
        _out_proj_kernel,
        out_shape=jax.ShapeDtypeStruct((m_rows, D_MODEL), F32),
        grid=(m_rows // tm,),
        in_specs=[pl.BlockSpec((tm, e), lambda m: (m, 0)),
                  _weight_spec(w_out, j),
                  pl.BlockSpec((tm, D_MODEL), lambda m: (m, 0)),
                  pl.BlockSpec((None, 1, 3 * D_MODEL), _mod_row_map(layer, rows_per_batch, tm)),
                  pl.BlockSpec((1, D_MODEL), const2),
                  pl.BlockSpec((1, D_MODEL), const2)],
        out_specs=pl.BlockSpec((tm, D_MODEL), lambda m: (m, 0)),
        compiler_params=_params(1),
        name="out_proj",
    )(y2d, w_out, x2d, mod3, ln_g.reshape(1, D_MODEL), ln_b.reshape(1, D_MODEL))


CONV_ROWS = 1024
CONV_COLS = 256


def _conv_kernel(x_ref, mod_ref, w_in_ref, cw_ref, w_out_ref, g_ref, b_ref, o_ref, acc_ref, *,
                 seq_len):
    h = _modulated(x_ref, mod_ref)
    rows = x_ref.shape[0]
    pos = lax.broadcasted_iota(jnp.int32, (rows, 1), 0) % seq_len
    has_prev = pos > 0
    has_next = pos < seq_len - 1
    e = D_MODEL
    for c in range(e // CONV_COLS):
        cols = slice(c * CONV_COLS, (c + 1) * CONV_COLS)

        def proj(part):
            lo = part * e + c * CONV_COLS
            return jnp.dot(h, w_in_ref[:, lo:lo + CONV_COLS], preferred_element_type=F32)

        p = proj(1) * proj(2)
        prev = jnp.where(has_prev, pltpu.roll(p, 1, 0), 0.0)
        nxt = jnp.where(has_next, pltpu.roll(p, rows - 1, 0), 0.0)
        conv = prev * cw_ref[0:1, cols] + p * cw_ref[1:2, cols] + nxt * cw_ref[2:3, cols]
        y = (proj(0) * conv * _silu(proj(3))).astype(BF16)
        part = jnp.dot(y, w_out_ref[cols, :].astype(BF16), preferred_element_type=F32)
        if c == 0:
            acc_ref[...] = part
        else:
            acc_ref[...] += part
    gate = mod_ref[:, 2 * D_MODEL:3 * D_MODEL]
    o_ref[...] = _residual_layer_norm(x_ref[...], acc_ref[...], gate, g_ref[...], b_ref[...])


def _conv_layer(x2d, mod3, w_in, conv_w, w_out, j, ln_g, ln_b, layer, rows_per_batch, seq_len):
    m_rows = x2d.shape[0]
    tm = CONV_ROWS
    const2 = lambda m: (0, 0)
    return pl.pallas_call(
        functools.partial(_conv_kernel, seq_len=seq_len),
        out_shape=jax.ShapeDtypeStruct((m_rows, D_MODEL), F32),
        grid=(m_rows // tm,),
        in_specs=[pl.BlockSpec((tm, D_MODEL), lambda m: (m, 0)),
                  pl.BlockSpec((None, 1, 3 * D_MODEL), _mod_row_map(layer, rows_per_batch, tm)),
                  _weight_spec(w_in, j),
                  pl.BlockSpec((None, 3, D_MODEL), lambda m: (j, 0, 0)),
                  _weight_spec(w_out, j),
                  pl.BlockSpec((1, D_MODEL), const2),
                  pl.BlockSpec((1, D_MODEL), const2)],
        out_specs=pl.BlockSpec((tm, D_MODEL), lambda m: (m, 0)),
        scratch_shapes=[pltpu.VMEM((tm, D_MODEL), F32)],
        compiler_params=_params(1),
        name="conv_layer",
    )(x2d, mod3, w_in, conv_w, w_out, ln_g.reshape(1, D_MODEL), ln_b.reshape(1, D_MODEL))


def kernel(x_prompt, x_sample, cache_k, cache_v, state_fwd, state_bwd, c, c_ctx, w_mod, b_mod, ln_g,
           ln_b, w_in_a, lam_a, subln_a, w_out_a, w_in_b, decay_fwd, decay_bwd, w_out_b, w_in_c,
           conv_c, w_out_c):
    bp, tp, d = x_prompt.shape
    bs, ts, _ = x_sample.shape
    cvec = jnp.concatenate([c, c_ctx[None], jnp.zeros((MOD_ROWS - bs - 1, d), F32)], axis=0)
    mod3 = _modulation(cvec, w_mod, b_mod)
    rope_tables = _rope_tables(ts)

    xp = x_prompt.reshape(bp * tp, d)
    xs = x_sample.reshape(bs * ts, d)
    n_attn = (DEPTH + N_MIXERS - 1) // N_MIXERS
    new_cache_k = new_cache_v = jax.ShapeDtypeStruct((bp, n_attn, tp, H_A, DV_A), F32)
    new_sf, new_sb = [], []
    for i in range(DEPTH):
        kind, j = i % N_MIXERS, i // N_MIXERS
        if kind == 0:
            q, k, vt, z, new_cache_k, new_cache_v = _attn_in(
                xp, mod3, w_in_a, j, i, None, None, (new_cache_k, new_cache_v))
            xp = _attention(q, k, vt, z, xp, tp, None, mod3, lam_a[j], subln_a[j], w_out_a, j,
                            ln_g[i], ln_b[i], i, False)
            q, k, vt, z = _attn_in(xs, mod3, w_in_a, j, i, ts, rope_tables, None)
            xs = _attention(q, k, vt, z, xs, ts, (cache_k, cache_v, j), mod3, lam_a[j],
                            subln_a[j], w_out_a, j, ln_g[i], ln_b[i], i, True)
        elif kind == 1:
            q, kt, v, g = _ret_in(xp, mod3, w_in_b, j, i, None)
            y, s_f, s_b = _retention(q, kt, v, g, tp, H_B, decay_fwd[j], decay_bwd[j], None)
            new_sf.append(s_f)
            new_sb.append(s_b)
            xp = _out_proj(y, w_out_b, j, xp, mod3, ln_g[i], ln_b[i], i, None)
            q, kt, v, g = _ret_in(xs, mod3, w_in_b, j, i, ts)
            (y,) = _retention(q, kt, v, g, ts, 1, decay_fwd[j], decay_bwd[j],
                              (state_fwd, state_bwd, j))
            xs = _out_proj(y, w_out_b, j, xs, mod3, ln_g[i], ln_b[i], i, ts)
        else:
            xp = _conv_layer(xp, mod3, w_in_c, conv_c, w_out_c, j, ln_g[i], ln_b[i], i, None, tp)
            xs = _conv_layer(xs, mod3, w_in_c, conv_c, w_out_c, j, ln_g[i], ln_b[i], i, ts, ts)
    y_prompt = xp.reshape(bp, tp, d)
    y_sample = xs.reshape(bs, ts, d)
    new_state_fwd = jnp.concatenate(new_sf, axis=1)
    new_state_bwd = jnp.concatenate(new_sb, axis=1)
    return (y_prompt, y_sample, new_cache_k, new_cache_v, new_state_fwd, new_state_bwd)
```

```python
import functools
import math

import jax
import jax.numpy as jnp
from jax import lax
from jax.experimental import pallas as pl
from jax.experimental.pallas import tpu as pltpu

F32 = jnp.float32
BF16 = jnp.bfloat16

D_MODEL = 1024
DEPTH = 4
N_MIXERS = 3
GRID_W = 64
H_A = 8
DH_A = 64
DV_A = 128
SCORE_SCALE = DH_A ** -0.5 * math.log2(math.e)
H_B = 4
DK_B = 256
DV_B = 512
E_B = H_B * DV_B
CHUNK = 256
ALPHA = (2.0 * DEPTH) ** 0.25
ROPE_BASE = 10000.0
LN_EPS = 1e-5
RMS_EPS = 1e-6

MOD_ROWS = 8
CTX_ROW = 4
VMEM_LIMIT_BYTES = 56 * 1024 * 1024
ROW_TILE = 512


def _params(n_axes):
    return pltpu.CompilerParams(dimension_semantics=("arbitrary",) * n_axes,
                                vmem_limit_bytes=VMEM_LIMIT_BYTES)


def _silu(x):
    return x * jax.nn.sigmoid(x)


def _residual_layer_norm(x, out, gate, g, b):
    r = ALPHA * x + gate * out
    mu = jnp.mean(r, axis=-1, keepdims=True)
    d = r - mu
    var = jnp.mean(d * d, axis=-1, keepdims=True)
    return d * lax.rsqrt(var + LN_EPS) * g + b


def _modulated(x_ref, mod_ref):
    shift = mod_ref[:, 0:D_MODEL]
    scale = mod_ref[:, D_MODEL:2 * D_MODEL]
    return x_ref[...] * (1.0 + scale) + shift


def _mod_row_map(layer, rows_per_batch, tile):
    if rows_per_batch is None:
        return lambda m, *_: (layer * MOD_ROWS + CTX_ROW, 0, 0)
    return lambda m, *_: (layer * MOD_ROWS + (m * tile) // rows_per_batch, 0, 0)


def _weight_spec(w, j):
    return pl.BlockSpec((None,) + w.shape[1:], lambda *_: (j, 0, 0), pipeline_mode=pl.Buffered(1))


def _mod_kernel(cv_ref, w_ref, b_ref, o_ref):
    s = _silu(cv_ref[...])
    o_ref[...] = jnp.dot(s, w_ref[...], preferred_element_type=F32) + b_ref[...]


def _modulation(cvec, w_mod, b_mod):
    tn = 1024
    n = 3 * D_MODEL
    out = pl.pallas_call(
        _mod_kernel,
        out_shape=jax.ShapeDtypeStruct((DEPTH, MOD_ROWS, n), F32),
        grid=(DEPTH, n // tn),
        in_specs=[pl.BlockSpec((MOD_ROWS, D_MODEL), lambda i, j: (0, 0)),
                  pl.BlockSpec((None, D_MODEL, tn), lambda i, j: (i, 0, j)),
                  pl.BlockSpec((None, 1, tn), lambda i, j: (i, 0, j))],
        out_specs=pl.BlockSpec((None, MOD_ROWS, tn), lambda i, j: (i, 0, j)),
        compiler_params=_params(2),
        name="modulation",
    )(cvec, w_mod, b_mod.reshape(DEPTH, 1, n))
    return out.reshape(DEPTH * MOD_ROWS, 1, n)


def _rope(xh, cos4, sin4, first_half):
    swapped = jnp.where(first_half, pltpu.roll(xh, 96, 1), pltpu.roll(xh, 32, 1))
    return xh * cos4 + swapped * sin4


def _store_heads(o_ref, x, seq_len):
    for b in range(o_ref.shape[0]):
        xb = x[b * seq_len:(b + 1) * seq_len].reshape(seq_len, H_A, DV_A)
        if len(o_ref.shape) == 4:
            o_ref[b] = xb
        else:
            o_ref[b, 0] = xb
            for s in range(1, o_ref.shape[1]):
                o_ref[b, s] = jnp.zeros_like(xb)


def _attn_in_kernel(*refs, rope, keep_f32, seq_len, n_aliased):
    if rope:
        x_ref, mod_ref, w_ref, cos_ref, sin_ref = refs[:5]
        outs = refs[5 + n_aliased:]
    else:
        x_ref, mod_ref, w_ref = refs[:3]
        outs = refs[3 + n_aliased:]
    q_ref, k_ref, vt_ref, z_ref = outs[:4]
    h = _modulated(x_ref, mod_ref)
    if rope:
        lane = lax.broadcasted_iota(jnp.int32, (1, DV_A), 1)
        first_half = (lane % (2 * 32)) < 32
        cos4 = cos_ref[...]
        sin4 = sin_ref[...]
    q_all = jnp.dot(h, w_ref[:, 0:D_MODEL], preferred_element_type=F32)
    k_all = jnp.dot(h, w_ref[:, D_MODEL:2 * D_MODEL], preferred_element_type=F32)
    if keep_f32:
        _store_heads(outs[4], k_all, seq_len)
    for hd in range(H_A):
        cols = slice(hd * DV_A, (hd + 1) * DV_A)
        q = q_all[:, cols]
        k = k_all[:, cols]
        if rope:
            q = _rope(q, cos4, sin4, first_half)
            k = _rope(k, cos4, sin4, first_half)
        q_ref[:, cols] = (q * SCORE_SCALE).astype(BF16)
        k_ref[:, cols] = k.astype(BF16)
    v = jnp.dot(h, w_ref[:, 2 * D_MODEL:3 * D_MODEL], preferred_element_type=F32)
    if keep_f32:
        _store_heads(outs[5], v, seq_len)
    vt_ref[...] = v.T.astype(BF16)
    z_ref[...] = jnp.dot(h, w_ref[:, 3 * D_MODEL:4 * D_MODEL], preferred_element_type=F32)


def _attn_in(x2d, mod3, w_in, j, layer, rows_per_batch, rope_tables, cache_out):
    m_rows = x2d.shape[0]
    tm = ROW_TILE
    rope = rope_tables is not None
    row_spec = pl.BlockSpec((tm, D_MODEL), lambda m: (m, 0))
    in_specs = [row_spec,
                pl.BlockSpec((None, 1, 3 * D_MODEL), _mod_row_map(layer, rows_per_batch, tm)),
                _weight_spec(w_in, j)]
    args = [x2d, mod3, w_in]
    if rope:
        seq_tiles = rows_per_batch // tm
        tab_spec = pl.BlockSpec((tm, DV_A), lambda m: (m % seq_tiles, 0))
        in_specs += [tab_spec, tab_spec]
        args += list(rope_tables)
    out_shape = [jax.ShapeDtypeStruct((m_rows, D_MODEL), BF16),
                 jax.ShapeDtypeStruct((m_rows, D_MODEL), BF16),
                 jax.ShapeDtypeStruct((D_MODEL, m_rows), BF16),
                 jax.ShapeDtypeStruct((m_rows, D_MODEL), F32)]
    out_specs = [row_spec, row_spec, pl.BlockSpec((D_MODEL, tm), lambda m: (0, m)), row_spec]
    aliases = {}
    seq_len = None
    if cache_out is not None:
        new_k, new_v = cache_out
        seq_len = new_k.shape[2]
        cache_shape = jax.ShapeDtypeStruct(new_k.shape, F32)
        if j == 0:
            cache_spec = pl.BlockSpec((tm // seq_len, new_k.shape[1], seq_len, H_A, DV_A),
                                      lambda m: (m, 0, 0, 0, 0))
        else:
            cache_spec = pl.BlockSpec((tm // seq_len, None, seq_len, H_A, DV_A),
                                      lambda m: (m, j, 0, 0, 0))
        out_shape += [cache_shape, cache_shape]
        out_specs += [cache_spec, cache_spec]
        if j > 0:
            in_specs += [pl.BlockSpec(memory_space=pl.ANY)] * 2
            args += [new_k, new_v]
            aliases = {len(args) - 2: 4, len(args) - 1: 5}
    return pl.pallas_call(
        functools.partial(_attn_in_kernel, rope=rope, keep_f32=cache_out is not None,
                          seq_len=seq_len, n_aliased=len(aliases)),
        out_shape=out_shape,
        grid=(m_rows // tm,),
        in_specs=in_specs,
        out_specs=out_specs,
        input_output_aliases=aliases,
        compiler_params=_params(1),
        name="attn_in_rope" if rope else "attn_in",
    )(*args)


KEY_CHUNK = 256


def _slab_reduce(op, x):
    parts = [x[i:i + 8] for i in range(0, x.shape[0], 8)]
    while len(parts) > 1:
        parts = [op(parts[i], parts[i + 1]) for i in range(0, len(parts) - 1, 2)] + (
            [parts[-1]] if len(parts) % 2 else [])
    return parts[0]


def _attn_kernel(*refs, layer_idx, has_ctx):
    if has_ctx:
        (q_ref, k_ref, vt_ref, z_ref, x_ref, kc_ref, vc_ref, mod_ref, lam_ref, subln_ref,
         w_ref, g_ref, b_ref, o_ref, y_ref, kcb_ref, vct_ref) = refs

        @pl.when(pl.program_id(1) == 0)
        def _():
            n_ctx = kc_ref.shape[0]
            kcb_ref[...] = kc_ref[...].reshape(n_ctx, D_MODEL).astype(BF16)
            vct_ref[...] = vc_ref[...].reshape(n_ctx, D_MODEL).T.astype(BF16)
    else:
        (q_ref, k_ref, vt_ref, z_ref, x_ref, mod_ref, lam_ref, subln_ref,
         w_ref, g_ref, b_ref, o_ref, y_ref) = refs
    tq = q_ref.shape[0]
    lam_init = 0.8 - 0.6 * math.exp(-0.3 * layer_idx)
    lm = lam_ref[...]
    lam = (jnp.exp(jnp.sum(lm[0:1] * lm[1:2], axis=-1, keepdims=True))
           - jnp.exp(jnp.sum(lm[2:3] * lm[3:4], axis=-1, keepdims=True)) + lam_init)
    lane = lax.broadcasted_iota(jnp.int32, (1, DV_A), 1)
    first = lane < DH_A
    subln = jnp.broadcast_to(subln_ref[...], (DV_A, tq))
    nt = (((1,), (1,)), ((), ()))
    t = k_ref.shape[0]

    def scores(hd):
        cols = slice(hd * DV_A, (hd + 1) * DV_A)
        qh = q_ref[:, cols]
        zero = jnp.zeros_like(qh)
        qq = jnp.concatenate([jnp.where(first, qh, zero), jnp.where(first, zero, qh)], axis=0)
        parts = [lax.dot_general(k_ref[:, cols], qq, nt, preferred_element_type=F32)]
        if has_ctx:
            parts.append(lax.dot_general(kcb_ref[:, cols], qq, nt, preferred_element_type=F32))
        return parts

    nxt = scores(0)
    for hd in range(H_A):
        cols = slice(hd * DV_A, (hd + 1) * DV_A)
        cur = nxt
        if hd + 1 < H_A:
            nxt = scores(hd + 1)
        m8 = functools.reduce(jnp.maximum, [_slab_reduce(jnp.maximum, s) for s in cur])
        m = jnp.max(m8, axis=0, keepdims=True)
        es = [jnp.exp2(s - m) for s in cur]
        l8 = functools.reduce(jnp.add, [_slab_reduce(jnp.add, e) for e in es])
        inv = 1.0 / jnp.sum(l8, axis=0, keepdims=True)
        acc = jnp.dot(vt_ref[cols, :], es[0].astype(BF16), preferred_element_type=F32)
        if has_ctx:
            acc = acc + jnp.dot(vct_ref[cols, :], es[1].astype(BF16), preferred_element_type=F32)
        ot = acc[:, :tq] * inv[:, :tq] - acc[:, tq:] * (inv[:, tq:] * lam)
        ot = ot * lax.rsqrt(jnp.mean(ot * ot, axis=0, keepdims=True) + RMS_EPS)
        ot = ot * subln * (1.0 - lam_init)
        y_ref[:, cols] = (ot.T * _silu(z_ref[:, cols])).astype(BF16)
    out = jnp.dot(y_ref[...], w_ref[...].astype(BF16), preferred_element_type=F32)
    gate = mod_ref[:, 2 * D_MODEL:3 * D_MODEL]
    o_ref[...] = _residual_layer_norm(x_ref[...], out, gate, g_ref[...], b_ref[...])


def _attention(q, k, vt, z, x, seq_len, ctx, mod3, lam, subln, w_out, j, ln_g, ln_b, layer,
               per_batch_rows):
    m_rows = q.shape[0]
    t = seq_len
    b = m_rows // t
    tq = 256
    nq = t // tq
    has_ctx = ctx is not None
    q_spec = pl.BlockSpec((tq, D_MODEL), lambda i, j: (i * nq + j, 0))
    k_spec = pl.BlockSpec((t, D_MODEL), lambda i, j: (i, 0))
    vt_spec = pl.BlockSpec((D_MODEL, t), lambda i, j: (0, i))
    if per_batch_rows:
        mod_map = lambda i, j: (layer * MOD_ROWS + i, 0, 0)
    else:
        mod_map = lambda i, j: (layer * MOD_ROWS + CTX_ROW, 0, 0)
    const2 = lambda i, j: (0, 0)
    in_specs = [q_spec, k_spec, vt_spec, q_spec, q_spec]
    args = [q, k, vt, z, x]
    scratch = [pltpu.VMEM((tq, D_MODEL), BF16)]
    if has_ctx:
        cache_k, cache_v, jj = ctx
        n_ctx = cache_k.shape[2]
        scratch += [pltpu.VMEM((n_ctx, D_MODEL), BF16), pltpu.VMEM((D_MODEL, n_ctx), BF16)]
        c_spec = pl.BlockSpec((None, None, n_ctx, H_A, DV_A),
                              lambda i, j: (i, jj, 0, 0, 0))
        in_specs += [c_spec, c_spec]
        args += [cache_k, cache_v]
    in_specs += [pl.BlockSpec((None, 1, 3 * D_MODEL), mod_map),
                 pl.BlockSpec((4, DH_A), const2),
                 pl.BlockSpec((DV_A, 1), const2),
                 _weight_spec(w_out, j),
                 pl.BlockSpec((1, D_MODEL), const2),
                 pl.BlockSpec((1, D_MODEL), const2)]
    args += [mod3, lam, subln.reshape(DV_A, 1), w_out, ln_g.reshape(1, D_MODEL),
             ln_b.reshape(1, D_MODEL)]
    return pl.pallas_call(
        functools.partial(_attn_kernel, layer_idx=layer, has_ctx=has_ctx),
        out_shape=jax.ShapeDtypeStruct((m_rows, D_MODEL), F32),
        grid=(b, nq),
        in_specs=in_specs,
        out_specs=q_spec,
        scratch_shapes=scratch,
        compiler_params=_params(2),
        name="diff_attn_ctx" if has_ctx else "diff_attn",
    )(*args)


def _rope_tables(n_tokens):
    rows = n_tokens // GRID_W
    r = jnp.repeat(jnp.arange(rows, dtype=F32), GRID_W)
    col = jnp.tile(jnp.arange(GRID_W, dtype=F32), rows)
    n_freq = DH_A // 4
    inv = ROPE_BASE ** (-jnp.arange(n_freq, dtype=F32) / n_freq)
    ang = jnp.concatenate([r[:, None] * inv, col[:, None] * inv], -1)
    cos, sin = jnp.cos(ang), jnp.sin(ang)
    return jnp.tile(cos, (1, 4)), jnp.concatenate([-sin, sin, -sin, sin], -1)


def _ret_in_kernel(x_ref, mod_ref, w_ref, q_ref, kt_ref, v_ref, g_ref):
    h = _modulated(x_ref, mod_ref)
    nq = H_B * DK_B
    q_ref[...] = jnp.dot(h, w_ref[:, 0:nq], preferred_element_type=F32).astype(BF16)
    k = jnp.dot(h, w_ref[:, nq:2 * nq], preferred_element_type=F32) * (DK_B ** -0.5)
    kt_ref[...] = k.T
    for c in range(E_B // 1024):
        v = jnp.dot(h, w_ref[:, 2 * nq + c * 1024:2 * nq + (c + 1) * 1024],
                    preferred_element_type=F32)
        v_ref[:, c * 1024:(c + 1) * 1024] = v.astype(BF16)
    for c in range(E_B // 1024):
        g_ref[:, c * 1024:(c + 1) * 1024] = jnp.dot(
            h, w_ref[:, 2 * nq + E_B + c * 1024:2 * nq + E_B + (c + 1) * 1024],
            preferred_element_type=F32)


def _ret_in(x2d, mod3, w_in, j, layer, rows_per_batch):
    m_rows = x2d.shape[0]
    tm = ROW_TILE
    nq = H_B * DK_B
    return pl.pallas_call(
        _ret_in_kernel,
        out_shape=[jax.ShapeDtypeStruct((m_rows, nq), BF16),
                   jax.ShapeDtypeStruct((nq, m_rows), F32),
                   jax.ShapeDtypeStruct((m_rows, E_B), BF16),
                   jax.ShapeDtypeStruct((m_rows, E_B), F32)],
        grid=(m_rows // tm,),
        in_specs=[pl.BlockSpec((tm, D_MODEL), lambda m: (m, 0)),
                  pl.BlockSpec((None, 1, 3 * D_MODEL), _mod_row_map(layer, rows_per_batch, tm)),
                  _weight_spec(w_in, j)],
        out_specs=[pl.BlockSpec((tm, nq), lambda m: (m, 0)),
                   pl.BlockSpec((nq, tm), lambda m: (0, m)),
                   pl.BlockSpec((tm, E_B), lambda m: (m, 0)),
                   pl.BlockSpec((tm, E_B), lambda m: (m, 0))],
        compiler_params=_params(1),
        name="ret_in",
    )(x2d, mod3, w_in)


def _retention_kernel(*refs, has_state, heads):
    if has_state:
        q_ref, kt_ref, v_ref, g_ref, af_ref, ab_ref, s0f_ref, s0b_ref, y_ref = refs
    else:
        q_ref, kt_ref, v_ref, g_ref, af_ref, ab_ref, y_ref, sf_ref, sb_ref = refs
    t = q_ref.shape[0]
    nc = t // CHUNK
    row = lax.broadcasted_iota(jnp.int32, (CHUNK, CHUNK), 0).astype(F32)
    col = lax.broadcasted_iota(jnp.int32, (CHUNK, CHUNK), 1).astype(F32)
    diff = row - col
    idx_col = lax.broadcasted_iota(jnp.int32, (CHUNK, 1), 0).astype(F32)
    idx_row = lax.broadcasted_iota(jnp.int32, (1, CHUNK), 1).astype(F32)

    def chunk(c):
        return slice(c * CHUNK, (c + 1) * CHUNK)

    for hh in range(heads):
        qk_cols = slice(hh * DK_B, (hh + 1) * DK_B)
        v_cols = slice(hh * DV_B, (hh + 1) * DV_B)
        lg_f = jnp.log1p(-jnp.exp(af_ref[hh]))
        lg_b = jnp.log1p(-jnp.exp(ab_ref[hh]))
        dmask = (jnp.where(diff >= 0, jnp.exp(jnp.maximum(diff, 0.0) * lg_f), 0.0)
                 + jnp.where(diff <= 0, jnp.exp(jnp.maximum(-diff, 0.0) * lg_b), 0.0))
        qd_f = jnp.exp((idx_col + 1.0) * lg_f)
        qd_b = jnp.exp((CHUNK - idx_col) * lg_b)
        kd_f = jnp.exp((CHUNK - 1.0 - idx_row) * lg_f)
        kd_b = jnp.exp(idx_row * lg_b)
        cd_f = jnp.exp(CHUNK * lg_f)
        cd_b = jnp.exp(CHUNK * lg_b)

        def states(order, kd, cd, s):
            seen = {}
            for n, c in enumerate(order):
                seen[c] = None if s is None else s.astype(BF16)
                if has_state and n == nc - 1:
                    return seen, None
                u = jnp.dot((kt_ref[qk_cols, chunk(c)] * kd).astype(BF16), v_ref[chunk(c), v_cols],
                            preferred_element_type=F32)
                s = u if s is None else s * cd + u
            return seen, s

        seen_f, s_f = states(range(nc), kd_f, cd_f, s0f_ref[hh] if has_state else None)
        seen_b, s_b = states(range(nc - 1, -1, -1), kd_b, cd_b, s0b_ref[hh] if has_state else None)
        if not has_state:
            sf_ref[hh] = s_f
            sb_ref[hh] = s_b
        for c in range(nc):
            qc = q_ref[chunk(c), qk_cols]
            qk = jnp.dot(qc, kt_ref[qk_cols, chunk(c)].astype(BF16), preferred_element_type=F32)
            o = jnp.dot((qk * dmask).astype(BF16), v_ref[chunk(c), v_cols],
                        preferred_element_type=F32)
            if seen_f[c] is not None:
                o = o + jnp.dot(qc, seen_f[c], preferred_element_type=F32) * qd_f
            if seen_b[c] is not None:
                o = o + jnp.dot(qc, seen_b[c], preferred_element_type=F32) * qd_b
            o = o * lax.rsqrt(jnp.mean(o * o, axis=-1, keepdims=True) + RMS_EPS)
            y_ref[chunk(c), v_cols] = (o * _silu(g_ref[chunk(c), v_cols])).astype(BF16)


def _retention(q, kt, v, g, seq_len, heads, decay_f, decay_b, states):
    t = seq_len
    b = q.shape[0] // t
    has_state = states is not None
    q_spec = pl.BlockSpec((t, heads * DK_B), lambda i, h: (i, h))
    kt_spec = pl.BlockSpec((heads * DK_B, t), lambda i, h: (h, i))
    vg_spec = pl.BlockSpec((t, heads * DV_B), lambda i, h: (i, h))
    a_spec = pl.BlockSpec((heads, 1, 1), lambda i, h: (h, 0, 0))
    in_specs = [q_spec, kt_spec, vg_spec, vg_spec, a_spec, a_spec]
    args = [q, kt, v, g, decay_f.reshape(H_B, 1, 1), decay_b.reshape(H_B, 1, 1)]
    out_shape = [jax.ShapeDtypeStruct((b * t, E_B), BF16)]
    out_specs = [vg_spec]
    if has_state:
        s_f, s_b, jj = states
        s_spec = pl.BlockSpec((None, None, heads, DK_B, DV_B), lambda i, h: (i, jj, h, 0, 0))
        in_specs += [s_spec, s_spec]
        args += [s_f, s_b]
    else:
        so_spec = pl.BlockSpec((None, None, heads, DK_B, DV_B), lambda i, h: (i, 0, h, 0, 0))
        out_shape += [jax.ShapeDtypeStruct((b, 1, H_B, DK_B, DV_B), F32)] * 2
        out_specs += [so_spec, so_spec]
    return pl.pallas_call(
        functools.partial(_retention_kernel, has_state=has_state, heads=heads),
        out_shape=out_shape,
        grid=(b, H_B // heads),
        in_specs=in_specs,
        out_specs=out_specs,
        compiler_params=_params(2),
        name="retention_state" if has_state else "retention",
    )(*args)


def _out_proj_kernel(y_ref, w_ref, x_ref, mod_ref, g_ref, b_ref, o_ref):
    out = jnp.dot(y_ref[...], w_ref[...].astype(BF16), preferred_element_type=F32)
    gate = mod_ref[:, 2 * D_MODEL:3 * D_MODEL]
    o_ref[...] = _residual_layer_norm(x_ref[...], out, gate, g_ref[...], b_ref[...])


def _out_proj(y2d, w_out, j, x2d, mod3, ln_g, ln_b, layer, rows_per_batch):
    m_rows, e = y2d.shape
    tm = ROW_TILE
    const2 = lambda m: (0, 0)
    return pl.pallas_call(
        _out_proj_kernel,
        out_shape=jax.ShapeDtypeStruct((m_rows, D_MODEL), F32),
        grid=(m_rows // tm,),
        in_specs=[pl.BlockSpec((tm, e), lambda m: (m, 0)),
                  _weight_spec(w_out, j),
                  pl.BlockSpec((tm, D_MODEL), lambda m: (m, 0)),
                  pl.BlockSpec((None, 1, 3 * D_MODEL), _mod_row_map(layer, rows_per_batch, tm)),
                  pl.BlockSpec((1, D_MODEL), const2),
                  pl.BlockSpec((1, D_MODEL), const2)],
        out_specs=pl.BlockSpec((tm, D_MODEL), lambda m: (m, 0)),
        compiler_params=_params(1),
        name="out_proj",
    )(y2d, w_out, x2d, mod3, ln_g.reshape(1, D_MODEL), ln_b.reshape(1, D_MODEL))


CONV_ROWS = 1024
CONV_COLS = 256


def _conv_kernel(x_ref, mod_ref, w_in_ref, cw_ref, w_out_ref, g_ref, b_ref, o_ref, acc_ref, *,
                 seq_len):
    h = _modulated(x_ref, mod_ref)
    rows = x_ref.shape[0]
    pos = lax.broadcasted_iota(jnp.int32, (rows, 1), 0) % seq_len
    has_prev = pos > 0
    has_next = pos < seq_len - 1
    e = D_MODEL
    for c in range(e // CONV_COLS):
        cols = slice(c * CONV_COLS, (c + 1) * CONV_COLS)

        def proj(part):
            lo = part * e + c * CONV_COLS
            return jnp.dot(h, w_in_ref[:, lo:lo + CONV_COLS], preferred_element_type=F32)

        p = proj(1) * proj(2)
        prev = jnp.where(has_prev, pltpu.roll(p, 1, 0), 0.0)
        nxt = jnp.where(has_next, pltpu.roll(p, rows - 1, 0), 0.0)
        conv = prev * cw_ref[0:1, cols] + p * cw_ref[1:2, cols] + nxt * cw_ref[2:3, cols]
        y = (proj(0) * conv * _silu(proj(3))).astype(BF16)
        part = jnp.dot(y, w_out_ref[cols, :].astype(BF16), preferred_element_type=F32)
        if c == 0:
            acc_ref[...] = part
        else:
            acc_ref[...] += part
    gate = mod_ref[:, 2 * D_MODEL:3 * D_MODEL]
    o_ref[...] = _residual_layer_norm(x_ref[...], acc_ref[...], gate, g_ref[...], b_ref[...])


def _conv_layer(x2d, mod3, w_in, conv_w, w_out, j, ln_g, ln_b, layer, rows_per_batch, seq_len):
    m_rows = x2d.shape[0]
    tm = CONV_ROWS
    const2 = lambda m: (0, 0)
    return pl.pallas_call(
        functools.partial(_conv_kernel, seq_len=seq_len),
        out_shape=jax.ShapeDtypeStruct((m_rows, D_MODEL), F32),
        grid=(m_rows // tm,),
        in_specs=[pl.BlockSpec((tm, D_MODEL), lambda m: (m, 0)),
                  pl.BlockSpec((None, 1, 3 * D_MODEL), _mod_row_map(layer, rows_per_batch, tm)),
                  _weight_spec(w_in, j),
                  pl.BlockSpec((None, 3, D_MODEL), lambda m: (j, 0, 0)),
                  _weight_spec(w_out, j),
                  pl.BlockSpec((1, D_MODEL), const2),
                  pl.BlockSpec((1, D_MODEL), const2)],
        out_specs=pl.BlockSpec((tm, D_MODEL), lambda m: (m, 0)),
        scratch_shapes=[pltpu.VMEM((tm, D_MODEL), F32)],
        compiler_params=_params(1),
        name="conv_layer",
    )(x2d, mod3, w_in, conv_w, w_out, ln_g.reshape(1, D_MODEL), ln_b.reshape(1, D_MODEL))


def kernel(x_prompt, x_sample, cache_k, cache_v, state_fwd, state_bwd, c, c_ctx, w_mod, b_mod, ln_g,
           ln_b, w_in_a, lam_a, subln_a, w_out_a, w_in_b, decay_fwd, decay_bwd, w_out_b, w_in_c,
           conv_c, w_out_c):
    bp, tp, d = x_prompt.shape
    bs, ts, _ = x_sample.shape
    cvec = jnp.concatenate([c, c_ctx[None], jnp.zeros((MOD_ROWS - bs - 1, d), F32)], axis=0)
    mod3 = _modulation(cvec, w_mod, b_mod)
    rope_tables = _rope_tables(ts)

    xp = x_prompt.reshape(bp * tp, d)
    xs = x_sample.reshape(bs * ts, d)
    n_attn = (DEPTH + N_MIXERS - 1) // N_MIXERS
    new_cache_k = new_cache_v = jax.ShapeDtypeStruct((bp, n_attn, tp, H_A, DV_A), F32)
    new_sf, new_sb = [], []
    for i in range(DEPTH):
        kind, j = i % N_MIXERS, i // N_MIXERS
        if kind == 0:
            q, k, vt, z, new_cache_k, new_cache_v = _attn_in(
                xp, mod3, w_in_a, j, i, None, None, (new_cache_k, new_cache_v))
            xp = _attention(q, k, vt, z, xp, tp, None, mod3, lam_a[j], subln_a[j], w_out_a, j,
                            ln_g[i], ln_b[i], i, False)
            q, k, vt, z = _attn_in(xs, mod3, w_in_a, j, i, ts, rope_tables, None)
            xs = _attention(q, k, vt, z, xs, ts, (cache_k, cache_v, j), mod3, lam_a[j],
                            subln_a[j], w_out_a, j, ln_g[i], ln_b[i], i, True)
        elif kind == 1:
            q, kt, v, g = _ret_in(xp, mod3, w_in_b, j, i, None)
            y, s_f, s_b = _retention(q, kt, v, g, tp, H_B, decay_fwd[j], decay_bwd[j], None)
            new_sf.append(s_f)
            new_sb.append(s_b)
            xp = _out_proj(y, w_out_b, j, xp, mod3, ln_g[i], ln_b[i], i, None)
            q, kt, v, g = _ret_in(xs, mod3, w_in_b, j, i, ts)
            (y,) = _retention(q, kt, v, g, ts, 1, decay_fwd[j], decay_bwd[j],
                              (state_fwd, state_bwd, j))
            xs = _out_proj(y, w_out_b, j, xs, mod3, ln_g[i], ln_b[i], i, ts)
        else:
            xp = _conv_layer(xp, mod3, w_in_c, conv_c, w_out_c, j, ln_g[i], ln_b[i], i, None, tp)
            xs = _conv_layer(xs, mod3, w_in_c, conv_c, w_out_c, j, ln_g[i], ln_b[i], i, ts, ts)
    y_prompt = xp.reshape(bp, tp, d)
    y_sample = xs.reshape(bs, ts, d)
    new_state_fwd = jnp.concatenate(new_sf, axis=1)
    new_state_bwd = jnp.concatenate(new_sb, axis=1)
    return (y_prompt, y_sample, new_cache_k, new_cache_v, new_state_fwd, new_state_bwd)
```

```python
import functools
import math

import jax
import jax.numpy as jnp
from jax import lax
from jax.experimental import pallas as pl
from jax.experimental.pallas import tpu as pltpu

F32 = jnp.float32
BF16 = jnp.bfloat16

D_MODEL = 1024
DEPTH = 4
N_MIXERS = 3
GRID_W = 64
H_A = 8
DH_A = 64
DV_A = 128
SCORE_SCALE = DH_A ** -0.5 * math.log2(math.e)
H_B = 4
DK_B = 256
DV_B = 512
E_B = H_B * DV_B
CHUNK = 256
ALPHA = (2.0 * DEPTH) ** 0.25
ROPE_BASE = 10000.0
LN_EPS = 1e-5
RMS_EPS = 1e-6

MOD_ROWS = 8
CTX_ROW = 4
VMEM_LIMIT_BYTES = 56 * 1024 * 1024
ROW_TILE = 512


def _params(n_axes):
    return pltpu.CompilerParams(dimension_semantics=("arbitrary",) * n_axes,
                                vmem_limit_bytes=VMEM_LIMIT_BYTES)


def _silu(x):
    return x * jax.nn.sigmoid(x)


def _residual_layer_norm(x, out, gate, g, b):
    r = ALPHA * x + gate * out
    mu = jnp.mean(r, axis=-1, keepdims=True)
    d = r - mu
    var = jnp.mean(d * d, axis=-1, keepdims=True)
    return d * lax.rsqrt(var + LN_EPS) * g + b


def _modulated(x_ref, mod_ref):
    shift = mod_ref[:, 0:D_MODEL]
    scale = mod_ref[:, D_MODEL:2 * D_MODEL]
    return x_ref[...] * (1.0 + scale) + shift


def _mod_row_map(layer, rows_per_batch, tile):
    if rows_per_batch is None:
        return lambda m, *_: (layer * MOD_ROWS + CTX_ROW, 0, 0)
    return lambda m, *_: (layer * MOD_ROWS + (m * tile) // rows_per_batch, 0, 0)


def _weight_spec(w, j):
    return pl.BlockSpec((None,) + w.shape[1:], lambda *_: (j, 0, 0), pipeline_mode=pl.Buffered(1))


def _mod_kernel(cv_ref, w_ref, b_ref, o_ref):
    s = _silu(cv_ref[...])
    o_ref[...] = jnp.dot(s, w_ref[...], preferred_element_type=F32) + b_ref[...]


def _modulation(cvec, w_mod, b_mod):
    tn = 1024
    n = 3 * D_MODEL
    out = pl.pallas_call(
        _mod_kernel,
        out_shape=jax.ShapeDtypeStruct((DEPTH, MOD_ROWS, n), F32),
        grid=(DEPTH, n // tn),
        in_specs=[pl.BlockSpec((MOD_ROWS, D_MODEL), lambda i, j: (0, 0)),
                  pl.BlockSpec((None, D_MODEL, tn), lambda i, j: (i, 0, j)),
                  pl.BlockSpec((None, 1, tn), lambda i, j: (i, 0, j))],
        out_specs=pl.BlockSpec((None, MOD_ROWS, tn), lambda i, j: (i, 0, j)),
        compiler_params=_params(2),
        name="modulation",
    )(cvec, w_mod, b_mod.reshape(DEPTH, 1, n))
    return out.reshape(DEPTH * MOD_ROWS, 1, n)


def _rope(xh, cos4, sin4, first_half):
    swapped = jnp.where(first_half, pltpu.roll(xh, 96, 1), pltpu.roll(xh, 32, 1))
    return xh * cos4 + swapped * sin4


def _store_heads(o_ref, x, seq_len):
    for b in range(o_ref.shape[0]):
        xb = x[b * seq_len:(b + 1) * seq_len].reshape(seq_len, H_A, DV_A)
        if len(o_ref.shape) == 4:
            o_ref[b] = xb
        else:
            o_ref[b, 0] = xb
            for s in range(1, o_ref.shape[1]):
                o_ref[b, s] = jnp.zeros_like(xb)


def _attn_in_kernel(*refs, rope, keep_f32, seq_len, n_aliased):
    if rope:
        x_ref, mod_ref, w_ref, cos_ref, sin_ref = refs[:5]
        outs = refs[5 + n_aliased:]
    else:
        x_ref, mod_ref, w_ref = refs[:3]
        outs = refs[3 + n_aliased:]
    q_ref, k_ref, vt_ref, z_ref = outs[:4]
    h = _modulated(x_ref, mod_ref)
    if rope:
        lane = lax.broadcasted_iota(jnp.int32, (1, DV_A), 1)
        first_half = (lane % (2 * 32)) < 32
        cos4 = cos_ref[...]
        sin4 = sin_ref[...]
    q_all = jnp.dot(h, w_ref[:, 0:D_MODEL], preferred_element_type=F32)
    k_all = jnp.dot(h, w_ref[:, D_MODEL:2 * D_MODEL], preferred_element_type=F32)
    if keep_f32:
        _store_heads(outs[4], k_all, seq_len)
    for hd in range(H_A):
        cols = slice(hd * DV_A, (hd + 1) * DV_A)
        q = q_all[:, cols]
        k = k_all[:, cols]
        if rope:
            q = _rope(q, cos4, sin4, first_half)
            k = _rope(k, cos4, sin4, first_half)
        q_ref[:, cols] = (q * SCORE_SCALE).astype(BF16)
        k_ref[:, cols] = k.astype(BF16)
    v = jnp.dot(h, w_ref[:, 2 * D_MODEL:3 * D_MODEL], preferred_element_type=F32)
    if keep_f32:
        _store_heads(outs[5], v, seq_len)
    vt_ref[...] = v.T.astype(BF16)
    z_ref[...] = jnp.dot(h, w_ref[:, 3 * D_MODEL:4 * D_MODEL], preferred_element_type=F32)


def _attn_in(x2d, mod3, w_in, j, layer, rows_per_batch, rope_tables, cache_out):
    m_rows = x2d.shape[0]
    tm = ROW_TILE
    rope = rope_tables is not None
    row_spec = pl.BlockSpec((tm, D_MODEL), lambda m: (m, 0))
    in_specs = [row_spec,
                pl.BlockSpec((None, 1, 3 * D_MODEL), _mod_row_map(layer, rows_per_batch, tm)),
                _weight_spec(w_in, j)]
    args = [x2d, mod3, w_in]
    if rope:
        seq_tiles = rows_per_batch // tm
        tab_spec = pl.BlockSpec((tm, DV_A), lambda m: (m % seq_tiles, 0))
        in_specs += [tab_spec, tab_spec]
        args += list(rope_tables)
    out_shape = [jax.ShapeDtypeStruct((m_rows, D_MODEL), BF16),
                 jax.ShapeDtypeStruct((m_rows, D_MODEL), BF16),
                 jax.ShapeDtypeStruct((D_MODEL, m_rows), BF16),
                 jax.ShapeDtypeStruct((m_rows, D_MODEL), F32)]
    out_specs = [row_spec, row_spec, pl.BlockSpec((D_MODEL, tm), lambda m: (0, m)), row_spec]
    aliases = {}
    seq_len = None
    if cache_out is not None:
        new_k, new_v = cache_out
        seq_len = new_k.shape[2]
        cache_shape = jax.ShapeDtypeStruct(new_k.shape, F32)
        if j == 0:
            cache_spec = pl.BlockSpec((tm // seq_len, new_k.shape[1], seq_len, H_A, DV_A),
                                      lambda m: (m, 0, 0, 0, 0))
        else:
            cache_spec = pl.BlockSpec((tm // seq_len, None, seq_len, H_A, DV_A),
                                      lambda m: (m, j, 0, 0, 0))
        out_shape += [cache_shape, cache_shape]
        out_specs += [cache_spec, cache_spec]
        if j > 0:
            in_specs += [pl.BlockSpec(memory_space=pl.ANY)] * 2
            args += [new_k, new_v]
            aliases = {len(args) - 2: 4, len(args) - 1: 5}
    return pl.pallas_call(
        functools.partial(_attn_in_kernel, rope=rope, keep_f32=cache_out is not None,
                          seq_len=seq_len, n_aliased=len(aliases)),
        out_shape=out_shape,
        grid=(m_rows // tm,),
        in_specs=in_specs,
        out_specs=out_specs,
        input_output_aliases=aliases,
        compiler_params=_params(1),
        name="attn_in_rope" if rope else "attn_in",
    )(*args)


ONES_ROWS = 16


def _slab_reduce(op, x):
    parts = [x[i:i + 8] for i in range(0, x.shape[0], 8)]
    while len(parts) > 1:
        parts = [op(parts[i], parts[i + 1]) for i in range(0, len(parts) - 1, 2)] + (
            [parts[-1]] if len(parts) % 2 else [])
    return parts[0]


def _attn_kernel(*refs, layer_idx, has_ctx):
    if has_ctx:
        (q_ref, k_ref, vt_ref, z_ref, x_ref, kc_ref, vc_ref, mod_ref, lam_ref, subln_ref,
         w_ref, g_ref, b_ref, o_ref, y_ref, kcb_ref, vct_ref) = refs

        @pl.when(pl.program_id(1) == 0)
        def _():
            n_ctx = kc_ref.shape[0]
            kcb_ref[...] = kc_ref[...].reshape(n_ctx, D_MODEL).astype(BF16)
            vct_ref[...] = vc_ref[...].reshape(n_ctx, D_MODEL).T.astype(BF16)
    else:
        (q_ref, k_ref, vt_ref, z_ref, x_ref, mod_ref, lam_ref, subln_ref,
         w_ref, g_ref, b_ref, o_ref, y_ref) = refs
    tq = q_ref.shape[0]
    lam_init = 0.8 - 0.6 * math.exp(-0.3 * layer_idx)
    lm = lam_ref[...]
    lam = (jnp.exp(jnp.sum(lm[0:1] * lm[1:2], axis=-1, keepdims=True))
           - jnp.exp(jnp.sum(lm[2:3] * lm[3:4], axis=-1, keepdims=True)) + lam_init)
    lane = lax.broadcasted_iota(jnp.int32, (1, DV_A), 1)
    first = lane < DH_A
    subln = jnp.broadcast_to(subln_ref[...], (DV_A, tq))
    nt = (((1,), (1,)), ((), ()))
    t = k_ref.shape[0]

    def scores(hd):
        cols = slice(hd * DV_A, (hd + 1) * DV_A)
        qh = q_ref[:, cols]
        zero = jnp.zeros_like(qh)
        qq = jnp.concatenate([jnp.where(first, qh, zero), jnp.where(first, zero, qh)], axis=0)
        parts = [lax.dot_general(k_ref[:, cols], qq, nt, preferred_element_type=F32)]
        if has_ctx:
            parts.append(lax.dot_general(kcb_ref[:, cols], qq, nt, preferred_element_type=F32))
        return parts

    def exps(parts):
        m8 = functools.reduce(jnp.maximum, [_slab_reduce(jnp.maximum, s) for s in parts])
        m = jnp.max(m8, axis=0, keepdims=True)
        return [jnp.exp2(s - m).astype(BF16) for s in parts]

    def with_ones(vals_t):
        return jnp.concatenate([vals_t, jnp.ones((ONES_ROWS, vals_t.shape[1]), BF16)], axis=0)

    def finish(hd, es):
        cols = slice(hd * DV_A, (hd + 1) * DV_A)
        acc = jnp.dot(with_ones(vt_ref[cols, :]), es[0], preferred_element_type=F32)
        if has_ctx:
            acc = acc + jnp.dot(with_ones(vct_ref[cols, :]), es[1], preferred_element_type=F32)
        inv = 1.0 / acc[DV_A:DV_A + 1, :]
        ot = acc[:DV_A, :tq] * inv[:, :tq] - acc[:DV_A, tq:] * (inv[:, tq:] * lam)
        ot = ot * lax.rsqrt(jnp.mean(ot * ot, axis=0, keepdims=True) + RMS_EPS)
        ot = ot * subln * (1.0 - lam_init)
        y_ref[:, cols] = (ot.T * _silu(z_ref[:, cols])).astype(BF16)

    s_ahead = {h: scores(h) for h in range(min(2, H_A))}
    e_ahead = {0: exps(s_ahead.pop(0))}
    for hd in range(H_A):
        if hd + 2 < H_A:
            s_ahead[hd + 2] = scores(hd + 2)
        if hd + 1 < H_A:
            e_ahead[hd + 1] = exps(s_ahead.pop(hd + 1))
        finish(hd, e_ahead.pop(hd))
    out = jnp.dot(y_ref[...], w_ref[...].astype(BF16), preferred_element_type=F32)
    gate = mod_ref[:, 2 * D_MODEL:3 * D_MODEL]
    o_ref[...] = _residual_layer_norm(x_ref[...], out, gate, g_ref[...], b_ref[...])


def _attention(q, k, vt, z, x, seq_len, ctx, mod3, lam, subln, w_out, j, ln_g, ln_b, layer,
               per_batch_rows):
    m_rows = q.shape[0]
    t = seq_len
    b = m_rows // t
    tq = 256
    nq = t // tq
    has_ctx = ctx is not None
    q_spec = pl.BlockSpec((tq, D_MODEL), lambda i, j: (i * nq + j, 0))
    k_spec = pl.BlockSpec((t, D_MODEL), lambda i, j: (i, 0))
    vt_spec = pl.BlockSpec((D_MODEL, t), lambda i, j: (0, i))
    if per_batch_rows:
        mod_map = lambda i, j: (layer * MOD_ROWS + i, 0, 0)
    else:
        mod_map = lambda i, j: (layer * MOD_ROWS + CTX_ROW, 0, 0)
    const2 = lambda i, j: (0, 0)
    in_specs = [q_spec, k_spec, vt_spec, q_spec, q_spec]
    args = [q, k, vt, z, x]
    scratch = [pltpu.VMEM((tq, D_MODEL), BF16)]
    if has_ctx:
        cache_k, cache_v, jj = ctx
        n_ctx = cache_k.shape[2]
        scratch += [pltpu.VMEM((n_ctx, D_MODEL), BF16), pltpu.VMEM((D_MODEL, n_ctx), BF16)]
        c_spec = pl.BlockSpec((None, None, n_ctx, H_A, DV_A),
                              lambda i, j: (i, jj, 0, 0, 0))
        in_specs += [c_spec, c_spec]
        args += [cache_k, cache_v]
    in_specs += [pl.BlockSpec((None, 1, 3 * D_MODEL), mod_map),
                 pl.BlockSpec((4, DH_A), const2),
                 pl.BlockSpec((DV_A, 1), const2),
                 _weight_spec(w_out, j),
                 pl.BlockSpec((1, D_MODEL), const2),
                 pl.BlockSpec((1, D_MODEL), const2)]
    args += [mod3, lam, subln.reshape(DV_A, 1), w_out, ln_g.reshape(1, D_MODEL),
             ln_b.reshape(1, D_MODEL)]
    return pl.pallas_call(
        functools.partial(_attn_kernel, layer_idx=layer, has_ctx=has_ctx),
        out_shape=jax.ShapeDtypeStruct((m_rows, D_MODEL), F32),
        grid=(b, nq),
        in_specs=in_specs,
        out_specs=q_spec,
        scratch_shapes=scratch,
        compiler_params=_params(2),
        name="diff_attn_ctx" if has_ctx else "diff_attn",
    )(*args)


def _rope_tables(n_tokens):
    rows = n_tokens // GRID_W
    r = jnp.repeat(jnp.arange(rows, dtype=F32), GRID_W)
    col = jnp.tile(jnp.arange(GRID_W, dtype=F32), rows)
    n_freq = DH_A // 4
    inv = ROPE_BASE ** (-jnp.arange(n_freq, dtype=F32) / n_freq)
    ang = jnp.concatenate([r[:, None] * inv, col[:, None] * inv], -1)
    cos, sin = jnp.cos(ang), jnp.sin(ang)
    return jnp.tile(cos, (1, 4)), jnp.concatenate([-sin, sin, -sin, sin], -1)


def _ret_in_kernel(x_ref, mod_ref, w_ref, q_ref, kt_ref, v_ref, g_ref):
    h = _modulated(x_ref, mod_ref)
    nq = H_B * DK_B
    q_ref[...] = jnp.dot(h, w_ref[:, 0:nq], preferred_element_type=F32).astype(BF16)
    k = jnp.dot(h, w_ref[:, nq:2 * nq], preferred_element_type=F32) * (DK_B ** -0.5)
    kt_ref[...] = k.T
    for c in range(E_B // 1024):
        v = jnp.dot(h, w_ref[:, 2 * nq + c * 1024:2 * nq + (c + 1) * 1024],
                    preferred_element_type=F32)
        v_ref[:, c * 1024:(c + 1) * 1024] = v.astype(BF16)
    for c in range(E_B // 1024):
        g_ref[:, c * 1024:(c + 1) * 1024] = jnp.dot(
            h, w_ref[:, 2 * nq + E_B + c * 1024:2 * nq + E_B + (c + 1) * 1024],
            preferred_element_type=F32)


def _ret_in(x2d, mod3, w_in, j, layer, rows_per_batch):
    m_rows = x2d.shape[0]
    tm = ROW_TILE
    nq = H_B * DK_B
    return pl.pallas_call(
        _ret_in_kernel,
        out_shape=[jax.ShapeDtypeStruct((m_rows, nq), BF16),
                   jax.ShapeDtypeStruct((nq, m_rows), F32),
                   jax.ShapeDtypeStruct((m_rows, E_B), BF16),
                   jax.ShapeDtypeStruct((m_rows, E_B), F32)],
        grid=(m_rows // tm,),
        in_specs=[pl.BlockSpec((tm, D_MODEL), lambda m: (m, 0)),
                  pl.BlockSpec((None, 1, 3 * D_MODEL), _mod_row_map(layer, rows_per_batch, tm)),
                  _weight_spec(w_in, j)],
        out_specs=[pl.BlockSpec((tm, nq), lambda m: (m, 0)),
                   pl.BlockSpec((nq, tm), lambda m: (0, m)),
                   pl.BlockSpec((tm, E_B), lambda m: (m, 0)),
                   pl.BlockSpec((tm, E_B), lambda m: (m, 0))],
        compiler_params=_params(1),
        name="ret_in",
    )(x2d, mod3, w_in)


def _retention_kernel(*refs, has_state, heads, fused_out):
    refs = list(refs)
    q_ref, kt_ref, v_ref, g_ref, af_ref, ab_ref = refs[:6]
    del refs[:6]
    if has_state:
        s0f_ref, s0b_ref = refs[:2]
        del refs[:2]
    if fused_out:
        x_ref, mod_ref, w_ref, lng_ref, lnb_ref = refs[:5]
        del refs[:5]
        o_ref = refs.pop(0)
        y_ref = refs.pop()
    else:
        y_ref = refs.pop(0)
    if not has_state:
        sf_ref, sb_ref = refs
    t = q_ref.shape[0]
    nc = t // CHUNK
    row = lax.broadcasted_iota(jnp.int32, (CHUNK, CHUNK), 0).astype(F32)
    col = lax.broadcasted_iota(jnp.int32, (CHUNK, CHUNK), 1).astype(F32)
    diff = row - col
    idx_col = lax.broadcasted_iota(jnp.int32, (CHUNK, 1), 0).astype(F32)
    idx_row = lax.broadcasted_iota(jnp.int32, (1, CHUNK), 1).astype(F32)

    def chunk(c):
        return slice(c * CHUNK, (c + 1) * CHUNK)

    for hh in range(heads):
        qk_cols = slice(hh * DK_B, (hh + 1) * DK_B)
        v_cols = slice(hh * DV_B, (hh + 1) * DV_B)
        lg_f = jnp.log1p(-jnp.exp(af_ref[hh]))
        lg_b = jnp.log1p(-jnp.exp(ab_ref[hh]))
        dmask = (jnp.where(diff >= 0, jnp.exp(jnp.maximum(diff, 0.0) * lg_f), 0.0)
                 + jnp.where(diff <= 0, jnp.exp(jnp.maximum(-diff, 0.0) * lg_b), 0.0))
        qd_f = jnp.exp((idx_col + 1.0) * lg_f)
        qd_b = jnp.exp((CHUNK - idx_col) * lg_b)
        kd_f = jnp.exp((CHUNK - 1.0 - idx_row) * lg_f)
        kd_b = jnp.exp(idx_row * lg_b)
        cd_f = jnp.exp(CHUNK * lg_f)
        cd_b = jnp.exp(CHUNK * lg_b)

        def states(order, kd, cd, s):
            seen = {}
            for n, c in enumerate(order):
                seen[c] = None if s is None else s.astype(BF16)
                if has_state and n == nc - 1:
                    return seen, None
                u = jnp.dot((kt_ref[qk_cols, chunk(c)] * kd).astype(BF16), v_ref[chunk(c), v_cols],
                            preferred_element_type=F32)
                s = u if s is None else s * cd + u
            return seen, s

        seen_f, s_f = states(range(nc), kd_f, cd_f, s0f_ref[hh] if has_state else None)
        seen_b, s_b = states(range(nc - 1, -1, -1), kd_b, cd_b, s0b_ref[hh] if has_state else None)
        if not has_state:
            sf_ref[hh] = s_f
            sb_ref[hh] = s_b
        for c in range(nc):
            qc = q_ref[chunk(c), qk_cols]
            qk = jnp.dot(qc, kt_ref[qk_cols, chunk(c)].astype(BF16), preferred_element_type=F32)
            o = jnp.dot((qk * dmask).astype(BF16), v_ref[chunk(c), v_cols],
                        preferred_element_type=F32)
            if seen_f[c] is not None:
                o = o + jnp.dot(qc, seen_f[c], preferred_element_type=F32) * qd_f
            if seen_b[c] is not None:
                o = o + jnp.dot(qc, seen_b[c], preferred_element_type=F32) * qd_b
            o = o * lax.rsqrt(jnp.mean(o * o, axis=-1, keepdims=True) + RMS_EPS)
            y_ref[chunk(c), v_cols] = (o * _silu(g_ref[chunk(c), v_cols])).astype(BF16)
    if fused_out:
        out = jnp.dot(y_ref[...], w_ref[...].astype(BF16), preferred_element_type=F32)
        gate = mod_ref[:, 2 * D_MODEL:3 * D_MODEL]
        o_ref[...] = _residual_layer_norm(x_ref[...], out, gate, lng_ref[...], lnb_ref[...])


def _retention(q, kt, v, g, seq_len, heads, decay_f, decay_b, states, out_proj=None):
    t = seq_len
    b = q.shape[0] // t
    has_state = states is not None
    fused_out = out_proj is not None
    assert not fused_out or heads == H_B
    q_spec = pl.BlockSpec((t, heads * DK_B), lambda i, h: (i, h))
    kt_spec = pl.BlockSpec((heads * DK_B, t), lambda i, h: (h, i))
    vg_spec = pl.BlockSpec((t, heads * DV_B), lambda i, h: (i, h))
    a_spec = pl.BlockSpec((heads, 1, 1), lambda i, h: (h, 0, 0))
    in_specs = [q_spec, kt_spec, vg_spec, vg_spec, a_spec, a_spec]
    args = [q, kt, v, g, decay_f.reshape(H_B, 1, 1), decay_b.reshape(H_B, 1, 1)]
    out_shape = [jax.ShapeDtypeStruct((b * t, E_B), BF16)]
    out_specs = [vg_spec]
    if has_state:
        s_f, s_b, jj = states
        s_spec = pl.BlockSpec((None, None, heads, DK_B, DV_B), lambda i, h: (i, jj, h, 0, 0))
        in_specs += [s_spec, s_spec]
        args += [s_f, s_b]
    scratch = []
    if fused_out:
        x2d, mod3, mod_map, w_out, jw, ln_g, ln_b = out_proj
        x_spec = pl.BlockSpec((t, D_MODEL), lambda i, h: (i, 0))
        const2 = lambda i, h: (0, 0)
        in_specs += [x_spec, pl.BlockSpec((None, 1, 3 * D_MODEL), mod_map), _weight_spec(w_out, jw),
                     pl.BlockSpec((1, D_MODEL), const2), pl.BlockSpec((1, D_MODEL), const2)]
        args += [x2d, mod3, w_out, ln_g.reshape(1, D_MODEL), ln_b.reshape(1, D_MODEL)]
        out_shape = [jax.ShapeDtypeStruct((b * t, D_MODEL), F32)]
        out_specs = [x_spec]
        scratch = [pltpu.VMEM((t, E_B), BF16)]
    if not has_state:
        so_spec = pl.BlockSpec((None, None, heads, DK_B, DV_B), lambda i, h: (i, 0, h, 0, 0))
        out_shape += [jax.ShapeDtypeStruct((b, 1, H_B, DK_B, DV_B), F32)] * 2
        out_specs += [so_spec, so_spec]
    return pl.pallas_call(
        functools.partial(_retention_kernel, has_state=has_state, heads=heads,
                          fused_out=fused_out),
        out_shape=out_shape,
        grid=(b, H_B // heads),
        in_specs=in_specs,
        out_specs=out_specs,
        scratch_shapes=scratch,
        compiler_params=_params(2),
        name="retention_state" if has_state else "retention",
    )(*args)


def _out_proj_kernel(y_ref, w_ref, x_ref, mod_ref, g_ref, b_ref, o_ref):
    out = jnp.dot(y_ref[...], w_ref[...].astype(BF16), preferred_element_type=F32)
    gate = mod_ref[:, 2 * D_MODEL:3 * D_MODEL]
    o_ref[...] = _residual_layer_norm(x_ref[...], out, gate, g_ref[...], b_ref[...])


def _out_proj(y2d, w_out, j, x2d, mod3, ln_g, ln_b, layer, rows_per_batch):
    m_rows, e = y2d.shape
    tm = ROW_TILE
    const2 = lambda m: (0, 0)
    return pl.pallas_call(
        _out_proj_kernel,
        out_shape=jax.ShapeDtypeStruct((m_rows, D_MODEL), F32),
        grid=(m_rows // tm,),
        in_specs=[pl.BlockSpec((tm, e), lambda m: (m, 0)),
                  _weight_spec(w_out, j),
                  pl.BlockSpec((tm, D_MODEL), lambda m: (m, 0)),
                  pl.BlockSpec((None, 1, 3 * D_MODEL), _mod_row_map(layer, rows_per_batch, tm)),
                  pl.BlockSpec((1, D_MODEL), const2),
                  pl.BlockSpec((1, D_MODEL), const2)],
        out_specs=pl.BlockSpec((tm, D_MODEL), lambda m: (m, 0)),
        compiler_params=_params(1),
        name="out_proj",
    )(y2d, w_out, x2d, mod3, ln_g.reshape(1, D_MODEL), ln_b.reshape(1, D_MODEL))


CONV_ROWS = 1024
CONV_COLS = 256


def _conv_kernel(x_ref, mod_ref, w_in_ref, cw_ref, w_out_ref, g_ref, b_ref, o_ref, acc_ref, *,
                 seq_len):
    h = _modulated(x_ref, mod_ref)
    rows = x_ref.shape[0]
    pos = lax.broadcasted_iota(jnp.int32, (rows, 1), 0) % seq_len
    has_prev = pos > 0
    has_next = pos < seq_len - 1
    e = D_MODEL
    for c in range(e // CONV_COLS):
        cols = slice(c * CONV_COLS, (c + 1) * CONV_COLS)

        def proj(part):
            lo = part * e + c * CONV_COLS
            return jnp.dot(h, w_in_ref[:, lo:lo + CONV_COLS], preferred_element_type=F32)

        p = proj(1) * proj(2)
        prev = jnp.where(has_prev, pltpu.roll(p, 1, 0), 0.0)
        nxt = jnp.where(has_next, pltpu.roll(p, rows - 1, 0), 0.0)
        conv = prev * cw_ref[0:1, cols] + p * cw_ref[1:2, cols] + nxt * cw_ref[2:3, cols]
        y = (proj(0) * conv * _silu(proj(3))).astype(BF16)
        part = jnp.dot(y, w_out_ref[cols, :].astype(BF16), preferred_element_type=F32)
        if c == 0:
            acc_ref[...] = part
        else:
            acc_ref[...] += part
    gate = mod_ref[:, 2 * D_MODEL:3 * D_MODEL]
    o_ref[...] = _residual_layer_norm(x_ref[...], acc_ref[...], gate, g_ref[...], b_ref[...])


def _conv_layer(x2d, mod3, w_in, conv_w, w_out, j, ln_g, ln_b, layer, rows_per_batch, seq_len):
    m_rows = x2d.shape[0]
    tm = CONV_ROWS
    const2 = lambda m: (0, 0)
    return pl.pallas_call(
        functools.partial(_conv_kernel, seq_len=seq_len),
        out_shape=jax.ShapeDtypeStruct((m_rows, D_MODEL), F32),
        grid=(m_rows // tm,),
        in_specs=[pl.BlockSpec((tm, D_MODEL), lambda m: (m, 0)),
                  pl.BlockSpec((None, 1, 3 * D_MODEL), _mod_row_map(layer, rows_per_batch, tm)),
                  _weight_spec(w_in, j),
                  pl.BlockSpec((None, 3, D_MODEL), lambda m: (j, 0, 0)),
                  _weight_spec(w_out, j),
                  pl.BlockSpec((1, D_MODEL), const2),
                  pl.BlockSpec((1, D_MODEL), const2)],
        out_specs=pl.BlockSpec((tm, D_MODEL), lambda m: (m, 0)),
        scratch_shapes=[pltpu.VMEM((tm, D_MODEL), F32)],
        compiler_params=_params(1),
        name="conv_layer",
    )(x2d, mod3, w_in, conv_w, w_out, ln_g.reshape(1, D_MODEL), ln_b.reshape(1, D_MODEL))


def kernel(x_prompt, x_sample, cache_k, cache_v, state_fwd, state_bwd, c, c_ctx, w_mod, b_mod, ln_g,
           ln_b, w_in_a, lam_a, subln_a, w_out_a, w_in_b, decay_fwd, decay_bwd, w_out_b, w_in_c,
           conv_c, w_out_c):
    bp, tp, d = x_prompt.shape
    bs, ts, _ = x_sample.shape
    cvec = jnp.concatenate([c, c_ctx[None], jnp.zeros((MOD_ROWS - bs - 1, d), F32)], axis=0)
    mod3 = _modulation(cvec, w_mod, b_mod)
    rope_tables = _rope_tables(ts)

    xp = x_prompt.reshape(bp * tp, d)
    xs = x_sample.reshape(bs * ts, d)
    n_attn = (DEPTH + N_MIXERS - 1) // N_MIXERS
    new_cache_k = new_cache_v = jax.ShapeDtypeStruct((bp, n_attn, tp, H_A, DV_A), F32)
    new_sf, new_sb = [], []
    for i in range(DEPTH):
        kind, j = i % N_MIXERS, i // N_MIXERS
        if kind == 0:
            q, k, vt, z, new_cache_k, new_cache_v = _attn_in(
                xp, mod3, w_in_a, j, i, None, None, (new_cache_k, new_cache_v))
            xp = _attention(q, k, vt, z, xp, tp, None, mod3, lam_a[j], subln_a[j], w_out_a, j,
                            ln_g[i], ln_b[i], i, False)
            q, k, vt, z = _attn_in(xs, mod3, w_in_a, j, i, ts, rope_tables, None)
            xs = _attention(q, k, vt, z, xs, ts, (cache_k, cache_v, j), mod3, lam_a[j],
                            subln_a[j], w_out_a, j, ln_g[i], ln_b[i], i, True)
        elif kind == 1:
            q, kt, v, g = _ret_in(xp, mod3, w_in_b, j, i, None)
            xp, s_f, s_b = _retention(
                q, kt, v, g, tp, H_B, decay_fwd[j], decay_bwd[j], None,
                out_proj=(xp, mod3, _mod_row_map(i, None, tp), w_out_b, j, ln_g[i], ln_b[i]))
            new_sf.append(s_f)
            new_sb.append(s_b)
            q, kt, v, g = _ret_in(xs, mod3, w_in_b, j, i, ts)
            (y,) = _retention(q, kt, v, g, ts, 1, decay_fwd[j], decay_bwd[j],
                              (state_fwd, state_bwd, j))
            xs = _out_proj(y, w_out_b, j, xs, mod3, ln_g[i], ln_b[i], i, ts)
        else:
            xp = _conv_layer(xp, mod3, w_in_c, conv_c, w_out_c, j, ln_g[i], ln_b[i], i, None, tp)
            xs = _conv_layer(xs, mod3, w_in_c, conv_c, w_out_c, j, ln_g[i], ln_b[i], i, ts, ts)
    y_prompt = xp.reshape(bp, tp, d)
    y_sample = xs.reshape(bs, ts, d)
    new_state_fwd = jnp.concatenate(new_sf, axis=1)
    new_state_bwd = jnp.concatenate(new_sb, axis=1)
    return (y_prompt, y_sample, new_cache_k, new_cache_v, new_state_fwd, new_state_bwd)
```

```python
import functools
import math

import jax
import jax.numpy as jnp
from jax import lax
from jax.experimental import pallas as pl
from jax.experimental.pallas import tpu as pltpu

F32 = jnp.float32
BF16 = jnp.bfloat16

D_MODEL = 1024
DEPTH = 4
N_MIXERS = 3
GRID_W = 64
H_A = 8
DH_A = 64
DV_A = 128
SCORE_SCALE = DH_A ** -0.5 * math.log2(math.e)
H_B = 4
DK_B = 256
DV_B = 512
E_B = H_B * DV_B
CHUNK = 256
ALPHA = (2.0 * DEPTH) ** 0.25
ROPE_BASE = 10000.0
LN_EPS = 1e-5
RMS_EPS = 1e-6

MOD_ROWS = 8
CTX_ROW = 4
VMEM_LIMIT_BYTES = 56 * 1024 * 1024
ROW_TILE = 512


def _params(n_axes):
    return pltpu.CompilerParams(dimension_semantics=("arbitrary",) * n_axes,
                                vmem_limit_bytes=VMEM_LIMIT_BYTES)


def _silu(x):
    return x * jax.nn.sigmoid(x)


def _residual_layer_norm(x, out, gate, g, b):
    r = ALPHA * x + gate * out
    mu = jnp.mean(r, axis=-1, keepdims=True)
    d = r - mu
    var = jnp.mean(d * d, axis=-1, keepdims=True)
    return d * lax.rsqrt(var + LN_EPS) * g + b


def _modulated(x, mod_ref):
    shift = mod_ref[:, 0:D_MODEL]
    scale = mod_ref[:, D_MODEL:2 * D_MODEL]
    return x * (1.0 + scale) + shift


def _mod_row_map(layer, rows_per_batch, tile):
    if rows_per_batch is None:
        return lambda m, *_: (layer * MOD_ROWS + CTX_ROW, 0, 0)
    return lambda m, *_: (layer * MOD_ROWS + (m * tile) // rows_per_batch, 0, 0)


def _both_streams_specs(x_ctx, x_lat, tm):
    n_ctx_tiles = x_ctx.shape[0] // tm
    return [pl.BlockSpec((tm, D_MODEL), lambda m: (jnp.minimum(m, n_ctx_tiles - 1), 0)),
            pl.BlockSpec((tm, D_MODEL), lambda m: (jnp.maximum(m - n_ctx_tiles, 0), 0))]


def _both_streams_rows(x_ctx_ref, x_lat_ref, n_ctx_tiles):
    return jnp.where(pl.program_id(0) < n_ctx_tiles, x_ctx_ref[...], x_lat_ref[...])


def _both_streams_mod_map(layer, n_ctx_tiles, tm, lat_rows_per_batch):
    def index_map(m):
        lat_row = ((m - n_ctx_tiles) * tm) // lat_rows_per_batch
        return (layer * MOD_ROWS + jnp.where(m < n_ctx_tiles, CTX_ROW, lat_row), 0, 0)
    return index_map


def _weight_spec(w, j):
    return pl.BlockSpec((None,) + w.shape[1:], lambda *_: (j, 0, 0), pipeline_mode=pl.Buffered(1))


def _mod_kernel(cv_ref, w_ref, b_ref, o_ref):
    s = _silu(cv_ref[...])
    o_ref[...] = jnp.dot(s, w_ref[...], preferred_element_type=F32) + b_ref[...]


def _modulation(cvec, w_mod, b_mod):
    tn = 1024
    n = 3 * D_MODEL
    out = pl.pallas_call(
        _mod_kernel,
        out_shape=jax.ShapeDtypeStruct((DEPTH, MOD_ROWS, n), F32),
        grid=(DEPTH, n // tn),
        in_specs=[pl.BlockSpec((MOD_ROWS, D_MODEL), lambda i, j: (0, 0)),
                  pl.BlockSpec((None, D_MODEL, tn), lambda i, j: (i, 0, j)),
                  pl.BlockSpec((None, 1, tn), lambda i, j: (i, 0, j))],
        out_specs=pl.BlockSpec((None, MOD_ROWS, tn), lambda i, j: (i, 0, j)),
        compiler_params=_params(2),
        name="modulation",
    )(cvec, w_mod, b_mod.reshape(DEPTH, 1, n))
    return out.reshape(DEPTH * MOD_ROWS, 1, n)


def _rope(xh, cos4, sin4, first_half):
    swapped = jnp.where(first_half, pltpu.roll(xh, 96, 1), pltpu.roll(xh, 32, 1))
    return xh * cos4 + swapped * sin4


def _store_heads(o_ref, x, seq_len):
    for b in range(o_ref.shape[0]):
        xb = x[b * seq_len:(b + 1) * seq_len].reshape(seq_len, H_A, DV_A)
        if len(o_ref.shape) == 4:
            o_ref[b] = xb
        else:
            o_ref[b, 0] = xb
            for s in range(1, o_ref.shape[1]):
                o_ref[b, s] = jnp.zeros_like(xb)


def _attn_in_kernel(*refs, rope, keep_f32, seq_len, n_aliased):
    if rope:
        x_ref, mod_ref, w_ref, cos_ref, sin_ref = refs[:5]
        outs = refs[5 + n_aliased:]
    else:
        x_ref, mod_ref, w_ref = refs[:3]
        outs = refs[3 + n_aliased:]
    q_ref, k_ref, vt_ref, z_ref = outs[:4]
    h = _modulated(x_ref[...], mod_ref)
    if rope:
        lane = lax.broadcasted_iota(jnp.int32, (1, DV_A), 1)
        first_half = (lane % (2 * 32)) < 32
        cos4 = cos_ref[...]
        sin4 = sin_ref[...]
    q_all = jnp.dot(h, w_ref[:, 0:D_MODEL], preferred_element_type=F32)
    k_all = jnp.dot(h, w_ref[:, D_MODEL:2 * D_MODEL], preferred_element_type=F32)
    if keep_f32:
        _store_heads(outs[4], k_all, seq_len)
    for hd in range(H_A):
        cols = slice(hd * DV_A, (hd + 1) * DV_A)
        q = q_all[:, cols]
        k = k_all[:, cols]
        if rope:
            q = _rope(q, cos4, sin4, first_half)
            k = _rope(k, cos4, sin4, first_half)
        q_ref[:, cols] = (q * SCORE_SCALE).astype(BF16)
        k_ref[:, cols] = k.astype(BF16)
    v = jnp.dot(h, w_ref[:, 2 * D_MODEL:3 * D_MODEL], preferred_element_type=F32)
    if keep_f32:
        _store_heads(outs[5], v, seq_len)
    vt_ref[...] = v.T.astype(BF16)
    z_ref[...] = jnp.dot(h, w_ref[:, 3 * D_MODEL:4 * D_MODEL], preferred_element_type=F32)


def _attn_in(x, m_rows, mod3, w_in, j, layer, rows_per_batch, rope_tables, cache_out):
    x2d, x_first_row = x
    tm = ROW_TILE
    rope = rope_tables is not None
    row_spec = pl.BlockSpec((tm, D_MODEL), lambda m: (m, 0))
    in_specs = [pl.BlockSpec((tm, D_MODEL), lambda m: (x_first_row // tm + m, 0)),
                pl.BlockSpec((None, 1, 3 * D_MODEL), _mod_row_map(layer, rows_per_batch, tm)),
                _weight_spec(w_in, j)]
    args = [x2d, mod3, w_in]
    if rope:
        seq_tiles = rows_per_batch // tm
        tab_spec = pl.BlockSpec((tm, DV_A), lambda m: (m % seq_tiles, 0))
        in_specs += [tab_spec, tab_spec]
        args += list(rope_tables)
    out_shape = [jax.ShapeDtypeStruct((m_rows, D_MODEL), BF16),
                 jax.ShapeDtypeStruct((m_rows, D_MODEL), BF16),
                 jax.ShapeDtypeStruct((D_MODEL, m_rows), BF16),
                 jax.ShapeDtypeStruct((m_rows, D_MODEL), F32)]
    out_specs = [row_spec, row_spec, pl.BlockSpec((D_MODEL, tm), lambda m: (0, m)), row_spec]
    aliases = {}
    seq_len = None
    if cache_out is not None:
        new_k, new_v = cache_out
        seq_len = new_k.shape[2]
        cache_shape = jax.ShapeDtypeStruct(new_k.shape, F32)
        if j == 0:
            cache_spec = pl.BlockSpec((tm // seq_len, new_k.shape[1], seq_len, H_A, DV_A),
                                      lambda m: (m, 0, 0, 0, 0))
        else:
            cache_spec = pl.BlockSpec((tm // seq_len, None, seq_len, H_A, DV_A),
                                      lambda m: (m, j, 0, 0, 0))
        out_shape += [cache_shape, cache_shape]
        out_specs += [cache_spec, cache_spec]
        if j > 0:
            in_specs += [pl.BlockSpec(memory_space=pl.ANY)] * 2
            args += [new_k, new_v]
            aliases = {len(args) - 2: 4, len(args) - 1: 5}
    return pl.pallas_call(
        functools.partial(_attn_in_kernel, rope=rope, keep_f32=cache_out is not None,
                          seq_len=seq_len, n_aliased=len(aliases)),
        out_shape=out_shape,
        grid=(m_rows // tm,),
        in_specs=in_specs,
        out_specs=out_specs,
        input_output_aliases=aliases,
        compiler_params=_params(1),
        name="attn_in_rope" if rope else "attn_in",
    )(*args)


ONES_ROWS = 16


def _slab_reduce(op, x):
    parts = [x[i:i + 8] for i in range(0, x.shape[0], 8)]
    while len(parts) > 1:
        parts = [op(parts[i], parts[i + 1]) for i in range(0, len(parts) - 1, 2)] + (
            [parts[-1]] if len(parts) % 2 else [])
    return parts[0]


def _attn_kernel(*refs, layer_idx, has_ctx):
    if has_ctx:
        (q_ref, k_ref, vt_ref, z_ref, x_ref, kc_ref, vc_ref, mod_ref, lam_ref, subln_ref,
         w_ref, g_ref, b_ref, o_ref, y_ref, kcb_ref, vct_ref) = refs

        @pl.when(pl.program_id(1) == 0)
        def _():
            n_ctx = kc_ref.shape[0]
            kcb_ref[...] = kc_ref[...].reshape(n_ctx, D_MODEL).astype(BF16)
            vct_ref[...] = vc_ref[...].reshape(n_ctx, D_MODEL).T.astype(BF16)
    else:
        (q_ref, k_ref, vt_ref, z_ref, x_ref, mod_ref, lam_ref, subln_ref,
         w_ref, g_ref, b_ref, o_ref, y_ref) = refs
    tq = q_ref.shape[0]
    lam_init = 0.8 - 0.6 * math.exp(-0.3 * layer_idx)
    lm = lam_ref[...]
    lam = (jnp.exp(jnp.sum(lm[0:1] * lm[1:2], axis=-1, keepdims=True))
           - jnp.exp(jnp.sum(lm[2:3] * lm[3:4], axis=-1, keepdims=True)) + lam_init)
    lane = lax.broadcasted_iota(jnp.int32, (1, DV_A), 1)
    first = lane < DH_A
    subln = jnp.broadcast_to(subln_ref[...], (DV_A, tq))
    nt = (((1,), (1,)), ((), ()))
    t = k_ref.shape[0]

    def scores(hd):
        cols = slice(hd * DV_A, (hd + 1) * DV_A)
        qh = q_ref[:, cols]
        zero = jnp.zeros_like(qh)
        qq = jnp.concatenate([jnp.where(first, qh, zero), jnp.where(first, zero, qh)], axis=0)
        parts = [lax.dot_general(k_ref[:, cols], qq, nt, preferred_element_type=F32)]
        if has_ctx:
            parts.append(lax.dot_general(kcb_ref[:, cols], qq, nt, preferred_element_type=F32))
        return parts

    def exps(parts):
        m8 = functools.reduce(jnp.maximum, [_slab_reduce(jnp.maximum, s) for s in parts])
        m = jnp.max(m8, axis=0, keepdims=True)
        return [jnp.exp2(s - m).astype(BF16) for s in parts]

    def with_ones(vals_t):
        return jnp.concatenate([vals_t, jnp.ones((ONES_ROWS, vals_t.shape[1]), BF16)], axis=0)

    def finish(hd, es):
        cols = slice(hd * DV_A, (hd + 1) * DV_A)
        acc = jnp.dot(with_ones(vt_ref[cols, :]), es[0], preferred_element_type=F32)
        if has_ctx:
            acc = acc + jnp.dot(with_ones(vct_ref[cols, :]), es[1], preferred_element_type=F32)
        inv = 1.0 / acc[DV_A:DV_A + 1, :]
        ot = acc[:DV_A, :tq] * inv[:, :tq] - acc[:DV_A, tq:] * (inv[:, tq:] * lam)
        ot = ot * lax.rsqrt(jnp.mean(ot * ot, axis=0, keepdims=True) + RMS_EPS)
        ot = ot * subln * (1.0 - lam_init)
        y_ref[:, cols] = (ot.T * _silu(z_ref[:, cols])).astype(BF16)

    s_ahead = {h: scores(h) for h in range(min(2, H_A))}
    e_ahead = {0: exps(s_ahead.pop(0))}
    for hd in range(H_A):
        if hd + 2 < H_A:
            s_ahead[hd + 2] = scores(hd + 2)
        if hd + 1 < H_A:
            e_ahead[hd + 1] = exps(s_ahead.pop(hd + 1))
        finish(hd, e_ahead.pop(hd))
    out = jnp.dot(y_ref[...], w_ref[...].astype(BF16), preferred_element_type=F32)
    gate = mod_ref[:, 2 * D_MODEL:3 * D_MODEL]
    o_ref[...] = _residual_layer_norm(x_ref[...], out, gate, g_ref[...], b_ref[...])


def _attention(q, k, vt, z, x, seq_len, ctx, mod3, lam, subln, w_out, j, ln_g, ln_b, layer,
               per_batch_rows):
    m_rows = q.shape[0]
    t = seq_len
    b = m_rows // t
    tq = 256
    nq = t // tq
    has_ctx = ctx is not None
    x, x_first_row = x
    x_spec = pl.BlockSpec((tq, D_MODEL), lambda i, j: (x_first_row // tq + i * nq + j, 0))
    q_spec = pl.BlockSpec((tq, D_MODEL), lambda i, j: (i * nq + j, 0))
    k_spec = pl.BlockSpec((t, D_MODEL), lambda i, j: (i, 0))
    vt_spec = pl.BlockSpec((D_MODEL, t), lambda i, j: (0, i))
    if per_batch_rows:
        mod_map = lambda i, j: (layer * MOD_ROWS + i, 0, 0)
    else:
        mod_map = lambda i, j: (layer * MOD_ROWS + CTX_ROW, 0, 0)
    const2 = lambda i, j: (0, 0)
    in_specs = [q_spec, k_spec, vt_spec, q_spec, x_spec]
    args = [q, k, vt, z, x]
    scratch = [pltpu.VMEM((tq, D_MODEL), BF16)]
    if has_ctx:
        cache_k, cache_v, jj = ctx
        n_ctx = cache_k.shape[2]
        scratch += [pltpu.VMEM((n_ctx, D_MODEL), BF16), pltpu.VMEM((D_MODEL, n_ctx), BF16)]
        c_spec = pl.BlockSpec((None, None, n_ctx, H_A, DV_A),
                              lambda i, j: (i, jj, 0, 0, 0))
        in_specs += [c_spec, c_spec]
        args += [cache_k, cache_v]
    in_specs += [pl.BlockSpec((None, 1, 3 * D_MODEL), mod_map),
                 pl.BlockSpec((4, DH_A), const2),
                 pl.BlockSpec((DV_A, 1), const2),
                 _weight_spec(w_out, j),
                 pl.BlockSpec((1, D_MODEL), const2),
                 pl.BlockSpec((1, D_MODEL), const2)]
    args += [mod3, lam, subln.reshape(DV_A, 1), w_out, ln_g.reshape(1, D_MODEL),
             ln_b.reshape(1, D_MODEL)]
    return pl.pallas_call(
        functools.partial(_attn_kernel, layer_idx=layer, has_ctx=has_ctx),
        out_shape=jax.ShapeDtypeStruct((m_rows, D_MODEL), F32),
        grid=(b, nq),
        in_specs=in_specs,
        out_specs=q_spec,
        scratch_shapes=scratch,
        compiler_params=_params(2),
        name="diff_attn_ctx" if has_ctx else "diff_attn",
    )(*args)


def _rope_tables(n_tokens):
    rows = n_tokens // GRID_W
    r = jnp.repeat(jnp.arange(rows, dtype=F32), GRID_W)
    col = jnp.tile(jnp.arange(GRID_W, dtype=F32), rows)
    n_freq = DH_A // 4
    inv = ROPE_BASE ** (-jnp.arange(n_freq, dtype=F32) / n_freq)
    ang = jnp.concatenate([r[:, None] * inv, col[:, None] * inv], -1)
    cos, sin = jnp.cos(ang), jnp.sin(ang)
    return jnp.tile(cos, (1, 4)), jnp.concatenate([-sin, sin, -sin, sin], -1)


def _ret_in_kernel(xc_ref, xl_ref, mod_ref, w_ref, q_ref, kt_ref, v_ref, g_ref, *, n_ctx_tiles):
    h = _modulated(_both_streams_rows(xc_ref, xl_ref, n_ctx_tiles), mod_ref)
    nq = H_B * DK_B
    q_ref[...] = jnp.dot(h, w_ref[:, 0:nq], preferred_element_type=F32).astype(BF16)
    k = jnp.dot(h, w_ref[:, nq:2 * nq], preferred_element_type=F32) * (DK_B ** -0.5)
    kt_ref[...] = k.T
    for c in range(E_B // 1024):
        v = jnp.dot(h, w_ref[:, 2 * nq + c * 1024:2 * nq + (c + 1) * 1024],
                    preferred_element_type=F32)
        v_ref[:, c * 1024:(c + 1) * 1024] = v.astype(BF16)
    for c in range(E_B // 1024):
        g_ref[:, c * 1024:(c + 1) * 1024] = jnp.dot(
            h, w_ref[:, 2 * nq + E_B + c * 1024:2 * nq + E_B + (c + 1) * 1024],
            preferred_element_type=F32)


def _ret_in(x_ctx, x_lat, mod3, w_in, j, layer, lat_rows_per_batch):
    m_rows = x_ctx.shape[0] + x_lat.shape[0]
    tm = ROW_TILE
    nq = H_B * DK_B
    n_ctx_tiles = x_ctx.shape[0] // tm
    return pl.pallas_call(
        functools.partial(_ret_in_kernel, n_ctx_tiles=n_ctx_tiles),
        out_shape=[jax.ShapeDtypeStruct((m_rows, nq), BF16),
                   jax.ShapeDtypeStruct((nq, m_rows), F32),
                   jax.ShapeDtypeStruct((m_rows, E_B), BF16),
                   jax.ShapeDtypeStruct((m_rows, E_B), F32)],
        grid=(m_rows // tm,),
        in_specs=_both_streams_specs(x_ctx, x_lat, tm) + [
            pl.BlockSpec((None, 1, 3 * D_MODEL),
                         _both_streams_mod_map(layer, n_ctx_tiles, tm, lat_rows_per_batch)),
            _weight_spec(w_in, j)],
        out_specs=[pl.BlockSpec((tm, nq), lambda m: (m, 0)),
                   pl.BlockSpec((nq, tm), lambda m: (0, m)),
                   pl.BlockSpec((tm, E_B), lambda m: (m, 0)),
                   pl.BlockSpec((tm, E_B), lambda m: (m, 0))],
        compiler_params=_params(1),
        name="ret_in",
    )(x_ctx, x_lat, mod3, w_in)


def _retention_kernel(*refs, has_state, heads, fused_out):
    refs = list(refs)
    q_ref, kt_ref, v_ref, g_ref, af_ref, ab_ref = refs[:6]
    del refs[:6]
    if has_state:
        s0f_ref, s0b_ref = refs[:2]
        del refs[:2]
    if fused_out:
        x_ref, mod_ref, w_ref, lng_ref, lnb_ref = refs[:5]
        del refs[:5]
        o_ref = refs.pop(0)
        y_ref = refs.pop()
    else:
        y_ref = refs.pop(0)
    if not has_state:
        sf_ref, sb_ref = refs
    t = q_ref.shape[0]
    nc = t // CHUNK
    row = lax.broadcasted_iota(jnp.int32, (CHUNK, CHUNK), 0).astype(F32)
    col = lax.broadcasted_iota(jnp.int32, (CHUNK, CHUNK), 1).astype(F32)
    diff = row - col
    idx_col = lax.broadcasted_iota(jnp.int32, (CHUNK, 1), 0).astype(F32)
    idx_row = lax.broadcasted_iota(jnp.int32, (1, CHUNK), 1).astype(F32)

    def chunk(c):
        return slice(c * CHUNK, (c + 1) * CHUNK)

    for hh in range(heads):
        qk_cols = slice(hh * DK_B, (hh + 1) * DK_B)
        v_cols = slice(hh * DV_B, (hh + 1) * DV_B)
        lg_f = jnp.log1p(-jnp.exp(af_ref[hh]))
        lg_b = jnp.log1p(-jnp.exp(ab_ref[hh]))
        dmask = (jnp.where(diff >= 0, jnp.exp(jnp.maximum(diff, 0.0) * lg_f), 0.0)
                 + jnp.where(diff <= 0, jnp.exp(jnp.maximum(-diff, 0.0) * lg_b), 0.0))
        qd_f = jnp.exp((idx_col + 1.0) * lg_f)
        qd_b = jnp.exp((CHUNK - idx_col) * lg_b)
        kd_f = jnp.exp((CHUNK - 1.0 - idx_row) * lg_f)
        kd_b = jnp.exp(idx_row * lg_b)
        cd_f = jnp.exp(CHUNK * lg_f)
        cd_b = jnp.exp(CHUNK * lg_b)

        def states(order, kd, cd, s):
            seen = {}
            for n, c in enumerate(order):
                seen[c] = None if s is None else s.astype(BF16)
                if has_state and n == nc - 1:
                    return seen, None
                u = jnp.dot((kt_ref[qk_cols, chunk(c)] * kd).astype(BF16), v_ref[chunk(c), v_cols],
                            preferred_element_type=F32)
                s = u if s is None else s * cd + u
            return seen, s

        seen_f, s_f = states(range(nc), kd_f, cd_f, s0f_ref[hh] if has_state else None)
        seen_b, s_b = states(range(nc - 1, -1, -1), kd_b, cd_b, s0b_ref[hh] if has_state else None)
        if not has_state:
            sf_ref[hh] = s_f
            sb_ref[hh] = s_b
        for c in range(nc):
            qc = q_ref[chunk(c), qk_cols]
            qk = jnp.dot(qc, kt_ref[qk_cols, chunk(c)].astype(BF16), preferred_element_type=F32)
            o = jnp.dot((qk * dmask).astype(BF16), v_ref[chunk(c), v_cols],
                        preferred_element_type=F32)
            if seen_f[c] is not None:
                o = o + jnp.dot(qc, seen_f[c], preferred_element_type=F32) * qd_f
            if seen_b[c] is not None:
                o = o + jnp.dot(qc, seen_b[c], preferred_element_type=F32) * qd_b
            o = o * lax.rsqrt(jnp.mean(o * o, axis=-1, keepdims=True) + RMS_EPS)
            y_ref[chunk(c), v_cols] = (o * _silu(g_ref[chunk(c), v_cols])).astype(BF16)
    if fused_out:
        out = jnp.dot(y_ref[...], w_ref[...].astype(BF16), preferred_element_type=F32)
        gate = mod_ref[:, 2 * D_MODEL:3 * D_MODEL]
        o_ref[...] = _residual_layer_norm(x_ref[...], out, gate, lng_ref[...], lnb_ref[...])


def _retention(q, kt, v, g, seq_len, n_seq, first_row, heads, decay_f, decay_b, states,
               out_proj=None):
    t = seq_len
    b = n_seq
    s0 = first_row // t
    has_state = states is not None
    fused_out = out_proj is not None
    assert not fused_out or heads == H_B
    q_spec = pl.BlockSpec((t, heads * DK_B), lambda i, h: (s0 + i, h))
    kt_spec = pl.BlockSpec((heads * DK_B, t), lambda i, h: (h, s0 + i))
    vg_spec = pl.BlockSpec((t, heads * DV_B), lambda i, h: (s0 + i, h))
    a_spec = pl.BlockSpec((heads, 1, 1), lambda i, h: (h, 0, 0))
    in_specs = [q_spec, kt_spec, vg_spec, vg_spec, a_spec, a_spec]
    args = [q, kt, v, g, decay_f.reshape(H_B, 1, 1), decay_b.reshape(H_B, 1, 1)]
    out_shape = [jax.ShapeDtypeStruct((b * t, E_B), BF16)]
    out_specs = [pl.BlockSpec((t, heads * DV_B), lambda i, h: (i, h))]
    if has_state:
        s_f, s_b, jj = states
        s_spec = pl.BlockSpec((None, None, heads, DK_B, DV_B), lambda i, h: (i, jj, h, 0, 0))
        in_specs += [s_spec, s_spec]
        args += [s_f, s_b]
    scratch = []
    if fused_out:
        x2d, mod3, mod_map, w_out, jw, ln_g, ln_b = out_proj
        x_spec = pl.BlockSpec((t, D_MODEL), lambda i, h: (i, 0))
        const2 = lambda i, h: (0, 0)
        in_specs += [x_spec, pl.BlockSpec((None, 1, 3 * D_MODEL), mod_map), _weight_spec(w_out, jw),
                     pl.BlockSpec((1, D_MODEL), const2), pl.BlockSpec((1, D_MODEL), const2)]
        args += [x2d, mod3, w_out, ln_g.reshape(1, D_MODEL), ln_b.reshape(1, D_MODEL)]
        out_shape = [jax.ShapeDtypeStruct((b * t, D_MODEL), F32)]
        out_specs = [x_spec]
        scratch = [pltpu.VMEM((t, E_B), BF16)]
    if not has_state:
        so_spec = pl.BlockSpec((None, None, heads, DK_B, DV_B), lambda i, h: (i, 0, h, 0, 0))
        out_shape += [jax.ShapeDtypeStruct((b, 1, H_B, DK_B, DV_B), F32)] * 2
        out_specs += [so_spec, so_spec]
    return pl.pallas_call(
        functools.partial(_retention_kernel, has_state=has_state, heads=heads,
                          fused_out=fused_out),
        out_shape=out_shape,
        grid=(b, H_B // heads),
        in_specs=in_specs,
        out_specs=out_specs,
        scratch_shapes=scratch,
        compiler_params=_params(2),
        name="retention_state" if has_state else "retention",
    )(*args)


def _out_proj_kernel(y_ref, w_ref, x_ref, mod_ref, g_ref, b_ref, o_ref):
    out = jnp.dot(y_ref[...], w_ref[...].astype(BF16), preferred_element_type=F32)
    gate = mod_ref[:, 2 * D_MODEL:3 * D_MODEL]
    o_ref[...] = _residual_layer_norm(x_ref[...], out, gate, g_ref[...], b_ref[...])


def _out_proj(y2d, w_out, j, x2d, mod3, ln_g, ln_b, layer, rows_per_batch):
    m_rows, e = y2d.shape
    tm = ROW_TILE
    const2 = lambda m: (0, 0)
    return pl.pallas_call(
        _out_proj_kernel,
        out_shape=jax.ShapeDtypeStruct((m_rows, D_MODEL), F32),
        grid=(m_rows // tm,),
        in_specs=[pl.BlockSpec((tm, e), lambda m: (m, 0)),
                  _weight_spec(w_out, j),
                  pl.BlockSpec((tm, D_MODEL), lambda m: (m, 0)),
                  pl.BlockSpec((None, 1, 3 * D_MODEL), _mod_row_map(layer, rows_per_batch, tm)),
                  pl.BlockSpec((1, D_MODEL), const2),
                  pl.BlockSpec((1, D_MODEL), const2)],
        out_specs=pl.BlockSpec((tm, D_MODEL), lambda m: (m, 0)),
        compiler_params=_params(1),
        name="out_proj",
    )(y2d, w_out, x2d, mod3, ln_g.reshape(1, D_MODEL), ln_b.reshape(1, D_MODEL))


CONV_ROWS = 1024
CONV_COLS = 256


def _conv_kernel(xc_ref, xl_ref, mod_ref, w_in_ref, cw_ref, w_out_ref, g_ref, b_ref, o_ref, *,
                 n_ctx_tiles, ctx_seq_len, lat_seq_len):
    x = _both_streams_rows(xc_ref, xl_ref, n_ctx_tiles)
    h = _modulated(x, mod_ref)
    rows = x.shape[0]
    is_ctx = pl.program_id(0) < n_ctx_tiles
    row = lax.broadcasted_iota(jnp.int32, (rows, 1), 0)
    pos = jnp.where(is_ctx, row % ctx_seq_len, row % lat_seq_len)
    has_prev = pos > 0
    has_next = pos < jnp.where(is_ctx, ctx_seq_len - 1, lat_seq_len - 1)
    e = D_MODEL
    for c in range(e // CONV_COLS):
        cols = slice(c * CONV_COLS, (c + 1) * CONV_COLS)

        def proj(part):
            lo = part * e + c * CONV_COLS
            return jnp.dot(h, w_in_ref[:, lo:lo + CONV_COLS], preferred_element_type=F32)

        p = proj(1) * proj(2)
        prev = jnp.where(has_prev, pltpu.roll(p, 1, 0), 0.0)
        nxt = jnp.where(has_next, pltpu.roll(p, rows - 1, 0), 0.0)
        conv = prev * cw_ref[0:1, cols] + p * cw_ref[1:2, cols] + nxt * cw_ref[2:3, cols]
        y = (proj(0) * conv * _silu(proj(3))).astype(BF16)
        part = jnp.dot(y, w_out_ref[cols, :].astype(BF16), preferred_element_type=F32)
        if c == 0:
            o_ref[...] = part
        else:
            o_ref[...] += part
    gate = mod_ref[:, 2 * D_MODEL:3 * D_MODEL]
    o_ref[...] = _residual_layer_norm(x, o_ref[...], gate, g_ref[...], b_ref[...])


def _conv_layer(x_ctx, x_lat, mod3, w_in, conv_w, w_out, j, ln_g, ln_b, layer, ctx_seq_len,
                lat_seq_len):
    m_rows = x_ctx.shape[0] + x_lat.shape[0]
    tm = CONV_ROWS
    n_ctx_tiles = x_ctx.shape[0] // tm
    const2 = lambda m: (0, 0)
    return pl.pallas_call(
        functools.partial(_conv_kernel, n_ctx_tiles=n_ctx_tiles, ctx_seq_len=ctx_seq_len,
                          lat_seq_len=lat_seq_len),
        out_shape=jax.ShapeDtypeStruct((m_rows, D_MODEL), F32),
        grid=(m_rows // tm,),
        in_specs=_both_streams_specs(x_ctx, x_lat, tm) + [
                  pl.BlockSpec((None, 1, 3 * D_MODEL),
                               _both_streams_mod_map(layer, n_ctx_tiles, tm, lat_seq_len)),
                  _weight_spec(w_in, j),
                  pl.BlockSpec((None, 3, D_MODEL), lambda m: (j, 0, 0)),
                  _weight_spec(w_out, j),
                  pl.BlockSpec((1, D_MODEL), const2),
                  pl.BlockSpec((1, D_MODEL), const2)],
        out_specs=pl.BlockSpec((tm, D_MODEL), lambda m: (m, 0)),
        compiler_params=_params(1),
        name="conv_layer",
    )(x_ctx, x_lat, mod3, w_in, conv_w, w_out, ln_g.reshape(1, D_MODEL), ln_b.reshape(1, D_MODEL))


def _separate(xp, xs, m_ctx):
    if xp[0] is xs[0]:
        return (xp[0][:m_ctx], 0), (xs[0][m_ctx:], 0)
    return xp, xs


def kernel(x_prompt, x_sample, cache_k, cache_v, state_fwd, state_bwd, c, c_ctx, w_mod, b_mod, ln_g,
           ln_b, w_in_a, lam_a, subln_a, w_out_a, w_in_b, decay_fwd, decay_bwd, w_out_b, w_in_c,
           conv_c, w_out_c):
    bp, tp, d = x_prompt.shape
    bs, ts, _ = x_sample.shape
    cvec = jnp.concatenate([c, c_ctx[None], jnp.zeros((MOD_ROWS - bs - 1, d), F32)], axis=0)
    mod3 = _modulation(cvec, w_mod, b_mod)
    rope_tables = _rope_tables(ts)

    m_ctx, m_lat = bp * tp, bs * ts
    xp = (x_prompt.reshape(m_ctx, d), 0)
    xs = (x_sample.reshape(m_lat, d), 0)
    n_attn = (DEPTH + N_MIXERS - 1) // N_MIXERS
    new_cache_k = new_cache_v = jax.ShapeDtypeStruct((bp, n_attn, tp, H_A, DV_A), F32)
    new_sf, new_sb = [], []
    for i in range(DEPTH):
        kind, j = i % N_MIXERS, i // N_MIXERS
        if kind == 0:
            q, k, vt, z, new_cache_k, new_cache_v = _attn_in(
                xp, m_ctx, mod3, w_in_a, j, i, None, None, (new_cache_k, new_cache_v))
            xp = (_attention(q, k, vt, z, xp, tp, None, mod3, lam_a[j], subln_a[j], w_out_a, j,
                             ln_g[i], ln_b[i], i, False), 0)
            q, k, vt, z = _attn_in(xs, m_lat, mod3, w_in_a, j, i, ts, rope_tables, None)
            xs = (_attention(q, k, vt, z, xs, ts, (cache_k, cache_v, j), mod3, lam_a[j],
                             subln_a[j], w_out_a, j, ln_g[i], ln_b[i], i, True), 0)
        else:
            xp, xs = _separate(xp, xs, m_ctx)
            if kind == 1:
                q, kt, v, g = _ret_in(xp[0], xs[0], mod3, w_in_b, j, i, ts)
                x_ctx, s_f, s_b = _retention(
                    q, kt, v, g, tp, bp, 0, H_B, decay_fwd[j], decay_bwd[j], None,
                    out_proj=(xp[0], mod3, _mod_row_map(i, None, tp), w_out_b, j, ln_g[i],
                              ln_b[i]))
                new_sf.append(s_f)
                new_sb.append(s_b)
                (y,) = _retention(q, kt, v, g, ts, bs, m_ctx, 1, decay_fwd[j], decay_bwd[j],
                                  (state_fwd, state_bwd, j))
                xp = (x_ctx, 0)
                xs = (_out_proj(y, w_out_b, j, xs[0], mod3, ln_g[i], ln_b[i], i, ts), 0)
            else:
                x_all = _conv_layer(xp[0], xs[0], mod3, w_in_c, conv_c, w_out_c, j, ln_g[i],
                                    ln_b[i], i, tp, ts)
                xp, xs = (x_all, 0), (x_all, m_ctx)
    xp, xs = _separate(xp, xs, m_ctx)
    y_prompt = xp[0].reshape(bp, tp, d)
    y_sample = xs[0].reshape(bs, ts, d)
    new_state_fwd = jnp.concatenate(new_sf, axis=1)
    new_state_bwd = jnp.concatenate(new_sb, axis=1)
    return (y_prompt, y_sample, new_cache_k, new_cache_v, new_state_fwd, new_state_bwd)
```

```python
import functools
import math

import jax
import jax.numpy as jnp
from jax import lax
from jax.experimental import pallas as pl
from jax.experimental.pallas import tpu as pltpu

F32 = jnp.float32
BF16 = jnp.bfloat16

D_MODEL = 1024
DEPTH = 4
N_MIXERS = 3
GRID_W = 64
H_A = 8
DH_A = 64
DV_A = 128
SCORE_SCALE = DH_A ** -0.5 * math.log2(math.e)
H_B = 4
DK_B = 256
DV_B = 512
E_B = H_B * DV_B
CHUNK = 256
ALPHA = (2.0 * DEPTH) ** 0.25
ROPE_BASE = 10000.0
LN_EPS = 1e-5
RMS_EPS = 1e-6

MOD_ROWS = 8
CTX_ROW = 4
VMEM_LIMIT_BYTES = 56 * 1024 * 1024
ROW_TILE = 512


def _params(n_axes):
    return pltpu.CompilerParams(dimension_semantics=("arbitrary",) * n_axes,
                                vmem_limit_bytes=VMEM_LIMIT_BYTES)


def _silu(x):
    return x * jax.nn.sigmoid(x)


def _residual_layer_norm(x, out, gate, g, b):
    r = ALPHA * x + gate * out
    mu = jnp.mean(r, axis=-1, keepdims=True)
    d = r - mu
    var = jnp.mean(d * d, axis=-1, keepdims=True)
    return d * lax.rsqrt(var + LN_EPS) * g + b


def _modulated(x, mod_ref):
    shift = mod_ref[:, 0:D_MODEL]
    scale = mod_ref[:, D_MODEL:2 * D_MODEL]
    return x * (1.0 + scale) + shift


def _mod_row_map(layer, rows_per_batch, tile):
    if rows_per_batch is None:
        return lambda m, *_: (layer * MOD_ROWS + CTX_ROW, 0, 0)
    return lambda m, *_: (layer * MOD_ROWS + (m * tile) // rows_per_batch, 0, 0)


def _both_streams_specs(x_ctx, x_lat, tm):
    n_ctx_tiles = x_ctx.shape[0] // tm
    return [pl.BlockSpec((tm, D_MODEL), lambda m: (jnp.minimum(m, n_ctx_tiles - 1), 0)),
            pl.BlockSpec((tm, D_MODEL), lambda m: (jnp.maximum(m - n_ctx_tiles, 0), 0))]


def _both_streams_rows(x_ctx_ref, x_lat_ref, n_ctx_tiles):
    return jnp.where(pl.program_id(0) < n_ctx_tiles, x_ctx_ref[...], x_lat_ref[...])


def _both_streams_mod_map(layer, n_ctx_tiles, tm, lat_rows_per_batch):
    def index_map(m):
        lat_row = ((m - n_ctx_tiles) * tm) // lat_rows_per_batch
        return (layer * MOD_ROWS + jnp.where(m < n_ctx_tiles, CTX_ROW, lat_row), 0, 0)
    return index_map


def _weight_spec(w, j):
    return pl.BlockSpec((None,) + w.shape[1:], lambda *_: (j, 0, 0), pipeline_mode=pl.Buffered(1))


def _mod_kernel(cv_ref, w_ref, b_ref, o_ref):
    s = _silu(cv_ref[...])
    o_ref[...] = jnp.dot(s, w_ref[...], preferred_element_type=F32) + b_ref[...]


def _modulation(cvec, w_mod, b_mod):
    tn = 1024
    n = 3 * D_MODEL
    out = pl.pallas_call(
        _mod_kernel,
        out_shape=jax.ShapeDtypeStruct((DEPTH, MOD_ROWS, n), F32),
        grid=(DEPTH, n // tn),
        in_specs=[pl.BlockSpec((MOD_ROWS, D_MODEL), lambda i, j: (0, 0)),
                  pl.BlockSpec((None, D_MODEL, tn), lambda i, j: (i, 0, j)),
                  pl.BlockSpec((None, 1, tn), lambda i, j: (i, 0, j))],
        out_specs=pl.BlockSpec((None, MOD_ROWS, tn), lambda i, j: (i, 0, j)),
        compiler_params=_params(2),
        name="modulation",
    )(cvec, w_mod, b_mod.reshape(DEPTH, 1, n))
    return out.reshape(DEPTH * MOD_ROWS, 1, n)


def _rope(xh, cos4, sin4, first_half):
    swapped = jnp.where(first_half, pltpu.roll(xh, 96, 1), pltpu.roll(xh, 32, 1))
    return xh * cos4 + swapped * sin4


def _store_heads(o_ref, x, seq_len):
    for b in range(o_ref.shape[0]):
        xb = x[b * seq_len:(b + 1) * seq_len].reshape(seq_len, H_A, DV_A)
        if len(o_ref.shape) == 4:
            o_ref[b] = xb
        else:
            o_ref[b, 0] = xb
            for s in range(1, o_ref.shape[1]):
                o_ref[b, s] = jnp.zeros_like(xb)


def _attn_in_kernel(*refs, rope, keep_f32, seq_len, n_aliased):
    if rope:
        x_ref, mod_ref, w_ref, cos_ref, sin_ref = refs[:5]
        outs = refs[5 + n_aliased:]
    else:
        x_ref, mod_ref, w_ref = refs[:3]
        outs = refs[3 + n_aliased:]
    q_ref, k_ref, vt_ref, z_ref = outs[:4]
    h = _modulated(x_ref[...], mod_ref)
    if rope:
        lane = lax.broadcasted_iota(jnp.int32, (1, DV_A), 1)
        first_half = (lane % (2 * 32)) < 32
        cos4 = cos_ref[...]
        sin4 = sin_ref[...]
    q_all = jnp.dot(h, w_ref[:, 0:D_MODEL], preferred_element_type=F32)
    k_all = jnp.dot(h, w_ref[:, D_MODEL:2 * D_MODEL], preferred_element_type=F32)
    if keep_f32:
        _store_heads(outs[4], k_all, seq_len)
    for hd in range(H_A):
        cols = slice(hd * DV_A, (hd + 1) * DV_A)
        q = q_all[:, cols]
        k = k_all[:, cols]
        if rope:
            q = _rope(q, cos4, sin4, first_half)
            k = _rope(k, cos4, sin4, first_half)
        q_ref[:, cols] = (q * SCORE_SCALE).astype(BF16)
        k_ref[:, cols] = k.astype(BF16)
    v = jnp.dot(h, w_ref[:, 2 * D_MODEL:3 * D_MODEL], preferred_element_type=F32)
    if keep_f32:
        _store_heads(outs[5], v, seq_len)
    vt_ref[...] = v.T.astype(BF16)
    z_ref[...] = jnp.dot(h, w_ref[:, 3 * D_MODEL:4 * D_MODEL], preferred_element_type=F32)


def _attn_in(x, m_rows, mod3, w_in, j, layer, rows_per_batch, rope_tables, cache_out):
    x2d, x_first_row = x
    tm = ROW_TILE
    rope = rope_tables is not None
    row_spec = pl.BlockSpec((tm, D_MODEL), lambda m: (m, 0))
    in_specs = [pl.BlockSpec((tm, D_MODEL), lambda m: (x_first_row // tm + m, 0)),
                pl.BlockSpec((None, 1, 3 * D_MODEL), _mod_row_map(layer, rows_per_batch, tm)),
                _weight_spec(w_in, j)]
    args = [x2d, mod3, w_in]
    if rope:
        seq_tiles = rows_per_batch // tm
        tab_spec = pl.BlockSpec((tm, DV_A), lambda m: (m % seq_tiles, 0))
        in_specs += [tab_spec, tab_spec]
        args += list(rope_tables)
    out_shape = [jax.ShapeDtypeStruct((m_rows, D_MODEL), BF16),
                 jax.ShapeDtypeStruct((m_rows, D_MODEL), BF16),
                 jax.ShapeDtypeStruct((D_MODEL, m_rows), BF16),
                 jax.ShapeDtypeStruct((m_rows, D_MODEL), F32)]
    out_specs = [row_spec, row_spec, pl.BlockSpec((D_MODEL, tm), lambda m: (0, m)), row_spec]
    aliases = {}
    seq_len = None
    if cache_out is not None:
        new_k, new_v = cache_out
        seq_len = new_k.shape[2]
        cache_shape = jax.ShapeDtypeStruct(new_k.shape, F32)
        if j == 0:
            cache_spec = pl.BlockSpec((tm // seq_len, new_k.shape[1], seq_len, H_A, DV_A),
                                      lambda m: (m, 0, 0, 0, 0))
        else:
            cache_spec = pl.BlockSpec((tm // seq_len, None, seq_len, H_A, DV_A),
                                      lambda m: (m, j, 0, 0, 0))
        out_shape += [cache_shape, cache_shape]
        out_specs += [cache_spec, cache_spec]
        if j > 0:
            in_specs += [pl.BlockSpec(memory_space=pl.ANY)] * 2
            args += [new_k, new_v]
            aliases = {len(args) - 2: 4, len(args) - 1: 5}
    return pl.pallas_call(
        functools.partial(_attn_in_kernel, rope=rope, keep_f32=cache_out is not None,
                          seq_len=seq_len, n_aliased=len(aliases)),
        out_shape=out_shape,
        grid=(m_rows // tm,),
        in_specs=in_specs,
        out_specs=out_specs,
        input_output_aliases=aliases,
        compiler_params=_params(1),
        name="attn_in_rope" if rope else "attn_in",
    )(*args)


ONES_ROWS = 16


def _slab_reduce(op, x):
    parts = [x[i:i + 8] for i in range(0, x.shape[0], 8)]
    while len(parts) > 1:
        parts = [op(parts[i], parts[i + 1]) for i in range(0, len(parts) - 1, 2)] + (
            [parts[-1]] if len(parts) % 2 else [])
    return parts[0]


def _attn_kernel(*refs, layer_idx, has_ctx, n_seq):
    if has_ctx:
        (q_ref, k_ref, vt_ref, z_ref, x_ref, kc_ref, vc_ref, mod_ref, lam_ref, subln_ref,
         w_ref, g_ref, b_ref, o_ref, y_ref, kcb_ref, vct_ref) = refs

        @pl.when(pl.program_id(1) == 0)
        def _():
            n_ctx = kc_ref.shape[0]
            kcb_ref[...] = kc_ref[...].reshape(n_ctx, D_MODEL).astype(BF16)
            vct_ref[...] = vc_ref[...].reshape(n_ctx, D_MODEL).T.astype(BF16)
    else:
        (q_ref, k_ref, vt_ref, z_ref, x_ref, mod_ref, lam_ref, subln_ref,
         w_ref, g_ref, b_ref, o_ref, y_ref) = refs
    tq = q_ref.shape[0] // n_seq
    lam_init = 0.8 - 0.6 * math.exp(-0.3 * layer_idx)
    lm = lam_ref[...]
    lam = (jnp.exp(jnp.sum(lm[0:1] * lm[1:2], axis=-1, keepdims=True))
           - jnp.exp(jnp.sum(lm[2:3] * lm[3:4], axis=-1, keepdims=True)) + lam_init)
    lane = lax.broadcasted_iota(jnp.int32, (1, DV_A), 1)
    first = lane < DH_A
    subln = jnp.broadcast_to(subln_ref[...], (DV_A, tq))
    nt = (((1,), (1,)), ((), ()))
    t = k_ref.shape[0] // n_seq

    def scores(unit):
        sq, hd = unit
        cols = slice(hd * DV_A, (hd + 1) * DV_A)
        qh = q_ref[sq * tq:(sq + 1) * tq, cols]
        zero = jnp.zeros_like(qh)
        qq = jnp.concatenate([jnp.where(first, qh, zero), jnp.where(first, zero, qh)], axis=0)
        parts = [lax.dot_general(k_ref[sq * t:(sq + 1) * t, cols], qq, nt,
                                 preferred_element_type=F32)]
        if has_ctx:
            parts.append(lax.dot_general(kcb_ref[:, cols], qq, nt, preferred_element_type=F32))
        return parts

    def exps(parts):
        m8 = functools.reduce(jnp.maximum, [_slab_reduce(jnp.maximum, s) for s in parts])
        m = jnp.max(m8, axis=0, keepdims=True)
        return [jnp.exp2(s - m).astype(BF16) for s in parts]

    def with_ones(vals_t):
        return jnp.concatenate([vals_t, jnp.ones((ONES_ROWS, vals_t.shape[1]), BF16)], axis=0)

    def finish(unit, es):
        sq, hd = unit
        cols = slice(hd * DV_A, (hd + 1) * DV_A)
        rows = slice(sq * tq, (sq + 1) * tq)
        acc = jnp.dot(with_ones(vt_ref[cols, sq * t:(sq + 1) * t]), es[0],
                      preferred_element_type=F32)
        if has_ctx:
            acc = acc + jnp.dot(with_ones(vct_ref[cols, :]), es[1], preferred_element_type=F32)
        inv = 1.0 / acc[DV_A:DV_A + 1, :]
        ot = acc[:DV_A, :tq] * inv[:, :tq] - acc[:DV_A, tq:] * (inv[:, tq:] * lam)
        ot = ot * lax.rsqrt(jnp.mean(ot * ot, axis=0, keepdims=True) + RMS_EPS)
        ot = ot * subln * (1.0 - lam_init)
        y_ref[rows, cols] = (ot.T * _silu(z_ref[rows, cols])).astype(BF16)
        if hd == H_A - 1:
            out = jnp.dot(y_ref[rows, :], w_ref[...].astype(BF16), preferred_element_type=F32)
            gate = mod_ref[:, 2 * D_MODEL:3 * D_MODEL]
            o_ref[rows, :] = _residual_layer_norm(x_ref[rows, :], out, gate, g_ref[...], b_ref[...])

    units = [(sq, hd) for sq in range(n_seq) for hd in range(H_A)]
    s_ahead = {u: scores(units[u]) for u in range(min(2, len(units)))}
    e_ahead = {0: exps(s_ahead.pop(0))}
    for u in range(len(units)):
        if u + 2 < len(units):
            s_ahead[u + 2] = scores(units[u + 2])
        if u + 1 < len(units):
            e_ahead[u + 1] = exps(s_ahead.pop(u + 1))
        finish(units[u], e_ahead.pop(u))


def _attention(q, k, vt, z, x, seq_len, ctx, mod3, lam, subln, w_out, j, ln_g, ln_b, layer,
               per_batch_rows):
    m_rows = q.shape[0]
    t = seq_len
    b = m_rows // t
    tq = 256
    nq = t // tq
    has_ctx = ctx is not None
    n_seq = 4 if (nq == 1 and not per_batch_rows and not has_ctx and b % 4 == 0) else 1
    rows = n_seq * tq
    x, x_first_row = x
    x_spec = pl.BlockSpec((rows, D_MODEL), lambda i, j: (x_first_row // rows + i * nq + j, 0))
    q_spec = pl.BlockSpec((rows, D_MODEL), lambda i, j: (i * nq + j, 0))
    k_spec = pl.BlockSpec((n_seq * t, D_MODEL), lambda i, j: (i, 0))
    vt_spec = pl.BlockSpec((D_MODEL, n_seq * t), lambda i, j: (0, i))
    if per_batch_rows:
        mod_map = lambda i, j: (layer * MOD_ROWS + i, 0, 0)
    else:
        mod_map = lambda i, j: (layer * MOD_ROWS + CTX_ROW, 0, 0)
    const2 = lambda i, j: (0, 0)
    in_specs = [q_spec, k_spec, vt_spec, q_spec, x_spec]
    args = [q, k, vt, z, x]
    scratch = [pltpu.VMEM((rows, D_MODEL), BF16)]
    if has_ctx:
        cache_k, cache_v, jj = ctx
        n_ctx = cache_k.shape[2]
        scratch += [pltpu.VMEM((n_ctx, D_MODEL), BF16), pltpu.VMEM((D_MODEL, n_ctx), BF16)]
        c_spec = pl.BlockSpec((None, None, n_ctx, H_A, DV_A),
                              lambda i, j: (i, jj, 0, 0, 0))
        in_specs += [c_spec, c_spec]
        args += [cache_k, cache_v]
    in_specs += [pl.BlockSpec((None, 1, 3 * D_MODEL), mod_map),
                 pl.BlockSpec((4, DH_A), const2),
                 pl.BlockSpec((DV_A, 1), const2),
                 _weight_spec(w_out, j),
                 pl.BlockSpec((1, D_MODEL), const2),
                 pl.BlockSpec((1, D_MODEL), const2)]
    args += [mod3, lam, subln.reshape(DV_A, 1), w_out, ln_g.reshape(1, D_MODEL),
             ln_b.reshape(1, D_MODEL)]
    return pl.pallas_call(
        functools.partial(_attn_kernel, layer_idx=layer, has_ctx=has_ctx, n_seq=n_seq),
        out_shape=jax.ShapeDtypeStruct((m_rows, D_MODEL), F32),
        grid=(b // n_seq, nq),
        in_specs=in_specs,
        out_specs=q_spec,
        scratch_shapes=scratch,
        compiler_params=_params(2),
        name="diff_attn_ctx" if has_ctx else "diff_attn",
    )(*args)


def _rope_tables(n_tokens):
    rows = n_tokens // GRID_W
    r = jnp.repeat(jnp.arange(rows, dtype=F32), GRID_W)
    col = jnp.tile(jnp.arange(GRID_W, dtype=F32), rows)
    n_freq = DH_A // 4
    inv = ROPE_BASE ** (-jnp.arange(n_freq, dtype=F32) / n_freq)
    ang = jnp.concatenate([r[:, None] * inv, col[:, None] * inv], -1)
    cos, sin = jnp.cos(ang), jnp.sin(ang)
    return jnp.tile(cos, (1, 4)), jnp.concatenate([-sin, sin, -sin, sin], -1)


def _ret_in_kernel(xc_ref, xl_ref, mod_ref, w_ref, q_ref, kt_ref, v_ref, g_ref, *, n_ctx_tiles):
    h = _modulated(_both_streams_rows(xc_ref, xl_ref, n_ctx_tiles), mod_ref)
    nq = H_B * DK_B
    q_ref[...] = jnp.dot(h, w_ref[:, 0:nq], preferred_element_type=F32).astype(BF16)
    k = jnp.dot(h, w_ref[:, nq:2 * nq], preferred_element_type=F32) * (DK_B ** -0.5)
    kt_ref[...] = k.T
    for c in range(E_B // 1024):
        v = jnp.dot(h, w_ref[:, 2 * nq + c * 1024:2 * nq + (c + 1) * 1024],
                    preferred_element_type=F32)
        v_ref[:, c * 1024:(c + 1) * 1024] = v.astype(BF16)
    for c in range(E_B // 1024):
        g_ref[:, c * 1024:(c + 1) * 1024] = jnp.dot(
            h, w_ref[:, 2 * nq + E_B + c * 1024:2 * nq + E_B + (c + 1) * 1024],
            preferred_element_type=F32)


def _ret_in(x_ctx, x_lat, mod3, w_in, j, layer, lat_rows_per_batch):
    m_rows = x_ctx.shape[0] + x_lat.shape[0]
    tm = ROW_TILE
    nq = H_B * DK_B
    n_ctx_tiles = x_ctx.shape[0] // tm
    return pl.pallas_call(
        functools.partial(_ret_in_kernel, n_ctx_tiles=n_ctx_tiles),
        out_shape=[jax.ShapeDtypeStruct((m_rows, nq), BF16),
                   jax.ShapeDtypeStruct((nq, m_rows), F32),
                   jax.ShapeDtypeStruct((m_rows, E_B), BF16),
                   jax.ShapeDtypeStruct((m_rows, E_B), F32)],
        grid=(m_rows // tm,),
        in_specs=_both_streams_specs(x_ctx, x_lat, tm) + [
            pl.BlockSpec((None, 1, 3 * D_MODEL),
                         _both_streams_mod_map(layer, n_ctx_tiles, tm, lat_rows_per_batch)),
            _weight_spec(w_in, j)],
        out_specs=[pl.BlockSpec((tm, nq), lambda m: (m, 0)),
                   pl.BlockSpec((nq, tm), lambda m: (0, m)),
                   pl.BlockSpec((tm, E_B), lambda m: (m, 0)),
                   pl.BlockSpec((tm, E_B), lambda m: (m, 0))],
        compiler_params=_params(1),
        name="ret_in",
    )(x_ctx, x_lat, mod3, w_in)


def _retention_kernel(*refs, has_state, heads, fused_out):
    refs = list(refs)
    q_ref, kt_ref, v_ref, g_ref, af_ref, ab_ref = refs[:6]
    del refs[:6]
    if has_state:
        s0f_ref, s0b_ref = refs[:2]
        del refs[:2]
    if fused_out:
        x_ref, mod_ref, w_ref, lng_ref, lnb_ref = refs[:5]
        del refs[:5]
        o_ref = refs.pop(0)
        y_ref = refs.pop()
    else:
        y_ref = refs.pop(0)
    if not has_state:
        sf_ref, sb_ref = refs
    t = q_ref.shape[0]
    nc = t // CHUNK
    row = lax.broadcasted_iota(jnp.int32, (CHUNK, CHUNK), 0).astype(F32)
    col = lax.broadcasted_iota(jnp.int32, (CHUNK, CHUNK), 1).astype(F32)
    diff = row - col
    idx_col = lax.broadcasted_iota(jnp.int32, (CHUNK, 1), 0).astype(F32)
    idx_row = lax.broadcasted_iota(jnp.int32, (1, CHUNK), 1).astype(F32)

    def chunk(c):
        return slice(c * CHUNK, (c + 1) * CHUNK)

    for hh in range(heads):
        qk_cols = slice(hh * DK_B, (hh + 1) * DK_B)
        v_cols = slice(hh * DV_B, (hh + 1) * DV_B)
        lg_f = jnp.log1p(-jnp.exp(af_ref[hh]))
        lg_b = jnp.log1p(-jnp.exp(ab_ref[hh]))
        dmask = (jnp.where(diff >= 0, jnp.exp(jnp.maximum(diff, 0.0) * lg_f), 0.0)
                 + jnp.where(diff <= 0, jnp.exp(jnp.maximum(-diff, 0.0) * lg_b), 0.0))
        qd_f = jnp.exp((idx_col + 1.0) * lg_f)
        qd_b = jnp.exp((CHUNK - idx_col) * lg_b)
        kd_f = jnp.exp((CHUNK - 1.0 - idx_row) * lg_f)
        kd_b = jnp.exp(idx_row * lg_b)
        cd_f = jnp.exp(CHUNK * lg_f)
        cd_b = jnp.exp(CHUNK * lg_b)

        def states(order, kd, cd, s):
            seen = {}
            for n, c in enumerate(order):
                seen[c] = None if s is None else s.astype(BF16)
                if has_state and n == nc - 1:
                    return seen, None
                u = jnp.dot((kt_ref[qk_cols, chunk(c)] * kd).astype(BF16), v_ref[chunk(c), v_cols],
                            preferred_element_type=F32)
                s = u if s is None else s * cd + u
            return seen, s

        seen_f, s_f = states(range(nc), kd_f, cd_f, s0f_ref[hh] if has_state else None)
        seen_b, s_b = states(range(nc - 1, -1, -1), kd_b, cd_b, s0b_ref[hh] if has_state else None)
        if not has_state:
            sf_ref[hh] = s_f
            sb_ref[hh] = s_b
        for c in range(nc):
            qc = q_ref[chunk(c), qk_cols]
            qk = jnp.dot(qc, kt_ref[qk_cols, chunk(c)].astype(BF16), preferred_element_type=F32)
            o = jnp.dot((qk * dmask).astype(BF16), v_ref[chunk(c), v_cols],
                        preferred_element_type=F32)
            if seen_f[c] is not None:
                o = o + jnp.dot(qc, seen_f[c], preferred_element_type=F32) * qd_f
            if seen_b[c] is not None:
                o = o + jnp.dot(qc, seen_b[c], preferred_element_type=F32) * qd_b
            o = o * lax.rsqrt(jnp.mean(o * o, axis=-1, keepdims=True) + RMS_EPS)
            y_ref[chunk(c), v_cols] = (o * _silu(g_ref[chunk(c), v_cols])).astype(BF16)
    if fused_out:
        out = jnp.dot(y_ref[...], w_ref[...].astype(BF16), preferred_element_type=F32)
        gate = mod_ref[:, 2 * D_MODEL:3 * D_MODEL]
        o_ref[...] = _residual_layer_norm(x_ref[...], out, gate, lng_ref[...], lnb_ref[...])


def _retention(q, kt, v, g, seq_len, n_seq, first_row, heads, decay_f, decay_b, states,
               out_proj=None):
    t = seq_len
    b = n_seq
    s0 = first_row // t
    has_state = states is not None
    fused_out = out_proj is not None
    assert not fused_out or heads == H_B
    q_spec = pl.BlockSpec((t, heads * DK_B), lambda i, h: (s0 + i, h))
    kt_spec = pl.BlockSpec((heads * DK_B, t), lambda i, h: (h, s0 + i))
    vg_spec = pl.BlockSpec((t, heads * DV_B), lambda i, h: (s0 + i, h))
    a_spec = pl.BlockSpec((heads, 1, 1), lambda i, h: (h, 0, 0))
    in_specs = [q_spec, kt_spec, vg_spec, vg_spec, a_spec, a_spec]
    args = [q, kt, v, g, decay_f.reshape(H_B, 1, 1), decay_b.reshape(H_B, 1, 1)]
    out_shape = [jax.ShapeDtypeStruct((b * t, E_B), BF16)]
    out_specs = [pl.BlockSpec((t, heads * DV_B), lambda i, h: (i, h))]
    if has_state:
        s_f, s_b, jj = states
        s_spec = pl.BlockSpec((None, None, heads, DK_B, DV_B), lambda i, h: (i, jj, h, 0, 0))
        in_specs += [s_spec, s_spec]
        args += [s_f, s_b]
    scratch = []
    if fused_out:
        x2d, mod3, mod_map, w_out, jw, ln_g, ln_b = out_proj
        x_spec = pl.BlockSpec((t, D_MODEL), lambda i, h: (i, 0))
        const2 = lambda i, h: (0, 0)
        in_specs += [x_spec, pl.BlockSpec((None, 1, 3 * D_MODEL), mod_map), _weight_spec(w_out, jw),
                     pl.BlockSpec((1, D_MODEL), const2), pl.BlockSpec((1, D_MODEL), const2)]
        args += [x2d, mod3, w_out, ln_g.reshape(1, D_MODEL), ln_b.reshape(1, D_MODEL)]
        out_shape = [jax.ShapeDtypeStruct((b * t, D_MODEL), F32)]
        out_specs = [x_spec]
        scratch = [pltpu.VMEM((t, E_B), BF16)]
    if not has_state:
        so_spec = pl.BlockSpec((None, None, heads, DK_B, DV_B), lambda i, h: (i, 0, h, 0, 0))
        out_shape += [jax.ShapeDtypeStruct((b, 1, H_B, DK_B, DV_B), F32)] * 2
        out_specs += [so_spec, so_spec]
    return pl.pallas_call(
        functools.partial(_retention_kernel, has_state=has_state, heads=heads,
                          fused_out=fused_out),
        out_shape=out_shape,
        grid=(b, H_B // heads),
        in_specs=in_specs,
        out_specs=out_specs,
        scratch_shapes=scratch,
        compiler_params=_params(2),
        name="retention_state" if has_state else "retention",
    )(*args)


def _out_proj_kernel(y_ref, w_ref, x_ref, mod_ref, g_ref, b_ref, o_ref):
    out = jnp.dot(y_ref[...], w_ref[...].astype(BF16), preferred_element_type=F32)
    gate = mod_ref[:, 2 * D_MODEL:3 * D_MODEL]
    o_ref[...] = _residual_layer_norm(x_ref[...], out, gate, g_ref[...], b_ref[...])


def _out_proj(y2d, w_out, j, x2d, mod3, ln_g, ln_b, layer, rows_per_batch):
    m_rows, e = y2d.shape
    tm = ROW_TILE
    const2 = lambda m: (0, 0)
    return pl.pallas_call(
        _out_proj_kernel,
        out_shape=jax.ShapeDtypeStruct((m_rows, D_MODEL), F32),
        grid=(m_rows // tm,),
        in_specs=[pl.BlockSpec((tm, e), lambda m: (m, 0)),
                  _weight_spec(w_out, j),
                  pl.BlockSpec((tm, D_MODEL), lambda m: (m, 0)),
                  pl.BlockSpec((None, 1, 3 * D_MODEL), _mod_row_map(layer, rows_per_batch, tm)),
                  pl.BlockSpec((1, D_MODEL), const2),
                  pl.BlockSpec((1, D_MODEL), const2)],
        out_specs=pl.BlockSpec((tm, D_MODEL), lambda m: (m, 0)),
        compiler_params=_params(1),
        name="out_proj",
    )(y2d, w_out, x2d, mod3, ln_g.reshape(1, D_MODEL), ln_b.reshape(1, D_MODEL))


CONV_ROWS = 1024
CONV_COLS = 256


def _conv_kernel(xc_ref, xl_ref, mod_ref, w_in_ref, cw_ref, w_out_ref, g_ref, b_ref, o_ref, *,
                 n_ctx_tiles, ctx_seq_len, lat_seq_len):
    x = _both_streams_rows(xc_ref, xl_ref, n_ctx_tiles)
    h = _modulated(x, mod_ref)
    rows = x.shape[0]
    is_ctx = pl.program_id(0) < n_ctx_tiles
    row = lax.broadcasted_iota(jnp.int32, (rows, 1), 0)
    pos = jnp.where(is_ctx, row % ctx_seq_len, row % lat_seq_len)
    has_prev = pos > 0
    has_next = pos < jnp.where(is_ctx, ctx_seq_len - 1, lat_seq_len - 1)
    e = D_MODEL
    for c in range(e // CONV_COLS):
        cols = slice(c * CONV_COLS, (c + 1) * CONV_COLS)

        def proj(part):
            lo = part * e + c * CONV_COLS
            return jnp.dot(h, w_in_ref[:, lo:lo + CONV_COLS], preferred_element_type=F32)

        p = proj(1) * proj(2)
        prev = jnp.where(has_prev, pltpu.roll(p, 1, 0), 0.0)
        nxt = jnp.where(has_next, pltpu.roll(p, rows - 1, 0), 0.0)
        conv = prev * cw_ref[0:1, cols] + p * cw_ref[1:2, cols] + nxt * cw_ref[2:3, cols]
        y = (proj(0) * conv * _silu(proj(3))).astype(BF16)
        part = jnp.dot(y, w_out_ref[cols, :].astype(BF16), preferred_element_type=F32)
        if c == 0:
            o_ref[...] = part
        else:
            o_ref[...] += part
    gate = mod_ref[:, 2 * D_MODEL:3 * D_MODEL]
    o_ref[...] = _residual_layer_norm(x, o_ref[...], gate, g_ref[...], b_ref[...])


def _conv_layer(x_ctx, x_lat, mod3, w_in, conv_w, w_out, j, ln_g, ln_b, layer, ctx_seq_len,
                lat_seq_len):
    m_rows = x_ctx.shape[0] + x_lat.shape[0]
    tm = CONV_ROWS
    n_ctx_tiles = x_ctx.shape[0] // tm
    const2 = lambda m: (0, 0)
    return pl.pallas_call(
        functools.partial(_conv_kernel, n_ctx_tiles=n_ctx_tiles, ctx_seq_len=ctx_seq_len,
                          lat_seq_len=lat_seq_len),
        out_shape=jax.ShapeDtypeStruct((m_rows, D_MODEL), F32),
        grid=(m_rows // tm,),
        in_specs=_both_streams_specs(x_ctx, x_lat, tm) + [
                  pl.BlockSpec((None, 1, 3 * D_MODEL),
                               _both_streams_mod_map(layer, n_ctx_tiles, tm, lat_seq_len)),
                  _weight_spec(w_in, j),
                  pl.BlockSpec((None, 3, D_MODEL), lambda m: (j, 0, 0)),
                  _weight_spec(w_out, j),
                  pl.BlockSpec((1, D_MODEL), const2),
                  pl.BlockSpec((1, D_MODEL), const2)],
        out_specs=pl.BlockSpec((tm, D_MODEL), lambda m: (m, 0)),
        compiler_params=_params(1),
        name="conv_layer",
    )(x_ctx, x_lat, mod3, w_in, conv_w, w_out, ln_g.reshape(1, D_MODEL), ln_b.reshape(1, D_MODEL))


def _separate(xp, xs, m_ctx):
    if xp[0] is xs[0]:
        return (xp[0][:m_ctx], 0), (xs[0][m_ctx:], 0)
    return xp, xs


def kernel(x_prompt, x_sample, cache_k, cache_v, state_fwd, state_bwd, c, c_ctx, w_mod, b_mod, ln_g,
           ln_b, w_in_a, lam_a, subln_a, w_out_a, w_in_b, decay_fwd, decay_bwd, w_out_b, w_in_c,
           conv_c, w_out_c):
    bp, tp, d = x_prompt.shape
    bs, ts, _ = x_sample.shape
    cvec = jnp.concatenate([c, c_ctx[None], jnp.zeros((MOD_ROWS - bs - 1, d), F32)], axis=0)
    mod3 = _modulation(cvec, w_mod, b_mod)
    rope_tables = _rope_tables(ts)

    m_ctx, m_lat = bp * tp, bs * ts
    xp = (x_prompt.reshape(m_ctx, d), 0)
    xs = (x_sample.reshape(m_lat, d), 0)
    n_attn = (DEPTH + N_MIXERS - 1) // N_MIXERS
    new_cache_k = new_cache_v = jax.ShapeDtypeStruct((bp, n_attn, tp, H_A, DV_A), F32)
    new_sf, new_sb = [], []
    for i in range(DEPTH):
        kind, j = i % N_MIXERS, i // N_MIXERS
        if kind == 0:
            q, k, vt, z, new_cache_k, new_cache_v = _attn_in(
                xp, m_ctx, mod3, w_in_a, j, i, None, None, (new_cache_k, new_cache_v))
            xp = (_attention(q, k, vt, z, xp, tp, None, mod3, lam_a[j], subln_a[j], w_out_a, j,
                             ln_g[i], ln_b[i], i, False), 0)
            q, k, vt, z = _attn_in(xs, m_lat, mod3, w_in_a, j, i, ts, rope_tables, None)
            xs = (_attention(q, k, vt, z, xs, ts, (cache_k, cache_v, j), mod3, lam_a[j],
                             subln_a[j], w_out_a, j, ln_g[i], ln_b[i], i, True), 0)
        else:
            xp, xs = _separate(xp, xs, m_ctx)
            if kind == 1:
                q, kt, v, g = _ret_in(xp[0], xs[0], mod3, w_in_b, j, i, ts)
                x_ctx, s_f, s_b = _retention(
                    q, kt, v, g, tp, bp, 0, H_B, decay_fwd[j], decay_bwd[j], None,
                    out_proj=(xp[0], mod3, _mod_row_map(i, None, tp), w_out_b, j, ln_g[i],
                              ln_b[i]))
                new_sf.append(s_f)
                new_sb.append(s_b)
                (y,) = _retention(q, kt, v, g, ts, bs, m_ctx, 1, decay_fwd[j], decay_bwd[j],
                                  (state_fwd, state_bwd, j))
                xp = (x_ctx, 0)
                xs = (_out_proj(y, w_out_b, j, xs[0], mod3, ln_g[i], ln_b[i], i, ts), 0)
            else:
                x_all = _conv_layer(xp[0], xs[0], mod3, w_in_c, conv_c, w_out_c, j, ln_g[i],
                                    ln_b[i], i, tp, ts)
                xp, xs = (x_all, 0), (x_all, m_ctx)
    xp, xs = _separate(xp, xs, m_ctx)
    y_prompt = xp[0].reshape(bp, tp, d)
    y_sample = xs[0].reshape(bs, ts, d)
    new_state_fwd = jnp.concatenate(new_sf, axis=1)
    new_state_bwd = jnp.concatenate(new_sb, axis=1)
    return (y_prompt, y_sample, new_cache_k, new_cache_v, new_state_fwd, new_state_bwd)
```

```python
import functools
import math

import jax
import jax.numpy as jnp
from jax import lax
from jax.experimental import pallas as pl
from jax.experimental.pallas import tpu as pltpu

F32 = jnp.float32
BF16 = jnp.bfloat16

D_MODEL = 1024
DEPTH = 4
N_MIXERS = 3
GRID_W = 64
H_A = 8
DH_A = 64
DV_A = 128
SCORE_SCALE = DH_A ** -0.5 * math.log2(math.e)
H_B = 4
DK_B = 256
DV_B = 512
E_B = H_B * DV_B
CHUNK = 256
ALPHA = (2.0 * DEPTH) ** 0.25
ROPE_BASE = 10000.0
LN_EPS = 1e-5
RMS_EPS = 1e-6

MOD_ROWS = 8
CTX_ROW = 4
VMEM_LIMIT_BYTES = 56 * 1024 * 1024
ROW_TILE = 512


def _params(n_axes):
    return pltpu.CompilerParams(dimension_semantics=("arbitrary",) * n_axes,
                                vmem_limit_bytes=VMEM_LIMIT_BYTES)


def _silu(x):
    return x * jax.nn.sigmoid(x)


def _residual_layer_norm(x, out, gate, g, b):
    r = ALPHA * x + gate * out
    mu = jnp.mean(r, axis=-1, keepdims=True)
    d = r - mu
    var = jnp.mean(d * d, axis=-1, keepdims=True)
    return d * lax.rsqrt(var + LN_EPS) * g + b


def _modulated(x, mod_ref):
    shift = mod_ref[:, 0:D_MODEL]
    scale = mod_ref[:, D_MODEL:2 * D_MODEL]
    return x * (1.0 + scale) + shift


def _mod_row_map(layer, rows_per_batch, tile):
    if rows_per_batch is None:
        return lambda m, *_: (layer * MOD_ROWS + CTX_ROW, 0, 0)
    return lambda m, *_: (layer * MOD_ROWS + (m * tile) // rows_per_batch, 0, 0)


def _both_streams_specs(x_ctx, x_lat, tm):
    n_ctx_tiles = x_ctx.shape[0] // tm
    return [pl.BlockSpec((tm, D_MODEL), lambda m: (jnp.minimum(m, n_ctx_tiles - 1), 0)),
            pl.BlockSpec((tm, D_MODEL), lambda m: (jnp.maximum(m - n_ctx_tiles, 0), 0))]


def _both_streams_rows(x_ctx_ref, x_lat_ref, n_ctx_tiles):
    return jnp.where(pl.program_id(0) < n_ctx_tiles, x_ctx_ref[...], x_lat_ref[...])


def _both_streams_mod_map(layer, n_ctx_tiles, tm, lat_rows_per_batch):
    def index_map(m):
        lat_row = ((m - n_ctx_tiles) * tm) // lat_rows_per_batch
        return (layer * MOD_ROWS + jnp.where(m < n_ctx_tiles, CTX_ROW, lat_row), 0, 0)
    return index_map


def _weight_spec(w, j):
    return pl.BlockSpec((None,) + w.shape[1:], lambda *_: (j, 0, 0), pipeline_mode=pl.Buffered(1))


WEIGHT_CHUNK_COLS = 1024
WEIGHT_COPIES_IN_FLIGHT = 2


class _StagedWeight:
    @staticmethod
    def scratch(w, chunk_cols):
        return [pltpu.VMEM(w.shape[1:], w.dtype),
                pltpu.SemaphoreType.DMA((w.shape[2] // chunk_cols,))]

    def __init__(self, w_hbm_ref, w_vmem_ref, sem_ref, j, chunk_cols):
        self.copies = [
            pltpu.make_async_copy(w_hbm_ref.at[j, :, pl.ds(c * chunk_cols, chunk_cols)],
                                  w_vmem_ref.at[:, pl.ds(c * chunk_cols, chunk_cols)], sem_ref.at[c])
            for c in range(w_vmem_ref.shape[1] // chunk_cols)]

    def _arrive(self, c):
        self.copies[c].wait()
        if c + WEIGHT_COPIES_IN_FLIGHT < len(self.copies):
            self.copies[c + WEIGHT_COPIES_IN_FLIGHT].start()

    def run(self, body):
        first = pl.program_id(0) == 0

        @pl.when(first)
        def _():
            for cp in self.copies[:WEIGHT_COPIES_IN_FLIGHT]:
                cp.start()
            body(self._arrive)

        @pl.when(jnp.logical_not(first))
        def _():
            body(lambda c: None)


def _mod_kernel(cv_ref, w_ref, b_ref, o_ref):
    s = _silu(cv_ref[...])
    o_ref[...] = jnp.dot(s, w_ref[...], preferred_element_type=F32) + b_ref[...]


def _modulation(cvec, w_mod, b_mod):
    tn = 1024
    n = 3 * D_MODEL
    out = pl.pallas_call(
        _mod_kernel,
        out_shape=jax.ShapeDtypeStruct((DEPTH, MOD_ROWS, n), F32),
        grid=(DEPTH, n // tn),
        in_specs=[pl.BlockSpec((MOD_ROWS, D_MODEL), lambda i, j: (0, 0)),
                  pl.BlockSpec((None, D_MODEL, tn), lambda i, j: (i, 0, j)),
                  pl.BlockSpec((None, 1, tn), lambda i, j: (i, 0, j))],
        out_specs=pl.BlockSpec((None, MOD_ROWS, tn), lambda i, j: (i, 0, j)),
        compiler_params=_params(2),
        name="modulation",
    )(cvec, w_mod, b_mod.reshape(DEPTH, 1, n))
    return out.reshape(DEPTH * MOD_ROWS, 1, n)


def _rope(xh, cos4, sin4, first_half):
    swapped = jnp.where(first_half, pltpu.roll(xh, 96, 1), pltpu.roll(xh, 32, 1))
    return xh * cos4 + swapped * sin4


def _store_heads(o_ref, x, seq_len):
    for b in range(o_ref.shape[0]):
        xb = x[b * seq_len:(b + 1) * seq_len].reshape(seq_len, H_A, DV_A)
        if len(o_ref.shape) == 4:
            o_ref[b] = xb
        else:
            o_ref[b, 0] = xb
            for s in range(1, o_ref.shape[1]):
                o_ref[b, s] = jnp.zeros_like(xb)


def _attn_in_kernel(*refs, rope, keep_f32, seq_len, n_aliased):
    if rope:
        x_ref, mod_ref, w_ref, cos_ref, sin_ref = refs[:5]
        outs = refs[5 + n_aliased:]
    else:
        x_ref, mod_ref, w_ref = refs[:3]
        outs = refs[3 + n_aliased:]
    q_ref, k_ref, vt_ref, z_ref = outs[:4]
    h = _modulated(x_ref[...], mod_ref)
    if rope:
        lane = lax.broadcasted_iota(jnp.int32, (1, DV_A), 1)
        first_half = (lane % (2 * 32)) < 32
        cos4 = cos_ref[...]
        sin4 = sin_ref[...]
    q_all = jnp.dot(h, w_ref[:, 0:D_MODEL], preferred_element_type=F32)
    k_all = jnp.dot(h, w_ref[:, D_MODEL:2 * D_MODEL], preferred_element_type=F32)
    if keep_f32:
        _store_heads(outs[4], k_all, seq_len)
    for hd in range(H_A):
        cols = slice(hd * DV_A, (hd + 1) * DV_A)
        q = q_all[:, cols]
        k = k_all[:, cols]
        if rope:
            q = _rope(q, cos4, sin4, first_half)
            k = _rope(k, cos4, sin4, first_half)
        q_ref[:, cols] = (q * SCORE_SCALE).astype(BF16)
        k_ref[:, cols] = k.astype(BF16)
    v = jnp.dot(h, w_ref[:, 2 * D_MODEL:3 * D_MODEL], preferred_element_type=F32)
    if keep_f32:
        _store_heads(outs[5], v, seq_len)
    vt_ref[...] = v.T.astype(BF16)
    z_ref[...] = jnp.dot(h, w_ref[:, 3 * D_MODEL:4 * D_MODEL], preferred_element_type=F32)


def _attn_in(x, m_rows, mod3, w_in, j, layer, rows_per_batch, rope_tables, cache_out):
    x2d, x_first_row = x
    tm = ROW_TILE
    rope = rope_tables is not None
    row_spec = pl.BlockSpec((tm, D_MODEL), lambda m: (m, 0))
    in_specs = [pl.BlockSpec((tm, D_MODEL), lambda m: (x_first_row // tm + m, 0)),
                pl.BlockSpec((None, 1, 3 * D_MODEL), _mod_row_map(layer, rows_per_batch, tm)),
                _weight_spec(w_in, j)]
    args = [x2d, mod3, w_in]
    if rope:
        seq_tiles = rows_per_batch // tm
        tab_spec = pl.BlockSpec((tm, DV_A), lambda m: (m % seq_tiles, 0))
        in_specs += [tab_spec, tab_spec]
        args += list(rope_tables)
    out_shape = [jax.ShapeDtypeStruct((m_rows, D_MODEL), BF16),
                 jax.ShapeDtypeStruct((m_rows, D_MODEL), BF16),
                 jax.ShapeDtypeStruct((D_MODEL, m_rows), BF16),
                 jax.ShapeDtypeStruct((m_rows, D_MODEL), F32)]
    out_specs = [row_spec, row_spec, pl.BlockSpec((D_MODEL, tm), lambda m: (0, m)), row_spec]
    aliases = {}
    seq_len = None
    if cache_out is not None:
        new_k, new_v = cache_out
        seq_len = new_k.shape[2]
        cache_shape = jax.ShapeDtypeStruct(new_k.shape, F32)
        if j == 0:
            cache_spec = pl.BlockSpec((tm // seq_len, new_k.shape[1], seq_len, H_A, DV_A),
                                      lambda m: (m, 0, 0, 0, 0))
        else:
            cache_spec = pl.BlockSpec((tm // seq_len, None, seq_len, H_A, DV_A),
                                      lambda m: (m, j, 0, 0, 0))
        out_shape += [cache_shape, cache_shape]
        out_specs += [cache_spec, cache_spec]
        if j > 0:
            in_specs += [pl.BlockSpec(memory_space=pl.ANY)] * 2
            args += [new_k, new_v]
            aliases = {len(args) - 2: 4, len(args) - 1: 5}
    return pl.pallas_call(
        functools.partial(_attn_in_kernel, rope=rope, keep_f32=cache_out is not None,
                          seq_len=seq_len, n_aliased=len(aliases)),
        out_shape=out_shape,
        grid=(m_rows // tm,),
        in_specs=in_specs,
        out_specs=out_specs,
        input_output_aliases=aliases,
        compiler_params=_params(1),
        name="attn_in_rope" if rope else "attn_in",
    )(*args)


ONES_ROWS = 16
ATTN_SUBBLOCKS = 1


def _slab_reduce(op, x):
    parts = [x[i:i + 8] for i in range(0, x.shape[0], 8)]
    while len(parts) > 1:
        parts = [op(parts[i], parts[i + 1]) for i in range(0, len(parts) - 1, 2)] + (
            [parts[-1]] if len(parts) % 2 else [])
    return parts[0]


def _attn_kernel(*refs, layer_idx, has_ctx, n_seq, shared_keys):
    if has_ctx:
        (q_ref, k_ref, vt_ref, z_ref, x_ref, kc_ref, vc_ref, mod_ref, lam_ref, subln_ref,
         w_ref, g_ref, b_ref, o_ref, y_ref, kcb_ref, vct_ref) = refs

        @pl.when(pl.program_id(1) == 0)
        def _():
            n_ctx = kc_ref.shape[0]
            kcb_ref[...] = kc_ref[...].reshape(n_ctx, D_MODEL).astype(BF16)
            vct_ref[...] = vc_ref[...].reshape(n_ctx, D_MODEL).T.astype(BF16)
    else:
        (q_ref, k_ref, vt_ref, z_ref, x_ref, mod_ref, lam_ref, subln_ref,
         w_ref, g_ref, b_ref, o_ref, y_ref) = refs
    tq = q_ref.shape[0] // n_seq
    lam_init = 0.8 - 0.6 * math.exp(-0.3 * layer_idx)
    lm = lam_ref[...]
    lam = (jnp.exp(jnp.sum(lm[0:1] * lm[1:2], axis=-1, keepdims=True))
           - jnp.exp(jnp.sum(lm[2:3] * lm[3:4], axis=-1, keepdims=True)) + lam_init)
    lane = lax.broadcasted_iota(jnp.int32, (1, DV_A), 1)
    first = lane < DH_A
    subln = jnp.broadcast_to(subln_ref[...], (DV_A, tq))
    nt = (((1,), (1,)), ((), ()))
    t = k_ref.shape[0] if shared_keys else k_ref.shape[0] // n_seq

    def key_rows(sq):
        return slice(0, t) if shared_keys else slice(sq * t, (sq + 1) * t)

    def scores(unit):
        sq, hd = unit
        cols = slice(hd * DV_A, (hd + 1) * DV_A)
        qh = q_ref[sq * tq:(sq + 1) * tq, cols]
        zero = jnp.zeros_like(qh)
        qq = jnp.concatenate([jnp.where(first, qh, zero), jnp.where(first, zero, qh)], axis=0)
        parts = [lax.dot_general(k_ref[key_rows(sq), cols], qq, nt, preferred_element_type=F32)]
        if has_ctx:
            parts.append(lax.dot_general(kcb_ref[:, cols], qq, nt, preferred_element_type=F32))
        return parts

    def exps(parts):
        m8 = functools.reduce(jnp.maximum, [_slab_reduce(jnp.maximum, s) for s in parts])
        m = jnp.max(m8, axis=0, keepdims=True)
        return [jnp.exp2(s - m).astype(BF16) for s in parts]

    def with_ones(vals_t):
        return jnp.concatenate([vals_t, jnp.ones((ONES_ROWS, vals_t.shape[1]), BF16)], axis=0)

    def finish(unit, es):
        sq, hd = unit
        cols = slice(hd * DV_A, (hd + 1) * DV_A)
        rows = slice(sq * tq, (sq + 1) * tq)
        acc = jnp.dot(with_ones(vt_ref[cols, key_rows(sq)]), es[0], preferred_element_type=F32)
        if has_ctx:
            acc = acc + jnp.dot(with_ones(vct_ref[cols, :]), es[1], preferred_element_type=F32)
        inv = 1.0 / acc[DV_A:DV_A + 1, :]
        ot = acc[:DV_A, :tq] * inv[:, :tq] - acc[:DV_A, tq:] * (inv[:, tq:] * lam)
        ot = ot * lax.rsqrt(jnp.mean(ot * ot, axis=0, keepdims=True) + RMS_EPS)
        ot = ot * subln * (1.0 - lam_init)
        y_ref[rows, cols] = (ot.T * _silu(z_ref[rows, cols])).astype(BF16)
        if hd == H_A - 1 and (sq == n_seq - 1 or not shared_keys):
            done = slice(0, n_seq * tq) if shared_keys else rows
            out = jnp.dot(y_ref[done, :], w_ref[...].astype(BF16), preferred_element_type=F32)
            gate = mod_ref[:, 2 * D_MODEL:3 * D_MODEL]
            o_ref[done, :] = _residual_layer_norm(x_ref[done, :], out, gate, g_ref[...], b_ref[...])

    units = [(sq, hd) for sq in range(n_seq) for hd in range(H_A)]
    s_ahead = {u: scores(units[u]) for u in range(min(2, len(units)))}
    e_ahead = {0: exps(s_ahead.pop(0))}
    for u in range(len(units)):
        if u + 2 < len(units):
            s_ahead[u + 2] = scores(units[u + 2])
        if u + 1 < len(units):
            e_ahead[u + 1] = exps(s_ahead.pop(u + 1))
        finish(units[u], e_ahead.pop(u))


def _attention(q, k, vt, z, x, seq_len, ctx, mod3, lam, subln, w_out, j, ln_g, ln_b, layer,
               per_batch_rows):
    m_rows = q.shape[0]
    t = seq_len
    b = m_rows // t
    has_ctx = ctx is not None
    if t <= 256 and not per_batch_rows and not has_ctx and b % 2 == 0:
        rows, n_seq, shared_keys, n_kseq = 2 * t, 2, False, 2
    else:
        rows, n_seq, shared_keys, n_kseq = 256, ATTN_SUBBLOCKS, True, 1
    nq = n_kseq * t // rows
    x, x_first_row = x
    x_spec = pl.BlockSpec((rows, D_MODEL), lambda i, j: (x_first_row // rows + i * nq + j, 0))
    q_spec = pl.BlockSpec((rows, D_MODEL), lambda i, j: (i * nq + j, 0))
    k_spec = pl.BlockSpec((n_kseq * t, D_MODEL), lambda i, j: (i, 0))
    vt_spec = pl.BlockSpec((D_MODEL, n_kseq * t), lambda i, j: (0, i))
    if per_batch_rows:
        mod_map = lambda i, j: (layer * MOD_ROWS + i, 0, 0)
    else:
        mod_map = lambda i, j: (layer * MOD_ROWS + CTX_ROW, 0, 0)
    const2 = lambda i, j: (0, 0)
    in_specs = [q_spec, k_spec, vt_spec, q_spec, x_spec]
    args = [q, k, vt, z, x]
    scratch = [pltpu.VMEM((rows, D_MODEL), BF16)]
    if has_ctx:
        cache_k, cache_v, jj = ctx
        n_ctx = cache_k.shape[2]
        scratch += [pltpu.VMEM((n_ctx, D_MODEL), BF16), pltpu.VMEM((D_MODEL, n_ctx), BF16)]
        c_spec = pl.BlockSpec((None, None, n_ctx, H_A, DV_A),
                              lambda i, j: (i, jj, 0, 0, 0))
        in_specs += [c_spec, c_spec]
        args += [cache_k, cache_v]
    in_specs += [pl.BlockSpec((None, 1, 3 * D_MODEL), mod_map),
                 pl.BlockSpec((4, DH_A), const2),
                 pl.BlockSpec((DV_A, 1), const2),
                 _weight_spec(w_out, j),
                 pl.BlockSpec((1, D_MODEL), const2),
                 pl.BlockSpec((1, D_MODEL), const2)]
    args += [mod3, lam, subln.reshape(DV_A, 1), w_out, ln_g.reshape(1, D_MODEL),
             ln_b.reshape(1, D_MODEL)]
    return pl.pallas_call(
        functools.partial(_attn_kernel, layer_idx=layer, has_ctx=has_ctx, n_seq=n_seq,
                          shared_keys=shared_keys),
        out_shape=jax.ShapeDtypeStruct((m_rows, D_MODEL), F32),
        grid=(b // n_kseq, nq),
        in_specs=in_specs,
        out_specs=q_spec,
        scratch_shapes=scratch,
        compiler_params=_params(2),
        name="diff_attn_ctx" if has_ctx else "diff_attn",
    )(*args)


def _rope_tables(n_tokens):
    rows = n_tokens // GRID_W
    r = jnp.repeat(jnp.arange(rows, dtype=F32), GRID_W)
    col = jnp.tile(jnp.arange(GRID_W, dtype=F32), rows)
    n_freq = DH_A // 4
    inv = ROPE_BASE ** (-jnp.arange(n_freq, dtype=F32) / n_freq)
    ang = jnp.concatenate([r[:, None] * inv, col[:, None] * inv], -1)
    cos, sin = jnp.cos(ang), jnp.sin(ang)
    return jnp.tile(cos, (1, 4)), jnp.concatenate([-sin, sin, -sin, sin], -1)


def _ret_in_kernel(xc_ref, xl_ref, mod_ref, w_hbm_ref, q_ref, kt_ref, v_ref, g_ref, w_ref, sem_ref, *,
                   n_ctx_tiles, j):
    nq = H_B * DK_B
    cc = WEIGHT_CHUNK_COLS
    assert nq == cc and E_B % cc == 0
    staged = _StagedWeight(w_hbm_ref, w_ref, sem_ref, j, cc)

    def body(arrive):
        h = _modulated(_both_streams_rows(xc_ref, xl_ref, n_ctx_tiles), mod_ref)
        arrive(0)
        q_ref[...] = jnp.dot(h, w_ref[:, 0:nq], preferred_element_type=F32).astype(BF16)
        arrive(1)
        k = jnp.dot(h, w_ref[:, nq:2 * nq], preferred_element_type=F32) * (DK_B ** -0.5)
        kt_ref[...] = k.T
        for c in range(E_B // cc):
            arrive(2 + c)
            v = jnp.dot(h, w_ref[:, 2 * nq + c * cc:2 * nq + (c + 1) * cc],
                        preferred_element_type=F32)
            v_ref[:, c * cc:(c + 1) * cc] = v.astype(BF16)
        for c in range(E_B // cc):
            arrive(2 + E_B // cc + c)
            g_ref[:, c * cc:(c + 1) * cc] = jnp.dot(
                h, w_ref[:, 2 * nq + E_B + c * cc:2 * nq + E_B + (c + 1) * cc],
                preferred_element_type=F32)

    staged.run(body)


def _ret_in(x_ctx, x_lat, mod3, w_in, j, layer, lat_rows_per_batch):
    m_rows = x_ctx.shape[0] + x_lat.shape[0]
    tm = ROW_TILE
    nq = H_B * DK_B
    n_ctx_tiles = x_ctx.shape[0] // tm
    return pl.pallas_call(
        functools.partial(_ret_in_kernel, n_ctx_tiles=n_ctx_tiles, j=j),
        out_shape=[jax.ShapeDtypeStruct((m_rows, nq), BF16),
                   jax.ShapeDtypeStruct((nq, m_rows), F32),
                   jax.ShapeDtypeStruct((m_rows, E_B), BF16),
                   jax.ShapeDtypeStruct((m_rows, E_B), F32)],
        grid=(m_rows // tm,),
        in_specs=_both_streams_specs(x_ctx, x_lat, tm) + [
            pl.BlockSpec((None, 1, 3 * D_MODEL),
                         _both_streams_mod_map(layer, n_ctx_tiles, tm, lat_rows_per_batch)),
            pl.BlockSpec(memory_space=pl.ANY)],
        out_specs=[pl.BlockSpec((tm, nq), lambda m: (m, 0)),
                   pl.BlockSpec((nq, tm), lambda m: (0, m)),
                   pl.BlockSpec((tm, E_B), lambda m: (m, 0)),
                   pl.BlockSpec((tm, E_B), lambda m: (m, 0))],
        scratch_shapes=_StagedWeight.scratch(w_in, WEIGHT_CHUNK_COLS),
        compiler_params=_params(1),
        name="ret_in",
    )(x_ctx, x_lat, mod3, w_in)


def _retention_kernel(*refs, has_state, heads, fused_out):
    refs = list(refs)
    q_ref, kt_ref, v_ref, g_ref, af_ref, ab_ref = refs[:6]
    del refs[:6]
    if has_state:
        s0f_ref, s0b_ref = refs[:2]
        del refs[:2]
    if fused_out:
        x_ref, mod_ref, w_ref, lng_ref, lnb_ref = refs[:5]
        del refs[:5]
        o_ref = refs.pop(0)
        y_ref = refs.pop()
    else:
        y_ref = refs.pop(0)
    if not has_state:
        sf_ref, sb_ref = refs
    t = q_ref.shape[0]
    nc = t // CHUNK
    row = lax.broadcasted_iota(jnp.int32, (CHUNK, CHUNK), 0).astype(F32)
    col = lax.broadcasted_iota(jnp.int32, (CHUNK, CHUNK), 1).astype(F32)
    diff = row - col
    idx_col = lax.broadcasted_iota(jnp.int32, (CHUNK, 1), 0).astype(F32)
    idx_row = lax.broadcasted_iota(jnp.int32, (1, CHUNK), 1).astype(F32)

    def chunk(c):
        return slice(c * CHUNK, (c + 1) * CHUNK)

    for hh in range(heads):
        qk_cols = slice(hh * DK_B, (hh + 1) * DK_B)
        v_cols = slice(hh * DV_B, (hh + 1) * DV_B)
        lg_f = jnp.log1p(-jnp.exp(af_ref[hh]))
        lg_b = jnp.log1p(-jnp.exp(ab_ref[hh]))
        dmask = (jnp.where(diff >= 0, jnp.exp(jnp.maximum(diff, 0.0) * lg_f), 0.0)
                 + jnp.where(diff <= 0, jnp.exp(jnp.maximum(-diff, 0.0) * lg_b), 0.0))
        qd_f = jnp.exp((idx_col + 1.0) * lg_f)
        qd_b = jnp.exp((CHUNK - idx_col) * lg_b)
        kd_f = jnp.exp((CHUNK - 1.0 - idx_row) * lg_f)
        kd_b = jnp.exp(idx_row * lg_b)
        cd_f = jnp.exp(CHUNK * lg_f)
        cd_b = jnp.exp(CHUNK * lg_b)

        def states(order, kd, cd, s):
            seen = {}
            for n, c in enumerate(order):
                seen[c] = None if s is None else s.astype(BF16)
                if has_state and n == nc - 1:
                    return seen, None
                u = jnp.dot((kt_ref[qk_cols, chunk(c)] * kd).astype(BF16), v_ref[chunk(c), v_cols],
                            preferred_element_type=F32)
                s = u if s is None else s * cd + u
            return seen, s

        seen_f, s_f = states(range(nc), kd_f, cd_f, s0f_ref[hh] if has_state else None)
        seen_b, s_b = states(range(nc - 1, -1, -1), kd_b, cd_b, s0b_ref[hh] if has_state else None)
        if not has_state:
            sf_ref[hh] = s_f
            sb_ref[hh] = s_b
        for c in range(nc):
            qc = q_ref[chunk(c), qk_cols]
            qk = jnp.dot(qc, kt_ref[qk_cols, chunk(c)].astype(BF16), preferred_element_type=F32)
            o = jnp.dot((qk * dmask).astype(BF16), v_ref[chunk(c), v_cols],
                        preferred_element_type=F32)
            if seen_f[c] is not None:
                o = o + jnp.dot(qc, seen_f[c], preferred_element_type=F32) * qd_f
            if seen_b[c] is not None:
                o = o + jnp.dot(qc, seen_b[c], preferred_element_type=F32) * qd_b
            o = o * lax.rsqrt(jnp.mean(o * o, axis=-1, keepdims=True) + RMS_EPS)
            y_ref[chunk(c), v_cols] = (o * _silu(g_ref[chunk(c), v_cols])).astype(BF16)
    if fused_out:
        out = jnp.dot(y_ref[...], w_ref[...].astype(BF16), preferred_element_type=F32)
        gate = mod_ref[:, 2 * D_MODEL:3 * D_MODEL]
        o_ref[...] = _residual_layer_norm(x_ref[...], out, gate, lng_ref[...], lnb_ref[...])


def _retention(q, kt, v, g, seq_len, n_seq, first_row, heads, decay_f, decay_b, states,
               out_proj=None):
    t = seq_len
    b = n_seq
    s0 = first_row // t
    has_state = states is not None
    fused_out = out_proj is not None
    assert not fused_out or heads == H_B
    q_spec = pl.BlockSpec((t, heads * DK_B), lambda i, h: (s0 + i, h))
    kt_spec = pl.BlockSpec((heads * DK_B, t), lambda i, h: (h, s0 + i))
    vg_spec = pl.BlockSpec((t, heads * DV_B), lambda i, h: (s0 + i, h))
    a_spec = pl.BlockSpec((heads, 1, 1), lambda i, h: (h, 0, 0))
    in_specs = [q_spec, kt_spec, vg_spec, vg_spec, a_spec, a_spec]
    args = [q, kt, v, g, decay_f.reshape(H_B, 1, 1), decay_b.reshape(H_B, 1, 1)]
    out_shape = [jax.ShapeDtypeStruct((b * t, E_B), BF16)]
    out_specs = [pl.BlockSpec((t, heads * DV_B), lambda i, h: (i, h))]
    if has_state:
        s_f, s_b, jj = states
        s_spec = pl.BlockSpec((None, None, heads, DK_B, DV_B), lambda i, h: (i, jj, h, 0, 0))
        in_specs += [s_spec, s_spec]
        args += [s_f, s_b]
    scratch = []
    if fused_out:
        x2d, mod3, mod_map, w_out, jw, ln_g, ln_b = out_proj
        x_spec = pl.BlockSpec((t, D_MODEL), lambda i, h: (i, 0))
        const2 = lambda i, h: (0, 0)
        in_specs += [x_spec, pl.BlockSpec((None, 1, 3 * D_MODEL), mod_map), _weight_spec(w_out, jw),
                     pl.BlockSpec((1, D_MODEL), const2), pl.BlockSpec((1, D_MODEL), const2)]
        args += [x2d, mod3, w_out, ln_g.reshape(1, D_MODEL), ln_b.reshape(1, D_MODEL)]
        out_shape = [jax.ShapeDtypeStruct((b * t, D_MODEL), F32)]
        out_specs = [x_spec]
        scratch = [pltpu.VMEM((t, E_B), BF16)]
    if not has_state:
        so_spec = pl.BlockSpec((None, None, heads, DK_B, DV_B), lambda i, h: (i, 0, h, 0, 0))
        out_shape += [jax.ShapeDtypeStruct((b, 1, H_B, DK_B, DV_B), F32)] * 2
        out_specs += [so_spec, so_spec]
    return pl.pallas_call(
        functools.partial(_retention_kernel, has_state=has_state, heads=heads,
                          fused_out=fused_out),
        out_shape=out_shape,
        grid=(b, H_B // heads),
        in_specs=in_specs,
        out_specs=out_specs,
        scratch_shapes=scratch,
        compiler_params=_params(2),
        name="retention_state" if has_state else "retention",
    )(*args)


def _out_proj_kernel(y_ref, w_ref, x_ref, mod_ref, g_ref, b_ref, o_ref):
    out = jnp.dot(y_ref[...], w_ref[...].astype(BF16), preferred_element_type=F32)
    gate = mod_ref[:, 2 * D_MODEL:3 * D_MODEL]
    o_ref[...] = _residual_layer_norm(x_ref[...], out, gate, g_ref[...], b_ref[...])


def _out_proj(y2d, w_out, j, x2d, mod3, ln_g, ln_b, layer, rows_per_batch):
    m_rows, e = y2d.shape
    tm = ROW_TILE
    const2 = lambda m: (0, 0)
    return pl.pallas_call(
        _out_proj_kernel,
        out_shape=jax.ShapeDtypeStruct((m_rows, D_MODEL), F32),
        grid=(m_rows // tm,),
        in_specs=[pl.BlockSpec((tm, e), lambda m: (m, 0)),
                  _weight_spec(w_out, j),
                  pl.BlockSpec((tm, D_MODEL), lambda m: (m, 0)),
                  pl.BlockSpec((None, 1, 3 * D_MODEL), _mod_row_map(layer, rows_per_batch, tm)),
                  pl.BlockSpec((1, D_MODEL), const2),
                  pl.BlockSpec((1, D_MODEL), const2)],
        out_specs=pl.BlockSpec((tm, D_MODEL), lambda m: (m, 0)),
        compiler_params=_params(1),
        name="out_proj",
    )(y2d, w_out, x2d, mod3, ln_g.reshape(1, D_MODEL), ln_b.reshape(1, D_MODEL))


CONV_ROWS = 1024
CONV_COLS = 256


def _conv_kernel(xc_ref, xl_ref, mod_ref, w_in_ref, cw_ref, w_out_ref, g_ref, b_ref, o_ref, *,
                 n_ctx_tiles, ctx_seq_len, lat_seq_len):
    x = _both_streams_rows(xc_ref, xl_ref, n_ctx_tiles)
    h = _modulated(x, mod_ref)
    rows = x.shape[0]
    is_ctx = pl.program_id(0) < n_ctx_tiles
    row = lax.broadcasted_iota(jnp.int32, (rows, 1), 0)
    pos = jnp.where(is_ctx, row % ctx_seq_len, row % lat_seq_len)
    has_prev = pos > 0
    has_next = pos < jnp.where(is_ctx, ctx_seq_len - 1, lat_seq_len - 1)
    e = D_MODEL
    for c in range(e // CONV_COLS):
        cols = slice(c * CONV_COLS, (c + 1) * CONV_COLS)

        def proj(part):
            lo = part * e + c * CONV_COLS
            return jnp.dot(h, w_in_ref[:, lo:lo + CONV_COLS], preferred_element_type=F32)

        p = proj(1) * proj(2)
        prev = jnp.where(has_prev, pltpu.roll(p, 1, 0), 0.0)
        nxt = jnp.where(has_next, pltpu.roll(p, rows - 1, 0), 0.0)
        conv = prev * cw_ref[0:1, cols] + p * cw_ref[1:2, cols] + nxt * cw_ref[2:3, cols]
        y = (proj(0) * conv * _silu(proj(3))).astype(BF16)
        part = jnp.dot(y, w_out_ref[cols, :].astype(BF16), preferred_element_type=F32)
        if c == 0:
            o_ref[...] = part
        else:
            o_ref[...] += part
    gate = mod_ref[:, 2 * D_MODEL:3 * D_MODEL]
    o_ref[...] = _residual_layer_norm(x, o_ref[...], gate, g_ref[...], b_ref[...])


def _conv_layer(x_ctx, x_lat, mod3, w_in, conv_w, w_out, j, ln_g, ln_b, layer, ctx_seq_len,
                lat_seq_len):
    m_rows = x_ctx.shape[0] + x_lat.shape[0]
    tm = CONV_ROWS
    n_ctx_tiles = x_ctx.shape[0] // tm
    const2 = lambda m: (0, 0)
    return pl.pallas_call(
        functools.partial(_conv_kernel, n_ctx_tiles=n_ctx_tiles, ctx_seq_len=ctx_seq_len,
                          lat_seq_len=lat_seq_len),
        out_shape=jax.ShapeDtypeStruct((m_rows, D_MODEL), F32),
        grid=(m_rows // tm,),
        in_specs=_both_streams_specs(x_ctx, x_lat, tm) + [
                  pl.BlockSpec((None, 1, 3 * D_MODEL),
                               _both_streams_mod_map(layer, n_ctx_tiles, tm, lat_seq_len)),
                  _weight_spec(w_in, j),
                  pl.BlockSpec((None, 3, D_MODEL), lambda m: (j, 0, 0)),
                  _weight_spec(w_out, j),
                  pl.BlockSpec((1, D_MODEL), const2),
                  pl.BlockSpec((1, D_MODEL), const2)],
        out_specs=pl.BlockSpec((tm, D_MODEL), lambda m: (m, 0)),
        compiler_params=_params(1),
        name="conv_layer",
    )(x_ctx, x_lat, mod3, w_in, conv_w, w_out, ln_g.reshape(1, D_MODEL), ln_b.reshape(1, D_MODEL))


def _separate(xp, xs, m_ctx):
    if xp[0] is xs[0]:
        return (xp[0][:m_ctx], 0), (xs[0][m_ctx:], 0)
    return xp, xs


def kernel(x_prompt, x_sample, cache_k, cache_v, state_fwd, state_bwd, c, c_ctx, w_mod, b_mod, ln_g,
           ln_b, w_in_a, lam_a, subln_a, w_out_a, w_in_b, decay_fwd, decay_bwd, w_out_b, w_in_c,
           conv_c, w_out_c):
    bp, tp, d = x_prompt.shape
    bs, ts, _ = x_sample.shape
    cvec = jnp.concatenate([c, c_ctx[None], jnp.zeros((MOD_ROWS - bs - 1, d), F32)], axis=0)
    mod3 = _modulation(cvec, w_mod, b_mod)
    rope_tables = _rope_tables(ts)

    m_ctx, m_lat = bp * tp, bs * ts
    xp = (x_prompt.reshape(m_ctx, d), 0)
    xs = (x_sample.reshape(m_lat, d), 0)
    n_attn = (DEPTH + N_MIXERS - 1) // N_MIXERS
    new_cache_k = new_cache_v = jax.ShapeDtypeStruct((bp, n_attn, tp, H_A, DV_A), F32)
    new_sf, new_sb = [], []
    for i in range(DEPTH):
        kind, j = i % N_MIXERS, i // N_MIXERS
        if kind == 0:
            q, k, vt, z, new_cache_k, new_cache_v = _attn_in(
                xp, m_ctx, mod3, w_in_a, j, i, None, None, (new_cache_k, new_cache_v))
            xp = (_attention(q, k, vt, z, xp, tp, None, mod3, lam_a[j], subln_a[j], w_out_a, j,
                             ln_g[i], ln_b[i], i, False), 0)
            q, k, vt, z = _attn_in(xs, m_lat, mod3, w_in_a, j, i, ts, rope_tables, None)
            xs = (_attention(q, k, vt, z, xs, ts, (cache_k, cache_v, j), mod3, lam_a[j],
                             subln_a[j], w_out_a, j, ln_g[i], ln_b[i], i, True), 0)
        else:
            xp, xs = _separate(xp, xs, m_ctx)
            if kind == 1:
                q, kt, v, g = _ret_in(xp[0], xs[0], mod3, w_in_b, j, i, ts)
                x_ctx, s_f, s_b = _retention(
                    q, kt, v, g, tp, bp, 0, H_B, decay_fwd[j], decay_bwd[j], None,
                    out_proj=(xp[0], mod3, _mod_row_map(i, None, tp), w_out_b, j, ln_g[i],
                              ln_b[i]))
                new_sf.append(s_f)
                new_sb.append(s_b)
                (y,) = _retention(q, kt, v, g, ts, bs, m_ctx, 1, decay_fwd[j], decay_bwd[j],
                                  (state_fwd, state_bwd, j))
                xp = (x_ctx, 0)
                xs = (_out_proj(y, w_out_b, j, xs[0], mod3, ln_g[i], ln_b[i], i, ts), 0)
            else:
                x_all = _conv_layer(xp[0], xs[0], mod3, w_in_c, conv_c, w_out_c, j, ln_g[i],
                                    ln_b[i], i, tp, ts)
                xp, xs = (x_all, 0), (x_all, m_ctx)
    xp, xs = _separate(xp, xs, m_ctx)
    y_prompt = xp[0].reshape(bp, tp, d)
    y_sample = xs[0].reshape(bs, ts, d)
    new_state_fwd = jnp.concatenate(new_sf, axis=1)
    new_state_bwd = jnp.concatenate(new_sb, axis=1)
    return (y_prompt, y_sample, new_cache_k, new_cache_v, new_state_fwd, new_state_bwd)
```

```python
import functools
import math

import jax
import jax.numpy as jnp
from jax import lax
from jax.experimental import pallas as pl
from jax.experimental.pallas import tpu as pltpu

F32 = jnp.float32
BF16 = jnp.bfloat16

D_MODEL = 1024
DEPTH = 4
N_MIXERS = 3
GRID_W = 64
H_A = 8
DH_A = 64
DV_A = 128
SCORE_SCALE = DH_A ** -0.5 * math.log2(math.e)
H_B = 4
DK_B = 256
DV_B = 512
E_B = H_B * DV_B
CHUNK = 256
ALPHA = (2.0 * DEPTH) ** 0.25
ROPE_BASE = 10000.0
LN_EPS = 1e-5
RMS_EPS = 1e-6

MOD_ROWS = 8
CTX_ROW = 4
VMEM_LIMIT_BYTES = 56 * 1024 * 1024
ROW_TILE = 512


def _params(n_axes):
    return pltpu.CompilerParams(dimension_semantics=("arbitrary",) * n_axes,
                                vmem_limit_bytes=VMEM_LIMIT_BYTES)


def _silu(x):
    return x * jax.nn.sigmoid(x)


def _residual_layer_norm(x, out, gate, g, b):
    r = ALPHA * x + gate * out
    mu = jnp.mean(r, axis=-1, keepdims=True)
    d = r - mu
    var = jnp.mean(d * d, axis=-1, keepdims=True)
    return d * lax.rsqrt(var + LN_EPS) * g + b


def _modulated(x, mod_ref):
    shift = mod_ref[:, 0:D_MODEL]
    scale = mod_ref[:, D_MODEL:2 * D_MODEL]
    return x * (1.0 + scale) + shift


def _mod_row_map(layer, rows_per_batch, tile):
    if rows_per_batch is None:
        return lambda m, *_: (layer * MOD_ROWS + CTX_ROW, 0, 0)
    return lambda m, *_: (layer * MOD_ROWS + (m * tile) // rows_per_batch, 0, 0)


def _both_streams_specs(x_ctx, x_lat, tm):
    n_ctx_tiles = x_ctx.shape[0] // tm
    return [pl.BlockSpec((tm, D_MODEL), lambda m: (jnp.minimum(m, n_ctx_tiles - 1), 0)),
            pl.BlockSpec((tm, D_MODEL), lambda m: (jnp.maximum(m - n_ctx_tiles, 0), 0))]


def _both_streams_rows(x_ctx_ref, x_lat_ref, n_ctx_tiles):
    return jnp.where(pl.program_id(0) < n_ctx_tiles, x_ctx_ref[...], x_lat_ref[...])


def _both_streams_mod_map(layer, n_ctx_tiles, tm, lat_rows_per_batch):
    def index_map(m):
        lat_row = ((m - n_ctx_tiles) * tm) // lat_rows_per_batch
        return (layer * MOD_ROWS + jnp.where(m < n_ctx_tiles, CTX_ROW, lat_row), 0, 0)
    return index_map


def _weight_spec(w, j):
    return pl.BlockSpec((None,) + w.shape[1:], lambda *_: (j, 0, 0), pipeline_mode=pl.Buffered(1))


WEIGHT_CHUNK_COLS = 1024
WEIGHT_COPIES_IN_FLIGHT = 2


class _StagedWeight:
    @staticmethod
    def scratch(w, chunk_cols):
        return [pltpu.VMEM(w.shape[1:], w.dtype),
                pltpu.SemaphoreType.DMA((w.shape[2] // chunk_cols,))]

    def __init__(self, w_hbm_ref, w_vmem_ref, sem_ref, j, chunk_cols):
        self.copies = [
            pltpu.make_async_copy(w_hbm_ref.at[j, :, pl.ds(c * chunk_cols, chunk_cols)],
                                  w_vmem_ref.at[:, pl.ds(c * chunk_cols, chunk_cols)], sem_ref.at[c])
            for c in range(w_vmem_ref.shape[1] // chunk_cols)]

    def _arrive(self, c):
        self.copies[c].wait()
        if c + WEIGHT_COPIES_IN_FLIGHT < len(self.copies):
            self.copies[c + WEIGHT_COPIES_IN_FLIGHT].start()

    def run(self, body):
        first = pl.program_id(0) == 0

        @pl.when(first)
        def _():
            for cp in self.copies[:WEIGHT_COPIES_IN_FLIGHT]:
                cp.start()
            body(self._arrive)

        @pl.when(jnp.logical_not(first))
        def _():
            body(lambda c: None)


def _mod_kernel(cv_ref, w_ref, b_ref, o_ref):
    s = _silu(cv_ref[...])
    o_ref[...] = jnp.dot(s, w_ref[...], preferred_element_type=F32) + b_ref[...]


def _modulation(cvec, w_mod, b_mod):
    tn = 1024
    n = 3 * D_MODEL
    out = pl.pallas_call(
        _mod_kernel,
        out_shape=jax.ShapeDtypeStruct((DEPTH, MOD_ROWS, n), F32),
        grid=(DEPTH, n // tn),
        in_specs=[pl.BlockSpec((MOD_ROWS, D_MODEL), lambda i, j: (0, 0)),
                  pl.BlockSpec((None, D_MODEL, tn), lambda i, j: (i, 0, j)),
                  pl.BlockSpec((None, 1, tn), lambda i, j: (i, 0, j))],
        out_specs=pl.BlockSpec((None, MOD_ROWS, tn), lambda i, j: (i, 0, j)),
        compiler_params=_params(2),
        name="modulation",
    )(cvec, w_mod, b_mod.reshape(DEPTH, 1, n))
    return out.reshape(DEPTH * MOD_ROWS, 1, n)


def _rope(xh, cos4, sin4, first_half):
    swapped = jnp.where(first_half, pltpu.roll(xh, 96, 1), pltpu.roll(xh, 32, 1))
    return xh * cos4 + swapped * sin4


def _store_heads(o_ref, x, seq_len):
    for b in range(o_ref.shape[0]):
        xb = x[b * seq_len:(b + 1) * seq_len].reshape(seq_len, H_A, DV_A)
        if len(o_ref.shape) == 4:
            o_ref[b] = xb
        else:
            o_ref[b, 0] = xb
            for s in range(1, o_ref.shape[1]):
                o_ref[b, s] = jnp.zeros_like(xb)


def _attn_in_kernel(*refs, rope, keep_f32, seq_len, n_aliased):
    if rope:
        x_ref, mod_ref, w_ref, cos_ref, sin_ref = refs[:5]
        outs = refs[5 + n_aliased:]
    else:
        x_ref, mod_ref, w_ref = refs[:3]
        outs = refs[3 + n_aliased:]
    q_ref, k_ref, vt_ref, z_ref = outs[:4]
    h = _modulated(x_ref[...], mod_ref)
    if rope:
        lane = lax.broadcasted_iota(jnp.int32, (1, DV_A), 1)
        first_half = (lane % (2 * 32)) < 32
        cos4 = cos_ref[...]
        sin4 = sin_ref[...]
    q_all = jnp.dot(h, w_ref[:, 0:D_MODEL], preferred_element_type=F32)
    k_all = jnp.dot(h, w_ref[:, D_MODEL:2 * D_MODEL], preferred_element_type=F32)
    if keep_f32:
        _store_heads(outs[4], k_all, seq_len)
    for hd in range(H_A):
        cols = slice(hd * DV_A, (hd + 1) * DV_A)
        q = q_all[:, cols]
        k = k_all[:, cols]
        if rope:
            q = _rope(q, cos4, sin4, first_half)
            k = _rope(k, cos4, sin4, first_half)
        q_ref[:, cols] = (q * SCORE_SCALE).astype(BF16)
        k_ref[:, cols] = k.astype(BF16)
    v = jnp.dot(h, w_ref[:, 2 * D_MODEL:3 * D_MODEL], preferred_element_type=F32)
    if keep_f32:
        _store_heads(outs[5], v, seq_len)
    vt_ref[...] = v.T.astype(BF16)
    z_ref[...] = jnp.dot(h, w_ref[:, 3 * D_MODEL:4 * D_MODEL], preferred_element_type=F32)


def _attn_in(x, m_rows, mod3, w_in, j, layer, rows_per_batch, rope_tables, cache_out):
    x2d, x_first_row = x
    tm = ROW_TILE
    rope = rope_tables is not None
    row_spec = pl.BlockSpec((tm, D_MODEL), lambda m: (m, 0))
    in_specs = [pl.BlockSpec((tm, D_MODEL), lambda m: (x_first_row // tm + m, 0)),
                pl.BlockSpec((None, 1, 3 * D_MODEL), _mod_row_map(layer, rows_per_batch, tm)),
                _weight_spec(w_in, j)]
    args = [x2d, mod3, w_in]
    if rope:
        seq_tiles = rows_per_batch // tm
        tab_spec = pl.BlockSpec((tm, DV_A), lambda m: (m % seq_tiles, 0))
        in_specs += [tab_spec, tab_spec]
        args += list(rope_tables)
    out_shape = [jax.ShapeDtypeStruct((m_rows, D_MODEL), BF16),
                 jax.ShapeDtypeStruct((m_rows, D_MODEL), BF16),
                 jax.ShapeDtypeStruct((D_MODEL, m_rows), BF16),
                 jax.ShapeDtypeStruct((m_rows, D_MODEL), F32)]
    out_specs = [row_spec, row_spec, pl.BlockSpec((D_MODEL, tm), lambda m: (0, m)), row_spec]
    aliases = {}
    seq_len = None
    if cache_out is not None:
        new_k, new_v = cache_out
        seq_len = new_k.shape[2]
        cache_shape = jax.ShapeDtypeStruct(new_k.shape, F32)
        if j == 0:
            cache_spec = pl.BlockSpec((tm // seq_len, new_k.shape[1], seq_len, H_A, DV_A),
                                      lambda m: (m, 0, 0, 0, 0))
        else:
            cache_spec = pl.BlockSpec((tm // seq_len, None, seq_len, H_A, DV_A),
                                      lambda m: (m, j, 0, 0, 0))
        out_shape += [cache_shape, cache_shape]
        out_specs += [cache_spec, cache_spec]
        if j > 0:
            in_specs += [pl.BlockSpec(memory_space=pl.ANY)] * 2
            args += [new_k, new_v]
            aliases = {len(args) - 2: 4, len(args) - 1: 5}
    return pl.pallas_call(
        functools.partial(_attn_in_kernel, rope=rope, keep_f32=cache_out is not None,
                          seq_len=seq_len, n_aliased=len(aliases)),
        out_shape=out_shape,
        grid=(m_rows // tm,),
        in_specs=in_specs,
        out_specs=out_specs,
        input_output_aliases=aliases,
        compiler_params=_params(1),
        name="attn_in_rope" if rope else "attn_in",
    )(*args)


ONES_ROWS = 16
ATTN_SUBBLOCKS = 1


def _slab_reduce(op, x):
    parts = [x[i:i + 8] for i in range(0, x.shape[0], 8)]
    while len(parts) > 1:
        parts = [op(parts[i], parts[i + 1]) for i in range(0, len(parts) - 1, 2)] + (
            [parts[-1]] if len(parts) % 2 else [])
    return parts[0]


def _attn_kernel(*refs, layer_idx, has_ctx, n_seq, shared_keys):
    if has_ctx:
        (q_ref, k_ref, vt_ref, z_ref, x_ref, kc_ref, vc_ref, mod_ref, lam_ref, subln_ref,
         w_ref, g_ref, b_ref, o_ref, y_ref, kcb_ref, vct_ref) = refs

        @pl.when(pl.program_id(1) == 0)
        def _():
            n_ctx = kc_ref.shape[0]
            kcb_ref[...] = kc_ref[...].reshape(n_ctx, D_MODEL).astype(BF16)
            vct_ref[...] = vc_ref[...].reshape(n_ctx, D_MODEL).T.astype(BF16)
    else:
        (q_ref, k_ref, vt_ref, z_ref, x_ref, mod_ref, lam_ref, subln_ref,
         w_ref, g_ref, b_ref, o_ref, y_ref) = refs
    tq = q_ref.shape[0] // n_seq
    lam_init = 0.8 - 0.6 * math.exp(-0.3 * layer_idx)
    lm = lam_ref[...]
    lam = (jnp.exp(jnp.sum(lm[0:1] * lm[1:2], axis=-1, keepdims=True))
           - jnp.exp(jnp.sum(lm[2:3] * lm[3:4], axis=-1, keepdims=True)) + lam_init)
    lane = lax.broadcasted_iota(jnp.int32, (1, DV_A), 1)
    first = lane < DH_A
    subln = jnp.broadcast_to(subln_ref[...], (DV_A, tq))
    nt = (((1,), (1,)), ((), ()))
    t = k_ref.shape[0] if shared_keys else k_ref.shape[0] // n_seq

    def key_rows(sq):
        return slice(0, t) if shared_keys else slice(sq * t, (sq + 1) * t)

    def scores(unit):
        sq, hd = unit
        cols = slice(hd * DV_A, (hd + 1) * DV_A)
        qh = q_ref[sq * tq:(sq + 1) * tq, cols]
        zero = jnp.zeros_like(qh)
        qq = jnp.concatenate([jnp.where(first, qh, zero), jnp.where(first, zero, qh)], axis=0)
        parts = [lax.dot_general(k_ref[key_rows(sq), cols], qq, nt, preferred_element_type=F32)]
        if has_ctx:
            parts.append(lax.dot_general(kcb_ref[:, cols], qq, nt, preferred_element_type=F32))
        return parts

    def exps(parts):
        m8 = functools.reduce(jnp.maximum, [_slab_reduce(jnp.maximum, s) for s in parts])
        m = jnp.max(m8, axis=0, keepdims=True)
        return [jnp.exp2(s - m).astype(BF16) for s in parts]

    def with_ones(vals_t):
        return jnp.concatenate([vals_t, jnp.ones((ONES_ROWS, vals_t.shape[1]), BF16)], axis=0)

    def finish(unit, es):
        sq, hd = unit
        cols = slice(hd * DV_A, (hd + 1) * DV_A)
        rows = slice(sq * tq, (sq + 1) * tq)
        acc = jnp.dot(with_ones(vt_ref[cols, key_rows(sq)]), es[0], preferred_element_type=F32)
        if has_ctx:
            acc = acc + jnp.dot(with_ones(vct_ref[cols, :]), es[1], preferred_element_type=F32)
        inv = 1.0 / acc[DV_A:DV_A + 1, :]
        ot = acc[:DV_A, :tq] * inv[:, :tq] - acc[:DV_A, tq:] * (inv[:, tq:] * lam)
        ot = ot * lax.rsqrt(jnp.mean(ot * ot, axis=0, keepdims=True) + RMS_EPS)
        ot = ot * subln * (1.0 - lam_init)
        y_ref[rows, cols] = (ot.T * _silu(z_ref[rows, cols])).astype(BF16)
        if hd == H_A - 1 and (sq == n_seq - 1 or not shared_keys):
            done = slice(0, n_seq * tq) if shared_keys else rows
            out = jnp.dot(y_ref[done, :], w_ref[...].astype(BF16), preferred_element_type=F32)
            gate = mod_ref[:, 2 * D_MODEL:3 * D_MODEL]
            o_ref[done, :] = _residual_layer_norm(x_ref[done, :], out, gate, g_ref[...], b_ref[...])

    units = [(sq, hd) for sq in range(n_seq) for hd in range(H_A)]
    s_ahead = {u: scores(units[u]) for u in range(min(2, len(units)))}
    e_ahead = {0: exps(s_ahead.pop(0))}
    for u in range(len(units)):
        if u + 2 < len(units):
            s_ahead[u + 2] = scores(units[u + 2])
        if u + 1 < len(units):
            e_ahead[u + 1] = exps(s_ahead.pop(u + 1))
        finish(units[u], e_ahead.pop(u))


def _attention(q, k, vt, z, x, seq_len, ctx, mod3, lam, subln, w_out, j, ln_g, ln_b, layer,
               per_batch_rows):
    m_rows = q.shape[0]
    t = seq_len
    b = m_rows // t
    has_ctx = ctx is not None
    if t <= 256 and not per_batch_rows and not has_ctx and b % 2 == 0:
        rows, n_seq, shared_keys, n_kseq = 2 * t, 2, False, 2
    else:
        rows, n_seq, shared_keys, n_kseq = 256, ATTN_SUBBLOCKS, True, 1
    nq = n_kseq * t // rows
    x, x_first_row = x
    x_spec = pl.BlockSpec((rows, D_MODEL), lambda i, j: (x_first_row // rows + i * nq + j, 0))
    q_spec = pl.BlockSpec((rows, D_MODEL), lambda i, j: (i * nq + j, 0))
    k_spec = pl.BlockSpec((n_kseq * t, D_MODEL), lambda i, j: (i, 0))
    vt_spec = pl.BlockSpec((D_MODEL, n_kseq * t), lambda i, j: (0, i))
    if per_batch_rows:
        mod_map = lambda i, j: (layer * MOD_ROWS + i, 0, 0)
    else:
        mod_map = lambda i, j: (layer * MOD_ROWS + CTX_ROW, 0, 0)
    const2 = lambda i, j: (0, 0)
    in_specs = [q_spec, k_spec, vt_spec, q_spec, x_spec]
    args = [q, k, vt, z, x]
    scratch = [pltpu.VMEM((rows, D_MODEL), BF16)]
    if has_ctx:
        cache_k, cache_v, jj = ctx
        n_ctx = cache_k.shape[2]
        scratch += [pltpu.VMEM((n_ctx, D_MODEL), BF16), pltpu.VMEM((D_MODEL, n_ctx), BF16)]
        c_spec = pl.BlockSpec((None, None, n_ctx, H_A, DV_A),
                              lambda i, j: (i, jj, 0, 0, 0))
        in_specs += [c_spec, c_spec]
        args += [cache_k, cache_v]
    in_specs += [pl.BlockSpec((None, 1, 3 * D_MODEL), mod_map),
                 pl.BlockSpec((4, DH_A), const2),
                 pl.BlockSpec((DV_A, 1), const2),
                 _weight_spec(w_out, j),
                 pl.BlockSpec((1, D_MODEL), const2),
                 pl.BlockSpec((1, D_MODEL), const2)]
    args += [mod3, lam, subln.reshape(DV_A, 1), w_out, ln_g.reshape(1, D_MODEL),
             ln_b.reshape(1, D_MODEL)]
    return pl.pallas_call(
        functools.partial(_attn_kernel, layer_idx=layer, has_ctx=has_ctx, n_seq=n_seq,
                          shared_keys=shared_keys),
        out_shape=jax.ShapeDtypeStruct((m_rows, D_MODEL), F32),
        grid=(b // n_kseq, nq),
        in_specs=in_specs,
        out_specs=q_spec,
        scratch_shapes=scratch,
        compiler_params=_params(2),
        name="diff_attn_ctx" if has_ctx else "diff_attn",
    )(*args)


def _rope_tables(n_tokens):
    rows = n_tokens // GRID_W
    r = jnp.repeat(jnp.arange(rows, dtype=F32), GRID_W)
    col = jnp.tile(jnp.arange(GRID_W, dtype=F32), rows)
    n_freq = DH_A // 4
    inv = ROPE_BASE ** (-jnp.arange(n_freq, dtype=F32) / n_freq)
    ang = jnp.concatenate([r[:, None] * inv, col[:, None] * inv], -1)
    cos, sin = jnp.cos(ang), jnp.sin(ang)
    return jnp.tile(cos, (1, 4)), jnp.concatenate([-sin, sin, -sin, sin], -1)


def _ret_in_kernel(xc_ref, xl_ref, mod_ref, w_hbm_ref, q_ref, kt_ref, v_ref, g_ref, w_ref, sem_ref, *,
                   n_ctx_tiles, j):
    nq = H_B * DK_B
    cc = WEIGHT_CHUNK_COLS
    assert nq == cc and E_B % cc == 0
    staged = _StagedWeight(w_hbm_ref, w_ref, sem_ref, j, cc)

    def body(arrive):
        h = _modulated(_both_streams_rows(xc_ref, xl_ref, n_ctx_tiles), mod_ref)
        arrive(0)
        q_ref[...] = jnp.dot(h, w_ref[:, 0:nq], preferred_element_type=F32).astype(BF16)
        arrive(1)
        k = jnp.dot(h, w_ref[:, nq:2 * nq], preferred_element_type=F32) * (DK_B ** -0.5)
        kt_ref[...] = k.T
        for c in range(E_B // cc):
            arrive(2 + c)
            v = jnp.dot(h, w_ref[:, 2 * nq + c * cc:2 * nq + (c + 1) * cc],
                        preferred_element_type=F32)
            v_ref[:, c * cc:(c + 1) * cc] = v.astype(BF16)
        for c in range(E_B // cc):
            arrive(2 + E_B // cc + c)
            g_ref[:, c * cc:(c + 1) * cc] = jnp.dot(
                h, w_ref[:, 2 * nq + E_B + c * cc:2 * nq + E_B + (c + 1) * cc],
                preferred_element_type=F32)

    staged.run(body)


def _ret_in(x_ctx, x_lat, mod3, w_in, j, layer, lat_rows_per_batch):
    m_rows = x_ctx.shape[0] + x_lat.shape[0]
    tm = ROW_TILE
    nq = H_B * DK_B
    n_ctx_tiles = x_ctx.shape[0] // tm
    return pl.pallas_call(
        functools.partial(_ret_in_kernel, n_ctx_tiles=n_ctx_tiles, j=j),
        out_shape=[jax.ShapeDtypeStruct((m_rows, nq), BF16),
                   jax.ShapeDtypeStruct((nq, m_rows), F32),
                   jax.ShapeDtypeStruct((m_rows, E_B), BF16),
                   jax.ShapeDtypeStruct((m_rows, E_B), F32)],
        grid=(m_rows // tm,),
        in_specs=_both_streams_specs(x_ctx, x_lat, tm) + [
            pl.BlockSpec((None, 1, 3 * D_MODEL),
                         _both_streams_mod_map(layer, n_ctx_tiles, tm, lat_rows_per_batch)),
            pl.BlockSpec(memory_space=pl.ANY)],
        out_specs=[pl.BlockSpec((tm, nq), lambda m: (m, 0)),
                   pl.BlockSpec((nq, tm), lambda m: (0, m)),
                   pl.BlockSpec((tm, E_B), lambda m: (m, 0)),
                   pl.BlockSpec((tm, E_B), lambda m: (m, 0))],
        scratch_shapes=_StagedWeight.scratch(w_in, WEIGHT_CHUNK_COLS),
        compiler_params=_params(1),
        name="ret_in",
    )(x_ctx, x_lat, mod3, w_in)


def _retention_kernel(*refs, has_state, heads, fused_out):
    refs = list(refs)
    q_ref, kt_ref, v_ref, g_ref, af_ref, ab_ref = refs[:6]
    del refs[:6]
    if has_state:
        s0f_ref, s0b_ref = refs[:2]
        del refs[:2]
    if fused_out:
        x_ref, mod_ref, w_ref, lng_ref, lnb_ref = refs[:5]
        del refs[:5]
        o_ref = refs.pop(0)
        y_ref = refs.pop()
    else:
        y_ref = refs.pop(0)
    if not has_state:
        sf_ref, sb_ref = refs
    t = q_ref.shape[0]
    nc = t // CHUNK
    row = lax.broadcasted_iota(jnp.int32, (CHUNK, CHUNK), 0).astype(F32)
    col = lax.broadcasted_iota(jnp.int32, (CHUNK, CHUNK), 1).astype(F32)
    diff = row - col
    idx_col = lax.broadcasted_iota(jnp.int32, (CHUNK, 1), 0).astype(F32)
    idx_row = lax.broadcasted_iota(jnp.int32, (1, CHUNK), 1).astype(F32)

    def chunk(c):
        return slice(c * CHUNK, (c + 1) * CHUNK)

    for hh in range(heads):
        qk_cols = slice(hh * DK_B, (hh + 1) * DK_B)
        v_cols = slice(hh * DV_B, (hh + 1) * DV_B)
        lg_f = jnp.log1p(-jnp.exp(af_ref[hh]))
        lg_b = jnp.log1p(-jnp.exp(ab_ref[hh]))
        dmask = (jnp.where(diff >= 0, jnp.exp(jnp.maximum(diff, 0.0) * lg_f), 0.0)
                 + jnp.where(diff <= 0, jnp.exp(jnp.maximum(-diff, 0.0) * lg_b), 0.0))
        qd_f = jnp.exp((idx_col + 1.0) * lg_f)
        qd_b = jnp.exp((CHUNK - idx_col) * lg_b)
        kd_f = jnp.exp((CHUNK - 1.0 - idx_row) * lg_f)
        kd_b = jnp.exp(idx_row * lg_b)
        cd_f = jnp.exp(CHUNK * lg_f)
        cd_b = jnp.exp(CHUNK * lg_b)

        def states(order, kd, cd, s):
            seen = {}
            for n, c in enumerate(order):
                seen[c] = None if s is None else s.astype(BF16)
                if has_state and n == nc - 1:
                    return seen, None
                u = jnp.dot((kt_ref[qk_cols, chunk(c)] * kd).astype(BF16), v_ref[chunk(c), v_cols],
                            preferred_element_type=F32)
                s = u if s is None else s * cd + u
            return seen, s

        seen_f, s_f = states(range(nc), kd_f, cd_f, s0f_ref[hh] if has_state else None)
        seen_b, s_b = states(range(nc - 1, -1, -1), kd_b, cd_b, s0b_ref[hh] if has_state else None)
        if not has_state:
            sf_ref[hh] = s_f
            sb_ref[hh] = s_b
        for c in range(nc):
            qc = q_ref[chunk(c), qk_cols]
            qk = jnp.dot(qc, kt_ref[qk_cols, chunk(c)].astype(BF16), preferred_element_type=F32)
            o = jnp.dot((qk * dmask).astype(BF16), v_ref[chunk(c), v_cols],
                        preferred_element_type=F32)
            if seen_f[c] is not None:
                o = o + jnp.dot(qc, seen_f[c], preferred_element_type=F32) * qd_f
            if seen_b[c] is not None:
                o = o + jnp.dot(qc, seen_b[c], preferred_element_type=F32) * qd_b
            o = o * lax.rsqrt(jnp.mean(o * o, axis=-1, keepdims=True) + RMS_EPS)
            y_ref[chunk(c), v_cols] = (o * _silu(g_ref[chunk(c), v_cols])).astype(BF16)
    if fused_out:
        part = jnp.dot(y_ref[...], w_ref[...].astype(BF16), preferred_element_type=F32)
        gate = mod_ref[:, 2 * D_MODEL:3 * D_MODEL]
        head_steps = H_B // heads
        if head_steps == 1:
            o_ref[...] = _residual_layer_norm(x_ref[...], part, gate, lng_ref[...], lnb_ref[...])
        else:
            step = pl.program_id(1)

            @pl.when(step == 0)
            def _():
                o_ref[...] = part

            @pl.when(jnp.logical_and(step > 0, step < head_steps - 1))
            def _():
                o_ref[...] += part

            @pl.when(step == head_steps - 1)
            def _():
                o_ref[...] = _residual_layer_norm(x_ref[...], o_ref[...] + part, gate, lng_ref[...],
                                                  lnb_ref[...])


def _retention(q, kt, v, g, seq_len, n_seq, first_row, heads, decay_f, decay_b, states,
               out_proj=None):
    t = seq_len
    b = n_seq
    s0 = first_row // t
    has_state = states is not None
    fused_out = out_proj is not None
    q_spec = pl.BlockSpec((t, heads * DK_B), lambda i, h: (s0 + i, h))
    kt_spec = pl.BlockSpec((heads * DK_B, t), lambda i, h: (h, s0 + i))
    vg_spec = pl.BlockSpec((t, heads * DV_B), lambda i, h: (s0 + i, h))
    a_spec = pl.BlockSpec((heads, 1, 1), lambda i, h: (h, 0, 0))
    in_specs = [q_spec, kt_spec, vg_spec, vg_spec, a_spec, a_spec]
    args = [q, kt, v, g, decay_f.reshape(H_B, 1, 1), decay_b.reshape(H_B, 1, 1)]
    out_shape = [jax.ShapeDtypeStruct((b * t, E_B), BF16)]
    out_specs = [pl.BlockSpec((t, heads * DV_B), lambda i, h: (i, h))]
    if has_state:
        s_f, s_b, jj = states
        s_spec = pl.BlockSpec((None, None, heads, DK_B, DV_B), lambda i, h: (i, jj, h, 0, 0))
        in_specs += [s_spec, s_spec]
        args += [s_f, s_b]
    scratch = []
    if fused_out:
        x2d, mod3, mod_map, w_out, jw, ln_g, ln_b = out_proj
        x_spec = pl.BlockSpec((t, D_MODEL), lambda i, h: (i, 0))
        const2 = lambda i, h: (0, 0)
        if heads == H_B:
            w_spec = _weight_spec(w_out, jw)
        else:
            w_spec = pl.BlockSpec((None, heads * DV_B, D_MODEL), lambda i, h: (jw, h, 0))
        in_specs += [x_spec, pl.BlockSpec((None, 1, 3 * D_MODEL), mod_map), w_spec,
                     pl.BlockSpec((1, D_MODEL), const2), pl.BlockSpec((1, D_MODEL), const2)]
        args += [x2d, mod3, w_out, ln_g.reshape(1, D_MODEL), ln_b.reshape(1, D_MODEL)]
        out_shape = [jax.ShapeDtypeStruct((b * t, D_MODEL), F32)]
        out_specs = [x_spec]
        scratch = [pltpu.VMEM((t, heads * DV_B), BF16)]
    if not has_state:
        so_spec = pl.BlockSpec((None, None, heads, DK_B, DV_B), lambda i, h: (i, 0, h, 0, 0))
        out_shape += [jax.ShapeDtypeStruct((b, 1, H_B, DK_B, DV_B), F32)] * 2
        out_specs += [so_spec, so_spec]
    return pl.pallas_call(
        functools.partial(_retention_kernel, has_state=has_state, heads=heads,
                          fused_out=fused_out),
        out_shape=out_shape,
        grid=(b, H_B // heads),
        in_specs=in_specs,
        out_specs=out_specs,
        scratch_shapes=scratch,
        compiler_params=_params(2),
        name="retention_state" if has_state else "retention",
    )(*args)


def _out_proj_kernel(y_ref, w_ref, x_ref, mod_ref, g_ref, b_ref, o_ref):
    out = jnp.dot(y_ref[...], w_ref[...].astype(BF16), preferred_element_type=F32)
    gate = mod_ref[:, 2 * D_MODEL:3 * D_MODEL]
    o_ref[...] = _residual_layer_norm(x_ref[...], out, gate, g_ref[...], b_ref[...])


def _out_proj(y2d, w_out, j, x2d, mod3, ln_g, ln_b, layer, rows_per_batch):
    m_rows, e = y2d.shape
    tm = ROW_TILE
    const2 = lambda m: (0, 0)
    return pl.pallas_call(
        _out_proj_kernel,
        out_shape=jax.ShapeDtypeStruct((m_rows, D_MODEL), F32),
        grid=(m_rows // tm,),
        in_specs=[pl.BlockSpec((tm, e), lambda m: (m, 0)),
                  _weight_spec(w_out, j),
                  pl.BlockSpec((tm, D_MODEL), lambda m: (m, 0)),
                  pl.BlockSpec((None, 1, 3 * D_MODEL), _mod_row_map(layer, rows_per_batch, tm)),
                  pl.BlockSpec((1, D_MODEL), const2),
                  pl.BlockSpec((1, D_MODEL), const2)],
        out_specs=pl.BlockSpec((tm, D_MODEL), lambda m: (m, 0)),
        compiler_params=_params(1),
        name="out_proj",
    )(y2d, w_out, x2d, mod3, ln_g.reshape(1, D_MODEL), ln_b.reshape(1, D_MODEL))


CONV_ROWS = 1024
CONV_COLS = 256


def _conv_kernel(xc_ref, xl_ref, mod_ref, w_in_ref, cw_ref, w_out_ref, g_ref, b_ref, o_ref, *,
                 n_ctx_tiles, ctx_seq_len, lat_seq_len):
    x = _both_streams_rows(xc_ref, xl_ref, n_ctx_tiles)
    h = _modulated(x, mod_ref)
    rows = x.shape[0]
    is_ctx = pl.program_id(0) < n_ctx_tiles
    row = lax.broadcasted_iota(jnp.int32, (rows, 1), 0)
    pos = jnp.where(is_ctx, row % ctx_seq_len, row % lat_seq_len)
    has_prev = pos > 0
    has_next = pos < jnp.where(is_ctx, ctx_seq_len - 1, lat_seq_len - 1)
    e = D_MODEL
    for c in range(e // CONV_COLS):
        cols = slice(c * CONV_COLS, (c + 1) * CONV_COLS)

        def proj(part):
            lo = part * e + c * CONV_COLS
            return jnp.dot(h, w_in_ref[:, lo:lo + CONV_COLS], preferred_element_type=F32)

        p = proj(1) * proj(2)
        prev = jnp.where(has_prev, pltpu.roll(p, 1, 0), 0.0)
        nxt = jnp.where(has_next, pltpu.roll(p, rows - 1, 0), 0.0)
        conv = prev * cw_ref[0:1, cols] + p * cw_ref[1:2, cols] + nxt * cw_ref[2:3, cols]
        y = (proj(0) * conv * _silu(proj(3))).astype(BF16)
        part = jnp.dot(y, w_out_ref[cols, :].astype(BF16), preferred_element_type=F32)
        if c == 0:
            o_ref[...] = part
        else:
            o_ref[...] += part
    gate = mod_ref[:, 2 * D_MODEL:3 * D_MODEL]
    o_ref[...] = _residual_layer_norm(x, o_ref[...], gate, g_ref[...], b_ref[...])


def _conv_layer(x_ctx, x_lat, mod3, w_in, conv_w, w_out, j, ln_g, ln_b, layer, ctx_seq_len,
                lat_seq_len):
    m_rows = x_ctx.shape[0] + x_lat.shape[0]
    tm = CONV_ROWS
    n_ctx_tiles = x_ctx.shape[0] // tm
    const2 = lambda m: (0, 0)
    return pl.pallas_call(
        functools.partial(_conv_kernel, n_ctx_tiles=n_ctx_tiles, ctx_seq_len=ctx_seq_len,
                          lat_seq_len=lat_seq_len),
        out_shape=jax.ShapeDtypeStruct((m_rows, D_MODEL), F32),
        grid=(m_rows // tm,),
        in_specs=_both_streams_specs(x_ctx, x_lat, tm) + [
                  pl.BlockSpec((None, 1, 3 * D_MODEL),
                               _both_streams_mod_map(layer, n_ctx_tiles, tm, lat_seq_len)),
                  _weight_spec(w_in, j),
                  pl.BlockSpec((None, 3, D_MODEL), lambda m: (j, 0, 0)),
                  _weight_spec(w_out, j),
                  pl.BlockSpec((1, D_MODEL), const2),
                  pl.BlockSpec((1, D_MODEL), const2)],
        out_specs=pl.BlockSpec((tm, D_MODEL), lambda m: (m, 0)),
        compiler_params=_params(1),
        name="conv_layer",
    )(x_ctx, x_lat, mod3, w_in, conv_w, w_out, ln_g.reshape(1, D_MODEL), ln_b.reshape(1, D_MODEL))


def _separate(xp, xs, m_ctx):
    if xp[0] is xs[0]:
        return (xp[0][:m_ctx], 0), (xs[0][m_ctx:], 0)
    return xp, xs


def kernel(x_prompt, x_sample, cache_k, cache_v, state_fwd, state_bwd, c, c_ctx, w_mod, b_mod, ln_g,
           ln_b, w_in_a, lam_a, subln_a, w_out_a, w_in_b, decay_fwd, decay_bwd, w_out_b, w_in_c,
           conv_c, w_out_c):
    bp, tp, d = x_prompt.shape
    bs, ts, _ = x_sample.shape
    cvec = jnp.concatenate([c, c_ctx[None], jnp.zeros((MOD_ROWS - bs - 1, d), F32)], axis=0)
    mod3 = _modulation(cvec, w_mod, b_mod)
    rope_tables = _rope_tables(ts)

    m_ctx, m_lat = bp * tp, bs * ts
    xp = (x_prompt.reshape(m_ctx, d), 0)
    xs = (x_sample.reshape(m_lat, d), 0)
    n_attn = (DEPTH + N_MIXERS - 1) // N_MIXERS
    new_cache_k = new_cache_v = jax.ShapeDtypeStruct((bp, n_attn, tp, H_A, DV_A), F32)
    new_sf, new_sb = [], []
    for i in range(DEPTH):
        kind, j = i % N_MIXERS, i // N_MIXERS
        if kind == 0:
            q, k, vt, z, new_cache_k, new_cache_v = _attn_in(
                xp, m_ctx, mod3, w_in_a, j, i, None, None, (new_cache_k, new_cache_v))
            xp = (_attention(q, k, vt, z, xp, tp, None, mod3, lam_a[j], subln_a[j], w_out_a, j,
                             ln_g[i], ln_b[i], i, False), 0)
            q, k, vt, z = _attn_in(xs, m_lat, mod3, w_in_a, j, i, ts, rope_tables, None)
            xs = (_attention(q, k, vt, z, xs, ts, (cache_k, cache_v, j), mod3, lam_a[j],
                             subln_a[j], w_out_a, j, ln_g[i], ln_b[i], i, True), 0)
        else:
            xp, xs = _separate(xp, xs, m_ctx)
            if kind == 1:
                q, kt, v, g = _ret_in(xp[0], xs[0], mod3, w_in_b, j, i, ts)
                x_ctx, s_f, s_b = _retention(
                    q, kt, v, g, tp, bp, 0, H_B, decay_fwd[j], decay_bwd[j], None,
                    out_proj=(xp[0], mod3, _mod_row_map(i, None, tp), w_out_b, j, ln_g[i],
                              ln_b[i]))
                new_sf.append(s_f)
                new_sb.append(s_b)
                (x_lat,) = _retention(
                    q, kt, v, g, ts, bs, m_ctx, 1, decay_fwd[j], decay_bwd[j],
                    (state_fwd, state_bwd, j),
                    out_proj=(xs[0], mod3, _mod_row_map(i, ts, ts), w_out_b, j, ln_g[i], ln_b[i]))
                xp, xs = (x_ctx, 0), (x_lat, 0)
            else:
                x_all = _conv_layer(xp[0], xs[0], mod3, w_in_c, conv_c, w_out_c, j, ln_g[i],
                                    ln_b[i], i, tp, ts)
                xp, xs = (x_all, 0), (x_all, m_ctx)
    xp, xs = _separate(xp, xs, m_ctx)
    y_prompt = xp[0].reshape(bp, tp, d)
    y_sample = xs[0].reshape(bs, ts, d)
    new_state_fwd = jnp.concatenate(new_sf, axis=1)
    new_state_bwd = jnp.concatenate(new_sb, axis=1)
    return (y_prompt, y_sample, new_cache_k, new_cache_v, new_state_fwd, new_state_bwd)
```

```python
import functools
import math

import jax
import jax.numpy as jnp
from jax import lax
from jax.experimental import pallas as pl
from jax.experimental.pallas import tpu as pltpu

F32 = jnp.float32
BF16 = jnp.bfloat16

D_MODEL = 1024
DEPTH = 4
N_MIXERS = 3
GRID_W = 64
H_A = 8
DH_A = 64
DV_A = 128
ROPE_HALF = DH_A // 2
SCORE_SCALE = DH_A ** -0.5 * math.log2(math.e)
H_B = 4
DK_B = 256
DV_B = 512
E_B = H_B * DV_B
CHUNK = 256
ALPHA = (2.0 * DEPTH) ** 0.25
ROPE_BASE = 10000.0
LN_EPS = 1e-5
RMS_EPS = 1e-6

MOD_ROWS = 8
CTX_ROW = 4
VMEM_LIMIT_BYTES = 58 * 1024 * 1024
ROW_TILE = 512


def _params(n_axes):
    return pltpu.CompilerParams(dimension_semantics=("arbitrary",) * n_axes,
                                vmem_limit_bytes=VMEM_LIMIT_BYTES)


def _silu(x):
    return x * jax.nn.sigmoid(x)


def _residual_layer_norm(x, out, gate, g, b):
    r = ALPHA * x + gate * out
    mu = jnp.mean(r, axis=-1, keepdims=True)
    d = r - mu
    var = jnp.mean(d * d, axis=-1, keepdims=True)
    return d * lax.rsqrt(var + LN_EPS) * g + b


def _modulated(x, mod_ref):
    shift = mod_ref[:, 0:D_MODEL]
    scale = mod_ref[:, D_MODEL:2 * D_MODEL]
    return x * (1.0 + scale) + shift


def _mod_row_map(layer, rows_per_batch, tile):
    if rows_per_batch is None:
        return lambda m, *_: (layer * MOD_ROWS + CTX_ROW, 0, 0)
    return lambda m, *_: (layer * MOD_ROWS + (m * tile) // rows_per_batch, 0, 0)


def _both_streams_specs(x_ctx, x_lat, tm):
    n_ctx_tiles = x_ctx.shape[0] // tm
    return [pl.BlockSpec((tm, D_MODEL), lambda m: (jnp.minimum(m, n_ctx_tiles - 1), 0)),
            pl.BlockSpec((tm, D_MODEL), lambda m: (jnp.maximum(m - n_ctx_tiles, 0), 0))]


def _both_streams_rows(x_ctx_ref, x_lat_ref, n_ctx_tiles):
    return jnp.where(pl.program_id(0) < n_ctx_tiles, x_ctx_ref[...], x_lat_ref[...])


def _both_streams_mod_map(layer, n_ctx_tiles, tm, lat_rows_per_batch):
    def index_map(m):
        lat_row = ((m - n_ctx_tiles) * tm) // lat_rows_per_batch
        return (layer * MOD_ROWS + jnp.where(m < n_ctx_tiles, CTX_ROW, lat_row), 0, 0)
    return index_map


def _weight_spec(w, j):
    return pl.BlockSpec((None,) + w.shape[1:], lambda *_: (j, 0, 0), pipeline_mode=pl.Buffered(1))


WEIGHT_CHUNK_COLS = 1024
WEIGHT_COPIES_IN_FLIGHT = 2


class _StagedWeight:
    @staticmethod
    def scratch(w, chunk_cols):
        return [pltpu.VMEM(w.shape[1:], w.dtype),
                pltpu.SemaphoreType.DMA((w.shape[2] // chunk_cols,))]

    def __init__(self, w_hbm_ref, w_vmem_ref, sem_ref, j, chunk_cols):
        self.copies = [
            pltpu.make_async_copy(w_hbm_ref.at[j, :, pl.ds(c * chunk_cols, chunk_cols)],
                                  w_vmem_ref.at[:, pl.ds(c * chunk_cols, chunk_cols)], sem_ref.at[c])
            for c in range(w_vmem_ref.shape[1] // chunk_cols)]

    def _arrive(self, c):
        self.copies[c].wait()
        if c + WEIGHT_COPIES_IN_FLIGHT < len(self.copies):
            self.copies[c + WEIGHT_COPIES_IN_FLIGHT].start()

    def run(self, body):
        first = pl.program_id(0) == 0

        @pl.when(first)
        def _():
            for cp in self.copies[:WEIGHT_COPIES_IN_FLIGHT]:
                cp.start()
            body(self._arrive)

        @pl.when(jnp.logical_not(first))
        def _():
            body(lambda c: None)


def _mod_kernel(cv_ref, w_ref, b_ref, o_ref):
    s = _silu(cv_ref[...])
    o_ref[...] = jnp.dot(s, w_ref[...], preferred_element_type=F32) + b_ref[...]


def _modulation(cvec, w_mod, b_mod):
    n = 3 * D_MODEL
    tn = n
    out = pl.pallas_call(
        _mod_kernel,
        out_shape=jax.ShapeDtypeStruct((DEPTH, MOD_ROWS, n), F32),
        grid=(DEPTH, n // tn),
        in_specs=[pl.BlockSpec((MOD_ROWS, D_MODEL), lambda i, j: (0, 0)),
                  pl.BlockSpec((None, D_MODEL, tn), lambda i, j: (i, 0, j)),
                  pl.BlockSpec((None, 1, tn), lambda i, j: (i, 0, j))],
        out_specs=pl.BlockSpec((None, MOD_ROWS, tn), lambda i, j: (i, 0, j)),
        compiler_params=_params(2),
        name="modulation",
    )(cvec, w_mod, b_mod.reshape(DEPTH, 1, n))
    return out.reshape(DEPTH * MOD_ROWS, 1, n)


def _rope(xh, cos4, sin4, first_half):
    swapped = jnp.where(first_half, pltpu.roll(xh, DV_A - ROPE_HALF, 1),
                        pltpu.roll(xh, ROPE_HALF, 1))
    return xh * cos4 + swapped * sin4


def _store_heads(o_ref, x, seq_len):
    for b in range(o_ref.shape[0]):
        xb = x[b * seq_len:(b + 1) * seq_len].reshape(seq_len, H_A, DV_A)
        if len(o_ref.shape) == 4:
            o_ref[b] = xb
        else:
            o_ref[b, 0] = xb
            for s in range(1, o_ref.shape[1]):
                o_ref[b, s] = jnp.zeros_like(xb)


def _attn_in_kernel(*refs, n_x, n_ctx_tiles, seq_len, n_aliased):
    mod_ref, w_ref, cos_ref, sin_ref = refs[n_x:n_x + 4]
    q_ref, k_ref, vt_ref, z_ref, ck_ref, cv_ref = refs[n_x + 4 + n_aliased:]
    is_ctx = pl.program_id(0) < n_ctx_tiles
    x = refs[0][...] if n_x == 1 else _both_streams_rows(refs[0], refs[1], n_ctx_tiles)
    h = _modulated(x, mod_ref)
    lane = lax.broadcasted_iota(jnp.int32, (1, DV_A), 1)
    first_half = (lane % DH_A) < ROPE_HALF
    cos4 = cos_ref[...]
    sin4 = sin_ref[...]
    q_all = jnp.dot(h, w_ref[:, 0:D_MODEL], preferred_element_type=F32)
    k_all = jnp.dot(h, w_ref[:, D_MODEL:2 * D_MODEL], preferred_element_type=F32)
    for hd in range(H_A):
        cols = slice(hd * DV_A, (hd + 1) * DV_A)
        q = q_all[:, cols]
        k = k_all[:, cols]
        q = jnp.where(is_ctx, q, _rope(q, cos4, sin4, first_half))
        k = jnp.where(is_ctx, k, _rope(k, cos4, sin4, first_half))
        q_ref[:, cols] = (q * SCORE_SCALE).astype(BF16)
        k_ref[:, cols] = k.astype(BF16)
    v = jnp.dot(h, w_ref[:, 2 * D_MODEL:3 * D_MODEL], preferred_element_type=F32)
    vt_ref[...] = v.T.astype(BF16)
    z_ref[...] = jnp.dot(h, w_ref[:, 3 * D_MODEL:4 * D_MODEL], preferred_element_type=F32)

    @pl.when(is_ctx)
    def _():
        _store_heads(ck_ref, k_all, seq_len)
        _store_heads(cv_ref, v, seq_len)


def _attn_in(x_ctx, x_lat, m_ctx, m_lat, mod3, w_in, j, layer, lat_rows_per_batch, rope_tables,
             cache_out):
    tm = ROW_TILE
    m_rows = m_ctx + m_lat
    n_ctx_tiles = m_ctx // tm
    seq_tiles = lat_rows_per_batch // tm
    row_spec = pl.BlockSpec((tm, D_MODEL), lambda m: (m, 0))
    if x_ctx[0] is x_lat[0]:
        assert x_ctx[1] == 0 and x_lat[1] == m_ctx
        x_specs, x_args = [row_spec], [x_ctx[0]]
    else:
        assert x_ctx[1] == 0 and x_lat[1] == 0
        x_specs, x_args = _both_streams_specs(x_ctx[0], x_lat[0], tm), [x_ctx[0], x_lat[0]]
    tab_spec = pl.BlockSpec(
        (tm, DV_A), lambda m: (jnp.where(m < n_ctx_tiles, 0, (m - n_ctx_tiles) % seq_tiles), 0))
    in_specs = x_specs + [
        pl.BlockSpec((None, 1, 3 * D_MODEL),
                     _both_streams_mod_map(layer, n_ctx_tiles, tm, lat_rows_per_batch)),
        _weight_spec(w_in, j), tab_spec, tab_spec]
    args = x_args + [mod3, w_in] + list(rope_tables)
    new_k, new_v = cache_out
    seq_len = new_k.shape[2]
    cache_shape = jax.ShapeDtypeStruct(new_k.shape, F32)
    ctx_block = lambda m: jnp.minimum(m, n_ctx_tiles - 1)
    aliases = {}
    if j == 0:
        cache_spec = pl.BlockSpec((tm // seq_len, new_k.shape[1], seq_len, H_A, DV_A),
                                  lambda m: (ctx_block(m), 0, 0, 0, 0))
    else:
        cache_spec = pl.BlockSpec((tm // seq_len, None, seq_len, H_A, DV_A),
                                  lambda m: (ctx_block(m), j, 0, 0, 0))
        in_specs += [pl.BlockSpec(memory_space=pl.ANY)] * 2
        args += [new_k, new_v]
        aliases = {len(args) - 2: 4, len(args) - 1: 5}
    return pl.pallas_call(
        functools.partial(_attn_in_kernel, n_x=len(x_args), n_ctx_tiles=n_ctx_tiles,
                          seq_len=seq_len, n_aliased=len(aliases)),
        out_shape=[jax.ShapeDtypeStruct((m_rows, D_MODEL), BF16),
                   jax.ShapeDtypeStruct((m_rows, D_MODEL), BF16),
                   jax.ShapeDtypeStruct((D_MODEL, m_rows), BF16),
                   jax.ShapeDtypeStruct((m_rows, D_MODEL), F32), cache_shape, cache_shape],
        grid=(m_rows // tm,),
        in_specs=in_specs,
        out_specs=[row_spec, row_spec, pl.BlockSpec((D_MODEL, tm), lambda m: (0, m)), row_spec,
                   cache_spec, cache_spec],
        input_output_aliases=aliases,
        compiler_params=_params(1),
        name="attn_in",
    )(*args)


ONES_ROWS = 16
ATTN_ROWS = 256
ATTN_SUBBLOCKS = 1


def _slab_reduce(op, x):
    parts = [x[i:i + 8] for i in range(0, x.shape[0], 8)]
    while len(parts) > 1:
        parts = [op(parts[i], parts[i + 1]) for i in range(0, len(parts) - 1, 2)] + (
            [parts[-1]] if len(parts) % 2 else [])
    return parts[0]


def _attn_kernel(*refs, layer_idx, has_ctx, n_seq, shared_keys):
    if has_ctx:
        (q_ref, k_ref, vt_ref, z_ref, x_ref, kc_ref, vc_ref, mod_ref, lam_ref, subln_ref,
         w_ref, g_ref, b_ref, o_ref, y_ref, kcb_ref, vct_ref) = refs

        @pl.when(pl.program_id(1) == 0)
        def _():
            n_ctx = kc_ref.shape[0]
            kcb_ref[...] = kc_ref[...].reshape(n_ctx, D_MODEL).astype(BF16)
            vct_ref[...] = vc_ref[...].reshape(n_ctx, D_MODEL).T.astype(BF16)
    else:
        (q_ref, k_ref, vt_ref, z_ref, x_ref, mod_ref, lam_ref, subln_ref,
         w_ref, g_ref, b_ref, o_ref, y_ref) = refs
    tq = q_ref.shape[0] // n_seq
    lam_init = 0.8 - 0.6 * math.exp(-0.3 * layer_idx)
    lm = lam_ref[...]
    lam = (jnp.exp(jnp.sum(lm[0:1] * lm[1:2], axis=-1, keepdims=True))
           - jnp.exp(jnp.sum(lm[2:3] * lm[3:4], axis=-1, keepdims=True)) + lam_init)
    lane = lax.broadcasted_iota(jnp.int32, (1, DV_A), 1)
    first = lane < DH_A
    subln = jnp.broadcast_to(subln_ref[...], (DV_A, tq))
    nt = (((1,), (1,)), ((), ()))
    t = k_ref.shape[0] if shared_keys else k_ref.shape[0] // n_seq

    def key_rows(sq):
        return slice(0, t) if shared_keys else slice(sq * t, (sq + 1) * t)

    def scores(unit):
        sq, hd = unit
        cols = slice(hd * DV_A, (hd + 1) * DV_A)
        qh = q_ref[sq * tq:(sq + 1) * tq, cols]
        zero = jnp.zeros_like(qh)
        qq = jnp.concatenate([jnp.where(first, qh, zero), jnp.where(first, zero, qh)], axis=0)
        parts = [lax.dot_general(k_ref[key_rows(sq), cols], qq, nt, preferred_element_type=F32)]
        if has_ctx:
            parts.append(lax.dot_general(kcb_ref[:, cols], qq, nt, preferred_element_type=F32))
        return parts

    def exps(parts):
        m8 = functools.reduce(jnp.maximum, [_slab_reduce(jnp.maximum, s) for s in parts])
        m = jnp.max(m8, axis=0, keepdims=True)
        return [jnp.exp2(s - m).astype(BF16) for s in parts]

    def with_ones(vals_t):
        return jnp.concatenate([vals_t, jnp.ones((ONES_ROWS, vals_t.shape[1]), BF16)], axis=0)

    def finish(unit, es):
        sq, hd = unit
        cols = slice(hd * DV_A, (hd + 1) * DV_A)
        rows = slice(sq * tq, (sq + 1) * tq)
        acc = jnp.dot(with_ones(vt_ref[cols, key_rows(sq)]), es[0], preferred_element_type=F32)
        if has_ctx:
            acc = acc + jnp.dot(with_ones(vct_ref[cols, :]), es[1], preferred_element_type=F32)
        inv = 1.0 / acc[DV_A:DV_A + 1, :]
        ot = acc[:DV_A, :tq] * inv[:, :tq] - acc[:DV_A, tq:] * (inv[:, tq:] * lam)
        ot = ot * lax.rsqrt(jnp.mean(ot * ot, axis=0, keepdims=True) + RMS_EPS)
        ot = ot * subln * (1.0 - lam_init)
        y_ref[rows, cols] = (ot.T * _silu(z_ref[rows, cols])).astype(BF16)
        if hd == H_A - 1 and (sq == n_seq - 1 or not shared_keys):
            done = slice(0, n_seq * tq) if shared_keys else rows
            out = jnp.dot(y_ref[done, :], w_ref[...].astype(BF16), preferred_element_type=F32)
            gate = mod_ref[:, 2 * D_MODEL:3 * D_MODEL]
            o_ref[done, :] = _residual_layer_norm(x_ref[done, :], out, gate, g_ref[...], b_ref[...])

    units = [(sq, hd) for sq in range(n_seq) for hd in range(H_A)]
    s_ahead = {u: scores(units[u]) for u in range(min(2, len(units)))}
    e_ahead = {0: exps(s_ahead.pop(0))}
    for u in range(len(units)):
        if u + 2 < len(units):
            s_ahead[u + 2] = scores(units[u + 2])
        if u + 1 < len(units):
            e_ahead[u + 1] = exps(s_ahead.pop(u + 1))
        finish(units[u], e_ahead.pop(u))


def _attention(qkvz, first_row, m_rows, x, seq_len, ctx, mod3, lam, subln, w_out, j, ln_g, ln_b,
               layer, per_batch_rows):
    q, k, vt, z = qkvz
    t = seq_len
    b = m_rows // t
    has_ctx = ctx is not None
    if t <= 256 and not per_batch_rows and not has_ctx and b % 2 == 0:
        rows, n_seq, shared_keys, n_kseq = 2 * t, 2, False, 2
    else:
        rows, n_seq, shared_keys, n_kseq = ATTN_ROWS, ATTN_SUBBLOCKS, True, 1
    nq = n_kseq * t // rows
    x, x_first_row = x
    r0, k0 = first_row // rows, first_row // (n_kseq * t)
    x_spec = pl.BlockSpec((rows, D_MODEL), lambda i, j: (x_first_row // rows + i * nq + j, 0))
    q_spec = pl.BlockSpec((rows, D_MODEL), lambda i, j: (r0 + i * nq + j, 0))
    o_spec = pl.BlockSpec((rows, D_MODEL), lambda i, j: (i * nq + j, 0))
    k_spec = pl.BlockSpec((n_kseq * t, D_MODEL), lambda i, j: (k0 + i, 0))
    vt_spec = pl.BlockSpec((D_MODEL, n_kseq * t), lambda i, j: (0, k0 + i))
    if per_batch_rows:
        mod_map = lambda i, j: (layer * MOD_ROWS + i, 0, 0)
    else:
        mod_map = lambda i, j: (layer * MOD_ROWS + CTX_ROW, 0, 0)
    const2 = lambda i, j: (0, 0)
    in_specs = [q_spec, k_spec, vt_spec, q_spec, x_spec]
    args = [q, k, vt, z, x]
    scratch = [pltpu.VMEM((rows, D_MODEL), BF16)]
    if has_ctx:
        cache_k, cache_v, jj = ctx
        n_ctx = cache_k.shape[2]
        scratch += [pltpu.VMEM((n_ctx, D_MODEL), BF16), pltpu.VMEM((D_MODEL, n_ctx), BF16)]
        c_spec = pl.BlockSpec((None, None, n_ctx, H_A, DV_A),
                              lambda i, j: (i, jj, 0, 0, 0))
        in_specs += [c_spec, c_spec]
        args += [cache_k, cache_v]
    in_specs += [pl.BlockSpec((None, 1, 3 * D_MODEL), mod_map),
                 pl.BlockSpec((4, DH_A), const2),
                 pl.BlockSpec((DV_A, 1), const2),
                 _weight_spec(w_out, j),
                 pl.BlockSpec((1, D_MODEL), const2),
                 pl.BlockSpec((1, D_MODEL), const2)]
    args += [mod3, lam, subln.reshape(DV_A, 1), w_out, ln_g.reshape(1, D_MODEL),
             ln_b.reshape(1, D_MODEL)]
    return pl.pallas_call(
        functools.partial(_attn_kernel, layer_idx=layer, has_ctx=has_ctx, n_seq=n_seq,
                          shared_keys=shared_keys),
        out_shape=jax.ShapeDtypeStruct((m_rows, D_MODEL), F32),
        grid=(b // n_kseq, nq),
        in_specs=in_specs,
        out_specs=o_spec,
        scratch_shapes=scratch,
        compiler_params=_params(2),
        name="diff_attn_ctx" if has_ctx else "diff_attn",
    )(*args)


def _rope_tables(n_tokens):
    rows = n_tokens // GRID_W
    r = jnp.repeat(jnp.arange(rows, dtype=F32), GRID_W)
    col = jnp.tile(jnp.arange(GRID_W, dtype=F32), rows)
    n_freq = DH_A // 4
    inv = ROPE_BASE ** (-jnp.arange(n_freq, dtype=F32) / n_freq)
    ang = jnp.concatenate([r[:, None] * inv, col[:, None] * inv], -1)
    cos, sin = jnp.cos(ang), jnp.sin(ang)
    return jnp.tile(cos, (1, 4)), jnp.concatenate([-sin, sin, -sin, sin], -1)


def _ret_in_kernel(xc_ref, xl_ref, mod_ref, w_hbm_ref, q_ref, kt_ref, v_ref, g_ref, w_ref, sem_ref, *,
                   n_ctx_tiles, j):
    nq = H_B * DK_B
    cc = WEIGHT_CHUNK_COLS
    assert nq == cc and E_B % cc == 0
    staged = _StagedWeight(w_hbm_ref, w_ref, sem_ref, j, cc)

    def body(arrive):
        h = _modulated(_both_streams_rows(xc_ref, xl_ref, n_ctx_tiles), mod_ref)
        arrive(0)
        q_ref[...] = jnp.dot(h, w_ref[:, 0:nq], preferred_element_type=F32).astype(BF16)
        arrive(1)
        k = jnp.dot(h, w_ref[:, nq:2 * nq], preferred_element_type=F32) * (DK_B ** -0.5)
        kt_ref[...] = k.T
        for c in range(E_B // cc):
            arrive(2 + c)
            v = jnp.dot(h, w_ref[:, 2 * nq + c * cc:2 * nq + (c + 1) * cc],
                        preferred_element_type=F32)
            v_ref[:, c * cc:(c + 1) * cc] = v.astype(BF16)
        for c in range(E_B // cc):
            arrive(2 + E_B // cc + c)
            g_ref[:, c * cc:(c + 1) * cc] = jnp.dot(
                h, w_ref[:, 2 * nq + E_B + c * cc:2 * nq + E_B + (c + 1) * cc],
                preferred_element_type=F32)

    staged.run(body)


def _ret_in(x_ctx, x_lat, mod3, w_in, j, layer, lat_rows_per_batch):
    m_rows = x_ctx.shape[0] + x_lat.shape[0]
    tm = ROW_TILE
    nq = H_B * DK_B
    n_ctx_tiles = x_ctx.shape[0] // tm
    return pl.pallas_call(
        functools.partial(_ret_in_kernel, n_ctx_tiles=n_ctx_tiles, j=j),
        out_shape=[jax.ShapeDtypeStruct((m_rows, nq), BF16),
                   jax.ShapeDtypeStruct((nq, m_rows), F32),
                   jax.ShapeDtypeStruct((m_rows, E_B), BF16),
                   jax.ShapeDtypeStruct((m_rows, E_B), F32)],
        grid=(m_rows // tm,),
        in_specs=_both_streams_specs(x_ctx, x_lat, tm) + [
            pl.BlockSpec((None, 1, 3 * D_MODEL),
                         _both_streams_mod_map(layer, n_ctx_tiles, tm, lat_rows_per_batch)),
            pl.BlockSpec(memory_space=pl.ANY)],
        out_specs=[pl.BlockSpec((tm, nq), lambda m: (m, 0)),
                   pl.BlockSpec((nq, tm), lambda m: (0, m)),
                   pl.BlockSpec((tm, E_B), lambda m: (m, 0)),
                   pl.BlockSpec((tm, E_B), lambda m: (m, 0))],
        scratch_shapes=_StagedWeight.scratch(w_in, WEIGHT_CHUNK_COLS),
        compiler_params=_params(1),
        name="ret_in",
    )(x_ctx, x_lat, mod3, w_in)


def _retention_kernel(*refs, has_state, heads, fused_out):
    refs = list(refs)
    q_ref, kt_ref, v_ref, g_ref, af_ref, ab_ref = refs[:6]
    del refs[:6]
    if has_state:
        s0f_ref, s0b_ref = refs[:2]
        del refs[:2]
    if fused_out:
        x_ref, mod_ref, w_ref, lng_ref, lnb_ref = refs[:5]
        del refs[:5]
        o_ref = refs.pop(0)
        y_ref = refs.pop()
    else:
        y_ref = refs.pop(0)
    if not has_state:
        sf_ref, sb_ref = refs
    t = q_ref.shape[0]
    nc = t // CHUNK
    row = lax.broadcasted_iota(jnp.int32, (CHUNK, CHUNK), 0).astype(F32)
    col = lax.broadcasted_iota(jnp.int32, (CHUNK, CHUNK), 1).astype(F32)
    diff = row - col
    idx_col = lax.broadcasted_iota(jnp.int32, (CHUNK, 1), 0).astype(F32)
    idx_row = lax.broadcasted_iota(jnp.int32, (1, CHUNK), 1).astype(F32)

    def chunk(c):
        return slice(c * CHUNK, (c + 1) * CHUNK)

    for hh in range(heads):
        qk_cols = slice(hh * DK_B, (hh + 1) * DK_B)
        v_cols = slice(hh * DV_B, (hh + 1) * DV_B)
        lg_f = jnp.log1p(-jnp.exp(af_ref[hh]))
        lg_b = jnp.log1p(-jnp.exp(ab_ref[hh]))
        dmask = (jnp.where(diff >= 0, jnp.exp(jnp.maximum(diff, 0.0) * lg_f), 0.0)
                 + jnp.where(diff <= 0, jnp.exp(jnp.maximum(-diff, 0.0) * lg_b), 0.0))
        qd_f = jnp.exp((idx_col + 1.0) * lg_f)
        qd_b = jnp.exp((CHUNK - idx_col) * lg_b)
        kd_f = jnp.exp((CHUNK - 1.0 - idx_row) * lg_f)
        kd_b = jnp.exp(idx_row * lg_b)
        cd_f = jnp.exp(CHUNK * lg_f)
        cd_b = jnp.exp(CHUNK * lg_b)

        def states(order, kd, cd, s):
            seen = {}
            for n, c in enumerate(order):
                seen[c] = None if s is None else s.astype(BF16)
                if has_state and n == nc - 1:
                    return seen, None
                u = jnp.dot((kt_ref[qk_cols, chunk(c)] * kd).astype(BF16), v_ref[chunk(c), v_cols],
                            preferred_element_type=F32)
                s = u if s is None else s * cd + u
            return seen, s

        seen_f, s_f = states(range(nc), kd_f, cd_f, s0f_ref[hh] if has_state else None)
        seen_b, s_b = states(range(nc - 1, -1, -1), kd_b, cd_b, s0b_ref[hh] if has_state else None)
        if not has_state:
            sf_ref[hh] = s_f
            sb_ref[hh] = s_b
        for c in range(nc):
            qc = q_ref[chunk(c), qk_cols]
            qk = jnp.dot(qc, kt_ref[qk_cols, chunk(c)].astype(BF16), preferred_element_type=F32)
            o = jnp.dot((qk * dmask).astype(BF16), v_ref[chunk(c), v_cols],
                        preferred_element_type=F32)
            if seen_f[c] is not None:
                o = o + jnp.dot(qc, seen_f[c], preferred_element_type=F32) * qd_f
            if seen_b[c] is not None:
                o = o + jnp.dot(qc, seen_b[c], preferred_element_type=F32) * qd_b
            o = o * lax.rsqrt(jnp.mean(o * o, axis=-1, keepdims=True) + RMS_EPS)
            y_ref[chunk(c), v_cols] = (o * _silu(g_ref[chunk(c), v_cols])).astype(BF16)
    if fused_out:
        part = jnp.dot(y_ref[...], w_ref[...].astype(BF16), preferred_element_type=F32)
        gate = mod_ref[:, 2 * D_MODEL:3 * D_MODEL]
        head_steps = H_B // heads
        if head_steps == 1:
            o_ref[...] = _residual_layer_norm(x_ref[...], part, gate, lng_ref[...], lnb_ref[...])
        else:
            step = pl.program_id(1)

            @pl.when(step == 0)
            def _():
                o_ref[...] = part

            @pl.when(jnp.logical_and(step > 0, step < head_steps - 1))
            def _():
                o_ref[...] += part

            @pl.when(step == head_steps - 1)
            def _():
                o_ref[...] = _residual_layer_norm(x_ref[...], o_ref[...] + part, gate, lng_ref[...],
                                                  lnb_ref[...])


def _retention(q, kt, v, g, seq_len, n_seq, first_row, heads, decay_f, decay_b, states,
               out_proj=None):
    t = seq_len
    b = n_seq
    s0 = first_row // t
    has_state = states is not None
    fused_out = out_proj is not None
    q_spec = pl.BlockSpec((t, heads * DK_B), lambda i, h: (s0 + i, h))
    kt_spec = pl.BlockSpec((heads * DK_B, t), lambda i, h: (h, s0 + i))
    vg_spec = pl.BlockSpec((t, heads * DV_B), lambda i, h: (s0 + i, h))
    a_spec = pl.BlockSpec((heads, 1, 1), lambda i, h: (h, 0, 0))
    in_specs = [q_spec, kt_spec, vg_spec, vg_spec, a_spec, a_spec]
    args = [q, kt, v, g, decay_f.reshape(H_B, 1, 1), decay_b.reshape(H_B, 1, 1)]
    out_shape = [jax.ShapeDtypeStruct((b * t, E_B), BF16)]
    out_specs = [pl.BlockSpec((t, heads * DV_B), lambda i, h: (i, h))]
    if has_state:
        s_f, s_b, jj = states
        s_spec = pl.BlockSpec((None, None, heads, DK_B, DV_B), lambda i, h: (i, jj, h, 0, 0))
        in_specs += [s_spec, s_spec]
        args += [s_f, s_b]
    scratch = []
    if fused_out:
        x2d, mod3, mod_map, w_out, jw, ln_g, ln_b = out_proj
        x_spec = pl.BlockSpec((t, D_MODEL), lambda i, h: (i, 0))
        const2 = lambda i, h: (0, 0)
        if heads == H_B:
            w_spec = _weight_spec(w_out, jw)
        else:
            w_spec = pl.BlockSpec((None, heads * DV_B, D_MODEL), lambda i, h: (jw, h, 0))
        in_specs += [x_spec, pl.BlockSpec((None, 1, 3 * D_MODEL), mod_map), w_spec,
                     pl.BlockSpec((1, D_MODEL), const2), pl.BlockSpec((1, D_MODEL), const2)]
        args += [x2d, mod3, w_out, ln_g.reshape(1, D_MODEL), ln_b.reshape(1, D_MODEL)]
        out_shape = [jax.ShapeDtypeStruct((b * t, D_MODEL), F32)]
        out_specs = [x_spec]
        scratch = [pltpu.VMEM((t, heads * DV_B), BF16)]
    if not has_state:
        so_spec = pl.BlockSpec((None, None, heads, DK_B, DV_B), lambda i, h: (i, 0, h, 0, 0))
        out_shape += [jax.ShapeDtypeStruct((b, 1, H_B, DK_B, DV_B), F32)] * 2
        out_specs += [so_spec, so_spec]
    return pl.pallas_call(
        functools.partial(_retention_kernel, has_state=has_state, heads=heads,
                          fused_out=fused_out),
        out_shape=out_shape,
        grid=(b, H_B // heads),
        in_specs=in_specs,
        out_specs=out_specs,
        scratch_shapes=scratch,
        compiler_params=_params(2),
        name="retention_state" if has_state else "retention",
    )(*args)


CONV_ROWS = 1024
CONV_COLS = 256


def _conv_kernel(xc_ref, xl_ref, mod_ref, w_in_ref, cw_ref, w_out_ref, g_ref, b_ref, o_ref, *,
                 n_ctx_tiles, ctx_seq_len, lat_seq_len):
    x = _both_streams_rows(xc_ref, xl_ref, n_ctx_tiles)
    h = _modulated(x, mod_ref)
    rows = x.shape[0]
    is_ctx = pl.program_id(0) < n_ctx_tiles
    row = lax.broadcasted_iota(jnp.int32, (rows, 1), 0)
    pos = jnp.where(is_ctx, row % ctx_seq_len, row % lat_seq_len)
    has_prev = pos > 0
    has_next = pos < jnp.where(is_ctx, ctx_seq_len - 1, lat_seq_len - 1)
    e = D_MODEL
    for c in range(e // CONV_COLS):
        cols = slice(c * CONV_COLS, (c + 1) * CONV_COLS)

        def proj(part):
            lo = part * e + c * CONV_COLS
            return jnp.dot(h, w_in_ref[:, lo:lo + CONV_COLS], preferred_element_type=F32)

        p = proj(1) * proj(2)
        prev = jnp.where(has_prev, pltpu.roll(p, 1, 0), 0.0)
        nxt = jnp.where(has_next, pltpu.roll(p, rows - 1, 0), 0.0)
        conv = prev * cw_ref[0:1, cols] + p * cw_ref[1:2, cols] + nxt * cw_ref[2:3, cols]
        y = (proj(0) * conv * _silu(proj(3))).astype(BF16)
        part = jnp.dot(y, w_out_ref[cols, :].astype(BF16), preferred_element_type=F32)
        if c == 0:
            o_ref[...] = part
        else:
            o_ref[...] += part
    gate = mod_ref[:, 2 * D_MODEL:3 * D_MODEL]
    o_ref[...] = _residual_layer_norm(x, o_ref[...], gate, g_ref[...], b_ref[...])


def _conv_layer(x_ctx, x_lat, mod3, w_in, conv_w, w_out, j, ln_g, ln_b, layer, ctx_seq_len,
                lat_seq_len):
    m_rows = x_ctx.shape[0] + x_lat.shape[0]
    tm = CONV_ROWS
    n_ctx_tiles = x_ctx.shape[0] // tm
    const2 = lambda m: (0, 0)
    return pl.pallas_call(
        functools.partial(_conv_kernel, n_ctx_tiles=n_ctx_tiles, ctx_seq_len=ctx_seq_len,
                          lat_seq_len=lat_seq_len),
        out_shape=jax.ShapeDtypeStruct((m_rows, D_MODEL), F32),
        grid=(m_rows // tm,),
        in_specs=_both_streams_specs(x_ctx, x_lat, tm) + [
                  pl.BlockSpec((None, 1, 3 * D_MODEL),
                               _both_streams_mod_map(layer, n_ctx_tiles, tm, lat_seq_len)),
                  _weight_spec(w_in, j),
                  pl.BlockSpec((None, 3, D_MODEL), lambda m: (j, 0, 0)),
                  _weight_spec(w_out, j),
                  pl.BlockSpec((1, D_MODEL), const2),
                  pl.BlockSpec((1, D_MODEL), const2)],
        out_specs=pl.BlockSpec((tm, D_MODEL), lambda m: (m, 0)),
        compiler_params=_params(1),
        name="conv_layer",
    )(x_ctx, x_lat, mod3, w_in, conv_w, w_out, ln_g.reshape(1, D_MODEL), ln_b.reshape(1, D_MODEL))


def _separate(xp, xs, m_ctx):
    if xp[0] is xs[0]:
        return (xp[0][:m_ctx], 0), (xs[0][m_ctx:], 0)
    return xp, xs


def kernel(x_prompt, x_sample, cache_k, cache_v, state_fwd, state_bwd, c, c_ctx, w_mod, b_mod, ln_g,
           ln_b, w_in_a, lam_a, subln_a, w_out_a, w_in_b, decay_fwd, decay_bwd, w_out_b, w_in_c,
           conv_c, w_out_c):
    bp, tp, d = x_prompt.shape
    bs, ts, _ = x_sample.shape
    cvec = jnp.concatenate([c, c_ctx[None], jnp.zeros((MOD_ROWS - bs - 1, d), F32)], axis=0)
    mod3 = _modulation(cvec, w_mod, b_mod)
    rope_tables = _rope_tables(ts)

    m_ctx, m_lat = bp * tp, bs * ts
    xp = (x_prompt.reshape(m_ctx, d), 0)
    xs = (x_sample.reshape(m_lat, d), 0)
    n_attn = (DEPTH + N_MIXERS - 1) // N_MIXERS
    new_cache_k = new_cache_v = jax.ShapeDtypeStruct((bp, n_attn, tp, H_A, DV_A), F32)
    new_sf, new_sb = [], []
    for i in range(DEPTH):
        kind, j = i % N_MIXERS, i // N_MIXERS
        if kind == 0:
            *qkvz, new_cache_k, new_cache_v = _attn_in(
                xp, xs, m_ctx, m_lat, mod3, w_in_a, j, i, ts, rope_tables,
                (new_cache_k, new_cache_v))
            xp = (_attention(qkvz, 0, m_ctx, xp, tp, None, mod3, lam_a[j], subln_a[j], w_out_a,
                             j, ln_g[i], ln_b[i], i, False), 0)
            xs = (_attention(qkvz, m_ctx, m_lat, xs, ts, (cache_k, cache_v, j), mod3, lam_a[j],
                             subln_a[j], w_out_a, j, ln_g[i], ln_b[i], i, True), 0)
        else:
            xp, xs = _separate(xp, xs, m_ctx)
            if kind == 1:
                q, kt, v, g = _ret_in(xp[0], xs[0], mod3, w_in_b, j, i, ts)
                x_ctx, s_f, s_b = _retention(
                    q, kt, v, g, tp, bp, 0, H_B, decay_fwd[j], decay_bwd[j], None,
                    out_proj=(xp[0], mod3, _mod_row_map(i, None, tp), w_out_b, j, ln_g[i],
                              ln_b[i]))
                new_sf.append(s_f)
                new_sb.append(s_b)
                (x_lat,) = _retention(
                    q, kt, v, g, ts, bs, m_ctx, 1, decay_fwd[j], decay_bwd[j],
                    (state_fwd, state_bwd, j),
                    out_proj=(xs[0], mod3, _mod_row_map(i, ts, ts), w_out_b, j, ln_g[i], ln_b[i]))
                xp, xs = (x_ctx, 0), (x_lat, 0)
            else:
                x_all = _conv_layer(xp[0], xs[0], mod3, w_in_c, conv_c, w_out_c, j, ln_g[i],
                                    ln_b[i], i, tp, ts)
                xp, xs = (x_all, 0), (x_all, m_ctx)
    xp, xs = _separate(xp, xs, m_ctx)
    y_prompt = xp[0].reshape(bp, tp, d)
    y_sample = xs[0].reshape(bs, ts, d)
    new_state_fwd = jnp.concatenate(new_sf, axis=1)
    new_state_bwd = jnp.concatenate(new_sb, axis=1)
    return (y_prompt, y_sample, new_cache_k, new_cache_v, new_state_fwd, new_state_bwd)
```

```python
import functools
import math

import jax
import jax.numpy as jnp
from jax import lax
from jax.experimental import pallas as pl
from jax.experimental.pallas import tpu as pltpu

F32 = jnp.float32
BF16 = jnp.bfloat16

D_MODEL = 1024
DEPTH = 4
N_MIXERS = 3
GRID_W = 64
H_A = 8
DH_A = 64
DV_A = 128
ROPE_HALF = DH_A // 2
SCORE_SCALE = DH_A ** -0.5 * math.log2(math.e)
H_B = 4
DK_B = 256
DV_B = 512
E_B = H_B * DV_B
CHUNK = 256
ALPHA = (2.0 * DEPTH) ** 0.25
ROPE_BASE = 10000.0
LN_EPS = 1e-5
RMS_EPS = 1e-6

MOD_ROWS = 8
CTX_ROW = 4
VMEM_LIMIT_BYTES = 58 * 1024 * 1024
ROW_TILE = 512


def _params(n_axes):
    return pltpu.CompilerParams(dimension_semantics=("arbitrary",) * n_axes,
                                vmem_limit_bytes=VMEM_LIMIT_BYTES)


def _silu(x):
    return x * jax.nn.sigmoid(x)


def _residual_layer_norm(x, out, gate, g, b):
    r = ALPHA * x + gate * out
    mu = jnp.mean(r, axis=-1, keepdims=True)
    d = r - mu
    var = jnp.mean(d * d, axis=-1, keepdims=True)
    return d * lax.rsqrt(var + LN_EPS) * g + b


def _modulated(x, mod_ref):
    shift = mod_ref[:, 0:D_MODEL]
    scale = mod_ref[:, D_MODEL:2 * D_MODEL]
    return x * (1.0 + scale) + shift


def _mod_row_map(layer, rows_per_batch, tile):
    if rows_per_batch is None:
        return lambda m, *_: (layer * MOD_ROWS + CTX_ROW, 0, 0)
    return lambda m, *_: (layer * MOD_ROWS + (m * tile) // rows_per_batch, 0, 0)


def _both_streams_specs(x_ctx, x_lat, tm):
    n_ctx_tiles = x_ctx.shape[0] // tm
    return [pl.BlockSpec((tm, D_MODEL), lambda m: (jnp.minimum(m, n_ctx_tiles - 1), 0)),
            pl.BlockSpec((tm, D_MODEL), lambda m: (jnp.maximum(m - n_ctx_tiles, 0), 0))]


def _both_streams_rows(x_ctx_ref, x_lat_ref, n_ctx_tiles):
    return jnp.where(pl.program_id(0) < n_ctx_tiles, x_ctx_ref[...], x_lat_ref[...])


def _both_streams_mod_map(layer, n_ctx_tiles, tm, lat_rows_per_batch):
    def index_map(m):
        lat_row = ((m - n_ctx_tiles) * tm) // lat_rows_per_batch
        return (layer * MOD_ROWS + jnp.where(m < n_ctx_tiles, CTX_ROW, lat_row), 0, 0)
    return index_map


def _weight_spec(w, j):
    return pl.BlockSpec((None,) + w.shape[1:], lambda *_: (j, 0, 0), pipeline_mode=pl.Buffered(1))


WEIGHT_CHUNK_COLS = 1024
WEIGHT_COPIES_IN_FLIGHT = 2


class _StagedWeight:
    @staticmethod
    def scratch(w, chunk_cols):
        return [pltpu.VMEM(w.shape[1:], w.dtype),
                pltpu.SemaphoreType.DMA((w.shape[2] // chunk_cols,))]

    def __init__(self, w_hbm_ref, w_vmem_ref, sem_ref, j, chunk_cols):
        self.copies = [
            pltpu.make_async_copy(w_hbm_ref.at[j, :, pl.ds(c * chunk_cols, chunk_cols)],
                                  w_vmem_ref.at[:, pl.ds(c * chunk_cols, chunk_cols)], sem_ref.at[c])
            for c in range(w_vmem_ref.shape[1] // chunk_cols)]

    def _arrive(self, c):
        self.copies[c].wait()
        if c + WEIGHT_COPIES_IN_FLIGHT < len(self.copies):
            self.copies[c + WEIGHT_COPIES_IN_FLIGHT].start()

    def run(self, body):
        first = pl.program_id(0) == 0

        @pl.when(first)
        def _():
            for cp in self.copies[:WEIGHT_COPIES_IN_FLIGHT]:
                cp.start()
            body(self._arrive)

        @pl.when(jnp.logical_not(first))
        def _():
            body(lambda c: None)


def _mod_kernel(cv_ref, w_ref, b_ref, o_ref):
    s = _silu(cv_ref[...])
    o_ref[...] = jnp.dot(s, w_ref[...], preferred_element_type=F32) + b_ref[...]


def _modulation(cvec, w_mod, b_mod):
    n = 3 * D_MODEL
    tn = n
    out = pl.pallas_call(
        _mod_kernel,
        out_shape=jax.ShapeDtypeStruct((DEPTH, MOD_ROWS, n), F32),
        grid=(DEPTH, n // tn),
        in_specs=[pl.BlockSpec((MOD_ROWS, D_MODEL), lambda i, j: (0, 0)),
                  pl.BlockSpec((None, D_MODEL, tn), lambda i, j: (i, 0, j)),
                  pl.BlockSpec((None, 1, tn), lambda i, j: (i, 0, j))],
        out_specs=pl.BlockSpec((None, MOD_ROWS, tn), lambda i, j: (i, 0, j)),
        compiler_params=_params(2),
        name="modulation",
    )(cvec, w_mod, b_mod.reshape(DEPTH, 1, n))
    return out.reshape(DEPTH * MOD_ROWS, 1, n)


def _rope(xh, cos4, sin4, first_half):
    swapped = jnp.where(first_half, pltpu.roll(xh, DV_A - ROPE_HALF, 1),
                        pltpu.roll(xh, ROPE_HALF, 1))
    return xh * cos4 + swapped * sin4


def _store_heads(o_ref, x, seq_len):
    for b in range(o_ref.shape[0]):
        xb = x[b * seq_len:(b + 1) * seq_len].reshape(seq_len, H_A, DV_A)
        if len(o_ref.shape) == 4:
            o_ref[b] = xb
        else:
            o_ref[b, 0] = xb
            for s in range(1, o_ref.shape[1]):
                o_ref[b, s] = jnp.zeros_like(xb)


def _attn_in_kernel(*refs, n_x, n_ctx_tiles, seq_len, n_aliased):
    mod_ref, w_ref, cos_ref, sin_ref = refs[n_x:n_x + 4]
    q_ref, k_ref, vt_ref, z_ref, ck_ref, cv_ref = refs[n_x + 4 + n_aliased:]
    is_ctx = pl.program_id(0) < n_ctx_tiles
    x = refs[0][...] if n_x == 1 else _both_streams_rows(refs[0], refs[1], n_ctx_tiles)
    h = _modulated(x, mod_ref)
    lane = lax.broadcasted_iota(jnp.int32, (1, DV_A), 1)
    first_half = (lane % DH_A) < ROPE_HALF
    cos4 = cos_ref[...]
    sin4 = sin_ref[...]
    q_all = jnp.dot(h, w_ref[:, 0:D_MODEL], preferred_element_type=F32)
    k_all = jnp.dot(h, w_ref[:, D_MODEL:2 * D_MODEL], preferred_element_type=F32)
    for hd in range(H_A):
        cols = slice(hd * DV_A, (hd + 1) * DV_A)
        q = q_all[:, cols]
        k = k_all[:, cols]
        q = jnp.where(is_ctx, q, _rope(q, cos4, sin4, first_half))
        k = jnp.where(is_ctx, k, _rope(k, cos4, sin4, first_half))
        q_ref[:, cols] = (q * SCORE_SCALE).astype(BF16)
        k_ref[:, cols] = k.astype(BF16)
    v = jnp.dot(h, w_ref[:, 2 * D_MODEL:3 * D_MODEL], preferred_element_type=F32)
    vt_ref[...] = v.T.astype(BF16)
    z_ref[...] = jnp.dot(h, w_ref[:, 3 * D_MODEL:4 * D_MODEL], preferred_element_type=F32)

    @pl.when(is_ctx)
    def _():
        _store_heads(ck_ref, k_all, seq_len)
        _store_heads(cv_ref, v, seq_len)


def _attn_in(x_ctx, x_lat, m_ctx, m_lat, mod3, w_in, j, layer, lat_rows_per_batch, rope_tables,
             cache_out):
    tm = ROW_TILE
    m_rows = m_ctx + m_lat
    n_ctx_tiles = m_ctx // tm
    seq_tiles = lat_rows_per_batch // tm
    row_spec = pl.BlockSpec((tm, D_MODEL), lambda m: (m, 0))
    if x_ctx[0] is x_lat[0]:
        assert x_ctx[1] == 0 and x_lat[1] == m_ctx
        x_specs, x_args = [row_spec], [x_ctx[0]]
    else:
        assert x_ctx[1] == 0 and x_lat[1] == 0
        x_specs, x_args = _both_streams_specs(x_ctx[0], x_lat[0], tm), [x_ctx[0], x_lat[0]]
    tab_spec = pl.BlockSpec(
        (tm, DV_A), lambda m: (jnp.where(m < n_ctx_tiles, 0, (m - n_ctx_tiles) % seq_tiles), 0))
    in_specs = x_specs + [
        pl.BlockSpec((None, 1, 3 * D_MODEL),
                     _both_streams_mod_map(layer, n_ctx_tiles, tm, lat_rows_per_batch)),
        _weight_spec(w_in, j), tab_spec, tab_spec]
    args = x_args + [mod3, w_in] + list(rope_tables)
    new_k, new_v = cache_out
    seq_len = new_k.shape[2]
    cache_shape = jax.ShapeDtypeStruct(new_k.shape, F32)
    ctx_block = lambda m: jnp.minimum(m, n_ctx_tiles - 1)
    aliases = {}
    if j == 0:
        cache_spec = pl.BlockSpec((tm // seq_len, new_k.shape[1], seq_len, H_A, DV_A),
                                  lambda m: (ctx_block(m), 0, 0, 0, 0))
    else:
        cache_spec = pl.BlockSpec((tm // seq_len, None, seq_len, H_A, DV_A),
                                  lambda m: (ctx_block(m), j, 0, 0, 0))
        in_specs += [pl.BlockSpec(memory_space=pl.ANY)] * 2
        args += [new_k, new_v]
        aliases = {len(args) - 2: 4, len(args) - 1: 5}
    return pl.pallas_call(
        functools.partial(_attn_in_kernel, n_x=len(x_args), n_ctx_tiles=n_ctx_tiles,
                          seq_len=seq_len, n_aliased=len(aliases)),
        out_shape=[jax.ShapeDtypeStruct((m_rows, D_MODEL), BF16),
                   jax.ShapeDtypeStruct((m_rows, D_MODEL), BF16),
                   jax.ShapeDtypeStruct((D_MODEL, m_rows), BF16),
                   jax.ShapeDtypeStruct((m_rows, D_MODEL), F32), cache_shape, cache_shape],
        grid=(m_rows // tm,),
        in_specs=in_specs,
        out_specs=[row_spec, row_spec, pl.BlockSpec((D_MODEL, tm), lambda m: (0, m)), row_spec,
                   cache_spec, cache_spec],
        input_output_aliases=aliases,
        compiler_params=_params(1),
        name="attn_in",
    )(*args)


ONES_ROWS = 16
ATTN_ROWS = 512
ATTN_SUBBLOCKS = 1


def _slab_reduce(op, x):
    parts = [x[i:i + 8] for i in range(0, x.shape[0], 8)]
    while len(parts) > 1:
        parts = [op(parts[i], parts[i + 1]) for i in range(0, len(parts) - 1, 2)] + (
            [parts[-1]] if len(parts) % 2 else [])
    return parts[0]


def _attn_kernel(*refs, layer_idx, has_ctx, n_seq, shared_keys):
    if has_ctx:
        (q_ref, k_ref, vt_ref, z_ref, x_ref, kc_ref, vc_ref, mod_ref, lam_ref, subln_ref,
         w_ref, g_ref, b_ref, o_ref, y_ref, kcb_ref, vct_ref) = refs

        @pl.when(pl.program_id(1) == 0)
        def _():
            n_ctx = kc_ref.shape[0]
            kcb_ref[...] = kc_ref[...].reshape(n_ctx, D_MODEL).astype(BF16)
            vct_ref[...] = vc_ref[...].reshape(n_ctx, D_MODEL).T.astype(BF16)
    else:
        (q_ref, k_ref, vt_ref, z_ref, x_ref, mod_ref, lam_ref, subln_ref,
         w_ref, g_ref, b_ref, o_ref, y_ref) = refs
    tq = q_ref.shape[0] // n_seq
    lam_init = 0.8 - 0.6 * math.exp(-0.3 * layer_idx)
    lm = lam_ref[...]
    lam = (jnp.exp(jnp.sum(lm[0:1] * lm[1:2], axis=-1, keepdims=True))
           - jnp.exp(jnp.sum(lm[2:3] * lm[3:4], axis=-1, keepdims=True)) + lam_init)
    lane = lax.broadcasted_iota(jnp.int32, (1, DV_A), 1)
    first = lane < DH_A
    subln = jnp.broadcast_to(subln_ref[...], (DV_A, tq))
    nt = (((1,), (1,)), ((), ()))
    t = k_ref.shape[0] if shared_keys else k_ref.shape[0] // n_seq

    def key_rows(sq):
        return slice(0, t) if shared_keys else slice(sq * t, (sq + 1) * t)

    def scores(unit):
        sq, hd = unit
        cols = slice(hd * DV_A, (hd + 1) * DV_A)
        qh = q_ref[sq * tq:(sq + 1) * tq, cols]
        zero = jnp.zeros_like(qh)
        qq = jnp.concatenate([jnp.where(first, qh, zero), jnp.where(first, zero, qh)], axis=0)
        parts = [lax.dot_general(k_ref[key_rows(sq), cols], qq, nt, preferred_element_type=F32)]
        if has_ctx:
            parts.append(lax.dot_general(kcb_ref[:, cols], qq, nt, preferred_element_type=F32))
        return parts

    def exps(parts):
        m8 = functools.reduce(jnp.maximum, [_slab_reduce(jnp.maximum, s) for s in parts])
        m = jnp.max(m8, axis=0, keepdims=True)
        return [jnp.exp2(s - m).astype(BF16) for s in parts]

    def with_ones(vals_t):
        return jnp.concatenate([vals_t, jnp.ones((ONES_ROWS, vals_t.shape[1]), BF16)], axis=0)

    def finish(unit, es):
        sq, hd = unit
        cols = slice(hd * DV_A, (hd + 1) * DV_A)
        rows = slice(sq * tq, (sq + 1) * tq)
        acc = jnp.dot(with_ones(vt_ref[cols, key_rows(sq)]), es[0], preferred_element_type=F32)
        if has_ctx:
            acc = acc + jnp.dot(with_ones(vct_ref[cols, :]), es[1], preferred_element_type=F32)
        inv = 1.0 / acc[DV_A:DV_A + 1, :]
        ot = acc[:DV_A, :tq] * inv[:, :tq] - acc[:DV_A, tq:] * (inv[:, tq:] * lam)
        ot = ot * lax.rsqrt(jnp.mean(ot * ot, axis=0, keepdims=True) + RMS_EPS)
        ot = ot * subln * (1.0 - lam_init)
        y_ref[rows, cols] = (ot.T * _silu(z_ref[rows, cols])).astype(BF16)
        if hd == H_A - 1 and (sq == n_seq - 1 or not shared_keys):
            done = slice(0, n_seq * tq) if shared_keys else rows
            out = jnp.dot(y_ref[done, :], w_ref[...].astype(BF16), preferred_element_type=F32)
            gate = mod_ref[:, 2 * D_MODEL:3 * D_MODEL]
            o_ref[done, :] = _residual_layer_norm(x_ref[done, :], out, gate, g_ref[...], b_ref[...])

    units = [(sq, hd) for sq in range(n_seq) for hd in range(H_A)]
    s_ahead = {u: scores(units[u]) for u in range(min(2, len(units)))}
    e_ahead = {0: exps(s_ahead.pop(0))}
    for u in range(len(units)):
        if u + 2 < len(units):
            s_ahead[u + 2] = scores(units[u + 2])
        if u + 1 < len(units):
            e_ahead[u + 1] = exps(s_ahead.pop(u + 1))
        finish(units[u], e_ahead.pop(u))


def _attention(qkvz, first_row, m_rows, x, seq_len, ctx, mod3, lam, subln, w_out, j, ln_g, ln_b,
               layer, per_batch_rows):
    q, k, vt, z = qkvz
    t = seq_len
    b = m_rows // t
    has_ctx = ctx is not None
    if t <= 256 and not per_batch_rows and not has_ctx and b % 2 == 0:
        rows, n_seq, shared_keys, n_kseq = 2 * t, 2, False, 2
    else:
        rows, n_seq, shared_keys, n_kseq = ATTN_ROWS, ATTN_SUBBLOCKS, True, 1
    nq = n_kseq * t // rows
    x, x_first_row = x
    r0, k0 = first_row // rows, first_row // (n_kseq * t)
    x_spec = pl.BlockSpec((rows, D_MODEL), lambda i, j: (x_first_row // rows + i * nq + j, 0))
    q_spec = pl.BlockSpec((rows, D_MODEL), lambda i, j: (r0 + i * nq + j, 0))
    o_spec = pl.BlockSpec((rows, D_MODEL), lambda i, j: (i * nq + j, 0))
    k_spec = pl.BlockSpec((n_kseq * t, D_MODEL), lambda i, j: (k0 + i, 0))
    vt_spec = pl.BlockSpec((D_MODEL, n_kseq * t), lambda i, j: (0, k0 + i))
    if per_batch_rows:
        mod_map = lambda i, j: (layer * MOD_ROWS + i, 0, 0)
    else:
        mod_map = lambda i, j: (layer * MOD_ROWS + CTX_ROW, 0, 0)
    const2 = lambda i, j: (0, 0)
    in_specs = [q_spec, k_spec, vt_spec, q_spec, x_spec]
    args = [q, k, vt, z, x]
    scratch = [pltpu.VMEM((rows, D_MODEL), BF16)]
    if has_ctx:
        cache_k, cache_v, jj = ctx
        n_ctx = cache_k.shape[2]
        scratch += [pltpu.VMEM((n_ctx, D_MODEL), BF16), pltpu.VMEM((D_MODEL, n_ctx), BF16)]
        c_spec = pl.BlockSpec((None, None, n_ctx, H_A, DV_A),
                              lambda i, j: (i, jj, 0, 0, 0))
        in_specs += [c_spec, c_spec]
        args += [cache_k, cache_v]
    in_specs += [pl.BlockSpec((None, 1, 3 * D_MODEL), mod_map),
                 pl.BlockSpec((4, DH_A), const2),
                 pl.BlockSpec((DV_A, 1), const2),
                 _weight_spec(w_out, j),
                 pl.BlockSpec((1, D_MODEL), const2),
                 pl.BlockSpec((1, D_MODEL), const2)]
    args += [mod3, lam, subln.reshape(DV_A, 1), w_out, ln_g.reshape(1, D_MODEL),
             ln_b.reshape(1, D_MODEL)]
    return pl.pallas_call(
        functools.partial(_attn_kernel, layer_idx=layer, has_ctx=has_ctx, n_seq=n_seq,
                          shared_keys=shared_keys),
        out_shape=jax.ShapeDtypeStruct((m_rows, D_MODEL), F32),
        grid=(b // n_kseq, nq),
        in_specs=in_specs,
        out_specs=o_spec,
        scratch_shapes=scratch,
        compiler_params=_params(2),
        name="diff_attn_ctx" if has_ctx else "diff_attn",
    )(*args)


def _rope_tables(n_tokens):
    rows = n_tokens // GRID_W
    r = jnp.repeat(jnp.arange(rows, dtype=F32), GRID_W)
    col = jnp.tile(jnp.arange(GRID_W, dtype=F32), rows)
    n_freq = DH_A // 4
    inv = ROPE_BASE ** (-jnp.arange(n_freq, dtype=F32) / n_freq)
    ang = jnp.concatenate([r[:, None] * inv, col[:, None] * inv], -1)
    cos, sin = jnp.cos(ang), jnp.sin(ang)
    return jnp.tile(cos, (1, 4)), jnp.concatenate([-sin, sin, -sin, sin], -1)


def _ret_in_kernel(xc_ref, xl_ref, mod_ref, w_hbm_ref, q_ref, kt_ref, v_ref, g_ref, w_ref, sem_ref, *,
                   n_ctx_tiles, j):
    nq = H_B * DK_B
    cc = WEIGHT_CHUNK_COLS
    assert nq == cc and E_B % cc == 0
    staged = _StagedWeight(w_hbm_ref, w_ref, sem_ref, j, cc)

    def body(arrive):
        h = _modulated(_both_streams_rows(xc_ref, xl_ref, n_ctx_tiles), mod_ref)
        arrive(0)
        q_ref[...] = jnp.dot(h, w_ref[:, 0:nq], preferred_element_type=F32).astype(BF16)
        arrive(1)
        k = jnp.dot(h, w_ref[:, nq:2 * nq], preferred_element_type=F32) * (DK_B ** -0.5)
        kt_ref[...] = k.T
        for c in range(E_B // cc):
            arrive(2 + c)
            v = jnp.dot(h, w_ref[:, 2 * nq + c * cc:2 * nq + (c + 1) * cc],
                        preferred_element_type=F32)
            v_ref[:, c * cc:(c + 1) * cc] = v.astype(BF16)
        for c in range(E_B // cc):
            arrive(2 + E_B // cc + c)
            g_ref[:, c * cc:(c + 1) * cc] = jnp.dot(
                h, w_ref[:, 2 * nq + E_B + c * cc:2 * nq + E_B + (c + 1) * cc],
                preferred_element_type=F32)

    staged.run(body)


def _ret_in(x_ctx, x_lat, mod3, w_in, j, layer, lat_rows_per_batch):
    m_rows = x_ctx.shape[0] + x_lat.shape[0]
    tm = ROW_TILE
    nq = H_B * DK_B
    n_ctx_tiles = x_ctx.shape[0] // tm
    return pl.pallas_call(
        functools.partial(_ret_in_kernel, n_ctx_tiles=n_ctx_tiles, j=j),
        out_shape=[jax.ShapeDtypeStruct((m_rows, nq), BF16),
                   jax.ShapeDtypeStruct((nq, m_rows), F32),
                   jax.ShapeDtypeStruct((m_rows, E_B), BF16),
                   jax.ShapeDtypeStruct((m_rows, E_B), F32)],
        grid=(m_rows // tm,),
        in_specs=_both_streams_specs(x_ctx, x_lat, tm) + [
            pl.BlockSpec((None, 1, 3 * D_MODEL),
                         _both_streams_mod_map(layer, n_ctx_tiles, tm, lat_rows_per_batch)),
            pl.BlockSpec(memory_space=pl.ANY)],
        out_specs=[pl.BlockSpec((tm, nq), lambda m: (m, 0)),
                   pl.BlockSpec((nq, tm), lambda m: (0, m)),
                   pl.BlockSpec((tm, E_B), lambda m: (m, 0)),
                   pl.BlockSpec((tm, E_B), lambda m: (m, 0))],
        scratch_shapes=_StagedWeight.scratch(w_in, WEIGHT_CHUNK_COLS),
        compiler_params=_params(1),
        name="ret_in",
    )(x_ctx, x_lat, mod3, w_in)


def _retention_kernel(*refs, has_state, heads, fused_out):
    refs = list(refs)
    q_ref, kt_ref, v_ref, g_ref, af_ref, ab_ref = refs[:6]
    del refs[:6]
    if has_state:
        s0f_ref, s0b_ref = refs[:2]
        del refs[:2]
    if fused_out:
        x_ref, mod_ref, w_ref, lng_ref, lnb_ref = refs[:5]
        del refs[:5]
        o_ref = refs.pop(0)
        y_ref = refs.pop()
    else:
        y_ref = refs.pop(0)
    if not has_state:
        sf_ref, sb_ref = refs
    t = q_ref.shape[0]
    nc = t // CHUNK
    row = lax.broadcasted_iota(jnp.int32, (CHUNK, CHUNK), 0).astype(F32)
    col = lax.broadcasted_iota(jnp.int32, (CHUNK, CHUNK), 1).astype(F32)
    diff = row - col
    idx_col = lax.broadcasted_iota(jnp.int32, (CHUNK, 1), 0).astype(F32)
    idx_row = lax.broadcasted_iota(jnp.int32, (1, CHUNK), 1).astype(F32)

    def chunk(c):
        return slice(c * CHUNK, (c + 1) * CHUNK)

    for hh in range(heads):
        qk_cols = slice(hh * DK_B, (hh + 1) * DK_B)
        v_cols = slice(hh * DV_B, (hh + 1) * DV_B)
        lg_f = jnp.log1p(-jnp.exp(af_ref[hh]))
        lg_b = jnp.log1p(-jnp.exp(ab_ref[hh]))
        dmask = (jnp.where(diff >= 0, jnp.exp(jnp.maximum(diff, 0.0) * lg_f), 0.0)
                 + jnp.where(diff <= 0, jnp.exp(jnp.maximum(-diff, 0.0) * lg_b), 0.0))
        qd_f = jnp.exp((idx_col + 1.0) * lg_f)
        qd_b = jnp.exp((CHUNK - idx_col) * lg_b)
        kd_f = jnp.exp((CHUNK - 1.0 - idx_row) * lg_f)
        kd_b = jnp.exp(idx_row * lg_b)
        cd_f = jnp.exp(CHUNK * lg_f)
        cd_b = jnp.exp(CHUNK * lg_b)

        def states(order, kd, cd, s):
            seen = {}
            for n, c in enumerate(order):
                seen[c] = None if s is None else s.astype(BF16)
                if has_state and n == nc - 1:
                    return seen, None
                u = jnp.dot((kt_ref[qk_cols, chunk(c)] * kd).astype(BF16), v_ref[chunk(c), v_cols],
                            preferred_element_type=F32)
                s = u if s is None else s * cd + u
            return seen, s

        seen_f, s_f = states(range(nc), kd_f, cd_f, s0f_ref[hh] if has_state else None)
        seen_b, s_b = states(range(nc - 1, -1, -1), kd_b, cd_b, s0b_ref[hh] if has_state else None)
        if not has_state:
            sf_ref[hh] = s_f
            sb_ref[hh] = s_b
        for c in range(nc):
            qc = q_ref[chunk(c), qk_cols]
            qk = jnp.dot(qc, kt_ref[qk_cols, chunk(c)].astype(BF16), preferred_element_type=F32)
            o = jnp.dot((qk * dmask).astype(BF16), v_ref[chunk(c), v_cols],
                        preferred_element_type=F32)
            if seen_f[c] is not None:
                o = o + jnp.dot(qc, seen_f[c], preferred_element_type=F32) * qd_f
            if seen_b[c] is not None:
                o = o + jnp.dot(qc, seen_b[c], preferred_element_type=F32) * qd_b
            o = o * lax.rsqrt(jnp.mean(o * o, axis=-1, keepdims=True) + RMS_EPS)
            y_ref[chunk(c), v_cols] = (o * _silu(g_ref[chunk(c), v_cols])).astype(BF16)
    if fused_out:
        part = jnp.dot(y_ref[...], w_ref[...].astype(BF16), preferred_element_type=F32)
        gate = mod_ref[:, 2 * D_MODEL:3 * D_MODEL]
        head_steps = H_B // heads
        if head_steps == 1:
            o_ref[...] = _residual_layer_norm(x_ref[...], part, gate, lng_ref[...], lnb_ref[...])
        else:
            step = pl.program_id(1)

            @pl.when(step == 0)
            def _():
                o_ref[...] = part

            @pl.when(jnp.logical_and(step > 0, step < head_steps - 1))
            def _():
                o_ref[...] += part

            @pl.when(step == head_steps - 1)
            def _():
                o_ref[...] = _residual_layer_norm(x_ref[...], o_ref[...] + part, gate, lng_ref[...],
                                                  lnb_ref[...])


def _retention(q, kt, v, g, seq_len, n_seq, first_row, heads, decay_f, decay_b, states,
               out_proj=None):
    t = seq_len
    b = n_seq
    s0 = first_row // t
    has_state = states is not None
    fused_out = out_proj is not None
    q_spec = pl.BlockSpec((t, heads * DK_B), lambda i, h: (s0 + i, h))
    kt_spec = pl.BlockSpec((heads * DK_B, t), lambda i, h: (h, s0 + i))
    vg_spec = pl.BlockSpec((t, heads * DV_B), lambda i, h: (s0 + i, h))
    a_spec = pl.BlockSpec((heads, 1, 1), lambda i, h: (h, 0, 0))
    in_specs = [q_spec, kt_spec, vg_spec, vg_spec, a_spec, a_spec]
    args = [q, kt, v, g, decay_f.reshape(H_B, 1, 1), decay_b.reshape(H_B, 1, 1)]
    out_shape = [jax.ShapeDtypeStruct((b * t, E_B), BF16)]
    out_specs = [pl.BlockSpec((t, heads * DV_B), lambda i, h: (i, h))]
    if has_state:
        s_f, s_b, jj = states
        s_spec = pl.BlockSpec((None, None, heads, DK_B, DV_B), lambda i, h: (i, jj, h, 0, 0))
        in_specs += [s_spec, s_spec]
        args += [s_f, s_b]
    scratch = []
    if fused_out:
        x2d, mod3, mod_map, w_out, jw, ln_g, ln_b = out_proj
        x_spec = pl.BlockSpec((t, D_MODEL), lambda i, h: (i, 0))
        const2 = lambda i, h: (0, 0)
        if heads == H_B:
            w_spec = _weight_spec(w_out, jw)
        else:
            w_spec = pl.BlockSpec((None, heads * DV_B, D_MODEL), lambda i, h: (jw, h, 0))
        in_specs += [x_spec, pl.BlockSpec((None, 1, 3 * D_MODEL), mod_map), w_spec,
                     pl.BlockSpec((1, D_MODEL), const2), pl.BlockSpec((1, D_MODEL), const2)]
        args += [x2d, mod3, w_out, ln_g.reshape(1, D_MODEL), ln_b.reshape(1, D_MODEL)]
        out_shape = [jax.ShapeDtypeStruct((b * t, D_MODEL), F32)]
        out_specs = [x_spec]
        scratch = [pltpu.VMEM((t, heads * DV_B), BF16)]
    if not has_state:
        so_spec = pl.BlockSpec((None, None, heads, DK_B, DV_B), lambda i, h: (i, 0, h, 0, 0))
        out_shape += [jax.ShapeDtypeStruct((b, 1, H_B, DK_B, DV_B), F32)] * 2
        out_specs += [so_spec, so_spec]
    return pl.pallas_call(
        functools.partial(_retention_kernel, has_state=has_state, heads=heads,
                          fused_out=fused_out),
        out_shape=out_shape,
        grid=(b, H_B // heads),
        in_specs=in_specs,
        out_specs=out_specs,
        scratch_shapes=scratch,
        compiler_params=_params(2),
        name="retention_state" if has_state else "retention",
    )(*args)


CONV_ROWS = 1024
CONV_COLS = 256


def _conv_kernel(xc_ref, xl_ref, mod_ref, w_in_ref, cw_ref, w_out_ref, g_ref, b_ref, o_ref, *,
                 n_ctx_tiles, ctx_seq_len, lat_seq_len):
    x = _both_streams_rows(xc_ref, xl_ref, n_ctx_tiles)
    h = _modulated(x, mod_ref)
    rows = x.shape[0]
    is_ctx = pl.program_id(0) < n_ctx_tiles
    row = lax.broadcasted_iota(jnp.int32, (rows, 1), 0)
    pos = jnp.where(is_ctx, row % ctx_seq_len, row % lat_seq_len)
    has_prev = pos > 0
    has_next = pos < jnp.where(is_ctx, ctx_seq_len - 1, lat_seq_len - 1)
    e = D_MODEL
    for c in range(e // CONV_COLS):
        cols = slice(c * CONV_COLS, (c + 1) * CONV_COLS)

        def proj(part):
            lo = part * e + c * CONV_COLS
            return jnp.dot(h, w_in_ref[:, lo:lo + CONV_COLS], preferred_element_type=F32)

        p = proj(1) * proj(2)
        prev = jnp.where(has_prev, pltpu.roll(p, 1, 0), 0.0)
        nxt = jnp.where(has_next, pltpu.roll(p, rows - 1, 0), 0.0)
        conv = prev * cw_ref[0:1, cols] + p * cw_ref[1:2, cols] + nxt * cw_ref[2:3, cols]
        y = (proj(0) * conv * _silu(proj(3))).astype(BF16)
        part = jnp.dot(y, w_out_ref[cols, :].astype(BF16), preferred_element_type=F32)
        if c == 0:
            o_ref[...] = part
        else:
            o_ref[...] += part
    gate = mod_ref[:, 2 * D_MODEL:3 * D_MODEL]
    o_ref[...] = _residual_layer_norm(x, o_ref[...], gate, g_ref[...], b_ref[...])


def _conv_layer(x_ctx, x_lat, mod3, w_in, conv_w, w_out, j, ln_g, ln_b, layer, ctx_seq_len,
                lat_seq_len):
    m_rows = x_ctx.shape[0] + x_lat.shape[0]
    tm = CONV_ROWS
    n_ctx_tiles = x_ctx.shape[0] // tm
    const2 = lambda m: (0, 0)
    return pl.pallas_call(
        functools.partial(_conv_kernel, n_ctx_tiles=n_ctx_tiles, ctx_seq_len=ctx_seq_len,
                          lat_seq_len=lat_seq_len),
        out_shape=jax.ShapeDtypeStruct((m_rows, D_MODEL), F32),
        grid=(m_rows // tm,),
        in_specs=_both_streams_specs(x_ctx, x_lat, tm) + [
                  pl.BlockSpec((None, 1, 3 * D_MODEL),
                               _both_streams_mod_map(layer, n_ctx_tiles, tm, lat_seq_len)),
                  _weight_spec(w_in, j),
                  pl.BlockSpec((None, 3, D_MODEL), lambda m: (j, 0, 0)),
                  _weight_spec(w_out, j),
                  pl.BlockSpec((1, D_MODEL), const2),
                  pl.BlockSpec((1, D_MODEL), const2)],
        out_specs=pl.BlockSpec((tm, D_MODEL), lambda m: (m, 0)),
        compiler_params=_params(1),
        name="conv_layer",
    )(x_ctx, x_lat, mod3, w_in, conv_w, w_out, ln_g.reshape(1, D_MODEL), ln_b.reshape(1, D_MODEL))


def _separate(xp, xs, m_ctx):
    if xp[0] is xs[0]:
        return (xp[0][:m_ctx], 0), (xs[0][m_ctx:], 0)
    return xp, xs


def kernel(x_prompt, x_sample, cache_k, cache_v, state_fwd, state_bwd, c, c_ctx, w_mod, b_mod, ln_g,
           ln_b, w_in_a, lam_a, subln_a, w_out_a, w_in_b, decay_fwd, decay_bwd, w_out_b, w_in_c,
           conv_c, w_out_c):
    bp, tp, d = x_prompt.shape
    bs, ts, _ = x_sample.shape
    cvec = jnp.concatenate([c, c_ctx[None], jnp.zeros((MOD_ROWS - bs - 1, d), F32)], axis=0)
    mod3 = _modulation(cvec, w_mod, b_mod)
    rope_tables = _rope_tables(ts)

    m_ctx, m_lat = bp * tp, bs * ts
    xp = (x_prompt.reshape(m_ctx, d), 0)
    xs = (x_sample.reshape(m_lat, d), 0)
    n_attn = (DEPTH + N_MIXERS - 1) // N_MIXERS
    new_cache_k = new_cache_v = jax.ShapeDtypeStruct((bp, n_attn, tp, H_A, DV_A), F32)
    new_sf, new_sb = [], []
    for i in range(DEPTH):
        kind, j = i % N_MIXERS, i // N_MIXERS
        if kind == 0:
            *qkvz, new_cache_k, new_cache_v = _attn_in(
                xp, xs, m_ctx, m_lat, mod3, w_in_a, j, i, ts, rope_tables,
                (new_cache_k, new_cache_v))
            xp = (_attention(qkvz, 0, m_ctx, xp, tp, None, mod3, lam_a[j], subln_a[j], w_out_a,
                             j, ln_g[i], ln_b[i], i, False), 0)
            xs = (_attention(qkvz, m_ctx, m_lat, xs, ts, (cache_k, cache_v, j), mod3, lam_a[j],
                             subln_a[j], w_out_a, j, ln_g[i], ln_b[i], i, True), 0)
        else:
            xp, xs = _separate(xp, xs, m_ctx)
            if kind == 1:
                q, kt, v, g = _ret_in(xp[0], xs[0], mod3, w_in_b, j, i, ts)
                x_ctx, s_f, s_b = _retention(
                    q, kt, v, g, tp, bp, 0, H_B, decay_fwd[j], decay_bwd[j], None,
                    out_proj=(xp[0], mod3, _mod_row_map(i, None, tp), w_out_b, j, ln_g[i],
                              ln_b[i]))
                new_sf.append(s_f)
                new_sb.append(s_b)
                (x_lat,) = _retention(
                    q, kt, v, g, ts, bs, m_ctx, 1, decay_fwd[j], decay_bwd[j],
                    (state_fwd, state_bwd, j),
                    out_proj=(xs[0], mod3, _mod_row_map(i, ts, ts), w_out_b, j, ln_g[i], ln_b[i]))
                xp, xs = (x_ctx, 0), (x_lat, 0)
            else:
                x_all = _conv_layer(xp[0], xs[0], mod3, w_in_c, conv_c, w_out_c, j, ln_g[i],
                                    ln_b[i], i, tp, ts)
                xp, xs = (x_all, 0), (x_all, m_ctx)
    xp, xs = _separate(xp, xs, m_ctx)
    y_prompt = xp[0].reshape(bp, tp, d)
    y_sample = xs[0].reshape(bs, ts, d)
    new_state_fwd = jnp.concatenate(new_sf, axis=1)
    new_state_bwd = jnp.concatenate(new_sb, axis=1)
    return (y_prompt, y_sample, new_cache_k, new_cache_v, new_state_fwd, new_state_bwd)
```

```python
import functools
import math

import jax
import jax.numpy as jnp
from jax import lax
from jax.experimental import pallas as pl
from jax.experimental.pallas import tpu as pltpu

F32 = jnp.float32
BF16 = jnp.bfloat16

D_MODEL = 1024
DEPTH = 4
N_MIXERS = 3
GRID_W = 64
H_A = 8
DH_A = 64
DV_A = 128
ROPE_HALF = DH_A // 2
SCORE_SCALE = DH_A ** -0.5 * math.log2(math.e)
H_B = 4
DK_B = 256
DV_B = 512
E_B = H_B * DV_B
CHUNK = 256
ALPHA = (2.0 * DEPTH) ** 0.25
ROPE_BASE = 10000.0
LN_EPS = 1e-5
RMS_EPS = 1e-6

MOD_ROWS = 8
CTX_ROW = 4
VMEM_LIMIT_BYTES = 58 * 1024 * 1024
ROW_TILE = 512


def _params(n_axes):
    return pltpu.CompilerParams(dimension_semantics=("arbitrary",) * n_axes,
                                vmem_limit_bytes=VMEM_LIMIT_BYTES)


def _silu(x):
    return x * jax.nn.sigmoid(x)


def _residual_layer_norm(x, out, gate, g, b):
    r = ALPHA * x + gate * out
    mu = jnp.mean(r, axis=-1, keepdims=True)
    d = r - mu
    var = jnp.mean(d * d, axis=-1, keepdims=True)
    return d * lax.rsqrt(var + LN_EPS) * g + b


def _modulated(x, mod_ref):
    shift = mod_ref[:, 0:D_MODEL]
    scale = mod_ref[:, D_MODEL:2 * D_MODEL]
    return x * (1.0 + scale) + shift


def _mod_row_map(layer, rows_per_batch, tile):
    if rows_per_batch is None:
        return lambda m, *_: (layer * MOD_ROWS + CTX_ROW, 0, 0)
    return lambda m, *_: (layer * MOD_ROWS + (m * tile) // rows_per_batch, 0, 0)


def _both_streams_specs(x_ctx, x_lat, tm):
    n_ctx_tiles = x_ctx.shape[0] // tm
    return [pl.BlockSpec((tm, D_MODEL), lambda m: (jnp.minimum(m, n_ctx_tiles - 1), 0)),
            pl.BlockSpec((tm, D_MODEL), lambda m: (jnp.maximum(m - n_ctx_tiles, 0), 0))]


def _both_streams_rows(x_ctx_ref, x_lat_ref, n_ctx_tiles):
    return jnp.where(pl.program_id(0) < n_ctx_tiles, x_ctx_ref[...], x_lat_ref[...])


def _both_streams_mod_map(layer, n_ctx_tiles, tm, lat_rows_per_batch):
    def index_map(m):
        lat_row = ((m - n_ctx_tiles) * tm) // lat_rows_per_batch
        return (layer * MOD_ROWS + jnp.where(m < n_ctx_tiles, CTX_ROW, lat_row), 0, 0)
    return index_map


def _weight_spec(w, j):
    return pl.BlockSpec((None,) + w.shape[1:], lambda *_: (j, 0, 0), pipeline_mode=pl.Buffered(1))


WEIGHT_CHUNK_COLS = 1024
WEIGHT_COPIES_IN_FLIGHT = 2


class _StagedWeight:
    @staticmethod
    def scratch(w, chunk_cols):
        return [pltpu.VMEM(w.shape[1:], w.dtype),
                pltpu.SemaphoreType.DMA((w.shape[2] // chunk_cols,))]

    def __init__(self, w_hbm_ref, w_vmem_ref, sem_ref, j, chunk_cols):
        self.copies = [
            pltpu.make_async_copy(w_hbm_ref.at[j, :, pl.ds(c * chunk_cols, chunk_cols)],
                                  w_vmem_ref.at[:, pl.ds(c * chunk_cols, chunk_cols)], sem_ref.at[c])
            for c in range(w_vmem_ref.shape[1] // chunk_cols)]

    def _arrive(self, c):
        self.copies[c].wait()
        if c + WEIGHT_COPIES_IN_FLIGHT < len(self.copies):
            self.copies[c + WEIGHT_COPIES_IN_FLIGHT].start()

    def run(self, body):
        first = pl.program_id(0) == 0

        @pl.when(first)
        def _():
            for cp in self.copies[:WEIGHT_COPIES_IN_FLIGHT]:
                cp.start()
            body(self._arrive)

        @pl.when(jnp.logical_not(first))
        def _():
            body(lambda c: None)


def _mod_kernel(cv_ref, w_ref, b_ref, o_ref):
    s = _silu(cv_ref[...])
    o_ref[...] = jnp.dot(s, w_ref[...], preferred_element_type=F32) + b_ref[...]


def _modulation(cvec, w_mod, b_mod):
    n = 3 * D_MODEL
    tn = n
    out = pl.pallas_call(
        _mod_kernel,
        out_shape=jax.ShapeDtypeStruct((DEPTH, MOD_ROWS, n), F32),
        grid=(DEPTH, n // tn),
        in_specs=[pl.BlockSpec((MOD_ROWS, D_MODEL), lambda i, j: (0, 0)),
                  pl.BlockSpec((None, D_MODEL, tn), lambda i, j: (i, 0, j)),
                  pl.BlockSpec((None, 1, tn), lambda i, j: (i, 0, j))],
        out_specs=pl.BlockSpec((None, MOD_ROWS, tn), lambda i, j: (i, 0, j)),
        compiler_params=_params(2),
        name="modulation",
    )(cvec, w_mod, b_mod.reshape(DEPTH, 1, n))
    return out.reshape(DEPTH * MOD_ROWS, 1, n)


def _rope(xh, cos4, sin4, first_half):
    swapped = jnp.where(first_half, pltpu.roll(xh, DV_A - ROPE_HALF, 1),
                        pltpu.roll(xh, ROPE_HALF, 1))
    return xh * cos4 + swapped * sin4


def _store_heads(o_ref, x, seq_len):
    for b in range(o_ref.shape[0]):
        xb = x[b * seq_len:(b + 1) * seq_len].reshape(seq_len, H_A, DV_A)
        if len(o_ref.shape) == 4:
            o_ref[b] = xb
        else:
            o_ref[b, 0] = xb
            for s in range(1, o_ref.shape[1]):
                o_ref[b, s] = jnp.zeros_like(xb)


def _attn_in_body(x_ref, mod_ref, w_ref, rope_refs, q_ref, k_ref, vt_ref, z_ref, cache_refs, seq_len):
    h = _modulated(x_ref[...], mod_ref)
    if rope_refs is not None:
        lane = lax.broadcasted_iota(jnp.int32, (1, DV_A), 1)
        first_half = (lane % DH_A) < ROPE_HALF
        cos4 = rope_refs[0][...]
        sin4 = rope_refs[1][...]
    q_all = jnp.dot(h, w_ref[:, 0:D_MODEL], preferred_element_type=F32)
    k_all = jnp.dot(h, w_ref[:, D_MODEL:2 * D_MODEL], preferred_element_type=F32)
    if cache_refs is not None:
        _store_heads(cache_refs[0], k_all, seq_len)
    for hd in range(H_A):
        cols = slice(hd * DV_A, (hd + 1) * DV_A)
        q = q_all[:, cols]
        k = k_all[:, cols]
        if rope_refs is not None:
            q = _rope(q, cos4, sin4, first_half)
            k = _rope(k, cos4, sin4, first_half)
        q_ref[:, cols] = (q * SCORE_SCALE).astype(BF16)
        k_ref[:, cols] = k.astype(BF16)
    v = jnp.dot(h, w_ref[:, 2 * D_MODEL:3 * D_MODEL], preferred_element_type=F32)
    if cache_refs is not None:
        _store_heads(cache_refs[1], v, seq_len)
    vt_ref[...] = v.T.astype(BF16)
    z_ref[...] = jnp.dot(h, w_ref[:, 3 * D_MODEL:4 * D_MODEL], preferred_element_type=F32)


def _attn_in_kernel(x_ref, mod_ref, w_ref, cos_ref, sin_ref, q_ref, k_ref, vt_ref, z_ref):
    _attn_in_body(x_ref, mod_ref, w_ref, (cos_ref, sin_ref), q_ref, k_ref, vt_ref, z_ref, None, None)


def _attn_in(x, m_rows, mod3, w_in, j, layer, rows_per_batch, rope_tables):
    x2d, x_first_row = x
    tm = ROW_TILE
    seq_tiles = rows_per_batch // tm
    row_spec = pl.BlockSpec((tm, D_MODEL), lambda m: (m, 0))
    tab_spec = pl.BlockSpec((tm, DV_A), lambda m: (m % seq_tiles, 0))
    return pl.pallas_call(
        _attn_in_kernel,
        out_shape=[jax.ShapeDtypeStruct((m_rows, D_MODEL), BF16),
                   jax.ShapeDtypeStruct((m_rows, D_MODEL), BF16),
                   jax.ShapeDtypeStruct((D_MODEL, m_rows), BF16),
                   jax.ShapeDtypeStruct((m_rows, D_MODEL), F32)],
        grid=(m_rows // tm,),
        in_specs=[pl.BlockSpec((tm, D_MODEL), lambda m: (x_first_row // tm + m, 0)),
                  pl.BlockSpec((None, 1, 3 * D_MODEL), _mod_row_map(layer, rows_per_batch, tm)),
                  _weight_spec(w_in, j), tab_spec, tab_spec],
        out_specs=[row_spec, row_spec, pl.BlockSpec((D_MODEL, tm), lambda m: (0, m)), row_spec],
        compiler_params=_params(1),
        name="attn_in_rope",
    )(x2d, mod3, w_in, *rope_tables)


ONES_ROWS = 16
CTX_ATTN_SEQS = 2
ATTN_ROWS = 512
ATTN_SUBBLOCKS = 1


def _slab_reduce(op, x):
    parts = [x[i:i + 8] for i in range(0, x.shape[0], 8)]
    while len(parts) > 1:
        parts = [op(parts[i], parts[i + 1]) for i in range(0, len(parts) - 1, 2)] + (
            [parts[-1]] if len(parts) % 2 else [])
    return parts[0]


def _attn_kernel(*refs, layer_idx, has_ctx, n_seq, shared_keys):
    if has_ctx:
        (q_ref, k_ref, vt_ref, z_ref, x_ref, kc_ref, vc_ref, mod_ref, lam_ref, subln_ref,
         w_ref, g_ref, b_ref, o_ref, y_ref, kcb_ref, vct_ref) = refs

        @pl.when(pl.program_id(1) == 0)
        def _():
            n_ctx = kc_ref.shape[0]
            kcb_ref[...] = kc_ref[...].reshape(n_ctx, D_MODEL).astype(BF16)
            vct_ref[...] = vc_ref[...].reshape(n_ctx, D_MODEL).T.astype(BF16)
    else:
        (q_ref, k_ref, vt_ref, z_ref, x_ref, mod_ref, lam_ref, subln_ref,
         w_ref, g_ref, b_ref, o_ref, y_ref) = refs
    tq = q_ref.shape[0] // n_seq
    lam_init = 0.8 - 0.6 * math.exp(-0.3 * layer_idx)
    lm = lam_ref[...]
    lam = (jnp.exp(jnp.sum(lm[0:1] * lm[1:2], axis=-1, keepdims=True))
           - jnp.exp(jnp.sum(lm[2:3] * lm[3:4], axis=-1, keepdims=True)) + lam_init)
    lane = lax.broadcasted_iota(jnp.int32, (1, DV_A), 1)
    first = lane < DH_A
    subln = jnp.broadcast_to(subln_ref[...], (DV_A, tq))
    nt = (((1,), (1,)), ((), ()))
    t = k_ref.shape[0] if shared_keys else k_ref.shape[0] // n_seq

    def key_rows(sq):
        return slice(0, t) if shared_keys else slice(sq * t, (sq + 1) * t)

    def scores(unit):
        sq, hd = unit
        cols = slice(hd * DV_A, (hd + 1) * DV_A)
        qh = q_ref[sq * tq:(sq + 1) * tq, cols]
        zero = jnp.zeros_like(qh)
        qq = jnp.concatenate([jnp.where(first, qh, zero), jnp.where(first, zero, qh)], axis=0)
        parts = [lax.dot_general(k_ref[key_rows(sq), cols], qq, nt, preferred_element_type=F32)]
        if has_ctx:
            parts.append(lax.dot_general(kcb_ref[:, cols], qq, nt, preferred_element_type=F32))
        return parts

    def exps(parts):
        m8 = functools.reduce(jnp.maximum, [_slab_reduce(jnp.maximum, s) for s in parts])
        m = jnp.max(m8, axis=0, keepdims=True)
        return [jnp.exp2(s - m).astype(BF16) for s in parts]

    def with_ones(vals_t):
        return jnp.concatenate([vals_t, jnp.ones((ONES_ROWS, vals_t.shape[1]), BF16)], axis=0)

    def finish(unit, es):
        sq, hd = unit
        cols = slice(hd * DV_A, (hd + 1) * DV_A)
        rows = slice(sq * tq, (sq + 1) * tq)
        acc = jnp.dot(with_ones(vt_ref[cols, key_rows(sq)]), es[0], preferred_element_type=F32)
        if has_ctx:
            acc = acc + jnp.dot(with_ones(vct_ref[cols, :]), es[1], preferred_element_type=F32)
        inv = 1.0 / acc[DV_A:DV_A + 1, :]
        ot = acc[:DV_A, :tq] * inv[:, :tq] - acc[:DV_A, tq:] * (inv[:, tq:] * lam)
        ot = ot * lax.rsqrt(jnp.mean(ot * ot, axis=0, keepdims=True) + RMS_EPS)
        ot = ot * subln * (1.0 - lam_init)
        y_ref[rows, cols] = (ot.T * _silu(z_ref[rows, cols])).astype(BF16)
        if hd == H_A - 1 and (sq == n_seq - 1 or not shared_keys):
            done = slice(0, n_seq * tq) if shared_keys else rows
            out = jnp.dot(y_ref[done, :], w_ref[...].astype(BF16), preferred_element_type=F32)
            gate = mod_ref[:, 2 * D_MODEL:3 * D_MODEL]
            o_ref[done, :] = _residual_layer_norm(x_ref[done, :], out, gate, g_ref[...], b_ref[...])

    units = [(sq, hd) for sq in range(n_seq) for hd in range(H_A)]
    s_ahead = {u: scores(units[u]) for u in range(min(2, len(units)))}
    e_ahead = {0: exps(s_ahead.pop(0))}
    for u in range(len(units)):
        if u + 2 < len(units):
            s_ahead[u + 2] = scores(units[u + 2])
        if u + 1 < len(units):
            e_ahead[u + 1] = exps(s_ahead.pop(u + 1))
        finish(units[u], e_ahead.pop(u))


def _attention(qkvz, first_row, m_rows, x, seq_len, ctx, mod3, lam, subln, w_out, j, ln_g, ln_b,
               layer, per_batch_rows):
    q, k, vt, z = qkvz
    t = seq_len
    b = m_rows // t
    has_ctx = ctx is not None
    if t <= 256 and not per_batch_rows and not has_ctx and b % 2 == 0:
        rows, n_seq, shared_keys, n_kseq = 2 * t, 2, False, 2
    else:
        rows, n_seq, shared_keys, n_kseq = ATTN_ROWS, ATTN_SUBBLOCKS, True, 1
    nq = n_kseq * t // rows
    x, x_first_row = x
    r0, k0 = first_row // rows, first_row // (n_kseq * t)
    x_spec = pl.BlockSpec((rows, D_MODEL), lambda i, j: (x_first_row // rows + i * nq + j, 0))
    q_spec = pl.BlockSpec((rows, D_MODEL), lambda i, j: (r0 + i * nq + j, 0))
    o_spec = pl.BlockSpec((rows, D_MODEL), lambda i, j: (i * nq + j, 0))
    k_spec = pl.BlockSpec((n_kseq * t, D_MODEL), lambda i, j: (k0 + i, 0))
    vt_spec = pl.BlockSpec((D_MODEL, n_kseq * t), lambda i, j: (0, k0 + i))
    if per_batch_rows:
        mod_map = lambda i, j: (layer * MOD_ROWS + i, 0, 0)
    else:
        mod_map = lambda i, j: (layer * MOD_ROWS + CTX_ROW, 0, 0)
    const2 = lambda i, j: (0, 0)
    in_specs = [q_spec, k_spec, vt_spec, q_spec, x_spec]
    args = [q, k, vt, z, x]
    scratch = [pltpu.VMEM((rows, D_MODEL), BF16)]
    if has_ctx:
        cache_k, cache_v, jj = ctx
        n_ctx = cache_k.shape[2]
        scratch += [pltpu.VMEM((n_ctx, D_MODEL), BF16), pltpu.VMEM((D_MODEL, n_ctx), BF16)]
        c_spec = pl.BlockSpec((None, None, n_ctx, H_A, DV_A),
                              lambda i, j: (i, jj, 0, 0, 0))
        in_specs += [c_spec, c_spec]
        args += [cache_k, cache_v]
    in_specs += [pl.BlockSpec((None, 1, 3 * D_MODEL), mod_map),
                 pl.BlockSpec((4, DH_A), const2),
                 pl.BlockSpec((DV_A, 1), const2),
                 _weight_spec(w_out, j),
                 pl.BlockSpec((1, D_MODEL), const2),
                 pl.BlockSpec((1, D_MODEL), const2)]
    args += [mod3, lam, subln.reshape(DV_A, 1), w_out, ln_g.reshape(1, D_MODEL),
             ln_b.reshape(1, D_MODEL)]
    return pl.pallas_call(
        functools.partial(_attn_kernel, layer_idx=layer, has_ctx=has_ctx, n_seq=n_seq,
                          shared_keys=shared_keys),
        out_shape=jax.ShapeDtypeStruct((m_rows, D_MODEL), F32),
        grid=(b // n_kseq, nq),
        in_specs=in_specs,
        out_specs=o_spec,
        scratch_shapes=scratch,
        compiler_params=_params(2),
        name="diff_attn_ctx" if has_ctx else "diff_attn",
    )(*args)


def _ctx_attn_layer_kernel(*refs, layer_idx, seq_len, n_aliased):
    x_ref, mod_ref, w_in_ref, lam_ref, subln_ref, w_out_ref, g_ref, b_ref = refs[:8]
    o_ref, ck_ref, cv_ref, q_ref, k_ref, vt_ref, z_ref, y_ref = refs[8 + n_aliased:]
    _attn_in_body(x_ref, mod_ref, w_in_ref, None, q_ref, k_ref, vt_ref, z_ref, (ck_ref, cv_ref),
                  seq_len)
    _attn_kernel(q_ref, k_ref, vt_ref, z_ref, x_ref, mod_ref, lam_ref, subln_ref, w_out_ref, g_ref,
                 b_ref, o_ref, y_ref, layer_idx=layer_idx, has_ctx=False,
                 n_seq=x_ref.shape[0] // seq_len, shared_keys=False)


def _ctx_attention_layer(x, m_rows, seq_len, mod3, w_in, w_out, j, lam, subln, ln_g, ln_b, layer,
                         cache_out):
    x2d, x_first_row = x
    rows = CTX_ATTN_SEQS * seq_len
    const2 = lambda m: (0, 0)
    in_specs = [pl.BlockSpec((rows, D_MODEL), lambda m: (x_first_row // rows + m, 0)),
                pl.BlockSpec((None, 1, 3 * D_MODEL), lambda m: (layer * MOD_ROWS + CTX_ROW, 0, 0)),
                _weight_spec(w_in, j),
                pl.BlockSpec((4, DH_A), const2),
                pl.BlockSpec((DV_A, 1), const2),
                _weight_spec(w_out, j),
                pl.BlockSpec((1, D_MODEL), const2),
                pl.BlockSpec((1, D_MODEL), const2)]
    args = [x2d, mod3, w_in, lam, subln.reshape(DV_A, 1), w_out, ln_g.reshape(1, D_MODEL),
            ln_b.reshape(1, D_MODEL)]
    new_k, new_v = cache_out
    cache_shape = jax.ShapeDtypeStruct(new_k.shape, F32)
    aliases = {}
    if j == 0:
        cache_spec = pl.BlockSpec((CTX_ATTN_SEQS, new_k.shape[1], seq_len, H_A, DV_A),
                                  lambda m: (m, 0, 0, 0, 0))
    else:
        cache_spec = pl.BlockSpec((CTX_ATTN_SEQS, None, seq_len, H_A, DV_A),
                                  lambda m: (m, j, 0, 0, 0))
        in_specs += [pl.BlockSpec(memory_space=pl.ANY)] * 2
        args += [new_k, new_v]
        aliases = {len(args) - 2: 1, len(args) - 1: 2}
    return pl.pallas_call(
        functools.partial(_ctx_attn_layer_kernel, layer_idx=layer, seq_len=seq_len,
                          n_aliased=len(aliases)),
        out_shape=[jax.ShapeDtypeStruct((m_rows, D_MODEL), F32), cache_shape, cache_shape],
        grid=(m_rows // rows,),
        in_specs=in_specs,
        out_specs=[pl.BlockSpec((rows, D_MODEL), lambda m: (m, 0)), cache_spec, cache_spec],
        scratch_shapes=[pltpu.VMEM((rows, D_MODEL), BF16), pltpu.VMEM((rows, D_MODEL), BF16),
                        pltpu.VMEM((D_MODEL, rows), BF16), pltpu.VMEM((rows, D_MODEL), F32),
                        pltpu.VMEM((rows, D_MODEL), BF16)],
        input_output_aliases=aliases,
        compiler_params=_params(1),
        name="ctx_attn_layer",
    )(*args)


def _rope_tables(n_tokens):
    rows = n_tokens // GRID_W
    r = jnp.repeat(jnp.arange(rows, dtype=F32), GRID_W)
    col = jnp.tile(jnp.arange(GRID_W, dtype=F32), rows)
    n_freq = DH_A // 4
    inv = ROPE_BASE ** (-jnp.arange(n_freq, dtype=F32) / n_freq)
    ang = jnp.concatenate([r[:, None] * inv, col[:, None] * inv], -1)
    cos, sin = jnp.cos(ang), jnp.sin(ang)
    return jnp.tile(cos, (1, 4)), jnp.concatenate([-sin, sin, -sin, sin], -1)


def _ret_in_kernel(xc_ref, xl_ref, mod_ref, w_hbm_ref, q_ref, kt_ref, v_ref, g_ref, w_ref, sem_ref, *,
                   n_ctx_tiles, j):
    nq = H_B * DK_B
    cc = WEIGHT_CHUNK_COLS
    assert nq == cc and E_B % cc == 0
    staged = _StagedWeight(w_hbm_ref, w_ref, sem_ref, j, cc)

    def body(arrive):
        h = _modulated(_both_streams_rows(xc_ref, xl_ref, n_ctx_tiles), mod_ref)
        arrive(0)
        q_ref[...] = jnp.dot(h, w_ref[:, 0:nq], preferred_element_type=F32).astype(BF16)
        arrive(1)
        k = jnp.dot(h, w_ref[:, nq:2 * nq], preferred_element_type=F32) * (DK_B ** -0.5)
        kt_ref[...] = k.T
        for c in range(E_B // cc):
            arrive(2 + c)
            v = jnp.dot(h, w_ref[:, 2 * nq + c * cc:2 * nq + (c + 1) * cc],
                        preferred_element_type=F32)
            v_ref[:, c * cc:(c + 1) * cc] = v.astype(BF16)
        for c in range(E_B // cc):
            arrive(2 + E_B // cc + c)
            g_ref[:, c * cc:(c + 1) * cc] = jnp.dot(
                h, w_ref[:, 2 * nq + E_B + c * cc:2 * nq + E_B + (c + 1) * cc],
                preferred_element_type=F32)

    staged.run(body)


def _ret_in(x_ctx, x_lat, mod3, w_in, j, layer, lat_rows_per_batch):
    m_rows = x_ctx.shape[0] + x_lat.shape[0]
    tm = ROW_TILE
    nq = H_B * DK_B
    n_ctx_tiles = x_ctx.shape[0] // tm
    return pl.pallas_call(
        functools.partial(_ret_in_kernel, n_ctx_tiles=n_ctx_tiles, j=j),
        out_shape=[jax.ShapeDtypeStruct((m_rows, nq), BF16),
                   jax.ShapeDtypeStruct((nq, m_rows), F32),
                   jax.ShapeDtypeStruct((m_rows, E_B), BF16),
                   jax.ShapeDtypeStruct((m_rows, E_B), F32)],
        grid=(m_rows // tm,),
        in_specs=_both_streams_specs(x_ctx, x_lat, tm) + [
            pl.BlockSpec((None, 1, 3 * D_MODEL),
                         _both_streams_mod_map(layer, n_ctx_tiles, tm, lat_rows_per_batch)),
            pl.BlockSpec(memory_space=pl.ANY)],
        out_specs=[pl.BlockSpec((tm, nq), lambda m: (m, 0)),
                   pl.BlockSpec((nq, tm), lambda m: (0, m)),
                   pl.BlockSpec((tm, E_B), lambda m: (m, 0)),
                   pl.BlockSpec((tm, E_B), lambda m: (m, 0))],
        scratch_shapes=_StagedWeight.scratch(w_in, WEIGHT_CHUNK_COLS),
        compiler_params=_params(1),
        name="ret_in",
    )(x_ctx, x_lat, mod3, w_in)


def _retention_kernel(*refs, has_state, heads, fused_out):
    refs = list(refs)
    q_ref, kt_ref, v_ref, g_ref, af_ref, ab_ref = refs[:6]
    del refs[:6]
    if has_state:
        s0f_ref, s0b_ref = refs[:2]
        del refs[:2]
    if fused_out:
        x_ref, mod_ref, w_ref, lng_ref, lnb_ref = refs[:5]
        del refs[:5]
        o_ref = refs.pop(0)
        y_ref = refs.pop()
    else:
        y_ref = refs.pop(0)
    if not has_state:
        sf_ref, sb_ref = refs
    t = q_ref.shape[0]
    nc = t // CHUNK
    row = lax.broadcasted_iota(jnp.int32, (CHUNK, CHUNK), 0).astype(F32)
    col = lax.broadcasted_iota(jnp.int32, (CHUNK, CHUNK), 1).astype(F32)
    diff = row - col
    idx_col = lax.broadcasted_iota(jnp.int32, (CHUNK, 1), 0).astype(F32)
    idx_row = lax.broadcasted_iota(jnp.int32, (1, CHUNK), 1).astype(F32)

    def chunk(c):
        return slice(c * CHUNK, (c + 1) * CHUNK)

    for hh in range(heads):
        qk_cols = slice(hh * DK_B, (hh + 1) * DK_B)
        v_cols = slice(hh * DV_B, (hh + 1) * DV_B)
        lg_f = jnp.log1p(-jnp.exp(af_ref[hh]))
        lg_b = jnp.log1p(-jnp.exp(ab_ref[hh]))
        dmask = (jnp.where(diff >= 0, jnp.exp(jnp.maximum(diff, 0.0) * lg_f), 0.0)
                 + jnp.where(diff <= 0, jnp.exp(jnp.maximum(-diff, 0.0) * lg_b), 0.0))
        qd_f = jnp.exp((idx_col + 1.0) * lg_f)
        qd_b = jnp.exp((CHUNK - idx_col) * lg_b)
        kd_f = jnp.exp((CHUNK - 1.0 - idx_row) * lg_f)
        kd_b = jnp.exp(idx_row * lg_b)
        cd_f = jnp.exp(CHUNK * lg_f)
        cd_b = jnp.exp(CHUNK * lg_b)

        def states(order, kd, cd, s):
            seen = {}
            for n, c in enumerate(order):
                seen[c] = None if s is None else s.astype(BF16)
                if has_state and n == nc - 1:
                    return seen, None
                u = jnp.dot((kt_ref[qk_cols, chunk(c)] * kd).astype(BF16), v_ref[chunk(c), v_cols],
                            preferred_element_type=F32)
                s = u if s is None else s * cd + u
            return seen, s

        seen_f, s_f = states(range(nc), kd_f, cd_f, s0f_ref[hh] if has_state else None)
        seen_b, s_b = states(range(nc - 1, -1, -1), kd_b, cd_b, s0b_ref[hh] if has_state else None)
        if not has_state:
            sf_ref[hh] = s_f
            sb_ref[hh] = s_b
        for c in range(nc):
            qc = q_ref[chunk(c), qk_cols]
            qk = jnp.dot(qc, kt_ref[qk_cols, chunk(c)].astype(BF16), preferred_element_type=F32)
            o = jnp.dot((qk * dmask).astype(BF16), v_ref[chunk(c), v_cols],
                        preferred_element_type=F32)
            if seen_f[c] is not None:
                o = o + jnp.dot(qc, seen_f[c], preferred_element_type=F32) * qd_f
            if seen_b[c] is not None:
                o = o + jnp.dot(qc, seen_b[c], preferred_element_type=F32) * qd_b
            o = o * lax.rsqrt(jnp.mean(o * o, axis=-1, keepdims=True) + RMS_EPS)
            y_ref[chunk(c), v_cols] = (o * _silu(g_ref[chunk(c), v_cols])).astype(BF16)
    if fused_out:
        part = jnp.dot(y_ref[...], w_ref[...].astype(BF16), preferred_element_type=F32)
        gate = mod_ref[:, 2 * D_MODEL:3 * D_MODEL]
        head_steps = H_B // heads
        if head_steps == 1:
            o_ref[...] = _residual_layer_norm(x_ref[...], part, gate, lng_ref[...], lnb_ref[...])
        else:
            step = pl.program_id(1)

            @pl.when(step == 0)
            def _():
                o_ref[...] = part

            @pl.when(jnp.logical_and(step > 0, step < head_steps - 1))
            def _():
                o_ref[...] += part

            @pl.when(step == head_steps - 1)
            def _():
                o_ref[...] = _residual_layer_norm(x_ref[...], o_ref[...] + part, gate, lng_ref[...],
                                                  lnb_ref[...])


def _retention(q, kt, v, g, seq_len, n_seq, first_row, heads, decay_f, decay_b, states,
               out_proj=None):
    t = seq_len
    b = n_seq
    s0 = first_row // t
    has_state = states is not None
    fused_out = out_proj is not None
    q_spec = pl.BlockSpec((t, heads * DK_B), lambda i, h: (s0 + i, h))
    kt_spec = pl.BlockSpec((heads * DK_B, t), lambda i, h: (h, s0 + i))
    vg_spec = pl.BlockSpec((t, heads * DV_B), lambda i, h: (s0 + i, h))
    a_spec = pl.BlockSpec((heads, 1, 1), lambda i, h: (h, 0, 0))
    in_specs = [q_spec, kt_spec, vg_spec, vg_spec, a_spec, a_spec]
    args = [q, kt, v, g, decay_f.reshape(H_B, 1, 1), decay_b.reshape(H_B, 1, 1)]
    out_shape = [jax.ShapeDtypeStruct((b * t, E_B), BF16)]
    out_specs = [pl.BlockSpec((t, heads * DV_B), lambda i, h: (i, h))]
    if has_state:
        s_f, s_b, jj = states
        s_spec = pl.BlockSpec((None, None, heads, DK_B, DV_B), lambda i, h: (i, jj, h, 0, 0))
        in_specs += [s_spec, s_spec]
        args += [s_f, s_b]
    scratch = []
    if fused_out:
        x2d, mod3, mod_map, w_out, jw, ln_g, ln_b = out_proj
        x_spec = pl.BlockSpec((t, D_MODEL), lambda i, h: (i, 0))
        const2 = lambda i, h: (0, 0)
        if heads == H_B:
            w_spec = _weight_spec(w_out, jw)
        else:
            w_spec = pl.BlockSpec((None, heads * DV_B, D_MODEL), lambda i, h: (jw, h, 0))
        in_specs += [x_spec, pl.BlockSpec((None, 1, 3 * D_MODEL), mod_map), w_spec,
                     pl.BlockSpec((1, D_MODEL), const2), pl.BlockSpec((1, D_MODEL), const2)]
        args += [x2d, mod3, w_out, ln_g.reshape(1, D_MODEL), ln_b.reshape(1, D_MODEL)]
        out_shape = [jax.ShapeDtypeStruct((b * t, D_MODEL), F32)]
        out_specs = [x_spec]
        scratch = [pltpu.VMEM((t, heads * DV_B), BF16)]
    if not has_state:
        so_spec = pl.BlockSpec((None, None, heads, DK_B, DV_B), lambda i, h: (i, 0, h, 0, 0))
        out_shape += [jax.ShapeDtypeStruct((b, 1, H_B, DK_B, DV_B), F32)] * 2
        out_specs += [so_spec, so_spec]
    return pl.pallas_call(
        functools.partial(_retention_kernel, has_state=has_state, heads=heads,
                          fused_out=fused_out),
        out_shape=out_shape,
        grid=(b, H_B // heads),
        in_specs=in_specs,
        out_specs=out_specs,
        scratch_shapes=scratch,
        compiler_params=_params(2),
        name="retention_state" if has_state else "retention",
    )(*args)


CONV_ROWS = 1024
CONV_COLS = 256


def _conv_kernel(xc_ref, xl_ref, mod_ref, w_in_ref, cw_ref, w_out_ref, g_ref, b_ref, o_ref, *,
                 n_ctx_tiles, ctx_seq_len, lat_seq_len):
    x = _both_streams_rows(xc_ref, xl_ref, n_ctx_tiles)
    h = _modulated(x, mod_ref)
    rows = x.shape[0]
    is_ctx = pl.program_id(0) < n_ctx_tiles
    row = lax.broadcasted_iota(jnp.int32, (rows, 1), 0)
    pos = jnp.where(is_ctx, row % ctx_seq_len, row % lat_seq_len)
    has_prev = pos > 0
    has_next = pos < jnp.where(is_ctx, ctx_seq_len - 1, lat_seq_len - 1)
    e = D_MODEL
    for c in range(e // CONV_COLS):
        cols = slice(c * CONV_COLS, (c + 1) * CONV_COLS)

        def proj(part):
            lo = part * e + c * CONV_COLS
            return jnp.dot(h, w_in_ref[:, lo:lo + CONV_COLS], preferred_element_type=F32)

        p = proj(1) * proj(2)
        prev = jnp.where(has_prev, pltpu.roll(p, 1, 0), 0.0)
        nxt = jnp.where(has_next, pltpu.roll(p, rows - 1, 0), 0.0)
        conv = prev * cw_ref[0:1, cols] + p * cw_ref[1:2, cols] + nxt * cw_ref[2:3, cols]
        y = (proj(0) * conv * _silu(proj(3))).astype(BF16)
        part = jnp.dot(y, w_out_ref[cols, :].astype(BF16), preferred_element_type=F32)
        if c == 0:
            o_ref[...] = part
        else:
            o_ref[...] += part
    gate = mod_ref[:, 2 * D_MODEL:3 * D_MODEL]
    o_ref[...] = _residual_layer_norm(x, o_ref[...], gate, g_ref[...], b_ref[...])


def _conv_layer(x_ctx, x_lat, mod3, w_in, conv_w, w_out, j, ln_g, ln_b, layer, ctx_seq_len,
                lat_seq_len):
    m_rows = x_ctx.shape[0] + x_lat.shape[0]
    tm = CONV_ROWS
    n_ctx_tiles = x_ctx.shape[0] // tm
    const2 = lambda m: (0, 0)
    return pl.pallas_call(
        functools.partial(_conv_kernel, n_ctx_tiles=n_ctx_tiles, ctx_seq_len=ctx_seq_len,
                          lat_seq_len=lat_seq_len),
        out_shape=jax.ShapeDtypeStruct((m_rows, D_MODEL), F32),
        grid=(m_rows // tm,),
        in_specs=_both_streams_specs(x_ctx, x_lat, tm) + [
                  pl.BlockSpec((None, 1, 3 * D_MODEL),
                               _both_streams_mod_map(layer, n_ctx_tiles, tm, lat_seq_len)),
                  _weight_spec(w_in, j),
                  pl.BlockSpec((None, 3, D_MODEL), lambda m: (j, 0, 0)),
                  _weight_spec(w_out, j),
                  pl.BlockSpec((1, D_MODEL), const2),
                  pl.BlockSpec((1, D_MODEL), const2)],
        out_specs=pl.BlockSpec((tm, D_MODEL), lambda m: (m, 0)),
        compiler_params=_params(1),
        name="conv_layer",
    )(x_ctx, x_lat, mod3, w_in, conv_w, w_out, ln_g.reshape(1, D_MODEL), ln_b.reshape(1, D_MODEL))


def _separate(xp, xs, m_ctx):
    if xp[0] is xs[0]:
        return (xp[0][:m_ctx], 0), (xs[0][m_ctx:], 0)
    return xp, xs


def kernel(x_prompt, x_sample, cache_k, cache_v, state_fwd, state_bwd, c, c_ctx, w_mod, b_mod, ln_g,
           ln_b, w_in_a, lam_a, subln_a, w_out_a, w_in_b, decay_fwd, decay_bwd, w_out_b, w_in_c,
           conv_c, w_out_c):
    bp, tp, d = x_prompt.shape
    bs, ts, _ = x_sample.shape
    cvec = jnp.concatenate([c, c_ctx[None], jnp.zeros((MOD_ROWS - bs - 1, d), F32)], axis=0)
    mod3 = _modulation(cvec, w_mod, b_mod)
    rope_tables = _rope_tables(ts)

    m_ctx, m_lat = bp * tp, bs * ts
    xp = (x_prompt.reshape(m_ctx, d), 0)
    xs = (x_sample.reshape(m_lat, d), 0)
    n_attn = (DEPTH + N_MIXERS - 1) // N_MIXERS
    new_cache_k = new_cache_v = jax.ShapeDtypeStruct((bp, n_attn, tp, H_A, DV_A), F32)
    new_sf, new_sb = [], []
    for i in range(DEPTH):
        kind, j = i % N_MIXERS, i // N_MIXERS
        if kind == 0:
            x_ctx, new_cache_k, new_cache_v = _ctx_attention_layer(
                xp, m_ctx, tp, mod3, w_in_a, w_out_a, j, lam_a[j], subln_a[j], ln_g[i], ln_b[i], i,
                (new_cache_k, new_cache_v))
            qkvz = _attn_in(xs, m_lat, mod3, w_in_a, j, i, ts, rope_tables)
            xs = (_attention(qkvz, 0, m_lat, xs, ts, (cache_k, cache_v, j), mod3, lam_a[j],
                             subln_a[j], w_out_a, j, ln_g[i], ln_b[i], i, True), 0)
            xp = (x_ctx, 0)
        else:
            xp, xs = _separate(xp, xs, m_ctx)
            if kind == 1:
                q, kt, v, g = _ret_in(xp[0], xs[0], mod3, w_in_b, j, i, ts)
                x_ctx, s_f, s_b = _retention(
                    q, kt, v, g, tp, bp, 0, H_B, decay_fwd[j], decay_bwd[j], None,
                    out_proj=(xp[0], mod3, _mod_row_map(i, None, tp), w_out_b, j, ln_g[i],
                              ln_b[i]))
                new_sf.append(s_f)
                new_sb.append(s_b)
                (x_lat,) = _retention(
                    q, kt, v, g, ts, bs, m_ctx, 1, decay_fwd[j], decay_bwd[j],
                    (state_fwd, state_bwd, j),
                    out_proj=(xs[0], mod3, _mod_row_map(i, ts, ts), w_out_b, j, ln_g[i], ln_b[i]))
                xp, xs = (x_ctx, 0), (x_lat, 0)
            else:
                x_all = _conv_layer(xp[0], xs[0], mod3, w_in_c, conv_c, w_out_c, j, ln_g[i],
                                    ln_b[i], i, tp, ts)
                xp, xs = (x_all, 0), (x_all, m_ctx)
    xp, xs = _separate(xp, xs, m_ctx)
    y_prompt = xp[0].reshape(bp, tp, d)
    y_sample = xs[0].reshape(bs, ts, d)
    new_state_fwd = jnp.concatenate(new_sf, axis=1)
    new_state_bwd = jnp.concatenate(new_sb, axis=1)
    return (y_prompt, y_sample, new_cache_k, new_cache_v, new_state_fwd, new_state_bwd)
```

```python
import functools
import math

import jax
import jax.numpy as jnp
from jax import lax
from jax.experimental import pallas as pl
from jax.experimental.pallas import tpu as pltpu

F32 = jnp.float32
BF16 = jnp.bfloat16

D_MODEL = 1024
DEPTH = 4
N_MIXERS = 3
GRID_W = 64
H_A = 8
DH_A = 64
DV_A = 128
ROPE_HALF = DH_A // 2
SCORE_SCALE = DH_A ** -0.5 * math.log2(math.e)
H_B = 4
DK_B = 256
DV_B = 512
E_B = H_B * DV_B
CHUNK = 256
ALPHA = (2.0 * DEPTH) ** 0.25
ROPE_BASE = 10000.0
LN_EPS = 1e-5
RMS_EPS = 1e-6

MOD_ROWS = 8
CTX_ROW = 4
VMEM_LIMIT_BYTES = 58 * 1024 * 1024
ROW_TILE = 512


def _params(n_axes):
    return pltpu.CompilerParams(dimension_semantics=("arbitrary",) * n_axes,
                                vmem_limit_bytes=VMEM_LIMIT_BYTES)


def _silu(x):
    return x * jax.nn.sigmoid(x)


def _residual_layer_norm(x, out, gate, g, b):
    r = ALPHA * x + gate * out
    mu = jnp.mean(r, axis=-1, keepdims=True)
    d = r - mu
    var = jnp.mean(d * d, axis=-1, keepdims=True)
    return d * lax.rsqrt(var + LN_EPS) * g + b


def _modulated(x, mod_ref):
    shift = mod_ref[:, 0:D_MODEL]
    scale = mod_ref[:, D_MODEL:2 * D_MODEL]
    return x * (1.0 + scale) + shift


def _mod_row_map(layer, rows_per_batch, tile):
    if rows_per_batch is None:
        return lambda m, *_: (layer * MOD_ROWS + CTX_ROW, 0, 0)
    return lambda m, *_: (layer * MOD_ROWS + (m * tile) // rows_per_batch, 0, 0)


def _both_streams_specs(x_ctx, x_lat, tm):
    n_ctx_tiles = x_ctx.shape[0] // tm
    return [pl.BlockSpec((tm, D_MODEL), lambda m: (jnp.minimum(m, n_ctx_tiles - 1), 0)),
            pl.BlockSpec((tm, D_MODEL), lambda m: (jnp.maximum(m - n_ctx_tiles, 0), 0))]


def _both_streams_rows(x_ctx_ref, x_lat_ref, n_ctx_tiles):
    return jnp.where(pl.program_id(0) < n_ctx_tiles, x_ctx_ref[...], x_lat_ref[...])


def _both_streams_mod_map(layer, n_ctx_tiles, tm, lat_rows_per_batch):
    def index_map(m):
        lat_row = ((m - n_ctx_tiles) * tm) // lat_rows_per_batch
        return (layer * MOD_ROWS + jnp.where(m < n_ctx_tiles, CTX_ROW, lat_row), 0, 0)
    return index_map


def _weight_spec(w, j):
    return pl.BlockSpec((None,) + w.shape[1:], lambda *_: (j, 0, 0), pipeline_mode=pl.Buffered(1))


WEIGHT_CHUNK_COLS = 1024
WEIGHT_COPIES_IN_FLIGHT = 2


class _StagedWeight:
    @staticmethod
    def scratch(w, chunk_cols):
        return [pltpu.VMEM(w.shape[1:], w.dtype),
                pltpu.SemaphoreType.DMA((w.shape[2] // chunk_cols,))]

    def __init__(self, w_hbm_ref, w_vmem_ref, sem_ref, j, chunk_cols):
        self.copies = [
            pltpu.make_async_copy(w_hbm_ref.at[j, :, pl.ds(c * chunk_cols, chunk_cols)],
                                  w_vmem_ref.at[:, pl.ds(c * chunk_cols, chunk_cols)], sem_ref.at[c])
            for c in range(w_vmem_ref.shape[1] // chunk_cols)]

    def _arrive(self, c):
        self.copies[c].wait()
        if c + WEIGHT_COPIES_IN_FLIGHT < len(self.copies):
            self.copies[c + WEIGHT_COPIES_IN_FLIGHT].start()

    def run(self, body):
        first = pl.program_id(0) == 0

        @pl.when(first)
        def _():
            for cp in self.copies[:WEIGHT_COPIES_IN_FLIGHT]:
                cp.start()
            body(self._arrive)

        @pl.when(jnp.logical_not(first))
        def _():
            body(lambda c: None)


def _mod_kernel(cv_ref, w_ref, b_ref, o_ref):
    s = _silu(cv_ref[...])
    o_ref[...] = jnp.dot(s, w_ref[...], preferred_element_type=F32) + b_ref[...]


def _modulation(cvec, w_mod, b_mod):
    n = 3 * D_MODEL
    tn = n
    out = pl.pallas_call(
        _mod_kernel,
        out_shape=jax.ShapeDtypeStruct((DEPTH, MOD_ROWS, n), F32),
        grid=(DEPTH, n // tn),
        in_specs=[pl.BlockSpec((MOD_ROWS, D_MODEL), lambda i, j: (0, 0)),
                  pl.BlockSpec((None, D_MODEL, tn), lambda i, j: (i, 0, j)),
                  pl.BlockSpec((None, 1, tn), lambda i, j: (i, 0, j))],
        out_specs=pl.BlockSpec((None, MOD_ROWS, tn), lambda i, j: (i, 0, j)),
        compiler_params=_params(2),
        name="modulation",
    )(cvec, w_mod, b_mod.reshape(DEPTH, 1, n))
    return out.reshape(DEPTH * MOD_ROWS, 1, n)


def _rope(xh, cos4, sin4, first_half):
    swapped = jnp.where(first_half, pltpu.roll(xh, DV_A - ROPE_HALF, 1),
                        pltpu.roll(xh, ROPE_HALF, 1))
    return xh * cos4 + swapped * sin4


def _store_heads(o_ref, x, seq_len, head0=0):
    heads = slice(head0, head0 + x.shape[1] // DV_A)
    for b in range(o_ref.shape[0]):
        xb = x[b * seq_len:(b + 1) * seq_len].reshape(seq_len, x.shape[1] // DV_A, DV_A)
        if len(o_ref.shape) == 4:
            o_ref[b, :, heads, :] = xb
        else:
            o_ref[b, 0, :, heads, :] = xb
            for s in range(1, o_ref.shape[1]):
                o_ref[b, s, :, heads, :] = jnp.zeros_like(xb)


def _attn_in_body(x_ref, mod_ref, w_ref, rope_refs, q_ref, k_ref, vt_ref, z_ref, cache_refs, seq_len):
    h = _modulated(x_ref[...], mod_ref)
    if rope_refs is not None:
        lane = lax.broadcasted_iota(jnp.int32, (1, DV_A), 1)
        first_half = (lane % DH_A) < ROPE_HALF
        cos4 = rope_refs[0][...]
        sin4 = rope_refs[1][...]
    q_all = jnp.dot(h, w_ref[:, 0:D_MODEL], preferred_element_type=F32)
    k_all = jnp.dot(h, w_ref[:, D_MODEL:2 * D_MODEL], preferred_element_type=F32)
    if cache_refs is not None:
        _store_heads(cache_refs[0], k_all, seq_len)
    for hd in range(H_A):
        cols = slice(hd * DV_A, (hd + 1) * DV_A)
        q = q_all[:, cols]
        k = k_all[:, cols]
        if rope_refs is not None:
            q = _rope(q, cos4, sin4, first_half)
            k = _rope(k, cos4, sin4, first_half)
        q_ref[:, cols] = (q * SCORE_SCALE).astype(BF16)
        k_ref[:, cols] = k.astype(BF16)
    v = jnp.dot(h, w_ref[:, 2 * D_MODEL:3 * D_MODEL], preferred_element_type=F32)
    if cache_refs is not None:
        _store_heads(cache_refs[1], v, seq_len)
    vt_ref[...] = v.T.astype(BF16)
    z_ref[...] = jnp.dot(h, w_ref[:, 3 * D_MODEL:4 * D_MODEL], preferred_element_type=F32)


def _attn_in_kernel(x_ref, mod_ref, w_ref, cos_ref, sin_ref, q_ref, k_ref, vt_ref, z_ref):
    _attn_in_body(x_ref, mod_ref, w_ref, (cos_ref, sin_ref), q_ref, k_ref, vt_ref, z_ref, None, None)


def _attn_in(x, m_rows, mod3, w_in, j, layer, rows_per_batch, rope_tables):
    x2d, x_first_row = x
    tm = ROW_TILE
    seq_tiles = rows_per_batch // tm
    row_spec = pl.BlockSpec((tm, D_MODEL), lambda m: (m, 0))
    tab_spec = pl.BlockSpec((tm, DV_A), lambda m: (m % seq_tiles, 0))
    return pl.pallas_call(
        _attn_in_kernel,
        out_shape=[jax.ShapeDtypeStruct((m_rows, D_MODEL), BF16),
                   jax.ShapeDtypeStruct((m_rows, D_MODEL), BF16),
                   jax.ShapeDtypeStruct((D_MODEL, m_rows), BF16),
                   jax.ShapeDtypeStruct((m_rows, D_MODEL), F32)],
        grid=(m_rows // tm,),
        in_specs=[pl.BlockSpec((tm, D_MODEL), lambda m: (x_first_row // tm + m, 0)),
                  pl.BlockSpec((None, 1, 3 * D_MODEL), _mod_row_map(layer, rows_per_batch, tm)),
                  _weight_spec(w_in, j), tab_spec, tab_spec],
        out_specs=[row_spec, row_spec, pl.BlockSpec((D_MODEL, tm), lambda m: (0, m)), row_spec],
        compiler_params=_params(1),
        name="attn_in_rope",
    )(x2d, mod3, w_in, *rope_tables)


ONES_ROWS = 16
CTX_ATTN_SEQS = 2
ATTN_ROWS = 512
ATTN_SUBBLOCKS = 1


def _slab_reduce(op, x):
    parts = [x[i:i + 8] for i in range(0, x.shape[0], 8)]
    while len(parts) > 1:
        parts = [op(parts[i], parts[i + 1]) for i in range(0, len(parts) - 1, 2)] + (
            [parts[-1]] if len(parts) % 2 else [])
    return parts[0]


def _attn_kernel(*refs, layer_idx, has_ctx, n_seq, shared_keys, before_unit=None):
    if has_ctx:
        (q_ref, k_ref, vt_ref, z_ref, x_ref, kc_ref, vc_ref, mod_ref, lam_ref, subln_ref,
         w_ref, g_ref, b_ref, o_ref, y_ref, kcb_ref, vct_ref) = refs

        @pl.when(pl.program_id(1) == 0)
        def _():
            n_ctx = kc_ref.shape[0]
            kcb_ref[...] = kc_ref[...].reshape(n_ctx, D_MODEL).astype(BF16)
            vct_ref[...] = vc_ref[...].reshape(n_ctx, D_MODEL).T.astype(BF16)
    else:
        (q_ref, k_ref, vt_ref, z_ref, x_ref, mod_ref, lam_ref, subln_ref,
         w_ref, g_ref, b_ref, o_ref, y_ref) = refs
    tq = q_ref.shape[0] // n_seq
    lam_init = 0.8 - 0.6 * math.exp(-0.3 * layer_idx)
    lm = lam_ref[...]
    lam = (jnp.exp(jnp.sum(lm[0:1] * lm[1:2], axis=-1, keepdims=True))
           - jnp.exp(jnp.sum(lm[2:3] * lm[3:4], axis=-1, keepdims=True)) + lam_init)
    lane = lax.broadcasted_iota(jnp.int32, (1, DV_A), 1)
    first = lane < DH_A
    subln = jnp.broadcast_to(subln_ref[...], (DV_A, tq))
    nt = (((1,), (1,)), ((), ()))
    t = k_ref.shape[0] if shared_keys else k_ref.shape[0] // n_seq

    def key_rows(sq):
        return slice(0, t) if shared_keys else slice(sq * t, (sq + 1) * t)

    def scores(unit):
        sq, hd = unit
        cols = slice(hd * DV_A, (hd + 1) * DV_A)
        qh = q_ref[sq * tq:(sq + 1) * tq, cols]
        zero = jnp.zeros_like(qh)
        qq = jnp.concatenate([jnp.where(first, qh, zero), jnp.where(first, zero, qh)], axis=0)
        parts = [lax.dot_general(k_ref[key_rows(sq), cols], qq, nt, preferred_element_type=F32)]
        if has_ctx:
            parts.append(lax.dot_general(kcb_ref[:, cols], qq, nt, preferred_element_type=F32))
        return parts

    def exps(parts):
        m8 = functools.reduce(jnp.maximum, [_slab_reduce(jnp.maximum, s) for s in parts])
        m = jnp.max(m8, axis=0, keepdims=True)
        return [jnp.exp2(s - m).astype(BF16) for s in parts]

    def with_ones(vals_t):
        return jnp.concatenate([vals_t, jnp.ones((ONES_ROWS, vals_t.shape[1]), BF16)], axis=0)

    def finish(unit, es):
        sq, hd = unit
        cols = slice(hd * DV_A, (hd + 1) * DV_A)
        rows = slice(sq * tq, (sq + 1) * tq)
        acc = jnp.dot(with_ones(vt_ref[cols, key_rows(sq)]), es[0], preferred_element_type=F32)
        if has_ctx:
            acc = acc + jnp.dot(with_ones(vct_ref[cols, :]), es[1], preferred_element_type=F32)
        inv = 1.0 / acc[DV_A:DV_A + 1, :]
        ot = acc[:DV_A, :tq] * inv[:, :tq] - acc[:DV_A, tq:] * (inv[:, tq:] * lam)
        ot = ot * lax.rsqrt(jnp.mean(ot * ot, axis=0, keepdims=True) + RMS_EPS)
        ot = ot * subln * (1.0 - lam_init)
        y_ref[rows, cols] = (ot.T * _silu(z_ref[rows, cols])).astype(BF16)

    units = [(sq, hd) for hd in range(H_A) for sq in range(n_seq)]
    if before_unit is not None:
        before_unit(-1)
    s_ahead = {0: scores(units[0])}
    for u in range(len(units)):
        if before_unit is not None:
            before_unit(u)
        if u + 1 < len(units):
            s_ahead[u + 1] = scores(units[u + 1])
        finish(units[u], exps(s_ahead.pop(u)))
    out = jnp.dot(y_ref[...], w_ref[...].astype(BF16), preferred_element_type=F32)
    gate = mod_ref[:, 2 * D_MODEL:3 * D_MODEL]
    o_ref[...] = _residual_layer_norm(x_ref[...], out, gate, g_ref[...], b_ref[...])


def _attention(qkvz, first_row, m_rows, x, seq_len, ctx, mod3, lam, subln, w_out, j, ln_g, ln_b,
               layer, per_batch_rows):
    q, k, vt, z = qkvz
    t = seq_len
    b = m_rows // t
    has_ctx = ctx is not None
    if t <= 256 and not per_batch_rows and not has_ctx and b % 2 == 0:
        rows, n_seq, shared_keys, n_kseq = 2 * t, 2, False, 2
    else:
        rows, n_seq, shared_keys, n_kseq = ATTN_ROWS, ATTN_SUBBLOCKS, True, 1
    nq = n_kseq * t // rows
    x, x_first_row = x
    r0, k0 = first_row // rows, first_row // (n_kseq * t)
    x_spec = pl.BlockSpec((rows, D_MODEL), lambda i, j: (x_first_row // rows + i * nq + j, 0))
    q_spec = pl.BlockSpec((rows, D_MODEL), lambda i, j: (r0 + i * nq + j, 0))
    o_spec = pl.BlockSpec((rows, D_MODEL), lambda i, j: (i * nq + j, 0))
    k_spec = pl.BlockSpec((n_kseq * t, D_MODEL), lambda i, j: (k0 + i, 0))
    vt_spec = pl.BlockSpec((D_MODEL, n_kseq * t), lambda i, j: (0, k0 + i))
    if per_batch_rows:
        mod_map = lambda i, j: (layer * MOD_ROWS + i, 0, 0)
    else:
        mod_map = lambda i, j: (layer * MOD_ROWS + CTX_ROW, 0, 0)
    const2 = lambda i, j: (0, 0)
    in_specs = [q_spec, k_spec, vt_spec, q_spec, x_spec]
    args = [q, k, vt, z, x]
    scratch = [pltpu.VMEM((rows, D_MODEL), BF16)]
    if has_ctx:
        cache_k, cache_v, jj = ctx
        n_ctx = cache_k.shape[2]
        scratch += [pltpu.VMEM((n_ctx, D_MODEL), BF16), pltpu.VMEM((D_MODEL, n_ctx), BF16)]
        c_spec = pl.BlockSpec((None, None, n_ctx, H_A, DV_A),
                              lambda i, j: (i, jj, 0, 0, 0))
        in_specs += [c_spec, c_spec]
        args += [cache_k, cache_v]
    in_specs += [pl.BlockSpec((None, 1, 3 * D_MODEL), mod_map),
                 pl.BlockSpec((4, DH_A), const2),
                 pl.BlockSpec((DV_A, 1), const2),
                 _weight_spec(w_out, j),
                 pl.BlockSpec((1, D_MODEL), const2),
                 pl.BlockSpec((1, D_MODEL), const2)]
    args += [mod3, lam, subln.reshape(DV_A, 1), w_out, ln_g.reshape(1, D_MODEL),
             ln_b.reshape(1, D_MODEL)]
    return pl.pallas_call(
        functools.partial(_attn_kernel, layer_idx=layer, has_ctx=has_ctx, n_seq=n_seq,
                          shared_keys=shared_keys),
        out_shape=jax.ShapeDtypeStruct((m_rows, D_MODEL), F32),
        grid=(b // n_kseq, nq),
        in_specs=in_specs,
        out_specs=o_spec,
        scratch_shapes=scratch,
        compiler_params=_params(2),
        name="diff_attn_ctx" if has_ctx else "diff_attn",
    )(*args)


def _ctx_attn_layer_kernel(*refs, layer_idx, seq_len, n_aliased):
    x_ref, mod_ref, w_in_ref, lam_ref, subln_ref, w_out_ref, g_ref, b_ref = refs[:8]
    o_ref, ck_ref, cv_ref, q_ref, k_ref, vt_ref, z_ref, y_ref = refs[8 + n_aliased:]
    _attn_in_body(x_ref, mod_ref, w_in_ref, None, q_ref, k_ref, vt_ref, z_ref, (ck_ref, cv_ref),
                  seq_len)
    _attn_kernel(q_ref, k_ref, vt_ref, z_ref, x_ref, mod_ref, lam_ref, subln_ref, w_out_ref, g_ref,
                 b_ref, o_ref, y_ref, layer_idx=layer_idx, has_ctx=False,
                 n_seq=x_ref.shape[0] // seq_len, shared_keys=False)


def _ctx_attention_layer(x, m_rows, seq_len, mod3, w_in, w_out, j, lam, subln, ln_g, ln_b, layer,
                         cache_out):
    x2d, x_first_row = x
    rows = CTX_ATTN_SEQS * seq_len
    const2 = lambda m: (0, 0)
    in_specs = [pl.BlockSpec((rows, D_MODEL), lambda m: (x_first_row // rows + m, 0)),
                pl.BlockSpec((None, 1, 3 * D_MODEL), lambda m: (layer * MOD_ROWS + CTX_ROW, 0, 0)),
                _weight_spec(w_in, j),
                pl.BlockSpec((4, DH_A), const2),
                pl.BlockSpec((DV_A, 1), const2),
                _weight_spec(w_out, j),
                pl.BlockSpec((1, D_MODEL), const2),
                pl.BlockSpec((1, D_MODEL), const2)]
    args = [x2d, mod3, w_in, lam, subln.reshape(DV_A, 1), w_out, ln_g.reshape(1, D_MODEL),
            ln_b.reshape(1, D_MODEL)]
    new_k, new_v = cache_out
    cache_shape = jax.ShapeDtypeStruct(new_k.shape, F32)
    aliases = {}
    if j == 0:
        cache_spec = pl.BlockSpec((CTX_ATTN_SEQS, new_k.shape[1], seq_len, H_A, DV_A),
                                  lambda m: (m, 0, 0, 0, 0))
    else:
        cache_spec = pl.BlockSpec((CTX_ATTN_SEQS, None, seq_len, H_A, DV_A),
                                  lambda m: (m, j, 0, 0, 0))
        in_specs += [pl.BlockSpec(memory_space=pl.ANY)] * 2
        args += [new_k, new_v]
        aliases = {len(args) - 2: 1, len(args) - 1: 2}
    return pl.pallas_call(
        functools.partial(_ctx_attn_layer_kernel, layer_idx=layer, seq_len=seq_len,
                          n_aliased=len(aliases)),
        out_shape=[jax.ShapeDtypeStruct((m_rows, D_MODEL), F32), cache_shape, cache_shape],
        grid=(m_rows // rows,),
        in_specs=in_specs,
        out_specs=[pl.BlockSpec((rows, D_MODEL), lambda m: (m, 0)), cache_spec, cache_spec],
        scratch_shapes=[pltpu.VMEM((rows, D_MODEL), BF16), pltpu.VMEM((rows, D_MODEL), BF16),
                        pltpu.VMEM((D_MODEL, rows), BF16), pltpu.VMEM((rows, D_MODEL), F32),
                        pltpu.VMEM((rows, D_MODEL), BF16)],
        input_output_aliases=aliases,
        compiler_params=_params(1),
        name="ctx_attn_layer",
    )(*args)


def _rope_tables(n_tokens):
    rows = n_tokens // GRID_W
    r = jnp.repeat(jnp.arange(rows, dtype=F32), GRID_W)
    col = jnp.tile(jnp.arange(GRID_W, dtype=F32), rows)
    n_freq = DH_A // 4
    inv = ROPE_BASE ** (-jnp.arange(n_freq, dtype=F32) / n_freq)
    ang = jnp.concatenate([r[:, None] * inv, col[:, None] * inv], -1)
    cos, sin = jnp.cos(ang), jnp.sin(ang)
    return jnp.tile(cos, (1, 4)), jnp.concatenate([-sin, sin, -sin, sin], -1)


def _ret_in_kernel(xc_ref, xl_ref, mod_ref, w_hbm_ref, q_ref, kt_ref, v_ref, g_ref, w_ref, sem_ref, *,
                   n_ctx_tiles, j):
    nq = H_B * DK_B
    cc = WEIGHT_CHUNK_COLS
    assert nq == cc and E_B % cc == 0
    staged = _StagedWeight(w_hbm_ref, w_ref, sem_ref, j, cc)

    def body(arrive):
        h = _modulated(_both_streams_rows(xc_ref, xl_ref, n_ctx_tiles), mod_ref)
        arrive(0)
        q_ref[...] = jnp.dot(h, w_ref[:, 0:nq], preferred_element_type=F32).astype(BF16)
        arrive(1)
        k = jnp.dot(h, w_ref[:, nq:2 * nq], preferred_element_type=F32) * (DK_B ** -0.5)
        kt_ref[...] = k.T
        for c in range(E_B // cc):
            arrive(2 + c)
            v = jnp.dot(h, w_ref[:, 2 * nq + c * cc:2 * nq + (c + 1) * cc],
                        preferred_element_type=F32)
            v_ref[:, c * cc:(c + 1) * cc] = v.astype(BF16)
        for c in range(E_B // cc):
            arrive(2 + E_B // cc + c)
            g_ref[:, c * cc:(c + 1) * cc] = jnp.dot(
                h, w_ref[:, 2 * nq + E_B + c * cc:2 * nq + E_B + (c + 1) * cc],
                preferred_element_type=F32)

    staged.run(body)


def _ret_in(x_ctx, x_lat, mod3, w_in, j, layer, lat_rows_per_batch):
    m_rows = x_ctx.shape[0] + x_lat.shape[0]
    tm = ROW_TILE
    nq = H_B * DK_B
    n_ctx_tiles = x_ctx.shape[0] // tm
    return pl.pallas_call(
        functools.partial(_ret_in_kernel, n_ctx_tiles=n_ctx_tiles, j=j),
        out_shape=[jax.ShapeDtypeStruct((m_rows, nq), BF16),
                   jax.ShapeDtypeStruct((nq, m_rows), F32),
                   jax.ShapeDtypeStruct((m_rows, E_B), BF16),
                   jax.ShapeDtypeStruct((m_rows, E_B), F32)],
        grid=(m_rows // tm,),
        in_specs=_both_streams_specs(x_ctx, x_lat, tm) + [
            pl.BlockSpec((None, 1, 3 * D_MODEL),
                         _both_streams_mod_map(layer, n_ctx_tiles, tm, lat_rows_per_batch)),
            pl.BlockSpec(memory_space=pl.ANY)],
        out_specs=[pl.BlockSpec((tm, nq), lambda m: (m, 0)),
                   pl.BlockSpec((nq, tm), lambda m: (0, m)),
                   pl.BlockSpec((tm, E_B), lambda m: (m, 0)),
                   pl.BlockSpec((tm, E_B), lambda m: (m, 0))],
        scratch_shapes=_StagedWeight.scratch(w_in, WEIGHT_CHUNK_COLS),
        compiler_params=_params(1),
        name="ret_in",
    )(x_ctx, x_lat, mod3, w_in)


def _retention_kernel(*refs, has_state, heads, fused_out):
    refs = list(refs)
    q_ref, kt_ref, v_ref, g_ref, af_ref, ab_ref = refs[:6]
    del refs[:6]
    if has_state:
        s0f_ref, s0b_ref = refs[:2]
        del refs[:2]
    if fused_out:
        x_ref, mod_ref, w_ref, lng_ref, lnb_ref = refs[:5]
        del refs[:5]
        o_ref = refs.pop(0)
        y_ref = refs.pop()
    else:
        y_ref = refs.pop(0)
    if not has_state:
        sf_ref, sb_ref = refs
    t = q_ref.shape[0]
    nc = t // CHUNK
    row = lax.broadcasted_iota(jnp.int32, (CHUNK, CHUNK), 0).astype(F32)
    col = lax.broadcasted_iota(jnp.int32, (CHUNK, CHUNK), 1).astype(F32)
    diff = row - col
    idx_col = lax.broadcasted_iota(jnp.int32, (CHUNK, 1), 0).astype(F32)
    idx_row = lax.broadcasted_iota(jnp.int32, (1, CHUNK), 1).astype(F32)

    def chunk(c):
        return slice(c * CHUNK, (c + 1) * CHUNK)

    for hh in range(heads):
        qk_cols = slice(hh * DK_B, (hh + 1) * DK_B)
        v_cols = slice(hh * DV_B, (hh + 1) * DV_B)
        lg_f = jnp.log1p(-jnp.exp(af_ref[hh]))
        lg_b = jnp.log1p(-jnp.exp(ab_ref[hh]))
        dmask = (jnp.where(diff >= 0, jnp.exp(jnp.maximum(diff, 0.0) * lg_f), 0.0)
                 + jnp.where(diff <= 0, jnp.exp(jnp.maximum(-diff, 0.0) * lg_b), 0.0))
        qd_f = jnp.exp((idx_col + 1.0) * lg_f)
        qd_b = jnp.exp((CHUNK - idx_col) * lg_b)
        kd_f = jnp.exp((CHUNK - 1.0 - idx_row) * lg_f)
        kd_b = jnp.exp(idx_row * lg_b)
        cd_f = jnp.exp(CHUNK * lg_f)
        cd_b = jnp.exp(CHUNK * lg_b)

        def states(order, kd, cd, s):
            seen = {}
            for n, c in enumerate(order):
                seen[c] = None if s is None else s.astype(BF16)
                if has_state and n == nc - 1:
                    return seen, None
                u = jnp.dot((kt_ref[qk_cols, chunk(c)] * kd).astype(BF16), v_ref[chunk(c), v_cols],
                            preferred_element_type=F32)
                s = u if s is None else s * cd + u
            return seen, s

        seen_f, s_f = states(range(nc), kd_f, cd_f, s0f_ref[hh] if has_state else None)
        seen_b, s_b = states(range(nc - 1, -1, -1), kd_b, cd_b, s0b_ref[hh] if has_state else None)
        if not has_state:
            sf_ref[hh] = s_f
            sb_ref[hh] = s_b
        for c in range(nc):
            qc = q_ref[chunk(c), qk_cols]
            qk = jnp.dot(qc, kt_ref[qk_cols, chunk(c)].astype(BF16), preferred_element_type=F32)
            o = jnp.dot((qk * dmask).astype(BF16), v_ref[chunk(c), v_cols],
                        preferred_element_type=F32)
            if seen_f[c] is not None:
                o = o + jnp.dot(qc, seen_f[c], preferred_element_type=F32) * qd_f
            if seen_b[c] is not None:
                o = o + jnp.dot(qc, seen_b[c], preferred_element_type=F32) * qd_b
            o = o * lax.rsqrt(jnp.mean(o * o, axis=-1, keepdims=True) + RMS_EPS)
            y_ref[chunk(c), v_cols] = (o * _silu(g_ref[chunk(c), v_cols])).astype(BF16)
    if fused_out:
        part = jnp.dot(y_ref[...], w_ref[...].astype(BF16), preferred_element_type=F32)
        gate = mod_ref[:, 2 * D_MODEL:3 * D_MODEL]
        head_steps = H_B // heads
        if head_steps == 1:
            o_ref[...] = _residual_layer_norm(x_ref[...], part, gate, lng_ref[...], lnb_ref[...])
        else:
            step = pl.program_id(1)

            @pl.when(step == 0)
            def _():
                o_ref[...] = part

            @pl.when(jnp.logical_and(step > 0, step < head_steps - 1))
            def _():
                o_ref[...] += part

            @pl.when(step == head_steps - 1)
            def _():
                o_ref[...] = _residual_layer_norm(x_ref[...], o_ref[...] + part, gate, lng_ref[...],
                                                  lnb_ref[...])


def _retention(q, kt, v, g, seq_len, n_seq, first_row, heads, decay_f, decay_b, states,
               out_proj=None):
    t = seq_len
    b = n_seq
    s0 = first_row // t
    has_state = states is not None
    fused_out = out_proj is not None
    q_spec = pl.BlockSpec((t, heads * DK_B), lambda i, h: (s0 + i, h))
    kt_spec = pl.BlockSpec((heads * DK_B, t), lambda i, h: (h, s0 + i))
    vg_spec = pl.BlockSpec((t, heads * DV_B), lambda i, h: (s0 + i, h))
    a_spec = pl.BlockSpec((heads, 1, 1), lambda i, h: (h, 0, 0))
    in_specs = [q_spec, kt_spec, vg_spec, vg_spec, a_spec, a_spec]
    args = [q, kt, v, g, decay_f.reshape(H_B, 1, 1), decay_b.reshape(H_B, 1, 1)]
    out_shape = [jax.ShapeDtypeStruct((b * t, E_B), BF16)]
    out_specs = [pl.BlockSpec((t, heads * DV_B), lambda i, h: (i, h))]
    if has_state:
        s_f, s_b, jj = states
        s_spec = pl.BlockSpec((None, None, heads, DK_B, DV_B), lambda i, h: (i, jj, h, 0, 0))
        in_specs += [s_spec, s_spec]
        args += [s_f, s_b]
    scratch = []
    if fused_out:
        x2d, mod3, mod_map, w_out, jw, ln_g, ln_b = out_proj
        x_spec = pl.BlockSpec((t, D_MODEL), lambda i, h: (i, 0))
        const2 = lambda i, h: (0, 0)
        if heads == H_B:
            w_spec = _weight_spec(w_out, jw)
        else:
            w_spec = pl.BlockSpec((None, heads * DV_B, D_MODEL), lambda i, h: (jw, h, 0))
        in_specs += [x_spec, pl.BlockSpec((None, 1, 3 * D_MODEL), mod_map), w_spec,
                     pl.BlockSpec((1, D_MODEL), const2), pl.BlockSpec((1, D_MODEL), const2)]
        args += [x2d, mod3, w_out, ln_g.reshape(1, D_MODEL), ln_b.reshape(1, D_MODEL)]
        out_shape = [jax.ShapeDtypeStruct((b * t, D_MODEL), F32)]
        out_specs = [x_spec]
        scratch = [pltpu.VMEM((t, heads * DV_B), BF16)]
    if not has_state:
        so_spec = pl.BlockSpec((None, None, heads, DK_B, DV_B), lambda i, h: (i, 0, h, 0, 0))
        out_shape += [jax.ShapeDtypeStruct((b, 1, H_B, DK_B, DV_B), F32)] * 2
        out_specs += [so_spec, so_spec]
    return pl.pallas_call(
        functools.partial(_retention_kernel, has_state=has_state, heads=heads,
                          fused_out=fused_out),
        out_shape=out_shape,
        grid=(b, H_B // heads),
        in_specs=in_specs,
        out_specs=out_specs,
        scratch_shapes=scratch,
        compiler_params=_params(2),
        name="retention_state" if has_state else "retention",
    )(*args)


CONV_ROWS = 1024
CONV_COLS = 256


def _conv_kernel(xc_ref, xl_ref, mod_ref, w_in_ref, cw_ref, w_out_ref, g_ref, b_ref, o_ref, *,
                 n_ctx_tiles, ctx_seq_len, lat_seq_len):
    x = _both_streams_rows(xc_ref, xl_ref, n_ctx_tiles)
    h = _modulated(x, mod_ref)
    rows = x.shape[0]
    is_ctx = pl.program_id(0) < n_ctx_tiles
    row = lax.broadcasted_iota(jnp.int32, (rows, 1), 0)
    pos = jnp.where(is_ctx, row % ctx_seq_len, row % lat_seq_len)
    has_prev = pos > 0
    has_next = pos < jnp.where(is_ctx, ctx_seq_len - 1, lat_seq_len - 1)
    e = D_MODEL
    for c in range(e // CONV_COLS):
        cols = slice(c * CONV_COLS, (c + 1) * CONV_COLS)

        def proj(part):
            lo = part * e + c * CONV_COLS
            return jnp.dot(h, w_in_ref[:, lo:lo + CONV_COLS], preferred_element_type=F32)

        p = proj(1) * proj(2)
        prev = jnp.where(has_prev, pltpu.roll(p, 1, 0), 0.0)
        nxt = jnp.where(has_next, pltpu.roll(p, rows - 1, 0), 0.0)
        conv = prev * cw_ref[0:1, cols] + p * cw_ref[1:2, cols] + nxt * cw_ref[2:3, cols]
        y = (proj(0) * conv * _silu(proj(3))).astype(BF16)
        part = jnp.dot(y, w_out_ref[cols, :].astype(BF16), preferred_element_type=F32)
        if c == 0:
            o_ref[...] = part
        else:
            o_ref[...] += part
    gate = mod_ref[:, 2 * D_MODEL:3 * D_MODEL]
    o_ref[...] = _residual_layer_norm(x, o_ref[...], gate, g_ref[...], b_ref[...])


def _conv_layer(x_ctx, x_lat, mod3, w_in, conv_w, w_out, j, ln_g, ln_b, layer, ctx_seq_len,
                lat_seq_len):
    m_rows = x_ctx.shape[0] + x_lat.shape[0]
    tm = CONV_ROWS
    n_ctx_tiles = x_ctx.shape[0] // tm
    const2 = lambda m: (0, 0)
    return pl.pallas_call(
        functools.partial(_conv_kernel, n_ctx_tiles=n_ctx_tiles, ctx_seq_len=ctx_seq_len,
                          lat_seq_len=lat_seq_len),
        out_shape=jax.ShapeDtypeStruct((m_rows, D_MODEL), F32),
        grid=(m_rows // tm,),
        in_specs=_both_streams_specs(x_ctx, x_lat, tm) + [
                  pl.BlockSpec((None, 1, 3 * D_MODEL),
                               _both_streams_mod_map(layer, n_ctx_tiles, tm, lat_seq_len)),
                  _weight_spec(w_in, j),
                  pl.BlockSpec((None, 3, D_MODEL), lambda m: (j, 0, 0)),
                  _weight_spec(w_out, j),
                  pl.BlockSpec((1, D_MODEL), const2),
                  pl.BlockSpec((1, D_MODEL), const2)],
        out_specs=pl.BlockSpec((tm, D_MODEL), lambda m: (m, 0)),
        compiler_params=_params(1),
        name="conv_layer",
    )(x_ctx, x_lat, mod3, w_in, conv_w, w_out, ln_g.reshape(1, D_MODEL), ln_b.reshape(1, D_MODEL))


def _separate(xp, xs, m_ctx):
    if xp[0] is xs[0]:
        return (xp[0][:m_ctx], 0), (xs[0][m_ctx:], 0)
    return xp, xs


def kernel(x_prompt, x_sample, cache_k, cache_v, state_fwd, state_bwd, c, c_ctx, w_mod, b_mod, ln_g,
           ln_b, w_in_a, lam_a, subln_a, w_out_a, w_in_b, decay_fwd, decay_bwd, w_out_b, w_in_c,
           conv_c, w_out_c):
    bp, tp, d = x_prompt.shape
    bs, ts, _ = x_sample.shape
    cvec = jnp.concatenate([c, c_ctx[None], jnp.zeros((MOD_ROWS - bs - 1, d), F32)], axis=0)
    mod3 = _modulation(cvec, w_mod, b_mod)
    rope_tables = _rope_tables(ts)

    m_ctx, m_lat = bp * tp, bs * ts
    xp = (x_prompt.reshape(m_ctx, d), 0)
    xs = (x_sample.reshape(m_lat, d), 0)
    n_attn = (DEPTH + N_MIXERS - 1) // N_MIXERS
    new_cache_k = new_cache_v = jax.ShapeDtypeStruct((bp, n_attn, tp, H_A, DV_A), F32)
    new_sf, new_sb = [], []
    for i in range(DEPTH):
        kind, j = i % N_MIXERS, i // N_MIXERS
        if kind == 0:
            x_ctx, new_cache_k, new_cache_v = _ctx_attention_layer(
                xp, m_ctx, tp, mod3, w_in_a, w_out_a, j, lam_a[j], subln_a[j], ln_g[i], ln_b[i], i,
                (new_cache_k, new_cache_v))
            qkvz = _attn_in(xs, m_lat, mod3, w_in_a, j, i, ts, rope_tables)
            xs = (_attention(qkvz, 0, m_lat, xs, ts, (cache_k, cache_v, j), mod3, lam_a[j],
                             subln_a[j], w_out_a, j, ln_g[i], ln_b[i], i, True), 0)
            xp = (x_ctx, 0)
        else:
            xp, xs = _separate(xp, xs, m_ctx)
            if kind == 1:
                q, kt, v, g = _ret_in(xp[0], xs[0], mod3, w_in_b, j, i, ts)
                x_ctx, s_f, s_b = _retention(
                    q, kt, v, g, tp, bp, 0, H_B, decay_fwd[j], decay_bwd[j], None,
                    out_proj=(xp[0], mod3, _mod_row_map(i, None, tp), w_out_b, j, ln_g[i],
                              ln_b[i]))
                new_sf.append(s_f)
                new_sb.append(s_b)
                (x_lat,) = _retention(
                    q, kt, v, g, ts, bs, m_ctx, 1, decay_fwd[j], decay_bwd[j],
                    (state_fwd, state_bwd, j),
                    out_proj=(xs[0], mod3, _mod_row_map(i, ts, ts), w_out_b, j, ln_g[i], ln_b[i]))
                xp, xs = (x_ctx, 0), (x_lat, 0)
            else:
                x_all = _conv_layer(xp[0], xs[0], mod3, w_in_c, conv_c, w_out_c, j, ln_g[i],
                                    ln_b[i], i, tp, ts)
                xp, xs = (x_all, 0), (x_all, m_ctx)
    xp, xs = _separate(xp, xs, m_ctx)
    y_prompt = xp[0].reshape(bp, tp, d)
    y_sample = xs[0].reshape(bs, ts, d)
    new_state_fwd = jnp.concatenate(new_sf, axis=1)
    new_state_bwd = jnp.concatenate(new_sb, axis=1)
    return (y_prompt, y_sample, new_cache_k, new_cache_v, new_state_fwd, new_state_bwd)
```

```python
import functools
import math

import jax
import jax.numpy as jnp
from jax import lax
from jax.experimental import pallas as pl
from jax.experimental.pallas import tpu as pltpu

F32 = jnp.float32
BF16 = jnp.bfloat16

D_MODEL = 1024
DEPTH = 4
N_MIXERS = 3
GRID_W = 64
H_A = 8
DH_A = 64
DV_A = 128
ROPE_HALF = DH_A // 2
SCORE_SCALE = DH_A ** -0.5 * math.log2(math.e)
H_B = 4
DK_B = 256
DV_B = 512
E_B = H_B * DV_B
CHUNK = 256
ALPHA = (2.0 * DEPTH) ** 0.25
ROPE_BASE = 10000.0
LN_EPS = 1e-5
RMS_EPS = 1e-6

MOD_ROWS = 8
CTX_ROW = 4
VMEM_LIMIT_BYTES = 58 * 1024 * 1024
ROW_TILE = 512


def _params(n_axes):
    return pltpu.CompilerParams(dimension_semantics=("arbitrary",) * n_axes,
                                vmem_limit_bytes=VMEM_LIMIT_BYTES)


def _silu(x):
    return x * jax.nn.sigmoid(x)


def _residual_layer_norm(x, out, gate, g, b):
    r = ALPHA * x + gate * out
    mu = jnp.mean(r, axis=-1, keepdims=True)
    d = r - mu
    var = jnp.mean(d * d, axis=-1, keepdims=True)
    return d * lax.rsqrt(var + LN_EPS) * g + b


def _modulated(x, mod_ref):
    shift = mod_ref[:, 0:D_MODEL]
    scale = mod_ref[:, D_MODEL:2 * D_MODEL]
    return x * (1.0 + scale) + shift


def _mod_row_map(layer, rows_per_batch, tile):
    if rows_per_batch is None:
        return lambda m, *_: (layer * MOD_ROWS + CTX_ROW, 0, 0)
    return lambda m, *_: (layer * MOD_ROWS + (m * tile) // rows_per_batch, 0, 0)


def _both_streams_specs(x_ctx, x_lat, tm):
    n_ctx_tiles = x_ctx.shape[0] // tm
    return [pl.BlockSpec((tm, D_MODEL), lambda m: (jnp.minimum(m, n_ctx_tiles - 1), 0)),
            pl.BlockSpec((tm, D_MODEL), lambda m: (jnp.maximum(m - n_ctx_tiles, 0), 0))]


def _both_streams_rows(x_ctx_ref, x_lat_ref, n_ctx_tiles):
    return jnp.where(pl.program_id(0) < n_ctx_tiles, x_ctx_ref[...], x_lat_ref[...])


def _both_streams_mod_map(layer, n_ctx_tiles, tm, lat_rows_per_batch):
    def index_map(m):
        lat_row = ((m - n_ctx_tiles) * tm) // lat_rows_per_batch
        return (layer * MOD_ROWS + jnp.where(m < n_ctx_tiles, CTX_ROW, lat_row), 0, 0)
    return index_map


def _weight_spec(w, j):
    return pl.BlockSpec((None,) + w.shape[1:], lambda *_: (j, 0, 0), pipeline_mode=pl.Buffered(1))


WEIGHT_CHUNK_COLS = 1024
WEIGHT_COPIES_IN_FLIGHT = 2


class _StagedWeight:
    @staticmethod
    def scratch(w, chunk_cols):
        return [pltpu.VMEM(w.shape[1:], w.dtype),
                pltpu.SemaphoreType.DMA((w.shape[2] // chunk_cols,))]

    def __init__(self, w_hbm_ref, w_vmem_ref, sem_ref, j, chunk_cols):
        self.copies = [
            pltpu.make_async_copy(w_hbm_ref.at[j, :, pl.ds(c * chunk_cols, chunk_cols)],
                                  w_vmem_ref.at[:, pl.ds(c * chunk_cols, chunk_cols)], sem_ref.at[c])
            for c in range(w_vmem_ref.shape[1] // chunk_cols)]

    def _arrive(self, c):
        self.copies[c].wait()
        if c + WEIGHT_COPIES_IN_FLIGHT < len(self.copies):
            self.copies[c + WEIGHT_COPIES_IN_FLIGHT].start()

    def run(self, body):
        first = pl.program_id(0) == 0

        @pl.when(first)
        def _():
            for cp in self.copies[:WEIGHT_COPIES_IN_FLIGHT]:
                cp.start()
            body(self._arrive)

        @pl.when(jnp.logical_not(first))
        def _():
            body(lambda c: None)


def _mod_kernel(cv_ref, w_ref, b_ref, o_ref):
    s = _silu(cv_ref[...])
    o_ref[...] = jnp.dot(s, w_ref[...], preferred_element_type=F32) + b_ref[...]


def _modulation(cvec, w_mod, b_mod):
    n = 3 * D_MODEL
    tn = n
    out = pl.pallas_call(
        _mod_kernel,
        out_shape=jax.ShapeDtypeStruct((DEPTH, MOD_ROWS, n), F32),
        grid=(DEPTH, n // tn),
        in_specs=[pl.BlockSpec((MOD_ROWS, D_MODEL), lambda i, j: (0, 0)),
                  pl.BlockSpec((None, D_MODEL, tn), lambda i, j: (i, 0, j)),
                  pl.BlockSpec((None, 1, tn), lambda i, j: (i, 0, j))],
        out_specs=pl.BlockSpec((None, MOD_ROWS, tn), lambda i, j: (i, 0, j)),
        compiler_params=_params(2),
        name="modulation",
    )(cvec, w_mod, b_mod.reshape(DEPTH, 1, n))
    return out.reshape(DEPTH * MOD_ROWS, 1, n)


def _rope(xh, cos4, sin4, first_half):
    swapped = jnp.where(first_half, pltpu.roll(xh, DV_A - ROPE_HALF, 1),
                        pltpu.roll(xh, ROPE_HALF, 1))
    return xh * cos4 + swapped * sin4


def _store_heads(o_ref, x, seq_len, head0=0):
    heads = slice(head0, head0 + x.shape[1] // DV_A)
    for b in range(o_ref.shape[0]):
        xb = x[b * seq_len:(b + 1) * seq_len].reshape(seq_len, x.shape[1] // DV_A, DV_A)
        if len(o_ref.shape) == 4:
            o_ref[b, :, heads, :] = xb
        else:
            o_ref[b, 0, :, heads, :] = xb
            for s in range(1, o_ref.shape[1]):
                o_ref[b, s, :, heads, :] = jnp.zeros_like(xb)


def _attn_in_body(x_ref, mod_ref, w_ref, rope_refs, q_ref, k_ref, vt_ref, z_ref, cache_refs, seq_len):
    h = _modulated(x_ref[...], mod_ref)
    if rope_refs is not None:
        lane = lax.broadcasted_iota(jnp.int32, (1, DV_A), 1)
        first_half = (lane % DH_A) < ROPE_HALF
        cos4 = rope_refs[0][...]
        sin4 = rope_refs[1][...]
    q_all = jnp.dot(h, w_ref[:, 0:D_MODEL], preferred_element_type=F32)
    k_all = jnp.dot(h, w_ref[:, D_MODEL:2 * D_MODEL], preferred_element_type=F32)
    if cache_refs is not None:
        _store_heads(cache_refs[0], k_all, seq_len)
    for hd in range(H_A):
        cols = slice(hd * DV_A, (hd + 1) * DV_A)
        q = q_all[:, cols]
        k = k_all[:, cols]
        if rope_refs is not None:
            q = _rope(q, cos4, sin4, first_half)
            k = _rope(k, cos4, sin4, first_half)
        q_ref[:, cols] = (q * SCORE_SCALE).astype(BF16)
        k_ref[:, cols] = k.astype(BF16)
    v = jnp.dot(h, w_ref[:, 2 * D_MODEL:3 * D_MODEL], preferred_element_type=F32)
    if cache_refs is not None:
        _store_heads(cache_refs[1], v, seq_len)
    vt_ref[...] = v.T.astype(BF16)
    z_ref[...] = jnp.dot(h, w_ref[:, 3 * D_MODEL:4 * D_MODEL], preferred_element_type=F32)


def _attn_in_kernel(x_ref, mod_ref, w_ref, cos_ref, sin_ref, q_ref, k_ref, vt_ref, z_ref):
    _attn_in_body(x_ref, mod_ref, w_ref, (cos_ref, sin_ref), q_ref, k_ref, vt_ref, z_ref, None, None)


def _attn_in(x, m_rows, mod3, w_in, j, layer, rows_per_batch, rope_tables):
    x2d, x_first_row = x
    tm = ROW_TILE
    seq_tiles = rows_per_batch // tm
    row_spec = pl.BlockSpec((tm, D_MODEL), lambda m: (m, 0))
    tab_spec = pl.BlockSpec((tm, DV_A), lambda m: (m % seq_tiles, 0))
    return pl.pallas_call(
        _attn_in_kernel,
        out_shape=[jax.ShapeDtypeStruct((m_rows, D_MODEL), BF16),
                   jax.ShapeDtypeStruct((m_rows, D_MODEL), BF16),
                   jax.ShapeDtypeStruct((D_MODEL, m_rows), BF16),
                   jax.ShapeDtypeStruct((m_rows, D_MODEL), F32)],
        grid=(m_rows // tm,),
        in_specs=[pl.BlockSpec((tm, D_MODEL), lambda m: (x_first_row // tm + m, 0)),
                  pl.BlockSpec((None, 1, 3 * D_MODEL), _mod_row_map(layer, rows_per_batch, tm)),
                  _weight_spec(w_in, j), tab_spec, tab_spec],
        out_specs=[row_spec, row_spec, pl.BlockSpec((D_MODEL, tm), lambda m: (0, m)), row_spec],
        compiler_params=_params(1),
        name="attn_in_rope",
    )(x2d, mod3, w_in, *rope_tables)


ONES_ROWS = 16
CTX_ATTN_SEQS = 2
ATTN_ROWS = 512
ATTN_SUBBLOCKS = 1


def _slab_reduce(op, x):
    parts = [x[i:i + 8] for i in range(0, x.shape[0], 8)]
    while len(parts) > 1:
        parts = [op(parts[i], parts[i + 1]) for i in range(0, len(parts) - 1, 2)] + (
            [parts[-1]] if len(parts) % 2 else [])
    return parts[0]


def _attn_kernel(*refs, layer_idx, has_ctx, n_seq, shared_keys, before_unit=None):
    if has_ctx:
        (q_ref, k_ref, vt_ref, z_ref, x_ref, kc_ref, vc_ref, mod_ref, lam_ref, subln_ref,
         w_ref, g_ref, b_ref, o_ref, y_ref, kcb_ref, vct_ref) = refs

        @pl.when(pl.program_id(1) == 0)
        def _():
            n_ctx = kc_ref.shape[0]
            kcb_ref[...] = kc_ref[...].reshape(n_ctx, D_MODEL).astype(BF16)
            vct_ref[...] = vc_ref[...].reshape(n_ctx, D_MODEL).T.astype(BF16)
    else:
        (q_ref, k_ref, vt_ref, z_ref, x_ref, mod_ref, lam_ref, subln_ref,
         w_ref, g_ref, b_ref, o_ref, y_ref) = refs
    tq = q_ref.shape[0] // n_seq
    lam_init = 0.8 - 0.6 * math.exp(-0.3 * layer_idx)
    lm = lam_ref[...]
    lam = (jnp.exp(jnp.sum(lm[0:1] * lm[1:2], axis=-1, keepdims=True))
           - jnp.exp(jnp.sum(lm[2:3] * lm[3:4], axis=-1, keepdims=True)) + lam_init)
    lane = lax.broadcasted_iota(jnp.int32, (1, DV_A), 1)
    first = lane < DH_A
    subln = jnp.broadcast_to(subln_ref[...], (DV_A, tq))
    nt = (((1,), (1,)), ((), ()))
    t = k_ref.shape[0] if shared_keys else k_ref.shape[0] // n_seq

    def key_rows(sq):
        return slice(0, t) if shared_keys else slice(sq * t, (sq + 1) * t)

    def scores(unit):
        sq, hd = unit
        cols = slice(hd * DV_A, (hd + 1) * DV_A)
        qh = q_ref[sq * tq:(sq + 1) * tq, cols]
        zero = jnp.zeros_like(qh)
        qq = jnp.concatenate([jnp.where(first, qh, zero), jnp.where(first, zero, qh)], axis=0)
        parts = [lax.dot_general(k_ref[key_rows(sq), cols], qq, nt, preferred_element_type=F32)]
        if has_ctx:
            parts.append(lax.dot_general(kcb_ref[:, cols], qq, nt, preferred_element_type=F32))
        return parts

    def exps(parts):
        m8 = functools.reduce(jnp.maximum, [_slab_reduce(jnp.maximum, s) for s in parts])
        m = jnp.max(m8, axis=0, keepdims=True)
        return [jnp.exp2(s - m).astype(BF16) for s in parts]

    def with_ones(vals_t):
        return jnp.concatenate([vals_t, jnp.ones((ONES_ROWS, vals_t.shape[1]), BF16)], axis=0)

    def finish(unit, es):
        sq, hd = unit
        cols = slice(hd * DV_A, (hd + 1) * DV_A)
        rows = slice(sq * tq, (sq + 1) * tq)
        acc = jnp.dot(with_ones(vt_ref[cols, key_rows(sq)]), es[0], preferred_element_type=F32)
        if has_ctx:
            acc = acc + jnp.dot(with_ones(vct_ref[cols, :]), es[1], preferred_element_type=F32)
        inv = 1.0 / acc[DV_A:DV_A + 1, :]
        ot = acc[:DV_A, :tq] * inv[:, :tq] - acc[:DV_A, tq:] * (inv[:, tq:] * lam)
        ot = ot * lax.rsqrt(jnp.mean(ot * ot, axis=0, keepdims=True) + RMS_EPS)
        ot = ot * subln * (1.0 - lam_init)
        y_ref[rows, cols] = (ot.T * _silu(z_ref[rows, cols])).astype(BF16)

    units = [(sq, hd) for hd in range(H_A) for sq in range(n_seq)]
    if before_unit is not None:
        before_unit(-1)
    s_ahead = {u: scores(units[u]) for u in range(min(2, len(units)))}
    e_ahead = {0: exps(s_ahead.pop(0))}
    for u in range(len(units)):
        if before_unit is not None:
            before_unit(u)
        if u + 2 < len(units):
            s_ahead[u + 2] = scores(units[u + 2])
        if u + 1 < len(units):
            e_ahead[u + 1] = exps(s_ahead.pop(u + 1))
        finish(units[u], e_ahead.pop(u))
    out = jnp.dot(y_ref[...], w_ref[...].astype(BF16), preferred_element_type=F32)
    gate = mod_ref[:, 2 * D_MODEL:3 * D_MODEL]
    o_ref[...] = _residual_layer_norm(x_ref[...], out, gate, g_ref[...], b_ref[...])


def _attention(qkvz, first_row, m_rows, x, seq_len, ctx, mod3, lam, subln, w_out, j, ln_g, ln_b,
               layer, per_batch_rows):
    q, k, vt, z = qkvz
    t = seq_len
    b = m_rows // t
    has_ctx = ctx is not None
    if t <= 256 and not per_batch_rows and not has_ctx and b % 2 == 0:
        rows, n_seq, shared_keys, n_kseq = 2 * t, 2, False, 2
    else:
        rows, n_seq, shared_keys, n_kseq = ATTN_ROWS, ATTN_SUBBLOCKS, True, 1
    nq = n_kseq * t // rows
    x, x_first_row = x
    r0, k0 = first_row // rows, first_row // (n_kseq * t)
    x_spec = pl.BlockSpec((rows, D_MODEL), lambda i, j: (x_first_row // rows + i * nq + j, 0))
    q_spec = pl.BlockSpec((rows, D_MODEL), lambda i, j: (r0 + i * nq + j, 0))
    o_spec = pl.BlockSpec((rows, D_MODEL), lambda i, j: (i * nq + j, 0))
    k_spec = pl.BlockSpec((n_kseq * t, D_MODEL), lambda i, j: (k0 + i, 0))
    vt_spec = pl.BlockSpec((D_MODEL, n_kseq * t), lambda i, j: (0, k0 + i))
    if per_batch_rows:
        mod_map = lambda i, j: (layer * MOD_ROWS + i, 0, 0)
    else:
        mod_map = lambda i, j: (layer * MOD_ROWS + CTX_ROW, 0, 0)
    const2 = lambda i, j: (0, 0)
    in_specs = [q_spec, k_spec, vt_spec, q_spec, x_spec]
    args = [q, k, vt, z, x]
    scratch = [pltpu.VMEM((rows, D_MODEL), BF16)]
    if has_ctx:
        cache_k, cache_v, jj = ctx
        n_ctx = cache_k.shape[2]
        scratch += [pltpu.VMEM((n_ctx, D_MODEL), BF16), pltpu.VMEM((D_MODEL, n_ctx), BF16)]
        c_spec = pl.BlockSpec((None, None, n_ctx, H_A, DV_A),
                              lambda i, j: (i, jj, 0, 0, 0))
        in_specs += [c_spec, c_spec]
        args += [cache_k, cache_v]
    in_specs += [pl.BlockSpec((None, 1, 3 * D_MODEL), mod_map),
                 pl.BlockSpec((4, DH_A), const2),
                 pl.BlockSpec((DV_A, 1), const2),
                 _weight_spec(w_out, j),
                 pl.BlockSpec((1, D_MODEL), const2),
                 pl.BlockSpec((1, D_MODEL), const2)]
    args += [mod3, lam, subln.reshape(DV_A, 1), w_out, ln_g.reshape(1, D_MODEL),
             ln_b.reshape(1, D_MODEL)]
    return pl.pallas_call(
        functools.partial(_attn_kernel, layer_idx=layer, has_ctx=has_ctx, n_seq=n_seq,
                          shared_keys=shared_keys),
        out_shape=jax.ShapeDtypeStruct((m_rows, D_MODEL), F32),
        grid=(b // n_kseq, nq),
        in_specs=in_specs,
        out_specs=o_spec,
        scratch_shapes=scratch,
        compiler_params=_params(2),
        name="diff_attn_ctx" if has_ctx else "diff_attn",
    )(*args)


def _ctx_attn_layer_kernel(*refs, layer_idx, seq_len, n_aliased):
    x_ref, mod_ref, w_in_ref, lam_ref, subln_ref, w_out_ref, g_ref, b_ref = refs[:8]
    o_ref, ck_ref, cv_ref, q_ref, k_ref, vt_ref, z_ref, y_ref = refs[8 + n_aliased:]
    _attn_in_body(x_ref, mod_ref, w_in_ref, None, q_ref, k_ref, vt_ref, z_ref, (ck_ref, cv_ref),
                  seq_len)
    _attn_kernel(q_ref, k_ref, vt_ref, z_ref, x_ref, mod_ref, lam_ref, subln_ref, w_out_ref, g_ref,
                 b_ref, o_ref, y_ref, layer_idx=layer_idx, has_ctx=False,
                 n_seq=x_ref.shape[0] // seq_len, shared_keys=False)


def _ctx_attention_layer(x, m_rows, seq_len, mod3, w_in, w_out, j, lam, subln, ln_g, ln_b, layer,
                         cache_out):
    x2d, x_first_row = x
    rows = CTX_ATTN_SEQS * seq_len
    const2 = lambda m: (0, 0)
    in_specs = [pl.BlockSpec((rows, D_MODEL), lambda m: (x_first_row // rows + m, 0)),
                pl.BlockSpec((None, 1, 3 * D_MODEL), lambda m: (layer * MOD_ROWS + CTX_ROW, 0, 0)),
                _weight_spec(w_in, j),
                pl.BlockSpec((4, DH_A), const2),
                pl.BlockSpec((DV_A, 1), const2),
                _weight_spec(w_out, j),
                pl.BlockSpec((1, D_MODEL), const2),
                pl.BlockSpec((1, D_MODEL), const2)]
    args = [x2d, mod3, w_in, lam, subln.reshape(DV_A, 1), w_out, ln_g.reshape(1, D_MODEL),
            ln_b.reshape(1, D_MODEL)]
    new_k, new_v = cache_out
    cache_shape = jax.ShapeDtypeStruct(new_k.shape, F32)
    aliases = {}
    if j == 0:
        cache_spec = pl.BlockSpec((CTX_ATTN_SEQS, new_k.shape[1], seq_len, H_A, DV_A),
                                  lambda m: (m, 0, 0, 0, 0))
    else:
        cache_spec = pl.BlockSpec((CTX_ATTN_SEQS, None, seq_len, H_A, DV_A),
                                  lambda m: (m, j, 0, 0, 0))
        in_specs += [pl.BlockSpec(memory_space=pl.ANY)] * 2
        args += [new_k, new_v]
        aliases = {len(args) - 2: 1, len(args) - 1: 2}
    return pl.pallas_call(
        functools.partial(_ctx_attn_layer_kernel, layer_idx=layer, seq_len=seq_len,
                          n_aliased=len(aliases)),
        out_shape=[jax.ShapeDtypeStruct((m_rows, D_MODEL), F32), cache_shape, cache_shape],
        grid=(m_rows // rows,),
        in_specs=in_specs,
        out_specs=[pl.BlockSpec((rows, D_MODEL), lambda m: (m, 0)), cache_spec, cache_spec],
        scratch_shapes=[pltpu.VMEM((rows, D_MODEL), BF16), pltpu.VMEM((rows, D_MODEL), BF16),
                        pltpu.VMEM((D_MODEL, rows), BF16), pltpu.VMEM((rows, D_MODEL), F32),
                        pltpu.VMEM((rows, D_MODEL), BF16)],
        input_output_aliases=aliases,
        compiler_params=_params(1),
        name="ctx_attn_layer",
    )(*args)


def _rope_tables(n_tokens):
    rows = n_tokens // GRID_W
    r = jnp.repeat(jnp.arange(rows, dtype=F32), GRID_W)
    col = jnp.tile(jnp.arange(GRID_W, dtype=F32), rows)
    n_freq = DH_A // 4
    inv = ROPE_BASE ** (-jnp.arange(n_freq, dtype=F32) / n_freq)
    ang = jnp.concatenate([r[:, None] * inv, col[:, None] * inv], -1)
    cos, sin = jnp.cos(ang), jnp.sin(ang)
    return jnp.tile(cos, (1, 4)), jnp.concatenate([-sin, sin, -sin, sin], -1)


def _ret_in_kernel(xc_ref, xl_ref, mod_ref, w_hbm_ref, q_ref, kt_ref, v_ref, g_ref, w_ref, sem_ref, *,
                   n_ctx_tiles, j):
    nq = H_B * DK_B
    cc = WEIGHT_CHUNK_COLS
    assert nq == cc and E_B % cc == 0
    staged = _StagedWeight(w_hbm_ref, w_ref, sem_ref, j, cc)

    def body(arrive):
        h = _modulated(_both_streams_rows(xc_ref, xl_ref, n_ctx_tiles), mod_ref)
        arrive(0)
        q_ref[...] = jnp.dot(h, w_ref[:, 0:nq], preferred_element_type=F32).astype(BF16)
        arrive(1)
        k = jnp.dot(h, w_ref[:, nq:2 * nq], preferred_element_type=F32) * (DK_B ** -0.5)
        kt_ref[...] = k.T
        for c in range(E_B // cc):
            arrive(2 + c)
            v = jnp.dot(h, w_ref[:, 2 * nq + c * cc:2 * nq + (c + 1) * cc],
                        preferred_element_type=F32)
            v_ref[:, c * cc:(c + 1) * cc] = v.astype(BF16)
        for c in range(E_B // cc):
            arrive(2 + E_B // cc + c)
            g_ref[:, c * cc:(c + 1) * cc] = jnp.dot(
                h, w_ref[:, 2 * nq + E_B + c * cc:2 * nq + E_B + (c + 1) * cc],
                preferred_element_type=F32).astype(BF16)

    staged.run(body)


def _ret_in(x_ctx, x_lat, mod3, w_in, j, layer, lat_rows_per_batch):
    m_rows = x_ctx.shape[0] + x_lat.shape[0]
    tm = ROW_TILE
    nq = H_B * DK_B
    n_ctx_tiles = x_ctx.shape[0] // tm
    return pl.pallas_call(
        functools.partial(_ret_in_kernel, n_ctx_tiles=n_ctx_tiles, j=j),
        out_shape=[jax.ShapeDtypeStruct((m_rows, nq), BF16),
                   jax.ShapeDtypeStruct((nq, m_rows), F32),
                   jax.ShapeDtypeStruct((m_rows, E_B), BF16),
                   jax.ShapeDtypeStruct((m_rows, E_B), BF16)],
        grid=(m_rows // tm,),
        in_specs=_both_streams_specs(x_ctx, x_lat, tm) + [
            pl.BlockSpec((None, 1, 3 * D_MODEL),
                         _both_streams_mod_map(layer, n_ctx_tiles, tm, lat_rows_per_batch)),
            pl.BlockSpec(memory_space=pl.ANY)],
        out_specs=[pl.BlockSpec((tm, nq), lambda m: (m, 0)),
                   pl.BlockSpec((nq, tm), lambda m: (0, m)),
                   pl.BlockSpec((tm, E_B), lambda m: (m, 0)),
                   pl.BlockSpec((tm, E_B), lambda m: (m, 0))],
        scratch_shapes=_StagedWeight.scratch(w_in, WEIGHT_CHUNK_COLS),
        compiler_params=_params(1),
        name="ret_in",
    )(x_ctx, x_lat, mod3, w_in)


def _retention_kernel(*refs, has_state, heads, fused_out):
    refs = list(refs)
    q_ref, kt_ref, v_ref, g_ref, af_ref, ab_ref = refs[:6]
    del refs[:6]
    if has_state:
        s0f_ref, s0b_ref = refs[:2]
        del refs[:2]
    if fused_out:
        x_ref, mod_ref, w_ref, lng_ref, lnb_ref = refs[:5]
        del refs[:5]
        o_ref = refs.pop(0)
        y_ref = refs.pop()
    else:
        y_ref = refs.pop(0)
    if not has_state:
        sf_ref, sb_ref = refs
    t = q_ref.shape[0]
    nc = t // CHUNK
    row = lax.broadcasted_iota(jnp.int32, (CHUNK, CHUNK), 0).astype(F32)
    col = lax.broadcasted_iota(jnp.int32, (CHUNK, CHUNK), 1).astype(F32)
    diff = row - col
    idx_col = lax.broadcasted_iota(jnp.int32, (CHUNK, 1), 0).astype(F32)
    idx_row = lax.broadcasted_iota(jnp.int32, (1, CHUNK), 1).astype(F32)

    def chunk(c):
        return slice(c * CHUNK, (c + 1) * CHUNK)

    for hh in range(heads):
        qk_cols = slice(hh * DK_B, (hh + 1) * DK_B)
        v_cols = slice(hh * DV_B, (hh + 1) * DV_B)
        lg_f = jnp.log1p(-jnp.exp(af_ref[hh]))
        lg_b = jnp.log1p(-jnp.exp(ab_ref[hh]))
        dmask = (jnp.where(diff >= 0, jnp.exp(jnp.maximum(diff, 0.0) * lg_f), 0.0)
                 + jnp.where(diff <= 0, jnp.exp(jnp.maximum(-diff, 0.0) * lg_b), 0.0))
        qd_f = jnp.exp((idx_col + 1.0) * lg_f)
        qd_b = jnp.exp((CHUNK - idx_col) * lg_b)
        kd_f = jnp.exp((CHUNK - 1.0 - idx_row) * lg_f)
        kd_b = jnp.exp(idx_row * lg_b)
        cd_f = jnp.exp(CHUNK * lg_f)
        cd_b = jnp.exp(CHUNK * lg_b)

        def states(order, kd, cd, s):
            seen = {}
            for n, c in enumerate(order):
                seen[c] = None if s is None else s.astype(BF16)
                if has_state and n == nc - 1:
                    return seen, None
                u = jnp.dot((kt_ref[qk_cols, chunk(c)] * kd).astype(BF16), v_ref[chunk(c), v_cols],
                            preferred_element_type=F32)
                s = u if s is None else s * cd + u
            return seen, s

        seen_f, s_f = states(range(nc), kd_f, cd_f, s0f_ref[hh] if has_state else None)
        seen_b, s_b = states(range(nc - 1, -1, -1), kd_b, cd_b, s0b_ref[hh] if has_state else None)
        if not has_state:
            sf_ref[hh] = s_f
            sb_ref[hh] = s_b
        for c in range(nc):
            qc = q_ref[chunk(c), qk_cols]
            qk = jnp.dot(qc, kt_ref[qk_cols, chunk(c)].astype(BF16), preferred_element_type=F32)
            o = jnp.dot((qk * dmask).astype(BF16), v_ref[chunk(c), v_cols],
                        preferred_element_type=F32)
            if seen_f[c] is not None:
                o = o + jnp.dot(qc, seen_f[c], preferred_element_type=F32) * qd_f
            if seen_b[c] is not None:
                o = o + jnp.dot(qc, seen_b[c], preferred_element_type=F32) * qd_b
            o = o * lax.rsqrt(jnp.mean(o * o, axis=-1, keepdims=True) + RMS_EPS)
            gate_pre = g_ref[chunk(c), v_cols].astype(F32)
            y_ref[chunk(c), v_cols] = (o * _silu(gate_pre)).astype(BF16)
    if fused_out:
        part = jnp.dot(y_ref[...], w_ref[...].astype(BF16), preferred_element_type=F32)
        gate = mod_ref[:, 2 * D_MODEL:3 * D_MODEL]
        head_steps = H_B // heads
        if head_steps == 1:
            o_ref[...] = _residual_layer_norm(x_ref[...], part, gate, lng_ref[...], lnb_ref[...])
        else:
            step = pl.program_id(1)

            @pl.when(step == 0)
            def _():
                o_ref[...] = part

            @pl.when(jnp.logical_and(step > 0, step < head_steps - 1))
            def _():
                o_ref[...] += part

            @pl.when(step == head_steps - 1)
            def _():
                o_ref[...] = _residual_layer_norm(x_ref[...], o_ref[...] + part, gate, lng_ref[...],
                                                  lnb_ref[...])


def _retention(q, kt, v, g, seq_len, n_seq, first_row, heads, decay_f, decay_b, states,
               out_proj=None):
    t = seq_len
    b = n_seq
    s0 = first_row // t
    has_state = states is not None
    fused_out = out_proj is not None
    q_spec = pl.BlockSpec((t, heads * DK_B), lambda i, h: (s0 + i, h))
    kt_spec = pl.BlockSpec((heads * DK_B, t), lambda i, h: (h, s0 + i))
    vg_spec = pl.BlockSpec((t, heads * DV_B), lambda i, h: (s0 + i, h))
    a_spec = pl.BlockSpec((heads, 1, 1), lambda i, h: (h, 0, 0))
    in_specs = [q_spec, kt_spec, vg_spec, vg_spec, a_spec, a_spec]
    args = [q, kt, v, g, decay_f.reshape(H_B, 1, 1), decay_b.reshape(H_B, 1, 1)]
    out_shape = [jax.ShapeDtypeStruct((b * t, E_B), BF16)]
    out_specs = [pl.BlockSpec((t, heads * DV_B), lambda i, h: (i, h))]
    if has_state:
        s_f, s_b, jj = states
        s_spec = pl.BlockSpec((None, None, heads, DK_B, DV_B), lambda i, h: (i, jj, h, 0, 0))
        in_specs += [s_spec, s_spec]
        args += [s_f, s_b]
    scratch = []
    if fused_out:
        x2d, mod3, mod_map, w_out, jw, ln_g, ln_b = out_proj
        x_spec = pl.BlockSpec((t, D_MODEL), lambda i, h: (i, 0))
        const2 = lambda i, h: (0, 0)
        if heads == H_B:
            w_spec = _weight_spec(w_out, jw)
        else:
            w_spec = pl.BlockSpec((None, heads * DV_B, D_MODEL), lambda i, h: (jw, h, 0))
        in_specs += [x_spec, pl.BlockSpec((None, 1, 3 * D_MODEL), mod_map), w_spec,
                     pl.BlockSpec((1, D_MODEL), const2), pl.BlockSpec((1, D_MODEL), const2)]
        args += [x2d, mod3, w_out, ln_g.reshape(1, D_MODEL), ln_b.reshape(1, D_MODEL)]
        out_shape = [jax.ShapeDtypeStruct((b * t, D_MODEL), F32)]
        out_specs = [x_spec]
        scratch = [pltpu.VMEM((t, heads * DV_B), BF16)]
    if not has_state:
        so_spec = pl.BlockSpec((None, None, heads, DK_B, DV_B), lambda i, h: (i, 0, h, 0, 0))
        out_shape += [jax.ShapeDtypeStruct((b, 1, H_B, DK_B, DV_B), F32)] * 2
        out_specs += [so_spec, so_spec]
    return pl.pallas_call(
        functools.partial(_retention_kernel, has_state=has_state, heads=heads,
                          fused_out=fused_out),
        out_shape=out_shape,
        grid=(b, H_B // heads),
        in_specs=in_specs,
        out_specs=out_specs,
        scratch_shapes=scratch,
        compiler_params=_params(2),
        name="retention_state" if has_state else "retention",
    )(*args)


CONV_ROWS = 1024
CONV_COLS = 256


def _conv_kernel(xc_ref, xl_ref, mod_ref, w_in_ref, cw_ref, w_out_ref, g_ref, b_ref, o_ref, *,
                 n_ctx_tiles, ctx_seq_len, lat_seq_len):
    x = _both_streams_rows(xc_ref, xl_ref, n_ctx_tiles)
    h = _modulated(x, mod_ref)
    rows = x.shape[0]
    is_ctx = pl.program_id(0) < n_ctx_tiles
    row = lax.broadcasted_iota(jnp.int32, (rows, 1), 0)
    pos = jnp.where(is_ctx, row % ctx_seq_len, row % lat_seq_len)
    has_prev = pos > 0
    has_next = pos < jnp.where(is_ctx, ctx_seq_len - 1, lat_seq_len - 1)
    e = D_MODEL
    for c in range(e // CONV_COLS):
        cols = slice(c * CONV_COLS, (c + 1) * CONV_COLS)

        def proj(part):
            lo = part * e + c * CONV_COLS
            return jnp.dot(h, w_in_ref[:, lo:lo + CONV_COLS], preferred_element_type=F32)

        p = proj(1) * proj(2)
        prev = jnp.where(has_prev, pltpu.roll(p, 1, 0), 0.0)
        nxt = jnp.where(has_next, pltpu.roll(p, rows - 1, 0), 0.0)
        conv = prev * cw_ref[0:1, cols] + p * cw_ref[1:2, cols] + nxt * cw_ref[2:3, cols]
        y = (proj(0) * conv * _silu(proj(3))).astype(BF16)
        part = jnp.dot(y, w_out_ref[cols, :].astype(BF16), preferred_element_type=F32)
        if c == 0:
            o_ref[...] = part
        else:
            o_ref[...] += part
    gate = mod_ref[:, 2 * D_MODEL:3 * D_MODEL]
    o_ref[...] = _residual_layer_norm(x, o_ref[...], gate, g_ref[...], b_ref[...])


def _conv_layer(x_ctx, x_lat, mod3, w_in, conv_w, w_out, j, ln_g, ln_b, layer, ctx_seq_len,
                lat_seq_len):
    m_rows = x_ctx.shape[0] + x_lat.shape[0]
    tm = CONV_ROWS
    n_ctx_tiles = x_ctx.shape[0] // tm
    const2 = lambda m: (0, 0)
    return pl.pallas_call(
        functools.partial(_conv_kernel, n_ctx_tiles=n_ctx_tiles, ctx_seq_len=ctx_seq_len,
                          lat_seq_len=lat_seq_len),
        out_shape=jax.ShapeDtypeStruct((m_rows, D_MODEL), F32),
        grid=(m_rows // tm,),
        in_specs=_both_streams_specs(x_ctx, x_lat, tm) + [
                  pl.BlockSpec((None, 1, 3 * D_MODEL),
                               _both_streams_mod_map(layer, n_ctx_tiles, tm, lat_seq_len)),
                  _weight_spec(w_in, j),
                  pl.BlockSpec((None, 3, D_MODEL), lambda m: (j, 0, 0)),
                  _weight_spec(w_out, j),
                  pl.BlockSpec((1, D_MODEL), const2),
                  pl.BlockSpec((1, D_MODEL), const2)],
        out_specs=pl.BlockSpec((tm, D_MODEL), lambda m: (m, 0)),
        compiler_params=_params(1),
        name="conv_layer",
    )(x_ctx, x_lat, mod3, w_in, conv_w, w_out, ln_g.reshape(1, D_MODEL), ln_b.reshape(1, D_MODEL))


def _separate(xp, xs, m_ctx):
    if xp[0] is xs[0]:
        return (xp[0][:m_ctx], 0), (xs[0][m_ctx:], 0)
    return xp, xs


def kernel(x_prompt, x_sample, cache_k, cache_v, state_fwd, state_bwd, c, c_ctx, w_mod, b_mod, ln_g,
           ln_b, w_in_a, lam_a, subln_a, w_out_a, w_in_b, decay_fwd, decay_bwd, w_out_b, w_in_c,
           conv_c, w_out_c):
    bp, tp, d = x_prompt.shape
    bs, ts, _ = x_sample.shape
    cvec = jnp.concatenate([c, c_ctx[None], jnp.zeros((MOD_ROWS - bs - 1, d), F32)], axis=0)
    mod3 = _modulation(cvec, w_mod, b_mod)
    rope_tables = _rope_tables(ts)

    m_ctx, m_lat = bp * tp, bs * ts
    xp = (x_prompt.reshape(m_ctx, d), 0)
    xs = (x_sample.reshape(m_lat, d), 0)
    n_attn = (DEPTH + N_MIXERS - 1) // N_MIXERS
    new_cache_k = new_cache_v = jax.ShapeDtypeStruct((bp, n_attn, tp, H_A, DV_A), F32)
    new_sf, new_sb = [], []
    for i in range(DEPTH):
        kind, j = i % N_MIXERS, i // N_MIXERS
        if kind == 0:
            x_ctx, new_cache_k, new_cache_v = _ctx_attention_layer(
                xp, m_ctx, tp, mod3, w_in_a, w_out_a, j, lam_a[j], subln_a[j], ln_g[i], ln_b[i], i,
                (new_cache_k, new_cache_v))
            qkvz = _attn_in(xs, m_lat, mod3, w_in_a, j, i, ts, rope_tables)
            xs = (_attention(qkvz, 0, m_lat, xs, ts, (cache_k, cache_v, j), mod3, lam_a[j],
                             subln_a[j], w_out_a, j, ln_g[i], ln_b[i], i, True), 0)
            xp = (x_ctx, 0)
        else:
            xp, xs = _separate(xp, xs, m_ctx)
            if kind == 1:
                q, kt, v, g = _ret_in(xp[0], xs[0], mod3, w_in_b, j, i, ts)
                x_ctx, s_f, s_b = _retention(
                    q, kt, v, g, tp, bp, 0, H_B, decay_fwd[j], decay_bwd[j], None,
                    out_proj=(xp[0], mod3, _mod_row_map(i, None, tp), w_out_b, j, ln_g[i],
                              ln_b[i]))
                new_sf.append(s_f)
                new_sb.append(s_b)
                (x_lat,) = _retention(
                    q, kt, v, g, ts, bs, m_ctx, 1, decay_fwd[j], decay_bwd[j],
                    (state_fwd, state_bwd, j),
                    out_proj=(xs[0], mod3, _mod_row_map(i, ts, ts), w_out_b, j, ln_g[i], ln_b[i]))
                xp, xs = (x_ctx, 0), (x_lat, 0)
            else:
                x_all = _conv_layer(xp[0], xs[0], mod3, w_in_c, conv_c, w_out_c, j, ln_g[i],
                                    ln_b[i], i, tp, ts)
                xp, xs = (x_all, 0), (x_all, m_ctx)
    xp, xs = _separate(xp, xs, m_ctx)
    y_prompt = xp[0].reshape(bp, tp, d)
    y_sample = xs[0].reshape(bs, ts, d)
    new_state_fwd = jnp.concatenate(new_sf, axis=1)
    new_state_bwd = jnp.concatenate(new_sb, axis=1)
    return (y_prompt, y_sample, new_cache_k, new_cache_v, new_state_fwd, new_state_bwd)
```

```python
import functools
import math

import jax
import jax.numpy as jnp
from jax import lax
from jax.experimental import pallas as pl
from jax.experimental.pallas import tpu as pltpu

F32 = jnp.float32
BF16 = jnp.bfloat16

D_MODEL = 1024
DEPTH = 4
N_MIXERS = 3
GRID_W = 64
H_A = 8
DH_A = 64
DV_A = 128
ROPE_HALF = DH_A // 2
SCORE_SCALE = DH_A ** -0.5 * math.log2(math.e)
H_B = 4
DK_B = 256
DV_B = 512
E_B = H_B * DV_B
CHUNK = 256
LAT_RET_HEADS = 2
ALPHA = (2.0 * DEPTH) ** 0.25
ROPE_BASE = 10000.0
LN_EPS = 1e-5
RMS_EPS = 1e-6

MOD_ROWS = 8
CTX_ROW = 4
VMEM_LIMIT_BYTES = 58 * 1024 * 1024
ROW_TILE = 512


def _params(n_axes):
    return pltpu.CompilerParams(dimension_semantics=("arbitrary",) * n_axes,
                                vmem_limit_bytes=VMEM_LIMIT_BYTES)


def _silu(x):
    return x * jax.nn.sigmoid(x)


def _residual_layer_norm(x, out, gate, g, b):
    r = ALPHA * x + gate * out
    mu = jnp.mean(r, axis=-1, keepdims=True)
    d = r - mu
    var = jnp.mean(d * d, axis=-1, keepdims=True)
    return d * lax.rsqrt(var + LN_EPS) * g + b


def _modulated(x, mod_ref):
    shift = mod_ref[:, 0:D_MODEL]
    scale = mod_ref[:, D_MODEL:2 * D_MODEL]
    return x * (1.0 + scale) + shift


def _mod_row_map(layer, rows_per_batch, tile):
    if rows_per_batch is None:
        return lambda m, *_: (layer * MOD_ROWS + CTX_ROW, 0, 0)
    return lambda m, *_: (layer * MOD_ROWS + (m * tile) // rows_per_batch, 0, 0)


def _both_streams_specs(x_ctx, x_lat, tm):
    n_ctx_tiles = x_ctx.shape[0] // tm
    return [pl.BlockSpec((tm, D_MODEL), lambda m: (jnp.minimum(m, n_ctx_tiles - 1), 0)),
            pl.BlockSpec((tm, D_MODEL), lambda m: (jnp.maximum(m - n_ctx_tiles, 0), 0))]


def _both_streams_rows(x_ctx_ref, x_lat_ref, n_ctx_tiles):
    return jnp.where(pl.program_id(0) < n_ctx_tiles, x_ctx_ref[...], x_lat_ref[...])


def _both_streams_mod_map(layer, n_ctx_tiles, tm, lat_rows_per_batch):
    def index_map(m):
        lat_row = ((m - n_ctx_tiles) * tm) // lat_rows_per_batch
        return (layer * MOD_ROWS + jnp.where(m < n_ctx_tiles, CTX_ROW, lat_row), 0, 0)
    return index_map


def _weight_spec(w, j):
    return pl.BlockSpec((None,) + w.shape[1:], lambda *_: (j, 0, 0), pipeline_mode=pl.Buffered(1))


WEIGHT_CHUNK_COLS = 1024
WEIGHT_COPIES_IN_FLIGHT = 2


class _StagedWeight:
    @staticmethod
    def scratch(w, chunk_cols):
        return [pltpu.VMEM(w.shape[1:], w.dtype),
                pltpu.SemaphoreType.DMA((w.shape[2] // chunk_cols,))]

    def __init__(self, w_hbm_ref, w_vmem_ref, sem_ref, j, chunk_cols):
        self.copies = [
            pltpu.make_async_copy(w_hbm_ref.at[j, :, pl.ds(c * chunk_cols, chunk_cols)],
                                  w_vmem_ref.at[:, pl.ds(c * chunk_cols, chunk_cols)], sem_ref.at[c])
            for c in range(w_vmem_ref.shape[1] // chunk_cols)]

    def _arrive(self, c):
        self.copies[c].wait()
        if c + WEIGHT_COPIES_IN_FLIGHT < len(self.copies):
            self.copies[c + WEIGHT_COPIES_IN_FLIGHT].start()

    def run(self, body):
        first = pl.program_id(0) == 0

        @pl.when(first)
        def _():
            for cp in self.copies[:WEIGHT_COPIES_IN_FLIGHT]:
                cp.start()
            body(self._arrive)

        @pl.when(jnp.logical_not(first))
        def _():
            body(lambda c: None)


def _mod_kernel(cv_ref, w_ref, b_ref, o_ref):
    s = _silu(cv_ref[...])
    o_ref[...] = jnp.dot(s, w_ref[...], preferred_element_type=F32) + b_ref[...]


def _modulation(cvec, w_mod, b_mod):
    n = 3 * D_MODEL
    tn = n
    out = pl.pallas_call(
        _mod_kernel,
        out_shape=jax.ShapeDtypeStruct((DEPTH, MOD_ROWS, n), F32),
        grid=(DEPTH, n // tn),
        in_specs=[pl.BlockSpec((MOD_ROWS, D_MODEL), lambda i, j: (0, 0)),
                  pl.BlockSpec((None, D_MODEL, tn), lambda i, j: (i, 0, j)),
                  pl.BlockSpec((None, 1, tn), lambda i, j: (i, 0, j))],
        out_specs=pl.BlockSpec((None, MOD_ROWS, tn), lambda i, j: (i, 0, j)),
        compiler_params=_params(2),
        name="modulation",
    )(cvec, w_mod, b_mod.reshape(DEPTH, 1, n))
    return out.reshape(DEPTH * MOD_ROWS, 1, n)


def _rope(xh, cos4, sin4, first_half):
    swapped = jnp.where(first_half, pltpu.roll(xh, DV_A - ROPE_HALF, 1),
                        pltpu.roll(xh, ROPE_HALF, 1))
    return xh * cos4 + swapped * sin4


def _store_heads(o_ref, x, seq_len, head0=0):
    heads = slice(head0, head0 + x.shape[1] // DV_A)
    for b in range(o_ref.shape[0]):
        xb = x[b * seq_len:(b + 1) * seq_len].reshape(seq_len, x.shape[1] // DV_A, DV_A)
        if len(o_ref.shape) == 4:
            o_ref[b, :, heads, :] = xb
        else:
            o_ref[b, 0, :, heads, :] = xb
            for s in range(1, o_ref.shape[1]):
                o_ref[b, s, :, heads, :] = jnp.zeros_like(xb)


def _attn_in_body(x_ref, mod_ref, w_ref, rope_refs, q_ref, k_ref, vt_ref, z_ref, cache_refs, seq_len):
    h = _modulated(x_ref[...], mod_ref)
    if rope_refs is not None:
        lane = lax.broadcasted_iota(jnp.int32, (1, DV_A), 1)
        first_half = (lane % DH_A) < ROPE_HALF
        cos4 = rope_refs[0][...]
        sin4 = rope_refs[1][...]
    q_all = jnp.dot(h, w_ref[:, 0:D_MODEL], preferred_element_type=F32)
    k_all = jnp.dot(h, w_ref[:, D_MODEL:2 * D_MODEL], preferred_element_type=F32)
    if cache_refs is not None:
        _store_heads(cache_refs[0], k_all, seq_len)
    for hd in range(H_A):
        cols = slice(hd * DV_A, (hd + 1) * DV_A)
        q = q_all[:, cols]
        k = k_all[:, cols]
        if rope_refs is not None:
            q = _rope(q, cos4, sin4, first_half)
            k = _rope(k, cos4, sin4, first_half)
        q_ref[:, cols] = (q * SCORE_SCALE).astype(BF16)
        k_ref[:, cols] = k.astype(BF16)
    v = jnp.dot(h, w_ref[:, 2 * D_MODEL:3 * D_MODEL], preferred_element_type=F32)
    if cache_refs is not None:
        _store_heads(cache_refs[1], v, seq_len)
    vt_ref[...] = v.T.astype(BF16)
    z_ref[...] = jnp.dot(h, w_ref[:, 3 * D_MODEL:4 * D_MODEL],
                         preferred_element_type=F32).astype(z_ref.dtype)


def _attn_in_kernel(x_ref, mod_ref, w_ref, cos_ref, sin_ref, q_ref, k_ref, vt_ref, z_ref):
    _attn_in_body(x_ref, mod_ref, w_ref, (cos_ref, sin_ref), q_ref, k_ref, vt_ref, z_ref, None, None)


def _attn_in(x, m_rows, mod3, w_in, j, layer, rows_per_batch, rope_tables):
    x2d, x_first_row = x
    tm = ROW_TILE
    seq_tiles = rows_per_batch // tm
    row_spec = pl.BlockSpec((tm, D_MODEL), lambda m: (m, 0))
    tab_spec = pl.BlockSpec((tm, DV_A), lambda m: (m % seq_tiles, 0))
    return pl.pallas_call(
        _attn_in_kernel,
        out_shape=[jax.ShapeDtypeStruct((m_rows, D_MODEL), BF16),
                   jax.ShapeDtypeStruct((m_rows, D_MODEL), BF16),
                   jax.ShapeDtypeStruct((D_MODEL, m_rows), BF16),
                   jax.ShapeDtypeStruct((m_rows, D_MODEL), BF16)],
        grid=(m_rows // tm,),
        in_specs=[pl.BlockSpec((tm, D_MODEL), lambda m: (x_first_row // tm + m, 0)),
                  pl.BlockSpec((None, 1, 3 * D_MODEL), _mod_row_map(layer, rows_per_batch, tm)),
                  _weight_spec(w_in, j), tab_spec, tab_spec],
        out_specs=[row_spec, row_spec, pl.BlockSpec((D_MODEL, tm), lambda m: (0, m)), row_spec],
        compiler_params=_params(1),
        name="attn_in_rope",
    )(x2d, mod3, w_in, *rope_tables)


ONES_ROWS = 16
CTX_ATTN_SEQS = 2
ATTN_ROWS = 512
ATTN_SUBBLOCKS = 1


def _slab_reduce(op, x):
    parts = [x[i:i + 8] for i in range(0, x.shape[0], 8)]
    while len(parts) > 1:
        parts = [op(parts[i], parts[i + 1]) for i in range(0, len(parts) - 1, 2)] + (
            [parts[-1]] if len(parts) % 2 else [])
    return parts[0]


def _attn_kernel(*refs, layer_idx, has_ctx, n_seq, shared_keys, before_unit=None):
    if has_ctx:
        (q_ref, k_ref, vt_ref, z_ref, x_ref, kc_ref, vc_ref, mod_ref, lam_ref, subln_ref,
         w_ref, g_ref, b_ref, o_ref, y_ref, kcb_ref, vct_ref) = refs

        @pl.when(pl.program_id(1) == 0)
        def _():
            n_ctx = kc_ref.shape[0]
            kcb_ref[...] = kc_ref[...].reshape(n_ctx, D_MODEL).astype(BF16)
            vct_ref[...] = vc_ref[...].reshape(n_ctx, D_MODEL).T.astype(BF16)
    else:
        (q_ref, k_ref, vt_ref, z_ref, x_ref, mod_ref, lam_ref, subln_ref,
         w_ref, g_ref, b_ref, o_ref, y_ref) = refs
    tq = q_ref.shape[0] // n_seq
    lam_init = 0.8 - 0.6 * math.exp(-0.3 * layer_idx)
    lm = lam_ref[...]
    lam = (jnp.exp(jnp.sum(lm[0:1] * lm[1:2], axis=-1, keepdims=True))
           - jnp.exp(jnp.sum(lm[2:3] * lm[3:4], axis=-1, keepdims=True)) + lam_init)
    lane = lax.broadcasted_iota(jnp.int32, (1, DV_A), 1)
    first = lane < DH_A
    subln = jnp.broadcast_to(subln_ref[...], (DV_A, tq))
    nt = (((1,), (1,)), ((), ()))
    t = k_ref.shape[0] if shared_keys else k_ref.shape[0] // n_seq

    def key_rows(sq):
        return slice(0, t) if shared_keys else slice(sq * t, (sq + 1) * t)

    def scores(unit):
        sq, hd = unit
        cols = slice(hd * DV_A, (hd + 1) * DV_A)
        qh = q_ref[sq * tq:(sq + 1) * tq, cols]
        zero = jnp.zeros_like(qh)
        qq = jnp.concatenate([jnp.where(first, qh, zero), jnp.where(first, zero, qh)], axis=0)
        parts = [lax.dot_general(k_ref[key_rows(sq), cols], qq, nt, preferred_element_type=F32)]
        if has_ctx:
            parts.append(lax.dot_general(kcb_ref[:, cols], qq, nt, preferred_element_type=F32))
        return parts

    def exps(parts):
        m8 = functools.reduce(jnp.maximum, [_slab_reduce(jnp.maximum, s) for s in parts])
        m = jnp.max(m8, axis=0, keepdims=True)
        return [jnp.exp2(s - m).astype(BF16) for s in parts]

    def with_ones(vals_t):
        return jnp.concatenate([vals_t, jnp.ones((ONES_ROWS, vals_t.shape[1]), BF16)], axis=0)

    def finish(unit, es):
        sq, hd = unit
        cols = slice(hd * DV_A, (hd + 1) * DV_A)
        rows = slice(sq * tq, (sq + 1) * tq)
        acc = jnp.dot(with_ones(vt_ref[cols, key_rows(sq)]), es[0], preferred_element_type=F32)
        if has_ctx:
            acc = acc + jnp.dot(with_ones(vct_ref[cols, :]), es[1], preferred_element_type=F32)
        inv = 1.0 / acc[DV_A:DV_A + 1, :]
        ot = acc[:DV_A, :tq] * inv[:, :tq] - acc[:DV_A, tq:] * (inv[:, tq:] * lam)
        ot = ot * lax.rsqrt(jnp.mean(ot * ot, axis=0, keepdims=True) + RMS_EPS)
        ot = ot * subln * (1.0 - lam_init)
        y_ref[rows, cols] = (ot.T * _silu(z_ref[rows, cols].astype(F32))).astype(BF16)

    units = [(sq, hd) for hd in range(H_A) for sq in range(n_seq)]
    if before_unit is not None:
        before_unit(-1)
    s_ahead = {u: scores(units[u]) for u in range(min(2, len(units)))}
    e_ahead = {0: exps(s_ahead.pop(0))}
    for u in range(len(units)):
        if before_unit is not None:
            before_unit(u)
        if u + 2 < len(units):
            s_ahead[u + 2] = scores(units[u + 2])
        if u + 1 < len(units):
            e_ahead[u + 1] = exps(s_ahead.pop(u + 1))
        finish(units[u], e_ahead.pop(u))
    out = jnp.dot(y_ref[...], w_ref[...].astype(BF16), preferred_element_type=F32)
    gate = mod_ref[:, 2 * D_MODEL:3 * D_MODEL]
    o_ref[...] = _residual_layer_norm(x_ref[...], out, gate, g_ref[...], b_ref[...])


def _attention(qkvz, first_row, m_rows, x, seq_len, ctx, mod3, lam, subln, w_out, j, ln_g, ln_b,
               layer, per_batch_rows):
    q, k, vt, z = qkvz
    t = seq_len
    b = m_rows // t
    has_ctx = ctx is not None
    if t <= 256 and not per_batch_rows and not has_ctx and b % 2 == 0:
        rows, n_seq, shared_keys, n_kseq = 2 * t, 2, False, 2
    else:
        rows, n_seq, shared_keys, n_kseq = ATTN_ROWS, ATTN_SUBBLOCKS, True, 1
    nq = n_kseq * t // rows
    x, x_first_row = x
    r0, k0 = first_row // rows, first_row // (n_kseq * t)
    x_spec = pl.BlockSpec((rows, D_MODEL), lambda i, j: (x_first_row // rows + i * nq + j, 0))
    q_spec = pl.BlockSpec((rows, D_MODEL), lambda i, j: (r0 + i * nq + j, 0))
    o_spec = pl.BlockSpec((rows, D_MODEL), lambda i, j: (i * nq + j, 0))
    k_spec = pl.BlockSpec((n_kseq * t, D_MODEL), lambda i, j: (k0 + i, 0))
    vt_spec = pl.BlockSpec((D_MODEL, n_kseq * t), lambda i, j: (0, k0 + i))
    if per_batch_rows:
        mod_map = lambda i, j: (layer * MOD_ROWS + i, 0, 0)
    else:
        mod_map = lambda i, j: (layer * MOD_ROWS + CTX_ROW, 0, 0)
    const2 = lambda i, j: (0, 0)
    in_specs = [q_spec, k_spec, vt_spec, q_spec, x_spec]
    args = [q, k, vt, z, x]
    scratch = [pltpu.VMEM((rows, D_MODEL), BF16)]
    if has_ctx:
        cache_k, cache_v, jj = ctx
        n_ctx = cache_k.shape[2]
        scratch += [pltpu.VMEM((n_ctx, D_MODEL), BF16), pltpu.VMEM((D_MODEL, n_ctx), BF16)]
        c_spec = pl.BlockSpec((None, None, n_ctx, H_A, DV_A),
                              lambda i, j: (i, jj, 0, 0, 0))
        in_specs += [c_spec, c_spec]
        args += [cache_k, cache_v]
    in_specs += [pl.BlockSpec((None, 1, 3 * D_MODEL), mod_map),
                 pl.BlockSpec((4, DH_A), const2),
                 pl.BlockSpec((DV_A, 1), const2),
                 _weight_spec(w_out, j),
                 pl.BlockSpec((1, D_MODEL), const2),
                 pl.BlockSpec((1, D_MODEL), const2)]
    args += [mod3, lam, subln.reshape(DV_A, 1), w_out, ln_g.reshape(1, D_MODEL),
             ln_b.reshape(1, D_MODEL)]
    return pl.pallas_call(
        functools.partial(_attn_kernel, layer_idx=layer, has_ctx=has_ctx, n_seq=n_seq,
                          shared_keys=shared_keys),
        out_shape=jax.ShapeDtypeStruct((m_rows, D_MODEL), F32),
        grid=(b // n_kseq, nq),
        in_specs=in_specs,
        out_specs=o_spec,
        scratch_shapes=scratch,
        compiler_params=_params(2),
        name="diff_attn_ctx" if has_ctx else "diff_attn",
    )(*args)


def _ctx_attn_layer_kernel(*refs, layer_idx, seq_len, n_aliased):
    x_ref, mod_ref, w_in_ref, lam_ref, subln_ref, w_out_ref, g_ref, b_ref = refs[:8]
    o_ref, ck_ref, cv_ref, q_ref, k_ref, vt_ref, z_ref, y_ref = refs[8 + n_aliased:]
    _attn_in_body(x_ref, mod_ref, w_in_ref, None, q_ref, k_ref, vt_ref, z_ref, (ck_ref, cv_ref),
                  seq_len)
    _attn_kernel(q_ref, k_ref, vt_ref, z_ref, x_ref, mod_ref, lam_ref, subln_ref, w_out_ref, g_ref,
                 b_ref, o_ref, y_ref, layer_idx=layer_idx, has_ctx=False,
                 n_seq=x_ref.shape[0] // seq_len, shared_keys=False)


def _ctx_attention_layer(x, m_rows, seq_len, mod3, w_in, w_out, j, lam, subln, ln_g, ln_b, layer,
                         cache_out):
    x2d, x_first_row = x
    rows = CTX_ATTN_SEQS * seq_len
    const2 = lambda m: (0, 0)
    in_specs = [pl.BlockSpec((rows, D_MODEL), lambda m: (x_first_row // rows + m, 0)),
                pl.BlockSpec((None, 1, 3 * D_MODEL), lambda m: (layer * MOD_ROWS + CTX_ROW, 0, 0)),
                _weight_spec(w_in, j),
                pl.BlockSpec((4, DH_A), const2),
                pl.BlockSpec((DV_A, 1), const2),
                _weight_spec(w_out, j),
                pl.BlockSpec((1, D_MODEL), const2),
                pl.BlockSpec((1, D_MODEL), const2)]
    args = [x2d, mod3, w_in, lam, subln.reshape(DV_A, 1), w_out, ln_g.reshape(1, D_MODEL),
            ln_b.reshape(1, D_MODEL)]
    new_k, new_v = cache_out
    cache_shape = jax.ShapeDtypeStruct(new_k.shape, F32)
    aliases = {}
    if j == 0:
        cache_spec = pl.BlockSpec((CTX_ATTN_SEQS, new_k.shape[1], seq_len, H_A, DV_A),
                                  lambda m: (m, 0, 0, 0, 0))
    else:
        cache_spec = pl.BlockSpec((CTX_ATTN_SEQS, None, seq_len, H_A, DV_A),
                                  lambda m: (m, j, 0, 0, 0))
        in_specs += [pl.BlockSpec(memory_space=pl.ANY)] * 2
        args += [new_k, new_v]
        aliases = {len(args) - 2: 1, len(args) - 1: 2}
    return pl.pallas_call(
        functools.partial(_ctx_attn_layer_kernel, layer_idx=layer, seq_len=seq_len,
                          n_aliased=len(aliases)),
        out_shape=[jax.ShapeDtypeStruct((m_rows, D_MODEL), F32), cache_shape, cache_shape],
        grid=(m_rows // rows,),
        in_specs=in_specs,
        out_specs=[pl.BlockSpec((rows, D_MODEL), lambda m: (m, 0)), cache_spec, cache_spec],
        scratch_shapes=[pltpu.VMEM((rows, D_MODEL), BF16), pltpu.VMEM((rows, D_MODEL), BF16),
                        pltpu.VMEM((D_MODEL, rows), BF16), pltpu.VMEM((rows, D_MODEL), BF16),
                        pltpu.VMEM((rows, D_MODEL), BF16)],
        input_output_aliases=aliases,
        compiler_params=_params(1),
        name="ctx_attn_layer",
    )(*args)


def _rope_tables(n_tokens):
    rows = n_tokens // GRID_W
    r = jnp.repeat(jnp.arange(rows, dtype=F32), GRID_W)
    col = jnp.tile(jnp.arange(GRID_W, dtype=F32), rows)
    n_freq = DH_A // 4
    inv = ROPE_BASE ** (-jnp.arange(n_freq, dtype=F32) / n_freq)
    ang = jnp.concatenate([r[:, None] * inv, col[:, None] * inv], -1)
    cos, sin = jnp.cos(ang), jnp.sin(ang)
    return jnp.tile(cos, (1, 4)), jnp.concatenate([-sin, sin, -sin, sin], -1)


def _ret_in_kernel(xc_ref, xl_ref, mod_ref, w_hbm_ref, q_ref, kt_ref, v_ref, g_ref, w_ref, sem_ref, *,
                   n_ctx_tiles, j):
    nq = H_B * DK_B
    cc = WEIGHT_CHUNK_COLS
    assert nq == cc and E_B % cc == 0
    staged = _StagedWeight(w_hbm_ref, w_ref, sem_ref, j, cc)

    def body(arrive):
        h = _modulated(_both_streams_rows(xc_ref, xl_ref, n_ctx_tiles), mod_ref)
        arrive(0)
        q_ref[...] = jnp.dot(h, w_ref[:, 0:nq], preferred_element_type=F32).astype(BF16)
        arrive(1)
        k = jnp.dot(h, w_ref[:, nq:2 * nq], preferred_element_type=F32) * (DK_B ** -0.5)
        kt_ref[...] = k.T
        for c in range(E_B // cc):
            arrive(2 + c)
            v = jnp.dot(h, w_ref[:, 2 * nq + c * cc:2 * nq + (c + 1) * cc],
                        preferred_element_type=F32)
            v_ref[:, c * cc:(c + 1) * cc] = v.astype(BF16)
        for c in range(E_B // cc):
            arrive(2 + E_B // cc + c)
            g_ref[:, c * cc:(c + 1) * cc] = jnp.dot(
                h, w_ref[:, 2 * nq + E_B + c * cc:2 * nq + E_B + (c + 1) * cc],
                preferred_element_type=F32).astype(BF16)

    staged.run(body)


def _ret_in(x_ctx, x_lat, mod3, w_in, j, layer, lat_rows_per_batch):
    m_rows = x_ctx.shape[0] + x_lat.shape[0]
    tm = ROW_TILE
    nq = H_B * DK_B
    n_ctx_tiles = x_ctx.shape[0] // tm
    return pl.pallas_call(
        functools.partial(_ret_in_kernel, n_ctx_tiles=n_ctx_tiles, j=j),
        out_shape=[jax.ShapeDtypeStruct((m_rows, nq), BF16),
                   jax.ShapeDtypeStruct((nq, m_rows), F32),
                   jax.ShapeDtypeStruct((m_rows, E_B), BF16),
                   jax.ShapeDtypeStruct((m_rows, E_B), BF16)],
        grid=(m_rows // tm,),
        in_specs=_both_streams_specs(x_ctx, x_lat, tm) + [
            pl.BlockSpec((None, 1, 3 * D_MODEL),
                         _both_streams_mod_map(layer, n_ctx_tiles, tm, lat_rows_per_batch)),
            pl.BlockSpec(memory_space=pl.ANY)],
        out_specs=[pl.BlockSpec((tm, nq), lambda m: (m, 0)),
                   pl.BlockSpec((nq, tm), lambda m: (0, m)),
                   pl.BlockSpec((tm, E_B), lambda m: (m, 0)),
                   pl.BlockSpec((tm, E_B), lambda m: (m, 0))],
        scratch_shapes=_StagedWeight.scratch(w_in, WEIGHT_CHUNK_COLS),
        compiler_params=_params(1),
        name="ret_in",
    )(x_ctx, x_lat, mod3, w_in)


def _retention_kernel(*refs, has_state, heads, fused_out):
    refs = list(refs)
    q_ref, kt_ref, v_ref, g_ref, af_ref, ab_ref = refs[:6]
    del refs[:6]
    if has_state:
        s0f_ref, s0b_ref = refs[:2]
        del refs[:2]
    if fused_out:
        x_ref, mod_ref, w_ref, lng_ref, lnb_ref = refs[:5]
        del refs[:5]
        o_ref = refs.pop(0)
        y_ref = refs.pop()
    else:
        y_ref = refs.pop(0)
    if not has_state:
        sf_ref, sb_ref = refs
    t = q_ref.shape[0]
    nc = t // CHUNK
    row = lax.broadcasted_iota(jnp.int32, (CHUNK, CHUNK), 0).astype(F32)
    col = lax.broadcasted_iota(jnp.int32, (CHUNK, CHUNK), 1).astype(F32)
    diff = row - col
    idx_col = lax.broadcasted_iota(jnp.int32, (CHUNK, 1), 0).astype(F32)
    idx_row = lax.broadcasted_iota(jnp.int32, (1, CHUNK), 1).astype(F32)

    def chunk(c):
        return slice(c * CHUNK, (c + 1) * CHUNK)

    for hh in range(heads):
        qk_cols = slice(hh * DK_B, (hh + 1) * DK_B)
        v_cols = slice(hh * DV_B, (hh + 1) * DV_B)
        lg_f = jnp.log1p(-jnp.exp(af_ref[hh]))
        lg_b = jnp.log1p(-jnp.exp(ab_ref[hh]))
        dmask = (jnp.where(diff >= 0, jnp.exp(jnp.maximum(diff, 0.0) * lg_f), 0.0)
                 + jnp.where(diff <= 0, jnp.exp(jnp.maximum(-diff, 0.0) * lg_b), 0.0))
        qd_f = jnp.exp((idx_col + 1.0) * lg_f)
        qd_b = jnp.exp((CHUNK - idx_col) * lg_b)
        kd_f = jnp.exp((CHUNK - 1.0 - idx_row) * lg_f)
        kd_b = jnp.exp(idx_row * lg_b)
        cd_f = jnp.exp(CHUNK * lg_f)
        cd_b = jnp.exp(CHUNK * lg_b)

        def states(order, kd, cd, s):
            seen = {}
            for n, c in enumerate(order):
                seen[c] = None if s is None else s.astype(BF16)
                if has_state and n == nc - 1:
                    return seen, None
                u = jnp.dot((kt_ref[qk_cols, chunk(c)] * kd).astype(BF16), v_ref[chunk(c), v_cols],
                            preferred_element_type=F32)
                s = u if s is None else s * cd + u
            return seen, s

        seen_f, s_f = states(range(nc), kd_f, cd_f, s0f_ref[hh] if has_state else None)
        seen_b, s_b = states(range(nc - 1, -1, -1), kd_b, cd_b, s0b_ref[hh] if has_state else None)
        if not has_state:
            sf_ref[hh] = s_f
            sb_ref[hh] = s_b
        for c in range(nc):
            qc = q_ref[chunk(c), qk_cols]
            qk = jnp.dot(qc, kt_ref[qk_cols, chunk(c)].astype(BF16), preferred_element_type=F32)
            o = jnp.dot((qk * dmask).astype(BF16), v_ref[chunk(c), v_cols],
                        preferred_element_type=F32)
            if seen_f[c] is not None:
                o = o + jnp.dot(qc, seen_f[c], preferred_element_type=F32) * qd_f
            if seen_b[c] is not None:
                o = o + jnp.dot(qc, seen_b[c], preferred_element_type=F32) * qd_b
            o = o * lax.rsqrt(jnp.mean(o * o, axis=-1, keepdims=True) + RMS_EPS)
            gate_pre = g_ref[chunk(c), v_cols].astype(F32)
            y_ref[chunk(c), v_cols] = (o * _silu(gate_pre)).astype(BF16)
    if fused_out:
        part = jnp.dot(y_ref[...], w_ref[...].astype(BF16), preferred_element_type=F32)
        gate = mod_ref[:, 2 * D_MODEL:3 * D_MODEL]
        head_steps = H_B // heads
        if head_steps == 1:
            o_ref[...] = _residual_layer_norm(x_ref[...], part, gate, lng_ref[...], lnb_ref[...])
        else:
            step = pl.program_id(1)

            @pl.when(step == 0)
            def _():
                o_ref[...] = part

            @pl.when(jnp.logical_and(step > 0, step < head_steps - 1))
            def _():
                o_ref[...] += part

            @pl.when(step == head_steps - 1)
            def _():
                o_ref[...] = _residual_layer_norm(x_ref[...], o_ref[...] + part, gate, lng_ref[...],
                                                  lnb_ref[...])


def _retention(q, kt, v, g, seq_len, n_seq, first_row, heads, decay_f, decay_b, states,
               out_proj=None):
    t = seq_len
    b = n_seq
    s0 = first_row // t
    has_state = states is not None
    fused_out = out_proj is not None
    q_spec = pl.BlockSpec((t, heads * DK_B), lambda i, h: (s0 + i, h))
    kt_spec = pl.BlockSpec((heads * DK_B, t), lambda i, h: (h, s0 + i))
    vg_spec = pl.BlockSpec((t, heads * DV_B), lambda i, h: (s0 + i, h))
    a_spec = pl.BlockSpec((heads, 1, 1), lambda i, h: (h, 0, 0))
    in_specs = [q_spec, kt_spec, vg_spec, vg_spec, a_spec, a_spec]
    args = [q, kt, v, g, decay_f.reshape(H_B, 1, 1), decay_b.reshape(H_B, 1, 1)]
    out_shape = [jax.ShapeDtypeStruct((b * t, E_B), BF16)]
    out_specs = [pl.BlockSpec((t, heads * DV_B), lambda i, h: (i, h))]
    if has_state:
        s_f, s_b, jj = states
        s_spec = pl.BlockSpec((None, None, heads, DK_B, DV_B), lambda i, h: (i, jj, h, 0, 0))
        in_specs += [s_spec, s_spec]
        args += [s_f, s_b]
    scratch = []
    if fused_out:
        x2d, mod3, mod_map, w_out, jw, ln_g, ln_b = out_proj
        x_spec = pl.BlockSpec((t, D_MODEL), lambda i, h: (i, 0))
        const2 = lambda i, h: (0, 0)
        if heads == H_B:
            w_spec = _weight_spec(w_out, jw)
        else:
            w_spec = pl.BlockSpec((None, heads * DV_B, D_MODEL), lambda i, h: (jw, h, 0))
        in_specs += [x_spec, pl.BlockSpec((None, 1, 3 * D_MODEL), mod_map), w_spec,
                     pl.BlockSpec((1, D_MODEL), const2), pl.BlockSpec((1, D_MODEL), const2)]
        args += [x2d, mod3, w_out, ln_g.reshape(1, D_MODEL), ln_b.reshape(1, D_MODEL)]
        out_shape = [jax.ShapeDtypeStruct((b * t, D_MODEL), F32)]
        out_specs = [x_spec]
        scratch = [pltpu.VMEM((t, heads * DV_B), BF16)]
    if not has_state:
        so_spec = pl.BlockSpec((None, None, heads, DK_B, DV_B), lambda i, h: (i, 0, h, 0, 0))
        out_shape += [jax.ShapeDtypeStruct((b, 1, H_B, DK_B, DV_B), F32)] * 2
        out_specs += [so_spec, so_spec]
    return pl.pallas_call(
        functools.partial(_retention_kernel, has_state=has_state, heads=heads,
                          fused_out=fused_out),
        out_shape=out_shape,
        grid=(b, H_B // heads),
        in_specs=in_specs,
        out_specs=out_specs,
        scratch_shapes=scratch,
        compiler_params=_params(2),
        name="retention_state" if has_state else "retention",
    )(*args)


CONV_ROWS = 1024
CONV_COLS = 256


def _conv_kernel(xc_ref, xl_ref, mod_ref, w_in_ref, cw_ref, w_out_ref, g_ref, b_ref, o_ref, *,
                 n_ctx_tiles, ctx_seq_len, lat_seq_len):
    x = _both_streams_rows(xc_ref, xl_ref, n_ctx_tiles)
    h = _modulated(x, mod_ref)
    rows = x.shape[0]
    is_ctx = pl.program_id(0) < n_ctx_tiles
    row = lax.broadcasted_iota(jnp.int32, (rows, 1), 0)
    pos = jnp.where(is_ctx, row % ctx_seq_len, row % lat_seq_len)
    has_prev = pos > 0
    has_next = pos < jnp.where(is_ctx, ctx_seq_len - 1, lat_seq_len - 1)
    e = D_MODEL
    for c in range(e // CONV_COLS):
        cols = slice(c * CONV_COLS, (c + 1) * CONV_COLS)

        def proj(part):
            lo = part * e + c * CONV_COLS
            return jnp.dot(h, w_in_ref[:, lo:lo + CONV_COLS], preferred_element_type=F32)

        p = proj(1) * proj(2)
        prev = jnp.where(has_prev, pltpu.roll(p, 1, 0), 0.0)
        nxt = jnp.where(has_next, pltpu.roll(p, rows - 1, 0), 0.0)
        conv = prev * cw_ref[0:1, cols] + p * cw_ref[1:2, cols] + nxt * cw_ref[2:3, cols]
        y = (proj(0) * conv * _silu(proj(3))).astype(BF16)
        part = jnp.dot(y, w_out_ref[cols, :].astype(BF16), preferred_element_type=F32)
        if c == 0:
            o_ref[...] = part
        else:
            o_ref[...] += part
    gate = mod_ref[:, 2 * D_MODEL:3 * D_MODEL]
    o_ref[...] = _residual_layer_norm(x, o_ref[...], gate, g_ref[...], b_ref[...])


def _conv_layer(x_ctx, x_lat, mod3, w_in, conv_w, w_out, j, ln_g, ln_b, layer, ctx_seq_len,
                lat_seq_len):
    m_rows = x_ctx.shape[0] + x_lat.shape[0]
    tm = CONV_ROWS
    n_ctx_tiles = x_ctx.shape[0] // tm
    const2 = lambda m: (0, 0)
    return pl.pallas_call(
        functools.partial(_conv_kernel, n_ctx_tiles=n_ctx_tiles, ctx_seq_len=ctx_seq_len,
                          lat_seq_len=lat_seq_len),
        out_shape=jax.ShapeDtypeStruct((m_rows, D_MODEL), F32),
        grid=(m_rows // tm,),
        in_specs=_both_streams_specs(x_ctx, x_lat, tm) + [
                  pl.BlockSpec((None, 1, 3 * D_MODEL),
                               _both_streams_mod_map(layer, n_ctx_tiles, tm, lat_seq_len)),
                  _weight_spec(w_in, j),
                  pl.BlockSpec((None, 3, D_MODEL), lambda m: (j, 0, 0)),
                  _weight_spec(w_out, j),
                  pl.BlockSpec((1, D_MODEL), const2),
                  pl.BlockSpec((1, D_MODEL), const2)],
        out_specs=pl.BlockSpec((tm, D_MODEL), lambda m: (m, 0)),
        compiler_params=_params(1),
        name="conv_layer",
    )(x_ctx, x_lat, mod3, w_in, conv_w, w_out, ln_g.reshape(1, D_MODEL), ln_b.reshape(1, D_MODEL))


def _separate(xp, xs, m_ctx):
    if xp[0] is xs[0]:
        return (xp[0][:m_ctx], 0), (xs[0][m_ctx:], 0)
    return xp, xs


def kernel(x_prompt, x_sample, cache_k, cache_v, state_fwd, state_bwd, c, c_ctx, w_mod, b_mod, ln_g,
           ln_b, w_in_a, lam_a, subln_a, w_out_a, w_in_b, decay_fwd, decay_bwd, w_out_b, w_in_c,
           conv_c, w_out_c):
    bp, tp, d = x_prompt.shape
    bs, ts, _ = x_sample.shape
    cvec = jnp.concatenate([c, c_ctx[None], jnp.zeros((MOD_ROWS - bs - 1, d), F32)], axis=0)
    mod3 = _modulation(cvec, w_mod, b_mod)
    rope_tables = _rope_tables(ts)

    m_ctx, m_lat = bp * tp, bs * ts
    xp = (x_prompt.reshape(m_ctx, d), 0)
    xs = (x_sample.reshape(m_lat, d), 0)
    n_attn = (DEPTH + N_MIXERS - 1) // N_MIXERS
    new_cache_k = new_cache_v = jax.ShapeDtypeStruct((bp, n_attn, tp, H_A, DV_A), F32)
    new_sf, new_sb = [], []
    for i in range(DEPTH):
        kind, j = i % N_MIXERS, i // N_MIXERS
        if kind == 0:
            x_ctx, new_cache_k, new_cache_v = _ctx_attention_layer(
                xp, m_ctx, tp, mod3, w_in_a, w_out_a, j, lam_a[j], subln_a[j], ln_g[i], ln_b[i], i,
                (new_cache_k, new_cache_v))
            qkvz = _attn_in(xs, m_lat, mod3, w_in_a, j, i, ts, rope_tables)
            xs = (_attention(qkvz, 0, m_lat, xs, ts, (cache_k, cache_v, j), mod3, lam_a[j],
                             subln_a[j], w_out_a, j, ln_g[i], ln_b[i], i, True), 0)
            xp = (x_ctx, 0)
        else:
            xp, xs = _separate(xp, xs, m_ctx)
            if kind == 1:
                q, kt, v, g = _ret_in(xp[0], xs[0], mod3, w_in_b, j, i, ts)
                x_ctx, s_f, s_b = _retention(
                    q, kt, v, g, tp, bp, 0, H_B, decay_fwd[j], decay_bwd[j], None,
                    out_proj=(xp[0], mod3, _mod_row_map(i, None, tp), w_out_b, j, ln_g[i],
                              ln_b[i]))
                new_sf.append(s_f)
                new_sb.append(s_b)
                (x_lat,) = _retention(
                    q, kt, v, g, ts, bs, m_ctx, LAT_RET_HEADS, decay_fwd[j], decay_bwd[j],
                    (state_fwd, state_bwd, j),
                    out_proj=(xs[0], mod3, _mod_row_map(i, ts, ts), w_out_b, j, ln_g[i], ln_b[i]))
                xp, xs = (x_ctx, 0), (x_lat, 0)
            else:
                x_all = _conv_layer(xp[0], xs[0], mod3, w_in_c, conv_c, w_out_c, j, ln_g[i],
                                    ln_b[i], i, tp, ts)
                xp, xs = (x_all, 0), (x_all, m_ctx)
    xp, xs = _separate(xp, xs, m_ctx)
    y_prompt = xp[0].reshape(bp, tp, d)
    y_sample = xs[0].reshape(bs, ts, d)
    new_state_fwd = jnp.concatenate(new_sf, axis=1)
    new_state_bwd = jnp.concatenate(new_sb, axis=1)
    return (y_prompt, y_sample, new_cache_k, new_cache_v, new_state_fwd, new_state_bwd)
```

```python
import functools
import math

import jax
import jax.numpy as jnp
from jax import lax
from jax.experimental import pallas as pl
from jax.experimental.pallas import tpu as pltpu

F32 = jnp.float32
BF16 = jnp.bfloat16

D_MODEL = 1024
DEPTH = 4
N_MIXERS = 3
GRID_W = 64
H_A = 8
DH_A = 64
DV_A = 128
ROPE_HALF = DH_A // 2
SCORE_SCALE = DH_A ** -0.5 * math.log2(math.e)
H_B = 4
DK_B = 256
DV_B = 512
E_B = H_B * DV_B
CHUNK = 256
LAT_RET_HEADS = 2
CTX_RET_SEQS = 2
ALPHA = (2.0 * DEPTH) ** 0.25
ROPE_BASE = 10000.0
LN_EPS = 1e-5
RMS_EPS = 1e-6

MOD_ROWS = 8
CTX_ROW = 4
VMEM_LIMIT_BYTES = 58 * 1024 * 1024
ROW_TILE = 512


def _params(n_axes):
    return pltpu.CompilerParams(dimension_semantics=("arbitrary",) * n_axes,
                                vmem_limit_bytes=VMEM_LIMIT_BYTES)


def _silu(x):
    return x * jax.nn.sigmoid(x)


def _residual_layer_norm(x, out, gate, g, b):
    r = ALPHA * x + gate * out
    mu = jnp.mean(r, axis=-1, keepdims=True)
    d = r - mu
    var = jnp.mean(d * d, axis=-1, keepdims=True)
    return d * lax.rsqrt(var + LN_EPS) * g + b


def _modulated(x, mod_ref):
    shift = mod_ref[:, 0:D_MODEL]
    scale = mod_ref[:, D_MODEL:2 * D_MODEL]
    return x * (1.0 + scale) + shift


def _mod_row_map(layer, rows_per_batch, tile):
    if rows_per_batch is None:
        return lambda m, *_: (layer * MOD_ROWS + CTX_ROW, 0, 0)
    return lambda m, *_: (layer * MOD_ROWS + (m * tile) // rows_per_batch, 0, 0)


def _both_streams_specs(x_ctx, x_lat, tm):
    n_ctx_tiles = x_ctx.shape[0] // tm
    return [pl.BlockSpec((tm, D_MODEL), lambda m: (jnp.minimum(m, n_ctx_tiles - 1), 0)),
            pl.BlockSpec((tm, D_MODEL), lambda m: (jnp.maximum(m - n_ctx_tiles, 0), 0))]


def _both_streams_rows(x_ctx_ref, x_lat_ref, n_ctx_tiles):
    return jnp.where(pl.program_id(0) < n_ctx_tiles, x_ctx_ref[...], x_lat_ref[...])


def _both_streams_mod_map(layer, n_ctx_tiles, tm, lat_rows_per_batch):
    def index_map(m):
        lat_row = ((m - n_ctx_tiles) * tm) // lat_rows_per_batch
        return (layer * MOD_ROWS + jnp.where(m < n_ctx_tiles, CTX_ROW, lat_row), 0, 0)
    return index_map


def _weight_spec(w, j):
    return pl.BlockSpec((None,) + w.shape[1:], lambda *_: (j, 0, 0), pipeline_mode=pl.Buffered(1))


WEIGHT_CHUNK_COLS = 1024
WEIGHT_COPIES_IN_FLIGHT = 2


class _StagedWeight:
    @staticmethod
    def scratch(w, chunk_cols):
        return [pltpu.VMEM(w.shape[1:], w.dtype),
                pltpu.SemaphoreType.DMA((w.shape[2] // chunk_cols,))]

    def __init__(self, w_hbm_ref, w_vmem_ref, sem_ref, j, chunk_cols):
        self.copies = [
            pltpu.make_async_copy(w_hbm_ref.at[j, :, pl.ds(c * chunk_cols, chunk_cols)],
                                  w_vmem_ref.at[:, pl.ds(c * chunk_cols, chunk_cols)], sem_ref.at[c])
            for c in range(w_vmem_ref.shape[1] // chunk_cols)]

    def _arrive(self, c):
        self.copies[c].wait()
        if c + WEIGHT_COPIES_IN_FLIGHT < len(self.copies):
            self.copies[c + WEIGHT_COPIES_IN_FLIGHT].start()

    def run(self, body):
        first = pl.program_id(0) == 0

        @pl.when(first)
        def _():
            for cp in self.copies[:WEIGHT_COPIES_IN_FLIGHT]:
                cp.start()
            body(self._arrive)

        @pl.when(jnp.logical_not(first))
        def _():
            body(lambda c: None)


def _mod_kernel(cv_ref, w_ref, b_ref, o_ref):
    s = _silu(cv_ref[...])
    o_ref[...] = jnp.dot(s, w_ref[...], preferred_element_type=F32) + b_ref[...]


def _modulation(cvec, w_mod, b_mod):
    n = 3 * D_MODEL
    tn = n
    out = pl.pallas_call(
        _mod_kernel,
        out_shape=jax.ShapeDtypeStruct((DEPTH, MOD_ROWS, n), F32),
        grid=(DEPTH, n // tn),
        in_specs=[pl.BlockSpec((MOD_ROWS, D_MODEL), lambda i, j: (0, 0)),
                  pl.BlockSpec((None, D_MODEL, tn), lambda i, j: (i, 0, j)),
                  pl.BlockSpec((None, 1, tn), lambda i, j: (i, 0, j))],
        out_specs=pl.BlockSpec((None, MOD_ROWS, tn), lambda i, j: (i, 0, j)),
        compiler_params=_params(2),
        name="modulation",
    )(cvec, w_mod, b_mod.reshape(DEPTH, 1, n))
    return out.reshape(DEPTH * MOD_ROWS, 1, n)


def _rope(xh, cos4, sin4, first_half):
    swapped = jnp.where(first_half, pltpu.roll(xh, DV_A - ROPE_HALF, 1),
                        pltpu.roll(xh, ROPE_HALF, 1))
    return xh * cos4 + swapped * sin4


def _store_heads(o_ref, x, seq_len, head0=0):
    heads = slice(head0, head0 + x.shape[1] // DV_A)
    for b in range(o_ref.shape[0]):
        xb = x[b * seq_len:(b + 1) * seq_len].reshape(seq_len, x.shape[1] // DV_A, DV_A)
        if len(o_ref.shape) == 4:
            o_ref[b, :, heads, :] = xb
        else:
            o_ref[b, 0, :, heads, :] = xb
            for s in range(1, o_ref.shape[1]):
                o_ref[b, s, :, heads, :] = jnp.zeros_like(xb)


def _attn_in_body(x_ref, mod_ref, w_ref, rope_refs, q_ref, k_ref, vt_ref, z_ref, cache_refs, seq_len):
    h = _modulated(x_ref[...], mod_ref)
    if rope_refs is not None:
        lane = lax.broadcasted_iota(jnp.int32, (1, DV_A), 1)
        first_half = (lane % DH_A) < ROPE_HALF
        cos4 = rope_refs[0][...]
        sin4 = rope_refs[1][...]
    q_all = jnp.dot(h, w_ref[:, 0:D_MODEL], preferred_element_type=F32)
    k_all = jnp.dot(h, w_ref[:, D_MODEL:2 * D_MODEL], preferred_element_type=F32)
    if cache_refs is not None:
        _store_heads(cache_refs[0], k_all, seq_len)
    for hd in range(H_A):
        cols = slice(hd * DV_A, (hd + 1) * DV_A)
        q = q_all[:, cols]
        k = k_all[:, cols]
        if rope_refs is not None:
            q = _rope(q, cos4, sin4, first_half)
            k = _rope(k, cos4, sin4, first_half)
        q_ref[:, cols] = (q * SCORE_SCALE).astype(BF16)
        k_ref[:, cols] = k.astype(BF16)
    v = jnp.dot(h, w_ref[:, 2 * D_MODEL:3 * D_MODEL], preferred_element_type=F32)
    if cache_refs is not None:
        _store_heads(cache_refs[1], v, seq_len)
    vt_ref[...] = v.T.astype(BF16)
    z_ref[...] = jnp.dot(h, w_ref[:, 3 * D_MODEL:4 * D_MODEL],
                         preferred_element_type=F32).astype(z_ref.dtype)


def _attn_in_kernel(x_ref, mod_ref, w_ref, cos_ref, sin_ref, q_ref, k_ref, vt_ref, z_ref):
    _attn_in_body(x_ref, mod_ref, w_ref, (cos_ref, sin_ref), q_ref, k_ref, vt_ref, z_ref, None, None)


def _attn_in(x, m_rows, mod3, w_in, j, layer, rows_per_batch, rope_tables):
    x2d, x_first_row = x
    tm = ROW_TILE
    seq_tiles = rows_per_batch // tm
    row_spec = pl.BlockSpec((tm, D_MODEL), lambda m: (m, 0))
    tab_spec = pl.BlockSpec((tm, DV_A), lambda m: (m % seq_tiles, 0))
    return pl.pallas_call(
        _attn_in_kernel,
        out_shape=[jax.ShapeDtypeStruct((m_rows, D_MODEL), BF16),
                   jax.ShapeDtypeStruct((m_rows, D_MODEL), BF16),
                   jax.ShapeDtypeStruct((D_MODEL, m_rows), BF16),
                   jax.ShapeDtypeStruct((m_rows, D_MODEL), BF16)],
        grid=(m_rows // tm,),
        in_specs=[pl.BlockSpec((tm, D_MODEL), lambda m: (x_first_row // tm + m, 0)),
                  pl.BlockSpec((None, 1, 3 * D_MODEL), _mod_row_map(layer, rows_per_batch, tm)),
                  _weight_spec(w_in, j), tab_spec, tab_spec],
        out_specs=[row_spec, row_spec, pl.BlockSpec((D_MODEL, tm), lambda m: (0, m)), row_spec],
        compiler_params=_params(1),
        name="attn_in_rope",
    )(x2d, mod3, w_in, *rope_tables)


ONES_ROWS = 16
CTX_ATTN_SEQS = 2
ATTN_ROWS = 512
ATTN_SUBBLOCKS = 1


def _slab_reduce(op, x):
    parts = [x[i:i + 8] for i in range(0, x.shape[0], 8)]
    while len(parts) > 1:
        parts = [op(parts[i], parts[i + 1]) for i in range(0, len(parts) - 1, 2)] + (
            [parts[-1]] if len(parts) % 2 else [])
    return parts[0]


def _attn_kernel(*refs, layer_idx, has_ctx, n_seq, shared_keys, before_unit=None):
    if has_ctx:
        (q_ref, k_ref, vt_ref, z_ref, x_ref, kc_ref, vc_ref, mod_ref, lam_ref, subln_ref,
         w_ref, g_ref, b_ref, o_ref, y_ref, kcb_ref, vct_ref) = refs

        @pl.when(pl.program_id(1) == 0)
        def _():
            n_ctx = kc_ref.shape[0]
            kcb_ref[...] = kc_ref[...].reshape(n_ctx, D_MODEL).astype(BF16)
            vct_ref[...] = vc_ref[...].reshape(n_ctx, D_MODEL).T.astype(BF16)
    else:
        (q_ref, k_ref, vt_ref, z_ref, x_ref, mod_ref, lam_ref, subln_ref,
         w_ref, g_ref, b_ref, o_ref, y_ref) = refs
    tq = q_ref.shape[0] // n_seq
    lam_init = 0.8 - 0.6 * math.exp(-0.3 * layer_idx)
    lm = lam_ref[...]
    lam = (jnp.exp(jnp.sum(lm[0:1] * lm[1:2], axis=-1, keepdims=True))
           - jnp.exp(jnp.sum(lm[2:3] * lm[3:4], axis=-1, keepdims=True)) + lam_init)
    lane = lax.broadcasted_iota(jnp.int32, (1, DV_A), 1)
    first = lane < DH_A
    subln = jnp.broadcast_to(subln_ref[...], (DV_A, tq))
    nt = (((1,), (1,)), ((), ()))
    t = k_ref.shape[0] if shared_keys else k_ref.shape[0] // n_seq

    def key_rows(sq):
        return slice(0, t) if shared_keys else slice(sq * t, (sq + 1) * t)

    def scores(unit):
        sq, hd = unit
        cols = slice(hd * DV_A, (hd + 1) * DV_A)
        qh = q_ref[sq * tq:(sq + 1) * tq, cols]
        zero = jnp.zeros_like(qh)
        qq = jnp.concatenate([jnp.where(first, qh, zero), jnp.where(first, zero, qh)], axis=0)
        parts = [lax.dot_general(k_ref[key_rows(sq), cols], qq, nt, preferred_element_type=F32)]
        if has_ctx:
            parts.append(lax.dot_general(kcb_ref[:, cols], qq, nt, preferred_element_type=F32))
        return parts

    def exps(parts):
        m8 = functools.reduce(jnp.maximum, [_slab_reduce(jnp.maximum, s) for s in parts])
        m = jnp.max(m8, axis=0, keepdims=True)
        return [jnp.exp2(s - m).astype(BF16) for s in parts]

    def with_ones(vals_t):
        return jnp.concatenate([vals_t, jnp.ones((ONES_ROWS, vals_t.shape[1]), BF16)], axis=0)

    def finish(unit, es):
        sq, hd = unit
        cols = slice(hd * DV_A, (hd + 1) * DV_A)
        rows = slice(sq * tq, (sq + 1) * tq)
        acc = jnp.dot(with_ones(vt_ref[cols, key_rows(sq)]), es[0], preferred_element_type=F32)
        if has_ctx:
            acc = acc + jnp.dot(with_ones(vct_ref[cols, :]), es[1], preferred_element_type=F32)
        inv = 1.0 / acc[DV_A:DV_A + 1, :]
        ot = acc[:DV_A, :tq] * inv[:, :tq] - acc[:DV_A, tq:] * (inv[:, tq:] * lam)
        ot = ot * lax.rsqrt(jnp.mean(ot * ot, axis=0, keepdims=True) + RMS_EPS)
        ot = ot * subln * (1.0 - lam_init)
        y_ref[rows, cols] = (ot.T * _silu(z_ref[rows, cols].astype(F32))).astype(BF16)

    units = [(sq, hd) for hd in range(H_A) for sq in range(n_seq)]
    if before_unit is not None:
        before_unit(-1)
    s_ahead = {u: scores(units[u]) for u in range(min(2, len(units)))}
    e_ahead = {0: exps(s_ahead.pop(0))}
    for u in range(len(units)):
        if before_unit is not None:
            before_unit(u)
        if u + 2 < len(units):
            s_ahead[u + 2] = scores(units[u + 2])
        if u + 1 < len(units):
            e_ahead[u + 1] = exps(s_ahead.pop(u + 1))
        finish(units[u], e_ahead.pop(u))
    out = jnp.dot(y_ref[...], w_ref[...].astype(BF16), preferred_element_type=F32)
    gate = mod_ref[:, 2 * D_MODEL:3 * D_MODEL]
    o_ref[...] = _residual_layer_norm(x_ref[...], out, gate, g_ref[...], b_ref[...])


def _attention(qkvz, first_row, m_rows, x, seq_len, ctx, mod3, lam, subln, w_out, j, ln_g, ln_b,
               layer, per_batch_rows):
    q, k, vt, z = qkvz
    t = seq_len
    b = m_rows // t
    has_ctx = ctx is not None
    if t <= 256 and not per_batch_rows and not has_ctx and b % 2 == 0:
        rows, n_seq, shared_keys, n_kseq = 2 * t, 2, False, 2
    else:
        rows, n_seq, shared_keys, n_kseq = ATTN_ROWS, ATTN_SUBBLOCKS, True, 1
    nq = n_kseq * t // rows
    x, x_first_row = x
    r0, k0 = first_row // rows, first_row // (n_kseq * t)
    x_spec = pl.BlockSpec((rows, D_MODEL), lambda i, j: (x_first_row // rows + i * nq + j, 0))
    q_spec = pl.BlockSpec((rows, D_MODEL), lambda i, j: (r0 + i * nq + j, 0))
    o_spec = pl.BlockSpec((rows, D_MODEL), lambda i, j: (i * nq + j, 0))
    k_spec = pl.BlockSpec((n_kseq * t, D_MODEL), lambda i, j: (k0 + i, 0))
    vt_spec = pl.BlockSpec((D_MODEL, n_kseq * t), lambda i, j: (0, k0 + i))
    if per_batch_rows:
        mod_map = lambda i, j: (layer * MOD_ROWS + i, 0, 0)
    else:
        mod_map = lambda i, j: (layer * MOD_ROWS + CTX_ROW, 0, 0)
    const2 = lambda i, j: (0, 0)
    in_specs = [q_spec, k_spec, vt_spec, q_spec, x_spec]
    args = [q, k, vt, z, x]
    scratch = [pltpu.VMEM((rows, D_MODEL), BF16)]
    if has_ctx:
        cache_k, cache_v, jj = ctx
        n_ctx = cache_k.shape[2]
        scratch += [pltpu.VMEM((n_ctx, D_MODEL), BF16), pltpu.VMEM((D_MODEL, n_ctx), BF16)]
        c_spec = pl.BlockSpec((None, None, n_ctx, H_A, DV_A),
                              lambda i, j: (i, jj, 0, 0, 0))
        in_specs += [c_spec, c_spec]
        args += [cache_k, cache_v]
    in_specs += [pl.BlockSpec((None, 1, 3 * D_MODEL), mod_map),
                 pl.BlockSpec((4, DH_A), const2),
                 pl.BlockSpec((DV_A, 1), const2),
                 _weight_spec(w_out, j),
                 pl.BlockSpec((1, D_MODEL), const2),
                 pl.BlockSpec((1, D_MODEL), const2)]
    args += [mod3, lam, subln.reshape(DV_A, 1), w_out, ln_g.reshape(1, D_MODEL),
             ln_b.reshape(1, D_MODEL)]
    return pl.pallas_call(
        functools.partial(_attn_kernel, layer_idx=layer, has_ctx=has_ctx, n_seq=n_seq,
                          shared_keys=shared_keys),
        out_shape=jax.ShapeDtypeStruct((m_rows, D_MODEL), F32),
        grid=(b // n_kseq, nq),
        in_specs=in_specs,
        out_specs=o_spec,
        scratch_shapes=scratch,
        compiler_params=_params(2),
        name="diff_attn_ctx" if has_ctx else "diff_attn",
    )(*args)


def _ctx_attn_layer_kernel(*refs, layer_idx, seq_len, n_aliased):
    x_ref, mod_ref, w_in_ref, lam_ref, subln_ref, w_out_ref, g_ref, b_ref = refs[:8]
    o_ref, ck_ref, cv_ref, q_ref, k_ref, vt_ref, z_ref, y_ref = refs[8 + n_aliased:]
    _attn_in_body(x_ref, mod_ref, w_in_ref, None, q_ref, k_ref, vt_ref, z_ref, (ck_ref, cv_ref),
                  seq_len)
    _attn_kernel(q_ref, k_ref, vt_ref, z_ref, x_ref, mod_ref, lam_ref, subln_ref, w_out_ref, g_ref,
                 b_ref, o_ref, y_ref, layer_idx=layer_idx, has_ctx=False,
                 n_seq=x_ref.shape[0] // seq_len, shared_keys=False)


def _ctx_attention_layer(x, m_rows, seq_len, mod3, w_in, w_out, j, lam, subln, ln_g, ln_b, layer,
                         cache_out):
    x2d, x_first_row = x
    rows = CTX_ATTN_SEQS * seq_len
    const2 = lambda m: (0, 0)
    in_specs = [pl.BlockSpec((rows, D_MODEL), lambda m: (x_first_row // rows + m, 0)),
                pl.BlockSpec((None, 1, 3 * D_MODEL), lambda m: (layer * MOD_ROWS + CTX_ROW, 0, 0)),
                _weight_spec(w_in, j),
                pl.BlockSpec((4, DH_A), const2),
                pl.BlockSpec((DV_A, 1), const2),
                _weight_spec(w_out, j),
                pl.BlockSpec((1, D_MODEL), const2),
                pl.BlockSpec((1, D_MODEL), const2)]
    args = [x2d, mod3, w_in, lam, subln.reshape(DV_A, 1), w_out, ln_g.reshape(1, D_MODEL),
            ln_b.reshape(1, D_MODEL)]
    new_k, new_v = cache_out
    cache_shape = jax.ShapeDtypeStruct(new_k.shape, F32)
    aliases = {}
    if j == 0:
        cache_spec = pl.BlockSpec((CTX_ATTN_SEQS, new_k.shape[1], seq_len, H_A, DV_A),
                                  lambda m: (m, 0, 0, 0, 0))
    else:
        cache_spec = pl.BlockSpec((CTX_ATTN_SEQS, None, seq_len, H_A, DV_A),
                                  lambda m: (m, j, 0, 0, 0))
        in_specs += [pl.BlockSpec(memory_space=pl.ANY)] * 2
        args += [new_k, new_v]
        aliases = {len(args) - 2: 1, len(args) - 1: 2}
    return pl.pallas_call(
        functools.partial(_ctx_attn_layer_kernel, layer_idx=layer, seq_len=seq_len,
                          n_aliased=len(aliases)),
        out_shape=[jax.ShapeDtypeStruct((m_rows, D_MODEL), F32), cache_shape, cache_shape],
        grid=(m_rows // rows,),
        in_specs=in_specs,
        out_specs=[pl.BlockSpec((rows, D_MODEL), lambda m: (m, 0)), cache_spec, cache_spec],
        scratch_shapes=[pltpu.VMEM((rows, D_MODEL), BF16), pltpu.VMEM((rows, D_MODEL), BF16),
                        pltpu.VMEM((D_MODEL, rows), BF16), pltpu.VMEM((rows, D_MODEL), BF16),
                        pltpu.VMEM((rows, D_MODEL), BF16)],
        input_output_aliases=aliases,
        compiler_params=_params(1),
        name="ctx_attn_layer",
    )(*args)


def _rope_tables(n_tokens):
    rows = n_tokens // GRID_W
    r = jnp.repeat(jnp.arange(rows, dtype=F32), GRID_W)
    col = jnp.tile(jnp.arange(GRID_W, dtype=F32), rows)
    n_freq = DH_A // 4
    inv = ROPE_BASE ** (-jnp.arange(n_freq, dtype=F32) / n_freq)
    ang = jnp.concatenate([r[:, None] * inv, col[:, None] * inv], -1)
    cos, sin = jnp.cos(ang), jnp.sin(ang)
    return jnp.tile(cos, (1, 4)), jnp.concatenate([-sin, sin, -sin, sin], -1)


def _ret_in_kernel(xc_ref, xl_ref, mod_ref, w_hbm_ref, q_ref, kt_ref, v_ref, g_ref, w_ref, sem_ref, *,
                   n_ctx_tiles, j):
    nq = H_B * DK_B
    cc = WEIGHT_CHUNK_COLS
    assert nq == cc and E_B % cc == 0
    staged = _StagedWeight(w_hbm_ref, w_ref, sem_ref, j, cc)

    def body(arrive):
        h = _modulated(_both_streams_rows(xc_ref, xl_ref, n_ctx_tiles), mod_ref)
        arrive(0)
        q_ref[...] = jnp.dot(h, w_ref[:, 0:nq], preferred_element_type=F32).astype(BF16)
        arrive(1)
        k = jnp.dot(h, w_ref[:, nq:2 * nq], preferred_element_type=F32) * (DK_B ** -0.5)
        kt_ref[...] = k.T
        for c in range(E_B // cc):
            arrive(2 + c)
            v = jnp.dot(h, w_ref[:, 2 * nq + c * cc:2 * nq + (c + 1) * cc],
                        preferred_element_type=F32)
            v_ref[:, c * cc:(c + 1) * cc] = v.astype(BF16)
        for c in range(E_B // cc):
            arrive(2 + E_B // cc + c)
            g_ref[:, c * cc:(c + 1) * cc] = jnp.dot(
                h, w_ref[:, 2 * nq + E_B + c * cc:2 * nq + E_B + (c + 1) * cc],
                preferred_element_type=F32).astype(BF16)

    staged.run(body)


def _ret_in(x_ctx, x_lat, mod3, w_in, j, layer, lat_rows_per_batch):
    m_rows = x_ctx.shape[0] + x_lat.shape[0]
    tm = ROW_TILE
    nq = H_B * DK_B
    n_ctx_tiles = x_ctx.shape[0] // tm
    return pl.pallas_call(
        functools.partial(_ret_in_kernel, n_ctx_tiles=n_ctx_tiles, j=j),
        out_shape=[jax.ShapeDtypeStruct((m_rows, nq), BF16),
                   jax.ShapeDtypeStruct((nq, m_rows), F32),
                   jax.ShapeDtypeStruct((m_rows, E_B), BF16),
                   jax.ShapeDtypeStruct((m_rows, E_B), BF16)],
        grid=(m_rows // tm,),
        in_specs=_both_streams_specs(x_ctx, x_lat, tm) + [
            pl.BlockSpec((None, 1, 3 * D_MODEL),
                         _both_streams_mod_map(layer, n_ctx_tiles, tm, lat_rows_per_batch)),
            pl.BlockSpec(memory_space=pl.ANY)],
        out_specs=[pl.BlockSpec((tm, nq), lambda m: (m, 0)),
                   pl.BlockSpec((nq, tm), lambda m: (0, m)),
                   pl.BlockSpec((tm, E_B), lambda m: (m, 0)),
                   pl.BlockSpec((tm, E_B), lambda m: (m, 0))],
        scratch_shapes=_StagedWeight.scratch(w_in, WEIGHT_CHUNK_COLS),
        compiler_params=_params(1),
        name="ret_in",
    )(x_ctx, x_lat, mod3, w_in)


def _retention_kernel(*refs, has_state, heads, seqs, fused_out):
    refs = list(refs)
    q_ref, kt_ref, v_ref, g_ref, af_ref, ab_ref = refs[:6]
    del refs[:6]
    if has_state:
        s0f_ref, s0b_ref = refs[:2]
        del refs[:2]
    if fused_out:
        x_ref, mod_ref, w_ref, lng_ref, lnb_ref = refs[:5]
        del refs[:5]
        o_ref = refs.pop(0)
        y_ref = refs.pop()
    else:
        y_ref = refs.pop(0)
    if not has_state:
        sf_ref, sb_ref = refs
    t = q_ref.shape[0] // seqs
    nc = t // CHUNK
    row = lax.broadcasted_iota(jnp.int32, (CHUNK, CHUNK), 0).astype(F32)
    col = lax.broadcasted_iota(jnp.int32, (CHUNK, CHUNK), 1).astype(F32)
    diff = row - col
    idx_col = lax.broadcasted_iota(jnp.int32, (CHUNK, 1), 0).astype(F32)
    idx_row = lax.broadcasted_iota(jnp.int32, (1, CHUNK), 1).astype(F32)

    def state(ref, sq, hh):
        return ref.at[hh] if seqs == 1 else ref.at[sq, hh]

    for hh, sq in [(hh, sq) for hh in range(heads) for sq in range(seqs)]:
        def chunk(c, sq=sq):
            return slice(sq * t + c * CHUNK, sq * t + (c + 1) * CHUNK)

        qk_cols = slice(hh * DK_B, (hh + 1) * DK_B)
        v_cols = slice(hh * DV_B, (hh + 1) * DV_B)
        lg_f = jnp.log1p(-jnp.exp(af_ref[hh]))
        lg_b = jnp.log1p(-jnp.exp(ab_ref[hh]))
        dmask = (jnp.where(diff >= 0, jnp.exp(jnp.maximum(diff, 0.0) * lg_f), 0.0)
                 + jnp.where(diff <= 0, jnp.exp(jnp.maximum(-diff, 0.0) * lg_b), 0.0))
        qd_f = jnp.exp((idx_col + 1.0) * lg_f)
        qd_b = jnp.exp((CHUNK - idx_col) * lg_b)
        kd_f = jnp.exp((CHUNK - 1.0 - idx_row) * lg_f)
        kd_b = jnp.exp(idx_row * lg_b)
        cd_f = jnp.exp(CHUNK * lg_f)
        cd_b = jnp.exp(CHUNK * lg_b)

        def states(order, kd, cd, s):
            seen = {}
            for n, c in enumerate(order):
                seen[c] = None if s is None else s.astype(BF16)
                if has_state and n == nc - 1:
                    return seen, None
                u = jnp.dot((kt_ref[qk_cols, chunk(c)] * kd).astype(BF16), v_ref[chunk(c), v_cols],
                            preferred_element_type=F32)
                s = u if s is None else s * cd + u
            return seen, s

        seen_f, s_f = states(range(nc), kd_f, cd_f,
                             state(s0f_ref, sq, hh)[...] if has_state else None)
        seen_b, s_b = states(range(nc - 1, -1, -1), kd_b, cd_b,
                             state(s0b_ref, sq, hh)[...] if has_state else None)
        if not has_state:
            state(sf_ref, sq, hh)[...] = s_f
            state(sb_ref, sq, hh)[...] = s_b
        for c in range(nc):
            qc = q_ref[chunk(c), qk_cols]
            qk = jnp.dot(qc, kt_ref[qk_cols, chunk(c)].astype(BF16), preferred_element_type=F32)
            o = jnp.dot((qk * dmask).astype(BF16), v_ref[chunk(c), v_cols],
                        preferred_element_type=F32)
            if seen_f[c] is not None:
                o = o + jnp.dot(qc, seen_f[c], preferred_element_type=F32) * qd_f
            if seen_b[c] is not None:
                o = o + jnp.dot(qc, seen_b[c], preferred_element_type=F32) * qd_b
            o = o * lax.rsqrt(jnp.mean(o * o, axis=-1, keepdims=True) + RMS_EPS)
            gate_pre = g_ref[chunk(c), v_cols].astype(F32)
            y_ref[chunk(c), v_cols] = (o * _silu(gate_pre)).astype(BF16)
    if fused_out:
        part = jnp.dot(y_ref[...], w_ref[...].astype(BF16), preferred_element_type=F32)
        gate = mod_ref[:, 2 * D_MODEL:3 * D_MODEL]
        head_steps = H_B // heads
        if head_steps == 1:
            o_ref[...] = _residual_layer_norm(x_ref[...], part, gate, lng_ref[...], lnb_ref[...])
        else:
            step = pl.program_id(1)

            @pl.when(step == 0)
            def _():
                o_ref[...] = part

            @pl.when(jnp.logical_and(step > 0, step < head_steps - 1))
            def _():
                o_ref[...] += part

            @pl.when(step == head_steps - 1)
            def _():
                o_ref[...] = _residual_layer_norm(x_ref[...], o_ref[...] + part, gate, lng_ref[...],
                                                  lnb_ref[...])


def _retention(q, kt, v, g, seq_len, n_seq, first_row, heads, seqs, decay_f, decay_b, states,
               out_proj=None):
    t = seqs * seq_len
    b = n_seq // seqs
    s0 = first_row // t
    has_state = states is not None
    fused_out = out_proj is not None
    seq_dim = None if seqs == 1 else seqs
    q_spec = pl.BlockSpec((t, heads * DK_B), lambda i, h: (s0 + i, h))
    kt_spec = pl.BlockSpec((heads * DK_B, t), lambda i, h: (h, s0 + i))
    vg_spec = pl.BlockSpec((t, heads * DV_B), lambda i, h: (s0 + i, h))
    a_spec = pl.BlockSpec((heads, 1, 1), lambda i, h: (h, 0, 0))
    in_specs = [q_spec, kt_spec, vg_spec, vg_spec, a_spec, a_spec]
    args = [q, kt, v, g, decay_f.reshape(H_B, 1, 1), decay_b.reshape(H_B, 1, 1)]
    out_shape = [jax.ShapeDtypeStruct((b * t, E_B), BF16)]
    out_specs = [pl.BlockSpec((t, heads * DV_B), lambda i, h: (i, h))]
    if has_state:
        s_f, s_b, jj = states
        s_spec = pl.BlockSpec((seq_dim, None, heads, DK_B, DV_B), lambda i, h: (i, jj, h, 0, 0))
        in_specs += [s_spec, s_spec]
        args += [s_f, s_b]
    scratch = []
    if fused_out:
        x2d, mod3, mod_map, w_out, jw, ln_g, ln_b = out_proj
        x_spec = pl.BlockSpec((t, D_MODEL), lambda i, h: (i, 0))
        const2 = lambda i, h: (0, 0)
        if heads == H_B:
            w_spec = _weight_spec(w_out, jw)
        else:
            w_spec = pl.BlockSpec((None, heads * DV_B, D_MODEL), lambda i, h: (jw, h, 0))
        in_specs += [x_spec, pl.BlockSpec((None, 1, 3 * D_MODEL), mod_map), w_spec,
                     pl.BlockSpec((1, D_MODEL), const2), pl.BlockSpec((1, D_MODEL), const2)]
        args += [x2d, mod3, w_out, ln_g.reshape(1, D_MODEL), ln_b.reshape(1, D_MODEL)]
        out_shape = [jax.ShapeDtypeStruct((b * t, D_MODEL), F32)]
        out_specs = [x_spec]
        scratch = [pltpu.VMEM((t, heads * DV_B), BF16)]
    if not has_state:
        so_spec = pl.BlockSpec((seq_dim, None, heads, DK_B, DV_B), lambda i, h: (i, 0, h, 0, 0))
        out_shape += [jax.ShapeDtypeStruct((n_seq, 1, H_B, DK_B, DV_B), F32)] * 2
        out_specs += [so_spec, so_spec]
    return pl.pallas_call(
        functools.partial(_retention_kernel, has_state=has_state, heads=heads, seqs=seqs,
                          fused_out=fused_out),
        out_shape=out_shape,
        grid=(b, H_B // heads),
        in_specs=in_specs,
        out_specs=out_specs,
        scratch_shapes=scratch,
        compiler_params=_params(2),
        name="retention_state" if has_state else "retention",
    )(*args)


CONV_ROWS = 1024
CONV_COLS = 256


def _conv_kernel(xc_ref, xl_ref, mod_ref, w_in_ref, cw_ref, w_out_ref, g_ref, b_ref, o_ref, *,
                 n_ctx_tiles, ctx_seq_len, lat_seq_len):
    x = _both_streams_rows(xc_ref, xl_ref, n_ctx_tiles)
    h = _modulated(x, mod_ref)
    rows = x.shape[0]
    is_ctx = pl.program_id(0) < n_ctx_tiles
    row = lax.broadcasted_iota(jnp.int32, (rows, 1), 0)
    pos = jnp.where(is_ctx, row % ctx_seq_len, row % lat_seq_len)
    has_prev = pos > 0
    has_next = pos < jnp.where(is_ctx, ctx_seq_len - 1, lat_seq_len - 1)
    e = D_MODEL
    for c in range(e // CONV_COLS):
        cols = slice(c * CONV_COLS, (c + 1) * CONV_COLS)

        def proj(part):
            lo = part * e + c * CONV_COLS
            return jnp.dot(h, w_in_ref[:, lo:lo + CONV_COLS], preferred_element_type=F32)

        p = proj(1) * proj(2)
        prev = jnp.where(has_prev, pltpu.roll(p, 1, 0), 0.0)
        nxt = jnp.where(has_next, pltpu.roll(p, rows - 1, 0), 0.0)
        conv = prev * cw_ref[0:1, cols] + p * cw_ref[1:2, cols] + nxt * cw_ref[2:3, cols]
        y = (proj(0) * conv * _silu(proj(3))).astype(BF16)
        part = jnp.dot(y, w_out_ref[cols, :].astype(BF16), preferred_element_type=F32)
        if c == 0:
            o_ref[...] = part
        else:
            o_ref[...] += part
    gate = mod_ref[:, 2 * D_MODEL:3 * D_MODEL]
    o_ref[...] = _residual_layer_norm(x, o_ref[...], gate, g_ref[...], b_ref[...])


def _conv_layer(x_ctx, x_lat, mod3, w_in, conv_w, w_out, j, ln_g, ln_b, layer, ctx_seq_len,
                lat_seq_len):
    m_rows = x_ctx.shape[0] + x_lat.shape[0]
    tm = CONV_ROWS
    n_ctx_tiles = x_ctx.shape[0] // tm
    const2 = lambda m: (0, 0)
    return pl.pallas_call(
        functools.partial(_conv_kernel, n_ctx_tiles=n_ctx_tiles, ctx_seq_len=ctx_seq_len,
                          lat_seq_len=lat_seq_len),
        out_shape=jax.ShapeDtypeStruct((m_rows, D_MODEL), F32),
        grid=(m_rows // tm,),
        in_specs=_both_streams_specs(x_ctx, x_lat, tm) + [
                  pl.BlockSpec((None, 1, 3 * D_MODEL),
                               _both_streams_mod_map(layer, n_ctx_tiles, tm, lat_seq_len)),
                  _weight_spec(w_in, j),
                  pl.BlockSpec((None, 3, D_MODEL), lambda m: (j, 0, 0)),
                  _weight_spec(w_out, j),
                  pl.BlockSpec((1, D_MODEL), const2),
                  pl.BlockSpec((1, D_MODEL), const2)],
        out_specs=pl.BlockSpec((tm, D_MODEL), lambda m: (m, 0)),
        compiler_params=_params(1),
        name="conv_layer",
    )(x_ctx, x_lat, mod3, w_in, conv_w, w_out, ln_g.reshape(1, D_MODEL), ln_b.reshape(1, D_MODEL))


def _separate(xp, xs, m_ctx):
    if xp[0] is xs[0]:
        return (xp[0][:m_ctx], 0), (xs[0][m_ctx:], 0)
    return xp, xs


def kernel(x_prompt, x_sample, cache_k, cache_v, state_fwd, state_bwd, c, c_ctx, w_mod, b_mod, ln_g,
           ln_b, w_in_a, lam_a, subln_a, w_out_a, w_in_b, decay_fwd, decay_bwd, w_out_b, w_in_c,
           conv_c, w_out_c):
    bp, tp, d = x_prompt.shape
    bs, ts, _ = x_sample.shape
    cvec = jnp.concatenate([c, c_ctx[None], jnp.zeros((MOD_ROWS - bs - 1, d), F32)], axis=0)
    mod3 = _modulation(cvec, w_mod, b_mod)
    rope_tables = _rope_tables(ts)

    m_ctx, m_lat = bp * tp, bs * ts
    xp = (x_prompt.reshape(m_ctx, d), 0)
    xs = (x_sample.reshape(m_lat, d), 0)
    n_attn = (DEPTH + N_MIXERS - 1) // N_MIXERS
    new_cache_k = new_cache_v = jax.ShapeDtypeStruct((bp, n_attn, tp, H_A, DV_A), F32)
    new_sf, new_sb = [], []
    for i in range(DEPTH):
        kind, j = i % N_MIXERS, i // N_MIXERS
        if kind == 0:
            x_ctx, new_cache_k, new_cache_v = _ctx_attention_layer(
                xp, m_ctx, tp, mod3, w_in_a, w_out_a, j, lam_a[j], subln_a[j], ln_g[i], ln_b[i], i,
                (new_cache_k, new_cache_v))
            qkvz = _attn_in(xs, m_lat, mod3, w_in_a, j, i, ts, rope_tables)
            xs = (_attention(qkvz, 0, m_lat, xs, ts, (cache_k, cache_v, j), mod3, lam_a[j],
                             subln_a[j], w_out_a, j, ln_g[i], ln_b[i], i, True), 0)
            xp = (x_ctx, 0)
        else:
            xp, xs = _separate(xp, xs, m_ctx)
            if kind == 1:
                q, kt, v, g = _ret_in(xp[0], xs[0], mod3, w_in_b, j, i, ts)
                x_ctx, s_f, s_b = _retention(
                    q, kt, v, g, tp, bp, 0, H_B, CTX_RET_SEQS, decay_fwd[j], decay_bwd[j], None,
                    out_proj=(xp[0], mod3, _mod_row_map(i, None, tp), w_out_b, j, ln_g[i],
                              ln_b[i]))
                new_sf.append(s_f)
                new_sb.append(s_b)
                (x_lat,) = _retention(
                    q, kt, v, g, ts, bs, m_ctx, LAT_RET_HEADS, 1, decay_fwd[j], decay_bwd[j],
                    (state_fwd, state_bwd, j),
                    out_proj=(xs[0], mod3, _mod_row_map(i, ts, ts), w_out_b, j, ln_g[i], ln_b[i]))
                xp, xs = (x_ctx, 0), (x_lat, 0)
            else:
                x_all = _conv_layer(xp[0], xs[0], mod3, w_in_c, conv_c, w_out_c, j, ln_g[i],
                                    ln_b[i], i, tp, ts)
                xp, xs = (x_all, 0), (x_all, m_ctx)
    xp, xs = _separate(xp, xs, m_ctx)
    y_prompt = xp[0].reshape(bp, tp, d)
    y_sample = xs[0].reshape(bs, ts, d)
    new_state_fwd = jnp.concatenate(new_sf, axis=1)
    new_state_bwd = jnp.concatenate(new_sb, axis=1)
    return (y_prompt, y_sample, new_cache_k, new_cache_v, new_state_fwd, new_state_bwd)
```

```python
import functools
import math

import jax
import jax.numpy as jnp
from jax import lax
from jax.experimental import pallas as pl
from jax.experimental.pallas import tpu as pltpu

F32 = jnp.float32
BF16 = jnp.bfloat16

D_MODEL = 1024
DEPTH = 4
N_MIXERS = 3
GRID_W = 64
H_A = 8
DH_A = 64
DV_A = 128
ROPE_HALF = DH_A // 2
SCORE_SCALE = DH_A ** -0.5 * math.log2(math.e)
H_B = 4
DK_B = 256
DV_B = 512
E_B = H_B * DV_B
CHUNK = 256
LAT_RET_HEADS = 2
CTX_RET_SEQS = 2
ALPHA = (2.0 * DEPTH) ** 0.25
ROPE_BASE = 10000.0
LN_EPS = 1e-5
RMS_EPS = 1e-6

MOD_ROWS = 8
CTX_ROW = 4
VMEM_LIMIT_BYTES = 58 * 1024 * 1024
ROW_TILE = 512


def _params(n_axes):
    return pltpu.CompilerParams(dimension_semantics=("arbitrary",) * n_axes,
                                vmem_limit_bytes=VMEM_LIMIT_BYTES)


def _silu(x):
    return x * jax.nn.sigmoid(x)


def _residual_layer_norm(x, out, gate, g, b):
    r = ALPHA * x + gate * out
    mu = jnp.mean(r, axis=-1, keepdims=True)
    d = r - mu
    var = jnp.mean(d * d, axis=-1, keepdims=True)
    return d * lax.rsqrt(var + LN_EPS) * g + b


def _modulated(x, mod_ref):
    shift = mod_ref[:, 0:D_MODEL]
    scale = mod_ref[:, D_MODEL:2 * D_MODEL]
    return x * (1.0 + scale) + shift


def _mod_row_map(layer, rows_per_batch, tile):
    if rows_per_batch is None:
        return lambda m, *_: (layer * MOD_ROWS + CTX_ROW, 0, 0)
    return lambda m, *_: (layer * MOD_ROWS + (m * tile) // rows_per_batch, 0, 0)


def _both_streams_specs(x_ctx, x_lat, tm):
    n_ctx_tiles = x_ctx.shape[0] // tm
    return [pl.BlockSpec((tm, D_MODEL), lambda m: (jnp.minimum(m, n_ctx_tiles - 1), 0)),
            pl.BlockSpec((tm, D_MODEL), lambda m: (jnp.maximum(m - n_ctx_tiles, 0), 0))]


def _both_streams_rows(x_ctx_ref, x_lat_ref, n_ctx_tiles):
    return jnp.where(pl.program_id(0) < n_ctx_tiles, x_ctx_ref[...], x_lat_ref[...])


def _both_streams_mod_map(layer, n_ctx_tiles, tm, lat_rows_per_batch):
    def index_map(m):
        lat_row = ((m - n_ctx_tiles) * tm) // lat_rows_per_batch
        return (layer * MOD_ROWS + jnp.where(m < n_ctx_tiles, CTX_ROW, lat_row), 0, 0)
    return index_map


def _weight_spec(w, j):
    return pl.BlockSpec((None,) + w.shape[1:], lambda *_: (j, 0, 0), pipeline_mode=pl.Buffered(1))


WEIGHT_CHUNK_COLS = 1024
WEIGHT_COPIES_IN_FLIGHT = 2


class _StagedWeight:
    @staticmethod
    def scratch(w, chunk_cols):
        return [pltpu.VMEM(w.shape[1:], w.dtype),
                pltpu.SemaphoreType.DMA((w.shape[2] // chunk_cols,))]

    def __init__(self, w_hbm_ref, w_vmem_ref, sem_ref, j, chunk_cols):
        self.copies = [
            pltpu.make_async_copy(w_hbm_ref.at[j, :, pl.ds(c * chunk_cols, chunk_cols)],
                                  w_vmem_ref.at[:, pl.ds(c * chunk_cols, chunk_cols)], sem_ref.at[c])
            for c in range(w_vmem_ref.shape[1] // chunk_cols)]

    def _arrive(self, c):
        self.copies[c].wait()
        if c + WEIGHT_COPIES_IN_FLIGHT < len(self.copies):
            self.copies[c + WEIGHT_COPIES_IN_FLIGHT].start()

    def run(self, body):
        first = pl.program_id(0) == 0

        @pl.when(first)
        def _():
            for cp in self.copies[:WEIGHT_COPIES_IN_FLIGHT]:
                cp.start()
            body(self._arrive)

        @pl.when(jnp.logical_not(first))
        def _():
            body(lambda c: None)


def _mod_kernel(cv_ref, w_ref, b_ref, o_ref):
    s = _silu(cv_ref[...])
    o_ref[...] = jnp.dot(s, w_ref[...], preferred_element_type=F32) + b_ref[...]


def _modulation(cvec, w_mod, b_mod):
    n = 3 * D_MODEL
    tn = n
    out = pl.pallas_call(
        _mod_kernel,
        out_shape=jax.ShapeDtypeStruct((DEPTH, MOD_ROWS, n), F32),
        grid=(DEPTH, n // tn),
        in_specs=[pl.BlockSpec((MOD_ROWS, D_MODEL), lambda i, j: (0, 0)),
                  pl.BlockSpec((None, D_MODEL, tn), lambda i, j: (i, 0, j)),
                  pl.BlockSpec((None, 1, tn), lambda i, j: (i, 0, j))],
        out_specs=pl.BlockSpec((None, MOD_ROWS, tn), lambda i, j: (i, 0, j)),
        compiler_params=_params(2),
        name="modulation",
    )(cvec, w_mod, b_mod.reshape(DEPTH, 1, n))
    return out.reshape(DEPTH * MOD_ROWS, 1, n)


def _rope(xh, cos4, sin4, first_half):
    swapped = jnp.where(first_half, pltpu.roll(xh, DV_A - ROPE_HALF, 1),
                        pltpu.roll(xh, ROPE_HALF, 1))
    return xh * cos4 + swapped * sin4


def _store_heads(o_ref, x, seq_len):
    for b in range(o_ref.shape[0]):
        xb = x[b * seq_len:(b + 1) * seq_len].reshape(seq_len, H_A, DV_A)
        if len(o_ref.shape) == 4:
            o_ref[b] = xb
        else:
            o_ref[b, 0] = xb
            for s in range(1, o_ref.shape[1]):
                o_ref[b, s] = jnp.zeros_like(xb)


def _attn_in_body(x_ref, mod_ref, w_ref, rope_refs, q_ref, k_ref, vt_ref, z_ref, cache_refs, seq_len):
    h = _modulated(x_ref[...], mod_ref)
    if rope_refs is not None:
        lane = lax.broadcasted_iota(jnp.int32, (1, DV_A), 1)
        first_half = (lane % DH_A) < ROPE_HALF
        cos4 = rope_refs[0][...]
        sin4 = rope_refs[1][...]
    q_all = jnp.dot(h, w_ref[:, 0:D_MODEL], preferred_element_type=F32)
    k_all = jnp.dot(h, w_ref[:, D_MODEL:2 * D_MODEL], preferred_element_type=F32)
    if cache_refs is not None:
        _store_heads(cache_refs[0], k_all, seq_len)
    for hd in range(H_A):
        cols = slice(hd * DV_A, (hd + 1) * DV_A)
        q = q_all[:, cols]
        k = k_all[:, cols]
        if rope_refs is not None:
            q = _rope(q, cos4, sin4, first_half)
            k = _rope(k, cos4, sin4, first_half)
        q_ref[:, cols] = (q * SCORE_SCALE).astype(BF16)
        k_ref[:, cols] = k.astype(BF16)
    v = jnp.dot(h, w_ref[:, 2 * D_MODEL:3 * D_MODEL], preferred_element_type=F32)
    if cache_refs is not None:
        _store_heads(cache_refs[1], v, seq_len)
    vt_ref[...] = v.T.astype(BF16)
    z_ref[...] = jnp.dot(h, w_ref[:, 3 * D_MODEL:4 * D_MODEL],
                         preferred_element_type=F32).astype(z_ref.dtype)


def _attn_in_kernel(x_ref, mod_ref, w_ref, cos_ref, sin_ref, q_ref, k_ref, vt_ref, z_ref):
    _attn_in_body(x_ref, mod_ref, w_ref, (cos_ref, sin_ref), q_ref, k_ref, vt_ref, z_ref, None, None)


def _attn_in(x, m_rows, mod3, w_in, j, layer, rows_per_batch, rope_tables):
    x2d, x_first_row = x
    tm = ROW_TILE
    seq_tiles = rows_per_batch // tm
    row_spec = pl.BlockSpec((tm, D_MODEL), lambda m: (m, 0))
    tab_spec = pl.BlockSpec((tm, DV_A), lambda m: (m % seq_tiles, 0))
    return pl.pallas_call(
        _attn_in_kernel,
        out_shape=[jax.ShapeDtypeStruct((m_rows, D_MODEL), BF16),
                   jax.ShapeDtypeStruct((m_rows, D_MODEL), BF16),
                   jax.ShapeDtypeStruct((D_MODEL, m_rows), BF16),
                   jax.ShapeDtypeStruct((m_rows, D_MODEL), BF16)],
        grid=(m_rows // tm,),
        in_specs=[pl.BlockSpec((tm, D_MODEL), lambda m: (x_first_row // tm + m, 0)),
                  pl.BlockSpec((None, 1, 3 * D_MODEL), _mod_row_map(layer, rows_per_batch, tm)),
                  _weight_spec(w_in, j), tab_spec, tab_spec],
        out_specs=[row_spec, row_spec, pl.BlockSpec((D_MODEL, tm), lambda m: (0, m)), row_spec],
        compiler_params=_params(1),
        name="attn_in_rope",
    )(x2d, mod3, w_in, *rope_tables)


ONES_ROWS = 16
CTX_ATTN_SEQS = 2
ATTN_ROWS = 512


def _slab_reduce(op, x):
    parts = [x[i:i + 8] for i in range(0, x.shape[0], 8)]
    while len(parts) > 1:
        parts = [op(parts[i], parts[i + 1]) for i in range(0, len(parts) - 1, 2)] + (
            [parts[-1]] if len(parts) % 2 else [])
    return parts[0]


def _attn_kernel(*refs, layer_idx, has_ctx, n_seq, shared_keys):
    if has_ctx:
        (q_ref, k_ref, vt_ref, z_ref, x_ref, kc_ref, vc_ref, mod_ref, lam_ref, subln_ref,
         w_ref, g_ref, b_ref, o_ref, y_ref, kcb_ref, vct_ref) = refs

        @pl.when(pl.program_id(1) == 0)
        def _():
            n_ctx = kc_ref.shape[0]
            kcb_ref[...] = kc_ref[...].reshape(n_ctx, D_MODEL).astype(BF16)
            vct_ref[...] = vc_ref[...].reshape(n_ctx, D_MODEL).T.astype(BF16)
    else:
        (q_ref, k_ref, vt_ref, z_ref, x_ref, mod_ref, lam_ref, subln_ref,
         w_ref, g_ref, b_ref, o_ref, y_ref) = refs
    tq = q_ref.shape[0] // n_seq
    lam_init = 0.8 - 0.6 * math.exp(-0.3 * layer_idx)
    lm = lam_ref[...]
    lam = (jnp.exp(jnp.sum(lm[0:1] * lm[1:2], axis=-1, keepdims=True))
           - jnp.exp(jnp.sum(lm[2:3] * lm[3:4], axis=-1, keepdims=True)) + lam_init)
    lane = lax.broadcasted_iota(jnp.int32, (1, DV_A), 1)
    first = lane < DH_A
    subln = jnp.broadcast_to(subln_ref[...], (DV_A, tq))
    nt = (((1,), (1,)), ((), ()))
    t = k_ref.shape[0] if shared_keys else k_ref.shape[0] // n_seq

    def key_rows(sq):
        return slice(0, t) if shared_keys else slice(sq * t, (sq + 1) * t)

    def scores(unit):
        sq, hd = unit
        cols = slice(hd * DV_A, (hd + 1) * DV_A)
        qh = q_ref[sq * tq:(sq + 1) * tq, cols]
        zero = jnp.zeros_like(qh)
        qq = jnp.concatenate([jnp.where(first, qh, zero), jnp.where(first, zero, qh)], axis=0)
        parts = [lax.dot_general(k_ref[key_rows(sq), cols], qq, nt, preferred_element_type=F32)]
        if has_ctx:
            parts.append(lax.dot_general(kcb_ref[:, cols], qq, nt, preferred_element_type=F32))
        return parts

    def exps(parts):
        m8 = functools.reduce(jnp.maximum, [_slab_reduce(jnp.maximum, s) for s in parts])
        m = jnp.max(m8, axis=0, keepdims=True)
        return [jnp.exp2(s - m).astype(BF16) for s in parts]

    def with_ones(vals_t):
        return jnp.concatenate([vals_t, jnp.ones((ONES_ROWS, vals_t.shape[1]), BF16)], axis=0)

    def finish(unit, es):
        sq, hd = unit
        cols = slice(hd * DV_A, (hd + 1) * DV_A)
        rows = slice(sq * tq, (sq + 1) * tq)
        acc = jnp.dot(with_ones(vt_ref[cols, key_rows(sq)]), es[0], preferred_element_type=F32)
        if has_ctx:
            acc = acc + jnp.dot(with_ones(vct_ref[cols, :]), es[1], preferred_element_type=F32)
        inv = 1.0 / acc[DV_A:DV_A + 1, :]
        ot = acc[:DV_A, :tq] * inv[:, :tq] - acc[:DV_A, tq:] * (inv[:, tq:] * lam)
        ot = ot * lax.rsqrt(jnp.mean(ot * ot, axis=0, keepdims=True) + RMS_EPS)
        ot = ot * subln * (1.0 - lam_init)
        y_ref[rows, cols] = (ot.T * _silu(z_ref[rows, cols].astype(F32))).astype(BF16)

    units = [(sq, hd) for hd in range(H_A) for sq in range(n_seq)]
    s_ahead = {u: scores(units[u]) for u in range(min(2, len(units)))}
    e_ahead = {0: exps(s_ahead.pop(0))}
    for u in range(len(units)):
        if u + 2 < len(units):
            s_ahead[u + 2] = scores(units[u + 2])
        if u + 1 < len(units):
            e_ahead[u + 1] = exps(s_ahead.pop(u + 1))
        finish(units[u], e_ahead.pop(u))
    out = jnp.dot(y_ref[...], w_ref[...].astype(BF16), preferred_element_type=F32)
    gate = mod_ref[:, 2 * D_MODEL:3 * D_MODEL]
    o_ref[...] = _residual_layer_norm(x_ref[...], out, gate, g_ref[...], b_ref[...])


def _attention(qkvz, first_row, m_rows, x, seq_len, ctx, mod3, lam, subln, w_out, j, ln_g, ln_b,
               layer, per_batch_rows):
    q, k, vt, z = qkvz
    t = seq_len
    b = m_rows // t
    has_ctx = ctx is not None
    if t <= 256 and not per_batch_rows and not has_ctx and b % 2 == 0:
        rows, n_seq, shared_keys, n_kseq = 2 * t, 2, False, 2
    else:
        rows, n_seq, shared_keys, n_kseq = ATTN_ROWS, 1, True, 1
    nq = n_kseq * t // rows
    x, x_first_row = x
    r0, k0 = first_row // rows, first_row // (n_kseq * t)
    x_spec = pl.BlockSpec((rows, D_MODEL), lambda i, j: (x_first_row // rows + i * nq + j, 0))
    q_spec = pl.BlockSpec((rows, D_MODEL), lambda i, j: (r0 + i * nq + j, 0))
    o_spec = pl.BlockSpec((rows, D_MODEL), lambda i, j: (i * nq + j, 0))
    k_spec = pl.BlockSpec((n_kseq * t, D_MODEL), lambda i, j: (k0 + i, 0))
    vt_spec = pl.BlockSpec((D_MODEL, n_kseq * t), lambda i, j: (0, k0 + i))
    if per_batch_rows:
        mod_map = lambda i, j: (layer * MOD_ROWS + i, 0, 0)
    else:
        mod_map = lambda i, j: (layer * MOD_ROWS + CTX_ROW, 0, 0)
    const2 = lambda i, j: (0, 0)
    in_specs = [q_spec, k_spec, vt_spec, q_spec, x_spec]
    args = [q, k, vt, z, x]
    scratch = [pltpu.VMEM((rows, D_MODEL), BF16)]
    if has_ctx:
        cache_k, cache_v, jj = ctx
        n_ctx = cache_k.shape[2]
        scratch += [pltpu.VMEM((n_ctx, D_MODEL), BF16), pltpu.VMEM((D_MODEL, n_ctx), BF16)]
        c_spec = pl.BlockSpec((None, None, n_ctx, H_A, DV_A),
                              lambda i, j: (i, jj, 0, 0, 0))
        in_specs += [c_spec, c_spec]
        args += [cache_k, cache_v]
    in_specs += [pl.BlockSpec((None, 1, 3 * D_MODEL), mod_map),
                 pl.BlockSpec((4, DH_A), const2),
                 pl.BlockSpec((DV_A, 1), const2),
                 _weight_spec(w_out, j),
                 pl.BlockSpec((1, D_MODEL), const2),
                 pl.BlockSpec((1, D_MODEL), const2)]
    args += [mod3, lam, subln.reshape(DV_A, 1), w_out, ln_g.reshape(1, D_MODEL),
             ln_b.reshape(1, D_MODEL)]
    return pl.pallas_call(
        functools.partial(_attn_kernel, layer_idx=layer, has_ctx=has_ctx, n_seq=n_seq,
                          shared_keys=shared_keys),
        out_shape=jax.ShapeDtypeStruct((m_rows, D_MODEL), F32),
        grid=(b // n_kseq, nq),
        in_specs=in_specs,
        out_specs=o_spec,
        scratch_shapes=scratch,
        compiler_params=_params(2),
        name="diff_attn_ctx" if has_ctx else "diff_attn",
    )(*args)


def _ctx_attn_layer_kernel(*refs, layer_idx, seq_len, n_aliased):
    x_ref, mod_ref, w_in_ref, lam_ref, subln_ref, w_out_ref, g_ref, b_ref = refs[:8]
    o_ref, ck_ref, cv_ref, q_ref, k_ref, vt_ref, z_ref, y_ref = refs[8 + n_aliased:]
    _attn_in_body(x_ref, mod_ref, w_in_ref, None, q_ref, k_ref, vt_ref, z_ref, (ck_ref, cv_ref),
                  seq_len)
    _attn_kernel(q_ref, k_ref, vt_ref, z_ref, x_ref, mod_ref, lam_ref, subln_ref, w_out_ref, g_ref,
                 b_ref, o_ref, y_ref, layer_idx=layer_idx, has_ctx=False,
                 n_seq=x_ref.shape[0] // seq_len, shared_keys=False)


def _ctx_attention_layer(x, m_rows, seq_len, mod3, w_in, w_out, j, lam, subln, ln_g, ln_b, layer,
                         cache_out):
    x2d, x_first_row = x
    rows = CTX_ATTN_SEQS * seq_len
    const2 = lambda m: (0, 0)
    in_specs = [pl.BlockSpec((rows, D_MODEL), lambda m: (x_first_row // rows + m, 0)),
                pl.BlockSpec((None, 1, 3 * D_MODEL), lambda m: (layer * MOD_ROWS + CTX_ROW, 0, 0)),
                _weight_spec(w_in, j),
                pl.BlockSpec((4, DH_A), const2),
                pl.BlockSpec((DV_A, 1), const2),
                _weight_spec(w_out, j),
                pl.BlockSpec((1, D_MODEL), const2),
                pl.BlockSpec((1, D_MODEL), const2)]
    args = [x2d, mod3, w_in, lam, subln.reshape(DV_A, 1), w_out, ln_g.reshape(1, D_MODEL),
            ln_b.reshape(1, D_MODEL)]
    new_k, new_v = cache_out
    cache_shape = jax.ShapeDtypeStruct(new_k.shape, F32)
    aliases = {}
    if j == 0:
        cache_spec = pl.BlockSpec((CTX_ATTN_SEQS, new_k.shape[1], seq_len, H_A, DV_A),
                                  lambda m: (m, 0, 0, 0, 0))
    else:
        cache_spec = pl.BlockSpec((CTX_ATTN_SEQS, None, seq_len, H_A, DV_A),
                                  lambda m: (m, j, 0, 0, 0))
        in_specs += [pl.BlockSpec(memory_space=pl.ANY)] * 2
        args += [new_k, new_v]
        aliases = {len(args) - 2: 1, len(args) - 1: 2}
    return pl.pallas_call(
        functools.partial(_ctx_attn_layer_kernel, layer_idx=layer, seq_len=seq_len,
                          n_aliased=len(aliases)),
        out_shape=[jax.ShapeDtypeStruct((m_rows, D_MODEL), F32), cache_shape, cache_shape],
        grid=(m_rows // rows,),
        in_specs=in_specs,
        out_specs=[pl.BlockSpec((rows, D_MODEL), lambda m: (m, 0)), cache_spec, cache_spec],
        scratch_shapes=[pltpu.VMEM((rows, D_MODEL), BF16), pltpu.VMEM((rows, D_MODEL), BF16),
                        pltpu.VMEM((D_MODEL, rows), BF16), pltpu.VMEM((rows, D_MODEL), BF16),
                        pltpu.VMEM((rows, D_MODEL), BF16)],
        input_output_aliases=aliases,
        compiler_params=_params(1),
        name="ctx_attn_layer",
    )(*args)


def _rope_tables(n_tokens):
    rows = n_tokens // GRID_W
    r = jnp.repeat(jnp.arange(rows, dtype=F32), GRID_W)
    col = jnp.tile(jnp.arange(GRID_W, dtype=F32), rows)
    n_freq = DH_A // 4
    inv = ROPE_BASE ** (-jnp.arange(n_freq, dtype=F32) / n_freq)
    ang = jnp.concatenate([r[:, None] * inv, col[:, None] * inv], -1)
    cos, sin = jnp.cos(ang), jnp.sin(ang)
    return jnp.tile(cos, (1, 4)), jnp.concatenate([-sin, sin, -sin, sin], -1)


def _ret_in_kernel(xc_ref, xl_ref, mod_ref, w_hbm_ref, q_ref, kt_ref, v_ref, g_ref, w_ref, sem_ref, *,
                   n_ctx_tiles, j):
    nq = H_B * DK_B
    cc = WEIGHT_CHUNK_COLS
    assert nq == cc and E_B % cc == 0
    staged = _StagedWeight(w_hbm_ref, w_ref, sem_ref, j, cc)

    def body(arrive):
        h = _modulated(_both_streams_rows(xc_ref, xl_ref, n_ctx_tiles), mod_ref)
        arrive(0)
        q_ref[...] = jnp.dot(h, w_ref[:, 0:nq], preferred_element_type=F32).astype(BF16)
        arrive(1)
        k = jnp.dot(h, w_ref[:, nq:2 * nq], preferred_element_type=F32) * (DK_B ** -0.5)
        kt_ref[...] = k.T
        for c in range(E_B // cc):
            arrive(2 + c)
            v = jnp.dot(h, w_ref[:, 2 * nq + c * cc:2 * nq + (c + 1) * cc],
                        preferred_element_type=F32)
            v_ref[:, c * cc:(c + 1) * cc] = v.astype(BF16)
        for c in range(E_B // cc):
            arrive(2 + E_B // cc + c)
            g_ref[:, c * cc:(c + 1) * cc] = jnp.dot(
                h, w_ref[:, 2 * nq + E_B + c * cc:2 * nq + E_B + (c + 1) * cc],
                preferred_element_type=F32).astype(BF16)

    staged.run(body)


def _ret_in(x_ctx, x_lat, mod3, w_in, j, layer, lat_rows_per_batch):
    m_rows = x_ctx.shape[0] + x_lat.shape[0]
    tm = ROW_TILE
    nq = H_B * DK_B
    n_ctx_tiles = x_ctx.shape[0] // tm
    return pl.pallas_call(
        functools.partial(_ret_in_kernel, n_ctx_tiles=n_ctx_tiles, j=j),
        out_shape=[jax.ShapeDtypeStruct((m_rows, nq), BF16),
                   jax.ShapeDtypeStruct((nq, m_rows), F32),
                   jax.ShapeDtypeStruct((m_rows, E_B), BF16),
                   jax.ShapeDtypeStruct((m_rows, E_B), BF16)],
        grid=(m_rows // tm,),
        in_specs=_both_streams_specs(x_ctx, x_lat, tm) + [
            pl.BlockSpec((None, 1, 3 * D_MODEL),
                         _both_streams_mod_map(layer, n_ctx_tiles, tm, lat_rows_per_batch)),
            pl.BlockSpec(memory_space=pl.ANY)],
        out_specs=[pl.BlockSpec((tm, nq), lambda m: (m, 0)),
                   pl.BlockSpec((nq, tm), lambda m: (0, m)),
                   pl.BlockSpec((tm, E_B), lambda m: (m, 0)),
                   pl.BlockSpec((tm, E_B), lambda m: (m, 0))],
        scratch_shapes=_StagedWeight.scratch(w_in, WEIGHT_CHUNK_COLS),
        compiler_params=_params(1),
        name="ret_in",
    )(x_ctx, x_lat, mod3, w_in)


def _retention_kernel(*refs, has_state, heads, seqs, fused_out):
    refs = list(refs)
    q_ref, kt_ref, v_ref, g_ref, af_ref, ab_ref = refs[:6]
    del refs[:6]
    if has_state:
        s0f_ref, s0b_ref = refs[:2]
        del refs[:2]
    if fused_out:
        x_ref, mod_ref, w_ref, lng_ref, lnb_ref = refs[:5]
        del refs[:5]
        o_ref = refs.pop(0)
        y_ref = refs.pop()
    else:
        y_ref = refs.pop(0)
    if not has_state:
        sf_ref, sb_ref = refs
    t = q_ref.shape[0] // seqs
    nc = t // CHUNK
    row = lax.broadcasted_iota(jnp.int32, (CHUNK, CHUNK), 0).astype(F32)
    col = lax.broadcasted_iota(jnp.int32, (CHUNK, CHUNK), 1).astype(F32)
    diff = row - col
    idx_col = lax.broadcasted_iota(jnp.int32, (CHUNK, 1), 0).astype(F32)
    idx_row = lax.broadcasted_iota(jnp.int32, (1, CHUNK), 1).astype(F32)

    def state(ref, sq, hh):
        return ref.at[hh] if seqs == 1 else ref.at[sq, hh]

    for hh, sq in [(hh, sq) for hh in range(heads) for sq in range(seqs)]:
        def chunk(c, sq=sq):
            return slice(sq * t + c * CHUNK, sq * t + (c + 1) * CHUNK)

        qk_cols = slice(hh * DK_B, (hh + 1) * DK_B)
        v_cols = slice(hh * DV_B, (hh + 1) * DV_B)
        lg_f = jnp.log1p(-jnp.exp(af_ref[hh]))
        lg_b = jnp.log1p(-jnp.exp(ab_ref[hh]))
        dmask = (jnp.where(diff >= 0, jnp.exp(jnp.maximum(diff, 0.0) * lg_f), 0.0)
                 + jnp.where(diff <= 0, jnp.exp(jnp.maximum(-diff, 0.0) * lg_b), 0.0))
        qd_f = jnp.exp((idx_col + 1.0) * lg_f)
        qd_b = jnp.exp((CHUNK - idx_col) * lg_b)
        kd_f = jnp.exp((CHUNK - 1.0 - idx_row) * lg_f)
        kd_b = jnp.exp(idx_row * lg_b)
        cd_f = jnp.exp(CHUNK * lg_f)
        cd_b = jnp.exp(CHUNK * lg_b)

        def states(order, kd, cd, s):
            seen = {}
            for n, c in enumerate(order):
                seen[c] = None if s is None else s.astype(BF16)
                if has_state and n == nc - 1:
                    return seen, None
                u = jnp.dot((kt_ref[qk_cols, chunk(c)] * kd).astype(BF16), v_ref[chunk(c), v_cols],
                            preferred_element_type=F32)
                s = u if s is None else s * cd + u
            return seen, s

        seen_f, s_f = states(range(nc), kd_f, cd_f,
                             state(s0f_ref, sq, hh)[...] if has_state else None)
        seen_b, s_b = states(range(nc - 1, -1, -1), kd_b, cd_b,
                             state(s0b_ref, sq, hh)[...] if has_state else None)
        if not has_state:
            state(sf_ref, sq, hh)[...] = s_f
            state(sb_ref, sq, hh)[...] = s_b
        for c in range(nc):
            qc = q_ref[chunk(c), qk_cols]
            qk = jnp.dot(qc, kt_ref[qk_cols, chunk(c)].astype(BF16), preferred_element_type=F32)
            o = jnp.dot((qk * dmask).astype(BF16), v_ref[chunk(c), v_cols],
                        preferred_element_type=F32)
            if seen_f[c] is not None:
                o = o + jnp.dot(qc, seen_f[c], preferred_element_type=F32) * qd_f
            if seen_b[c] is not None:
                o = o + jnp.dot(qc, seen_b[c], preferred_element_type=F32) * qd_b
            o = o * lax.rsqrt(jnp.mean(o * o, axis=-1, keepdims=True) + RMS_EPS)
            gate_pre = g_ref[chunk(c), v_cols].astype(F32)
            y_ref[chunk(c), v_cols] = (o * _silu(gate_pre)).astype(BF16)
    if fused_out:
        part = jnp.dot(y_ref[...], w_ref[...].astype(BF16), preferred_element_type=F32)
        gate = mod_ref[:, 2 * D_MODEL:3 * D_MODEL]
        head_steps = H_B // heads
        if head_steps == 1:
            o_ref[...] = _residual_layer_norm(x_ref[...], part, gate, lng_ref[...], lnb_ref[...])
        else:
            step = pl.program_id(1)

            @pl.when(step == 0)
            def _():
                o_ref[...] = part

            @pl.when(jnp.logical_and(step > 0, step < head_steps - 1))
            def _():
                o_ref[...] += part

            @pl.when(step == head_steps - 1)
            def _():
                o_ref[...] = _residual_layer_norm(x_ref[...], o_ref[...] + part, gate, lng_ref[...],
                                                  lnb_ref[...])


def _retention(q, kt, v, g, seq_len, n_seq, first_row, heads, seqs, decay_f, decay_b, states,
               out_proj=None):
    t = seqs * seq_len
    b = n_seq // seqs
    s0 = first_row // t
    has_state = states is not None
    fused_out = out_proj is not None
    seq_dim = None if seqs == 1 else seqs
    q_spec = pl.BlockSpec((t, heads * DK_B), lambda i, h: (s0 + i, h))
    kt_spec = pl.BlockSpec((heads * DK_B, t), lambda i, h: (h, s0 + i))
    vg_spec = pl.BlockSpec((t, heads * DV_B), lambda i, h: (s0 + i, h))
    a_spec = pl.BlockSpec((heads, 1, 1), lambda i, h: (h, 0, 0))
    in_specs = [q_spec, kt_spec, vg_spec, vg_spec, a_spec, a_spec]
    args = [q, kt, v, g, decay_f.reshape(H_B, 1, 1), decay_b.reshape(H_B, 1, 1)]
    out_shape = [jax.ShapeDtypeStruct((b * t, E_B), BF16)]
    out_specs = [pl.BlockSpec((t, heads * DV_B), lambda i, h: (i, h))]
    if has_state:
        s_f, s_b, jj = states
        s_spec = pl.BlockSpec((seq_dim, None, heads, DK_B, DV_B), lambda i, h: (i, jj, h, 0, 0))
        in_specs += [s_spec, s_spec]
        args += [s_f, s_b]
    scratch = []
    if fused_out:
        x2d, mod3, mod_map, w_out, jw, ln_g, ln_b = out_proj
        x_spec = pl.BlockSpec((t, D_MODEL), lambda i, h: (i, 0))
        const2 = lambda i, h: (0, 0)
        if heads == H_B:
            w_spec = _weight_spec(w_out, jw)
        else:
            w_spec = pl.BlockSpec((None, heads * DV_B, D_MODEL), lambda i, h: (jw, h, 0))
        in_specs += [x_spec, pl.BlockSpec((None, 1, 3 * D_MODEL), mod_map), w_spec,
                     pl.BlockSpec((1, D_MODEL), const2), pl.BlockSpec((1, D_MODEL), const2)]
        args += [x2d, mod3, w_out, ln_g.reshape(1, D_MODEL), ln_b.reshape(1, D_MODEL)]
        out_shape = [jax.ShapeDtypeStruct((b * t, D_MODEL), F32)]
        out_specs = [x_spec]
        scratch = [pltpu.VMEM((t, heads * DV_B), BF16)]
    if not has_state:
        so_spec = pl.BlockSpec((seq_dim, None, heads, DK_B, DV_B), lambda i, h: (i, 0, h, 0, 0))
        out_shape += [jax.ShapeDtypeStruct((n_seq, 1, H_B, DK_B, DV_B), F32)] * 2
        out_specs += [so_spec, so_spec]
    return pl.pallas_call(
        functools.partial(_retention_kernel, has_state=has_state, heads=heads, seqs=seqs,
                          fused_out=fused_out),
        out_shape=out_shape,
        grid=(b, H_B // heads),
        in_specs=in_specs,
        out_specs=out_specs,
        scratch_shapes=scratch,
        compiler_params=_params(2),
        name="retention_state" if has_state else "retention",
    )(*args)


CONV_ROWS = 1024
CONV_COLS = 256


def _conv_kernel(xc_ref, xl_ref, mod_ref, w_in_ref, cw_ref, w_out_ref, g_ref, b_ref, o_ref, *,
                 n_ctx_tiles, ctx_seq_len, lat_seq_len):
    x = _both_streams_rows(xc_ref, xl_ref, n_ctx_tiles)
    h = _modulated(x, mod_ref)
    rows = x.shape[0]
    is_ctx = pl.program_id(0) < n_ctx_tiles
    row = lax.broadcasted_iota(jnp.int32, (rows, 1), 0)
    pos = jnp.where(is_ctx, row % ctx_seq_len, row % lat_seq_len)
    has_prev = pos > 0
    has_next = pos < jnp.where(is_ctx, ctx_seq_len - 1, lat_seq_len - 1)
    e = D_MODEL
    for c in range(e // CONV_COLS):
        cols = slice(c * CONV_COLS, (c + 1) * CONV_COLS)

        def proj(part):
            lo = part * e + c * CONV_COLS
            return jnp.dot(h, w_in_ref[:, lo:lo + CONV_COLS], preferred_element_type=F32)

        p = proj(1) * proj(2)
        prev = jnp.where(has_prev, pltpu.roll(p, 1, 0), 0.0)
        nxt = jnp.where(has_next, pltpu.roll(p, rows - 1, 0), 0.0)
        conv = prev * cw_ref[0:1, cols] + p * cw_ref[1:2, cols] + nxt * cw_ref[2:3, cols]
        y = (proj(0) * conv * _silu(proj(3))).astype(BF16)
        part = jnp.dot(y, w_out_ref[cols, :].astype(BF16), preferred_element_type=F32)
        if c == 0:
            o_ref[...] = part
        else:
            o_ref[...] += part
    gate = mod_ref[:, 2 * D_MODEL:3 * D_MODEL]
    o_ref[...] = _residual_layer_norm(x, o_ref[...], gate, g_ref[...], b_ref[...])


def _conv_layer(x_ctx, x_lat, mod3, w_in, conv_w, w_out, j, ln_g, ln_b, layer, ctx_seq_len,
                lat_seq_len):
    m_rows = x_ctx.shape[0] + x_lat.shape[0]
    tm = CONV_ROWS
    n_ctx_tiles = x_ctx.shape[0] // tm
    const2 = lambda m: (0, 0)
    return pl.pallas_call(
        functools.partial(_conv_kernel, n_ctx_tiles=n_ctx_tiles, ctx_seq_len=ctx_seq_len,
                          lat_seq_len=lat_seq_len),
        out_shape=jax.ShapeDtypeStruct((m_rows, D_MODEL), F32),
        grid=(m_rows // tm,),
        in_specs=_both_streams_specs(x_ctx, x_lat, tm) + [
                  pl.BlockSpec((None, 1, 3 * D_MODEL),
                               _both_streams_mod_map(layer, n_ctx_tiles, tm, lat_seq_len)),
                  _weight_spec(w_in, j),
                  pl.BlockSpec((None, 3, D_MODEL), lambda m: (j, 0, 0)),
                  _weight_spec(w_out, j),
                  pl.BlockSpec((1, D_MODEL), const2),
                  pl.BlockSpec((1, D_MODEL), const2)],
        out_specs=pl.BlockSpec((tm, D_MODEL), lambda m: (m, 0)),
        compiler_params=_params(1),
        name="conv_layer",
    )(x_ctx, x_lat, mod3, w_in, conv_w, w_out, ln_g.reshape(1, D_MODEL), ln_b.reshape(1, D_MODEL))


def _separate(xp, xs, m_ctx):
    if xp[0] is xs[0]:
        return (xp[0][:m_ctx], 0), (xs[0][m_ctx:], 0)
    return xp, xs


def kernel(x_prompt, x_sample, cache_k, cache_v, state_fwd, state_bwd, c, c_ctx, w_mod, b_mod, ln_g,
           ln_b, w_in_a, lam_a, subln_a, w_out_a, w_in_b, decay_fwd, decay_bwd, w_out_b, w_in_c,
           conv_c, w_out_c):
    bp, tp, d = x_prompt.shape
    bs, ts, _ = x_sample.shape
    cvec = jnp.concatenate([c, c_ctx[None], jnp.zeros((MOD_ROWS - bs - 1, d), F32)], axis=0)
    mod3 = _modulation(cvec, w_mod, b_mod)
    rope_tables = _rope_tables(ts)

    m_ctx, m_lat = bp * tp, bs * ts
    xp = (x_prompt.reshape(m_ctx, d), 0)
    xs = (x_sample.reshape(m_lat, d), 0)
    n_attn = (DEPTH + N_MIXERS - 1) // N_MIXERS
    new_cache_k = new_cache_v = jax.ShapeDtypeStruct((bp, n_attn, tp, H_A, DV_A), F32)
    new_sf, new_sb = [], []
    for i in range(DEPTH):
        kind, j = i % N_MIXERS, i // N_MIXERS
        if kind == 0:
            x_ctx, new_cache_k, new_cache_v = _ctx_attention_layer(
                xp, m_ctx, tp, mod3, w_in_a, w_out_a, j, lam_a[j], subln_a[j], ln_g[i], ln_b[i], i,
                (new_cache_k, new_cache_v))
            qkvz = _attn_in(xs, m_lat, mod3, w_in_a, j, i, ts, rope_tables)
            xs = (_attention(qkvz, 0, m_lat, xs, ts, (cache_k, cache_v, j), mod3, lam_a[j],
                             subln_a[j], w_out_a, j, ln_g[i], ln_b[i], i, True), 0)
            xp = (x_ctx, 0)
        else:
            xp, xs = _separate(xp, xs, m_ctx)
            if kind == 1:
                q, kt, v, g = _ret_in(xp[0], xs[0], mod3, w_in_b, j, i, ts)
                x_ctx, s_f, s_b = _retention(
                    q, kt, v, g, tp, bp, 0, H_B, CTX_RET_SEQS, decay_fwd[j], decay_bwd[j], None,
                    out_proj=(xp[0], mod3, _mod_row_map(i, None, tp), w_out_b, j, ln_g[i],
                              ln_b[i]))
                new_sf.append(s_f)
                new_sb.append(s_b)
                (x_lat,) = _retention(
                    q, kt, v, g, ts, bs, m_ctx, LAT_RET_HEADS, 1, decay_fwd[j], decay_bwd[j],
                    (state_fwd, state_bwd, j),
                    out_proj=(xs[0], mod3, _mod_row_map(i, ts, ts), w_out_b, j, ln_g[i], ln_b[i]))
                xp, xs = (x_ctx, 0), (x_lat, 0)
            else:
                x_all = _conv_layer(xp[0], xs[0], mod3, w_in_c, conv_c, w_out_c, j, ln_g[i],
                                    ln_b[i], i, tp, ts)
                xp, xs = (x_all, 0), (x_all, m_ctx)
    xp, xs = _separate(xp, xs, m_ctx)
    y_prompt = xp[0].reshape(bp, tp, d)
    y_sample = xs[0].reshape(bs, ts, d)
    new_state_fwd = jnp.concatenate(new_sf, axis=1)
    new_state_bwd = jnp.concatenate(new_sb, axis=1)
    return (y_prompt, y_sample, new_cache_k, new_cache_v, new_state_fwd, new_state_bwd)
```

```python
import functools
import math

import jax
import jax.numpy as jnp
from jax import lax
from jax.experimental import pallas as pl
from jax.experimental.pallas import tpu as pltpu

F32 = jnp.float32
BF16 = jnp.bfloat16

D_MODEL = 1024
DEPTH = 4
N_MIXERS = 3
GRID_W = 64
H_A = 8
DH_A = 64
DV_A = 128
ROPE_HALF = DH_A // 2
SCORE_SCALE = DH_A ** -0.5 * math.log2(math.e)
H_B = 4
DK_B = 256
DV_B = 512
E_B = H_B * DV_B
CHUNK = 256
LAT_RET_HEADS = 2
CTX_RET_SEQS = 2
ALPHA = (2.0 * DEPTH) ** 0.25
ROPE_BASE = 10000.0
LN_EPS = 1e-5
RMS_EPS = 1e-6

MOD_ROWS = 8
CTX_ROW = 4
VMEM_LIMIT_BYTES = 58 * 1024 * 1024
ROW_TILE = 512
RET_IN_COLS = 1024


def _params(n_axes):
    return pltpu.CompilerParams(dimension_semantics=("arbitrary",) * n_axes,
                                vmem_limit_bytes=VMEM_LIMIT_BYTES)


def _silu(x):
    return x * jax.nn.sigmoid(x)


def _residual_layer_norm(x, out, gate, g, b):
    r = ALPHA * x + gate * out
    mu = jnp.mean(r, axis=-1, keepdims=True)
    d = r - mu
    var = jnp.mean(d * d, axis=-1, keepdims=True)
    return d * lax.rsqrt(var + LN_EPS) * g + b


def _modulated(x, mod_ref):
    shift = mod_ref[:, 0:D_MODEL]
    scale = mod_ref[:, D_MODEL:2 * D_MODEL]
    return x * (1.0 + scale) + shift


def _mod_row_map(layer, rows_per_batch, tile):
    if rows_per_batch is None:
        return lambda m, *_: (layer * MOD_ROWS + CTX_ROW, 0, 0)
    return lambda m, *_: (layer * MOD_ROWS + (m * tile) // rows_per_batch, 0, 0)


def _both_streams_specs(x_ctx, x_lat, tm):
    n_ctx_tiles = x_ctx.shape[0] // tm
    return [pl.BlockSpec((tm, D_MODEL), lambda m: (jnp.minimum(m, n_ctx_tiles - 1), 0)),
            pl.BlockSpec((tm, D_MODEL), lambda m: (jnp.maximum(m - n_ctx_tiles, 0), 0))]


def _both_streams_rows(x_ctx_ref, x_lat_ref, n_ctx_tiles):
    return jnp.where(pl.program_id(0) < n_ctx_tiles, x_ctx_ref[...], x_lat_ref[...])


def _both_streams_mod_map(layer, n_ctx_tiles, tm, lat_rows_per_batch):
    def index_map(m):
        lat_row = ((m - n_ctx_tiles) * tm) // lat_rows_per_batch
        return (layer * MOD_ROWS + jnp.where(m < n_ctx_tiles, CTX_ROW, lat_row), 0, 0)
    return index_map


def _weight_spec(w, j):
    return pl.BlockSpec((None,) + w.shape[1:], lambda *_: (j, 0, 0), pipeline_mode=pl.Buffered(1))


def _mod_kernel(c_ref, cctx_ref, w_ref, b_ref, o_ref):
    pad = jnp.zeros((MOD_ROWS - c_ref.shape[0] - 1, D_MODEL), F32)
    s = _silu(jnp.concatenate([c_ref[...], cctx_ref[...], pad], axis=0))
    m = jnp.dot(s, w_ref[...], preferred_element_type=F32) + b_ref[...]
    for r in range(MOD_ROWS):
        o_ref[r] = m[r:r + 1]


def _modulation(c, c_ctx, w_mod, b_mod):
    n = 3 * D_MODEL
    return pl.pallas_call(
        _mod_kernel,
        out_shape=jax.ShapeDtypeStruct((DEPTH * MOD_ROWS, 1, n), F32),
        grid=(DEPTH,),
        in_specs=[pl.BlockSpec(c.shape, lambda i: (0, 0)),
                  pl.BlockSpec((1, D_MODEL), lambda i: (0, 0)),
                  pl.BlockSpec((None, D_MODEL, n), lambda i: (i, 0, 0)),
                  pl.BlockSpec((None, 1, n), lambda i: (i, 0, 0))],
        out_specs=pl.BlockSpec((MOD_ROWS, 1, n), lambda i: (i, 0, 0)),
        compiler_params=_params(1),
        name="modulation",
    )(c, c_ctx.reshape(1, D_MODEL), w_mod, b_mod.reshape(DEPTH, 1, n))


def _rope(xh, cos4, sin4, first_half):
    swapped = jnp.where(first_half, pltpu.roll(xh, DV_A - ROPE_HALF, 1),
                        pltpu.roll(xh, ROPE_HALF, 1))
    return xh * cos4 + swapped * sin4


def _store_heads(o_ref, x, seq_len):
    for b in range(o_ref.shape[0]):
        xb = x[b * seq_len:(b + 1) * seq_len].reshape(seq_len, H_A, DV_A)
        if len(o_ref.shape) == 4:
            o_ref[b] = xb
        else:
            o_ref[b, 0] = xb
            for s in range(1, o_ref.shape[1]):
                o_ref[b, s] = jnp.zeros_like(xb)


def _attn_in_body(x_ref, mod_ref, w_ref, rope_refs, q_ref, k_ref, vt_ref, z_ref, cache_refs, seq_len):
    h = _modulated(x_ref[...], mod_ref)
    if rope_refs is not None:
        lane = lax.broadcasted_iota(jnp.int32, (1, DV_A), 1)
        first_half = (lane % DH_A) < ROPE_HALF
        cos4 = rope_refs[0][...]
        sin4 = rope_refs[1][...]
    q_all = jnp.dot(h, w_ref[:, 0:D_MODEL], preferred_element_type=F32)
    k_all = jnp.dot(h, w_ref[:, D_MODEL:2 * D_MODEL], preferred_element_type=F32)
    if cache_refs is not None:
        _store_heads(cache_refs[0], k_all, seq_len)
    for hd in range(H_A):
        cols = slice(hd * DV_A, (hd + 1) * DV_A)
        q = q_all[:, cols]
        k = k_all[:, cols]
        if rope_refs is not None:
            q = _rope(q, cos4, sin4, first_half)
            k = _rope(k, cos4, sin4, first_half)
        q_ref[:, cols] = (q * SCORE_SCALE).astype(BF16)
        k_ref[:, cols] = k.astype(BF16)
    v = jnp.dot(h, w_ref[:, 2 * D_MODEL:3 * D_MODEL], preferred_element_type=F32)
    if cache_refs is not None:
        _store_heads(cache_refs[1], v, seq_len)
    vt_ref[...] = v.T.astype(BF16)
    z_ref[...] = jnp.dot(h, w_ref[:, 3 * D_MODEL:4 * D_MODEL],
                         preferred_element_type=F32).astype(z_ref.dtype)


def _attn_in_kernel(x_ref, mod_ref, w_ref, cos_ref, sin_ref, q_ref, k_ref, vt_ref, z_ref):
    _attn_in_body(x_ref, mod_ref, w_ref, (cos_ref, sin_ref), q_ref, k_ref, vt_ref, z_ref, None, None)


def _attn_in(x, m_rows, mod3, w_in, j, layer, rows_per_batch, rope_tables):
    x2d, x_first_row = x
    tm = ROW_TILE
    seq_tiles = rows_per_batch // tm
    row_spec = pl.BlockSpec((tm, D_MODEL), lambda m: (m, 0))
    tab_spec = pl.BlockSpec((tm, DV_A), lambda m: (m % seq_tiles, 0))
    return pl.pallas_call(
        _attn_in_kernel,
        out_shape=[jax.ShapeDtypeStruct((m_rows, D_MODEL), BF16),
                   jax.ShapeDtypeStruct((m_rows, D_MODEL), BF16),
                   jax.ShapeDtypeStruct((D_MODEL, m_rows), BF16),
                   jax.ShapeDtypeStruct((m_rows, D_MODEL), BF16)],
        grid=(m_rows // tm,),
        in_specs=[pl.BlockSpec((tm, D_MODEL), lambda m: (x_first_row // tm + m, 0)),
                  pl.BlockSpec((None, 1, 3 * D_MODEL), _mod_row_map(layer, rows_per_batch, tm)),
                  _weight_spec(w_in, j), tab_spec, tab_spec],
        out_specs=[row_spec, row_spec, pl.BlockSpec((D_MODEL, tm), lambda m: (0, m)), row_spec],
        compiler_params=_params(1),
        name="attn_in_rope",
    )(x2d, mod3, w_in, *rope_tables)


ONES_ROWS = 16
CTX_ATTN_SEQS = 2
ATTN_ROWS = 512


def _slab_reduce(op, x):
    parts = [x[i:i + 8] for i in range(0, x.shape[0], 8)]
    while len(parts) > 1:
        parts = [op(parts[i], parts[i + 1]) for i in range(0, len(parts) - 1, 2)] + (
            [parts[-1]] if len(parts) % 2 else [])
    return parts[0]


def _attn_kernel(*refs, layer_idx, has_ctx, n_seq, shared_keys):
    if has_ctx:
        (q_ref, k_ref, vt_ref, z_ref, x_ref, kc_ref, vc_ref, mod_ref, lam_ref, subln_ref,
         w_ref, g_ref, b_ref, o_ref, y_ref, kcb_ref, vct_ref) = refs

        @pl.when(pl.program_id(1) == 0)
        def _():
            n_ctx = kc_ref.shape[0]
            kcb_ref[...] = kc_ref[...].reshape(n_ctx, D_MODEL).astype(BF16)
            vct_ref[...] = vc_ref[...].reshape(n_ctx, D_MODEL).T.astype(BF16)
    else:
        (q_ref, k_ref, vt_ref, z_ref, x_ref, mod_ref, lam_ref, subln_ref,
         w_ref, g_ref, b_ref, o_ref, y_ref) = refs
    tq = q_ref.shape[0] // n_seq
    lam_init = 0.8 - 0.6 * math.exp(-0.3 * layer_idx)
    lm = lam_ref[...]
    lam = (jnp.exp(jnp.sum(lm[0:1] * lm[1:2], axis=-1, keepdims=True))
           - jnp.exp(jnp.sum(lm[2:3] * lm[3:4], axis=-1, keepdims=True)) + lam_init)
    lane = lax.broadcasted_iota(jnp.int32, (1, DV_A), 1)
    first = lane < DH_A
    subln = jnp.broadcast_to(subln_ref[...], (DV_A, tq))
    nt = (((1,), (1,)), ((), ()))
    t = k_ref.shape[0] if shared_keys else k_ref.shape[0] // n_seq

    def key_rows(sq):
        return slice(0, t) if shared_keys else slice(sq * t, (sq + 1) * t)

    def scores(unit):
        sq, hd = unit
        cols = slice(hd * DV_A, (hd + 1) * DV_A)
        qh = q_ref[sq * tq:(sq + 1) * tq, cols]
        zero = jnp.zeros_like(qh)
        qq = jnp.concatenate([jnp.where(first, qh, zero), jnp.where(first, zero, qh)], axis=0)
        parts = [lax.dot_general(k_ref[key_rows(sq), cols], qq, nt, preferred_element_type=F32)]
        if has_ctx:
            parts.append(lax.dot_general(kcb_ref[:, cols], qq, nt, preferred_element_type=F32))
        return parts

    def exps(parts):
        m8 = functools.reduce(jnp.maximum, [_slab_reduce(jnp.maximum, s) for s in parts])
        m = jnp.max(m8, axis=0, keepdims=True)
        return [jnp.exp2(s - m).astype(BF16) for s in parts]

    def with_ones(vals_t):
        return jnp.concatenate([vals_t, jnp.ones((ONES_ROWS, vals_t.shape[1]), BF16)], axis=0)

    def finish(unit, es):
        sq, hd = unit
        cols = slice(hd * DV_A, (hd + 1) * DV_A)
        rows = slice(sq * tq, (sq + 1) * tq)
        acc = jnp.dot(with_ones(vt_ref[cols, key_rows(sq)]), es[0], preferred_element_type=F32)
        if has_ctx:
            acc = acc + jnp.dot(with_ones(vct_ref[cols, :]), es[1], preferred_element_type=F32)
        inv = 1.0 / acc[DV_A:DV_A + 1, :]
        ot = acc[:DV_A, :tq] * inv[:, :tq] - acc[:DV_A, tq:] * (inv[:, tq:] * lam)
        ot = ot * lax.rsqrt(jnp.mean(ot * ot, axis=0, keepdims=True) + RMS_EPS)
        ot = ot * subln * (1.0 - lam_init)
        y_ref[rows, cols] = (ot.T * _silu(z_ref[rows, cols].astype(F32))).astype(BF16)

    units = [(sq, hd) for hd in range(H_A) for sq in range(n_seq)]
    s_ahead = {u: scores(units[u]) for u in range(min(2, len(units)))}
    e_ahead = {0: exps(s_ahead.pop(0))}
    for u in range(len(units)):
        if u + 2 < len(units):
            s_ahead[u + 2] = scores(units[u + 2])
        if u + 1 < len(units):
            e_ahead[u + 1] = exps(s_ahead.pop(u + 1))
        finish(units[u], e_ahead.pop(u))
    out = jnp.dot(y_ref[...], w_ref[...].astype(BF16), preferred_element_type=F32)
    gate = mod_ref[:, 2 * D_MODEL:3 * D_MODEL]
    o_ref[...] = _residual_layer_norm(x_ref[...], out, gate, g_ref[...], b_ref[...])


def _attention(qkvz, first_row, m_rows, x, seq_len, ctx, mod3, lam, subln, w_out, j, ln_g, ln_b,
               layer, per_batch_rows):
    q, k, vt, z = qkvz
    t = seq_len
    b = m_rows // t
    has_ctx = ctx is not None
    if t <= 256 and not per_batch_rows and not has_ctx and b % 2 == 0:
        rows, n_seq, shared_keys, n_kseq = 2 * t, 2, False, 2
    else:
        rows, n_seq, shared_keys, n_kseq = ATTN_ROWS, 1, True, 1
    nq = n_kseq * t // rows
    x, x_first_row = x
    r0, k0 = first_row // rows, first_row // (n_kseq * t)
    x_spec = pl.BlockSpec((rows, D_MODEL), lambda i, j: (x_first_row // rows + i * nq + j, 0))
    q_spec = pl.BlockSpec((rows, D_MODEL), lambda i, j: (r0 + i * nq + j, 0))
    o_spec = pl.BlockSpec((rows, D_MODEL), lambda i, j: (i * nq + j, 0))
    k_spec = pl.BlockSpec((n_kseq * t, D_MODEL), lambda i, j: (k0 + i, 0))
    vt_spec = pl.BlockSpec((D_MODEL, n_kseq * t), lambda i, j: (0, k0 + i))
    if per_batch_rows:
        mod_map = lambda i, j: (layer * MOD_ROWS + i, 0, 0)
    else:
        mod_map = lambda i, j: (layer * MOD_ROWS + CTX_ROW, 0, 0)
    const2 = lambda i, j: (0, 0)
    in_specs = [q_spec, k_spec, vt_spec, q_spec, x_spec]
    args = [q, k, vt, z, x]
    scratch = [pltpu.VMEM((rows, D_MODEL), BF16)]
    if has_ctx:
        cache_k, cache_v, jj = ctx
        n_ctx = cache_k.shape[2]
        scratch += [pltpu.VMEM((n_ctx, D_MODEL), BF16), pltpu.VMEM((D_MODEL, n_ctx), BF16)]
        c_spec = pl.BlockSpec((None, None, n_ctx, H_A, DV_A),
                              lambda i, j: (i, jj, 0, 0, 0))
        in_specs += [c_spec, c_spec]
        args += [cache_k, cache_v]
    in_specs += [pl.BlockSpec((None, 1, 3 * D_MODEL), mod_map),
                 pl.BlockSpec((4, DH_A), const2),
                 pl.BlockSpec((DV_A, 1), const2),
                 _weight_spec(w_out, j),
                 pl.BlockSpec((1, D_MODEL), const2),
                 pl.BlockSpec((1, D_MODEL), const2)]
    args += [mod3, lam, subln.reshape(DV_A, 1), w_out, ln_g.reshape(1, D_MODEL),
             ln_b.reshape(1, D_MODEL)]
    return pl.pallas_call(
        functools.partial(_attn_kernel, layer_idx=layer, has_ctx=has_ctx, n_seq=n_seq,
                          shared_keys=shared_keys),
        out_shape=jax.ShapeDtypeStruct((m_rows, D_MODEL), F32),
        grid=(b // n_kseq, nq),
        in_specs=in_specs,
        out_specs=o_spec,
        scratch_shapes=scratch,
        compiler_params=_params(2),
        name="diff_attn_ctx" if has_ctx else "diff_attn",
    )(*args)


def _ctx_attn_layer_kernel(*refs, layer_idx, seq_len, n_aliased):
    x_ref, mod_ref, w_in_ref, lam_ref, subln_ref, w_out_ref, g_ref, b_ref = refs[:8]
    o_ref, ck_ref, cv_ref, q_ref, k_ref, vt_ref, z_ref, y_ref = refs[8 + n_aliased:]
    _attn_in_body(x_ref, mod_ref, w_in_ref, None, q_ref, k_ref, vt_ref, z_ref, (ck_ref, cv_ref),
                  seq_len)
    _attn_kernel(q_ref, k_ref, vt_ref, z_ref, x_ref, mod_ref, lam_ref, subln_ref, w_out_ref, g_ref,
                 b_ref, o_ref, y_ref, layer_idx=layer_idx, has_ctx=False,
                 n_seq=x_ref.shape[0] // seq_len, shared_keys=False)


def _ctx_attention_layer(x, m_rows, seq_len, mod3, w_in, w_out, j, lam, subln, ln_g, ln_b, layer,
                         cache_out):
    x2d, x_first_row = x
    rows = CTX_ATTN_SEQS * seq_len
    const2 = lambda m: (0, 0)
    in_specs = [pl.BlockSpec((rows, D_MODEL), lambda m: (x_first_row // rows + m, 0)),
                pl.BlockSpec((None, 1, 3 * D_MODEL), lambda m: (layer * MOD_ROWS + CTX_ROW, 0, 0)),
                _weight_spec(w_in, j),
                pl.BlockSpec((4, DH_A), const2),
                pl.BlockSpec((DV_A, 1), const2),
                _weight_spec(w_out, j),
                pl.BlockSpec((1, D_MODEL), const2),
                pl.BlockSpec((1, D_MODEL), const2)]
    args = [x2d, mod3, w_in, lam, subln.reshape(DV_A, 1), w_out, ln_g.reshape(1, D_MODEL),
            ln_b.reshape(1, D_MODEL)]
    new_k, new_v = cache_out
    cache_shape = jax.ShapeDtypeStruct(new_k.shape, F32)
    aliases = {}
    if j == 0:
        cache_spec = pl.BlockSpec((CTX_ATTN_SEQS, new_k.shape[1], seq_len, H_A, DV_A),
                                  lambda m: (m, 0, 0, 0, 0))
    else:
        cache_spec = pl.BlockSpec((CTX_ATTN_SEQS, None, seq_len, H_A, DV_A),
                                  lambda m: (m, j, 0, 0, 0))
        in_specs += [pl.BlockSpec(memory_space=pl.ANY)] * 2
        args += [new_k, new_v]
        aliases = {len(args) - 2: 1, len(args) - 1: 2}
    return pl.pallas_call(
        functools.partial(_ctx_attn_layer_kernel, layer_idx=layer, seq_len=seq_len,
                          n_aliased=len(aliases)),
        out_shape=[jax.ShapeDtypeStruct((m_rows, D_MODEL), F32), cache_shape, cache_shape],
        grid=(m_rows // rows,),
        in_specs=in_specs,
        out_specs=[pl.BlockSpec((rows, D_MODEL), lambda m: (m, 0)), cache_spec, cache_spec],
        scratch_shapes=[pltpu.VMEM((rows, D_MODEL), BF16), pltpu.VMEM((rows, D_MODEL), BF16),
                        pltpu.VMEM((D_MODEL, rows), BF16), pltpu.VMEM((rows, D_MODEL), BF16),
                        pltpu.VMEM((rows, D_MODEL), BF16)],
        input_output_aliases=aliases,
        compiler_params=_params(1),
        name="ctx_attn_layer",
    )(*args)


def _rope_tables(n_tokens):
    rows = n_tokens // GRID_W
    r = jnp.repeat(jnp.arange(rows, dtype=F32), GRID_W)
    col = jnp.tile(jnp.arange(GRID_W, dtype=F32), rows)
    n_freq = DH_A // 4
    inv = ROPE_BASE ** (-jnp.arange(n_freq, dtype=F32) / n_freq)
    ang = jnp.concatenate([r[:, None] * inv, col[:, None] * inv], -1)
    cos, sin = jnp.cos(ang), jnp.sin(ang)
    return jnp.tile(cos, (1, 4)), jnp.concatenate([-sin, sin, -sin, sin], -1)


def _ret_in_kernel(xc_ref, xl_ref, mod_ref, w_ref, q_ref, kt_ref, v_ref, g_ref, *, n_ctx_tiles):
    h = _modulated(_both_streams_rows(xc_ref, xl_ref, n_ctx_tiles), mod_ref)
    nq = H_B * DK_B
    cc = RET_IN_COLS
    q_ref[...] = jnp.dot(h, w_ref[:, 0:nq], preferred_element_type=F32).astype(BF16)
    k = jnp.dot(h, w_ref[:, nq:2 * nq], preferred_element_type=F32) * (DK_B ** -0.5)
    kt_ref[...] = k.T
    for c in range(E_B // cc):
        v = jnp.dot(h, w_ref[:, 2 * nq + c * cc:2 * nq + (c + 1) * cc], preferred_element_type=F32)
        v_ref[:, c * cc:(c + 1) * cc] = v.astype(BF16)
    for c in range(E_B // cc):
        g_ref[:, c * cc:(c + 1) * cc] = jnp.dot(
            h, w_ref[:, 2 * nq + E_B + c * cc:2 * nq + E_B + (c + 1) * cc],
            preferred_element_type=F32).astype(BF16)


def _ret_in(x_ctx, x_lat, mod3, w_in, j, layer, lat_rows_per_batch):
    m_rows = x_ctx.shape[0] + x_lat.shape[0]
    tm = ROW_TILE
    nq = H_B * DK_B
    n_ctx_tiles = x_ctx.shape[0] // tm
    return pl.pallas_call(
        functools.partial(_ret_in_kernel, n_ctx_tiles=n_ctx_tiles),
        out_shape=[jax.ShapeDtypeStruct((m_rows, nq), BF16),
                   jax.ShapeDtypeStruct((nq, m_rows), F32),
                   jax.ShapeDtypeStruct((m_rows, E_B), BF16),
                   jax.ShapeDtypeStruct((m_rows, E_B), BF16)],
        grid=(m_rows // tm,),
        in_specs=_both_streams_specs(x_ctx, x_lat, tm) + [
            pl.BlockSpec((None, 1, 3 * D_MODEL),
                         _both_streams_mod_map(layer, n_ctx_tiles, tm, lat_rows_per_batch)),
            _weight_spec(w_in, j)],
        out_specs=[pl.BlockSpec((tm, nq), lambda m: (m, 0)),
                   pl.BlockSpec((nq, tm), lambda m: (0, m)),
                   pl.BlockSpec((tm, E_B), lambda m: (m, 0)),
                   pl.BlockSpec((tm, E_B), lambda m: (m, 0))],
        compiler_params=_params(1),
        name="ret_in",
    )(x_ctx, x_lat, mod3, w_in)


def _retention_kernel(*refs, has_state, heads, seqs, fused_out):
    refs = list(refs)
    q_ref, kt_ref, v_ref, g_ref, af_ref, ab_ref = refs[:6]
    del refs[:6]
    if has_state:
        s0f_ref, s0b_ref = refs[:2]
        del refs[:2]
    if fused_out:
        x_ref, mod_ref, w_ref, lng_ref, lnb_ref = refs[:5]
        del refs[:5]
        o_ref = refs.pop(0)
        y_ref = refs.pop()
    else:
        y_ref = refs.pop(0)
    if not has_state:
        sf_ref, sb_ref = refs
    t = q_ref.shape[0] // seqs
    nc = t // CHUNK
    row = lax.broadcasted_iota(jnp.int32, (CHUNK, CHUNK), 0).astype(F32)
    col = lax.broadcasted_iota(jnp.int32, (CHUNK, CHUNK), 1).astype(F32)
    diff = row - col
    idx_col = lax.broadcasted_iota(jnp.int32, (CHUNK, 1), 0).astype(F32)
    idx_row = lax.broadcasted_iota(jnp.int32, (1, CHUNK), 1).astype(F32)

    def state(ref, sq, hh):
        return ref.at[hh] if seqs == 1 else ref.at[sq, hh]

    for hh, sq in [(hh, sq) for hh in range(heads) for sq in range(seqs)]:
        def chunk(c, sq=sq):
            return slice(sq * t + c * CHUNK, sq * t + (c + 1) * CHUNK)

        qk_cols = slice(hh * DK_B, (hh + 1) * DK_B)
        v_cols = slice(hh * DV_B, (hh + 1) * DV_B)
        lg_f = jnp.log1p(-jnp.exp(af_ref[hh]))
        lg_b = jnp.log1p(-jnp.exp(ab_ref[hh]))
        dmask = (jnp.where(diff >= 0, jnp.exp(jnp.maximum(diff, 0.0) * lg_f), 0.0)
                 + jnp.where(diff <= 0, jnp.exp(jnp.maximum(-diff, 0.0) * lg_b), 0.0))
        qd_f = jnp.exp((idx_col + 1.0) * lg_f)
        qd_b = jnp.exp((CHUNK - idx_col) * lg_b)
        kd_f = jnp.exp((CHUNK - 1.0 - idx_row) * lg_f)
        kd_b = jnp.exp(idx_row * lg_b)
        cd_f = jnp.exp(CHUNK * lg_f)
        cd_b = jnp.exp(CHUNK * lg_b)

        def states(order, kd, cd, s):
            seen = {}
            for n, c in enumerate(order):
                seen[c] = None if s is None else s.astype(BF16)
                if has_state and n == nc - 1:
                    return seen, None
                u = jnp.dot((kt_ref[qk_cols, chunk(c)] * kd).astype(BF16), v_ref[chunk(c), v_cols],
                            preferred_element_type=F32)
                s = u if s is None else s * cd + u
            return seen, s

        seen_f, s_f = states(range(nc), kd_f, cd_f,
                             state(s0f_ref, sq, hh)[...] if has_state else None)
        seen_b, s_b = states(range(nc - 1, -1, -1), kd_b, cd_b,
                             state(s0b_ref, sq, hh)[...] if has_state else None)
        if not has_state:
            state(sf_ref, sq, hh)[...] = s_f
            state(sb_ref, sq, hh)[...] = s_b
        for c in range(nc):
            qc = q_ref[chunk(c), qk_cols]
            qk = jnp.dot(qc, kt_ref[qk_cols, chunk(c)].astype(BF16), preferred_element_type=F32)
            o = jnp.dot((qk * dmask).astype(BF16), v_ref[chunk(c), v_cols],
                        preferred_element_type=F32)
            if seen_f[c] is not None:
                o = o + jnp.dot(qc, seen_f[c], preferred_element_type=F32) * qd_f
            if seen_b[c] is not None:
                o = o + jnp.dot(qc, seen_b[c], preferred_element_type=F32) * qd_b
            o = o * lax.rsqrt(jnp.mean(o * o, axis=-1, keepdims=True) + RMS_EPS)
            gate_pre = g_ref[chunk(c), v_cols].astype(F32)
            y_ref[chunk(c), v_cols] = (o * _silu(gate_pre)).astype(BF16)
    if fused_out:
        part = jnp.dot(y_ref[...], w_ref[...].astype(BF16), preferred_element_type=F32)
        gate = mod_ref[:, 2 * D_MODEL:3 * D_MODEL]
        head_steps = H_B // heads
        if head_steps == 1:
            o_ref[...] = _residual_layer_norm(x_ref[...], part, gate, lng_ref[...], lnb_ref[...])
        else:
            step = pl.program_id(1)

            @pl.when(step == 0)
            def _():
                o_ref[...] = part

            @pl.when(jnp.logical_and(step > 0, step < head_steps - 1))
            def _():
                o_ref[...] += part

            @pl.when(step == head_steps - 1)
            def _():
                o_ref[...] = _residual_layer_norm(x_ref[...], o_ref[...] + part, gate, lng_ref[...],
                                                  lnb_ref[...])


def _retention(q, kt, v, g, seq_len, n_seq, first_row, heads, seqs, decay_f, decay_b, states,
               out_proj=None):
    t = seqs * seq_len
    b = n_seq // seqs
    s0 = first_row // t
    has_state = states is not None
    fused_out = out_proj is not None
    seq_dim = None if seqs == 1 else seqs
    q_spec = pl.BlockSpec((t, heads * DK_B), lambda i, h: (s0 + i, h))
    kt_spec = pl.BlockSpec((heads * DK_B, t), lambda i, h: (h, s0 + i))
    vg_spec = pl.BlockSpec((t, heads * DV_B), lambda i, h: (s0 + i, h))
    a_spec = pl.BlockSpec((heads, 1, 1), lambda i, h: (h, 0, 0))
    in_specs = [q_spec, kt_spec, vg_spec, vg_spec, a_spec, a_spec]
    args = [q, kt, v, g, decay_f.reshape(H_B, 1, 1), decay_b.reshape(H_B, 1, 1)]
    out_shape = [jax.ShapeDtypeStruct((b * t, E_B), BF16)]
    out_specs = [pl.BlockSpec((t, heads * DV_B), lambda i, h: (i, h))]
    if has_state:
        s_f, s_b, jj = states
        s_spec = pl.BlockSpec((seq_dim, None, heads, DK_B, DV_B), lambda i, h: (i, jj, h, 0, 0))
        in_specs += [s_spec, s_spec]
        args += [s_f, s_b]
    scratch = []
    if fused_out:
        x2d, mod3, mod_map, w_out, jw, ln_g, ln_b = out_proj
        x_spec = pl.BlockSpec((t, D_MODEL), lambda i, h: (i, 0))
        const2 = lambda i, h: (0, 0)
        if heads == H_B:
            w_spec = _weight_spec(w_out, jw)
        else:
            w_spec = pl.BlockSpec((None, heads * DV_B, D_MODEL), lambda i, h: (jw, h, 0))
        in_specs += [x_spec, pl.BlockSpec((None, 1, 3 * D_MODEL), mod_map), w_spec,
                     pl.BlockSpec((1, D_MODEL), const2), pl.BlockSpec((1, D_MODEL), const2)]
        args += [x2d, mod3, w_out, ln_g.reshape(1, D_MODEL), ln_b.reshape(1, D_MODEL)]
        out_shape = [jax.ShapeDtypeStruct((b * t, D_MODEL), F32)]
        out_specs = [x_spec]
        scratch = [pltpu.VMEM((t, heads * DV_B), BF16)]
    if not has_state:
        so_spec = pl.BlockSpec((seq_dim, None, heads, DK_B, DV_B), lambda i, h: (i, 0, h, 0, 0))
        out_shape += [jax.ShapeDtypeStruct((n_seq, 1, H_B, DK_B, DV_B), F32)] * 2
        out_specs += [so_spec, so_spec]
    return pl.pallas_call(
        functools.partial(_retention_kernel, has_state=has_state, heads=heads, seqs=seqs,
                          fused_out=fused_out),
        out_shape=out_shape,
        grid=(b, H_B // heads),
        in_specs=in_specs,
        out_specs=out_specs,
        scratch_shapes=scratch,
        compiler_params=_params(2),
        name="retention_state" if has_state else "retention",
    )(*args)


CONV_ROWS = 1024
CONV_COLS = 256


def _conv_kernel(xc_ref, xl_ref, mod_ref, w_in_ref, cw_ref, w_out_ref, g_ref, b_ref, o_ref, *,
                 n_ctx_tiles, ctx_seq_len, lat_seq_len):
    x = _both_streams_rows(xc_ref, xl_ref, n_ctx_tiles)
    h = _modulated(x, mod_ref)
    rows = x.shape[0]
    is_ctx = pl.program_id(0) < n_ctx_tiles
    row = lax.broadcasted_iota(jnp.int32, (rows, 1), 0)
    pos = jnp.where(is_ctx, row % ctx_seq_len, row % lat_seq_len)
    has_prev = pos > 0
    has_next = pos < jnp.where(is_ctx, ctx_seq_len - 1, lat_seq_len - 1)
    e = D_MODEL
    for c in range(e // CONV_COLS):
        cols = slice(c * CONV_COLS, (c + 1) * CONV_COLS)

        def proj(part):
            lo = part * e + c * CONV_COLS
            return jnp.dot(h, w_in_ref[:, lo:lo + CONV_COLS], preferred_element_type=F32)

        p = proj(1) * proj(2)
        prev = jnp.where(has_prev, pltpu.roll(p, 1, 0), 0.0)
        nxt = jnp.where(has_next, pltpu.roll(p, rows - 1, 0), 0.0)
        conv = prev * cw_ref[0:1, cols] + p * cw_ref[1:2, cols] + nxt * cw_ref[2:3, cols]
        y = (proj(0) * conv * _silu(proj(3))).astype(BF16)
        part = jnp.dot(y, w_out_ref[cols, :].astype(BF16), preferred_element_type=F32)
        if c == 0:
            o_ref[...] = part
        else:
            o_ref[...] += part
    gate = mod_ref[:, 2 * D_MODEL:3 * D_MODEL]
    o_ref[...] = _residual_layer_norm(x, o_ref[...], gate, g_ref[...], b_ref[...])


def _conv_layer(x_ctx, x_lat, mod3, w_in, conv_w, w_out, j, ln_g, ln_b, layer, ctx_seq_len,
                lat_seq_len):
    m_rows = x_ctx.shape[0] + x_lat.shape[0]
    tm = CONV_ROWS
    n_ctx_tiles = x_ctx.shape[0] // tm
    const2 = lambda m: (0, 0)
    return pl.pallas_call(
        functools.partial(_conv_kernel, n_ctx_tiles=n_ctx_tiles, ctx_seq_len=ctx_seq_len,
                          lat_seq_len=lat_seq_len),
        out_shape=jax.ShapeDtypeStruct((m_rows, D_MODEL), F32),
        grid=(m_rows // tm,),
        in_specs=_both_streams_specs(x_ctx, x_lat, tm) + [
                  pl.BlockSpec((None, 1, 3 * D_MODEL),
                               _both_streams_mod_map(layer, n_ctx_tiles, tm, lat_seq_len)),
                  _weight_spec(w_in, j),
                  pl.BlockSpec((None, 3, D_MODEL), lambda m: (j, 0, 0)),
                  _weight_spec(w_out, j),
                  pl.BlockSpec((1, D_MODEL), const2),
                  pl.BlockSpec((1, D_MODEL), const2)],
        out_specs=pl.BlockSpec((tm, D_MODEL), lambda m: (m, 0)),
        compiler_params=_params(1),
        name="conv_layer",
    )(x_ctx, x_lat, mod3, w_in, conv_w, w_out, ln_g.reshape(1, D_MODEL), ln_b.reshape(1, D_MODEL))


def _separate(xp, xs, m_ctx):
    if xp[0] is xs[0]:
        return (xp[0][:m_ctx], 0), (xs[0][m_ctx:], 0)
    return xp, xs


def kernel(x_prompt, x_sample, cache_k, cache_v, state_fwd, state_bwd, c, c_ctx, w_mod, b_mod, ln_g,
           ln_b, w_in_a, lam_a, subln_a, w_out_a, w_in_b, decay_fwd, decay_bwd, w_out_b, w_in_c,
           conv_c, w_out_c):
    bp, tp, d = x_prompt.shape
    bs, ts, _ = x_sample.shape
    assert bs == CTX_ROW < MOD_ROWS
    mod3 = _modulation(c, c_ctx, w_mod, b_mod)
    rope_tables = _rope_tables(ts)

    m_ctx, m_lat = bp * tp, bs * ts
    xp = (x_prompt.reshape(m_ctx, d), 0)
    xs = (x_sample.reshape(m_lat, d), 0)
    n_attn = (DEPTH + N_MIXERS - 1) // N_MIXERS
    new_cache_k = new_cache_v = jax.ShapeDtypeStruct((bp, n_attn, tp, H_A, DV_A), F32)
    new_sf, new_sb = [], []
    for i in range(DEPTH):
        kind, j = i % N_MIXERS, i // N_MIXERS
        if kind == 0:
            x_ctx, new_cache_k, new_cache_v = _ctx_attention_layer(
                xp, m_ctx, tp, mod3, w_in_a, w_out_a, j, lam_a[j], subln_a[j], ln_g[i], ln_b[i], i,
                (new_cache_k, new_cache_v))
            qkvz = _attn_in(xs, m_lat, mod3, w_in_a, j, i, ts, rope_tables)
            xs = (_attention(qkvz, 0, m_lat, xs, ts, (cache_k, cache_v, j), mod3, lam_a[j],
                             subln_a[j], w_out_a, j, ln_g[i], ln_b[i], i, True), 0)
            xp = (x_ctx, 0)
        else:
            xp, xs = _separate(xp, xs, m_ctx)
            if kind == 1:
                q, kt, v, g = _ret_in(xp[0], xs[0], mod3, w_in_b, j, i, ts)
                x_ctx, s_f, s_b = _retention(
                    q, kt, v, g, tp, bp, 0, H_B, CTX_RET_SEQS, decay_fwd[j], decay_bwd[j], None,
                    out_proj=(xp[0], mod3, _mod_row_map(i, None, tp), w_out_b, j, ln_g[i],
                              ln_b[i]))
                new_sf.append(s_f)
                new_sb.append(s_b)
                (x_lat,) = _retention(
                    q, kt, v, g, ts, bs, m_ctx, LAT_RET_HEADS, 1, decay_fwd[j], decay_bwd[j],
                    (state_fwd, state_bwd, j),
                    out_proj=(xs[0], mod3, _mod_row_map(i, ts, ts), w_out_b, j, ln_g[i], ln_b[i]))
                xp, xs = (x_ctx, 0), (x_lat, 0)
            else:
                x_all = _conv_layer(xp[0], xs[0], mod3, w_in_c, conv_c, w_out_c, j, ln_g[i],
                                    ln_b[i], i, tp, ts)
                xp, xs = (x_all, 0), (x_all, m_ctx)
    xp, xs = _separate(xp, xs, m_ctx)
    y_prompt = xp[0].reshape(bp, tp, d)
    y_sample = xs[0].reshape(bs, ts, d)
    new_state_fwd = jnp.concatenate(new_sf, axis=1)
    new_state_bwd = jnp.concatenate(new_sb, axis=1)
    return (y_prompt, y_sample, new_cache_k, new_cache_v, new_state_fwd, new_state_bwd)
```

```python
import functools
import math

import jax
import jax.numpy as jnp
from jax import lax
from jax.experimental import pallas as pl
from jax.experimental.pallas import tpu as pltpu

F32 = jnp.float32
BF16 = jnp.bfloat16

D_MODEL = 1024
DEPTH = 4
N_MIXERS = 3
GRID_W = 64
H_A = 8
DH_A = 64
DV_A = 128
ROPE_HALF = DH_A // 2
SCORE_SCALE = DH_A ** -0.5 * math.log2(math.e)
H_B = 4
DK_B = 256
DV_B = 512
E_B = H_B * DV_B
CHUNK = 256
LAT_RET_HEADS = 2
CTX_RET_SEQS = 2
ALPHA = (2.0 * DEPTH) ** 0.25
ROPE_BASE = 10000.0
LN_EPS = 1e-5
RMS_EPS = 1e-6

MOD_ROWS = 8
CTX_ROW = 4
VMEM_LIMIT_BYTES = 58 * 1024 * 1024
ROW_TILE = 512


def _params(n_axes):
    return pltpu.CompilerParams(dimension_semantics=("arbitrary",) * n_axes,
                                vmem_limit_bytes=VMEM_LIMIT_BYTES)


def _silu(x):
    return x * jax.nn.sigmoid(x)


def _residual_layer_norm(x, out, gate, g, b):
    r = ALPHA * x + gate * out
    mu = jnp.mean(r, axis=-1, keepdims=True)
    d = r - mu
    var = jnp.mean(d * d, axis=-1, keepdims=True)
    return d * lax.rsqrt(var + LN_EPS) * g + b


def _modulated(x, mod_ref):
    shift = mod_ref[:, 0:D_MODEL]
    scale = mod_ref[:, D_MODEL:2 * D_MODEL]
    return x * (1.0 + scale) + shift


def _mod_row_map(layer, rows_per_batch, tile):
    if rows_per_batch is None:
        return lambda m, *_: (layer * MOD_ROWS + CTX_ROW, 0, 0)
    return lambda m, *_: (layer * MOD_ROWS + (m * tile) // rows_per_batch, 0, 0)


def _both_streams_specs(x_ctx, x_lat, tm):
    n_ctx_tiles = x_ctx.shape[0] // tm
    return [pl.BlockSpec((tm, D_MODEL), lambda m: (jnp.minimum(m, n_ctx_tiles - 1), 0)),
            pl.BlockSpec((tm, D_MODEL), lambda m: (jnp.maximum(m - n_ctx_tiles, 0), 0))]


def _both_streams_rows(x_ctx_ref, x_lat_ref, n_ctx_tiles):
    return jnp.where(pl.program_id(0) < n_ctx_tiles, x_ctx_ref[...], x_lat_ref[...])


def _both_streams_mod_map(layer, n_ctx_tiles, tm, lat_rows_per_batch):
    def index_map(m):
        lat_row = ((m - n_ctx_tiles) * tm) // lat_rows_per_batch
        return (layer * MOD_ROWS + jnp.where(m < n_ctx_tiles, CTX_ROW, lat_row), 0, 0)
    return index_map


def _weight_spec(w, j):
    return pl.BlockSpec((None,) + w.shape[1:], lambda *_: (j, 0, 0), pipeline_mode=pl.Buffered(1))


WEIGHT_CHUNK_COLS = 1024
WEIGHT_COPIES_IN_FLIGHT = 2


class _StagedWeight:
    @staticmethod
    def scratch(w, chunk_cols):
        return [pltpu.VMEM(w.shape[1:], w.dtype),
                pltpu.SemaphoreType.DMA((w.shape[2] // chunk_cols,))]

    def __init__(self, w_hbm_ref, w_vmem_ref, sem_ref, j, chunk_cols):
        self.copies = [
            pltpu.make_async_copy(w_hbm_ref.at[j, :, pl.ds(c * chunk_cols, chunk_cols)],
                                  w_vmem_ref.at[:, pl.ds(c * chunk_cols, chunk_cols)], sem_ref.at[c])
            for c in range(w_vmem_ref.shape[1] // chunk_cols)]

    def _arrive(self, c):
        self.copies[c].wait()
        if c + WEIGHT_COPIES_IN_FLIGHT < len(self.copies):
            self.copies[c + WEIGHT_COPIES_IN_FLIGHT].start()

    def run(self, body):
        first = pl.program_id(0) == 0

        @pl.when(first)
        def _():
            for cp in self.copies[:WEIGHT_COPIES_IN_FLIGHT]:
                cp.start()
            body(self._arrive)

        @pl.when(jnp.logical_not(first))
        def _():
            body(lambda c: None)


def _mod_kernel(c_ref, cctx_ref, w_ref, b_ref, o_ref):
    pad = jnp.zeros((MOD_ROWS - c_ref.shape[0] - 1, D_MODEL), F32)
    s = _silu(jnp.concatenate([c_ref[...], cctx_ref[...], pad], axis=0))
    m = jnp.dot(s, w_ref[...], preferred_element_type=F32) + b_ref[...]
    for r in range(MOD_ROWS):
        o_ref[r] = m[r:r + 1]


def _modulation(c, c_ctx, w_mod, b_mod):
    n = 3 * D_MODEL
    return pl.pallas_call(
        _mod_kernel,
        out_shape=jax.ShapeDtypeStruct((DEPTH * MOD_ROWS, 1, n), F32),
        grid=(DEPTH,),
        in_specs=[pl.BlockSpec(c.shape, lambda i: (0, 0)),
                  pl.BlockSpec((1, D_MODEL), lambda i: (0, 0)),
                  pl.BlockSpec((None, D_MODEL, n), lambda i: (i, 0, 0)),
                  pl.BlockSpec((None, 1, n), lambda i: (i, 0, 0))],
        out_specs=pl.BlockSpec((MOD_ROWS, 1, n), lambda i: (i, 0, 0)),
        compiler_params=_params(1),
        name="modulation",
    )(c, c_ctx.reshape(1, D_MODEL), w_mod, b_mod.reshape(DEPTH, 1, n))


def _rope(xh, cos4, sin4, first_half):
    swapped = jnp.where(first_half, pltpu.roll(xh, DV_A - ROPE_HALF, 1),
                        pltpu.roll(xh, ROPE_HALF, 1))
    return xh * cos4 + swapped * sin4


def _store_heads(o_ref, x, seq_len):
    for b in range(o_ref.shape[0]):
        xb = x[b * seq_len:(b + 1) * seq_len].reshape(seq_len, H_A, DV_A)
        if len(o_ref.shape) == 4:
            o_ref[b] = xb
        else:
            o_ref[b, 0] = xb
            for s in range(1, o_ref.shape[1]):
                o_ref[b, s] = jnp.zeros_like(xb)


def _attn_in_body(x_ref, mod_ref, w_ref, rope_refs, q_ref, k_ref, vt_ref, z_ref, cache_refs, seq_len):
    h = _modulated(x_ref[...], mod_ref)
    if rope_refs is not None:
        lane = lax.broadcasted_iota(jnp.int32, (1, DV_A), 1)
        first_half = (lane % DH_A) < ROPE_HALF
        cos4 = rope_refs[0][...]
        sin4 = rope_refs[1][...]
    q_all = jnp.dot(h, w_ref[:, 0:D_MODEL], preferred_element_type=F32)
    k_all = jnp.dot(h, w_ref[:, D_MODEL:2 * D_MODEL], preferred_element_type=F32)
    if cache_refs is not None:
        _store_heads(cache_refs[0], k_all, seq_len)
    for hd in range(H_A):
        cols = slice(hd * DV_A, (hd + 1) * DV_A)
        q = q_all[:, cols]
        k = k_all[:, cols]
        if rope_refs is not None:
            q = _rope(q, cos4, sin4, first_half)
            k = _rope(k, cos4, sin4, first_half)
        q_ref[:, cols] = (q * SCORE_SCALE).astype(BF16)
        k_ref[:, cols] = k.astype(BF16)
    v = jnp.dot(h, w_ref[:, 2 * D_MODEL:3 * D_MODEL], preferred_element_type=F32)
    if cache_refs is not None:
        _store_heads(cache_refs[1], v, seq_len)
    vt_ref[...] = v.T.astype(BF16)
    z_ref[...] = jnp.dot(h, w_ref[:, 3 * D_MODEL:4 * D_MODEL],
                         preferred_element_type=F32).astype(z_ref.dtype)


def _attn_in_kernel(x_ref, mod_ref, w_ref, cos_ref, sin_ref, q_ref, k_ref, vt_ref, z_ref):
    _attn_in_body(x_ref, mod_ref, w_ref, (cos_ref, sin_ref), q_ref, k_ref, vt_ref, z_ref, None, None)


def _attn_in(x, m_rows, mod3, w_in, j, layer, rows_per_batch, rope_tables):
    x2d, x_first_row = x
    tm = ROW_TILE
    seq_tiles = rows_per_batch // tm
    row_spec = pl.BlockSpec((tm, D_MODEL), lambda m: (m, 0))
    tab_spec = pl.BlockSpec((tm, DV_A), lambda m: (m % seq_tiles, 0))
    return pl.pallas_call(
        _attn_in_kernel,
        out_shape=[jax.ShapeDtypeStruct((m_rows, D_MODEL), BF16),
                   jax.ShapeDtypeStruct((m_rows, D_MODEL), BF16),
                   jax.ShapeDtypeStruct((D_MODEL, m_rows), BF16),
                   jax.ShapeDtypeStruct((m_rows, D_MODEL), BF16)],
        grid=(m_rows // tm,),
        in_specs=[pl.BlockSpec((tm, D_MODEL), lambda m: (x_first_row // tm + m, 0)),
                  pl.BlockSpec((None, 1, 3 * D_MODEL), _mod_row_map(layer, rows_per_batch, tm)),
                  _weight_spec(w_in, j), tab_spec, tab_spec],
        out_specs=[row_spec, row_spec, pl.BlockSpec((D_MODEL, tm), lambda m: (0, m)), row_spec],
        compiler_params=_params(1),
        name="attn_in_rope",
    )(x2d, mod3, w_in, *rope_tables)


ONES_ROWS = 16
CTX_ATTN_SEQS = 2
ATTN_ROWS = 512


def _slab_reduce(op, x):
    parts = [x[i:i + 8] for i in range(0, x.shape[0], 8)]
    while len(parts) > 1:
        parts = [op(parts[i], parts[i + 1]) for i in range(0, len(parts) - 1, 2)] + (
            [parts[-1]] if len(parts) % 2 else [])
    return parts[0]


def _attn_kernel(*refs, layer_idx, has_ctx, n_seq, shared_keys):
    if has_ctx:
        (q_ref, k_ref, vt_ref, z_ref, x_ref, kc_ref, vc_ref, mod_ref, lam_ref, subln_ref,
         w_ref, g_ref, b_ref, o_ref, y_ref, kcb_ref, vct_ref) = refs

        @pl.when(pl.program_id(1) == 0)
        def _():
            n_ctx = kc_ref.shape[0]
            kcb_ref[...] = kc_ref[...].reshape(n_ctx, D_MODEL).astype(BF16)
            vct_ref[...] = vc_ref[...].reshape(n_ctx, D_MODEL).T.astype(BF16)
    else:
        (q_ref, k_ref, vt_ref, z_ref, x_ref, mod_ref, lam_ref, subln_ref,
         w_ref, g_ref, b_ref, o_ref, y_ref) = refs
    tq = q_ref.shape[0] // n_seq
    lam_init = 0.8 - 0.6 * math.exp(-0.3 * layer_idx)
    lm = lam_ref[...]
    lam = (jnp.exp(jnp.sum(lm[0:1] * lm[1:2], axis=-1, keepdims=True))
           - jnp.exp(jnp.sum(lm[2:3] * lm[3:4], axis=-1, keepdims=True)) + lam_init)
    lane = lax.broadcasted_iota(jnp.int32, (1, DV_A), 1)
    first = lane < DH_A
    subln = jnp.broadcast_to(subln_ref[...], (DV_A, tq))
    nt = (((1,), (1,)), ((), ()))
    t = k_ref.shape[0] if shared_keys else k_ref.shape[0] // n_seq

    def key_rows(sq):
        return slice(0, t) if shared_keys else slice(sq * t, (sq + 1) * t)

    def scores(unit):
        sq, hd = unit
        cols = slice(hd * DV_A, (hd + 1) * DV_A)
        qh = q_ref[sq * tq:(sq + 1) * tq, cols]
        zero = jnp.zeros_like(qh)
        qq = jnp.concatenate([jnp.where(first, qh, zero), jnp.where(first, zero, qh)], axis=0)
        parts = [lax.dot_general(k_ref[key_rows(sq), cols], qq, nt, preferred_element_type=F32)]
        if has_ctx:
            parts.append(lax.dot_general(kcb_ref[:, cols], qq, nt, preferred_element_type=F32))
        return parts

    def exps(parts):
        m8 = functools.reduce(jnp.maximum, [_slab_reduce(jnp.maximum, s) for s in parts])
        m = jnp.max(m8, axis=0, keepdims=True)
        return [jnp.exp2(s - m).astype(BF16) for s in parts]

    def with_ones(vals_t):
        return jnp.concatenate([vals_t, jnp.ones((ONES_ROWS, vals_t.shape[1]), BF16)], axis=0)

    def finish(unit, es):
        sq, hd = unit
        cols = slice(hd * DV_A, (hd + 1) * DV_A)
        rows = slice(sq * tq, (sq + 1) * tq)
        acc = jnp.dot(with_ones(vt_ref[cols, key_rows(sq)]), es[0], preferred_element_type=F32)
        if has_ctx:
            acc = acc + jnp.dot(with_ones(vct_ref[cols, :]), es[1], preferred_element_type=F32)
        inv = 1.0 / acc[DV_A:DV_A + 1, :]
        ot = acc[:DV_A, :tq] * inv[:, :tq] - acc[:DV_A, tq:] * (inv[:, tq:] * lam)
        ot = ot * lax.rsqrt(jnp.mean(ot * ot, axis=0, keepdims=True) + RMS_EPS)
        ot = ot * subln * (1.0 - lam_init)
        y_ref[rows, cols] = (ot.T * _silu(z_ref[rows, cols].astype(F32))).astype(BF16)

    units = [(sq, hd) for hd in range(H_A) for sq in range(n_seq)]
    s_ahead = {u: scores(units[u]) for u in range(min(2, len(units)))}
    e_ahead = {0: exps(s_ahead.pop(0))}
    for u in range(len(units)):
        if u + 2 < len(units):
            s_ahead[u + 2] = scores(units[u + 2])
        if u + 1 < len(units):
            e_ahead[u + 1] = exps(s_ahead.pop(u + 1))
        finish(units[u], e_ahead.pop(u))
    out = jnp.dot(y_ref[...], w_ref[...].astype(BF16), preferred_element_type=F32)
    gate = mod_ref[:, 2 * D_MODEL:3 * D_MODEL]
    o_ref[...] = _residual_layer_norm(x_ref[...], out, gate, g_ref[...], b_ref[...])


def _attention(qkvz, first_row, m_rows, x, seq_len, ctx, mod3, lam, subln, w_out, j, ln_g, ln_b,
               layer, per_batch_rows):
    q, k, vt, z = qkvz
    t = seq_len
    b = m_rows // t
    has_ctx = ctx is not None
    if t <= 256 and not per_batch_rows and not has_ctx and b % 2 == 0:
        rows, n_seq, shared_keys, n_kseq = 2 * t, 2, False, 2
    else:
        rows, n_seq, shared_keys, n_kseq = ATTN_ROWS, 1, True, 1
    nq = n_kseq * t // rows
    x, x_first_row = x
    r0, k0 = first_row // rows, first_row // (n_kseq * t)
    x_spec = pl.BlockSpec((rows, D_MODEL), lambda i, j: (x_first_row // rows + i * nq + j, 0))
    q_spec = pl.BlockSpec((rows, D_MODEL), lambda i, j: (r0 + i * nq + j, 0))
    o_spec = pl.BlockSpec((rows, D_MODEL), lambda i, j: (i * nq + j, 0))
    k_spec = pl.BlockSpec((n_kseq * t, D_MODEL), lambda i, j: (k0 + i, 0))
    vt_spec = pl.BlockSpec((D_MODEL, n_kseq * t), lambda i, j: (0, k0 + i))
    if per_batch_rows:
        mod_map = lambda i, j: (layer * MOD_ROWS + i, 0, 0)
    else:
        mod_map = lambda i, j: (layer * MOD_ROWS + CTX_ROW, 0, 0)
    const2 = lambda i, j: (0, 0)
    in_specs = [q_spec, k_spec, vt_spec, q_spec, x_spec]
    args = [q, k, vt, z, x]
    scratch = [pltpu.VMEM((rows, D_MODEL), BF16)]
    if has_ctx:
        cache_k, cache_v, jj = ctx
        n_ctx = cache_k.shape[2]
        scratch += [pltpu.VMEM((n_ctx, D_MODEL), BF16), pltpu.VMEM((D_MODEL, n_ctx), BF16)]
        c_spec = pl.BlockSpec((None, None, n_ctx, H_A, DV_A),
                              lambda i, j: (i, jj, 0, 0, 0))
        in_specs += [c_spec, c_spec]
        args += [cache_k, cache_v]
    in_specs += [pl.BlockSpec((None, 1, 3 * D_MODEL), mod_map),
                 pl.BlockSpec((4, DH_A), const2),
                 pl.BlockSpec((DV_A, 1), const2),
                 _weight_spec(w_out, j),
                 pl.BlockSpec((1, D_MODEL), const2),
                 pl.BlockSpec((1, D_MODEL), const2)]
    args += [mod3, lam, subln.reshape(DV_A, 1), w_out, ln_g.reshape(1, D_MODEL),
             ln_b.reshape(1, D_MODEL)]
    return pl.pallas_call(
        functools.partial(_attn_kernel, layer_idx=layer, has_ctx=has_ctx, n_seq=n_seq,
                          shared_keys=shared_keys),
        out_shape=jax.ShapeDtypeStruct((m_rows, D_MODEL), F32),
        grid=(b // n_kseq, nq),
        in_specs=in_specs,
        out_specs=o_spec,
        scratch_shapes=scratch,
        compiler_params=_params(2),
        name="diff_attn_ctx" if has_ctx else "diff_attn",
    )(*args)


def _ctx_attn_layer_kernel(*refs, layer_idx, seq_len, n_aliased):
    x_ref, mod_ref, w_in_ref, lam_ref, subln_ref, w_out_ref, g_ref, b_ref = refs[:8]
    o_ref, ck_ref, cv_ref, q_ref, k_ref, vt_ref, z_ref, y_ref = refs[8 + n_aliased:]
    _attn_in_body(x_ref, mod_ref, w_in_ref, None, q_ref, k_ref, vt_ref, z_ref, (ck_ref, cv_ref),
                  seq_len)
    _attn_kernel(q_ref, k_ref, vt_ref, z_ref, x_ref, mod_ref, lam_ref, subln_ref, w_out_ref, g_ref,
                 b_ref, o_ref, y_ref, layer_idx=layer_idx, has_ctx=False,
                 n_seq=x_ref.shape[0] // seq_len, shared_keys=False)


def _ctx_attention_layer(x, m_rows, seq_len, mod3, w_in, w_out, j, lam, subln, ln_g, ln_b, layer,
                         cache_out):
    x2d, x_first_row = x
    rows = CTX_ATTN_SEQS * seq_len
    const2 = lambda m: (0, 0)
    in_specs = [pl.BlockSpec((rows, D_MODEL), lambda m: (x_first_row // rows + m, 0)),
                pl.BlockSpec((None, 1, 3 * D_MODEL), lambda m: (layer * MOD_ROWS + CTX_ROW, 0, 0)),
                _weight_spec(w_in, j),
                pl.BlockSpec((4, DH_A), const2),
                pl.BlockSpec((DV_A, 1), const2),
                _weight_spec(w_out, j),
                pl.BlockSpec((1, D_MODEL), const2),
                pl.BlockSpec((1, D_MODEL), const2)]
    args = [x2d, mod3, w_in, lam, subln.reshape(DV_A, 1), w_out, ln_g.reshape(1, D_MODEL),
            ln_b.reshape(1, D_MODEL)]
    new_k, new_v = cache_out
    cache_shape = jax.ShapeDtypeStruct(new_k.shape, F32)
    aliases = {}
    if j == 0:
        cache_spec = pl.BlockSpec((CTX_ATTN_SEQS, new_k.shape[1], seq_len, H_A, DV_A),
                                  lambda m: (m, 0, 0, 0, 0))
    else:
        cache_spec = pl.BlockSpec((CTX_ATTN_SEQS, None, seq_len, H_A, DV_A),
                                  lambda m: (m, j, 0, 0, 0))
        in_specs += [pl.BlockSpec(memory_space=pl.ANY)] * 2
        args += [new_k, new_v]
        aliases = {len(args) - 2: 1, len(args) - 1: 2}
    return pl.pallas_call(
        functools.partial(_ctx_attn_layer_kernel, layer_idx=layer, seq_len=seq_len,
                          n_aliased=len(aliases)),
        out_shape=[jax.ShapeDtypeStruct((m_rows, D_MODEL), F32), cache_shape, cache_shape],
        grid=(m_rows // rows,),
        in_specs=in_specs,
        out_specs=[pl.BlockSpec((rows, D_MODEL), lambda m: (m, 0)), cache_spec, cache_spec],
        scratch_shapes=[pltpu.VMEM((rows, D_MODEL), BF16), pltpu.VMEM((rows, D_MODEL), BF16),
                        pltpu.VMEM((D_MODEL, rows), BF16), pltpu.VMEM((rows, D_MODEL), BF16),
                        pltpu.VMEM((rows, D_MODEL), BF16)],
        input_output_aliases=aliases,
        compiler_params=_params(1),
        name="ctx_attn_layer",
    )(*args)


def _rope_tables(n_tokens):
    rows = n_tokens // GRID_W
    r = jnp.repeat(jnp.arange(rows, dtype=F32), GRID_W)
    col = jnp.tile(jnp.arange(GRID_W, dtype=F32), rows)
    n_freq = DH_A // 4
    inv = ROPE_BASE ** (-jnp.arange(n_freq, dtype=F32) / n_freq)
    ang = jnp.concatenate([r[:, None] * inv, col[:, None] * inv], -1)
    cos, sin = jnp.cos(ang), jnp.sin(ang)
    return jnp.tile(cos, (1, 4)), jnp.concatenate([-sin, sin, -sin, sin], -1)


def _ret_in_kernel(xc_ref, xl_ref, mod_ref, w_hbm_ref, q_ref, kt_ref, v_ref, g_ref, w_ref, sem_ref, *,
                   n_ctx_tiles, j):
    nq = H_B * DK_B
    cc = WEIGHT_CHUNK_COLS
    assert nq == cc and E_B % cc == 0
    staged = _StagedWeight(w_hbm_ref, w_ref, sem_ref, j, cc)

    def body(arrive):
        h = _modulated(_both_streams_rows(xc_ref, xl_ref, n_ctx_tiles), mod_ref)
        arrive(0)
        q_ref[...] = jnp.dot(h, w_ref[:, 0:nq], preferred_element_type=F32).astype(BF16)
        arrive(1)
        k = jnp.dot(h, w_ref[:, nq:2 * nq], preferred_element_type=F32) * (DK_B ** -0.5)
        kt_ref[...] = k.T
        for c in range(E_B // cc):
            arrive(2 + c)
            v = jnp.dot(h, w_ref[:, 2 * nq + c * cc:2 * nq + (c + 1) * cc],
                        preferred_element_type=F32)
            v_ref[:, c * cc:(c + 1) * cc] = v.astype(BF16)
        for c in range(E_B // cc):
            arrive(2 + E_B // cc + c)
            g_ref[:, c * cc:(c + 1) * cc] = jnp.dot(
                h, w_ref[:, 2 * nq + E_B + c * cc:2 * nq + E_B + (c + 1) * cc],
                preferred_element_type=F32).astype(BF16)

    staged.run(body)


def _ret_in(x_ctx, x_lat, mod3, w_in, j, layer, lat_rows_per_batch):
    m_rows = x_ctx.shape[0] + x_lat.shape[0]
    tm = ROW_TILE
    nq = H_B * DK_B
    n_ctx_tiles = x_ctx.shape[0] // tm
    return pl.pallas_call(
        functools.partial(_ret_in_kernel, n_ctx_tiles=n_ctx_tiles, j=j),
        out_shape=[jax.ShapeDtypeStruct((m_rows, nq), BF16),
                   jax.ShapeDtypeStruct((nq, m_rows), F32),
                   jax.ShapeDtypeStruct((m_rows, E_B), BF16),
                   jax.ShapeDtypeStruct((m_rows, E_B), BF16)],
        grid=(m_rows // tm,),
        in_specs=_both_streams_specs(x_ctx, x_lat, tm) + [
            pl.BlockSpec((None, 1, 3 * D_MODEL),
                         _both_streams_mod_map(layer, n_ctx_tiles, tm, lat_rows_per_batch)),
            pl.BlockSpec(memory_space=pl.ANY)],
        out_specs=[pl.BlockSpec((tm, nq), lambda m: (m, 0)),
                   pl.BlockSpec((nq, tm), lambda m: (0, m)),
                   pl.BlockSpec((tm, E_B), lambda m: (m, 0)),
                   pl.BlockSpec((tm, E_B), lambda m: (m, 0))],
        scratch_shapes=_StagedWeight.scratch(w_in, WEIGHT_CHUNK_COLS),
        compiler_params=_params(1),
        name="ret_in",
    )(x_ctx, x_lat, mod3, w_in)


def _retention_kernel(*refs, has_state, heads, seqs, fused_out):
    refs = list(refs)
    q_ref, kt_ref, v_ref, g_ref, af_ref, ab_ref = refs[:6]
    del refs[:6]
    if has_state:
        s0f_ref, s0b_ref = refs[:2]
        del refs[:2]
    if fused_out:
        x_ref, mod_ref, w_ref, lng_ref, lnb_ref = refs[:5]
        del refs[:5]
        o_ref = refs.pop(0)
        y_ref = refs.pop()
    else:
        y_ref = refs.pop(0)
    if not has_state:
        sf_ref, sb_ref = refs
    t = q_ref.shape[0] // seqs
    nc = t // CHUNK
    row = lax.broadcasted_iota(jnp.int32, (CHUNK, CHUNK), 0).astype(F32)
    col = lax.broadcasted_iota(jnp.int32, (CHUNK, CHUNK), 1).astype(F32)
    diff = row - col
    idx_col = lax.broadcasted_iota(jnp.int32, (CHUNK, 1), 0).astype(F32)
    idx_row = lax.broadcasted_iota(jnp.int32, (1, CHUNK), 1).astype(F32)

    def state(ref, sq, hh):
        return ref.at[hh] if seqs == 1 else ref.at[sq, hh]

    for hh, sq in [(hh, sq) for hh in range(heads) for sq in range(seqs)]:
        def chunk(c, sq=sq):
            return slice(sq * t + c * CHUNK, sq * t + (c + 1) * CHUNK)

        qk_cols = slice(hh * DK_B, (hh + 1) * DK_B)
        v_cols = slice(hh * DV_B, (hh + 1) * DV_B)
        lg_f = jnp.log1p(-jnp.exp(af_ref[hh]))
        lg_b = jnp.log1p(-jnp.exp(ab_ref[hh]))
        dmask = (jnp.where(diff >= 0, jnp.exp(jnp.maximum(diff, 0.0) * lg_f), 0.0)
                 + jnp.where(diff <= 0, jnp.exp(jnp.maximum(-diff, 0.0) * lg_b), 0.0))
        qd_f = jnp.exp((idx_col + 1.0) * lg_f)
        qd_b = jnp.exp((CHUNK - idx_col) * lg_b)
        kd_f = jnp.exp((CHUNK - 1.0 - idx_row) * lg_f)
        kd_b = jnp.exp(idx_row * lg_b)
        cd_f = jnp.exp(CHUNK * lg_f)
        cd_b = jnp.exp(CHUNK * lg_b)

        def states(order, kd, cd, s):
            seen = {}
            for n, c in enumerate(order):
                seen[c] = None if s is None else s.astype(BF16)
                if has_state and n == nc - 1:
                    return seen, None
                u = jnp.dot((kt_ref[qk_cols, chunk(c)] * kd).astype(BF16), v_ref[chunk(c), v_cols],
                            preferred_element_type=F32)
                s = u if s is None else s * cd + u
            return seen, s

        seen_f, s_f = states(range(nc), kd_f, cd_f,
                             state(s0f_ref, sq, hh)[...] if has_state else None)
        seen_b, s_b = states(range(nc - 1, -1, -1), kd_b, cd_b,
                             state(s0b_ref, sq, hh)[...] if has_state else None)
        if not has_state:
            state(sf_ref, sq, hh)[...] = s_f
            state(sb_ref, sq, hh)[...] = s_b
        for c in range(nc):
            qc = q_ref[chunk(c), qk_cols]
            qk = jnp.dot(qc, kt_ref[qk_cols, chunk(c)].astype(BF16), preferred_element_type=F32)
            o = jnp.dot((qk * dmask).astype(BF16), v_ref[chunk(c), v_cols],
                        preferred_element_type=F32)
            if seen_f[c] is not None:
                o = o + jnp.dot(qc, seen_f[c], preferred_element_type=F32) * qd_f
            if seen_b[c] is not None:
                o = o + jnp.dot(qc, seen_b[c], preferred_element_type=F32) * qd_b
            o = o * lax.rsqrt(jnp.mean(o * o, axis=-1, keepdims=True) + RMS_EPS)
            gate_pre = g_ref[chunk(c), v_cols].astype(F32)
            y_ref[chunk(c), v_cols] = (o * _silu(gate_pre)).astype(BF16)
    if fused_out:
        part = jnp.dot(y_ref[...], w_ref[...].astype(BF16), preferred_element_type=F32)
        gate = mod_ref[:, 2 * D_MODEL:3 * D_MODEL]
        head_steps = H_B // heads
        if head_steps == 1:
            o_ref[...] = _residual_layer_norm(x_ref[...], part, gate, lng_ref[...], lnb_ref[...])
        else:
            step = pl.program_id(1)

            @pl.when(step == 0)
            def _():
                o_ref[...] = part

            @pl.when(jnp.logical_and(step > 0, step < head_steps - 1))
            def _():
                o_ref[...] += part

            @pl.when(step == head_steps - 1)
            def _():
                o_ref[...] = _residual_layer_norm(x_ref[...], o_ref[...] + part, gate, lng_ref[...],
                                                  lnb_ref[...])


def _retention(q, kt, v, g, seq_len, n_seq, first_row, heads, seqs, decay_f, decay_b, states,
               out_proj=None):
    t = seqs * seq_len
    b = n_seq // seqs
    s0 = first_row // t
    has_state = states is not None
    fused_out = out_proj is not None
    seq_dim = None if seqs == 1 else seqs
    q_spec = pl.BlockSpec((t, heads * DK_B), lambda i, h: (s0 + i, h))
    kt_spec = pl.BlockSpec((heads * DK_B, t), lambda i, h: (h, s0 + i))
    vg_spec = pl.BlockSpec((t, heads * DV_B), lambda i, h: (s0 + i, h))
    a_spec = pl.BlockSpec((heads, 1, 1), lambda i, h: (h, 0, 0))
    in_specs = [q_spec, kt_spec, vg_spec, vg_spec, a_spec, a_spec]
    args = [q, kt, v, g, decay_f.reshape(H_B, 1, 1), decay_b.reshape(H_B, 1, 1)]
    out_shape = [jax.ShapeDtypeStruct((b * t, E_B), BF16)]
    out_specs = [pl.BlockSpec((t, heads * DV_B), lambda i, h: (i, h))]
    if has_state:
        s_f, s_b, jj = states
        s_spec = pl.BlockSpec((seq_dim, None, heads, DK_B, DV_B), lambda i, h: (i, jj, h, 0, 0))
        in_specs += [s_spec, s_spec]
        args += [s_f, s_b]
    scratch = []
    if fused_out:
        x2d, mod3, mod_map, w_out, jw, ln_g, ln_b = out_proj
        x_spec = pl.BlockSpec((t, D_MODEL), lambda i, h: (i, 0))
        const2 = lambda i, h: (0, 0)
        if heads == H_B:
            w_spec = _weight_spec(w_out, jw)
        else:
            w_spec = pl.BlockSpec((None, heads * DV_B, D_MODEL), lambda i, h: (jw, h, 0))
        in_specs += [x_spec, pl.BlockSpec((None, 1, 3 * D_MODEL), mod_map), w_spec,
                     pl.BlockSpec((1, D_MODEL), const2), pl.BlockSpec((1, D_MODEL), const2)]
        args += [x2d, mod3, w_out, ln_g.reshape(1, D_MODEL), ln_b.reshape(1, D_MODEL)]
        out_shape = [jax.ShapeDtypeStruct((b * t, D_MODEL), F32)]
        out_specs = [x_spec]
        scratch = [pltpu.VMEM((t, heads * DV_B), BF16)]
    if not has_state:
        so_spec = pl.BlockSpec((seq_dim, None, heads, DK_B, DV_B), lambda i, h: (i, 0, h, 0, 0))
        out_shape += [jax.ShapeDtypeStruct((n_seq, 1, H_B, DK_B, DV_B), F32)] * 2
        out_specs += [so_spec, so_spec]
    return pl.pallas_call(
        functools.partial(_retention_kernel, has_state=has_state, heads=heads, seqs=seqs,
                          fused_out=fused_out),
        out_shape=out_shape,
        grid=(b, H_B // heads),
        in_specs=in_specs,
        out_specs=out_specs,
        scratch_shapes=scratch,
        compiler_params=_params(2),
        name="retention_state" if has_state else "retention",
    )(*args)


CONV_ROWS = 1024
CONV_COLS = 256


def _conv_kernel(xc_ref, xl_ref, mod_ref, w_in_ref, cw_ref, w_out_ref, g_ref, b_ref, o_ref, *,
                 n_ctx_tiles, ctx_seq_len, lat_seq_len):
    x = _both_streams_rows(xc_ref, xl_ref, n_ctx_tiles)
    h = _modulated(x, mod_ref)
    rows = x.shape[0]
    is_ctx = pl.program_id(0) < n_ctx_tiles
    row = lax.broadcasted_iota(jnp.int32, (rows, 1), 0)
    pos = jnp.where(is_ctx, row % ctx_seq_len, row % lat_seq_len)
    has_prev = pos > 0
    has_next = pos < jnp.where(is_ctx, ctx_seq_len - 1, lat_seq_len - 1)
    e = D_MODEL
    for c in range(e // CONV_COLS):
        cols = slice(c * CONV_COLS, (c + 1) * CONV_COLS)

        def proj(part):
            lo = part * e + c * CONV_COLS
            return jnp.dot(h, w_in_ref[:, lo:lo + CONV_COLS], preferred_element_type=F32)

        p = proj(1) * proj(2)
        prev = jnp.where(has_prev, pltpu.roll(p, 1, 0), 0.0)
        nxt = jnp.where(has_next, pltpu.roll(p, rows - 1, 0), 0.0)
        conv = prev * cw_ref[0:1, cols] + p * cw_ref[1:2, cols] + nxt * cw_ref[2:3, cols]
        y = (proj(0) * conv * _silu(proj(3))).astype(BF16)
        part = jnp.dot(y, w_out_ref[cols, :].astype(BF16), preferred_element_type=F32)
        if c == 0:
            o_ref[...] = part
        else:
            o_ref[...] += part
    gate = mod_ref[:, 2 * D_MODEL:3 * D_MODEL]
    o_ref[...] = _residual_layer_norm(x, o_ref[...], gate, g_ref[...], b_ref[...])


def _conv_layer(x_ctx, x_lat, mod3, w_in, conv_w, w_out, j, ln_g, ln_b, layer, ctx_seq_len,
                lat_seq_len):
    m_rows = x_ctx.shape[0] + x_lat.shape[0]
    tm = CONV_ROWS
    n_ctx_tiles = x_ctx.shape[0] // tm
    const2 = lambda m: (0, 0)
    return pl.pallas_call(
        functools.partial(_conv_kernel, n_ctx_tiles=n_ctx_tiles, ctx_seq_len=ctx_seq_len,
                          lat_seq_len=lat_seq_len),
        out_shape=jax.ShapeDtypeStruct((m_rows, D_MODEL), F32),
        grid=(m_rows // tm,),
        in_specs=_both_streams_specs(x_ctx, x_lat, tm) + [
                  pl.BlockSpec((None, 1, 3 * D_MODEL),
                               _both_streams_mod_map(layer, n_ctx_tiles, tm, lat_seq_len)),
                  _weight_spec(w_in, j),
                  pl.BlockSpec((None, 3, D_MODEL), lambda m: (j, 0, 0)),
                  _weight_spec(w_out, j),
                  pl.BlockSpec((1, D_MODEL), const2),
                  pl.BlockSpec((1, D_MODEL), const2)],
        out_specs=pl.BlockSpec((tm, D_MODEL), lambda m: (m, 0)),
        compiler_params=_params(1),
        name="conv_layer",
    )(x_ctx, x_lat, mod3, w_in, conv_w, w_out, ln_g.reshape(1, D_MODEL), ln_b.reshape(1, D_MODEL))


def _separate(xp, xs, m_ctx):
    if xp[0] is xs[0]:
        return (xp[0][:m_ctx], 0), (xs[0][m_ctx:], 0)
    return xp, xs


def kernel(x_prompt, x_sample, cache_k, cache_v, state_fwd, state_bwd, c, c_ctx, w_mod, b_mod, ln_g,
           ln_b, w_in_a, lam_a, subln_a, w_out_a, w_in_b, decay_fwd, decay_bwd, w_out_b, w_in_c,
           conv_c, w_out_c):
    bp, tp, d = x_prompt.shape
    bs, ts, _ = x_sample.shape
    assert bs == CTX_ROW < MOD_ROWS
    mod3 = _modulation(c, c_ctx, w_mod, b_mod)
    rope_tables = _rope_tables(ts)

    m_ctx, m_lat = bp * tp, bs * ts
    xp = (x_prompt.reshape(m_ctx, d), 0)
    xs = (x_sample.reshape(m_lat, d), 0)
    n_attn = (DEPTH + N_MIXERS - 1) // N_MIXERS
    new_cache_k = new_cache_v = jax.ShapeDtypeStruct((bp, n_attn, tp, H_A, DV_A), F32)
    new_sf, new_sb = [], []
    for i in range(DEPTH):
        kind, j = i % N_MIXERS, i // N_MIXERS
        if kind == 0:
            x_ctx, new_cache_k, new_cache_v = _ctx_attention_layer(
                xp, m_ctx, tp, mod3, w_in_a, w_out_a, j, lam_a[j], subln_a[j], ln_g[i], ln_b[i], i,
                (new_cache_k, new_cache_v))
            qkvz = _attn_in(xs, m_lat, mod3, w_in_a, j, i, ts, rope_tables)
            xs = (_attention(qkvz, 0, m_lat, xs, ts, (cache_k, cache_v, j), mod3, lam_a[j],
                             subln_a[j], w_out_a, j, ln_g[i], ln_b[i], i, True), 0)
            xp = (x_ctx, 0)
        else:
            xp, xs = _separate(xp, xs, m_ctx)
            if kind == 1:
                q, kt, v, g = _ret_in(xp[0], xs[0], mod3, w_in_b, j, i, ts)
                x_ctx, s_f, s_b = _retention(
                    q, kt, v, g, tp, bp, 0, H_B, CTX_RET_SEQS, decay_fwd[j], decay_bwd[j], None,
                    out_proj=(xp[0], mod3, _mod_row_map(i, None, tp), w_out_b, j, ln_g[i],
                              ln_b[i]))
                new_sf.append(s_f)
                new_sb.append(s_b)
                (x_lat,) = _retention(
                    q, kt, v, g, ts, bs, m_ctx, LAT_RET_HEADS, 1, decay_fwd[j], decay_bwd[j],
                    (state_fwd, state_bwd, j),
                    out_proj=(xs[0], mod3, _mod_row_map(i, ts, ts), w_out_b, j, ln_g[i], ln_b[i]))
                xp, xs = (x_ctx, 0), (x_lat, 0)
            else:
                x_all = _conv_layer(xp[0], xs[0], mod3, w_in_c, conv_c, w_out_c, j, ln_g[i],
                                    ln_b[i], i, tp, ts)
                xp, xs = (x_all, 0), (x_all, m_ctx)
    xp, xs = _separate(xp, xs, m_ctx)
    y_prompt = xp[0].reshape(bp, tp, d)
    y_sample = xs[0].reshape(bs, ts, d)
    new_state_fwd = jnp.concatenate(new_sf, axis=1)
    new_state_bwd = jnp.concatenate(new_sb, axis=1)
    return (y_prompt, y_sample, new_cache_k, new_cache_v, new_state_fwd, new_state_bwd)
```

```python
import functools
import math

import jax
import jax.numpy as jnp
from jax import lax
from jax.experimental import pallas as pl
from jax.experimental.pallas import tpu as pltpu

F32 = jnp.float32
BF16 = jnp.bfloat16

D_MODEL = 1024
DEPTH = 4
N_MIXERS = 3
GRID_W = 64
H_A = 8
DH_A = 64
DV_A = 128
ROPE_HALF = DH_A // 2
SCORE_SCALE = DH_A ** -0.5 * math.log2(math.e)
H_B = 4
DK_B = 256
DV_B = 512
E_B = H_B * DV_B
CHUNK = 256
LAT_RET_HEADS = 2
CTX_RET_SEQS = 2
ALPHA = (2.0 * DEPTH) ** 0.25
ROPE_BASE = 10000.0
LN_EPS = 1e-5
RMS_EPS = 1e-6

MOD_ROWS = 8
CTX_ROW = 4
VMEM_LIMIT_BYTES = 58 * 1024 * 1024
ROW_TILE = 512


def _params(n_axes):
    return pltpu.CompilerParams(dimension_semantics=("arbitrary",) * n_axes,
                                vmem_limit_bytes=VMEM_LIMIT_BYTES)


def _silu(x):
    return x * jax.nn.sigmoid(x)


def _residual_layer_norm(x, out, gate, g, b):
    r = ALPHA * x + gate * out
    mu = jnp.mean(r, axis=-1, keepdims=True)
    d = r - mu
    var = jnp.mean(d * d, axis=-1, keepdims=True)
    return d * lax.rsqrt(var + LN_EPS) * g + b


def _modulated(x, mod_ref):
    shift = mod_ref[:, 0:D_MODEL]
    scale = mod_ref[:, D_MODEL:2 * D_MODEL]
    return x * (1.0 + scale) + shift


def _mod_row_map(layer, rows_per_batch, tile):
    if rows_per_batch is None:
        return lambda m, *_: (layer * MOD_ROWS + CTX_ROW, 0, 0)
    return lambda m, *_: (layer * MOD_ROWS + (m * tile) // rows_per_batch, 0, 0)


def _both_streams_specs(x_ctx, x_lat, tm):
    n_ctx_tiles = x_ctx.shape[0] // tm
    return [pl.BlockSpec((tm, D_MODEL), lambda m: (jnp.minimum(m, n_ctx_tiles - 1), 0)),
            pl.BlockSpec((tm, D_MODEL), lambda m: (jnp.maximum(m - n_ctx_tiles, 0), 0))]


def _both_streams_rows(x_ctx_ref, x_lat_ref, n_ctx_tiles):
    return jnp.where(pl.program_id(0) < n_ctx_tiles, x_ctx_ref[...], x_lat_ref[...])


def _both_streams_mod_map(layer, n_ctx_tiles, tm, lat_rows_per_batch):
    def index_map(m):
        lat_row = ((m - n_ctx_tiles) * tm) // lat_rows_per_batch
        return (layer * MOD_ROWS + jnp.where(m < n_ctx_tiles, CTX_ROW, lat_row), 0, 0)
    return index_map


def _weight_spec(w, j):
    return pl.BlockSpec((None,) + w.shape[1:], lambda *_: (j, 0, 0), pipeline_mode=pl.Buffered(1))


WEIGHT_CHUNK_COLS = 1024
WEIGHT_COPIES_IN_FLIGHT = 2


class _StagedWeight:
    @staticmethod
    def scratch(w, chunk_cols):
        return [pltpu.VMEM(w.shape[1:], w.dtype),
                pltpu.SemaphoreType.DMA((w.shape[2] // chunk_cols,))]

    def __init__(self, w_hbm_ref, w_vmem_ref, sem_ref, j, chunk_cols):
        self.copies = [
            pltpu.make_async_copy(w_hbm_ref.at[j, :, pl.ds(c * chunk_cols, chunk_cols)],
                                  w_vmem_ref.at[:, pl.ds(c * chunk_cols, chunk_cols)], sem_ref.at[c])
            for c in range(w_vmem_ref.shape[1] // chunk_cols)]

    def _arrive(self, c):
        self.copies[c].wait()
        if c + WEIGHT_COPIES_IN_FLIGHT < len(self.copies):
            self.copies[c + WEIGHT_COPIES_IN_FLIGHT].start()

    def run(self, body):
        first = pl.program_id(0) == 0

        @pl.when(first)
        def _():
            for cp in self.copies[:WEIGHT_COPIES_IN_FLIGHT]:
                cp.start()
            body(self._arrive)

        @pl.when(jnp.logical_not(first))
        def _():
            body(lambda c: None)


def _mod_kernel(c_ref, cctx_ref, w_ref, b_ref, o_ref):
    pad = jnp.zeros((MOD_ROWS - c_ref.shape[0] - 1, D_MODEL), F32)
    s = _silu(jnp.concatenate([c_ref[...], cctx_ref[...], pad], axis=0))
    m = jnp.dot(s, w_ref[...], preferred_element_type=F32) + b_ref[...]
    for r in range(MOD_ROWS):
        o_ref[r] = m[r:r + 1]


def _modulation(c, c_ctx, w_mod, b_mod):
    n = 3 * D_MODEL
    return pl.pallas_call(
        _mod_kernel,
        out_shape=jax.ShapeDtypeStruct((DEPTH * MOD_ROWS, 1, n), F32),
        grid=(DEPTH,),
        in_specs=[pl.BlockSpec(c.shape, lambda i: (0, 0)),
                  pl.BlockSpec((1, D_MODEL), lambda i: (0, 0)),
                  pl.BlockSpec((None, D_MODEL, n), lambda i: (i, 0, 0)),
                  pl.BlockSpec((None, 1, n), lambda i: (i, 0, 0))],
        out_specs=pl.BlockSpec((MOD_ROWS, 1, n), lambda i: (i, 0, 0)),
        compiler_params=_params(1),
        name="modulation",
    )(c, c_ctx.reshape(1, D_MODEL), w_mod, b_mod.reshape(DEPTH, 1, n))


def _rope(xh, cos4, sin4, first_half):
    swapped = jnp.where(first_half, pltpu.roll(xh, DV_A - ROPE_HALF, 1),
                        pltpu.roll(xh, ROPE_HALF, 1))
    return xh * cos4 + swapped * sin4


def _store_heads(o_ref, x, seq_len):
    for b in range(o_ref.shape[0]):
        xb = x[b * seq_len:(b + 1) * seq_len].reshape(seq_len, H_A, DV_A)
        if len(o_ref.shape) == 4:
            o_ref[b] = xb
        else:
            o_ref[b, 0] = xb
            for s in range(1, o_ref.shape[1]):
                o_ref[b, s] = jnp.zeros_like(xb)


def _attn_in_body(x_ref, mod_ref, w_ref, rope_refs, q_ref, k_ref, vt_ref, z_ref, cache_refs, seq_len):
    h = _modulated(x_ref[...], mod_ref)
    if rope_refs is not None:
        lane = lax.broadcasted_iota(jnp.int32, (1, DV_A), 1)
        first_half = (lane % DH_A) < ROPE_HALF
        cos4 = rope_refs[0][...]
        sin4 = rope_refs[1][...]
    q_all = jnp.dot(h, w_ref[:, 0:D_MODEL], preferred_element_type=F32)
    k_all = jnp.dot(h, w_ref[:, D_MODEL:2 * D_MODEL], preferred_element_type=F32)
    if cache_refs is not None:
        _store_heads(cache_refs[0], k_all, seq_len)
    for hd in range(H_A):
        cols = slice(hd * DV_A, (hd + 1) * DV_A)
        q = q_all[:, cols]
        k = k_all[:, cols]
        if rope_refs is not None:
            q = _rope(q, cos4, sin4, first_half)
            k = _rope(k, cos4, sin4, first_half)
        q_ref[:, cols] = (q * SCORE_SCALE).astype(BF16)
        k_ref[:, cols] = k.astype(BF16)
    v = jnp.dot(h, w_ref[:, 2 * D_MODEL:3 * D_MODEL], preferred_element_type=F32)
    if cache_refs is not None:
        _store_heads(cache_refs[1], v, seq_len)
    vt_ref[...] = v.T.astype(BF16)
    z_ref[...] = jnp.dot(h, w_ref[:, 3 * D_MODEL:4 * D_MODEL],
                         preferred_element_type=F32).astype(z_ref.dtype)


def _attn_in_kernel(x_ref, mod_ref, w_ref, cos_ref, sin_ref, q_ref, k_ref, vt_ref, z_ref):
    _attn_in_body(x_ref, mod_ref, w_ref, (cos_ref, sin_ref), q_ref, k_ref, vt_ref, z_ref, None, None)


def _attn_in(x, m_rows, mod3, w_in, j, layer, rows_per_batch, rope_tables):
    x2d, x_first_row = x
    tm = ROW_TILE
    seq_tiles = rows_per_batch // tm
    row_spec = pl.BlockSpec((tm, D_MODEL), lambda m: (m, 0))
    tab_spec = pl.BlockSpec((tm, DV_A), lambda m: (m % seq_tiles, 0))
    return pl.pallas_call(
        _attn_in_kernel,
        out_shape=[jax.ShapeDtypeStruct((m_rows, D_MODEL), BF16),
                   jax.ShapeDtypeStruct((m_rows, D_MODEL), BF16),
                   jax.ShapeDtypeStruct((D_MODEL, m_rows), BF16),
                   jax.ShapeDtypeStruct((m_rows, D_MODEL), BF16)],
        grid=(m_rows // tm,),
        in_specs=[pl.BlockSpec((tm, D_MODEL), lambda m: (x_first_row // tm + m, 0)),
                  pl.BlockSpec((None, 1, 3 * D_MODEL), _mod_row_map(layer, rows_per_batch, tm)),
                  _weight_spec(w_in, j), tab_spec, tab_spec],
        out_specs=[row_spec, row_spec, pl.BlockSpec((D_MODEL, tm), lambda m: (0, m)), row_spec],
        compiler_params=_params(1),
        name="attn_in_rope",
    )(x2d, mod3, w_in, *rope_tables)


ONES_ROWS = 16
CTX_ATTN_SEQS = 2
ATTN_ROWS = 512


def _slab_reduce(op, x):
    parts = [x[i:i + 8] for i in range(0, x.shape[0], 8)]
    while len(parts) > 1:
        parts = [op(parts[i], parts[i + 1]) for i in range(0, len(parts) - 1, 2)] + (
            [parts[-1]] if len(parts) % 2 else [])
    return parts[0]


def _attn_kernel(*refs, layer_idx, has_ctx, n_seq, shared_keys):
    if has_ctx:
        (q_ref, k_ref, vt_ref, z_ref, x_ref, kc_ref, vc_ref, mod_ref, lam_ref, subln_ref,
         w_ref, g_ref, b_ref, o_ref, y_ref, kcb_ref, vct_ref) = refs

        @pl.when(pl.program_id(1) == 0)
        def _():
            n_ctx = kc_ref.shape[0]
            kcb_ref[...] = kc_ref[...].reshape(n_ctx, D_MODEL).astype(BF16)
            vct_ref[...] = vc_ref[...].reshape(n_ctx, D_MODEL).T.astype(BF16)
    else:
        (q_ref, k_ref, vt_ref, z_ref, x_ref, mod_ref, lam_ref, subln_ref,
         w_ref, g_ref, b_ref, o_ref, y_ref) = refs
    tq = q_ref.shape[0] // n_seq
    lam_init = 0.8 - 0.6 * math.exp(-0.3 * layer_idx)
    lm = lam_ref[...]
    lam = (jnp.exp(jnp.sum(lm[0:1] * lm[1:2], axis=-1, keepdims=True))
           - jnp.exp(jnp.sum(lm[2:3] * lm[3:4], axis=-1, keepdims=True)) + lam_init)
    lane = lax.broadcasted_iota(jnp.int32, (1, DV_A), 1)
    first = lane < DH_A
    subln = jnp.broadcast_to(subln_ref[...], (DV_A, tq))
    nt = (((1,), (1,)), ((), ()))
    t = k_ref.shape[0] if shared_keys else k_ref.shape[0] // n_seq

    def key_rows(sq):
        return slice(0, t) if shared_keys else slice(sq * t, (sq + 1) * t)

    def scores(unit):
        sq, hd = unit
        cols = slice(hd * DV_A, (hd + 1) * DV_A)
        qh = q_ref[sq * tq:(sq + 1) * tq, cols]
        zero = jnp.zeros_like(qh)
        qq = jnp.concatenate([jnp.where(first, qh, zero), jnp.where(first, zero, qh)], axis=0)
        parts = [lax.dot_general(k_ref[key_rows(sq), cols], qq, nt, preferred_element_type=F32)]
        if has_ctx:
            parts.append(lax.dot_general(kcb_ref[:, cols], qq, nt, preferred_element_type=F32))
        return parts

    def exps(parts):
        m8 = functools.reduce(jnp.maximum, [_slab_reduce(jnp.maximum, s) for s in parts])
        m = jnp.max(m8, axis=0, keepdims=True)
        return [jnp.exp2(s - m).astype(BF16) for s in parts]

    def with_ones(vals_t):
        return jnp.concatenate([vals_t, jnp.ones((ONES_ROWS, vals_t.shape[1]), BF16)], axis=0)

    def finish(unit, es):
        sq, hd = unit
        cols = slice(hd * DV_A, (hd + 1) * DV_A)
        rows = slice(sq * tq, (sq + 1) * tq)
        acc = jnp.dot(with_ones(vt_ref[cols, key_rows(sq)]), es[0], preferred_element_type=F32)
        if has_ctx:
            acc = acc + jnp.dot(with_ones(vct_ref[cols, :]), es[1], preferred_element_type=F32)
        inv = 1.0 / acc[DV_A:DV_A + 1, :]
        ot = acc[:DV_A, :tq] * inv[:, :tq] - acc[:DV_A, tq:] * (inv[:, tq:] * lam)
        ot = ot * lax.rsqrt(jnp.mean(ot * ot, axis=0, keepdims=True) + RMS_EPS)
        ot = ot * subln * (1.0 - lam_init)
        y_ref[rows, cols] = (ot.T * _silu(z_ref[rows, cols].astype(F32))).astype(BF16)

    units = [(sq, hd) for hd in range(H_A) for sq in range(n_seq)]
    s_ahead = {u: scores(units[u]) for u in range(min(2, len(units)))}
    e_ahead = {0: exps(s_ahead.pop(0))}
    for u in range(len(units)):
        if u + 2 < len(units):
            s_ahead[u + 2] = scores(units[u + 2])
        if u + 1 < len(units):
            e_ahead[u + 1] = exps(s_ahead.pop(u + 1))
        finish(units[u], e_ahead.pop(u))
    out = jnp.dot(y_ref[...], w_ref[...].astype(BF16), preferred_element_type=F32)
    gate = mod_ref[:, 2 * D_MODEL:3 * D_MODEL]
    o_ref[...] = _residual_layer_norm(x_ref[...], out, gate, g_ref[...], b_ref[...])


def _latent_attention(qkvz, x, seq_len, ctx, mod3, lam, subln, w_out, j, ln_g, ln_b, layer):
    q, k, vt, z = qkvz
    m_rows = q.shape[0]
    t = seq_len
    rows = ATTN_ROWS
    nq = t // rows
    x, x_first_row = x
    x_spec = pl.BlockSpec((rows, D_MODEL), lambda i, j: (x_first_row // rows + i * nq + j, 0))
    q_spec = pl.BlockSpec((rows, D_MODEL), lambda i, j: (i * nq + j, 0))
    k_spec = pl.BlockSpec((t, D_MODEL), lambda i, j: (i, 0))
    vt_spec = pl.BlockSpec((D_MODEL, t), lambda i, j: (0, i))
    cache_k, cache_v, slot = ctx
    n_ctx = cache_k.shape[2]
    c_spec = pl.BlockSpec((None, None, n_ctx, H_A, DV_A), lambda i, j: (i, slot, 0, 0, 0))
    const2 = lambda i, j: (0, 0)
    return pl.pallas_call(
        functools.partial(_attn_kernel, layer_idx=layer, has_ctx=True, n_seq=1, shared_keys=True),
        out_shape=jax.ShapeDtypeStruct((m_rows, D_MODEL), F32),
        grid=(m_rows // t, nq),
        in_specs=[q_spec, k_spec, vt_spec, q_spec, x_spec, c_spec, c_spec,
                  pl.BlockSpec((None, 1, 3 * D_MODEL), lambda i, j: (layer * MOD_ROWS + i, 0, 0)),
                  pl.BlockSpec((4, DH_A), const2),
                  pl.BlockSpec((DV_A, 1), const2),
                  _weight_spec(w_out, j),
                  pl.BlockSpec((1, D_MODEL), const2),
                  pl.BlockSpec((1, D_MODEL), const2)],
        out_specs=q_spec,
        scratch_shapes=[pltpu.VMEM((rows, D_MODEL), BF16), pltpu.VMEM((n_ctx, D_MODEL), BF16),
                        pltpu.VMEM((D_MODEL, n_ctx), BF16)],
        compiler_params=_params(2),
        name="diff_attn_ctx",
    )(q, k, vt, z, x, cache_k, cache_v, mod3, lam, subln.reshape(DV_A, 1), w_out,
      ln_g.reshape(1, D_MODEL), ln_b.reshape(1, D_MODEL))


def _ctx_attn_layer_kernel(*refs, layer_idx, seq_len, n_aliased):
    x_ref, mod_ref, w_in_ref, lam_ref, subln_ref, w_out_ref, g_ref, b_ref = refs[:8]
    o_ref, ck_ref, cv_ref, q_ref, k_ref, vt_ref, z_ref, y_ref = refs[8 + n_aliased:]
    _attn_in_body(x_ref, mod_ref, w_in_ref, None, q_ref, k_ref, vt_ref, z_ref, (ck_ref, cv_ref),
                  seq_len)
    _attn_kernel(q_ref, k_ref, vt_ref, z_ref, x_ref, mod_ref, lam_ref, subln_ref, w_out_ref, g_ref,
                 b_ref, o_ref, y_ref, layer_idx=layer_idx, has_ctx=False,
                 n_seq=x_ref.shape[0] // seq_len, shared_keys=False)


def _ctx_attention_layer(x, m_rows, seq_len, mod3, w_in, w_out, j, lam, subln, ln_g, ln_b, layer,
                         cache_out):
    x2d, x_first_row = x
    rows = CTX_ATTN_SEQS * seq_len
    const2 = lambda m: (0, 0)
    in_specs = [pl.BlockSpec((rows, D_MODEL), lambda m: (x_first_row // rows + m, 0)),
                pl.BlockSpec((None, 1, 3 * D_MODEL), lambda m: (layer * MOD_ROWS + CTX_ROW, 0, 0)),
                _weight_spec(w_in, j),
                pl.BlockSpec((4, DH_A), const2),
                pl.BlockSpec((DV_A, 1), const2),
                _weight_spec(w_out, j),
                pl.BlockSpec((1, D_MODEL), const2),
                pl.BlockSpec((1, D_MODEL), const2)]
    args = [x2d, mod3, w_in, lam, subln.reshape(DV_A, 1), w_out, ln_g.reshape(1, D_MODEL),
            ln_b.reshape(1, D_MODEL)]
    new_k, new_v = cache_out
    cache_shape = jax.ShapeDtypeStruct(new_k.shape, F32)
    aliases = {}
    if j == 0:
        cache_spec = pl.BlockSpec((CTX_ATTN_SEQS, new_k.shape[1], seq_len, H_A, DV_A),
                                  lambda m: (m, 0, 0, 0, 0))
    else:
        cache_spec = pl.BlockSpec((CTX_ATTN_SEQS, None, seq_len, H_A, DV_A),
                                  lambda m: (m, j, 0, 0, 0))
        in_specs += [pl.BlockSpec(memory_space=pl.ANY)] * 2
        args += [new_k, new_v]
        aliases = {len(args) - 2: 1, len(args) - 1: 2}
    return pl.pallas_call(
        functools.partial(_ctx_attn_layer_kernel, layer_idx=layer, seq_len=seq_len,
                          n_aliased=len(aliases)),
        out_shape=[jax.ShapeDtypeStruct((m_rows, D_MODEL), F32), cache_shape, cache_shape],
        grid=(m_rows // rows,),
        in_specs=in_specs,
        out_specs=[pl.BlockSpec((rows, D_MODEL), lambda m: (m, 0)), cache_spec, cache_spec],
        scratch_shapes=[pltpu.VMEM((rows, D_MODEL), BF16), pltpu.VMEM((rows, D_MODEL), BF16),
                        pltpu.VMEM((D_MODEL, rows), BF16), pltpu.VMEM((rows, D_MODEL), BF16),
                        pltpu.VMEM((rows, D_MODEL), BF16)],
        input_output_aliases=aliases,
        compiler_params=_params(1),
        name="ctx_attn_layer",
    )(*args)


def _rope_tables(n_tokens):
    rows = n_tokens // GRID_W
    r = jnp.repeat(jnp.arange(rows, dtype=F32), GRID_W)
    col = jnp.tile(jnp.arange(GRID_W, dtype=F32), rows)
    n_freq = DH_A // 4
    inv = ROPE_BASE ** (-jnp.arange(n_freq, dtype=F32) / n_freq)
    ang = jnp.concatenate([r[:, None] * inv, col[:, None] * inv], -1)
    cos, sin = jnp.cos(ang), jnp.sin(ang)
    return jnp.tile(cos, (1, 4)), jnp.concatenate([-sin, sin, -sin, sin], -1)


def _ret_in_kernel(xc_ref, xl_ref, mod_ref, w_hbm_ref, q_ref, kt_ref, v_ref, g_ref, w_ref, sem_ref, *,
                   n_ctx_tiles, j):
    nq = H_B * DK_B
    cc = WEIGHT_CHUNK_COLS
    assert nq == cc and E_B % cc == 0
    staged = _StagedWeight(w_hbm_ref, w_ref, sem_ref, j, cc)

    def body(arrive):
        h = _modulated(_both_streams_rows(xc_ref, xl_ref, n_ctx_tiles), mod_ref)
        arrive(0)
        q_ref[...] = jnp.dot(h, w_ref[:, 0:nq], preferred_element_type=F32).astype(BF16)
        arrive(1)
        k = jnp.dot(h, w_ref[:, nq:2 * nq], preferred_element_type=F32) * (DK_B ** -0.5)
        kt_ref[...] = k.T
        for c in range(E_B // cc):
            arrive(2 + c)
            v = jnp.dot(h, w_ref[:, 2 * nq + c * cc:2 * nq + (c + 1) * cc],
                        preferred_element_type=F32)
            v_ref[:, c * cc:(c + 1) * cc] = v.astype(BF16)
        for c in range(E_B // cc):
            arrive(2 + E_B // cc + c)
            g_ref[:, c * cc:(c + 1) * cc] = jnp.dot(
                h, w_ref[:, 2 * nq + E_B + c * cc:2 * nq + E_B + (c + 1) * cc],
                preferred_element_type=F32).astype(BF16)

    staged.run(body)


def _ret_in(x_ctx, x_lat, mod3, w_in, j, layer, lat_rows_per_batch):
    m_rows = x_ctx.shape[0] + x_lat.shape[0]
    tm = ROW_TILE
    nq = H_B * DK_B
    n_ctx_tiles = x_ctx.shape[0] // tm
    return pl.pallas_call(
        functools.partial(_ret_in_kernel, n_ctx_tiles=n_ctx_tiles, j=j),
        out_shape=[jax.ShapeDtypeStruct((m_rows, nq), BF16),
                   jax.ShapeDtypeStruct((nq, m_rows), F32),
                   jax.ShapeDtypeStruct((m_rows, E_B), BF16),
                   jax.ShapeDtypeStruct((m_rows, E_B), BF16)],
        grid=(m_rows // tm,),
        in_specs=_both_streams_specs(x_ctx, x_lat, tm) + [
            pl.BlockSpec((None, 1, 3 * D_MODEL),
                         _both_streams_mod_map(layer, n_ctx_tiles, tm, lat_rows_per_batch)),
            pl.BlockSpec(memory_space=pl.ANY)],
        out_specs=[pl.BlockSpec((tm, nq), lambda m: (m, 0)),
                   pl.BlockSpec((nq, tm), lambda m: (0, m)),
                   pl.BlockSpec((tm, E_B), lambda m: (m, 0)),
                   pl.BlockSpec((tm, E_B), lambda m: (m, 0))],
        scratch_shapes=_StagedWeight.scratch(w_in, WEIGHT_CHUNK_COLS),
        compiler_params=_params(1),
        name="ret_in",
    )(x_ctx, x_lat, mod3, w_in)


def _retention_kernel(*refs, has_state, heads, seqs, fused_out):
    refs = list(refs)
    q_ref, kt_ref, v_ref, g_ref, af_ref, ab_ref = refs[:6]
    del refs[:6]
    if has_state:
        s0f_ref, s0b_ref = refs[:2]
        del refs[:2]
    if fused_out:
        x_ref, mod_ref, w_ref, lng_ref, lnb_ref = refs[:5]
        del refs[:5]
        o_ref = refs.pop(0)
        y_ref = refs.pop()
    else:
        y_ref = refs.pop(0)
    if not has_state:
        sf_ref, sb_ref = refs
    t = q_ref.shape[0] // seqs
    nc = t // CHUNK
    row = lax.broadcasted_iota(jnp.int32, (CHUNK, CHUNK), 0).astype(F32)
    col = lax.broadcasted_iota(jnp.int32, (CHUNK, CHUNK), 1).astype(F32)
    diff = row - col
    idx_col = lax.broadcasted_iota(jnp.int32, (CHUNK, 1), 0).astype(F32)
    idx_row = lax.broadcasted_iota(jnp.int32, (1, CHUNK), 1).astype(F32)

    def state(ref, sq, hh):
        return ref.at[hh] if seqs == 1 else ref.at[sq, hh]

    for hh, sq in [(hh, sq) for hh in range(heads) for sq in range(seqs)]:
        def chunk(c, sq=sq):
            return slice(sq * t + c * CHUNK, sq * t + (c + 1) * CHUNK)

        qk_cols = slice(hh * DK_B, (hh + 1) * DK_B)
        v_cols = slice(hh * DV_B, (hh + 1) * DV_B)
        lg_f = jnp.log1p(-jnp.exp(af_ref[hh]))
        lg_b = jnp.log1p(-jnp.exp(ab_ref[hh]))
        dmask = (jnp.where(diff >= 0, jnp.exp(jnp.maximum(diff, 0.0) * lg_f), 0.0)
                 + jnp.where(diff <= 0, jnp.exp(jnp.maximum(-diff, 0.0) * lg_b), 0.0))
        qd_f = jnp.exp((idx_col + 1.0) * lg_f)
        qd_b = jnp.exp((CHUNK - idx_col) * lg_b)
        kd_f = jnp.exp((CHUNK - 1.0 - idx_row) * lg_f)
        kd_b = jnp.exp(idx_row * lg_b)
        cd_f = jnp.exp(CHUNK * lg_f)
        cd_b = jnp.exp(CHUNK * lg_b)

        def states(order, kd, cd, s):
            seen = {}
            for n, c in enumerate(order):
                seen[c] = None if s is None else s.astype(BF16)
                if has_state and n == nc - 1:
                    return seen, None
                u = jnp.dot((kt_ref[qk_cols, chunk(c)] * kd).astype(BF16), v_ref[chunk(c), v_cols],
                            preferred_element_type=F32)
                s = u if s is None else s * cd + u
            return seen, s

        seen_f, s_f = states(range(nc), kd_f, cd_f,
                             state(s0f_ref, sq, hh)[...] if has_state else None)
        seen_b, s_b = states(range(nc - 1, -1, -1), kd_b, cd_b,
                             state(s0b_ref, sq, hh)[...] if has_state else None)
        if not has_state:
            state(sf_ref, sq, hh)[...] = s_f
            state(sb_ref, sq, hh)[...] = s_b
        for c in range(nc):
            qc = q_ref[chunk(c), qk_cols]
            qk = jnp.dot(qc, kt_ref[qk_cols, chunk(c)].astype(BF16), preferred_element_type=F32)
            o = jnp.dot((qk * dmask).astype(BF16), v_ref[chunk(c), v_cols],
                        preferred_element_type=F32)
            if seen_f[c] is not None:
                o = o + jnp.dot(qc, seen_f[c], preferred_element_type=F32) * qd_f
            if seen_b[c] is not None:
                o = o + jnp.dot(qc, seen_b[c], preferred_element_type=F32) * qd_b
            o = o * lax.rsqrt(jnp.mean(o * o, axis=-1, keepdims=True) + RMS_EPS)
            gate_pre = g_ref[chunk(c), v_cols].astype(F32)
            y_ref[chunk(c), v_cols] = (o * _silu(gate_pre)).astype(BF16)
    if fused_out:
        part = jnp.dot(y_ref[...], w_ref[...].astype(BF16), preferred_element_type=F32)
        gate = mod_ref[:, 2 * D_MODEL:3 * D_MODEL]
        head_steps = H_B // heads
        if head_steps == 1:
            o_ref[...] = _residual_layer_norm(x_ref[...], part, gate, lng_ref[...], lnb_ref[...])
        else:
            step = pl.program_id(1)

            @pl.when(step == 0)
            def _():
                o_ref[...] = part

            @pl.when(jnp.logical_and(step > 0, step < head_steps - 1))
            def _():
                o_ref[...] += part

            @pl.when(step == head_steps - 1)
            def _():
                o_ref[...] = _residual_layer_norm(x_ref[...], o_ref[...] + part, gate, lng_ref[...],
                                                  lnb_ref[...])


def _retention(q, kt, v, g, seq_len, n_seq, first_row, heads, seqs, decay_f, decay_b, states,
               out_proj=None):
    t = seqs * seq_len
    b = n_seq // seqs
    s0 = first_row // t
    has_state = states is not None
    fused_out = out_proj is not None
    seq_dim = None if seqs == 1 else seqs
    q_spec = pl.BlockSpec((t, heads * DK_B), lambda i, h: (s0 + i, h))
    kt_spec = pl.BlockSpec((heads * DK_B, t), lambda i, h: (h, s0 + i))
    vg_spec = pl.BlockSpec((t, heads * DV_B), lambda i, h: (s0 + i, h))
    a_spec = pl.BlockSpec((heads, 1, 1), lambda i, h: (h, 0, 0))
    in_specs = [q_spec, kt_spec, vg_spec, vg_spec, a_spec, a_spec]
    args = [q, kt, v, g, decay_f.reshape(H_B, 1, 1), decay_b.reshape(H_B, 1, 1)]
    out_shape = [jax.ShapeDtypeStruct((b * t, E_B), BF16)]
    out_specs = [pl.BlockSpec((t, heads * DV_B), lambda i, h: (i, h))]
    if has_state:
        s_f, s_b, jj = states
        s_spec = pl.BlockSpec((seq_dim, None, heads, DK_B, DV_B), lambda i, h: (i, jj, h, 0, 0))
        in_specs += [s_spec, s_spec]
        args += [s_f, s_b]
    scratch = []
    if fused_out:
        x2d, mod3, mod_map, w_out, jw, ln_g, ln_b = out_proj
        x_spec = pl.BlockSpec((t, D_MODEL), lambda i, h: (i, 0))
        const2 = lambda i, h: (0, 0)
        if heads == H_B:
            w_spec = _weight_spec(w_out, jw)
        else:
            w_spec = pl.BlockSpec((None, heads * DV_B, D_MODEL), lambda i, h: (jw, h, 0))
        in_specs += [x_spec, pl.BlockSpec((None, 1, 3 * D_MODEL), mod_map), w_spec,
                     pl.BlockSpec((1, D_MODEL), const2), pl.BlockSpec((1, D_MODEL), const2)]
        args += [x2d, mod3, w_out, ln_g.reshape(1, D_MODEL), ln_b.reshape(1, D_MODEL)]
        out_shape = [jax.ShapeDtypeStruct((b * t, D_MODEL), F32)]
        out_specs = [x_spec]
        scratch = [pltpu.VMEM((t, heads * DV_B), BF16)]
    if not has_state:
        so_spec = pl.BlockSpec((seq_dim, None, heads, DK_B, DV_B), lambda i, h: (i, 0, h, 0, 0))
        out_shape += [jax.ShapeDtypeStruct((n_seq, 1, H_B, DK_B, DV_B), F32)] * 2
        out_specs += [so_spec, so_spec]
    return pl.pallas_call(
        functools.partial(_retention_kernel, has_state=has_state, heads=heads, seqs=seqs,
                          fused_out=fused_out),
        out_shape=out_shape,
        grid=(b, H_B // heads),
        in_specs=in_specs,
        out_specs=out_specs,
        scratch_shapes=scratch,
        compiler_params=_params(2),
        name="retention_state" if has_state else "retention",
    )(*args)


CONV_ROWS = 1024
CONV_COLS = 256


def _conv_kernel(xc_ref, xl_ref, mod_ref, w_in_ref, cw_ref, w_out_ref, g_ref, b_ref, o_ref, *,
                 n_ctx_tiles, ctx_seq_len, lat_seq_len):
    x = _both_streams_rows(xc_ref, xl_ref, n_ctx_tiles)
    h = _modulated(x, mod_ref)
    rows = x.shape[0]
    is_ctx = pl.program_id(0) < n_ctx_tiles
    row = lax.broadcasted_iota(jnp.int32, (rows, 1), 0)
    pos = jnp.where(is_ctx, row % ctx_seq_len, row % lat_seq_len)
    has_prev = pos > 0
    has_next = pos < jnp.where(is_ctx, ctx_seq_len - 1, lat_seq_len - 1)
    e = D_MODEL
    for c in range(e // CONV_COLS):
        cols = slice(c * CONV_COLS, (c + 1) * CONV_COLS)

        def proj(part):
            lo = part * e + c * CONV_COLS
            return jnp.dot(h, w_in_ref[:, lo:lo + CONV_COLS], preferred_element_type=F32)

        p = proj(1) * proj(2)
        prev = jnp.where(has_prev, pltpu.roll(p, 1, 0), 0.0)
        nxt = jnp.where(has_next, pltpu.roll(p, rows - 1, 0), 0.0)
        conv = prev * cw_ref[0:1, cols] + p * cw_ref[1:2, cols] + nxt * cw_ref[2:3, cols]
        y = (proj(0) * conv * _silu(proj(3))).astype(BF16)
        part = jnp.dot(y, w_out_ref[cols, :].astype(BF16), preferred_element_type=F32)
        if c == 0:
            o_ref[...] = part
        else:
            o_ref[...] += part
    gate = mod_ref[:, 2 * D_MODEL:3 * D_MODEL]
    o_ref[...] = _residual_layer_norm(x, o_ref[...], gate, g_ref[...], b_ref[...])


def _conv_layer(x_ctx, x_lat, mod3, w_in, conv_w, w_out, j, ln_g, ln_b, layer, ctx_seq_len,
                lat_seq_len):
    m_rows = x_ctx.shape[0] + x_lat.shape[0]
    tm = CONV_ROWS
    n_ctx_tiles = x_ctx.shape[0] // tm
    const2 = lambda m: (0, 0)
    return pl.pallas_call(
        functools.partial(_conv_kernel, n_ctx_tiles=n_ctx_tiles, ctx_seq_len=ctx_seq_len,
                          lat_seq_len=lat_seq_len),
        out_shape=jax.ShapeDtypeStruct((m_rows, D_MODEL), F32),
        grid=(m_rows // tm,),
        in_specs=_both_streams_specs(x_ctx, x_lat, tm) + [
                  pl.BlockSpec((None, 1, 3 * D_MODEL),
                               _both_streams_mod_map(layer, n_ctx_tiles, tm, lat_seq_len)),
                  _weight_spec(w_in, j),
                  pl.BlockSpec((None, 3, D_MODEL), lambda m: (j, 0, 0)),
                  _weight_spec(w_out, j),
                  pl.BlockSpec((1, D_MODEL), const2),
                  pl.BlockSpec((1, D_MODEL), const2)],
        out_specs=pl.BlockSpec((tm, D_MODEL), lambda m: (m, 0)),
        compiler_params=_params(1),
        name="conv_layer",
    )(x_ctx, x_lat, mod3, w_in, conv_w, w_out, ln_g.reshape(1, D_MODEL), ln_b.reshape(1, D_MODEL))


def _separate(xp, xs, m_ctx):
    if xp[0] is xs[0]:
        return (xp[0][:m_ctx], 0), (xs[0][m_ctx:], 0)
    return xp, xs


def kernel(x_prompt, x_sample, cache_k, cache_v, state_fwd, state_bwd, c, c_ctx, w_mod, b_mod, ln_g,
           ln_b, w_in_a, lam_a, subln_a, w_out_a, w_in_b, decay_fwd, decay_bwd, w_out_b, w_in_c,
           conv_c, w_out_c):
    bp, tp, d = x_prompt.shape
    bs, ts, _ = x_sample.shape
    assert bs == CTX_ROW < MOD_ROWS
    mod3 = _modulation(c, c_ctx, w_mod, b_mod)
    rope_tables = _rope_tables(ts)

    m_ctx, m_lat = bp * tp, bs * ts
    xp = (x_prompt.reshape(m_ctx, d), 0)
    xs = (x_sample.reshape(m_lat, d), 0)
    n_attn = (DEPTH + N_MIXERS - 1) // N_MIXERS
    new_cache_k = new_cache_v = jax.ShapeDtypeStruct((bp, n_attn, tp, H_A, DV_A), F32)
    new_sf, new_sb = [], []
    for i in range(DEPTH):
        kind, j = i % N_MIXERS, i // N_MIXERS
        if kind == 0:
            x_ctx, new_cache_k, new_cache_v = _ctx_attention_layer(
                xp, m_ctx, tp, mod3, w_in_a, w_out_a, j, lam_a[j], subln_a[j], ln_g[i], ln_b[i], i,
                (new_cache_k, new_cache_v))
            qkvz = _attn_in(xs, m_lat, mod3, w_in_a, j, i, ts, rope_tables)
            xs = (_latent_attention(qkvz, xs, ts, (cache_k, cache_v, j), mod3, lam_a[j],
                                    subln_a[j], w_out_a, j, ln_g[i], ln_b[i], i), 0)
            xp = (x_ctx, 0)
        else:
            xp, xs = _separate(xp, xs, m_ctx)
            if kind == 1:
                q, kt, v, g = _ret_in(xp[0], xs[0], mod3, w_in_b, j, i, ts)
                x_ctx, s_f, s_b = _retention(
                    q, kt, v, g, tp, bp, 0, H_B, CTX_RET_SEQS, decay_fwd[j], decay_bwd[j], None,
                    out_proj=(xp[0], mod3, _mod_row_map(i, None, tp), w_out_b, j, ln_g[i],
                              ln_b[i]))
                new_sf.append(s_f)
                new_sb.append(s_b)
                (x_lat,) = _retention(
                    q, kt, v, g, ts, bs, m_ctx, LAT_RET_HEADS, 1, decay_fwd[j], decay_bwd[j],
                    (state_fwd, state_bwd, j),
                    out_proj=(xs[0], mod3, _mod_row_map(i, ts, ts), w_out_b, j, ln_g[i], ln_b[i]))
                xp, xs = (x_ctx, 0), (x_lat, 0)
            else:
                x_all = _conv_layer(xp[0], xs[0], mod3, w_in_c, conv_c, w_out_c, j, ln_g[i],
                                    ln_b[i], i, tp, ts)
                xp, xs = (x_all, 0), (x_all, m_ctx)
    xp, xs = _separate(xp, xs, m_ctx)
    y_prompt = xp[0].reshape(bp, tp, d)
    y_sample = xs[0].reshape(bs, ts, d)
    new_state_fwd = jnp.concatenate(new_sf, axis=1)
    new_state_bwd = jnp.concatenate(new_sb, axis=1)
    return (y_prompt, y_sample, new_cache_k, new_cache_v, new_state_fwd, new_state_bwd)
```

```python
import functools
import math

import jax
import jax.numpy as jnp
import numpy as np
from jax import lax
from jax.experimental import pallas as pl
from jax.experimental.pallas import tpu as pltpu

F32 = jnp.float32
BF16 = jnp.bfloat16

D_MODEL = 1024
DEPTH = 4
N_MIXERS = 3
GRID_W = 64
H_A = 8
DH_A = 64
DV_A = 128
ROPE_HALF = DH_A // 2
SCORE_SCALE = DH_A ** -0.5 * math.log2(math.e)
H_B = 4
DK_B = 256
DV_B = 512
E_B = H_B * DV_B
CHUNK = 256
LAT_RET_HEADS = 2
CTX_RET_SEQS = 2
ALPHA = (2.0 * DEPTH) ** 0.25
ROPE_BASE = 10000.0
LN_EPS = 1e-5
RMS_EPS = 1e-6

MOD_ROWS = 8
CTX_ROW = 4
VMEM_LIMIT_BYTES = 58 * 1024 * 1024
ROW_TILE = 512


def _params(n_axes):
    return pltpu.CompilerParams(dimension_semantics=("arbitrary",) * n_axes,
                                vmem_limit_bytes=VMEM_LIMIT_BYTES)


def _silu(x):
    return x * jax.nn.sigmoid(x)


def _residual_layer_norm(x, out, gate, g, b):
    r = ALPHA * x + gate * out
    mu = jnp.mean(r, axis=-1, keepdims=True)
    d = r - mu
    var = jnp.mean(d * d, axis=-1, keepdims=True)
    return d * lax.rsqrt(var + LN_EPS) * g + b


def _modulated(x, mod_ref):
    shift = mod_ref[:, 0:D_MODEL]
    scale = mod_ref[:, D_MODEL:2 * D_MODEL]
    return x * (1.0 + scale) + shift


def _mod_row_map(layer, rows_per_batch, tile):
    if rows_per_batch is None:
        return lambda m, *_: (layer * MOD_ROWS + CTX_ROW, 0, 0)
    return lambda m, *_: (layer * MOD_ROWS + (m * tile) // rows_per_batch, 0, 0)


def _both_streams_specs(x_ctx, x_lat, tm):
    n_ctx_tiles = x_ctx.shape[0] // tm
    return [pl.BlockSpec((tm, D_MODEL), lambda m: (jnp.minimum(m, n_ctx_tiles - 1), 0)),
            pl.BlockSpec((tm, D_MODEL), lambda m: (jnp.maximum(m - n_ctx_tiles, 0), 0))]


def _both_streams_rows(x_ctx_ref, x_lat_ref, n_ctx_tiles):
    return jnp.where(pl.program_id(0) < n_ctx_tiles, x_ctx_ref[...], x_lat_ref[...])


def _both_streams_mod_map(layer, n_ctx_tiles, tm, lat_rows_per_batch):
    def index_map(m):
        lat_row = ((m - n_ctx_tiles) * tm) // lat_rows_per_batch
        return (layer * MOD_ROWS + jnp.where(m < n_ctx_tiles, CTX_ROW, lat_row), 0, 0)
    return index_map


def _weight_spec(w, j):
    return pl.BlockSpec((None,) + w.shape[1:], lambda *_: (j, 0, 0), pipeline_mode=pl.Buffered(1))


WEIGHT_CHUNK_COLS = 1024
WEIGHT_COPIES_IN_FLIGHT = 2


class _StagedWeight:
    @staticmethod
    def scratch(w, chunk_cols):
        return [pltpu.VMEM(w.shape[1:], w.dtype),
                pltpu.SemaphoreType.DMA((w.shape[2] // chunk_cols,))]

    def __init__(self, w_hbm_ref, w_vmem_ref, sem_ref, j, chunk_cols):
        self.copies = [
            pltpu.make_async_copy(w_hbm_ref.at[j, :, pl.ds(c * chunk_cols, chunk_cols)],
                                  w_vmem_ref.at[:, pl.ds(c * chunk_cols, chunk_cols)], sem_ref.at[c])
            for c in range(w_vmem_ref.shape[1] // chunk_cols)]

    def _arrive(self, c):
        self.copies[c].wait()
        if c + WEIGHT_COPIES_IN_FLIGHT < len(self.copies):
            self.copies[c + WEIGHT_COPIES_IN_FLIGHT].start()

    def run(self, body):
        first = pl.program_id(0) == 0

        @pl.when(first)
        def _():
            for cp in self.copies[:WEIGHT_COPIES_IN_FLIGHT]:
                cp.start()
            body(self._arrive)

        @pl.when(jnp.logical_not(first))
        def _():
            body(lambda c: None)


def _mod_kernel(c_ref, cctx_ref, w_ref, b_ref, o_ref):
    pad = jnp.zeros((MOD_ROWS - c_ref.shape[0] - 1, D_MODEL), F32)
    s = _silu(jnp.concatenate([c_ref[...], cctx_ref[...], pad], axis=0))
    m = jnp.dot(s, w_ref[...], preferred_element_type=F32) + b_ref[...]
    for r in range(MOD_ROWS):
        o_ref[r] = m[r:r + 1]


def _modulation(c, c_ctx, w_mod, b_mod):
    n = 3 * D_MODEL
    return pl.pallas_call(
        _mod_kernel,
        out_shape=jax.ShapeDtypeStruct((DEPTH * MOD_ROWS, 1, n), F32),
        grid=(DEPTH,),
        in_specs=[pl.BlockSpec(c.shape, lambda i: (0, 0)),
                  pl.BlockSpec((1, D_MODEL), lambda i: (0, 0)),
                  pl.BlockSpec((None, D_MODEL, n), lambda i: (i, 0, 0)),
                  pl.BlockSpec((None, 1, n), lambda i: (i, 0, 0))],
        out_specs=pl.BlockSpec((MOD_ROWS, 1, n), lambda i: (i, 0, 0)),
        compiler_params=_params(1),
        name="modulation",
    )(c, c_ctx.reshape(1, D_MODEL), w_mod, b_mod.reshape(DEPTH, 1, n))


def _rope(xh, cos4, sin4, first_half):
    swapped = jnp.where(first_half, pltpu.roll(xh, DV_A - ROPE_HALF, 1),
                        pltpu.roll(xh, ROPE_HALF, 1))
    return xh * cos4 + swapped * sin4


def _store_heads(o_ref, x, seq_len):
    for b in range(o_ref.shape[0]):
        xb = x[b * seq_len:(b + 1) * seq_len].reshape(seq_len, H_A, DV_A)
        if len(o_ref.shape) == 4:
            o_ref[b] = xb
        else:
            o_ref[b, 0] = xb
            for s in range(1, o_ref.shape[1]):
                o_ref[b, s] = jnp.zeros_like(xb)


def _attn_in_body(x_ref, mod_ref, w_ref, rope_refs, q_ref, k_ref, vt_ref, z_ref, cache_refs, seq_len):
    h = _modulated(x_ref[...], mod_ref)
    if rope_refs is not None:
        lane = lax.broadcasted_iota(jnp.int32, (1, DV_A), 1)
        first_half = (lane % DH_A) < ROPE_HALF
        cos4 = rope_refs[0][...]
        sin4 = rope_refs[1][...]
    q_all = jnp.dot(h, w_ref[:, 0:D_MODEL], preferred_element_type=F32)
    k_all = jnp.dot(h, w_ref[:, D_MODEL:2 * D_MODEL], preferred_element_type=F32)
    if cache_refs is not None:
        _store_heads(cache_refs[0], k_all, seq_len)
    for hd in range(H_A):
        cols = slice(hd * DV_A, (hd + 1) * DV_A)
        q = q_all[:, cols]
        k = k_all[:, cols]
        if rope_refs is not None:
            q = _rope(q, cos4, sin4, first_half)
            k = _rope(k, cos4, sin4, first_half)
        q_ref[:, cols] = (q * SCORE_SCALE).astype(BF16)
        k_ref[:, cols] = k.astype(BF16)
    v = jnp.dot(h, w_ref[:, 2 * D_MODEL:3 * D_MODEL], preferred_element_type=F32)
    if cache_refs is not None:
        _store_heads(cache_refs[1], v, seq_len)
    vt_ref[...] = v.T.astype(BF16)
    z_ref[...] = jnp.dot(h, w_ref[:, 3 * D_MODEL:4 * D_MODEL],
                         preferred_element_type=F32).astype(z_ref.dtype)


def _attn_in_kernel(x_ref, mod_ref, w_ref, cos_ref, sin_ref, q_ref, k_ref, vt_ref, z_ref):
    _attn_in_body(x_ref, mod_ref, w_ref, (cos_ref, sin_ref), q_ref, k_ref, vt_ref, z_ref, None, None)


def _attn_in(x, m_rows, mod3, w_in, j, layer, rows_per_batch, rope_tables):
    x2d, x_first_row = x
    tm = ROW_TILE
    seq_tiles = rows_per_batch // tm
    row_spec = pl.BlockSpec((tm, D_MODEL), lambda m: (m, 0))
    tab_spec = pl.BlockSpec((tm, DV_A), lambda m: (m % seq_tiles, 0))
    return pl.pallas_call(
        _attn_in_kernel,
        out_shape=[jax.ShapeDtypeStruct((m_rows, D_MODEL), BF16),
                   jax.ShapeDtypeStruct((m_rows, D_MODEL), BF16),
                   jax.ShapeDtypeStruct((D_MODEL, m_rows), BF16),
                   jax.ShapeDtypeStruct((m_rows, D_MODEL), BF16)],
        grid=(m_rows // tm,),
        in_specs=[pl.BlockSpec((tm, D_MODEL), lambda m: (x_first_row // tm + m, 0)),
                  pl.BlockSpec((None, 1, 3 * D_MODEL), _mod_row_map(layer, rows_per_batch, tm)),
                  _weight_spec(w_in, j), tab_spec, tab_spec],
        out_specs=[row_spec, row_spec, pl.BlockSpec((D_MODEL, tm), lambda m: (0, m)), row_spec],
        compiler_params=_params(1),
        name="attn_in_rope",
    )(x2d, mod3, w_in, *rope_tables)


ONES_ROWS = 16
CTX_ATTN_SEQS = 2
ATTN_ROWS = 512


def _slab_reduce(op, x):
    parts = [x[i:i + 8] for i in range(0, x.shape[0], 8)]
    while len(parts) > 1:
        parts = [op(parts[i], parts[i + 1]) for i in range(0, len(parts) - 1, 2)] + (
            [parts[-1]] if len(parts) % 2 else [])
    return parts[0]


def _attn_kernel(*refs, layer_idx, has_ctx, n_seq, shared_keys):
    if has_ctx:
        (q_ref, k_ref, vt_ref, z_ref, x_ref, kc_ref, vc_ref, mod_ref, lam_ref, subln_ref,
         w_ref, g_ref, b_ref, o_ref, y_ref, kcb_ref, vct_ref) = refs

        @pl.when(pl.program_id(1) == 0)
        def _():
            n_ctx = kc_ref.shape[0]
            kcb_ref[...] = kc_ref[...].reshape(n_ctx, D_MODEL).astype(BF16)
            vct_ref[...] = vc_ref[...].reshape(n_ctx, D_MODEL).T.astype(BF16)
    else:
        (q_ref, k_ref, vt_ref, z_ref, x_ref, mod_ref, lam_ref, subln_ref,
         w_ref, g_ref, b_ref, o_ref, y_ref) = refs
    tq = q_ref.shape[0] // n_seq
    lam_init = 0.8 - 0.6 * math.exp(-0.3 * layer_idx)
    lm = lam_ref[...]
    lam = (jnp.exp(jnp.sum(lm[0:1] * lm[1:2], axis=-1, keepdims=True))
           - jnp.exp(jnp.sum(lm[2:3] * lm[3:4], axis=-1, keepdims=True)) + lam_init)
    lane = lax.broadcasted_iota(jnp.int32, (1, DV_A), 1)
    first = lane < DH_A
    subln = jnp.broadcast_to(subln_ref[...], (DV_A, tq))
    nt = (((1,), (1,)), ((), ()))
    t = k_ref.shape[0] if shared_keys else k_ref.shape[0] // n_seq

    def key_rows(sq):
        return slice(0, t) if shared_keys else slice(sq * t, (sq + 1) * t)

    def scores(unit):
        sq, hd = unit
        cols = slice(hd * DV_A, (hd + 1) * DV_A)
        qh = q_ref[sq * tq:(sq + 1) * tq, cols]
        zero = jnp.zeros_like(qh)
        qq = jnp.concatenate([jnp.where(first, qh, zero), jnp.where(first, zero, qh)], axis=0)
        parts = [lax.dot_general(k_ref[key_rows(sq), cols], qq, nt, preferred_element_type=F32)]
        if has_ctx:
            parts.append(lax.dot_general(kcb_ref[:, cols], qq, nt, preferred_element_type=F32))
        return parts

    def exps(parts):
        m8 = functools.reduce(jnp.maximum, [_slab_reduce(jnp.maximum, s) for s in parts])
        m = jnp.max(m8, axis=0, keepdims=True)
        return [jnp.exp2(s - m).astype(BF16) for s in parts]

    def with_ones(vals_t):
        return jnp.concatenate([vals_t, jnp.ones((ONES_ROWS, vals_t.shape[1]), BF16)], axis=0)

    def finish(unit, es):
        sq, hd = unit
        cols = slice(hd * DV_A, (hd + 1) * DV_A)
        rows = slice(sq * tq, (sq + 1) * tq)
        acc = jnp.dot(with_ones(vt_ref[cols, key_rows(sq)]), es[0], preferred_element_type=F32)
        if has_ctx:
            acc = acc + jnp.dot(with_ones(vct_ref[cols, :]), es[1], preferred_element_type=F32)
        inv = 1.0 / acc[DV_A:DV_A + 1, :]
        ot = acc[:DV_A, :tq] * inv[:, :tq] - acc[:DV_A, tq:] * (inv[:, tq:] * lam)
        ot = ot * lax.rsqrt(jnp.mean(ot * ot, axis=0, keepdims=True) + RMS_EPS)
        ot = ot * subln * (1.0 - lam_init)
        y_ref[rows, cols] = (ot.T * _silu(z_ref[rows, cols].astype(F32))).astype(BF16)

    units = [(sq, hd) for hd in range(H_A) for sq in range(n_seq)]
    s_ahead = {u: scores(units[u]) for u in range(min(2, len(units)))}
    e_ahead = {0: exps(s_ahead.pop(0))}
    for u in range(len(units)):
        if u + 2 < len(units):
            s_ahead[u + 2] = scores(units[u + 2])
        if u + 1 < len(units):
            e_ahead[u + 1] = exps(s_ahead.pop(u + 1))
        finish(units[u], e_ahead.pop(u))
    out = jnp.dot(y_ref[...], w_ref[...].astype(BF16), preferred_element_type=F32)
    gate = mod_ref[:, 2 * D_MODEL:3 * D_MODEL]
    o_ref[...] = _residual_layer_norm(x_ref[...], out, gate, g_ref[...], b_ref[...])


def _latent_attention(qkvz, x, seq_len, ctx, mod3, lam, subln, w_out, j, ln_g, ln_b, layer):
    q, k, vt, z = qkvz
    m_rows = q.shape[0]
    t = seq_len
    rows = ATTN_ROWS
    nq = t // rows
    x, x_first_row = x
    x_spec = pl.BlockSpec((rows, D_MODEL), lambda i, j: (x_first_row // rows + i * nq + j, 0))
    q_spec = pl.BlockSpec((rows, D_MODEL), lambda i, j: (i * nq + j, 0))
    k_spec = pl.BlockSpec((t, D_MODEL), lambda i, j: (i, 0))
    vt_spec = pl.BlockSpec((D_MODEL, t), lambda i, j: (0, i))
    cache_k, cache_v, slot = ctx
    n_ctx = cache_k.shape[2]
    c_spec = pl.BlockSpec((None, None, n_ctx, H_A, DV_A), lambda i, j: (i, slot, 0, 0, 0))
    const2 = lambda i, j: (0, 0)
    return pl.pallas_call(
        functools.partial(_attn_kernel, layer_idx=layer, has_ctx=True, n_seq=1, shared_keys=True),
        out_shape=jax.ShapeDtypeStruct((m_rows, D_MODEL), F32),
        grid=(m_rows // t, nq),
        in_specs=[q_spec, k_spec, vt_spec, q_spec, x_spec, c_spec, c_spec,
                  pl.BlockSpec((None, 1, 3 * D_MODEL), lambda i, j: (layer * MOD_ROWS + i, 0, 0)),
                  pl.BlockSpec((4, DH_A), const2),
                  pl.BlockSpec((DV_A, 1), const2),
                  _weight_spec(w_out, j),
                  pl.BlockSpec((1, D_MODEL), const2),
                  pl.BlockSpec((1, D_MODEL), const2)],
        out_specs=q_spec,
        scratch_shapes=[pltpu.VMEM((rows, D_MODEL), BF16), pltpu.VMEM((n_ctx, D_MODEL), BF16),
                        pltpu.VMEM((D_MODEL, n_ctx), BF16)],
        compiler_params=_params(2),
        name="diff_attn_ctx",
    )(q, k, vt, z, x, cache_k, cache_v, mod3, lam, subln.reshape(DV_A, 1), w_out,
      ln_g.reshape(1, D_MODEL), ln_b.reshape(1, D_MODEL))


def _ctx_attn_layer_kernel(*refs, layer_idx, seq_len, n_aliased):
    x_ref, mod_ref, w_in_ref, lam_ref, subln_ref, w_out_ref, g_ref, b_ref = refs[:8]
    o_ref, ck_ref, cv_ref, q_ref, k_ref, vt_ref, z_ref, y_ref = refs[8 + n_aliased:]
    _attn_in_body(x_ref, mod_ref, w_in_ref, None, q_ref, k_ref, vt_ref, z_ref, (ck_ref, cv_ref),
                  seq_len)
    _attn_kernel(q_ref, k_ref, vt_ref, z_ref, x_ref, mod_ref, lam_ref, subln_ref, w_out_ref, g_ref,
                 b_ref, o_ref, y_ref, layer_idx=layer_idx, has_ctx=False,
                 n_seq=x_ref.shape[0] // seq_len, shared_keys=False)


def _ctx_attention_layer(x, m_rows, seq_len, mod3, w_in, w_out, j, lam, subln, ln_g, ln_b, layer,
                         cache_out):
    x2d, x_first_row = x
    rows = CTX_ATTN_SEQS * seq_len
    const2 = lambda m: (0, 0)
    in_specs = [pl.BlockSpec((rows, D_MODEL), lambda m: (x_first_row // rows + m, 0)),
                pl.BlockSpec((None, 1, 3 * D_MODEL), lambda m: (layer * MOD_ROWS + CTX_ROW, 0, 0)),
                _weight_spec(w_in, j),
                pl.BlockSpec((4, DH_A), const2),
                pl.BlockSpec((DV_A, 1), const2),
                _weight_spec(w_out, j),
                pl.BlockSpec((1, D_MODEL), const2),
                pl.BlockSpec((1, D_MODEL), const2)]
    args = [x2d, mod3, w_in, lam, subln.reshape(DV_A, 1), w_out, ln_g.reshape(1, D_MODEL),
            ln_b.reshape(1, D_MODEL)]
    new_k, new_v = cache_out
    cache_shape = jax.ShapeDtypeStruct(new_k.shape, F32)
    aliases = {}
    if j == 0:
        cache_spec = pl.BlockSpec((CTX_ATTN_SEQS, new_k.shape[1], seq_len, H_A, DV_A),
                                  lambda m: (m, 0, 0, 0, 0))
    else:
        cache_spec = pl.BlockSpec((CTX_ATTN_SEQS, None, seq_len, H_A, DV_A),
                                  lambda m: (m, j, 0, 0, 0))
        in_specs += [pl.BlockSpec(memory_space=pl.ANY)] * 2
        args += [new_k, new_v]
        aliases = {len(args) - 2: 1, len(args) - 1: 2}
    return pl.pallas_call(
        functools.partial(_ctx_attn_layer_kernel, layer_idx=layer, seq_len=seq_len,
                          n_aliased=len(aliases)),
        out_shape=[jax.ShapeDtypeStruct((m_rows, D_MODEL), F32), cache_shape, cache_shape],
        grid=(m_rows // rows,),
        in_specs=in_specs,
        out_specs=[pl.BlockSpec((rows, D_MODEL), lambda m: (m, 0)), cache_spec, cache_spec],
        scratch_shapes=[pltpu.VMEM((rows, D_MODEL), BF16), pltpu.VMEM((rows, D_MODEL), BF16),
                        pltpu.VMEM((D_MODEL, rows), BF16), pltpu.VMEM((rows, D_MODEL), BF16),
                        pltpu.VMEM((rows, D_MODEL), BF16)],
        input_output_aliases=aliases,
        compiler_params=_params(1),
        name="ctx_attn_layer",
    )(*args)


def _rope_tables(n_tokens):
    rows = n_tokens // GRID_W
    r = np.repeat(np.arange(rows, dtype=np.float32), GRID_W)
    col = np.tile(np.arange(GRID_W, dtype=np.float32), rows)
    n_freq = DH_A // 4
    inv = (ROPE_BASE ** (-np.arange(n_freq, dtype=np.float32) / n_freq)).astype(np.float32)
    ang = np.concatenate([r[:, None] * inv, col[:, None] * inv], -1)
    cos, sin = np.cos(ang), np.sin(ang)
    return jnp.asarray(np.tile(cos, (1, 4))), jnp.asarray(np.concatenate([-sin, sin, -sin, sin], -1))


def _ret_in_kernel(xc_ref, xl_ref, mod_ref, w_hbm_ref, q_ref, kt_ref, v_ref, g_ref, w_ref, sem_ref, *,
                   n_ctx_tiles, j):
    nq = H_B * DK_B
    cc = WEIGHT_CHUNK_COLS
    assert nq == cc and E_B % cc == 0
    staged = _StagedWeight(w_hbm_ref, w_ref, sem_ref, j, cc)

    def body(arrive):
        h = _modulated(_both_streams_rows(xc_ref, xl_ref, n_ctx_tiles), mod_ref)
        arrive(0)
        q_ref[...] = jnp.dot(h, w_ref[:, 0:nq], preferred_element_type=F32).astype(BF16)
        arrive(1)
        k = jnp.dot(h, w_ref[:, nq:2 * nq], preferred_element_type=F32) * (DK_B ** -0.5)
        kt_ref[...] = k.T
        for c in range(E_B // cc):
            arrive(2 + c)
            v = jnp.dot(h, w_ref[:, 2 * nq + c * cc:2 * nq + (c + 1) * cc],
                        preferred_element_type=F32)
            v_ref[:, c * cc:(c + 1) * cc] = v.astype(BF16)
        for c in range(E_B // cc):
            arrive(2 + E_B // cc + c)
            g_ref[:, c * cc:(c + 1) * cc] = jnp.dot(
                h, w_ref[:, 2 * nq + E_B + c * cc:2 * nq + E_B + (c + 1) * cc],
                preferred_element_type=F32).astype(BF16)

    staged.run(body)


def _ret_in(x_ctx, x_lat, mod3, w_in, j, layer, lat_rows_per_batch):
    m_rows = x_ctx.shape[0] + x_lat.shape[0]
    tm = ROW_TILE
    nq = H_B * DK_B
    n_ctx_tiles = x_ctx.shape[0] // tm
    return pl.pallas_call(
        functools.partial(_ret_in_kernel, n_ctx_tiles=n_ctx_tiles, j=j),
        out_shape=[jax.ShapeDtypeStruct((m_rows, nq), BF16),
                   jax.ShapeDtypeStruct((nq, m_rows), F32),
                   jax.ShapeDtypeStruct((m_rows, E_B), BF16),
                   jax.ShapeDtypeStruct((m_rows, E_B), BF16)],
        grid=(m_rows // tm,),
        in_specs=_both_streams_specs(x_ctx, x_lat, tm) + [
            pl.BlockSpec((None, 1, 3 * D_MODEL),
                         _both_streams_mod_map(layer, n_ctx_tiles, tm, lat_rows_per_batch)),
            pl.BlockSpec(memory_space=pl.ANY)],
        out_specs=[pl.BlockSpec((tm, nq), lambda m: (m, 0)),
                   pl.BlockSpec((nq, tm), lambda m: (0, m)),
                   pl.BlockSpec((tm, E_B), lambda m: (m, 0)),
                   pl.BlockSpec((tm, E_B), lambda m: (m, 0))],
        scratch_shapes=_StagedWeight.scratch(w_in, WEIGHT_CHUNK_COLS),
        compiler_params=_params(1),
        name="ret_in",
    )(x_ctx, x_lat, mod3, w_in)


def _retention_kernel(*refs, has_state, heads, seqs, fused_out):
    refs = list(refs)
    q_ref, kt_ref, v_ref, g_ref, af_ref, ab_ref = refs[:6]
    del refs[:6]
    if has_state:
        s0f_ref, s0b_ref = refs[:2]
        del refs[:2]
    if fused_out:
        x_ref, mod_ref, w_ref, lng_ref, lnb_ref = refs[:5]
        del refs[:5]
        o_ref = refs.pop(0)
        y_ref = refs.pop()
    else:
        y_ref = refs.pop(0)
    if not has_state:
        sf_ref, sb_ref = refs
    t = q_ref.shape[0] // seqs
    nc = t // CHUNK
    row = lax.broadcasted_iota(jnp.int32, (CHUNK, CHUNK), 0).astype(F32)
    col = lax.broadcasted_iota(jnp.int32, (CHUNK, CHUNK), 1).astype(F32)
    diff = row - col
    idx_col = lax.broadcasted_iota(jnp.int32, (CHUNK, 1), 0).astype(F32)
    idx_row = lax.broadcasted_iota(jnp.int32, (1, CHUNK), 1).astype(F32)

    def state(ref, sq, hh):
        return ref.at[hh] if seqs == 1 else ref.at[sq, hh]

    for hh, sq in [(hh, sq) for hh in range(heads) for sq in range(seqs)]:
        def chunk(c, sq=sq):
            return slice(sq * t + c * CHUNK, sq * t + (c + 1) * CHUNK)

        qk_cols = slice(hh * DK_B, (hh + 1) * DK_B)
        v_cols = slice(hh * DV_B, (hh + 1) * DV_B)
        lg_f = jnp.log1p(-jnp.exp(af_ref[hh]))
        lg_b = jnp.log1p(-jnp.exp(ab_ref[hh]))
        dmask = (jnp.where(diff >= 0, jnp.exp(jnp.maximum(diff, 0.0) * lg_f), 0.0)
                 + jnp.where(diff <= 0, jnp.exp(jnp.maximum(-diff, 0.0) * lg_b), 0.0))
        qd_f = jnp.exp((idx_col + 1.0) * lg_f)
        qd_b = jnp.exp((CHUNK - idx_col) * lg_b)
        kd_f = jnp.exp((CHUNK - 1.0 - idx_row) * lg_f)
        kd_b = jnp.exp(idx_row * lg_b)
        cd_f = jnp.exp(CHUNK * lg_f)
        cd_b = jnp.exp(CHUNK * lg_b)

        def states(order, kd, cd, s):
            seen = {}
            for n, c in enumerate(order):
                seen[c] = None if s is None else s.astype(BF16)
                if has_state and n == nc - 1:
                    return seen, None
                u = jnp.dot((kt_ref[qk_cols, chunk(c)] * kd).astype(BF16), v_ref[chunk(c), v_cols],
                            preferred_element_type=F32)
                s = u if s is None else s * cd + u
            return seen, s

        seen_f, s_f = states(range(nc), kd_f, cd_f,
                             state(s0f_ref, sq, hh)[...] if has_state else None)
        seen_b, s_b = states(range(nc - 1, -1, -1), kd_b, cd_b,
                             state(s0b_ref, sq, hh)[...] if has_state else None)
        if not has_state:
            state(sf_ref, sq, hh)[...] = s_f
            state(sb_ref, sq, hh)[...] = s_b
        for c in range(nc):
            qc = q_ref[chunk(c), qk_cols]
            qk = jnp.dot(qc, kt_ref[qk_cols, chunk(c)].astype(BF16), preferred_element_type=F32)
            o = jnp.dot((qk * dmask).astype(BF16), v_ref[chunk(c), v_cols],
                        preferred_element_type=F32)
            if seen_f[c] is not None:
                o = o + jnp.dot(qc, seen_f[c], preferred_element_type=F32) * qd_f
            if seen_b[c] is not None:
                o = o + jnp.dot(qc, seen_b[c], preferred_element_type=F32) * qd_b
            o = o * lax.rsqrt(jnp.mean(o * o, axis=-1, keepdims=True) + RMS_EPS)
            gate_pre = g_ref[chunk(c), v_cols].astype(F32)
            y_ref[chunk(c), v_cols] = (o * _silu(gate_pre)).astype(BF16)
    if fused_out:
        part = jnp.dot(y_ref[...], w_ref[...].astype(BF16), preferred_element_type=F32)
        gate = mod_ref[:, 2 * D_MODEL:3 * D_MODEL]
        head_steps = H_B // heads
        if head_steps == 1:
            o_ref[...] = _residual_layer_norm(x_ref[...], part, gate, lng_ref[...], lnb_ref[...])
        else:
            step = pl.program_id(1)

            @pl.when(step == 0)
            def _():
                o_ref[...] = part

            @pl.when(jnp.logical_and(step > 0, step < head_steps - 1))
            def _():
                o_ref[...] += part

            @pl.when(step == head_steps - 1)
            def _():
                o_ref[...] = _residual_layer_norm(x_ref[...], o_ref[...] + part, gate, lng_ref[...],
                                                  lnb_ref[...])


def _retention(q, kt, v, g, seq_len, n_seq, first_row, heads, seqs, decay_f, decay_b, states,
               out_proj=None):
    t = seqs * seq_len
    b = n_seq // seqs
    s0 = first_row // t
    has_state = states is not None
    fused_out = out_proj is not None
    seq_dim = None if seqs == 1 else seqs
    q_spec = pl.BlockSpec((t, heads * DK_B), lambda i, h: (s0 + i, h))
    kt_spec = pl.BlockSpec((heads * DK_B, t), lambda i, h: (h, s0 + i))
    vg_spec = pl.BlockSpec((t, heads * DV_B), lambda i, h: (s0 + i, h))
    a_spec = pl.BlockSpec((heads, 1, 1), lambda i, h: (h, 0, 0))
    in_specs = [q_spec, kt_spec, vg_spec, vg_spec, a_spec, a_spec]
    args = [q, kt, v, g, decay_f.reshape(H_B, 1, 1), decay_b.reshape(H_B, 1, 1)]
    out_shape = [jax.ShapeDtypeStruct((b * t, E_B), BF16)]
    out_specs = [pl.BlockSpec((t, heads * DV_B), lambda i, h: (i, h))]
    if has_state:
        s_f, s_b, jj = states
        s_spec = pl.BlockSpec((seq_dim, None, heads, DK_B, DV_B), lambda i, h: (i, jj, h, 0, 0))
        in_specs += [s_spec, s_spec]
        args += [s_f, s_b]
    scratch = []
    if fused_out:
        x2d, mod3, mod_map, w_out, jw, ln_g, ln_b = out_proj
        x_spec = pl.BlockSpec((t, D_MODEL), lambda i, h: (i, 0))
        const2 = lambda i, h: (0, 0)
        if heads == H_B:
            w_spec = _weight_spec(w_out, jw)
        else:
            w_spec = pl.BlockSpec((None, heads * DV_B, D_MODEL), lambda i, h: (jw, h, 0))
        in_specs += [x_spec, pl.BlockSpec((None, 1, 3 * D_MODEL), mod_map), w_spec,
                     pl.BlockSpec((1, D_MODEL), const2), pl.BlockSpec((1, D_MODEL), const2)]
        args += [x2d, mod3, w_out, ln_g.reshape(1, D_MODEL), ln_b.reshape(1, D_MODEL)]
        out_shape = [jax.ShapeDtypeStruct((b * t, D_MODEL), F32)]
        out_specs = [x_spec]
        scratch = [pltpu.VMEM((t, heads * DV_B), BF16)]
    if not has_state:
        so_spec = pl.BlockSpec((seq_dim, None, heads, DK_B, DV_B), lambda i, h: (i, 0, h, 0, 0))
        out_shape += [jax.ShapeDtypeStruct((n_seq, 1, H_B, DK_B, DV_B), F32)] * 2
        out_specs += [so_spec, so_spec]
    return pl.pallas_call(
        functools.partial(_retention_kernel, has_state=has_state, heads=heads, seqs=seqs,
                          fused_out=fused_out),
        out_shape=out_shape,
        grid=(b, H_B // heads),
        in_specs=in_specs,
        out_specs=out_specs,
        scratch_shapes=scratch,
        compiler_params=_params(2),
        name="retention_state" if has_state else "retention",
    )(*args)


CONV_ROWS = 1024
CONV_COLS = 256


def _conv_kernel(xc_ref, xl_ref, mod_ref, w_in_ref, cw_ref, w_out_ref, g_ref, b_ref, o_ref, *,
                 n_ctx_tiles, ctx_seq_len, lat_seq_len):
    x = _both_streams_rows(xc_ref, xl_ref, n_ctx_tiles)
    h = _modulated(x, mod_ref)
    rows = x.shape[0]
    is_ctx = pl.program_id(0) < n_ctx_tiles
    row = lax.broadcasted_iota(jnp.int32, (rows, 1), 0)
    pos = jnp.where(is_ctx, row % ctx_seq_len, row % lat_seq_len)
    has_prev = pos > 0
    has_next = pos < jnp.where(is_ctx, ctx_seq_len - 1, lat_seq_len - 1)
    e = D_MODEL
    for c in range(e // CONV_COLS):
        cols = slice(c * CONV_COLS, (c + 1) * CONV_COLS)

        def proj(part):
            lo = part * e + c * CONV_COLS
            return jnp.dot(h, w_in_ref[:, lo:lo + CONV_COLS], preferred_element_type=F32)

        p = proj(1) * proj(2)
        prev = jnp.where(has_prev, pltpu.roll(p, 1, 0), 0.0)
        nxt = jnp.where(has_next, pltpu.roll(p, rows - 1, 0), 0.0)
        conv = prev * cw_ref[0:1, cols] + p * cw_ref[1:2, cols] + nxt * cw_ref[2:3, cols]
        y = (proj(0) * conv * _silu(proj(3))).astype(BF16)
        part = jnp.dot(y, w_out_ref[cols, :].astype(BF16), preferred_element_type=F32)
        if c == 0:
            o_ref[...] = part
        else:
            o_ref[...] += part
    gate = mod_ref[:, 2 * D_MODEL:3 * D_MODEL]
    o_ref[...] = _residual_layer_norm(x, o_ref[...], gate, g_ref[...], b_ref[...])


def _conv_layer(x_ctx, x_lat, mod3, w_in, conv_w, w_out, j, ln_g, ln_b, layer, ctx_seq_len,
                lat_seq_len):
    m_rows = x_ctx.shape[0] + x_lat.shape[0]
    tm = CONV_ROWS
    n_ctx_tiles = x_ctx.shape[0] // tm
    const2 = lambda m: (0, 0)
    return pl.pallas_call(
        functools.partial(_conv_kernel, n_ctx_tiles=n_ctx_tiles, ctx_seq_len=ctx_seq_len,
                          lat_seq_len=lat_seq_len),
        out_shape=jax.ShapeDtypeStruct((m_rows, D_MODEL), F32),
        grid=(m_rows // tm,),
        in_specs=_both_streams_specs(x_ctx, x_lat, tm) + [
                  pl.BlockSpec((None, 1, 3 * D_MODEL),
                               _both_streams_mod_map(layer, n_ctx_tiles, tm, lat_seq_len)),
                  _weight_spec(w_in, j),
                  pl.BlockSpec((None, 3, D_MODEL), lambda m: (j, 0, 0)),
                  _weight_spec(w_out, j),
                  pl.BlockSpec((1, D_MODEL), const2),
                  pl.BlockSpec((1, D_MODEL), const2)],
        out_specs=pl.BlockSpec((tm, D_MODEL), lambda m: (m, 0)),
        compiler_params=_params(1),
        name="conv_layer",
    )(x_ctx, x_lat, mod3, w_in, conv_w, w_out, ln_g.reshape(1, D_MODEL), ln_b.reshape(1, D_MODEL))


def _separate(xp, xs, m_ctx):
    if xp[0] is xs[0]:
        return (xp[0][:m_ctx], 0), (xs[0][m_ctx:], 0)
    return xp, xs


def kernel(x_prompt, x_sample, cache_k, cache_v, state_fwd, state_bwd, c, c_ctx, w_mod, b_mod, ln_g,
           ln_b, w_in_a, lam_a, subln_a, w_out_a, w_in_b, decay_fwd, decay_bwd, w_out_b, w_in_c,
           conv_c, w_out_c):
    bp, tp, d = x_prompt.shape
    bs, ts, _ = x_sample.shape
    assert bs == CTX_ROW < MOD_ROWS
    mod3 = _modulation(c, c_ctx, w_mod, b_mod)
    rope_tables = _rope_tables(ts)

    m_ctx, m_lat = bp * tp, bs * ts
    xp = (x_prompt.reshape(m_ctx, d), 0)
    xs = (x_sample.reshape(m_lat, d), 0)
    n_attn = (DEPTH + N_MIXERS - 1) // N_MIXERS
    new_cache_k = new_cache_v = jax.ShapeDtypeStruct((bp, n_attn, tp, H_A, DV_A), F32)
    new_sf, new_sb = [], []
    for i in range(DEPTH):
        kind, j = i % N_MIXERS, i // N_MIXERS
        if kind == 0:
            x_ctx, new_cache_k, new_cache_v = _ctx_attention_layer(
                xp, m_ctx, tp, mod3, w_in_a, w_out_a, j, lam_a[j], subln_a[j], ln_g[i], ln_b[i], i,
                (new_cache_k, new_cache_v))
            qkvz = _attn_in(xs, m_lat, mod3, w_in_a, j, i, ts, rope_tables)
            xs = (_latent_attention(qkvz, xs, ts, (cache_k, cache_v, j), mod3, lam_a[j],
                                    subln_a[j], w_out_a, j, ln_g[i], ln_b[i], i), 0)
            xp = (x_ctx, 0)
        else:
            xp, xs = _separate(xp, xs, m_ctx)
            if kind == 1:
                q, kt, v, g = _ret_in(xp[0], xs[0], mod3, w_in_b, j, i, ts)
                x_ctx, s_f, s_b = _retention(
                    q, kt, v, g, tp, bp, 0, H_B, CTX_RET_SEQS, decay_fwd[j], decay_bwd[j], None,
                    out_proj=(xp[0], mod3, _mod_row_map(i, None, tp), w_out_b, j, ln_g[i],
                              ln_b[i]))
                new_sf.append(s_f)
                new_sb.append(s_b)
                (x_lat,) = _retention(
                    q, kt, v, g, ts, bs, m_ctx, LAT_RET_HEADS, 1, decay_fwd[j], decay_bwd[j],
                    (state_fwd, state_bwd, j),
                    out_proj=(xs[0], mod3, _mod_row_map(i, ts, ts), w_out_b, j, ln_g[i], ln_b[i]))
                xp, xs = (x_ctx, 0), (x_lat, 0)
            else:
                x_all = _conv_layer(xp[0], xs[0], mod3, w_in_c, conv_c, w_out_c, j, ln_g[i],
                                    ln_b[i], i, tp, ts)
                xp, xs = (x_all, 0), (x_all, m_ctx)
    xp, xs = _separate(xp, xs, m_ctx)
    y_prompt = xp[0].reshape(bp, tp, d)
    y_sample = xs[0].reshape(bs, ts, d)
    new_state_fwd = jnp.concatenate(new_sf, axis=1)
    new_state_bwd = jnp.concatenate(new_sb, axis=1)
    return (y_prompt, y_sample, new_cache_k, new_cache_v, new_state_fwd, new_state_bwd)
```

```python
import functools
import math

import jax
import jax.numpy as jnp
from jax import lax
from jax.experimental import pallas as pl
from jax.experimental.pallas import tpu as pltpu

F32 = jnp.float32
BF16 = jnp.bfloat16

D_MODEL = 1024
DEPTH = 4
N_MIXERS = 3
GRID_W = 64
H_A = 8
DH_A = 64
DV_A = 128
ROPE_HALF = DH_A // 2
SCORE_SCALE = DH_A ** -0.5 * math.log2(math.e)
H_B = 4
DK_B = 256
DV_B = 512
E_B = H_B * DV_B
CHUNK = 256
LAT_RET_HEADS = 2
CTX_RET_SEQS = 2
ALPHA = (2.0 * DEPTH) ** 0.25
ROPE_BASE = 10000.0
LN_EPS = 1e-5
RMS_EPS = 1e-6

MOD_ROWS = 8
CTX_ROW = 4
VMEM_LIMIT_BYTES = 58 * 1024 * 1024
ROW_TILE = 512


def _params(n_axes):
    return pltpu.CompilerParams(dimension_semantics=("arbitrary",) * n_axes,
                                vmem_limit_bytes=VMEM_LIMIT_BYTES)


def _silu(x):
    return x * jax.nn.sigmoid(x)


def _residual_layer_norm(x, out, gate, g, b):
    r = ALPHA * x + gate * out
    mu = jnp.mean(r, axis=-1, keepdims=True)
    d = r - mu
    var = jnp.mean(d * d, axis=-1, keepdims=True)
    return d * lax.rsqrt(var + LN_EPS) * g + b


def _modulated(x, mod_ref):
    shift = mod_ref[:, 0:D_MODEL]
    scale = mod_ref[:, D_MODEL:2 * D_MODEL]
    return x * (1.0 + scale) + shift


def _mod_row_map(layer, rows_per_batch, tile):
    if rows_per_batch is None:
        return lambda m, *_: (layer * MOD_ROWS + CTX_ROW, 0, 0)
    return lambda m, *_: (layer * MOD_ROWS + (m * tile) // rows_per_batch, 0, 0)


def _both_streams_specs(x_ctx, x_lat, tm):
    n_ctx_tiles = x_ctx.shape[0] // tm
    return [pl.BlockSpec((tm, D_MODEL), lambda m: (jnp.minimum(m, n_ctx_tiles - 1), 0)),
            pl.BlockSpec((tm, D_MODEL), lambda m: (jnp.maximum(m - n_ctx_tiles, 0), 0))]


def _both_streams_rows(x_ctx_ref, x_lat_ref, n_ctx_tiles):
    return jnp.where(pl.program_id(0) < n_ctx_tiles, x_ctx_ref[...], x_lat_ref[...])


def _both_streams_mod_map(layer, n_ctx_tiles, tm, lat_rows_per_batch):
    def index_map(m):
        lat_row = ((m - n_ctx_tiles) * tm) // lat_rows_per_batch
        return (layer * MOD_ROWS + jnp.where(m < n_ctx_tiles, CTX_ROW, lat_row), 0, 0)
    return index_map


def _weight_spec(w, j):
    return pl.BlockSpec((None,) + w.shape[1:], lambda *_: (j, 0, 0), pipeline_mode=pl.Buffered(1))


WEIGHT_CHUNK_COLS = 1024
WEIGHT_COPIES_IN_FLIGHT = 2


class _StagedWeight:
    @staticmethod
    def scratch(w, chunk_cols):
        return [pltpu.VMEM(w.shape[1:], w.dtype),
                pltpu.SemaphoreType.DMA((w.shape[2] // chunk_cols,))]

    def __init__(self, w_hbm_ref, w_vmem_ref, sem_ref, j, chunk_cols):
        self.copies = [
            pltpu.make_async_copy(w_hbm_ref.at[j, :, pl.ds(c * chunk_cols, chunk_cols)],
                                  w_vmem_ref.at[:, pl.ds(c * chunk_cols, chunk_cols)], sem_ref.at[c])
            for c in range(w_vmem_ref.shape[1] // chunk_cols)]

    def _arrive(self, c):
        self.copies[c].wait()
        if c + WEIGHT_COPIES_IN_FLIGHT < len(self.copies):
            self.copies[c + WEIGHT_COPIES_IN_FLIGHT].start()

    def run(self, body):
        first = pl.program_id(0) == 0

        @pl.when(first)
        def _():
            for cp in self.copies[:WEIGHT_COPIES_IN_FLIGHT]:
                cp.start()
            body(self._arrive)

        @pl.when(jnp.logical_not(first))
        def _():
            body(lambda c: None)


def _mod_kernel(c_ref, cctx_ref, w_ref, b_ref, o_ref):
    pad = jnp.zeros((MOD_ROWS - c_ref.shape[0] - 1, D_MODEL), F32)
    s = _silu(jnp.concatenate([c_ref[...], cctx_ref[...], pad], axis=0))
    m = jnp.dot(s, w_ref[...], preferred_element_type=F32) + b_ref[...]
    for r in range(MOD_ROWS):
        o_ref[r] = m[r:r + 1]


def _modulation(c, c_ctx, w_mod, b_mod):
    n = 3 * D_MODEL
    return pl.pallas_call(
        _mod_kernel,
        out_shape=jax.ShapeDtypeStruct((DEPTH * MOD_ROWS, 1, n), F32),
        grid=(DEPTH,),
        in_specs=[pl.BlockSpec(c.shape, lambda i: (0, 0)),
                  pl.BlockSpec((1, D_MODEL), lambda i: (0, 0)),
                  pl.BlockSpec((None, D_MODEL, n), lambda i: (i, 0, 0)),
                  pl.BlockSpec((None, 1, n), lambda i: (i, 0, 0))],
        out_specs=pl.BlockSpec((MOD_ROWS, 1, n), lambda i: (i, 0, 0)),
        compiler_params=_params(1),
        name="modulation",
    )(c, c_ctx.reshape(1, D_MODEL), w_mod, b_mod.reshape(DEPTH, 1, n))


def _rope(xh, cos4, sin4, first_half):
    swapped = jnp.where(first_half, pltpu.roll(xh, DV_A - ROPE_HALF, 1),
                        pltpu.roll(xh, ROPE_HALF, 1))
    return xh * cos4 + swapped * sin4


def _store_heads(o_ref, x, seq_len):
    for b in range(o_ref.shape[0]):
        xb = x[b * seq_len:(b + 1) * seq_len].reshape(seq_len, H_A, DV_A)
        if len(o_ref.shape) == 4:
            o_ref[b] = xb
        else:
            o_ref[b, 0] = xb
            for s in range(1, o_ref.shape[1]):
                o_ref[b, s] = jnp.zeros_like(xb)


def _attn_in_body(x_ref, mod_ref, w_ref, rope_refs, q_ref, k_ref, vt_ref, z_ref, cache_refs, seq_len):
    h = _modulated(x_ref[...], mod_ref)
    if rope_refs is not None:
        lane = lax.broadcasted_iota(jnp.int32, (1, DV_A), 1)
        first_half = (lane % DH_A) < ROPE_HALF
        cos4 = rope_refs[0][...]
        sin4 = rope_refs[1][...]
    q_all = jnp.dot(h, w_ref[:, 0:D_MODEL], preferred_element_type=F32)
    k_all = jnp.dot(h, w_ref[:, D_MODEL:2 * D_MODEL], preferred_element_type=F32)
    if cache_refs is not None:
        _store_heads(cache_refs[0], k_all, seq_len)
    for hd in range(H_A):
        cols = slice(hd * DV_A, (hd + 1) * DV_A)
        q = q_all[:, cols]
        k = k_all[:, cols]
        if rope_refs is not None:
            q = _rope(q, cos4, sin4, first_half)
            k = _rope(k, cos4, sin4, first_half)
        q_ref[:, cols] = (q * SCORE_SCALE).astype(BF16)
        k_ref[:, cols] = k.astype(BF16)
    v = jnp.dot(h, w_ref[:, 2 * D_MODEL:3 * D_MODEL], preferred_element_type=F32)
    if cache_refs is not None:
        _store_heads(cache_refs[1], v, seq_len)
    vt_ref[...] = v.T.astype(BF16)
    z_ref[...] = jnp.dot(h, w_ref[:, 3 * D_MODEL:4 * D_MODEL],
                         preferred_element_type=F32).astype(z_ref.dtype)


def _attn_in_kernel(x_ref, mod_ref, w_ref, cos_ref, sin_ref, q_ref, k_ref, vt_ref, z_ref):
    _attn_in_body(x_ref, mod_ref, w_ref, (cos_ref, sin_ref), q_ref, k_ref, vt_ref, z_ref, None, None)


def _attn_in(x, m_rows, mod3, w_in, j, layer, rows_per_batch, rope_tables):
    x2d, x_first_row = x
    tm = ROW_TILE
    seq_tiles = rows_per_batch // tm
    row_spec = pl.BlockSpec((tm, D_MODEL), lambda m: (m, 0))
    tab_spec = pl.BlockSpec((tm, DV_A), lambda m: (m % seq_tiles, 0))
    return pl.pallas_call(
        _attn_in_kernel,
        out_shape=[jax.ShapeDtypeStruct((m_rows, D_MODEL), BF16),
                   jax.ShapeDtypeStruct((m_rows, D_MODEL), BF16),
                   jax.ShapeDtypeStruct((D_MODEL, m_rows), BF16),
                   jax.ShapeDtypeStruct((m_rows, D_MODEL), BF16)],
        grid=(m_rows // tm,),
        in_specs=[pl.BlockSpec((tm, D_MODEL), lambda m: (x_first_row // tm + m, 0)),
                  pl.BlockSpec((None, 1, 3 * D_MODEL), _mod_row_map(layer, rows_per_batch, tm)),
                  _weight_spec(w_in, j), tab_spec, tab_spec],
        out_specs=[row_spec, row_spec, pl.BlockSpec((D_MODEL, tm), lambda m: (0, m)), row_spec],
        compiler_params=_params(1),
        name="attn_in_rope",
    )(x2d, mod3, w_in, *rope_tables)


ONES_ROWS = 16
CTX_ATTN_SEQS = 2
ATTN_ROWS = 512


def _slab_reduce(op, x):
    parts = [x[i:i + 8] for i in range(0, x.shape[0], 8)]
    while len(parts) > 1:
        parts = [op(parts[i], parts[i + 1]) for i in range(0, len(parts) - 1, 2)] + (
            [parts[-1]] if len(parts) % 2 else [])
    return parts[0]


def _attn_kernel(*refs, layer_idx, has_ctx, n_seq, shared_keys):
    if has_ctx:
        (q_ref, k_ref, vt_ref, z_ref, x_ref, kc_ref, vc_ref, mod_ref, lam_ref, subln_ref,
         w_ref, g_ref, b_ref, o_ref, y_ref, kcb_ref, vct_ref) = refs

        @pl.when(pl.program_id(1) == 0)
        def _():
            n_ctx = kc_ref.shape[0]
            kcb_ref[...] = kc_ref[...].reshape(n_ctx, D_MODEL).astype(BF16)
            vct_ref[...] = vc_ref[...].reshape(n_ctx, D_MODEL).T.astype(BF16)
    else:
        (q_ref, k_ref, vt_ref, z_ref, x_ref, mod_ref, lam_ref, subln_ref,
         w_ref, g_ref, b_ref, o_ref, y_ref) = refs
    tq = q_ref.shape[0] // n_seq
    lam_init = 0.8 - 0.6 * math.exp(-0.3 * layer_idx)
    lm = lam_ref[...]
    lam = (jnp.exp(jnp.sum(lm[0:1] * lm[1:2], axis=-1, keepdims=True))
           - jnp.exp(jnp.sum(lm[2:3] * lm[3:4], axis=-1, keepdims=True)) + lam_init)
    lane = lax.broadcasted_iota(jnp.int32, (1, DV_A), 1)
    first = lane < DH_A
    subln = jnp.broadcast_to(subln_ref[...], (DV_A, tq))
    nt = (((1,), (1,)), ((), ()))
    t = k_ref.shape[0] if shared_keys else k_ref.shape[0] // n_seq

    def key_rows(sq):
        return slice(0, t) if shared_keys else slice(sq * t, (sq + 1) * t)

    def scores(unit):
        sq, hd = unit
        cols = slice(hd * DV_A, (hd + 1) * DV_A)
        qh = q_ref[sq * tq:(sq + 1) * tq, cols]
        zero = jnp.zeros_like(qh)
        qq = jnp.concatenate([jnp.where(first, qh, zero), jnp.where(first, zero, qh)], axis=0)
        parts = [lax.dot_general(k_ref[key_rows(sq), cols], qq, nt, preferred_element_type=F32)]
        if has_ctx:
            parts.append(lax.dot_general(kcb_ref[:, cols], qq, nt, preferred_element_type=F32))
        return parts

    def exps(parts):
        m8 = functools.reduce(jnp.maximum, [_slab_reduce(jnp.maximum, s) for s in parts])
        m = jnp.max(m8, axis=0, keepdims=True)
        return [jnp.exp2(s - m).astype(BF16) for s in parts]

    def with_ones(vals_t):
        return jnp.concatenate([vals_t, jnp.ones((ONES_ROWS, vals_t.shape[1]), BF16)], axis=0)

    def finish(unit, es):
        sq, hd = unit
        cols = slice(hd * DV_A, (hd + 1) * DV_A)
        rows = slice(sq * tq, (sq + 1) * tq)
        acc = jnp.dot(with_ones(vt_ref[cols, key_rows(sq)]), es[0], preferred_element_type=F32)
        if has_ctx:
            acc = acc + jnp.dot(with_ones(vct_ref[cols, :]), es[1], preferred_element_type=F32)
        inv = 1.0 / acc[DV_A:DV_A + 1, :]
        ot = acc[:DV_A, :tq] * inv[:, :tq] - acc[:DV_A, tq:] * (inv[:, tq:] * lam)
        ot = ot * lax.rsqrt(jnp.mean(ot * ot, axis=0, keepdims=True) + RMS_EPS)
        ot = ot * subln * (1.0 - lam_init)
        y_ref[rows, cols] = (ot.T * _silu(z_ref[rows, cols].astype(F32))).astype(BF16)

    units = [(sq, hd) for hd in range(H_A) for sq in range(n_seq)]
    s_ahead = {u: scores(units[u]) for u in range(min(2, len(units)))}
    e_ahead = {0: exps(s_ahead.pop(0))}
    for u in range(len(units)):
        if u + 2 < len(units):
            s_ahead[u + 2] = scores(units[u + 2])
        if u + 1 < len(units):
            e_ahead[u + 1] = exps(s_ahead.pop(u + 1))
        finish(units[u], e_ahead.pop(u))
    out = jnp.dot(y_ref[...], w_ref[...].astype(BF16), preferred_element_type=F32)
    gate = mod_ref[:, 2 * D_MODEL:3 * D_MODEL]
    o_ref[...] = _residual_layer_norm(x_ref[...], out, gate, g_ref[...], b_ref[...])


def _latent_attention(qkvz, x, seq_len, ctx, mod3, lam, subln, w_out, j, ln_g, ln_b, layer):
    q, k, vt, z = qkvz
    m_rows = q.shape[0]
    t = seq_len
    rows = ATTN_ROWS
    nq = t // rows
    x, x_first_row = x
    x_spec = pl.BlockSpec((rows, D_MODEL), lambda i, j: (x_first_row // rows + i * nq + j, 0))
    q_spec = pl.BlockSpec((rows, D_MODEL), lambda i, j: (i * nq + j, 0))
    k_spec = pl.BlockSpec((t, D_MODEL), lambda i, j: (i, 0))
    vt_spec = pl.BlockSpec((D_MODEL, t), lambda i, j: (0, i))
    cache_k, cache_v, slot = ctx
    n_ctx = cache_k.shape[2]
    c_spec = pl.BlockSpec((None, None, n_ctx, H_A, DV_A), lambda i, j: (i, slot, 0, 0, 0))
    const2 = lambda i, j: (0, 0)
    return pl.pallas_call(
        functools.partial(_attn_kernel, layer_idx=layer, has_ctx=True, n_seq=1, shared_keys=True),
        out_shape=jax.ShapeDtypeStruct((m_rows, D_MODEL), F32),
        grid=(m_rows // t, nq),
        in_specs=[q_spec, k_spec, vt_spec, q_spec, x_spec, c_spec, c_spec,
                  pl.BlockSpec((None, 1, 3 * D_MODEL), lambda i, j: (layer * MOD_ROWS + i, 0, 0)),
                  pl.BlockSpec((4, DH_A), const2),
                  pl.BlockSpec((DV_A, 1), const2),
                  _weight_spec(w_out, j),
                  pl.BlockSpec((1, D_MODEL), const2),
                  pl.BlockSpec((1, D_MODEL), const2)],
        out_specs=q_spec,
        scratch_shapes=[pltpu.VMEM((rows, D_MODEL), BF16), pltpu.VMEM((n_ctx, D_MODEL), BF16),
                        pltpu.VMEM((D_MODEL, n_ctx), BF16)],
        compiler_params=_params(2),
        name="diff_attn_ctx",
    )(q, k, vt, z, x, cache_k, cache_v, mod3, lam, subln.reshape(DV_A, 1), w_out,
      ln_g.reshape(1, D_MODEL), ln_b.reshape(1, D_MODEL))


def _ctx_attn_layer_kernel(*refs, layer_idx, seq_len, n_aliased):
    x_ref, mod_ref, w_in_ref, lam_ref, subln_ref, w_out_ref, g_ref, b_ref = refs[:8]
    o_ref, ck_ref, cv_ref, q_ref, k_ref, vt_ref, z_ref, y_ref = refs[8 + n_aliased:]
    _attn_in_body(x_ref, mod_ref, w_in_ref, None, q_ref, k_ref, vt_ref, z_ref, (ck_ref, cv_ref),
                  seq_len)
    _attn_kernel(q_ref, k_ref, vt_ref, z_ref, x_ref, mod_ref, lam_ref, subln_ref, w_out_ref, g_ref,
                 b_ref, o_ref, y_ref, layer_idx=layer_idx, has_ctx=False,
                 n_seq=x_ref.shape[0] // seq_len, shared_keys=False)


def _ctx_attention_layer(x, m_rows, seq_len, mod3, w_in, w_out, j, lam, subln, ln_g, ln_b, layer,
                         cache_out):
    x2d, x_first_row = x
    rows = CTX_ATTN_SEQS * seq_len
    const2 = lambda m: (0, 0)
    in_specs = [pl.BlockSpec((rows, D_MODEL), lambda m: (x_first_row // rows + m, 0)),
                pl.BlockSpec((None, 1, 3 * D_MODEL), lambda m: (layer * MOD_ROWS + CTX_ROW, 0, 0)),
                _weight_spec(w_in, j),
                pl.BlockSpec((4, DH_A), const2),
                pl.BlockSpec((DV_A, 1), const2),
                _weight_spec(w_out, j),
                pl.BlockSpec((1, D_MODEL), const2),
                pl.BlockSpec((1, D_MODEL), const2)]
    args = [x2d, mod3, w_in, lam, subln.reshape(DV_A, 1), w_out, ln_g.reshape(1, D_MODEL),
            ln_b.reshape(1, D_MODEL)]
    new_k, new_v = cache_out
    cache_shape = jax.ShapeDtypeStruct(new_k.shape, F32)
    aliases = {}
    if j == 0:
        cache_spec = pl.BlockSpec((CTX_ATTN_SEQS, new_k.shape[1], seq_len, H_A, DV_A),
                                  lambda m: (m, 0, 0, 0, 0))
    else:
        cache_spec = pl.BlockSpec((CTX_ATTN_SEQS, None, seq_len, H_A, DV_A),
                                  lambda m: (m, j, 0, 0, 0))
        in_specs += [pl.BlockSpec(memory_space=pl.ANY)] * 2
        args += [new_k, new_v]
        aliases = {len(args) - 2: 1, len(args) - 1: 2}
    return pl.pallas_call(
        functools.partial(_ctx_attn_layer_kernel, layer_idx=layer, seq_len=seq_len,
                          n_aliased=len(aliases)),
        out_shape=[jax.ShapeDtypeStruct((m_rows, D_MODEL), F32), cache_shape, cache_shape],
        grid=(m_rows // rows,),
        in_specs=in_specs,
        out_specs=[pl.BlockSpec((rows, D_MODEL), lambda m: (m, 0)), cache_spec, cache_spec],
        scratch_shapes=[pltpu.VMEM((rows, D_MODEL), BF16), pltpu.VMEM((rows, D_MODEL), BF16),
                        pltpu.VMEM((D_MODEL, rows), BF16), pltpu.VMEM((rows, D_MODEL), BF16),
                        pltpu.VMEM((rows, D_MODEL), BF16)],
        input_output_aliases=aliases,
        compiler_params=_params(1),
        name="ctx_attn_layer",
    )(*args)


def _rope_tables(n_tokens):
    rows = n_tokens // GRID_W
    r = jnp.repeat(jnp.arange(rows, dtype=F32), GRID_W)
    col = jnp.tile(jnp.arange(GRID_W, dtype=F32), rows)
    n_freq = DH_A // 4
    inv = ROPE_BASE ** (-jnp.arange(n_freq, dtype=F32) / n_freq)
    ang = jnp.concatenate([r[:, None] * inv, col[:, None] * inv], -1)
    cos, sin = jnp.cos(ang), jnp.sin(ang)
    return jnp.tile(cos, (1, 4)), jnp.concatenate([-sin, sin, -sin, sin], -1)


def _ret_in_kernel(xc_ref, xl_ref, mod_ref, w_hbm_ref, q_ref, kt_ref, v_ref, g_ref, w_ref, sem_ref, *,
                   n_ctx_tiles, j):
    nq = H_B * DK_B
    cc = WEIGHT_CHUNK_COLS
    assert nq == cc and E_B % cc == 0
    staged = _StagedWeight(w_hbm_ref, w_ref, sem_ref, j, cc)

    def body(arrive):
        h = _modulated(_both_streams_rows(xc_ref, xl_ref, n_ctx_tiles), mod_ref)
        arrive(0)
        q_ref[...] = jnp.dot(h, w_ref[:, 0:nq], preferred_element_type=F32).astype(BF16)
        arrive(1)
        k = jnp.dot(h, w_ref[:, nq:2 * nq], preferred_element_type=F32) * (DK_B ** -0.5)
        kt_ref[...] = k.T
        for c in range(E_B // cc):
            arrive(2 + c)
            v = jnp.dot(h, w_ref[:, 2 * nq + c * cc:2 * nq + (c + 1) * cc],
                        preferred_element_type=F32)
            v_ref[:, c * cc:(c + 1) * cc] = v.astype(BF16)
        for c in range(E_B // cc):
            arrive(2 + E_B // cc + c)
            g_ref[:, c * cc:(c + 1) * cc] = jnp.dot(
                h, w_ref[:, 2 * nq + E_B + c * cc:2 * nq + E_B + (c + 1) * cc],
                preferred_element_type=F32).astype(BF16)

    staged.run(body)


def _ret_in(x_ctx, x_lat, mod3, w_in, j, layer, lat_rows_per_batch):
    m_rows = x_ctx.shape[0] + x_lat.shape[0]
    tm = ROW_TILE
    nq = H_B * DK_B
    n_ctx_tiles = x_ctx.shape[0] // tm
    return pl.pallas_call(
        functools.partial(_ret_in_kernel, n_ctx_tiles=n_ctx_tiles, j=j),
        out_shape=[jax.ShapeDtypeStruct((m_rows, nq), BF16),
                   jax.ShapeDtypeStruct((nq, m_rows), F32),
                   jax.ShapeDtypeStruct((m_rows, E_B), BF16),
                   jax.ShapeDtypeStruct((m_rows, E_B), BF16)],
        grid=(m_rows // tm,),
        in_specs=_both_streams_specs(x_ctx, x_lat, tm) + [
            pl.BlockSpec((None, 1, 3 * D_MODEL),
                         _both_streams_mod_map(layer, n_ctx_tiles, tm, lat_rows_per_batch)),
            pl.BlockSpec(memory_space=pl.ANY)],
        out_specs=[pl.BlockSpec((tm, nq), lambda m: (m, 0)),
                   pl.BlockSpec((nq, tm), lambda m: (0, m)),
                   pl.BlockSpec((tm, E_B), lambda m: (m, 0)),
                   pl.BlockSpec((tm, E_B), lambda m: (m, 0))],
        scratch_shapes=_StagedWeight.scratch(w_in, WEIGHT_CHUNK_COLS),
        compiler_params=_params(1),
        name="ret_in",
    )(x_ctx, x_lat, mod3, w_in)


RING_SLOTS = 3


def _ring_fetch(ring_ref, sem_ref, step, n_steps, src_at):
    def copy(s):
        slot = s % RING_SLOTS
        return pltpu.make_async_copy(src_at(s), ring_ref.at[slot], sem_ref.at[slot])

    @pl.when(step == 0)
    def _():
        for s in range(min(RING_SLOTS - 1, n_steps)):
            copy(s).start()

    @pl.when(step + (RING_SLOTS - 1) < n_steps)
    def _():
        copy(step + (RING_SLOTS - 1)).start()

    copy(step).wait()
    return ring_ref.at[step % RING_SLOTS]


def _retention_kernel(*refs, has_state, heads, seqs, fused_out, ring=None):
    refs = list(refs)
    q_ref, kt_ref, v_ref, g_ref, af_ref, ab_ref = refs[:6]
    del refs[:6]
    if ring is not None:
        first_block, n_steps = ring
        g_sem, g_ring, v_sem, v_ring, kt_sem, kt_ring = [refs.pop() for _ in range(6)]
        rows = q_ref.shape[0]
        step = pl.program_id(0)
        kt_ref = _ring_fetch(kt_ring, kt_sem, step, n_steps,
                             lambda s, src=kt_ref: src.at[:, pl.ds((first_block + s) * rows, rows)])
        v_ref = _ring_fetch(v_ring, v_sem, step, n_steps,
                            lambda s, src=v_ref: src.at[pl.ds((first_block + s) * rows, rows), :])
        g_ref = _ring_fetch(g_ring, g_sem, step, n_steps,
                            lambda s, src=g_ref: src.at[pl.ds((first_block + s) * rows, rows), :])
    if has_state:
        s0f_ref, s0b_ref = refs[:2]
        del refs[:2]
    if fused_out:
        x_ref, mod_ref, w_ref, lng_ref, lnb_ref = refs[:5]
        del refs[:5]
        o_ref = refs.pop(0)
        y_ref = refs.pop()
    else:
        y_ref = refs.pop(0)
    if not has_state:
        sf_ref, sb_ref = refs
    t = q_ref.shape[0] // seqs
    nc = t // CHUNK
    row = lax.broadcasted_iota(jnp.int32, (CHUNK, CHUNK), 0).astype(F32)
    col = lax.broadcasted_iota(jnp.int32, (CHUNK, CHUNK), 1).astype(F32)
    diff = row - col
    idx_col = lax.broadcasted_iota(jnp.int32, (CHUNK, 1), 0).astype(F32)
    idx_row = lax.broadcasted_iota(jnp.int32, (1, CHUNK), 1).astype(F32)

    def state(ref, sq, hh):
        return ref.at[hh] if seqs == 1 else ref.at[sq, hh]

    for hh, sq in [(hh, sq) for hh in range(heads) for sq in range(seqs)]:
        def chunk(c, sq=sq):
            return slice(sq * t + c * CHUNK, sq * t + (c + 1) * CHUNK)

        qk_cols = slice(hh * DK_B, (hh + 1) * DK_B)
        v_cols = slice(hh * DV_B, (hh + 1) * DV_B)
        lg_f = jnp.log1p(-jnp.exp(af_ref[hh]))
        lg_b = jnp.log1p(-jnp.exp(ab_ref[hh]))
        dmask = (jnp.where(diff >= 0, jnp.exp(jnp.maximum(diff, 0.0) * lg_f), 0.0)
                 + jnp.where(diff <= 0, jnp.exp(jnp.maximum(-diff, 0.0) * lg_b), 0.0))
        qd_f = jnp.exp((idx_col + 1.0) * lg_f)
        qd_b = jnp.exp((CHUNK - idx_col) * lg_b)
        kd_f = jnp.exp((CHUNK - 1.0 - idx_row) * lg_f)
        kd_b = jnp.exp(idx_row * lg_b)
        cd_f = jnp.exp(CHUNK * lg_f)
        cd_b = jnp.exp(CHUNK * lg_b)

        def states(order, kd, cd, s):
            seen = {}
            for n, c in enumerate(order):
                seen[c] = None if s is None else s.astype(BF16)
                if has_state and n == nc - 1:
                    return seen, None
                u = jnp.dot((kt_ref[qk_cols, chunk(c)] * kd).astype(BF16), v_ref[chunk(c), v_cols],
                            preferred_element_type=F32)
                s = u if s is None else s * cd + u
            return seen, s

        seen_f, s_f = states(range(nc), kd_f, cd_f,
                             state(s0f_ref, sq, hh)[...] if has_state else None)
        seen_b, s_b = states(range(nc - 1, -1, -1), kd_b, cd_b,
                             state(s0b_ref, sq, hh)[...] if has_state else None)
        if not has_state:
            state(sf_ref, sq, hh)[...] = s_f
            state(sb_ref, sq, hh)[...] = s_b
        for c in range(nc):
            qc = q_ref[chunk(c), qk_cols]
            qk = jnp.dot(qc, kt_ref[qk_cols, chunk(c)].astype(BF16), preferred_element_type=F32)
            o = jnp.dot((qk * dmask).astype(BF16), v_ref[chunk(c), v_cols],
                        preferred_element_type=F32)
            if seen_f[c] is not None:
                o = o + jnp.dot(qc, seen_f[c], preferred_element_type=F32) * qd_f
            if seen_b[c] is not None:
                o = o + jnp.dot(qc, seen_b[c], preferred_element_type=F32) * qd_b
            o = o * lax.rsqrt(jnp.mean(o * o, axis=-1, keepdims=True) + RMS_EPS)
            gate_pre = g_ref[chunk(c), v_cols].astype(F32)
            y_ref[chunk(c), v_cols] = (o * _silu(gate_pre)).astype(BF16)
    if fused_out:
        part = jnp.dot(y_ref[...], w_ref[...].astype(BF16), preferred_element_type=F32)
        gate = mod_ref[:, 2 * D_MODEL:3 * D_MODEL]
        head_steps = H_B // heads
        if head_steps == 1:
            o_ref[...] = _residual_layer_norm(x_ref[...], part, gate, lng_ref[...], lnb_ref[...])
        else:
            step = pl.program_id(1)

            @pl.when(step == 0)
            def _():
                o_ref[...] = part

            @pl.when(jnp.logical_and(step > 0, step < head_steps - 1))
            def _():
                o_ref[...] += part

            @pl.when(step == head_steps - 1)
            def _():
                o_ref[...] = _residual_layer_norm(x_ref[...], o_ref[...] + part, gate, lng_ref[...],
                                                  lnb_ref[...])


def _retention(q, kt, v, g, seq_len, n_seq, first_row, heads, seqs, decay_f, decay_b, states,
               out_proj=None):
    t = seqs * seq_len
    b = n_seq // seqs
    s0 = first_row // t
    has_state = states is not None
    fused_out = out_proj is not None
    seq_dim = None if seqs == 1 else seqs
    q_spec = pl.BlockSpec((t, heads * DK_B), lambda i, h: (s0 + i, h))
    kt_spec = pl.BlockSpec((heads * DK_B, t), lambda i, h: (h, s0 + i))
    vg_spec = pl.BlockSpec((t, heads * DV_B), lambda i, h: (s0 + i, h))
    a_spec = pl.BlockSpec((heads, 1, 1), lambda i, h: (h, 0, 0))
    in_specs = [q_spec, kt_spec, vg_spec, vg_spec, a_spec, a_spec]
    args = [q, kt, v, g, decay_f.reshape(H_B, 1, 1), decay_b.reshape(H_B, 1, 1)]
    out_shape = [jax.ShapeDtypeStruct((b * t, E_B), BF16)]
    out_specs = [pl.BlockSpec((t, heads * DV_B), lambda i, h: (i, h))]
    if has_state:
        s_f, s_b, jj = states
        s_spec = pl.BlockSpec((seq_dim, None, heads, DK_B, DV_B), lambda i, h: (i, jj, h, 0, 0))
        in_specs += [s_spec, s_spec]
        args += [s_f, s_b]
    scratch = []
    if fused_out:
        x2d, mod3, mod_map, w_out, jw, ln_g, ln_b = out_proj
        x_spec = pl.BlockSpec((t, D_MODEL), lambda i, h: (i, 0))
        const2 = lambda i, h: (0, 0)
        if heads == H_B:
            w_spec = _weight_spec(w_out, jw)
        else:
            w_spec = pl.BlockSpec((None, heads * DV_B, D_MODEL), lambda i, h: (jw, h, 0))
        in_specs += [x_spec, pl.BlockSpec((None, 1, 3 * D_MODEL), mod_map), w_spec,
                     pl.BlockSpec((1, D_MODEL), const2), pl.BlockSpec((1, D_MODEL), const2)]
        args += [x2d, mod3, w_out, ln_g.reshape(1, D_MODEL), ln_b.reshape(1, D_MODEL)]
        out_shape = [jax.ShapeDtypeStruct((b * t, D_MODEL), F32)]
        out_specs = [x_spec]
        scratch = [pltpu.VMEM((t, heads * DV_B), BF16)]
    if not has_state:
        so_spec = pl.BlockSpec((seq_dim, None, heads, DK_B, DV_B), lambda i, h: (i, 0, h, 0, 0))
        out_shape += [jax.ShapeDtypeStruct((n_seq, 1, H_B, DK_B, DV_B), F32)] * 2
        out_specs += [so_spec, so_spec]
    ring = None
    if heads == H_B and fused_out and not has_state:
        ring = (s0, b)
        in_specs[1:4] = [pl.BlockSpec(memory_space=pl.ANY)] * 3
        for shape, dtype in (((H_B * DK_B, t), F32), ((t, E_B), BF16), ((t, E_B), BF16)):
            scratch += [pltpu.VMEM((RING_SLOTS,) + shape, dtype),
                        pltpu.SemaphoreType.DMA((RING_SLOTS,))]
    return pl.pallas_call(
        functools.partial(_retention_kernel, has_state=has_state, heads=heads, seqs=seqs,
                          fused_out=fused_out, ring=ring),
        out_shape=out_shape,
        grid=(b, H_B // heads),
        in_specs=in_specs,
        out_specs=out_specs,
        scratch_shapes=scratch,
        compiler_params=_params(2),
        name="retention_state" if has_state else "retention",
    )(*args)


CONV_ROWS = 1024
CONV_COLS = 256


def _conv_kernel(xc_ref, xl_ref, mod_ref, w_in_ref, cw_ref, w_out_ref, g_ref, b_ref, o_ref, *,
                 n_ctx_tiles, ctx_seq_len, lat_seq_len):
    x = _both_streams_rows(xc_ref, xl_ref, n_ctx_tiles)
    h = _modulated(x, mod_ref)
    rows = x.shape[0]
    is_ctx = pl.program_id(0) < n_ctx_tiles
    row = lax.broadcasted_iota(jnp.int32, (rows, 1), 0)
    pos = jnp.where(is_ctx, row % ctx_seq_len, row % lat_seq_len)
    has_prev = pos > 0
    has_next = pos < jnp.where(is_ctx, ctx_seq_len - 1, lat_seq_len - 1)
    e = D_MODEL
    for c in range(e // CONV_COLS):
        cols = slice(c * CONV_COLS, (c + 1) * CONV_COLS)

        def proj(part):
            lo = part * e + c * CONV_COLS
            return jnp.dot(h, w_in_ref[:, lo:lo + CONV_COLS], preferred_element_type=F32)

        p = proj(1) * proj(2)
        prev = jnp.where(has_prev, pltpu.roll(p, 1, 0), 0.0)
        nxt = jnp.where(has_next, pltpu.roll(p, rows - 1, 0), 0.0)
        conv = prev * cw_ref[0:1, cols] + p * cw_ref[1:2, cols] + nxt * cw_ref[2:3, cols]
        y = (proj(0) * conv * _silu(proj(3))).astype(BF16)
        part = jnp.dot(y, w_out_ref[cols, :].astype(BF16), preferred_element_type=F32)
        if c == 0:
            o_ref[...] = part
        else:
            o_ref[...] += part
    gate = mod_ref[:, 2 * D_MODEL:3 * D_MODEL]
    o_ref[...] = _residual_layer_norm(x, o_ref[...], gate, g_ref[...], b_ref[...])


def _conv_layer(x_ctx, x_lat, mod3, w_in, conv_w, w_out, j, ln_g, ln_b, layer, ctx_seq_len,
                lat_seq_len):
    m_rows = x_ctx.shape[0] + x_lat.shape[0]
    tm = CONV_ROWS
    n_ctx_tiles = x_ctx.shape[0] // tm
    const2 = lambda m: (0, 0)
    return pl.pallas_call(
        functools.partial(_conv_kernel, n_ctx_tiles=n_ctx_tiles, ctx_seq_len=ctx_seq_len,
                          lat_seq_len=lat_seq_len),
        out_shape=jax.ShapeDtypeStruct((m_rows, D_MODEL), F32),
        grid=(m_rows // tm,),
        in_specs=_both_streams_specs(x_ctx, x_lat, tm) + [
                  pl.BlockSpec((None, 1, 3 * D_MODEL),
                               _both_streams_mod_map(layer, n_ctx_tiles, tm, lat_seq_len)),
                  _weight_spec(w_in, j),
                  pl.BlockSpec((None, 3, D_MODEL), lambda m: (j, 0, 0)),
                  _weight_spec(w_out, j),
                  pl.BlockSpec((1, D_MODEL), const2),
                  pl.BlockSpec((1, D_MODEL), const2)],
        out_specs=pl.BlockSpec((tm, D_MODEL), lambda m: (m, 0)),
        compiler_params=_params(1),
        name="conv_layer",
    )(x_ctx, x_lat, mod3, w_in, conv_w, w_out, ln_g.reshape(1, D_MODEL), ln_b.reshape(1, D_MODEL))


def _separate(xp, xs, m_ctx):
    if xp[0] is xs[0]:
        return (xp[0][:m_ctx], 0), (xs[0][m_ctx:], 0)
    return xp, xs


def kernel(x_prompt, x_sample, cache_k, cache_v, state_fwd, state_bwd, c, c_ctx, w_mod, b_mod, ln_g,
           ln_b, w_in_a, lam_a, subln_a, w_out_a, w_in_b, decay_fwd, decay_bwd, w_out_b, w_in_c,
           conv_c, w_out_c):
    bp, tp, d = x_prompt.shape
    bs, ts, _ = x_sample.shape
    assert bs == CTX_ROW < MOD_ROWS
    mod3 = _modulation(c, c_ctx, w_mod, b_mod)
    rope_tables = _rope_tables(ts)

    m_ctx, m_lat = bp * tp, bs * ts
    xp = (x_prompt.reshape(m_ctx, d), 0)
    xs = (x_sample.reshape(m_lat, d), 0)
    n_attn = (DEPTH + N_MIXERS - 1) // N_MIXERS
    new_cache_k = new_cache_v = jax.ShapeDtypeStruct((bp, n_attn, tp, H_A, DV_A), F32)
    new_sf, new_sb = [], []
    for i in range(DEPTH):
        kind, j = i % N_MIXERS, i // N_MIXERS
        if kind == 0:
            x_ctx, new_cache_k, new_cache_v = _ctx_attention_layer(
                xp, m_ctx, tp, mod3, w_in_a, w_out_a, j, lam_a[j], subln_a[j], ln_g[i], ln_b[i], i,
                (new_cache_k, new_cache_v))
            qkvz = _attn_in(xs, m_lat, mod3, w_in_a, j, i, ts, rope_tables)
            xs = (_latent_attention(qkvz, xs, ts, (cache_k, cache_v, j), mod3, lam_a[j],
                                    subln_a[j], w_out_a, j, ln_g[i], ln_b[i], i), 0)
            xp = (x_ctx, 0)
        else:
            xp, xs = _separate(xp, xs, m_ctx)
            if kind == 1:
                q, kt, v, g = _ret_in(xp[0], xs[0], mod3, w_in_b, j, i, ts)
                x_ctx, s_f, s_b = _retention(
                    q, kt, v, g, tp, bp, 0, H_B, CTX_RET_SEQS, decay_fwd[j], decay_bwd[j], None,
                    out_proj=(xp[0], mod3, _mod_row_map(i, None, tp), w_out_b, j, ln_g[i],
                              ln_b[i]))
                new_sf.append(s_f)
                new_sb.append(s_b)
                (x_lat,) = _retention(
                    q, kt, v, g, ts, bs, m_ctx, LAT_RET_HEADS, 1, decay_fwd[j], decay_bwd[j],
                    (state_fwd, state_bwd, j),
                    out_proj=(xs[0], mod3, _mod_row_map(i, ts, ts), w_out_b, j, ln_g[i], ln_b[i]))
                xp, xs = (x_ctx, 0), (x_lat, 0)
            else:
                x_all = _conv_layer(xp[0], xs[0], mod3, w_in_c, conv_c, w_out_c, j, ln_g[i],
                                    ln_b[i], i, tp, ts)
                xp, xs = (x_all, 0), (x_all, m_ctx)
    xp, xs = _separate(xp, xs, m_ctx)
    y_prompt = xp[0].reshape(bp, tp, d)
    y_sample = xs[0].reshape(bs, ts, d)
    new_state_fwd = jnp.concatenate(new_sf, axis=1)
    new_state_bwd = jnp.concatenate(new_sb, axis=1)
    return (y_prompt, y_sample, new_cache_k, new_cache_v, new_state_fwd, new_state_bwd)
```

```python
import functools
import math

import jax
import jax.numpy as jnp
from jax import lax
from jax.experimental import pallas as pl
from jax.experimental.pallas import tpu as pltpu

F32 = jnp.float32
BF16 = jnp.bfloat16

D_MODEL = 1024
DEPTH = 4
N_MIXERS = 3
GRID_W = 64
H_A = 8
DH_A = 64
DV_A = 128
ROPE_HALF = DH_A // 2
SCORE_SCALE = DH_A ** -0.5 * math.log2(math.e)
H_B = 4
DK_B = 256
DV_B = 512
E_B = H_B * DV_B
CHUNK = 256
LAT_RET_HEADS = 2
CTX_RET_SEQS = 2
ALPHA = (2.0 * DEPTH) ** 0.25
ROPE_BASE = 10000.0
LN_EPS = 1e-5
RMS_EPS = 1e-6

MOD_ROWS = 8
CTX_ROW = 4
VMEM_LIMIT_BYTES = 58 * 1024 * 1024
ROW_TILE = 512


def _params(n_axes):
    return pltpu.CompilerParams(dimension_semantics=("arbitrary",) * n_axes,
                                vmem_limit_bytes=VMEM_LIMIT_BYTES)


def _silu(x):
    return x * jax.nn.sigmoid(x)


def _residual_layer_norm(x, out, gate, g, b):
    r = ALPHA * x + gate * out
    mu = jnp.mean(r, axis=-1, keepdims=True)
    d = r - mu
    var = jnp.mean(d * d, axis=-1, keepdims=True)
    return d * lax.rsqrt(var + LN_EPS) * g + b


def _dot_bf16(h_bf16, w_f32):
    return jnp.dot(h_bf16, w_f32.astype(BF16), preferred_element_type=F32)


def _modulated(x, mod_ref):
    shift = mod_ref[:, 0:D_MODEL]
    scale = mod_ref[:, D_MODEL:2 * D_MODEL]
    return x * (1.0 + scale) + shift


def _mod_row_map(layer, rows_per_batch, tile):
    if rows_per_batch is None:
        return lambda m, *_: (layer * MOD_ROWS + CTX_ROW, 0, 0)
    return lambda m, *_: (layer * MOD_ROWS + (m * tile) // rows_per_batch, 0, 0)


def _both_streams_specs(x_ctx, x_lat, tm):
    n_ctx_tiles = x_ctx.shape[0] // tm
    return [pl.BlockSpec((tm, D_MODEL), lambda m: (jnp.minimum(m, n_ctx_tiles - 1), 0)),
            pl.BlockSpec((tm, D_MODEL), lambda m: (jnp.maximum(m - n_ctx_tiles, 0), 0))]


def _both_streams_rows(x_ctx_ref, x_lat_ref, n_ctx_tiles):
    return jnp.where(pl.program_id(0) < n_ctx_tiles, x_ctx_ref[...], x_lat_ref[...])


def _both_streams_mod_map(layer, n_ctx_tiles, tm, lat_rows_per_batch):
    def index_map(m):
        lat_row = ((m - n_ctx_tiles) * tm) // lat_rows_per_batch
        return (layer * MOD_ROWS + jnp.where(m < n_ctx_tiles, CTX_ROW, lat_row), 0, 0)
    return index_map


def _weight_spec(w, j):
    return pl.BlockSpec((None,) + w.shape[1:], lambda *_: (j, 0, 0), pipeline_mode=pl.Buffered(1))


WEIGHT_CHUNK_COLS = 1024
WEIGHT_COPIES_IN_FLIGHT = 2


class _StagedWeight:
    @staticmethod
    def scratch(w, chunk_cols):
        return [pltpu.VMEM(w.shape[1:], w.dtype),
                pltpu.SemaphoreType.DMA((w.shape[2] // chunk_cols,))]

    def __init__(self, w_hbm_ref, w_vmem_ref, sem_ref, j, chunk_cols):
        self.copies = [
            pltpu.make_async_copy(w_hbm_ref.at[j, :, pl.ds(c * chunk_cols, chunk_cols)],
                                  w_vmem_ref.at[:, pl.ds(c * chunk_cols, chunk_cols)], sem_ref.at[c])
            for c in range(w_vmem_ref.shape[1] // chunk_cols)]

    def _arrive(self, c):
        self.copies[c].wait()
        if c + WEIGHT_COPIES_IN_FLIGHT < len(self.copies):
            self.copies[c + WEIGHT_COPIES_IN_FLIGHT].start()

    def run(self, body):
        first = pl.program_id(0) == 0

        @pl.when(first)
        def _():
            for cp in self.copies[:WEIGHT_COPIES_IN_FLIGHT]:
                cp.start()
            body(self._arrive)

        @pl.when(jnp.logical_not(first))
        def _():
            body(lambda c: None)


def _mod_kernel(c_ref, cctx_ref, w_ref, b_ref, o_ref):
    pad = jnp.zeros((MOD_ROWS - c_ref.shape[0] - 1, D_MODEL), F32)
    s = _silu(jnp.concatenate([c_ref[...], cctx_ref[...], pad], axis=0))
    m = jnp.dot(s, w_ref[...], preferred_element_type=F32) + b_ref[...]
    for r in range(MOD_ROWS):
        o_ref[r] = m[r:r + 1]


def _modulation(c, c_ctx, w_mod, b_mod):
    n = 3 * D_MODEL
    return pl.pallas_call(
        _mod_kernel,
        out_shape=jax.ShapeDtypeStruct((DEPTH * MOD_ROWS, 1, n), F32),
        grid=(DEPTH,),
        in_specs=[pl.BlockSpec(c.shape, lambda i: (0, 0)),
                  pl.BlockSpec((1, D_MODEL), lambda i: (0, 0)),
                  pl.BlockSpec((None, D_MODEL, n), lambda i: (i, 0, 0)),
                  pl.BlockSpec((None, 1, n), lambda i: (i, 0, 0))],
        out_specs=pl.BlockSpec((MOD_ROWS, 1, n), lambda i: (i, 0, 0)),
        compiler_params=_params(1),
        name="modulation",
    )(c, c_ctx.reshape(1, D_MODEL), w_mod, b_mod.reshape(DEPTH, 1, n))


def _rope(xh, cos4, sin4, first_half):
    swapped = jnp.where(first_half, pltpu.roll(xh, DV_A - ROPE_HALF, 1),
                        pltpu.roll(xh, ROPE_HALF, 1))
    return xh * cos4 + swapped * sin4


def _store_heads(o_ref, x, seq_len):
    for b in range(o_ref.shape[0]):
        xb = x[b * seq_len:(b + 1) * seq_len].reshape(seq_len, H_A, DV_A)
        if len(o_ref.shape) == 4:
            o_ref[b] = xb
        else:
            o_ref[b, 0] = xb
            for s in range(1, o_ref.shape[1]):
                o_ref[b, s] = jnp.zeros_like(xb)


def _attn_in_body(x_ref, mod_ref, w_ref, rope_refs, q_ref, k_ref, vt_ref, z_ref, cache_refs, seq_len):
    h = _modulated(x_ref[...], mod_ref)
    if rope_refs is not None:
        lane = lax.broadcasted_iota(jnp.int32, (1, DV_A), 1)
        first_half = (lane % DH_A) < ROPE_HALF
        cos4 = rope_refs[0][...]
        sin4 = rope_refs[1][...]
    h = h.astype(BF16)
    q_all = _dot_bf16(h, w_ref[:, 0:D_MODEL])
    k_all = _dot_bf16(h, w_ref[:, D_MODEL:2 * D_MODEL])
    if cache_refs is not None:
        _store_heads(cache_refs[0], k_all, seq_len)
    for hd in range(H_A):
        cols = slice(hd * DV_A, (hd + 1) * DV_A)
        q = q_all[:, cols]
        k = k_all[:, cols]
        if rope_refs is not None:
            q = _rope(q, cos4, sin4, first_half)
            k = _rope(k, cos4, sin4, first_half)
        q_ref[:, cols] = (q * SCORE_SCALE).astype(BF16)
        k_ref[:, cols] = k.astype(BF16)
    v = _dot_bf16(h, w_ref[:, 2 * D_MODEL:3 * D_MODEL])
    if cache_refs is not None:
        _store_heads(cache_refs[1], v, seq_len)
    vt_ref[...] = v.T.astype(BF16)
    z_ref[...] = _dot_bf16(h, w_ref[:, 3 * D_MODEL:4 * D_MODEL]).astype(z_ref.dtype)


def _attn_in_kernel(x_ref, mod_ref, w_ref, cos_ref, sin_ref, q_ref, k_ref, vt_ref, z_ref):
    _attn_in_body(x_ref, mod_ref, w_ref, (cos_ref, sin_ref), q_ref, k_ref, vt_ref, z_ref, None, None)


def _attn_in(x, m_rows, mod3, w_in, j, layer, rows_per_batch, rope_tables):
    x2d, x_first_row = x
    tm = ROW_TILE
    seq_tiles = rows_per_batch // tm
    row_spec = pl.BlockSpec((tm, D_MODEL), lambda m: (m, 0))
    tab_spec = pl.BlockSpec((tm, DV_A), lambda m: (m % seq_tiles, 0))
    return pl.pallas_call(
        _attn_in_kernel,
        out_shape=[jax.ShapeDtypeStruct((m_rows, D_MODEL), BF16),
                   jax.ShapeDtypeStruct((m_rows, D_MODEL), BF16),
                   jax.ShapeDtypeStruct((D_MODEL, m_rows), BF16),
                   jax.ShapeDtypeStruct((m_rows, D_MODEL), BF16)],
        grid=(m_rows // tm,),
        in_specs=[pl.BlockSpec((tm, D_MODEL), lambda m: (x_first_row // tm + m, 0)),
                  pl.BlockSpec((None, 1, 3 * D_MODEL), _mod_row_map(layer, rows_per_batch, tm)),
                  _weight_spec(w_in, j), tab_spec, tab_spec],
        out_specs=[row_spec, row_spec, pl.BlockSpec((D_MODEL, tm), lambda m: (0, m)), row_spec],
        compiler_params=_params(1),
        name="attn_in_rope",
    )(x2d, mod3, w_in, *rope_tables)


ONES_ROWS = 16
CTX_ATTN_SEQS = 2
ATTN_ROWS = 512


def _slab_reduce(op, x):
    parts = [x[i:i + 8] for i in range(0, x.shape[0], 8)]
    while len(parts) > 1:
        parts = [op(parts[i], parts[i + 1]) for i in range(0, len(parts) - 1, 2)] + (
            [parts[-1]] if len(parts) % 2 else [])
    return parts[0]


def _attn_kernel(*refs, layer_idx, has_ctx, n_seq, shared_keys):
    if has_ctx:
        (q_ref, k_ref, vt_ref, z_ref, x_ref, kc_ref, vc_ref, mod_ref, lam_ref, subln_ref,
         w_ref, g_ref, b_ref, o_ref, y_ref, kcb_ref, vct_ref) = refs

        @pl.when(pl.program_id(1) == 0)
        def _():
            n_ctx = kc_ref.shape[0]
            kcb_ref[...] = kc_ref[...].reshape(n_ctx, D_MODEL).astype(BF16)
            vct_ref[...] = vc_ref[...].reshape(n_ctx, D_MODEL).T.astype(BF16)
    else:
        (q_ref, k_ref, vt_ref, z_ref, x_ref, mod_ref, lam_ref, subln_ref,
         w_ref, g_ref, b_ref, o_ref, y_ref) = refs
    tq = q_ref.shape[0] // n_seq
    lam_init = 0.8 - 0.6 * math.exp(-0.3 * layer_idx)
    lm = lam_ref[...]
    lam = (jnp.exp(jnp.sum(lm[0:1] * lm[1:2], axis=-1, keepdims=True))
           - jnp.exp(jnp.sum(lm[2:3] * lm[3:4], axis=-1, keepdims=True)) + lam_init)
    lane = lax.broadcasted_iota(jnp.int32, (1, DV_A), 1)
    first = lane < DH_A
    subln = jnp.broadcast_to(subln_ref[...], (DV_A, tq))
    nt = (((1,), (1,)), ((), ()))
    t = k_ref.shape[0] if shared_keys else k_ref.shape[0] // n_seq

    def key_rows(sq):
        return slice(0, t) if shared_keys else slice(sq * t, (sq + 1) * t)

    def scores(unit):
        sq, hd = unit
        cols = slice(hd * DV_A, (hd + 1) * DV_A)
        qh = q_ref[sq * tq:(sq + 1) * tq, cols]
        zero = jnp.zeros_like(qh)
        qq = jnp.concatenate([jnp.where(first, qh, zero), jnp.where(first, zero, qh)], axis=0)
        parts = [lax.dot_general(k_ref[key_rows(sq), cols], qq, nt, preferred_element_type=F32)]
        if has_ctx:
            parts.append(lax.dot_general(kcb_ref[:, cols], qq, nt, preferred_element_type=F32))
        return parts

    def exps(parts):
        m8 = functools.reduce(jnp.maximum, [_slab_reduce(jnp.maximum, s) for s in parts])
        m = jnp.max(m8, axis=0, keepdims=True)
        return [jnp.exp2(s - m).astype(BF16) for s in parts]

    def with_ones(vals_t):
        return jnp.concatenate([vals_t, jnp.ones((ONES_ROWS, vals_t.shape[1]), BF16)], axis=0)

    def finish(unit, es):
        sq, hd = unit
        cols = slice(hd * DV_A, (hd + 1) * DV_A)
        rows = slice(sq * tq, (sq + 1) * tq)
        acc = jnp.dot(with_ones(vt_ref[cols, key_rows(sq)]), es[0], preferred_element_type=F32)
        if has_ctx:
            acc = acc + jnp.dot(with_ones(vct_ref[cols, :]), es[1], preferred_element_type=F32)
        inv = 1.0 / acc[DV_A:DV_A + 1, :]
        ot = acc[:DV_A, :tq] * inv[:, :tq] - acc[:DV_A, tq:] * (inv[:, tq:] * lam)
        ot = ot * lax.rsqrt(jnp.mean(ot * ot, axis=0, keepdims=True) + RMS_EPS)
        ot = ot * subln * (1.0 - lam_init)
        y_ref[rows, cols] = (ot.T * _silu(z_ref[rows, cols].astype(F32))).astype(BF16)

    units = [(sq, hd) for hd in range(H_A) for sq in range(n_seq)]
    s_ahead = {u: scores(units[u]) for u in range(min(2, len(units)))}
    e_ahead = {0: exps(s_ahead.pop(0))}
    for u in range(len(units)):
        if u + 2 < len(units):
            s_ahead[u + 2] = scores(units[u + 2])
        if u + 1 < len(units):
            e_ahead[u + 1] = exps(s_ahead.pop(u + 1))
        finish(units[u], e_ahead.pop(u))
    out = jnp.dot(y_ref[...], w_ref[...].astype(BF16), preferred_element_type=F32)
    gate = mod_ref[:, 2 * D_MODEL:3 * D_MODEL]
    o_ref[...] = _residual_layer_norm(x_ref[...], out, gate, g_ref[...], b_ref[...])


def _latent_attention(qkvz, x, seq_len, ctx, mod3, lam, subln, w_out, j, ln_g, ln_b, layer):
    q, k, vt, z = qkvz
    m_rows = q.shape[0]
    t = seq_len
    rows = ATTN_ROWS
    nq = t // rows
    x, x_first_row = x
    x_spec = pl.BlockSpec((rows, D_MODEL), lambda i, j: (x_first_row // rows + i * nq + j, 0))
    q_spec = pl.BlockSpec((rows, D_MODEL), lambda i, j: (i * nq + j, 0))
    k_spec = pl.BlockSpec((t, D_MODEL), lambda i, j: (i, 0))
    vt_spec = pl.BlockSpec((D_MODEL, t), lambda i, j: (0, i))
    cache_k, cache_v, slot = ctx
    n_ctx = cache_k.shape[2]
    c_spec = pl.BlockSpec((None, None, n_ctx, H_A, DV_A), lambda i, j: (i, slot, 0, 0, 0))
    const2 = lambda i, j: (0, 0)
    return pl.pallas_call(
        functools.partial(_attn_kernel, layer_idx=layer, has_ctx=True, n_seq=1, shared_keys=True),
        out_shape=jax.ShapeDtypeStruct((m_rows, D_MODEL), F32),
        grid=(m_rows // t, nq),
        in_specs=[q_spec, k_spec, vt_spec, q_spec, x_spec, c_spec, c_spec,
                  pl.BlockSpec((None, 1, 3 * D_MODEL), lambda i, j: (layer * MOD_ROWS + i, 0, 0)),
                  pl.BlockSpec((4, DH_A), const2),
                  pl.BlockSpec((DV_A, 1), const2),
                  _weight_spec(w_out, j),
                  pl.BlockSpec((1, D_MODEL), const2),
                  pl.BlockSpec((1, D_MODEL), const2)],
        out_specs=q_spec,
        scratch_shapes=[pltpu.VMEM((rows, D_MODEL), BF16), pltpu.VMEM((n_ctx, D_MODEL), BF16),
                        pltpu.VMEM((D_MODEL, n_ctx), BF16)],
        compiler_params=_params(2),
        name="diff_attn_ctx",
    )(q, k, vt, z, x, cache_k, cache_v, mod3, lam, subln.reshape(DV_A, 1), w_out,
      ln_g.reshape(1, D_MODEL), ln_b.reshape(1, D_MODEL))


def _ctx_attn_layer_kernel(*refs, layer_idx, seq_len, n_aliased):
    x_ref, mod_ref, w_in_ref, lam_ref, subln_ref, w_out_ref, g_ref, b_ref = refs[:8]
    o_ref, ck_ref, cv_ref, q_ref, k_ref, vt_ref, z_ref, y_ref = refs[8 + n_aliased:]
    _attn_in_body(x_ref, mod_ref, w_in_ref, None, q_ref, k_ref, vt_ref, z_ref, (ck_ref, cv_ref),
                  seq_len)
    _attn_kernel(q_ref, k_ref, vt_ref, z_ref, x_ref, mod_ref, lam_ref, subln_ref, w_out_ref, g_ref,
                 b_ref, o_ref, y_ref, layer_idx=layer_idx, has_ctx=False,
                 n_seq=x_ref.shape[0] // seq_len, shared_keys=False)


def _ctx_attention_layer(x, m_rows, seq_len, mod3, w_in, w_out, j, lam, subln, ln_g, ln_b, layer,
                         cache_out):
    x2d, x_first_row = x
    rows = CTX_ATTN_SEQS * seq_len
    const2 = lambda m: (0, 0)
    in_specs = [pl.BlockSpec((rows, D_MODEL), lambda m: (x_first_row // rows + m, 0)),
                pl.BlockSpec((None, 1, 3 * D_MODEL), lambda m: (layer * MOD_ROWS + CTX_ROW, 0, 0)),
                _weight_spec(w_in, j),
                pl.BlockSpec((4, DH_A), const2),
                pl.BlockSpec((DV_A, 1), const2),
                _weight_spec(w_out, j),
                pl.BlockSpec((1, D_MODEL), const2),
                pl.BlockSpec((1, D_MODEL), const2)]
    args = [x2d, mod3, w_in, lam, subln.reshape(DV_A, 1), w_out, ln_g.reshape(1, D_MODEL),
            ln_b.reshape(1, D_MODEL)]
    new_k, new_v = cache_out
    cache_shape = jax.ShapeDtypeStruct(new_k.shape, F32)
    aliases = {}
    if j == 0:
        cache_spec = pl.BlockSpec((CTX_ATTN_SEQS, new_k.shape[1], seq_len, H_A, DV_A),
                                  lambda m: (m, 0, 0, 0, 0))
    else:
        cache_spec = pl.BlockSpec((CTX_ATTN_SEQS, None, seq_len, H_A, DV_A),
                                  lambda m: (m, j, 0, 0, 0))
        in_specs += [pl.BlockSpec(memory_space=pl.ANY)] * 2
        args += [new_k, new_v]
        aliases = {len(args) - 2: 1, len(args) - 1: 2}
    return pl.pallas_call(
        functools.partial(_ctx_attn_layer_kernel, layer_idx=layer, seq_len=seq_len,
                          n_aliased=len(aliases)),
        out_shape=[jax.ShapeDtypeStruct((m_rows, D_MODEL), F32), cache_shape, cache_shape],
        grid=(m_rows // rows,),
        in_specs=in_specs,
        out_specs=[pl.BlockSpec((rows, D_MODEL), lambda m: (m, 0)), cache_spec, cache_spec],
        scratch_shapes=[pltpu.VMEM((rows, D_MODEL), BF16), pltpu.VMEM((rows, D_MODEL), BF16),
                        pltpu.VMEM((D_MODEL, rows), BF16), pltpu.VMEM((rows, D_MODEL), BF16),
                        pltpu.VMEM((rows, D_MODEL), BF16)],
        input_output_aliases=aliases,
        compiler_params=_params(1),
        name="ctx_attn_layer",
    )(*args)


def _rope_tables(n_tokens):
    rows = n_tokens // GRID_W
    r = jnp.repeat(jnp.arange(rows, dtype=F32), GRID_W)
    col = jnp.tile(jnp.arange(GRID_W, dtype=F32), rows)
    n_freq = DH_A // 4
    inv = ROPE_BASE ** (-jnp.arange(n_freq, dtype=F32) / n_freq)
    ang = jnp.concatenate([r[:, None] * inv, col[:, None] * inv], -1)
    cos, sin = jnp.cos(ang), jnp.sin(ang)
    return jnp.tile(cos, (1, 4)), jnp.concatenate([-sin, sin, -sin, sin], -1)


def _ret_in_kernel(xc_ref, xl_ref, mod_ref, w_hbm_ref, q_ref, kt_ref, v_ref, g_ref, w_ref, sem_ref, *,
                   n_ctx_tiles, j):
    nq = H_B * DK_B
    cc = WEIGHT_CHUNK_COLS
    assert nq == cc and E_B % cc == 0
    staged = _StagedWeight(w_hbm_ref, w_ref, sem_ref, j, cc)

    def body(arrive):
        h = _modulated(_both_streams_rows(xc_ref, xl_ref, n_ctx_tiles), mod_ref).astype(BF16)
        arrive(0)
        q_ref[...] = _dot_bf16(h, w_ref[:, 0:nq]).astype(BF16)
        arrive(1)
        k = _dot_bf16(h, w_ref[:, nq:2 * nq]) * (DK_B ** -0.5)
        kt_ref[...] = k.T
        for c in range(E_B // cc):
            arrive(2 + c)
            v = _dot_bf16(h, w_ref[:, 2 * nq + c * cc:2 * nq + (c + 1) * cc])
            v_ref[:, c * cc:(c + 1) * cc] = v.astype(BF16)
        for c in range(E_B // cc):
            arrive(2 + E_B // cc + c)
            g_ref[:, c * cc:(c + 1) * cc] = _dot_bf16(
                h, w_ref[:, 2 * nq + E_B + c * cc:2 * nq + E_B + (c + 1) * cc]).astype(BF16)

    staged.run(body)


def _ret_in(x_ctx, x_lat, mod3, w_in, j, layer, lat_rows_per_batch):
    m_rows = x_ctx.shape[0] + x_lat.shape[0]
    tm = ROW_TILE
    nq = H_B * DK_B
    n_ctx_tiles = x_ctx.shape[0] // tm
    return pl.pallas_call(
        functools.partial(_ret_in_kernel, n_ctx_tiles=n_ctx_tiles, j=j),
        out_shape=[jax.ShapeDtypeStruct((m_rows, nq), BF16),
                   jax.ShapeDtypeStruct((nq, m_rows), F32),
                   jax.ShapeDtypeStruct((m_rows, E_B), BF16),
                   jax.ShapeDtypeStruct((m_rows, E_B), BF16)],
        grid=(m_rows // tm,),
        in_specs=_both_streams_specs(x_ctx, x_lat, tm) + [
            pl.BlockSpec((None, 1, 3 * D_MODEL),
                         _both_streams_mod_map(layer, n_ctx_tiles, tm, lat_rows_per_batch)),
            pl.BlockSpec(memory_space=pl.ANY)],
        out_specs=[pl.BlockSpec((tm, nq), lambda m: (m, 0)),
                   pl.BlockSpec((nq, tm), lambda m: (0, m)),
                   pl.BlockSpec((tm, E_B), lambda m: (m, 0)),
                   pl.BlockSpec((tm, E_B), lambda m: (m, 0))],
        scratch_shapes=_StagedWeight.scratch(w_in, WEIGHT_CHUNK_COLS),
        compiler_params=_params(1),
        name="ret_in",
    )(x_ctx, x_lat, mod3, w_in)


def _retention_kernel(*refs, has_state, heads, seqs, fused_out):
    refs = list(refs)
    q_ref, kt_ref, v_ref, g_ref, af_ref, ab_ref = refs[:6]
    del refs[:6]
    if has_state:
        s0f_ref, s0b_ref = refs[:2]
        del refs[:2]
    if fused_out:
        x_ref, mod_ref, w_ref, lng_ref, lnb_ref = refs[:5]
        del refs[:5]
        o_ref = refs.pop(0)
        y_ref = refs.pop()
    else:
        y_ref = refs.pop(0)
    if not has_state:
        sf_ref, sb_ref = refs
    t = q_ref.shape[0] // seqs
    nc = t // CHUNK
    row = lax.broadcasted_iota(jnp.int32, (CHUNK, CHUNK), 0).astype(F32)
    col = lax.broadcasted_iota(jnp.int32, (CHUNK, CHUNK), 1).astype(F32)
    diff = row - col
    idx_col = lax.broadcasted_iota(jnp.int32, (CHUNK, 1), 0).astype(F32)
    idx_row = lax.broadcasted_iota(jnp.int32, (1, CHUNK), 1).astype(F32)

    def state(ref, sq, hh):
        return ref.at[hh] if seqs == 1 else ref.at[sq, hh]

    for hh, sq in [(hh, sq) for hh in range(heads) for sq in range(seqs)]:
        def chunk(c, sq=sq):
            return slice(sq * t + c * CHUNK, sq * t + (c + 1) * CHUNK)

        qk_cols = slice(hh * DK_B, (hh + 1) * DK_B)
        v_cols = slice(hh * DV_B, (hh + 1) * DV_B)
        lg_f = jnp.log1p(-jnp.exp(af_ref[hh]))
        lg_b = jnp.log1p(-jnp.exp(ab_ref[hh]))
        dmask = (jnp.where(diff >= 0, jnp.exp(jnp.maximum(diff, 0.0) * lg_f), 0.0)
                 + jnp.where(diff <= 0, jnp.exp(jnp.maximum(-diff, 0.0) * lg_b), 0.0))
        qd_f = jnp.exp((idx_col + 1.0) * lg_f)
        qd_b = jnp.exp((CHUNK - idx_col) * lg_b)
        kd_f = jnp.exp((CHUNK - 1.0 - idx_row) * lg_f)
        kd_b = jnp.exp(idx_row * lg_b)
        cd_f = jnp.exp(CHUNK * lg_f)
        cd_b = jnp.exp(CHUNK * lg_b)

        def states(order, kd, cd, s):
            seen = {}
            for n, c in enumerate(order):
                seen[c] = None if s is None else s.astype(BF16)
                if has_state and n == nc - 1:
                    return seen, None
                u = jnp.dot((kt_ref[qk_cols, chunk(c)] * kd).astype(BF16), v_ref[chunk(c), v_cols],
                            preferred_element_type=F32)
                s = u if s is None else s * cd + u
            return seen, s

        seen_f, s_f = states(range(nc), kd_f, cd_f,
                             state(s0f_ref, sq, hh)[...] if has_state else None)
        seen_b, s_b = states(range(nc - 1, -1, -1), kd_b, cd_b,
                             state(s0b_ref, sq, hh)[...] if has_state else None)
        if not has_state:
            state(sf_ref, sq, hh)[...] = s_f
            state(sb_ref, sq, hh)[...] = s_b
        for c in range(nc):
            qc = q_ref[chunk(c), qk_cols]
            qk = jnp.dot(qc, kt_ref[qk_cols, chunk(c)].astype(BF16), preferred_element_type=F32)
            o = jnp.dot((qk * dmask).astype(BF16), v_ref[chunk(c), v_cols],
                        preferred_element_type=F32)
            if seen_f[c] is not None:
                o = o + jnp.dot(qc, seen_f[c], preferred_element_type=F32) * qd_f
            if seen_b[c] is not None:
                o = o + jnp.dot(qc, seen_b[c], preferred_element_type=F32) * qd_b
            o = o * lax.rsqrt(jnp.mean(o * o, axis=-1, keepdims=True) + RMS_EPS)
            gate_pre = g_ref[chunk(c), v_cols].astype(F32)
            y_ref[chunk(c), v_cols] = (o * _silu(gate_pre)).astype(BF16)
    if fused_out:
        part = jnp.dot(y_ref[...], w_ref[...].astype(BF16), preferred_element_type=F32)
        gate = mod_ref[:, 2 * D_MODEL:3 * D_MODEL]
        head_steps = H_B // heads
        if head_steps == 1:
            o_ref[...] = _residual_layer_norm(x_ref[...], part, gate, lng_ref[...], lnb_ref[...])
        else:
            step = pl.program_id(1)

            @pl.when(step == 0)
            def _():
                o_ref[...] = part

            @pl.when(jnp.logical_and(step > 0, step < head_steps - 1))
            def _():
                o_ref[...] += part

            @pl.when(step == head_steps - 1)
            def _():
                o_ref[...] = _residual_layer_norm(x_ref[...], o_ref[...] + part, gate, lng_ref[...],
                                                  lnb_ref[...])


def _retention(q, kt, v, g, seq_len, n_seq, first_row, heads, seqs, decay_f, decay_b, states,
               out_proj=None):
    t = seqs * seq_len
    b = n_seq // seqs
    s0 = first_row // t
    has_state = states is not None
    fused_out = out_proj is not None
    seq_dim = None if seqs == 1 else seqs
    q_spec = pl.BlockSpec((t, heads * DK_B), lambda i, h: (s0 + i, h))
    kt_spec = pl.BlockSpec((heads * DK_B, t), lambda i, h: (h, s0 + i))
    vg_spec = pl.BlockSpec((t, heads * DV_B), lambda i, h: (s0 + i, h))
    a_spec = pl.BlockSpec((heads, 1, 1), lambda i, h: (h, 0, 0))
    in_specs = [q_spec, kt_spec, vg_spec, vg_spec, a_spec, a_spec]
    args = [q, kt, v, g, decay_f.reshape(H_B, 1, 1), decay_b.reshape(H_B, 1, 1)]
    out_shape = [jax.ShapeDtypeStruct((b * t, E_B), BF16)]
    out_specs = [pl.BlockSpec((t, heads * DV_B), lambda i, h: (i, h))]
    if has_state:
        s_f, s_b, jj = states
        s_spec = pl.BlockSpec((seq_dim, None, heads, DK_B, DV_B), lambda i, h: (i, jj, h, 0, 0))
        in_specs += [s_spec, s_spec]
        args += [s_f, s_b]
    scratch = []
    if fused_out:
        x2d, mod3, mod_map, w_out, jw, ln_g, ln_b = out_proj
        x_spec = pl.BlockSpec((t, D_MODEL), lambda i, h: (i, 0))
        const2 = lambda i, h: (0, 0)
        if heads == H_B:
            w_spec = _weight_spec(w_out, jw)
        else:
            w_spec = pl.BlockSpec((None, heads * DV_B, D_MODEL), lambda i, h: (jw, h, 0))
        in_specs += [x_spec, pl.BlockSpec((None, 1, 3 * D_MODEL), mod_map), w_spec,
                     pl.BlockSpec((1, D_MODEL), const2), pl.BlockSpec((1, D_MODEL), const2)]
        args += [x2d, mod3, w_out, ln_g.reshape(1, D_MODEL), ln_b.reshape(1, D_MODEL)]
        out_shape = [jax.ShapeDtypeStruct((b * t, D_MODEL), F32)]
        out_specs = [x_spec]
        scratch = [pltpu.VMEM((t, heads * DV_B), BF16)]
    if not has_state:
        so_spec = pl.BlockSpec((seq_dim, None, heads, DK_B, DV_B), lambda i, h: (i, 0, h, 0, 0))
        out_shape += [jax.ShapeDtypeStruct((n_seq, 1, H_B, DK_B, DV_B), F32)] * 2
        out_specs += [so_spec, so_spec]
    return pl.pallas_call(
        functools.partial(_retention_kernel, has_state=has_state, heads=heads, seqs=seqs,
                          fused_out=fused_out),
        out_shape=out_shape,
        grid=(b, H_B // heads),
        in_specs=in_specs,
        out_specs=out_specs,
        scratch_shapes=scratch,
        compiler_params=_params(2),
        name="retention_state" if has_state else "retention",
    )(*args)


CONV_ROWS = 1024
CONV_COLS = 256


def _conv_kernel(xc_ref, xl_ref, mod_ref, w_in_ref, cw_ref, w_out_ref, g_ref, b_ref, o_ref, *,
                 n_ctx_tiles, ctx_seq_len, lat_seq_len):
    x = _both_streams_rows(xc_ref, xl_ref, n_ctx_tiles)
    h = _modulated(x, mod_ref).astype(BF16)
    rows = x.shape[0]
    is_ctx = pl.program_id(0) < n_ctx_tiles
    row = lax.broadcasted_iota(jnp.int32, (rows, 1), 0)
    pos = jnp.where(is_ctx, row % ctx_seq_len, row % lat_seq_len)
    has_prev = pos > 0
    has_next = pos < jnp.where(is_ctx, ctx_seq_len - 1, lat_seq_len - 1)
    e = D_MODEL
    for c in range(e // CONV_COLS):
        cols = slice(c * CONV_COLS, (c + 1) * CONV_COLS)

        def proj(part):
            lo = part * e + c * CONV_COLS
            return _dot_bf16(h, w_in_ref[:, lo:lo + CONV_COLS])

        p = proj(1) * proj(2)
        prev = jnp.where(has_prev, pltpu.roll(p, 1, 0), 0.0)
        nxt = jnp.where(has_next, pltpu.roll(p, rows - 1, 0), 0.0)
        conv = prev * cw_ref[0:1, cols] + p * cw_ref[1:2, cols] + nxt * cw_ref[2:3, cols]
        y = (proj(0) * conv * _silu(proj(3))).astype(BF16)
        part = jnp.dot(y, w_out_ref[cols, :].astype(BF16), preferred_element_type=F32)
        if c == 0:
            o_ref[...] = part
        else:
            o_ref[...] += part
    gate = mod_ref[:, 2 * D_MODEL:3 * D_MODEL]
    o_ref[...] = _residual_layer_norm(x, o_ref[...], gate, g_ref[...], b_ref[...])


def _conv_layer(x_ctx, x_lat, mod3, w_in, conv_w, w_out, j, ln_g, ln_b, layer, ctx_seq_len,
                lat_seq_len):
    m_rows = x_ctx.shape[0] + x_lat.shape[0]
    tm = CONV_ROWS
    n_ctx_tiles = x_ctx.shape[0] // tm
    const2 = lambda m: (0, 0)
    return pl.pallas_call(
        functools.partial(_conv_kernel, n_ctx_tiles=n_ctx_tiles, ctx_seq_len=ctx_seq_len,
                          lat_seq_len=lat_seq_len),
        out_shape=jax.ShapeDtypeStruct((m_rows, D_MODEL), F32),
        grid=(m_rows // tm,),
        in_specs=_both_streams_specs(x_ctx, x_lat, tm) + [
                  pl.BlockSpec((None, 1, 3 * D_MODEL),
                               _both_streams_mod_map(layer, n_ctx_tiles, tm, lat_seq_len)),
                  _weight_spec(w_in, j),
                  pl.BlockSpec((None, 3, D_MODEL), lambda m: (j, 0, 0)),
                  _weight_spec(w_out, j),
                  pl.BlockSpec((1, D_MODEL), const2),
                  pl.BlockSpec((1, D_MODEL), const2)],
        out_specs=pl.BlockSpec((tm, D_MODEL), lambda m: (m, 0)),
        compiler_params=_params(1),
        name="conv_layer",
    )(x_ctx, x_lat, mod3, w_in, conv_w, w_out, ln_g.reshape(1, D_MODEL), ln_b.reshape(1, D_MODEL))


def _separate(xp, xs, m_ctx):
    if xp[0] is xs[0]:
        return (xp[0][:m_ctx], 0), (xs[0][m_ctx:], 0)
    return xp, xs


def kernel(x_prompt, x_sample, cache_k, cache_v, state_fwd, state_bwd, c, c_ctx, w_mod, b_mod, ln_g,
           ln_b, w_in_a, lam_a, subln_a, w_out_a, w_in_b, decay_fwd, decay_bwd, w_out_b, w_in_c,
           conv_c, w_out_c):
    bp, tp, d = x_prompt.shape
    bs, ts, _ = x_sample.shape
    assert bs == CTX_ROW < MOD_ROWS
    mod3 = _modulation(c, c_ctx, w_mod, b_mod)
    rope_tables = _rope_tables(ts)

    m_ctx, m_lat = bp * tp, bs * ts
    xp = (x_prompt.reshape(m_ctx, d), 0)
    xs = (x_sample.reshape(m_lat, d), 0)
    n_attn = (DEPTH + N_MIXERS - 1) // N_MIXERS
    new_cache_k = new_cache_v = jax.ShapeDtypeStruct((bp, n_attn, tp, H_A, DV_A), F32)
    new_sf, new_sb = [], []
    for i in range(DEPTH):
        kind, j = i % N_MIXERS, i // N_MIXERS
        if kind == 0:
            x_ctx, new_cache_k, new_cache_v = _ctx_attention_layer(
                xp, m_ctx, tp, mod3, w_in_a, w_out_a, j, lam_a[j], subln_a[j], ln_g[i], ln_b[i], i,
                (new_cache_k, new_cache_v))
            qkvz = _attn_in(xs, m_lat, mod3, w_in_a, j, i, ts, rope_tables)
            xs = (_latent_attention(qkvz, xs, ts, (cache_k, cache_v, j), mod3, lam_a[j],
                                    subln_a[j], w_out_a, j, ln_g[i], ln_b[i], i), 0)
            xp = (x_ctx, 0)
        else:
            xp, xs = _separate(xp, xs, m_ctx)
            if kind == 1:
                q, kt, v, g = _ret_in(xp[0], xs[0], mod3, w_in_b, j, i, ts)
                x_ctx, s_f, s_b = _retention(
                    q, kt, v, g, tp, bp, 0, H_B, CTX_RET_SEQS, decay_fwd[j], decay_bwd[j], None,
                    out_proj=(xp[0], mod3, _mod_row_map(i, None, tp), w_out_b, j, ln_g[i],
                              ln_b[i]))
                new_sf.append(s_f)
                new_sb.append(s_b)
                (x_lat,) = _retention(
                    q, kt, v, g, ts, bs, m_ctx, LAT_RET_HEADS, 1, decay_fwd[j], decay_bwd[j],
                    (state_fwd, state_bwd, j),
                    out_proj=(xs[0], mod3, _mod_row_map(i, ts, ts), w_out_b, j, ln_g[i], ln_b[i]))
                xp, xs = (x_ctx, 0), (x_lat, 0)
            else:
                x_all = _conv_layer(xp[0], xs[0], mod3, w_in_c, conv_c, w_out_c, j, ln_g[i],
                                    ln_b[i], i, tp, ts)
                xp, xs = (x_all, 0), (x_all, m_ctx)
    xp, xs = _separate(xp, xs, m_ctx)
    y_prompt = xp[0].reshape(bp, tp, d)
    y_sample = xs[0].reshape(bs, ts, d)
    new_state_fwd = jnp.concatenate(new_sf, axis=1)
    new_state_bwd = jnp.concatenate(new_sb, axis=1)
    return (y_prompt, y_sample, new_cache_k, new_cache_v, new_state_fwd, new_state_bwd)
```
